```python
import math
import jax, jax.numpy as jnp
from jax import lax
import numpy as np

D_MODEL = 1024
BATCH = 4
SEQ = 4096
DEPTH = 1
DEC_BATCH = 128
DEC_SEQ = 1
PAST_LEN = 8192
PAGE_SIZE = 128

N_Q_HEADS = 8
N_KV_HEADS = 2
HEAD_DIM = 64
Q_PER_KV = N_Q_HEADS // N_KV_HEADS
ATTN_WIDTH = N_Q_HEADS * HEAD_DIM
KV_WIDTH = N_KV_HEADS * HEAD_DIM
WINDOW = 128
ROPE_DIM = HEAD_DIM // 4
ROPE_THETA = 500000.0
SSM_WIDTH = D_MODEL // 2
SSM_GROUP = 16
N_SSM_GROUPS = SSM_WIDTH // SSM_GROUP
SSM_STATE = 64
GATE_WIDTH = 2 * D_MODEL
IN_WIDTH = ATTN_WIDTH + 2 * KV_WIDTH + SSM_WIDTH + GATE_WIDTH
D_FF = -(-8 * D_MODEL // (3 * 256)) * 256
NORM_EPS = 1e-5

kernel_name = "swa_sink_s5_gated_hybrid_step"


def rms_norm(x, g):
    xf = x.astype(jnp.float32)
    y = xf * lax.rsqrt(jnp.mean(xf * xf, axis=-1, keepdims=True) + NORM_EPS)
    return (y * g.astype(jnp.float32)).astype(x.dtype)


def rope_partial(x, pos):
    half = ROPE_DIM // 2
    inv_freq = ROPE_THETA ** (-(jnp.arange(half, dtype=jnp.float32) * 2.0 / ROPE_DIM))
    ang = pos.astype(jnp.float32)[:, None] * inv_freq[None, :]
    cos = jnp.cos(ang)[:, None, :]
    sin = jnp.sin(ang)[:, None, :]
    xr = x[..., :ROPE_DIM].astype(jnp.float32)
    x1, x2 = xr[..., :half], xr[..., half:]
    rot = jnp.concatenate([x1 * cos - x2 * sin, x2 * cos + x1 * sin], axis=-1)
    return jnp.concatenate([rot.astype(x.dtype), x[..., ROPE_DIM:]], axis=-1)


def sink_softmax(scores, mask, sinks):
    s = jnp.where(mask, scores, -jnp.inf)
    sk = jnp.broadcast_to(sinks.astype(jnp.float32)[:, :, None, None], s.shape[:-1] + (1,))
    p = jax.nn.softmax(jnp.concatenate([s, sk], axis=-1), axis=-1)
    return p[..., :-1]


def swa_prompt(q, k, v, sinks):
    B, T = q.shape[0], q.shape[1]
    nb = T // WINDOW
    qb = q.reshape(B, nb, WINDOW, N_KV_HEADS, Q_PER_KV, HEAD_DIM)
    kb = k.reshape(B, nb, WINDOW, N_KV_HEADS, HEAD_DIM)
    vb = v.reshape(B, nb, WINDOW, N_KV_HEADS, HEAD_DIM)
    kk = jnp.concatenate([jnp.concatenate([jnp.zeros_like(kb[:, :1]), kb[:, :-1]], axis=1), kb], axis=2)
    vv = jnp.concatenate([jnp.concatenate([jnp.zeros_like(vb[:, :1]), vb[:, :-1]], axis=1), vb], axis=2)
    scale = HEAD_DIM ** -0.5
    scores = jnp.einsum('bnqkgd,bnskd->bnkgqs', qb, kk, preferred_element_type=jnp.float32) * scale
    n = jnp.arange(nb)[:, None, None]
    i = jnp.arange(WINDOW)[None, :, None]
    j = jnp.arange(2 * WINDOW)[None, None, :]
    qpos = n * WINDOW + i
    kpos = (n - 1) * WINDOW + j
    mask = (kpos >= 0) & (kpos <= qpos) & (qpos - kpos < WINDOW)
    p = sink_softmax(scores, mask[None, :, None, None], sinks.reshape(N_KV_HEADS, Q_PER_KV))
    out = jnp.einsum('bnkgqs,bnskd->bnqkgd', p.astype(v.dtype), vv)
    wb = min(WINDOW, T)
    return out.reshape(B, T, ATTN_WIDTH), k[:, T - wb:], v[:, T - wb:]


def swa_sample(q, k, v, sinks, k_buf, v_buf):
    DB, S = q.shape[0], q.shape[1]
    wb = k_buf.shape[1]
    kk = jnp.concatenate([k_buf.astype(k.dtype), k], axis=1)
    vv = jnp.concatenate([v_buf.astype(v.dtype), v], axis=1)
    qg = q.reshape(DB, S, N_KV_HEADS, Q_PER_KV, HEAD_DIM)
    scale = HEAD_DIM ** -0.5
    scores = jnp.einsum('bqkgd,bskd->bkgqs', qg, kk, preferred_element_type=jnp.float32) * scale
    qpos = PAST_LEN + jnp.arange(S)[:, None]
    kpos = PAST_LEN - wb + jnp.arange(wb + S)[None, :]
    mask = (kpos <= qpos) & (qpos - kpos < WINDOW)
    p = sink_softmax(scores, mask, sinks.reshape(N_KV_HEADS, Q_PER_KV))
    out = jnp.einsum('bkgqs,bskd->bqkgd', p.astype(v.dtype), vv)
    return out.reshape(DB, S, ATTN_WIDTH), kk[:, S:], vv[:, S:]


def s5_discretize(lam_re, lam_im, log_dt, b_re, b_im):
    lam_re = lam_re.astype(jnp.float32)
    lam_im = lam_im.astype(jnp.float32)
    dt = jnp.exp(log_dt.astype(jnp.float32))[:, None]
    mag = jnp.exp(lam_re * dt)
    lb_re = mag * jnp.cos(lam_im * dt)
    lb_im = mag * jnp.sin(lam_im * dt)
    den = lam_re * lam_re + lam_im * lam_im
    nr = lb_re - 1.0
    c_re = ((nr * lam_re + lb_im * lam_im) / den)[..., None]
    c_im = ((lb_im * lam_re - nr * lam_im) / den)[..., None]
    b_re = b_re.astype(jnp.float32)
    b_im = b_im.astype(jnp.float32)
    bb_re = c_re * b_re - c_im * b_im
    bb_im = c_re * b_im + c_im * b_re
    return lb_re, lb_im, bb_re, bb_im


def _affine_combine(e1, e2):
    ar1, ai1, br1, bi1 = e1
    ar2, ai2, br2, bi2 = e2
    return (ar1 * ar2 - ai1 * ai2,
            ar1 * ai2 + ai1 * ar2,
            ar2 * br1 - ai2 * bi1 + br2,
            ar2 * bi1 + ai2 * br1 + bi2)


def s5_branch(u, h0_re, h0_im, lam_re, lam_im, log_dt, b_re, b_im, c_re, c_im, d, w_glu, b_glu):
    B, T = u.shape[0], u.shape[1]
    lb_re, lb_im, bb_re, bb_im = s5_discretize(lam_re, lam_im, log_dt, b_re, b_im)
    ug = u.reshape(B, T, N_SSM_GROUPS, SSM_GROUP).astype(jnp.float32)
    bu_re = jnp.einsum('gph,btgh->btgp', bb_re, ug)
    bu_im = jnp.einsum('gph,btgh->btgp', bb_im, ug)
    h0r = h0_re.astype(jnp.float32)
    h0i = h0_im.astype(jnp.float32)
    bu_re = bu_re.at[:, 0].add(lb_re * h0r - lb_im * h0i)
    bu_im = bu_im.at[:, 0].add(lb_re * h0i + lb_im * h0r)
    a_re = jnp.broadcast_to(lb_re, bu_re.shape)
    a_im = jnp.broadcast_to(lb_im, bu_im.shape)
    _, _, h_re, h_im = lax.associative_scan(_affine_combine, (a_re, a_im, bu_re, bu_im), axis=1)
    y = (jnp.einsum('ghp,btgp->btgh', c_re.astype(jnp.float32), h_re)
         - jnp.einsum('ghp,btgp->btgh', c_im.astype(jnp.float32), h_im))
    y = y.reshape(B, T, SSM_WIDTH) + d.astype(jnp.float32) * u.astype(jnp.float32)
    z = jax.nn.gelu(y).astype(u.dtype)
    out = z * jax.nn.sigmoid(z @ w_glu + b_glu)
    return out, h_re[:, -1].astype(u.dtype), h_im[:, -1].astype(u.dtype)


def hybrid_layer(x, pos, attend, h0_re, h0_im, norm1_g, w_in, b_gate, attn_sinks,
                 ssm_lam_re, ssm_lam_im, ssm_log_dt, ssm_b_re, ssm_b_im, ssm_c_re, ssm_c_im,
                 ssm_d, w_glu, b_glu, w_branch_attn, w_branch_ssm, w_out, norm2_g,
                 w_ffn_gate, w_ffn_up, w_ffn_down):
    B, T = x.shape[0], x.shape[1]
    h = rms_norm(x, norm1_g)
    proj = h @ w_in
    o1 = ATTN_WIDTH
    o2 = o1 + KV_WIDTH
    o3 = o2 + KV_WIDTH
    o4 = o3 + SSM_WIDTH
    q = rope_partial(proj[..., :o1].reshape(B, T, N_Q_HEADS, HEAD_DIM), pos)
    k = rope_partial(proj[..., o1:o2].reshape(B, T, N_KV_HEADS, HEAD_DIM), pos)
    v = proj[..., o2:o3].reshape(B, T, N_KV_HEADS, HEAD_DIM)
    u = proj[..., o3:o4]
    gates = proj[..., o4:] + b_gate
    attn_out, k_state, v_state = attend(q, k, v, attn_sinks)
    ssm_out, hr, hi = s5_branch(u, h0_re, h0_im, ssm_lam_re, ssm_lam_im, ssm_log_dt, ssm_b_re,
                                ssm_b_im, ssm_c_re, ssm_c_im, ssm_d, w_glu, b_glu)
    merged = (jax.nn.sigmoid(gates[..., :D_MODEL]) * (attn_out @ w_branch_attn)
              + jax.nn.sigmoid(gates[..., D_MODEL:]) * (ssm_out @ w_branch_ssm))
    x = x + merged @ w_out
    h2 = rms_norm(x, norm2_g)
    x = x + (jax.nn.silu(h2 @ w_ffn_gate) * (h2 @ w_ffn_up)) @ w_ffn_down
    return x, k_state, v_state, hr, hi


def setup_inputs(seed: int = 0) -> dict:
    key = jax.random.key(seed)
    ks = jax.random.split(key, 32)
    f32 = jnp.float32

    def nrm(k, shape, scale):
        return jax.random.normal(k, shape, f32) * scale

    wb = min(WINDOW, PAST_LEN)
    G, P, H = N_SSM_GROUPS, SSM_STATE, SSM_GROUP
    lam_im0 = math.pi * jnp.arange(P, dtype=f32)
    return {
        "x_prompt": nrm(ks[0], (BATCH, SEQ, D_MODEL), 1.0),
        "x_sample": nrm(ks[1], (DEC_BATCH, DEC_SEQ, D_MODEL), 1.0),
        "state_k_win": nrm(ks[2], (DEPTH, DEC_BATCH, wb, N_KV_HEADS, HEAD_DIM), 1.0),
        "state_v_win": nrm(ks[3], (DEPTH, DEC_BATCH, wb, N_KV_HEADS, HEAD_DIM), 1.0),
        "state_ssm_re": nrm(ks[4], (DEPTH, DEC_BATCH, G, P), 0.3),
        "state_ssm_im": nrm(ks[5], (DEPTH, DEC_BATCH, G, P), 0.3),
        "norm1_g": 1.0 + nrm(ks[6], (DEPTH, D_MODEL), 0.02),
        "w_in": nrm(ks[7], (DEPTH, D_MODEL, IN_WIDTH), D_MODEL ** -0.5),
        "b_gate": nrm(ks[8], (DEPTH, GATE_WIDTH), 0.02),
        "attn_sinks": nrm(ks[9], (DEPTH, N_Q_HEADS), 1.0),
        "ssm_lam_re": -0.5 + nrm(ks[10], (DEPTH, G, P), 0.01),
        "ssm_lam_im": lam_im0 + nrm(ks[11], (DEPTH, G, P), 0.01),
        "ssm_log_dt": jax.random.uniform(ks[12], (DEPTH, G), f32, math.log(1e-3), math.log(1e-1)),
        "ssm_b_re": nrm(ks[13], (DEPTH, G, P, H), (2 * H) ** -0.5),
        "ssm_b_im": nrm(ks[14], (DEPTH, G, P, H), (2 * H) ** -0.5),
        "ssm_c_re": nrm(ks[15], (DEPTH, G, H, P), P ** -0.5),
        "ssm_c_im": nrm(ks[16], (DEPTH, G, H, P), P ** -0.5),
        "ssm_d": nrm(ks[17], (DEPTH, SSM_WIDTH), 1.0),
        "w_glu": nrm(ks[18], (DEPTH, SSM_WIDTH, SSM_WIDTH), SSM_WIDTH ** -0.5),
        "b_glu": nrm(ks[19], (DEPTH, SSM_WIDTH), 0.02),
        "w_branch_attn": nrm(ks[20], (DEPTH, ATTN_WIDTH, D_MODEL), ATTN_WIDTH ** -0.5),
        "w_branch_ssm": nrm(ks[21], (DEPTH, SSM_WIDTH, D_MODEL), SSM_WIDTH ** -0.5),
        "w_out": nrm(ks[22], (DEPTH, D_MODEL, D_MODEL), D_MODEL ** -0.5),
        "norm2_g": 1.0 + nrm(ks[23], (DEPTH, D_MODEL), 0.02),
        "w_ffn_gate": nrm(ks[24], (DEPTH, D_MODEL, D_FF), D_MODEL ** -0.5),
        "w_ffn_up": nrm(ks[25], (DEPTH, D_MODEL, D_FF), D_MODEL ** -0.5),
        "w_ffn_down": nrm(ks[26], (DEPTH, D_FF, D_MODEL), D_FF ** -0.5),
        "norm_f_g": 1.0 + nrm(ks[27], (D_MODEL,), 0.02),
    }


def reference(x_prompt, x_sample, state_k_win, state_v_win, state_ssm_re, state_ssm_im,
              norm1_g, w_in, b_gate, attn_sinks, ssm_lam_re, ssm_lam_im, ssm_log_dt,
              ssm_b_re, ssm_b_im, ssm_c_re, ssm_c_im, ssm_d, w_glu, b_glu,
              w_branch_attn, w_branch_ssm, w_out, norm2_g, w_ffn_gate, w_ffn_up, w_ffn_down,
              norm_f_g):
    pos_p = jnp.arange(x_prompt.shape[1], dtype=jnp.int32)
    pos_s = PAST_LEN + jnp.arange(x_sample.shape[1], dtype=jnp.int32)
    zeros_h = jnp.zeros((x_prompt.shape[0], N_SSM_GROUPS, SSM_STATE), x_prompt.dtype)
    xp, xs = x_prompt, x_sample
    kp_l, vp_l, hrp_l, hip_l = [], [], [], []
    ks_l, vs_l, hrs_l, his_l = [], [], [], []
    for l in range(DEPTH):
        weights = (norm1_g[l], w_in[l], b_gate[l], attn_sinks[l], ssm_lam_re[l], ssm_lam_im[l],
                   ssm_log_dt[l], ssm_b_re[l], ssm_b_im[l], ssm_c_re[l], ssm_c_im[l], ssm_d[l],
                   w_glu[l], b_glu[l], w_branch_attn[l], w_branch_ssm[l], w_out[l], norm2_g[l],
                   w_ffn_gate[l], w_ffn_up[l], w_ffn_down[l])
        xp, kp, vp, hrp, hip = hybrid_layer(xp, pos_p, swa_prompt, zeros_h, zeros_h, *weights)
        kbuf, vbuf = state_k_win[l], state_v_win[l]
        attend_s = lambda q, k, v, s, kb=kbuf, vb=vbuf: swa_sample(q, k, v, s, kb, vb)
        xs, ksn, vsn, hrs, his = hybrid_layer(xs, pos_s, attend_s, state_ssm_re[l], state_ssm_im[l], *weights)
        kp_l.append(kp); vp_l.append(vp); hrp_l.append(hrp); hip_l.append(hip)
        ks_l.append(ksn); vs_l.append(vsn); hrs_l.append(hrs); his_l.append(his)
    y_prompt = rms_norm(xp, norm_f_g)
    y_sample = rms_norm(xs, norm_f_g)
    return (y_prompt, y_sample,
            jnp.stack(kp_l), jnp.stack(vp_l), jnp.stack(hrp_l), jnp.stack(hip_l),
            jnp.stack(ks_l), jnp.stack(vs_l), jnp.stack(hrs_l), jnp.stack(his_l))
```

```python
import math

import jax
import jax.numpy as jnp
from jax import lax
from jax.experimental import pallas as pl
from jax.experimental.pallas import tpu as pltpu

D_MODEL = 1024
N_Q_HEADS = 8
N_KV_HEADS = 2
HEAD_DIM = 64
ATTN_WIDTH = N_Q_HEADS * HEAD_DIM
KV_WIDTH = N_KV_HEADS * HEAD_DIM
WINDOW = 128
ROPE_DIM = HEAD_DIM // 4
ROPE_HALF = ROPE_DIM // 2
ROPE_THETA = 500000.0
SSM_WIDTH = D_MODEL // 2
SSM_GROUP = 16
N_SSM_GROUPS = SSM_WIDTH // SSM_GROUP
SSM_STATE = 64
N_STATES = N_SSM_GROUPS * SSM_STATE
GATE_WIDTH = 2 * D_MODEL
D_FF = -(-8 * D_MODEL // (3 * 256)) * 256
NORM_EPS = 1e-5
PAST_LEN = 8192

LANES = 128
CHUNK = 8
OCT = LANES // SSM_GROUP
N_OCT = N_SSM_GROUPS // OCT
OCT_STATES = OCT * SSM_STATE
OCT_COL = OCT_STATES // LANES
N_COL = N_STATES // LANES
S5_ROWS = 128
SEG = 16
MIX_TB = 512
FFN_TB = 512
NEG_BIG = -1e30
VMEM_LIMIT = 56 * 1024 * 1024

BF = jnp.bfloat16
F32 = jnp.float32


def _dot(a, b):
    return jnp.dot(a, b, preferred_element_type=F32)


def _dot_nt(a, b):
    return lax.dot_general(a, b, (((1,), (1,)), ((), ())), preferred_element_type=F32)


def _rms(x, g):
    return x * lax.rsqrt(jnp.mean(x * x, axis=-1, keepdims=True) + NORM_EPS) * g


def _sigmoid(x):
    return 1.0 / (1.0 + jnp.exp(-x))


def _gelu_tanh(x):
    c = math.sqrt(2.0 / math.pi)
    return 0.5 * x * (1.0 + jnp.tanh(c * (x + 0.044715 * (x * x * x))))


def _rope(a, rc, rs1, rs2):
    return a * rc + pltpu.roll(a, ROPE_HALF, 1) * rs1 + pltpu.roll(a, LANES - ROPE_HALF, 1) * rs2


def _const_spec(shape):
    nd = len(shape)
    return pl.BlockSpec(shape, lambda *_: (0,) * nd, pipeline_mode=pl.Buffered(1))


def _s5_prompt_kernel(x_ref, g1_ref, wu_ref, m_ref, e_ref, f_ref, a_ref, d_ref, wglu_ref, bglu_ref,
                      out_ref, hfin_ref,
                      sre, sim, xre, xim, car):
    blk = pl.program_id(1)

    @pl.when(blk == 0)
    def _():
        car[...] = jnp.zeros_like(car)

    g1 = g1_ref[...]
    hn = jnp.concatenate(
        [_rms(x_ref[:, i * D_MODEL:(i + 1) * D_MODEL], g1) for i in range(CHUNK)], axis=0).astype(BF)
    u = _dot(hn, wu_ref[...])
    ub = u.astype(BF)
    rows = S5_ROWS

    y_in = []
    for o in range(N_OCT):
        uo = jnp.concatenate(
            [ub[i * rows:(i + 1) * rows, o * LANES:(o + 1) * LANES] for i in range(CHUNK)], axis=1)
        s = _dot(uo, e_ref[o])
        for cc in range(OCT_COL):
            sre[o * OCT_COL + cc] = s[:, cc * LANES:(cc + 1) * LANES]
            sim[o * OCT_COL + cc] = s[:, OCT_STATES + cc * LANES:OCT_STATES + (cc + 1) * LANES]
        y_in.append(_dot(uo, m_ref[o]))

    sub = lax.broadcasted_iota(jnp.int32, (8, LANES), 0)

    def column(c, carry):
        tab = a_ref[c]
        are, aim = tab[0:1], tab[1:2]
        bre, bim = tab[2:3], tab[3:4]

        def step(cr, ci, r):
            s_r = sre[c, pl.ds(r, 8, stride=SEG), :]
            s_i = sim[c, pl.ds(r, 8, stride=SEG), :]
            return are * cr - aim * ci + s_r, are * ci + aim * cr + s_i

        cr = jnp.zeros((8, LANES), F32)
        ci = jnp.zeros((8, LANES), F32)
        for r in range(SEG):
            cr, ci = step(cr, ci, r)
        cv = car[c]
        pr, pi = cv[0:1], cv[1:2]
        sr = jnp.zeros((8, LANES), F32)
        si = jnp.zeros((8, LANES), F32)
        for s in range(8):
            sr = jnp.where(sub == s, pr, sr)
            si = jnp.where(sub == s, pi, si)
            pr, pi = (bre * pr - bim * pi + cr[s:s + 1], bre * pi + bim * pr + ci[s:s + 1])
        end = jnp.where(sub == 0, pr, jnp.where(sub == 1, pi, 0.0))
        car[c] = end
        hfin_ref[c] = end
        cr, ci = sr, si
        for r in range(SEG):
            xre[c, pl.ds(r, 8, stride=SEG), :] = cr
            xim[c, pl.ds(r, 8, stride=SEG), :] = ci
            cr, ci = step(cr, ci, r)
        return carry

    lax.fori_loop(0, N_COL, column, 0)

    y_oct = []
    for o in range(N_OCT):
        cols = range(o * OCT_COL, (o + 1) * OCT_COL)
        hp = jnp.concatenate([xre[c] for c in cols] + [xim[c] for c in cols], axis=1).astype(BF)
        y_oct.append(y_in[o] + _dot(hp, f_ref[o]))
    y = jnp.concatenate(
        [jnp.concatenate([y_oct[o][:, j * LANES:(j + 1) * LANES] for o in range(N_OCT)], axis=1)
         for j in range(CHUNK)], axis=0)
    y = y + d_ref[...] * u
    z = _gelu_tanh(y)
    gate = _dot(z.astype(BF), wglu_ref[...]) + bglu_ref[...]
    res = (z * _sigmoid(gate)).astype(out_ref.dtype)
    for j in range(CHUNK):
        out_ref[:, j * SSM_WIDTH:(j + 1) * SSM_WIDTH] = res[j * rows:(j + 1) * rows, :]


def _s5_prompt(x, g1, wu, m_op, e_op, f_op, a_tab, d, wglu, bglu):
    b, t, _ = x.shape
    rows_per_seq = t // CHUNK
    nblk = rows_per_seq // S5_ROWS
    x2 = x.reshape(b * rows_per_seq, CHUNK * D_MODEL)
    row_map = lambda i, j: (i * nblk + j, 0)
    out, hfin = pl.pallas_call(
        _s5_prompt_kernel,
        grid=(b, nblk),
        in_specs=[
            pl.BlockSpec((S5_ROWS, CHUNK * D_MODEL), row_map),
            _const_spec((1, D_MODEL)),
            _const_spec((D_MODEL, SSM_WIDTH)),
            _const_spec(m_op.shape),
            _const_spec(e_op.shape),
            _const_spec(f_op.shape),
            _const_spec(a_tab.shape),
            _const_spec((1, SSM_WIDTH)),
            _const_spec((SSM_WIDTH, SSM_WIDTH)),
            _const_spec((1, SSM_WIDTH)),
        ],
        out_specs=[
            pl.BlockSpec((S5_ROWS, CHUNK * SSM_WIDTH), row_map),
            pl.BlockSpec((None, N_COL, 8, LANES), lambda i, j: (i, 0, 0, 0)),
        ],
        out_shape=[
            jax.ShapeDtypeStruct((b * rows_per_seq, CHUNK * SSM_WIDTH), BF),
            jax.ShapeDtypeStruct((b, N_COL, 8, LANES), F32),
        ],
        scratch_shapes=[
            pltpu.VMEM((N_COL, S5_ROWS, LANES), F32), pltpu.VMEM((N_COL, S5_ROWS, LANES), F32),
            pltpu.VMEM((N_COL, S5_ROWS, LANES), F32), pltpu.VMEM((N_COL, S5_ROWS, LANES), F32),
            pltpu.VMEM((N_COL, 8, LANES), F32),
        ],
        compiler_params=pltpu.CompilerParams(
            dimension_semantics=("arbitrary", "arbitrary"), vmem_limit_bytes=VMEM_LIMIT),
        name="s5_prompt",
    )(x2, g1, wu, m_op, e_op, f_op, a_tab, d, wglu, bglu)
    return out.reshape(b * t, SSM_WIDTH), hfin[:, :, 0, :], hfin[:, :, 1, :]


def _merge_out(x, hn, attn_proj, ssm, wg_ref, bg_ref, wbs_ref, wo_ref):
    gates = _dot(hn, wg_ref[...]) + bg_ref[...]
    merged = (_sigmoid(gates[:, :D_MODEL]) * attn_proj
              + _sigmoid(gates[:, D_MODEL:]) * _dot(ssm, wbs_ref[...]))
    return x + _dot(merged.astype(BF), wo_ref[...])


def _half_split(a, ar, lo):
    z = jnp.zeros_like(a)
    return (jnp.where(lo, a, z).astype(BF), jnp.where(lo, z, ar).astype(BF),
            jnp.where(lo, ar, z).astype(BF), jnp.where(lo, z, a).astype(BF))


def _softmax_sink(s, bias, sink):
    s = s + bias
    m = jnp.maximum(jnp.max(s, axis=-1, keepdims=True), sink)
    p = jnp.exp(s - m)
    den = jnp.sum(p, axis=-1, keepdims=True) + jnp.exp(sink - m)
    return (p * (1.0 / den)).astype(BF)


def _mix_prompt_kernel(sinks_ref, x_ref, ssm_ref, rc_ref, rs1_ref, rs2_ref, g1_ref, wqkv_ref,
                       wg_ref, bg_ref, wba_ref, wbs_ref, wo_ref,
                       x1_ref, kwin_ref, vwin_ref, kprev, vprev):
    t = pl.program_id(1)

    @pl.when(t == 0)
    def _():
        kprev[...] = jnp.zeros_like(kprev)
        vprev[...] = jnp.zeros_like(vprev)

    x = x_ref[...]
    hn = _rms(x, g1_ref[...]).astype(BF)
    qkv = _dot(hn, wqkv_ref[...])
    rc, rs1, rs2 = rc_ref[...], rs1_ref[...], rs2_ref[...]
    scale = HEAD_DIM ** -0.5
    q = [(_rope(qkv[:, c * LANES:(c + 1) * LANES], rc, rs1, rs2) * scale).astype(BF)
         for c in range(ATTN_WIDTH // LANES)]
    k = _rope(qkv[:, ATTN_WIDTH:ATTN_WIDTH + KV_WIDTH], rc, rs1, rs2)
    v = qkv[:, ATTN_WIDTH + KV_WIDTH:ATTN_WIDTH + 2 * KV_WIDTH]

    w = WINDOW
    n_sub = MIX_TB // w
    lane = lax.broadcasted_iota(jnp.int32, (2 * w, LANES), 1)
    lo = lane < HEAD_DIM
    qi = lax.broadcasted_iota(jnp.int32, (w, 2 * w), 0)
    kj = lax.broadcasted_iota(jnp.int32, (w, 2 * w), 1)
    band = (kj > qi) & (kj <= qi + w)
    first = band & ((kj >= w) | (t > 0))
    bias_band = jnp.where(band, 0.0, NEG_BIG).astype(F32)
    bias_first = jnp.where(first, 0.0, NEG_BIG).astype(F32)
    row = lax.broadcasted_iota(jnp.int32, (2 * w, 1), 0)

    def sink_col(ha, hb):
        return jnp.where(row < w, sinks_ref[ha], sinks_ref[hb])

    attn_rows = []
    for sb in range(n_sub):
        cur = slice(sb * w, (sb + 1) * w)
        if sb == 0:
            kcat = jnp.concatenate([kprev[...], k[cur]], axis=0)
            vcat = jnp.concatenate([vprev[...], v[cur]], axis=0)
            bias = bias_first
        else:
            kcat = k[(sb - 1) * w:(sb + 1) * w]
            vcat = v[(sb - 1) * w:(sb + 1) * w]
            bias = bias_band
        bias2 = jnp.concatenate([bias, bias], axis=0)
        ka, kb, kc, kd = _half_split(kcat, pltpu.roll(kcat, HEAD_DIM, 1), lo)
        va, vb, vc, vd = _half_split(vcat, pltpu.roll(vcat, HEAD_DIM, 1), lo)
        outs = []
        for grp, (k_e, k_o, v_e, v_o) in enumerate(((ka, kb, va, vb), (kc, kd, vc, vd))):
            qq = jnp.concatenate([q[2 * grp][cur], q[2 * grp + 1][cur]], axis=0)
            p_e = _softmax_sink(_dot_nt(qq, k_e), bias2, sink_col(4 * grp, 4 * grp + 2))
            p_o = _softmax_sink(_dot_nt(qq, k_o), bias2, sink_col(4 * grp + 1, 4 * grp + 3))
            o2 = _dot(p_e, v_e) + _dot(p_o, v_o)
            outs += [o2[:w], o2[w:]]
        attn_rows.append(jnp.concatenate(outs, axis=1))
    attn = jnp.concatenate(attn_rows, axis=0).astype(BF)

    kprev[...] = k[MIX_TB - w:]
    vprev[...] = v[MIX_TB - w:]
    kwin_ref[...] = k[MIX_TB - w:]
    vwin_ref[...] = v[MIX_TB - w:]
    x1_ref[...] = _merge_out(x, hn, _dot(attn, wba_ref[...]), ssm_ref[...], wg_ref, bg_ref,
                             wbs_ref, wo_ref)


def _mix_prompt(x, ssm, rc, rs1, rs2, sinks, g1, wqkv, wg, bg, wba, wbs, wo):
    b, t, _ = x.shape
    nblk = t // MIX_TB
    x2 = x.reshape(b * t, D_MODEL)
    row_map = lambda i, j: (i * nblk + j, 0)
    x1, kwin, vwin = pl.pallas_call(
        _mix_prompt_kernel,
        grid=(b, nblk),
        in_specs=[
            pl.BlockSpec(memory_space=pltpu.SMEM),
            pl.BlockSpec((MIX_TB, D_MODEL), row_map),
            pl.BlockSpec((MIX_TB, SSM_WIDTH), row_map),
            pl.BlockSpec((MIX_TB, LANES), lambda i, j: (j, 0)),
            pl.BlockSpec((MIX_TB, LANES), lambda i, j: (j, 0)),
            pl.BlockSpec((MIX_TB, LANES), lambda i, j: (j, 0)),
            _const_spec((1, D_MODEL)),
            _const_spec(wqkv.shape),
            _const_spec(wg.shape),
            _const_spec(bg.shape),
            _const_spec(wba.shape),
            _const_spec(wbs.shape),
            _const_spec(wo.shape),
        ],
        out_specs=[
            pl.BlockSpec((MIX_TB, D_MODEL), row_map),
            pl.BlockSpec((None, WINDOW, KV_WIDTH), lambda i, j: (i, 0, 0)),
            pl.BlockSpec((None, WINDOW, KV_WIDTH), lambda i, j: (i, 0, 0)),
        ],
        out_shape=[
            jax.ShapeDtypeStruct((b * t, D_MODEL), F32),
            jax.ShapeDtypeStruct((b, WINDOW, KV_WIDTH), F32),
            jax.ShapeDtypeStruct((b, WINDOW, KV_WIDTH), F32),
        ],
        scratch_shapes=[pltpu.VMEM((WINDOW, KV_WIDTH), F32), pltpu.VMEM((WINDOW, KV_WIDTH), F32)],
        compiler_params=pltpu.CompilerParams(
            dimension_semantics=("arbitrary", "arbitrary"), vmem_limit_bytes=VMEM_LIMIT),
        name="mix_prompt",
    )(sinks, x2, ssm, rc, rs1, rs2, g1, wqkv, wg, bg, wba, wbs, wo)
    return x1, kwin, vwin


QROWS = 16
SAMPLE_TB = 32


def _sample_attn_kernel(x_ref, kbuf_ref, vbuf_ref, rc_ref, rs1_ref, rs2_ref, sinkc_ref, g1_ref, wqkv_ref,
                        o_ref, kout_ref, vout_ref, qz, knew, vnew):
    nb = x_ref.shape[0]
    hn = _rms(x_ref[...], g1_ref[...]).astype(BF)
    qkv = _dot(hn, wqkv_ref[...])
    rc, rs1, rs2 = rc_ref[...], rs1_ref[...], rs2_ref[...]
    scale = HEAD_DIM ** -0.5
    knew[...] = _rope(qkv[:, ATTN_WIDTH:ATTN_WIDTH + KV_WIDTH], rc, rs1, rs2)
    vnew[...] = qkv[:, ATTN_WIDTH + KV_WIDTH:ATTN_WIDTH + 2 * KV_WIDTH]

    lane = lax.broadcasted_iota(jnp.int32, (nb, LANES), 1)
    lo = lane < HEAD_DIM
    qz[...] = jnp.zeros_like(qz)
    for c in range(ATTN_WIDTH // LANES):
        qc = _rope(qkv[:, c * LANES:(c + 1) * LANES], rc, rs1, rs2) * scale
        qr = pltpu.roll(qc, HEAD_DIM, 1)
        zero = jnp.zeros_like(qc)
        if c < 2:
            even, odd = jnp.where(lo, qc, zero), jnp.where(lo, qr, zero)
        else:
            even, odd = jnp.where(lo, zero, qr), jnp.where(lo, zero, qc)
        qz[pl.ds(2 * c, nb, stride=QROWS), :] = even
        qz[pl.ds(2 * c + 1, nb, stride=QROWS), :] = odd

    sink = sinkc_ref[...]

    def body(b, carry):
        kout_ref[b, 0:WINDOW - 1, :] = kbuf_ref[b, 1:WINDOW, :]
        vout_ref[b, 0:WINDOW - 1, :] = vbuf_ref[b, 1:WINDOW, :]
        kout_ref[b, WINDOW - 1:WINDOW, :] = knew[pl.ds(b, 1), :]
        vout_ref[b, WINDOW - 1:WINDOW, :] = vnew[pl.ds(b, 1), :]
        qb = qz[pl.ds(b * QROWS, QROWS), :].astype(BF)
        s = _dot_nt(qb, kout_ref[b].astype(BF))
        m = jnp.maximum(jnp.max(s, axis=-1, keepdims=True), sink)
        p = jnp.exp(s - m)
        den = jnp.sum(p, axis=-1, keepdims=True) + jnp.exp(sink - m)
        p = (p * (1.0 / den)).astype(BF)
        o_ref[pl.ds(b * QROWS, QROWS), :] = _dot(p, vout_ref[b].astype(BF))
        return carry

    lax.fori_loop(0, nb, body, 0)


def _sample_attn(x, kbuf, vbuf, rc, rs1, rs2, sinkc, g1, wqkv):
    nb = x.shape[0]
    tb = SAMPLE_TB
    return pl.pallas_call(
        _sample_attn_kernel,
        grid=(nb // tb,),
        in_specs=[
            pl.BlockSpec((tb, D_MODEL), lambda i: (i, 0)),
            pl.BlockSpec((tb, WINDOW, KV_WIDTH), lambda i: (i, 0, 0)),
            pl.BlockSpec((tb, WINDOW, KV_WIDTH), lambda i: (i, 0, 0)),
            _const_spec(rc.shape), _const_spec(rs1.shape), _const_spec(rs2.shape),
            _const_spec(sinkc.shape), _const_spec(g1.shape), _const_spec(wqkv.shape),
        ],
        out_specs=[
            pl.BlockSpec((tb * QROWS, LANES), lambda i: (i, 0)),
            pl.BlockSpec((tb, WINDOW, KV_WIDTH), lambda i: (i, 0, 0)),
            pl.BlockSpec((tb, WINDOW, KV_WIDTH), lambda i: (i, 0, 0)),
        ],
        out_shape=[
            jax.ShapeDtypeStruct((nb * QROWS, LANES), F32),
            jax.ShapeDtypeStruct((nb, WINDOW, KV_WIDTH), F32),
            jax.ShapeDtypeStruct((nb, WINDOW, KV_WIDTH), F32),
        ],
        scratch_shapes=[pltpu.VMEM((tb * QROWS, LANES), F32),
                        pltpu.VMEM((tb, KV_WIDTH), F32), pltpu.VMEM((tb, KV_WIDTH), F32)],
        compiler_params=pltpu.CompilerParams(
            dimension_semantics=("arbitrary",), vmem_limit_bytes=VMEM_LIMIT),
        name="sample_attn",
    )(x, kbuf, vbuf, rc, rs1, rs2, sinkc, g1, wqkv)


def _sample_tail_kernel(x_ref, o3_ref, h0re_ref, h0im_ref, g1_ref, wu_ref, wg_ref, bg_ref, wbad_ref,
                        wbs_ref, wo_ref, lb_ref, bblk_ref, cblk_ref, d_ref, wglu_ref, bglu_ref,
                        x1_ref, hre_ref, him_ref):
    nb = x_ref.shape[0]
    x = x_ref[...]
    hn = _rms(x, g1_ref[...]).astype(BF)

    lane = lax.broadcasted_iota(jnp.int32, (nb, LANES), 1)
    lo = lane < HEAD_DIM
    a = jnp.zeros((nb, D_MODEL), F32)
    zero = jnp.zeros((nb, LANES), F32)
    for h in range(N_Q_HEADS):
        oh = o3_ref[pl.ds(h, nb, stride=QROWS), :]
        oh = jnp.where(lo, oh, zero) if h < N_Q_HEADS // 2 else jnp.where(lo, zero, oh)
        a = a + _dot(oh.astype(BF), wbad_ref[h])

    u = _dot(hn, wu_ref[...])
    ub = u.astype(BF)
    lre, lim = lb_ref[0:1, :], lb_ref[1:2, :]
    y_cols = []
    for o in range(N_OCT):
        sl = slice(o * OCT_STATES, (o + 1) * OCT_STATES)
        bu = _dot(ub[:, o * LANES:(o + 1) * LANES], bblk_ref[o])
        h0r, h0i = h0re_ref[:, sl], h0im_ref[:, sl]
        hr = bu[:, :OCT_STATES] + (lre[:, sl] * h0r - lim[:, sl] * h0i)
        hi = bu[:, OCT_STATES:] + (lre[:, sl] * h0i + lim[:, sl] * h0r)
        hre_ref[:, sl] = hr
        him_ref[:, sl] = hi
        y_cols.append(_dot(jnp.concatenate([hr, hi], axis=1).astype(BF), cblk_ref[o]))
    y = jnp.concatenate(y_cols, axis=1) + d_ref[...] * u
    z = _gelu_tanh(y)
    gate = _dot(z.astype(BF), wglu_ref[...]) + bglu_ref[...]
    ssm = (z * _sigmoid(gate)).astype(BF)

    x1_ref[...] = _merge_out(x, hn, a, ssm, wg_ref, bg_ref, wbs_ref, wo_ref)


def _sample_tail(x, o3, h0re, h0im, g1, wu, wg, bg, wbad, wbs, wo, lb, bblk, cblk, d, wglu, bglu):
    nb = x.shape[0]
    args = (x, o3, h0re, h0im, g1, wu, wg, bg, wbad, wbs, wo, lb, bblk, cblk, d, wglu, bglu)
    out_shapes = ((nb, D_MODEL), (nb, N_STATES), (nb, N_STATES))
    return pl.pallas_call(
        _sample_tail_kernel,
        grid=(1,),
        in_specs=[_const_spec(a.shape) for a in args],
        out_specs=[pl.BlockSpec(s, lambda i: (0, 0)) for s in out_shapes],
        out_shape=[jax.ShapeDtypeStruct(s, F32) for s in out_shapes],
        compiler_params=pltpu.CompilerParams(
            dimension_semantics=("arbitrary",), vmem_limit_bytes=VMEM_LIMIT),
        name="sample_tail",
    )(*args)


def _ffn_kernel(x_ref, g2_ref, wgate_ref, wup_ref, wdown_ref, gf_ref, y_ref):
    x = x_ref[...]
    h = _rms(x, g2_ref[...]).astype(BF)
    gate = _dot(h, wgate_ref[...])
    up = _dot(h, wup_ref[...])
    act = (gate * _sigmoid(gate) * up).astype(BF)
    x2 = x + _dot(act, wdown_ref[...])
    y_ref[...] = _rms(x2, gf_ref[...])


def _ffn(x, g2, wgate, wup, wdown, gf, tb):
    n = x.shape[0]
    return pl.pallas_call(
        _ffn_kernel,
        grid=(n // tb,),
        in_specs=[
            pl.BlockSpec((tb, D_MODEL), lambda i: (i, 0)),
            _const_spec((1, D_MODEL)),
            _const_spec(wgate.shape),
            _const_spec(wup.shape),
            _const_spec(wdown.shape),
            _const_spec((1, D_MODEL)),
        ],
        out_specs=pl.BlockSpec((tb, D_MODEL), lambda i: (i, 0)),
        out_shape=jax.ShapeDtypeStruct((n, D_MODEL), F32),
        compiler_params=pltpu.CompilerParams(
            dimension_semantics=("arbitrary",), vmem_limit_bytes=VMEM_LIMIT),
        name="ffn",
    )(x, g2, wgate, wup, wdown, gf)


def _rope_tables(pos):
    inv_freq = ROPE_THETA ** (-(jnp.arange(ROPE_HALF, dtype=F32) * 2.0 / ROPE_DIM))
    ang = pos.astype(F32)[:, None] * inv_freq[None, :]
    cos, sin = jnp.cos(ang), jnp.sin(ang)
    n = pos.shape[0]
    pad = jnp.zeros((n, HEAD_DIM - ROPE_DIM), F32)
    zero = jnp.zeros_like(sin)
    rc = jnp.concatenate([cos, cos, pad + 1.0], axis=1)
    rs1 = jnp.concatenate([zero, sin, pad], axis=1)
    rs2 = jnp.concatenate([-sin, zero, pad], axis=1)
    rep = LANES // HEAD_DIM
    return jnp.tile(rc, (1, rep)), jnp.tile(rs1, (1, rep)), jnp.tile(rs2, (1, rep))


def _cmul(ar, ai, br, bi):
    return ar * br - ai * bi, ar * bi + ai * br


def _ssm_operators(lam_re, lam_im, log_dt, b_re, b_im, c_re, c_im):
    hp = lax.Precision.HIGHEST
    dt = jnp.exp(log_dt)[:, None]
    mag = jnp.exp(lam_re * dt)
    lb_re = mag * jnp.cos(lam_im * dt)
    lb_im = mag * jnp.sin(lam_im * dt)
    den = lam_re * lam_re + lam_im * lam_im
    nr = lb_re - 1.0
    k_re = ((nr * lam_re + lb_im * lam_im) / den)[..., None]
    k_im = ((lb_im * lam_re - nr * lam_im) / den)[..., None]
    bb_re = k_re * b_re - k_im * b_im
    bb_im = k_re * b_im + k_im * b_re

    pw_re, pw_im = [jnp.ones_like(lb_re)], [jnp.zeros_like(lb_im)]
    for _ in range(CHUNK):
        r, i = _cmul(pw_re[-1], pw_im[-1], lb_re, lb_im)
        pw_re.append(r)
        pw_im.append(i)
    a_re, a_im = pw_re[CHUNK], pw_im[CHUNK]
    s_re, s_im = a_re, a_im
    for _ in range(int(math.log2(SEG))):
        s_re, s_im = _cmul(s_re, s_im, s_re, s_im)
    lp_re = jnp.stack(pw_re[CHUNK - 1::-1])
    lp_im = jnp.stack(pw_im[CHUNK - 1::-1])
    pw_re = jnp.stack(pw_re)
    pw_im = jnp.stack(pw_im)

    eye = jnp.eye(OCT, dtype=F32)
    g4 = lambda a: a.reshape(a.shape[:-3] + (N_OCT, OCT) + a.shape[-2:])

    e_re = lp_re[..., None] * bb_re[None] - lp_im[..., None] * bb_im[None]
    e_im = lp_re[..., None] * bb_im[None] + lp_im[..., None] * bb_re[None]
    e_ri = jnp.stack([e_re, e_im], axis=1)
    e_op = jnp.einsum('irogph,gk->oighrkp', g4(e_ri), eye, precision=hp)
    e_op = e_op.reshape(N_OCT, CHUNK * LANES, 2 * OCT_STATES)

    cl_re = c_re[None] * pw_re[1:, :, None, :] - c_im[None] * pw_im[1:, :, None, :]
    cl_im = c_re[None] * pw_im[1:, :, None, :] + c_im[None] * pw_re[1:, :, None, :]
    f_ri = jnp.stack([cl_re, -cl_im], axis=1)
    f_op = jnp.einsum('jroghp,gk->orgpjkh', g4(f_ri), eye, precision=hp)
    f_op = f_op.reshape(N_OCT, 2 * OCT_STATES, CHUNK * LANES)

    cp_re = c_re[None] * pw_re[:CHUNK, :, None, :] - c_im[None] * pw_im[:CHUNK, :, None, :]
    cp_im = c_re[None] * pw_im[:CHUNK, :, None, :] + c_im[None] * pw_re[:CHUNK, :, None, :]
    k_tau = (jnp.einsum('tgap,gph->tgah', cp_re, bb_re, precision=hp)
             - jnp.einsum('tgap,gph->tgah', cp_im, bb_im, precision=hp))
    ii = jnp.arange(CHUNK)
    lag = ii[None, :] - ii[:, None]
    k_ij = jnp.where((lag >= 0)[:, :, None, None, None], k_tau[jnp.maximum(lag, 0)], 0.0)
    m_op = jnp.einsum('ijogah,gk->oighjka', g4(k_ij), eye, precision=hp)
    m_op = m_op.reshape(N_OCT, CHUNK * LANES, CHUNK * LANES)

    bblk = jnp.einsum('rogph,gk->oghrkp', g4(jnp.stack([bb_re, bb_im])), eye, precision=hp)
    bblk = bblk.reshape(N_OCT, LANES, 2 * OCT_STATES)
    cblk = jnp.einsum('roghp,gk->orgpkh', g4(jnp.stack([c_re, -c_im])), eye, precision=hp)
    cblk = cblk.reshape(N_OCT, 2 * OCT_STATES, LANES)

    flat = lambda a: a.reshape(1, N_STATES)
    col = lambda a: a.reshape(N_COL, 1, LANES)
    a_tab = jnp.concatenate([col(a_re), col(a_im), col(s_re), col(s_im),
                             jnp.zeros((N_COL, 4, LANES), F32)], axis=1)
    lb = jnp.concatenate([flat(lb_re), flat(lb_im)], axis=0)
    return (m_op.astype(BF), e_op.astype(BF), f_op.astype(BF), a_tab, lb,
            bblk.astype(BF), cblk.astype(BF))


def kernel(x_prompt, x_sample, state_k_win, state_v_win, state_ssm_re, state_ssm_im, norm1_g, w_in, b_gate, attn_sinks, ssm_lam_re, ssm_lam_im, ssm_log_dt, ssm_b_re, ssm_b_im, ssm_c_re, ssm_c_im, ssm_d, w_glu, b_glu, w_branch_attn, w_branch_ssm, w_out, norm2_g, w_ffn_gate, w_ffn_up, w_ffn_down, norm_f_g):
    depth = w_in.shape[0]
    assert depth == 1
    b, t, _ = x_prompt.shape
    nb, s_len, _ = x_sample.shape
    assert s_len == 1 and state_k_win.shape[2] == WINDOW
    l = 0
    o1 = ATTN_WIDTH + 2 * KV_WIDTH
    o2 = o1 + SSM_WIDTH
    w_in_b = w_in[l].astype(BF)
    wqkv, wu, wg = w_in_b[:, :o1], w_in_b[:, o1:o2], w_in_b[:, o2:]
    g1 = norm1_g[l].reshape(1, D_MODEL)
    g2 = norm2_g[l].reshape(1, D_MODEL)
    gf = norm_f_g.reshape(1, D_MODEL)
    bg = b_gate[l].reshape(1, GATE_WIDTH)
    d = ssm_d[l].reshape(1, SSM_WIDTH)
    wglu = w_glu[l].astype(BF)
    bglu = b_glu[l].reshape(1, SSM_WIDTH)
    wba = w_branch_attn[l].astype(BF)
    wbs = w_branch_ssm[l].astype(BF)
    wo = w_out[l].astype(BF)
    wgate = w_ffn_gate[l].astype(BF)
    wup = w_ffn_up[l].astype(BF)
    wdown = w_ffn_down[l].astype(BF)
    sinks = attn_sinks[l]

    m_op, e_op, f_op, a_tab, lb, bblk, cblk = _ssm_operators(
        ssm_lam_re[l], ssm_lam_im[l], ssm_log_dt[l], ssm_b_re[l], ssm_b_im[l], ssm_c_re[l], ssm_c_im[l])

    rc, rs1, rs2 = _rope_tables(jnp.arange(t, dtype=jnp.int32))
    ssm_p, hre_p, him_p = _s5_prompt(x_prompt, g1, wu, m_op, e_op, f_op, a_tab, d, wglu, bglu)
    x1_p, kwin_p, vwin_p = _mix_prompt(x_prompt, ssm_p, rc, rs1, rs2, sinks, g1, wqkv, wg, bg, wba, wbs, wo)
    y_p = _ffn(x1_p, g2, wgate, wup, wdown, gf, FFN_TB).reshape(b, t, D_MODEL)

    rcs, rs1s, rs2s = _rope_tables(PAST_LEN + jnp.arange(1, dtype=jnp.int32))
    sinkc = jnp.concatenate([sinks, jnp.zeros((QROWS - N_Q_HEADS,), F32)]).reshape(QROWS, 1)
    wbad = jnp.concatenate([wba.reshape(N_Q_HEADS, HEAD_DIM, D_MODEL)] * 2, axis=1)
    xs = x_sample.reshape(nb, D_MODEL)
    o3, kwin_s, vwin_s = _sample_attn(
        xs, state_k_win[l].reshape(nb, WINDOW, KV_WIDTH), state_v_win[l].reshape(nb, WINDOW, KV_WIDTH),
        rcs, rs1s, rs2s, sinkc, g1, wqkv)
    x1_s, hre_s, him_s = _sample_tail(
        xs, o3, state_ssm_re[l].reshape(nb, N_STATES), state_ssm_im[l].reshape(nb, N_STATES),
        g1, wu, wg, bg, wbad, wbs, wo, lb, bblk, cblk, d, wglu, bglu)
    y_s = _ffn(x1_s, g2, wgate, wup, wdown, gf, nb).reshape(nb, 1, D_MODEL)

    kv_shape_p = (1, b, WINDOW, N_KV_HEADS, HEAD_DIM)
    st_shape_p = (1, b, N_SSM_GROUPS, SSM_STATE)
    kv_shape_s = (1, nb, WINDOW, N_KV_HEADS, HEAD_DIM)
    st_shape_s = (1, nb, N_SSM_GROUPS, SSM_STATE)
    return (y_p, y_s,
            kwin_p.reshape(kv_shape_p), vwin_p.reshape(kv_shape_p),
            hre_p.reshape(st_shape_p), him_p.reshape(st_shape_p),
            kwin_s.reshape(kv_shape_s), vwin_s.reshape(kv_shape_s),
            hre_s.reshape(st_shape_s), him_s.reshape(st_shape_s))
```

```python
import math

import jax
import jax.numpy as jnp
from jax import lax
from jax.experimental import pallas as pl
from jax.experimental.pallas import tpu as pltpu

D_MODEL = 1024
N_Q_HEADS = 8
N_KV_HEADS = 2
HEAD_DIM = 64
ATTN_WIDTH = N_Q_HEADS * HEAD_DIM
KV_WIDTH = N_KV_HEADS * HEAD_DIM
WINDOW = 128
ROPE_DIM = HEAD_DIM // 4
ROPE_HALF = ROPE_DIM // 2
ROPE_THETA = 500000.0
SSM_WIDTH = D_MODEL // 2
SSM_GROUP = 16
N_SSM_GROUPS = SSM_WIDTH // SSM_GROUP
SSM_STATE = 64
N_STATES = N_SSM_GROUPS * SSM_STATE
GATE_WIDTH = 2 * D_MODEL
D_FF = -(-8 * D_MODEL // (3 * 256)) * 256
NORM_EPS = 1e-5
PAST_LEN = 8192

LANES = 128
CHUNK = 8
OCT = LANES // SSM_GROUP
N_OCT = N_SSM_GROUPS // OCT
OCT_STATES = OCT * SSM_STATE
OCT_COL = OCT_STATES // LANES
N_COL = N_STATES // LANES
S5_ROWS = 128
SEG = 16
M_PAIR = 2
MIX_TB = 512
FFN_TB = 512
NEG_BIG = -1e30
VMEM_LIMIT = 56 * 1024 * 1024

BF = jnp.bfloat16
F32 = jnp.float32


def _dot(a, b):
    return jnp.dot(a, b, preferred_element_type=F32)


def _dot_nt(a, b):
    return lax.dot_general(a, b, (((1,), (1,)), ((), ())), preferred_element_type=F32)


def _rms(x, g):
    return x * lax.rsqrt(jnp.mean(x * x, axis=-1, keepdims=True) + NORM_EPS) * g


def _sigmoid(x):
    return 1.0 / (1.0 + jnp.exp(-x))


def _gelu_tanh(x):
    c = math.sqrt(2.0 / math.pi)
    return 0.5 * x * (1.0 + jnp.tanh(c * (x + 0.044715 * (x * x * x))))


def _rope(a, rc, rs1, rs2):
    return a * rc + pltpu.roll(a, ROPE_HALF, 1) * rs1 + pltpu.roll(a, LANES - ROPE_HALF, 1) * rs2


def _const_spec(shape):
    nd = len(shape)
    return pl.BlockSpec(shape, lambda *_: (0,) * nd, pipeline_mode=pl.Buffered(1))


def _build_chunk_operators(bblk_ref, ctblk_ref, lcol_ref, m_s, e_s, f_s):
    for o in range(N_OCT):
        ct = ctblk_ref[o]
        lr, li = lcol_ref[o, 0:1, :], lcol_ref[o, 1:2, :]
        er, ei = bblk_ref[o, :, :OCT_STATES], bblk_ref[o, :, OCT_STATES:]
        k_blk = []
        for tau in range(CHUNK):
            e_cat = jnp.concatenate([er, ei], axis=1)
            i = CHUNK - 1 - tau
            e_s[o, i * LANES:(i + 1) * LANES, :] = e_cat.astype(BF)
            k_blk.append(lax.dot_general(e_cat, ct, (((1,), (1,)), ((), ())),
                                         preferred_element_type=F32,
                                         precision=lax.Precision.HIGHEST).astype(BF))
            er, ei = er * lr - ei * li, er * li + ei * lr
        zero = jnp.zeros((LANES, LANES), BF)
        for j in range(CHUNK):
            jt, jj = divmod(j, M_PAIR)
            for i in range(M_PAIR * (jt + 1)):
                m_s[jt][o, i * LANES:(i + 1) * LANES, jj * LANES:(jj + 1) * LANES] = (
                    k_blk[j - i] if j >= i else zero)
        tr, ti = ct[:, :OCT_STATES], -ct[:, OCT_STATES:]
        for j in range(CHUNK):
            tr, ti = tr * lr - ti * li, tr * li + ti * lr
            f_s[o, :OCT_STATES, j * LANES:(j + 1) * LANES] = tr.T.astype(BF)
            f_s[o, OCT_STATES:, j * LANES:(j + 1) * LANES] = (-ti).T.astype(BF)


def _s5_prompt_kernel(x_ref, g1_ref, wu_ref, bblk_ref, ctblk_ref, lcol_ref, a_ref, d_ref,
                      wglu_ref, bglu_ref,
                      out_ref, hfin_ref,
                      m0_ref, m1_ref, m2_ref, m3_ref, e_ref, f_ref, us, ys, sre, sim, car):
    m_ref = (m0_ref, m1_ref, m2_ref, m3_ref)
    blk = pl.program_id(1)

    @pl.when((pl.program_id(0) == 0) & (blk == 0))
    def _():
        _build_chunk_operators(bblk_ref, ctblk_ref, lcol_ref, m_ref, e_ref, f_ref)

    @pl.when(blk == 0)
    def _():
        car[...] = jnp.zeros_like(car)

    rows = S5_ROWS
    hn = _rms(x_ref[...], g1_ref[...]).astype(BF)
    u = _dot(hn, wu_ref[...])
    for cc in range(N_OCT):
        us[cc] = u[:, cc * LANES:(cc + 1) * LANES]

    y_in = []
    for o in range(N_OCT):
        uo = jnp.concatenate(
            [us[o, pl.ds(i, rows, stride=CHUNK), :].astype(BF) for i in range(CHUNK)], axis=1)
        s = _dot(uo, e_ref[o])
        for cc in range(OCT_COL):
            sre[o * OCT_COL + cc] = s[:, cc * LANES:(cc + 1) * LANES]
            sim[o * OCT_COL + cc] = s[:, OCT_STATES + cc * LANES:OCT_STATES + (cc + 1) * LANES]
        y_in.append(jnp.concatenate(
            [_dot(uo[:, :(jt + 1) * M_PAIR * LANES], m_ref[jt][o]) for jt in range(CHUNK // M_PAIR)],
            axis=1))

    sub = lax.broadcasted_iota(jnp.int32, (8, LANES), 0)

    def column(c, carry):
        tab = a_ref[c]
        are, aim = tab[0:1], tab[1:2]
        bre, bim = tab[2:3], tab[3:4]

        def step(cr, ci, r, keep_entering):
            slab = pl.ds(r, 8, stride=SEG)
            s_r, s_i = sre[c, slab, :], sim[c, slab, :]
            if keep_entering:
                sre[c, slab, :] = cr
                sim[c, slab, :] = ci
            return are * cr - aim * ci + s_r, are * ci + aim * cr + s_i

        cr = jnp.zeros((8, LANES), F32)
        ci = jnp.zeros((8, LANES), F32)
        for r in range(SEG):
            cr, ci = step(cr, ci, r, False)
        cv = car[c]
        pr, pi = cv[0:1], cv[1:2]
        sr = jnp.zeros((8, LANES), F32)
        si = jnp.zeros((8, LANES), F32)
        for s in range(8):
            sr = jnp.where(sub == s, pr, sr)
            si = jnp.where(sub == s, pi, si)
            pr, pi = (bre * pr - bim * pi + cr[s:s + 1], bre * pi + bim * pr + ci[s:s + 1])
        end = jnp.where(sub == 0, pr, jnp.where(sub == 1, pi, 0.0))
        car[c] = end
        hfin_ref[c] = end
        cr, ci = sr, si
        for r in range(SEG):
            cr, ci = step(cr, ci, r, True)
        return carry

    lax.fori_loop(0, N_COL, column, 0)

    for o in range(N_OCT):
        cols = range(o * OCT_COL, (o + 1) * OCT_COL)
        hp = jnp.concatenate([sre[c] for c in cols] + [sim[c] for c in cols], axis=1).astype(BF)
        yo = y_in[o] + _dot(hp, f_ref[o])
        for j in range(CHUNK):
            ys[o, pl.ds(j, rows, stride=CHUNK), :] = yo[:, j * LANES:(j + 1) * LANES]
    y = jnp.concatenate([ys[o] for o in range(N_OCT)], axis=1)
    y = y + d_ref[...] * jnp.concatenate([us[cc] for cc in range(N_OCT)], axis=1)
    z = _gelu_tanh(y)
    gate = _dot(z.astype(BF), wglu_ref[...]) + bglu_ref[...]
    out_ref[...] = (z * _sigmoid(gate)).astype(out_ref.dtype)


def _s5_prompt(x, g1, wu, bblk, ctblk, lcol, a_tab, d, wglu, bglu):
    b, t, _ = x.shape
    tb = S5_ROWS * CHUNK
    nblk = t // tb
    x2 = x.reshape(b * t, D_MODEL)
    row_map = lambda i, j: (i * nblk + j, 0)
    op_shape = (N_OCT, CHUNK * LANES, CHUNK * LANES)
    out, hfin = pl.pallas_call(
        _s5_prompt_kernel,
        grid=(b, nblk),
        in_specs=[
            pl.BlockSpec((tb, D_MODEL), row_map),
            _const_spec((1, D_MODEL)),
            _const_spec((D_MODEL, SSM_WIDTH)),
            _const_spec(bblk.shape),
            _const_spec(ctblk.shape),
            _const_spec(lcol.shape),
            _const_spec(a_tab.shape),
            _const_spec((1, SSM_WIDTH)),
            _const_spec((SSM_WIDTH, SSM_WIDTH)),
            _const_spec((1, SSM_WIDTH)),
        ],
        out_specs=[
            pl.BlockSpec((tb, SSM_WIDTH), row_map),
            pl.BlockSpec((None, N_COL, 8, LANES), lambda i, j: (i, 0, 0, 0)),
        ],
        out_shape=[
            jax.ShapeDtypeStruct((b * t, SSM_WIDTH), BF),
            jax.ShapeDtypeStruct((b, N_COL, 8, LANES), F32),
        ],
        scratch_shapes=[
            *[pltpu.VMEM((N_OCT, (jt + 1) * M_PAIR * LANES, M_PAIR * LANES), BF)
              for jt in range(CHUNK // M_PAIR)],
            pltpu.VMEM(op_shape, BF), pltpu.VMEM(op_shape, BF),
            pltpu.VMEM((N_OCT, tb, LANES), F32), pltpu.VMEM((N_OCT, tb, LANES), F32),
            pltpu.VMEM((N_COL, S5_ROWS, LANES), F32), pltpu.VMEM((N_COL, S5_ROWS, LANES), F32),
            pltpu.VMEM((N_COL, 8, LANES), F32),
        ],
        compiler_params=pltpu.CompilerParams(
            dimension_semantics=("arbitrary", "arbitrary"), vmem_limit_bytes=VMEM_LIMIT),
        name="s5_prompt",
    )(x2, g1, wu, bblk, ctblk, lcol, a_tab, d, wglu, bglu)
    return out, hfin[:, :, 0, :], hfin[:, :, 1, :]


def _merge_out(x, hn, attn_proj, ssm, wg_ref, bg_ref, wbs_ref, wo_ref):
    gates = _dot(hn, wg_ref[...]) + bg_ref[...]
    merged = (_sigmoid(gates[:, :D_MODEL]) * attn_proj
              + _sigmoid(gates[:, D_MODEL:]) * _dot(ssm, wbs_ref[...]))
    return x + _dot(merged.astype(BF), wo_ref[...])


def _half_split(a, ar, lo):
    z = jnp.zeros_like(a)
    return (jnp.where(lo, a, z).astype(BF), jnp.where(lo, z, ar).astype(BF),
            jnp.where(lo, ar, z).astype(BF), jnp.where(lo, z, a).astype(BF))


def _softmax_sink(s, bias, sink):
    s = s + bias
    m = jnp.maximum(jnp.max(s, axis=-1, keepdims=True), sink)
    p = jnp.exp(s - m)
    den = jnp.sum(p, axis=-1, keepdims=True) + jnp.exp(sink - m)
    return (p * (1.0 / den)).astype(BF)


def _mix_prompt_kernel(sinks_ref, x_ref, ssm_ref, rc_ref, rs1_ref, rs2_ref, g1_ref, wqkv_ref,
                       wg_ref, bg_ref, wba_ref, wbs_ref, wo_ref,
                       x1_ref, kwin_ref, vwin_ref, kprev, vprev):
    t = pl.program_id(1)

    @pl.when(t == 0)
    def _():
        kprev[...] = jnp.zeros_like(kprev)
        vprev[...] = jnp.zeros_like(vprev)

    x = x_ref[...]
    hn = _rms(x, g1_ref[...]).astype(BF)
    qkv = _dot(hn, wqkv_ref[...])
    rc, rs1, rs2 = rc_ref[...], rs1_ref[...], rs2_ref[...]
    scale = HEAD_DIM ** -0.5
    q = [(_rope(qkv[:, c * LANES:(c + 1) * LANES], rc, rs1, rs2) * scale).astype(BF)
         for c in range(ATTN_WIDTH // LANES)]
    k = _rope(qkv[:, ATTN_WIDTH:ATTN_WIDTH + KV_WIDTH], rc, rs1, rs2)
    v = qkv[:, ATTN_WIDTH + KV_WIDTH:ATTN_WIDTH + 2 * KV_WIDTH]

    w = WINDOW
    n_sub = MIX_TB // w
    lane = lax.broadcasted_iota(jnp.int32, (2 * w, LANES), 1)
    lo = lane < HEAD_DIM
    qi = lax.broadcasted_iota(jnp.int32, (w, 2 * w), 0)
    kj = lax.broadcasted_iota(jnp.int32, (w, 2 * w), 1)
    band = (kj > qi) & (kj <= qi + w)
    first = band & ((kj >= w) | (t > 0))
    bias_band = jnp.where(band, 0.0, NEG_BIG).astype(F32)
    bias_first = jnp.where(first, 0.0, NEG_BIG).astype(F32)
    row = lax.broadcasted_iota(jnp.int32, (2 * w, 1), 0)

    def sink_col(ha, hb):
        return jnp.where(row < w, sinks_ref[ha], sinks_ref[hb])

    attn_rows = []
    for sb in range(n_sub):
        cur = slice(sb * w, (sb + 1) * w)
        if sb == 0:
            kcat = jnp.concatenate([kprev[...], k[cur]], axis=0)
            vcat = jnp.concatenate([vprev[...], v[cur]], axis=0)
            bias = bias_first
        else:
            kcat = k[(sb - 1) * w:(sb + 1) * w]
            vcat = v[(sb - 1) * w:(sb + 1) * w]
            bias = bias_band
        bias2 = jnp.concatenate([bias, bias], axis=0)
        ka, kb, kc, kd = _half_split(kcat, pltpu.roll(kcat, HEAD_DIM, 1), lo)
        va, vb, vc, vd = _half_split(vcat, pltpu.roll(vcat, HEAD_DIM, 1), lo)
        outs = []
        for grp, (k_e, k_o, v_e, v_o) in enumerate(((ka, kb, va, vb), (kc, kd, vc, vd))):
            qq = jnp.concatenate([q[2 * grp][cur], q[2 * grp + 1][cur]], axis=0)
            p_e = _softmax_sink(_dot_nt(qq, k_e), bias2, sink_col(4 * grp, 4 * grp + 2))
            p_o = _softmax_sink(_dot_nt(qq, k_o), bias2, sink_col(4 * grp + 1, 4 * grp + 3))
            o2 = _dot(p_e, v_e) + _dot(p_o, v_o)
            outs += [o2[:w], o2[w:]]
        attn_rows.append(jnp.concatenate(outs, axis=1))
    attn = jnp.concatenate(attn_rows, axis=0).astype(BF)

    kprev[...] = k[MIX_TB - w:]
    vprev[...] = v[MIX_TB - w:]
    kwin_ref[...] = k[MIX_TB - w:]
    vwin_ref[...] = v[MIX_TB - w:]
    x1_ref[...] = _merge_out(x, hn, _dot(attn, wba_ref[...]), ssm_ref[...], wg_ref, bg_ref,
                             wbs_ref, wo_ref)


def _mix_prompt(x, ssm, rc, rs1, rs2, sinks, g1, wqkv, wg, bg, wba, wbs, wo):
    b, t, _ = x.shape
    nblk = t // MIX_TB
    x2 = x.reshape(b * t, D_MODEL)
    row_map = lambda i, j: (i * nblk + j, 0)
    x1, kwin, vwin = pl.pallas_call(
        _mix_prompt_kernel,
        grid=(b, nblk),
        in_specs=[
            pl.BlockSpec(memory_space=pltpu.SMEM),
            pl.BlockSpec((MIX_TB, D_MODEL), row_map),
            pl.BlockSpec((MIX_TB, SSM_WIDTH), row_map),
            pl.BlockSpec((MIX_TB, LANES), lambda i, j: (j, 0)),
            pl.BlockSpec((MIX_TB, LANES), lambda i, j: (j, 0)),
            pl.BlockSpec((MIX_TB, LANES), lambda i, j: (j, 0)),
            _const_spec((1, D_MODEL)),
            _const_spec(wqkv.shape),
            _const_spec(wg.shape),
            _const_spec(bg.shape),
            _const_spec(wba.shape),
            _const_spec(wbs.shape),
            _const_spec(wo.shape),
        ],
        out_specs=[
            pl.BlockSpec((MIX_TB, D_MODEL), row_map),
            pl.BlockSpec((None, WINDOW, KV_WIDTH), lambda i, j: (i, 0, 0)),
            pl.BlockSpec((None, WINDOW, KV_WIDTH), lambda i, j: (i, 0, 0)),
        ],
        out_shape=[
            jax.ShapeDtypeStruct((b * t, D_MODEL), F32),
            jax.ShapeDtypeStruct((b, WINDOW, KV_WIDTH), F32),
            jax.ShapeDtypeStruct((b, WINDOW, KV_WIDTH), F32),
        ],
        scratch_shapes=[pltpu.VMEM((WINDOW, KV_WIDTH), F32), pltpu.VMEM((WINDOW, KV_WIDTH), F32)],
        compiler_params=pltpu.CompilerParams(
            dimension_semantics=("arbitrary", "arbitrary"), vmem_limit_bytes=VMEM_LIMIT),
        name="mix_prompt",
    )(sinks, x2, ssm, rc, rs1, rs2, g1, wqkv, wg, bg, wba, wbs, wo)
    return x1, kwin, vwin


QROWS = 16
SAMPLE_TB = 32


def _sample_attn_kernel(x_ref, kbuf_ref, vbuf_ref, rc_ref, rs1_ref, rs2_ref, sinkc_ref, g1_ref, wqkv_ref,
                        o_ref, kout_ref, vout_ref, qz, knew, vnew):
    nb = x_ref.shape[0]
    hn = _rms(x_ref[...], g1_ref[...]).astype(BF)
    qkv = _dot(hn, wqkv_ref[...])
    rc, rs1, rs2 = rc_ref[...], rs1_ref[...], rs2_ref[...]
    scale = HEAD_DIM ** -0.5
    knew[...] = _rope(qkv[:, ATTN_WIDTH:ATTN_WIDTH + KV_WIDTH], rc, rs1, rs2)
    vnew[...] = qkv[:, ATTN_WIDTH + KV_WIDTH:ATTN_WIDTH + 2 * KV_WIDTH]

    lane = lax.broadcasted_iota(jnp.int32, (nb, LANES), 1)
    lo = lane < HEAD_DIM
    qz[...] = jnp.zeros_like(qz)
    for c in range(ATTN_WIDTH // LANES):
        qc = _rope(qkv[:, c * LANES:(c + 1) * LANES], rc, rs1, rs2) * scale
        qr = pltpu.roll(qc, HEAD_DIM, 1)
        zero = jnp.zeros_like(qc)
        if c < 2:
            even, odd = jnp.where(lo, qc, zero), jnp.where(lo, qr, zero)
        else:
            even, odd = jnp.where(lo, zero, qr), jnp.where(lo, zero, qc)
        qz[pl.ds(2 * c, nb, stride=QROWS), :] = even
        qz[pl.ds(2 * c + 1, nb, stride=QROWS), :] = odd

    sink = sinkc_ref[...]

    def body(b, carry):
        kout_ref[b, 0:WINDOW - 1, :] = kbuf_ref[b, 1:WINDOW, :]
        vout_ref[b, 0:WINDOW - 1, :] = vbuf_ref[b, 1:WINDOW, :]
        kout_ref[b, WINDOW - 1:WINDOW, :] = knew[pl.ds(b, 1), :]
        vout_ref[b, WINDOW - 1:WINDOW, :] = vnew[pl.ds(b, 1), :]
        qb = qz[pl.ds(b * QROWS, QROWS), :].astype(BF)
        s = _dot_nt(qb, kout_ref[b].astype(BF))
        m = jnp.maximum(jnp.max(s, axis=-1, keepdims=True), sink)
        p = jnp.exp(s - m)
        den = jnp.sum(p, axis=-1, keepdims=True) + jnp.exp(sink - m)
        p = (p * (1.0 / den)).astype(BF)
        o_ref[pl.ds(b * QROWS, QROWS), :] = _dot(p, vout_ref[b].astype(BF))
        return carry

    lax.fori_loop(0, nb, body, 0)


def _sample_attn(x, kbuf, vbuf, rc, rs1, rs2, sinkc, g1, wqkv):
    nb = x.shape[0]
    tb = SAMPLE_TB
    return pl.pallas_call(
        _sample_attn_kernel,
        grid=(nb // tb,),
        in_specs=[
            pl.BlockSpec((tb, D_MODEL), lambda i: (i, 0)),
            pl.BlockSpec((tb, WINDOW, KV_WIDTH), lambda i: (i, 0, 0)),
            pl.BlockSpec((tb, WINDOW, KV_WIDTH), lambda i: (i, 0, 0)),
            _const_spec(rc.shape), _const_spec(rs1.shape), _const_spec(rs2.shape),
            _const_spec(sinkc.shape), _const_spec(g1.shape), _const_spec(wqkv.shape),
        ],
        out_specs=[
            pl.BlockSpec((tb * QROWS, LANES), lambda i: (i, 0)),
            pl.BlockSpec((tb, WINDOW, KV_WIDTH), lambda i: (i, 0, 0)),
            pl.BlockSpec((tb, WINDOW, KV_WIDTH), lambda i: (i, 0, 0)),
        ],
        out_shape=[
            jax.ShapeDtypeStruct((nb * QROWS, LANES), F32),
            jax.ShapeDtypeStruct((nb, WINDOW, KV_WIDTH), F32),
            jax.ShapeDtypeStruct((nb, WINDOW, KV_WIDTH), F32),
        ],
        scratch_shapes=[pltpu.VMEM((tb * QROWS, LANES), F32),
                        pltpu.VMEM((tb, KV_WIDTH), F32), pltpu.VMEM((tb, KV_WIDTH), F32)],
        compiler_params=pltpu.CompilerParams(
            dimension_semantics=("arbitrary",), vmem_limit_bytes=VMEM_LIMIT),
        name="sample_attn",
    )(x, kbuf, vbuf, rc, rs1, rs2, sinkc, g1, wqkv)


def _sample_tail_kernel(x_ref, o3_ref, h0re_ref, h0im_ref, g1_ref, wu_ref, wg_ref, bg_ref, wbad_ref,
                        wbs_ref, wo_ref, lb_ref, bblk_ref, ctblk_ref, d_ref, wglu_ref, bglu_ref,
                        x1_ref, hre_ref, him_ref):
    nb = x_ref.shape[0]
    x = x_ref[...]
    hn = _rms(x, g1_ref[...]).astype(BF)

    lane = lax.broadcasted_iota(jnp.int32, (nb, LANES), 1)
    lo = lane < HEAD_DIM
    a = jnp.zeros((nb, D_MODEL), F32)
    zero = jnp.zeros((nb, LANES), F32)
    for h in range(N_Q_HEADS):
        oh = o3_ref[pl.ds(h, nb, stride=QROWS), :]
        oh = jnp.where(lo, oh, zero) if h < N_Q_HEADS // 2 else jnp.where(lo, zero, oh)
        a = a + _dot(oh.astype(BF), wbad_ref[h])

    u = _dot(hn, wu_ref[...])
    ub = u.astype(BF)
    lre, lim = lb_ref[0:1, :], lb_ref[1:2, :]
    y_cols = []
    for o in range(N_OCT):
        sl = slice(o * OCT_STATES, (o + 1) * OCT_STATES)
        bu = _dot(ub[:, o * LANES:(o + 1) * LANES], bblk_ref[o])
        h0r, h0i = h0re_ref[:, sl], h0im_ref[:, sl]
        hr = bu[:, :OCT_STATES] + (lre[:, sl] * h0r - lim[:, sl] * h0i)
        hi = bu[:, OCT_STATES:] + (lre[:, sl] * h0i + lim[:, sl] * h0r)
        hre_ref[:, sl] = hr
        him_ref[:, sl] = hi
        y_cols.append(_dot_nt(jnp.concatenate([hr, hi], axis=1).astype(BF), ctblk_ref[o]))
    y = jnp.concatenate(y_cols, axis=1) + d_ref[...] * u
    z = _gelu_tanh(y)
    gate = _dot(z.astype(BF), wglu_ref[...]) + bglu_ref[...]
    ssm = (z * _sigmoid(gate)).astype(BF)

    x1_ref[...] = _merge_out(x, hn, a, ssm, wg_ref, bg_ref, wbs_ref, wo_ref)


def _sample_tail(x, o3, h0re, h0im, g1, wu, wg, bg, wbad, wbs, wo, lb, bblk, ctblk, d, wglu, bglu):
    nb = x.shape[0]
    args = (x, o3, h0re, h0im, g1, wu, wg, bg, wbad, wbs, wo, lb, bblk, ctblk, d, wglu, bglu)
    out_shapes = ((nb, D_MODEL), (nb, N_STATES), (nb, N_STATES))
    return pl.pallas_call(
        _sample_tail_kernel,
        grid=(1,),
        in_specs=[_const_spec(a.shape) for a in args],
        out_specs=[pl.BlockSpec(s, lambda i: (0, 0)) for s in out_shapes],
        out_shape=[jax.ShapeDtypeStruct(s, F32) for s in out_shapes],
        compiler_params=pltpu.CompilerParams(
            dimension_semantics=("arbitrary",), vmem_limit_bytes=VMEM_LIMIT),
        name="sample_tail",
    )(*args)


def _ffn_kernel(x_ref, g2_ref, wgate_ref, wup_ref, wdown_ref, gf_ref, y_ref):
    x = x_ref[...]
    h = _rms(x, g2_ref[...]).astype(BF)
    gate = _dot(h, wgate_ref[...])
    up = _dot(h, wup_ref[...])
    act = (gate * _sigmoid(gate) * up).astype(BF)
    x2 = x + _dot(act, wdown_ref[...])
    y_ref[...] = _rms(x2, gf_ref[...])


def _ffn(x, g2, wgate, wup, wdown, gf, tb):
    n = x.shape[0]
    return pl.pallas_call(
        _ffn_kernel,
        grid=(n // tb,),
        in_specs=[
            pl.BlockSpec((tb, D_MODEL), lambda i: (i, 0)),
            _const_spec((1, D_MODEL)),
            _const_spec(wgate.shape),
            _const_spec(wup.shape),
            _const_spec(wdown.shape),
            _const_spec((1, D_MODEL)),
        ],
        out_specs=pl.BlockSpec((tb, D_MODEL), lambda i: (i, 0)),
        out_shape=jax.ShapeDtypeStruct((n, D_MODEL), F32),
        compiler_params=pltpu.CompilerParams(
            dimension_semantics=("arbitrary",), vmem_limit_bytes=VMEM_LIMIT),
        name="ffn",
    )(x, g2, wgate, wup, wdown, gf)


def _rope_tables(pos):
    inv_freq = ROPE_THETA ** (-(jnp.arange(ROPE_HALF, dtype=F32) * 2.0 / ROPE_DIM))
    ang = pos.astype(F32)[:, None] * inv_freq[None, :]
    cos, sin = jnp.cos(ang), jnp.sin(ang)
    n = pos.shape[0]
    pad = jnp.zeros((n, HEAD_DIM - ROPE_DIM), F32)
    zero = jnp.zeros_like(sin)
    rc = jnp.concatenate([cos, cos, pad + 1.0], axis=1)
    rs1 = jnp.concatenate([zero, sin, pad], axis=1)
    rs2 = jnp.concatenate([-sin, zero, pad], axis=1)
    rep = LANES // HEAD_DIM
    return jnp.tile(rc, (1, rep)), jnp.tile(rs1, (1, rep)), jnp.tile(rs2, (1, rep))


def _cmul(ar, ai, br, bi):
    return ar * br - ai * bi, ar * bi + ai * br


def _ssm_tables(lam_re, lam_im, log_dt, b_re, b_im, c_re, c_im):
    dt = jnp.exp(log_dt)[:, None]
    mag = jnp.exp(lam_re * dt)
    lb_re = mag * jnp.cos(lam_im * dt)
    lb_im = mag * jnp.sin(lam_im * dt)
    den = lam_re * lam_re + lam_im * lam_im
    nr = lb_re - 1.0
    k_re = ((nr * lam_re + lb_im * lam_im) / den)[..., None]
    k_im = ((lb_im * lam_re - nr * lam_im) / den)[..., None]
    bb_re = k_re * b_re - k_im * b_im
    bb_im = k_re * b_im + k_im * b_re

    a_re, a_im = lb_re, lb_im
    for _ in range(int(math.log2(CHUNK))):
        a_re, a_im = _cmul(a_re, a_im, a_re, a_im)
    s_re, s_im = a_re, a_im
    for _ in range(int(math.log2(SEG))):
        s_re, s_im = _cmul(s_re, s_im, s_re, s_im)

    eye = jnp.eye(OCT, dtype=F32).reshape(1, OCT, 1, OCT, 1)

    def block_diag(a):
        r, c = a.shape[1:]
        return (a.reshape(N_OCT, OCT, r, 1, c) * eye).reshape(N_OCT, OCT * r, OCT * c)

    bblk = jnp.concatenate([block_diag(jnp.swapaxes(bb_re, 1, 2)),
                            block_diag(jnp.swapaxes(bb_im, 1, 2))], axis=2)
    ctblk = jnp.concatenate([block_diag(c_re), block_diag(-c_im)], axis=2)

    oct_cols = lambda a: a.reshape(N_OCT, 1, OCT_STATES)
    lcol = jnp.concatenate([oct_cols(lb_re), oct_cols(lb_im),
                            jnp.zeros((N_OCT, 6, OCT_STATES), F32)], axis=1)

    flat = lambda a: a.reshape(1, N_STATES)
    col = lambda a: a.reshape(N_COL, 1, LANES)
    a_tab = jnp.concatenate([col(a_re), col(a_im), col(s_re), col(s_im),
                             jnp.zeros((N_COL, 4, LANES), F32)], axis=1)
    lb = jnp.concatenate([flat(lb_re), flat(lb_im)], axis=0)
    return bblk, ctblk, lcol, a_tab, lb


def kernel(x_prompt, x_sample, state_k_win, state_v_win, state_ssm_re, state_ssm_im, norm1_g, w_in, b_gate, attn_sinks, ssm_lam_re, ssm_lam_im, ssm_log_dt, ssm_b_re, ssm_b_im, ssm_c_re, ssm_c_im, ssm_d, w_glu, b_glu, w_branch_attn, w_branch_ssm, w_out, norm2_g, w_ffn_gate, w_ffn_up, w_ffn_down, norm_f_g):
    depth = w_in.shape[0]
    assert depth == 1
    b, t, _ = x_prompt.shape
    nb, s_len, _ = x_sample.shape
    assert s_len == 1 and state_k_win.shape[2] == WINDOW
    l = 0
    o1 = ATTN_WIDTH + 2 * KV_WIDTH
    o2 = o1 + SSM_WIDTH
    w_in_b = w_in[l].astype(BF)
    wqkv, wu, wg = w_in_b[:, :o1], w_in_b[:, o1:o2], w_in_b[:, o2:]
    g1 = norm1_g[l].reshape(1, D_MODEL)
    g2 = norm2_g[l].reshape(1, D_MODEL)
    gf = norm_f_g.reshape(1, D_MODEL)
    bg = b_gate[l].reshape(1, GATE_WIDTH)
    d = ssm_d[l].reshape(1, SSM_WIDTH)
    wglu = w_glu[l].astype(BF)
    bglu = b_glu[l].reshape(1, SSM_WIDTH)
    wba = w_branch_attn[l].astype(BF)
    wbs = w_branch_ssm[l].astype(BF)
    wo = w_out[l].astype(BF)
    wgate = w_ffn_gate[l].astype(BF)
    wup = w_ffn_up[l].astype(BF)
    wdown = w_ffn_down[l].astype(BF)
    sinks = attn_sinks[l]

    bblk, ctblk, lcol, a_tab, lb = _ssm_tables(
        ssm_lam_re[l], ssm_lam_im[l], ssm_log_dt[l], ssm_b_re[l], ssm_b_im[l], ssm_c_re[l], ssm_c_im[l])

    rc, rs1, rs2 = _rope_tables(jnp.arange(t, dtype=jnp.int32))
    ssm_p, hre_p, him_p = _s5_prompt(x_prompt, g1, wu, bblk, ctblk, lcol, a_tab, d, wglu, bglu)
    x1_p, kwin_p, vwin_p = _mix_prompt(x_prompt, ssm_p, rc, rs1, rs2, sinks, g1, wqkv, wg, bg, wba, wbs, wo)
    y_p = _ffn(x1_p, g2, wgate, wup, wdown, gf, FFN_TB).reshape(b, t, D_MODEL)

    rcs, rs1s, rs2s = _rope_tables(PAST_LEN + jnp.arange(1, dtype=jnp.int32))
    sinkc = jnp.concatenate([sinks, jnp.zeros((QROWS - N_Q_HEADS,), F32)]).reshape(QROWS, 1)
    wbad = jnp.concatenate([wba.reshape(N_Q_HEADS, HEAD_DIM, D_MODEL)] * 2, axis=1)
    xs = x_sample.reshape(nb, D_MODEL)
    o3, kwin_s, vwin_s = _sample_attn(
        xs, state_k_win[l].reshape(nb, WINDOW, KV_WIDTH), state_v_win[l].reshape(nb, WINDOW, KV_WIDTH),
        rcs, rs1s, rs2s, sinkc, g1, wqkv)
    x1_s, hre_s, him_s = _sample_tail(
        xs, o3, state_ssm_re[l].reshape(nb, N_STATES), state_ssm_im[l].reshape(nb, N_STATES),
        g1, wu, wg, bg, wbad, wbs, wo, lb, bblk.astype(BF), ctblk.astype(BF), d, wglu, bglu)
    y_s = _ffn(x1_s, g2, wgate, wup, wdown, gf, nb).reshape(nb, 1, D_MODEL)

    kv_shape_p = (1, b, WINDOW, N_KV_HEADS, HEAD_DIM)
    st_shape_p = (1, b, N_SSM_GROUPS, SSM_STATE)
    kv_shape_s = (1, nb, WINDOW, N_KV_HEADS, HEAD_DIM)
    st_shape_s = (1, nb, N_SSM_GROUPS, SSM_STATE)
    return (y_p, y_s,
            kwin_p.reshape(kv_shape_p), vwin_p.reshape(kv_shape_p),
            hre_p.reshape(st_shape_p), him_p.reshape(st_shape_p),
            kwin_s.reshape(kv_shape_s), vwin_s.reshape(kv_shape_s),
            hre_s.reshape(st_shape_s), him_s.reshape(st_shape_s))
```

```python
import math

import jax
import jax.numpy as jnp
from jax import lax
from jax.experimental import pallas as pl
from jax.experimental.pallas import tpu as pltpu

D_MODEL = 1024
N_Q_HEADS = 8
N_KV_HEADS = 2
HEAD_DIM = 64
ATTN_WIDTH = N_Q_HEADS * HEAD_DIM
KV_WIDTH = N_KV_HEADS * HEAD_DIM
WINDOW = 128
ROPE_DIM = HEAD_DIM // 4
ROPE_HALF = ROPE_DIM // 2
ROPE_THETA = 500000.0
SSM_WIDTH = D_MODEL // 2
SSM_GROUP = 16
N_SSM_GROUPS = SSM_WIDTH // SSM_GROUP
SSM_STATE = 64
N_STATES = N_SSM_GROUPS * SSM_STATE
GATE_WIDTH = 2 * D_MODEL
D_FF = -(-8 * D_MODEL // (3 * 256)) * 256
NORM_EPS = 1e-5
PAST_LEN = 8192

LANES = 128
CHUNK = 8
OCT = LANES // SSM_GROUP
N_OCT = N_SSM_GROUPS // OCT
OCT_STATES = OCT * SSM_STATE
OCT_COL = OCT_STATES // LANES
N_COL = N_STATES // LANES
S5_ROWS = 128
SEG = 16
M_PAIR = 2
MIX_TB = 512
FFN_TB = 512
NEG_BIG = -1e30
VMEM_LIMIT = 56 * 1024 * 1024

BF = jnp.bfloat16
F32 = jnp.float32


def _dot(a, b):
    return jnp.dot(a, b, preferred_element_type=F32)


def _dot_nt(a, b):
    return lax.dot_general(a, b, (((1,), (1,)), ((), ())), preferred_element_type=F32)


def _dot_nt_split(a, b):
    a_hi, b_hi = a.astype(BF), b.astype(BF)
    a_lo = (a - a_hi.astype(F32)).astype(BF)
    b_lo = (b - b_hi.astype(F32)).astype(BF)
    return _dot_nt(a_hi, b_hi) + (_dot_nt(a_hi, b_lo) + _dot_nt(a_lo, b_hi))


def _rms(x, g):
    return x * lax.rsqrt(jnp.mean(x * x, axis=-1, keepdims=True) + NORM_EPS) * g


def _sigmoid(x):
    return 1.0 / (1.0 + jnp.exp(-x))


def _gelu_tanh(x):
    c = math.sqrt(2.0 / math.pi)
    return 0.5 * x * (1.0 + jnp.tanh(c * (x + 0.044715 * (x * x * x))))


def _rope(a, rc, rs1, rs2):
    return a * rc + pltpu.roll(a, ROPE_HALF, 1) * rs1 + pltpu.roll(a, LANES - ROPE_HALF, 1) * rs2


def _const_spec(shape):
    nd = len(shape)
    return pl.BlockSpec(shape, lambda *_: (0,) * nd, pipeline_mode=pl.Buffered(1))


def _build_chunk_operators(bblk_ref, ctblk_ref, lcol_ref, m_s, e_s, f_s):
    for o in range(N_OCT):
        ct = ctblk_ref[o]
        lr, li = lcol_ref[o, 0:1, :], lcol_ref[o, 1:2, :]
        er, ei = bblk_ref[o, :, :OCT_STATES], bblk_ref[o, :, OCT_STATES:]
        k_blk = []
        for tau in range(CHUNK):
            e_cat = jnp.concatenate([er, ei], axis=1)
            i = CHUNK - 1 - tau
            e_s[o, i * LANES:(i + 1) * LANES, :] = e_cat.astype(BF)
            k_blk.append(_dot_nt_split(e_cat, ct).astype(BF))
            er, ei = er * lr - ei * li, er * li + ei * lr
        zero = jnp.zeros((LANES, LANES), BF)
        for j in range(CHUNK):
            jt, jj = divmod(j, M_PAIR)
            for i in range(M_PAIR * (jt + 1)):
                m_s[jt][o, i * LANES:(i + 1) * LANES, jj * LANES:(jj + 1) * LANES] = (
                    k_blk[j - i] if j >= i else zero)
        tr, ti = ct[:, :OCT_STATES], -ct[:, OCT_STATES:]
        for j in range(CHUNK):
            tr, ti = tr * lr - ti * li, tr * li + ti * lr
            f_s[o, :OCT_STATES, j * LANES:(j + 1) * LANES] = tr.T.astype(BF)
            f_s[o, OCT_STATES:, j * LANES:(j + 1) * LANES] = (-ti).T.astype(BF)


def _s5_prompt_kernel(x_ref, g1_ref, wu_ref, bblk_ref, ctblk_ref, lcol_ref, a_ref, d_ref,
                      wglu_ref, bglu_ref,
                      out_ref, hfin_ref,
                      m0_ref, m1_ref, m2_ref, m3_ref, e_ref, f_ref, us, ys, sre, sim, car):
    m_ref = (m0_ref, m1_ref, m2_ref, m3_ref)
    blk = pl.program_id(1)

    @pl.when((pl.program_id(0) == 0) & (blk == 0))
    def _():
        _build_chunk_operators(bblk_ref, ctblk_ref, lcol_ref, m_ref, e_ref, f_ref)

    @pl.when(blk == 0)
    def _():
        car[...] = jnp.zeros_like(car)

    rows = S5_ROWS
    hn = _rms(x_ref[...], g1_ref[...]).astype(BF)
    u = _dot(hn, wu_ref[...])
    for cc in range(N_OCT):
        us[cc] = u[:, cc * LANES:(cc + 1) * LANES]

    sub = lax.broadcasted_iota(jnp.int32, (8, LANES), 0)

    def scan_column(c):
        tab = a_ref[c]
        are, aim = tab[0:1], tab[1:2]
        bre, bim = tab[2:3], tab[3:4]

        def step(cr, ci, r, keep_entering):
            slab = pl.ds(r, 8, stride=SEG)
            s_r, s_i = sre[c, slab, :], sim[c, slab, :]
            if keep_entering:
                sre[c, slab, :] = cr
                sim[c, slab, :] = ci
            return are * cr - aim * ci + s_r, are * ci + aim * cr + s_i

        cr = jnp.zeros((8, LANES), F32)
        ci = jnp.zeros((8, LANES), F32)
        for r in range(SEG):
            cr, ci = step(cr, ci, r, False)
        cv = car[c]
        pr, pi = cv[0:1], cv[1:2]
        sr = jnp.zeros((8, LANES), F32)
        si = jnp.zeros((8, LANES), F32)
        for s in range(8):
            sr = jnp.where(sub == s, pr, sr)
            si = jnp.where(sub == s, pi, si)
            pr, pi = (bre * pr - bim * pi + cr[s:s + 1], bre * pi + bim * pr + ci[s:s + 1])
        end = jnp.where(sub == 0, pr, jnp.where(sub == 1, pi, 0.0))
        car[c] = end
        hfin_ref[c] = end
        cr, ci = sr, si
        for r in range(SEG):
            cr, ci = step(cr, ci, r, True)

    for o in range(N_OCT):
        uo = jnp.concatenate(
            [us[o, pl.ds(i, rows, stride=CHUNK), :].astype(BF) for i in range(CHUNK)], axis=1)
        s = _dot(uo, e_ref[o])
        cols = range(o * OCT_COL, (o + 1) * OCT_COL)
        for cc, c in enumerate(cols):
            sre[c] = s[:, cc * LANES:(cc + 1) * LANES]
            sim[c] = s[:, OCT_STATES + cc * LANES:OCT_STATES + (cc + 1) * LANES]
        y_in = jnp.concatenate(
            [_dot(uo[:, :(jt + 1) * M_PAIR * LANES], m_ref[jt][o]) for jt in range(CHUNK // M_PAIR)],
            axis=1)
        for c in cols:
            scan_column(c)
        hp = jnp.concatenate([sre[c] for c in cols] + [sim[c] for c in cols], axis=1).astype(BF)
        yo = y_in + _dot(hp, f_ref[o])
        for j in range(CHUNK):
            ys[o, pl.ds(j, rows, stride=CHUNK), :] = yo[:, j * LANES:(j + 1) * LANES]
    y = jnp.concatenate([ys[o] for o in range(N_OCT)], axis=1)
    y = y + d_ref[...] * jnp.concatenate([us[cc] for cc in range(N_OCT)], axis=1)
    z = _gelu_tanh(y)
    gate = _dot(z.astype(BF), wglu_ref[...]) + bglu_ref[...]
    out_ref[...] = (z * _sigmoid(gate)).astype(out_ref.dtype)


def _s5_prompt(x, g1, wu, bblk, ctblk, lcol, a_tab, d, wglu, bglu):
    b, t, _ = x.shape
    tb = S5_ROWS * CHUNK
    nblk = t // tb
    x2 = x.reshape(b * t, D_MODEL)
    row_map = lambda i, j: (i * nblk + j, 0)
    op_shape = (N_OCT, CHUNK * LANES, CHUNK * LANES)
    out, hfin = pl.pallas_call(
        _s5_prompt_kernel,
        grid=(b, nblk),
        in_specs=[
            pl.BlockSpec((tb, D_MODEL), row_map),
            _const_spec((1, D_MODEL)),
            _const_spec((D_MODEL, SSM_WIDTH)),
            _const_spec(bblk.shape),
            _const_spec(ctblk.shape),
            _const_spec(lcol.shape),
            _const_spec(a_tab.shape),
            _const_spec((1, SSM_WIDTH)),
            _const_spec((SSM_WIDTH, SSM_WIDTH)),
            _const_spec((1, SSM_WIDTH)),
        ],
        out_specs=[
            pl.BlockSpec((tb, SSM_WIDTH), row_map),
            pl.BlockSpec((None, N_COL, 8, LANES), lambda i, j: (i, 0, 0, 0)),
        ],
        out_shape=[
            jax.ShapeDtypeStruct((b * t, SSM_WIDTH), BF),
            jax.ShapeDtypeStruct((b, N_COL, 8, LANES), F32),
        ],
        scratch_shapes=[
            *[pltpu.VMEM((N_OCT, (jt + 1) * M_PAIR * LANES, M_PAIR * LANES), BF)
              for jt in range(CHUNK // M_PAIR)],
            pltpu.VMEM(op_shape, BF), pltpu.VMEM(op_shape, BF),
            pltpu.VMEM((N_OCT, tb, LANES), F32), pltpu.VMEM((N_OCT, tb, LANES), F32),
            pltpu.VMEM((N_COL, S5_ROWS, LANES), F32), pltpu.VMEM((N_COL, S5_ROWS, LANES), F32),
            pltpu.VMEM((N_COL, 8, LANES), F32),
        ],
        compiler_params=pltpu.CompilerParams(
            dimension_semantics=("arbitrary", "arbitrary"), vmem_limit_bytes=VMEM_LIMIT),
        name="s5_prompt",
    )(x2, g1, wu, bblk, ctblk, lcol, a_tab, d, wglu, bglu)
    return out, hfin[:, :, 0, :], hfin[:, :, 1, :]


def _gate_cols(hn, wg_ref, bg_ref, cols):
    return _sigmoid(_dot(hn, wg_ref[:, cols]) + bg_ref[:, cols])


def _merge_out(x, gates, attn_proj, ssm_proj, wo_ref):
    merged = gates[:, :D_MODEL] * attn_proj + gates[:, D_MODEL:] * ssm_proj
    return x + _dot(merged.astype(BF), wo_ref[...])


def _half_split(a, ar, lo):
    z = jnp.zeros_like(a)
    return (jnp.where(lo, a, z).astype(BF), jnp.where(lo, z, ar).astype(BF),
            jnp.where(lo, ar, z).astype(BF), jnp.where(lo, z, a).astype(BF))


def _softmax_sink(s, bias, sink):
    s = s + bias
    m = jnp.maximum(jnp.max(s, axis=-1, keepdims=True), sink)
    p = jnp.exp(s - m)
    den = jnp.sum(p, axis=-1, keepdims=True) + jnp.exp(sink - m)
    return (p * (1.0 / den)).astype(BF)


def _mix_prompt_kernel(sinks_ref, x_ref, ssm_ref, rc_ref, rs1_ref, rs2_ref, g1_ref, wqkv_ref,
                       wg_ref, bg_ref, wba_ref, wbs_ref, wo_ref,
                       x1_ref, kwin_ref, vwin_ref, kprev, vprev):
    t = pl.program_id(1)

    @pl.when(t == 0)
    def _():
        kprev[...] = jnp.zeros_like(kprev)
        vprev[...] = jnp.zeros_like(vprev)

    x = x_ref[...]
    hn = _rms(x, g1_ref[...]).astype(BF)
    qkv = _dot(hn, wqkv_ref[...])
    rc, rs1, rs2 = rc_ref[...], rs1_ref[...], rs2_ref[...]
    scale = HEAD_DIM ** -0.5
    q = [(_rope(qkv[:, c * LANES:(c + 1) * LANES], rc, rs1, rs2) * scale).astype(BF)
         for c in range(ATTN_WIDTH // LANES)]
    k = _rope(qkv[:, ATTN_WIDTH:ATTN_WIDTH + KV_WIDTH], rc, rs1, rs2)
    v = qkv[:, ATTN_WIDTH + KV_WIDTH:ATTN_WIDTH + 2 * KV_WIDTH]

    w = WINDOW
    n_sub = MIX_TB // w
    lane = lax.broadcasted_iota(jnp.int32, (2 * w, LANES), 1)
    lo = lane < HEAD_DIM
    qi = lax.broadcasted_iota(jnp.int32, (w, 2 * w), 0)
    kj = lax.broadcasted_iota(jnp.int32, (w, 2 * w), 1)
    band = (kj > qi) & (kj <= qi + w)
    first = band & ((kj >= w) | (t > 0))
    bias_band = jnp.where(band, 0.0, NEG_BIG).astype(F32)
    bias_first = jnp.where(first, 0.0, NEG_BIG).astype(F32)
    row = lax.broadcasted_iota(jnp.int32, (2 * w, 1), 0)

    def sink_col(ha, hb):
        return jnp.where(row < w, sinks_ref[ha], sinks_ref[hb])

    ssm_proj = _dot(ssm_ref[...], wbs_ref[...])
    gate_w = GATE_WIDTH // n_sub
    gate_cols = []
    attn_rows = []
    for sb in range(n_sub):
        cur = slice(sb * w, (sb + 1) * w)
        if sb == 0:
            kcat = jnp.concatenate([kprev[...], k[cur]], axis=0)
            vcat = jnp.concatenate([vprev[...], v[cur]], axis=0)
            bias = bias_first
        else:
            kcat = k[(sb - 1) * w:(sb + 1) * w]
            vcat = v[(sb - 1) * w:(sb + 1) * w]
            bias = bias_band
        bias2 = jnp.concatenate([bias, bias], axis=0)
        ka, kb, kc, kd = _half_split(kcat, pltpu.roll(kcat, HEAD_DIM, 1), lo)
        va, vb, vc, vd = _half_split(vcat, pltpu.roll(vcat, HEAD_DIM, 1), lo)
        outs = []
        for grp, (k_e, k_o, v_e, v_o) in enumerate(((ka, kb, va, vb), (kc, kd, vc, vd))):
            qq = jnp.concatenate([q[2 * grp][cur], q[2 * grp + 1][cur]], axis=0)
            p_e = _softmax_sink(_dot_nt(qq, k_e), bias2, sink_col(4 * grp, 4 * grp + 2))
            p_o = _softmax_sink(_dot_nt(qq, k_o), bias2, sink_col(4 * grp + 1, 4 * grp + 3))
            o2 = _dot(p_e, v_e) + _dot(p_o, v_o)
            outs += [o2[:w], o2[w:]]
        attn_rows.append(jnp.concatenate(outs, axis=1))
        gate_cols.append(_gate_cols(hn, wg_ref, bg_ref, slice(sb * gate_w, (sb + 1) * gate_w)))
    attn = jnp.concatenate(attn_rows, axis=0).astype(BF)
    gates = jnp.concatenate(gate_cols, axis=1)

    kprev[...] = k[MIX_TB - w:]
    vprev[...] = v[MIX_TB - w:]
    kwin_ref[...] = k[MIX_TB - w:]
    vwin_ref[...] = v[MIX_TB - w:]
    x1_ref[...] = _merge_out(x, gates, _dot(attn, wba_ref[...]), ssm_proj, wo_ref)


def _mix_prompt(x, ssm, rc, rs1, rs2, sinks, g1, wqkv, wg, bg, wba, wbs, wo):
    b, t, _ = x.shape
    nblk = t // MIX_TB
    x2 = x.reshape(b * t, D_MODEL)
    row_map = lambda i, j: (i * nblk + j, 0)
    x1, kwin, vwin = pl.pallas_call(
        _mix_prompt_kernel,
        grid=(b, nblk),
        in_specs=[
            pl.BlockSpec(memory_space=pltpu.SMEM),
            pl.BlockSpec((MIX_TB, D_MODEL), row_map),
            pl.BlockSpec((MIX_TB, SSM_WIDTH), row_map),
            pl.BlockSpec((MIX_TB, LANES), lambda i, j: (j, 0)),
            pl.BlockSpec((MIX_TB, LANES), lambda i, j: (j, 0)),
            pl.BlockSpec((MIX_TB, LANES), lambda i, j: (j, 0)),
            _const_spec((1, D_MODEL)),
            _const_spec(wqkv.shape),
            _const_spec(wg.shape),
            _const_spec(bg.shape),
            _const_spec(wba.shape),
            _const_spec(wbs.shape),
            _const_spec(wo.shape),
        ],
        out_specs=[
            pl.BlockSpec((MIX_TB, D_MODEL), row_map),
            pl.BlockSpec((None, WINDOW, KV_WIDTH), lambda i, j: (i, 0, 0)),
            pl.BlockSpec((None, WINDOW, KV_WIDTH), lambda i, j: (i, 0, 0)),
        ],
        out_shape=[
            jax.ShapeDtypeStruct((b * t, D_MODEL), F32),
            jax.ShapeDtypeStruct((b, WINDOW, KV_WIDTH), F32),
            jax.ShapeDtypeStruct((b, WINDOW, KV_WIDTH), F32),
        ],
        scratch_shapes=[pltpu.VMEM((WINDOW, KV_WIDTH), F32), pltpu.VMEM((WINDOW, KV_WIDTH), F32)],
        compiler_params=pltpu.CompilerParams(
            dimension_semantics=("arbitrary", "arbitrary"), vmem_limit_bytes=VMEM_LIMIT),
        name="mix_prompt",
    )(sinks, x2, ssm, rc, rs1, rs2, g1, wqkv, wg, bg, wba, wbs, wo)
    return x1, kwin, vwin


QROWS = 16
SAMPLE_TB = 32


def _sample_attn_kernel(x_ref, kbuf_ref, vbuf_ref, rc_ref, rs1_ref, rs2_ref, sinkc_ref, g1_ref, wqkv_ref,
                        o_ref, kout_ref, vout_ref, qz, knew, vnew):
    nb = x_ref.shape[0]
    hn = _rms(x_ref[...], g1_ref[...]).astype(BF)
    qkv = _dot(hn, wqkv_ref[...])
    rc, rs1, rs2 = rc_ref[...], rs1_ref[...], rs2_ref[...]
    scale = HEAD_DIM ** -0.5
    knew[...] = _rope(qkv[:, ATTN_WIDTH:ATTN_WIDTH + KV_WIDTH], rc, rs1, rs2)
    vnew[...] = qkv[:, ATTN_WIDTH + KV_WIDTH:ATTN_WIDTH + 2 * KV_WIDTH]

    lane = lax.broadcasted_iota(jnp.int32, (nb, LANES), 1)
    lo = lane < HEAD_DIM
    qz[...] = jnp.zeros_like(qz)
    for c in range(ATTN_WIDTH // LANES):
        qc = _rope(qkv[:, c * LANES:(c + 1) * LANES], rc, rs1, rs2) * scale
        qr = pltpu.roll(qc, HEAD_DIM, 1)
        zero = jnp.zeros_like(qc)
        if c < 2:
            even, odd = jnp.where(lo, qc, zero), jnp.where(lo, qr, zero)
        else:
            even, odd = jnp.where(lo, zero, qr), jnp.where(lo, zero, qc)
        qz[pl.ds(2 * c, nb, stride=QROWS), :] = even
        qz[pl.ds(2 * c + 1, nb, stride=QROWS), :] = odd

    def shift(b, carry):
        kout_ref[b, 0:WINDOW - 1, :] = kbuf_ref[b, 1:WINDOW, :]
        vout_ref[b, 0:WINDOW - 1, :] = vbuf_ref[b, 1:WINDOW, :]
        kout_ref[b, WINDOW - 1:WINDOW, :] = knew[pl.ds(b, 1), :]
        vout_ref[b, WINDOW - 1:WINDOW, :] = vnew[pl.ds(b, 1), :]
        return carry

    lax.fori_loop(0, nb, shift, 0)

    sink = sinkc_ref[...]
    q3 = qz[...].reshape(nb, QROWS, LANES).astype(BF)
    s = jnp.einsum('bhd,bkd->bhk', q3, kout_ref[...].astype(BF), preferred_element_type=F32)
    m = jnp.maximum(jnp.max(s, axis=-1, keepdims=True), sink)
    p = jnp.exp(s - m)
    den = jnp.sum(p, axis=-1, keepdims=True) + jnp.exp(sink - m)
    p = (p * (1.0 / den)).astype(BF)
    o3 = jnp.einsum('bhk,bkd->bhd', p, vout_ref[...].astype(BF), preferred_element_type=F32)
    o_ref[...] = o3.reshape(nb * QROWS, LANES)


def _sample_attn(x, kbuf, vbuf, rc, rs1, rs2, sinkc, g1, wqkv):
    nb = x.shape[0]
    tb = SAMPLE_TB
    return pl.pallas_call(
        _sample_attn_kernel,
        grid=(nb // tb,),
        in_specs=[
            pl.BlockSpec((tb, D_MODEL), lambda i: (i, 0)),
            pl.BlockSpec((tb, WINDOW, KV_WIDTH), lambda i: (i, 0, 0)),
            pl.BlockSpec((tb, WINDOW, KV_WIDTH), lambda i: (i, 0, 0)),
            _const_spec(rc.shape), _const_spec(rs1.shape), _const_spec(rs2.shape),
            _const_spec(sinkc.shape), _const_spec(g1.shape), _const_spec(wqkv.shape),
        ],
        out_specs=[
            pl.BlockSpec((tb * QROWS, LANES), lambda i: (i, 0)),
            pl.BlockSpec((tb, WINDOW, KV_WIDTH), lambda i: (i, 0, 0)),
            pl.BlockSpec((tb, WINDOW, KV_WIDTH), lambda i: (i, 0, 0)),
        ],
        out_shape=[
            jax.ShapeDtypeStruct((nb * QROWS, LANES), F32),
            jax.ShapeDtypeStruct((nb, WINDOW, KV_WIDTH), F32),
            jax.ShapeDtypeStruct((nb, WINDOW, KV_WIDTH), F32),
        ],
        scratch_shapes=[pltpu.VMEM((tb * QROWS, LANES), F32),
                        pltpu.VMEM((tb, KV_WIDTH), F32), pltpu.VMEM((tb, KV_WIDTH), F32)],
        compiler_params=pltpu.CompilerParams(
            dimension_semantics=("arbitrary",), vmem_limit_bytes=VMEM_LIMIT),
        name="sample_attn",
    )(x, kbuf, vbuf, rc, rs1, rs2, sinkc, g1, wqkv)


def _sample_tail_kernel(x_ref, o3_ref, h0re_ref, h0im_ref, g1_ref, wu_ref, wg_ref, bg_ref, wbad_ref,
                        wbs_ref, wo_ref, lb_ref, bblk_ref, ctblk_ref, d_ref, wglu_ref, bglu_ref,
                        x1_ref, hre_ref, him_ref):
    nb = x_ref.shape[0]
    x = x_ref[...]
    hn = _rms(x, g1_ref[...]).astype(BF)

    lane = lax.broadcasted_iota(jnp.int32, (nb, LANES), 1)
    lo = lane < HEAD_DIM
    a = jnp.zeros((nb, D_MODEL), F32)
    zero = jnp.zeros((nb, LANES), F32)
    for h in range(N_Q_HEADS):
        oh = o3_ref[pl.ds(h, nb, stride=QROWS), :]
        oh = jnp.where(lo, oh, zero) if h < N_Q_HEADS // 2 else jnp.where(lo, zero, oh)
        a = a + _dot(oh.astype(BF), wbad_ref[h])

    u = _dot(hn, wu_ref[...])
    ub = u.astype(BF)
    lre, lim = lb_ref[0:1, :], lb_ref[1:2, :]
    y_cols = []
    for o in range(N_OCT):
        sl = slice(o * OCT_STATES, (o + 1) * OCT_STATES)
        bu = _dot(ub[:, o * LANES:(o + 1) * LANES], bblk_ref[o])
        h0r, h0i = h0re_ref[:, sl], h0im_ref[:, sl]
        hr = bu[:, :OCT_STATES] + (lre[:, sl] * h0r - lim[:, sl] * h0i)
        hi = bu[:, OCT_STATES:] + (lre[:, sl] * h0i + lim[:, sl] * h0r)
        hre_ref[:, sl] = hr
        him_ref[:, sl] = hi
        y_cols.append(_dot_nt(jnp.concatenate([hr, hi], axis=1).astype(BF), ctblk_ref[o]))
    y = jnp.concatenate(y_cols, axis=1) + d_ref[...] * u
    z = _gelu_tanh(y)
    gate = _dot(z.astype(BF), wglu_ref[...]) + bglu_ref[...]
    ssm = (z * _sigmoid(gate)).astype(BF)

    gates = _gate_cols(hn, wg_ref, bg_ref, slice(0, GATE_WIDTH))
    x1_ref[...] = _merge_out(x, gates, a, _dot(ssm, wbs_ref[...]), wo_ref)


def _sample_tail(x, o3, h0re, h0im, g1, wu, wg, bg, wbad, wbs, wo, lb, bblk, ctblk, d, wglu, bglu):
    nb = x.shape[0]
    args = (x, o3, h0re, h0im, g1, wu, wg, bg, wbad, wbs, wo, lb, bblk, ctblk, d, wglu, bglu)
    out_shapes = ((nb, D_MODEL), (nb, N_STATES), (nb, N_STATES))
    return pl.pallas_call(
        _sample_tail_kernel,
        grid=(1,),
        in_specs=[_const_spec(a.shape) for a in args],
        out_specs=[pl.BlockSpec(s, lambda i: (0, 0)) for s in out_shapes],
        out_shape=[jax.ShapeDtypeStruct(s, F32) for s in out_shapes],
        compiler_params=pltpu.CompilerParams(
            dimension_semantics=("arbitrary",), vmem_limit_bytes=VMEM_LIMIT),
        name="sample_tail",
    )(*args)


def _ffn_kernel(x_ref, g2_ref, wgate_ref, wup_ref, wdown_ref, gf_ref, y_ref):
    x = x_ref[...]
    h = _rms(x, g2_ref[...]).astype(BF)
    gate = _dot(h, wgate_ref[...])
    up = _dot(h, wup_ref[...])
    act = (gate * _sigmoid(gate) * up).astype(BF)
    x2 = x + _dot(act, wdown_ref[...])
    y_ref[...] = _rms(x2, gf_ref[...])


def _ffn(x, g2, wgate, wup, wdown, gf, tb):
    n = x.shape[0]
    return pl.pallas_call(
        _ffn_kernel,
        grid=(n // tb,),
        in_specs=[
            pl.BlockSpec((tb, D_MODEL), lambda i: (i, 0)),
            _const_spec((1, D_MODEL)),
            _const_spec(wgate.shape),
            _const_spec(wup.shape),
            _const_spec(wdown.shape),
            _const_spec((1, D_MODEL)),
        ],
        out_specs=pl.BlockSpec((tb, D_MODEL), lambda i: (i, 0)),
        out_shape=jax.ShapeDtypeStruct((n, D_MODEL), F32),
        compiler_params=pltpu.CompilerParams(
            dimension_semantics=("arbitrary",), vmem_limit_bytes=VMEM_LIMIT),
        name="ffn",
    )(x, g2, wgate, wup, wdown, gf)


def _rope_tables(pos):
    inv_freq = ROPE_THETA ** (-(jnp.arange(ROPE_HALF, dtype=F32) * 2.0 / ROPE_DIM))
    ang = pos.astype(F32)[:, None] * inv_freq[None, :]
    cos, sin = jnp.cos(ang), jnp.sin(ang)
    n = pos.shape[0]
    pad = jnp.zeros((n, HEAD_DIM - ROPE_DIM), F32)
    zero = jnp.zeros_like(sin)
    rc = jnp.concatenate([cos, cos, pad + 1.0], axis=1)
    rs1 = jnp.concatenate([zero, sin, pad], axis=1)
    rs2 = jnp.concatenate([-sin, zero, pad], axis=1)
    rep = LANES // HEAD_DIM
    return jnp.tile(rc, (1, rep)), jnp.tile(rs1, (1, rep)), jnp.tile(rs2, (1, rep))


def _cmul(ar, ai, br, bi):
    return ar * br - ai * bi, ar * bi + ai * br


def _ssm_tables(lam_re, lam_im, log_dt, b_re, b_im, c_re, c_im):
    dt = jnp.exp(log_dt)[:, None]
    mag = jnp.exp(lam_re * dt)
    lb_re = mag * jnp.cos(lam_im * dt)
    lb_im = mag * jnp.sin(lam_im * dt)
    den = lam_re * lam_re + lam_im * lam_im
    nr = lb_re - 1.0
    k_re = ((nr * lam_re + lb_im * lam_im) / den)[..., None]
    k_im = ((lb_im * lam_re - nr * lam_im) / den)[..., None]
    bb_re = k_re * b_re - k_im * b_im
    bb_im = k_re * b_im + k_im * b_re

    a_re, a_im = lb_re, lb_im
    for _ in range(int(math.log2(CHUNK))):
        a_re, a_im = _cmul(a_re, a_im, a_re, a_im)
    s_re, s_im = a_re, a_im
    for _ in range(int(math.log2(SEG))):
        s_re, s_im = _cmul(s_re, s_im, s_re, s_im)

    eye = jnp.eye(OCT, dtype=F32).reshape(1, OCT, 1, OCT, 1)

    def block_diag(a):
        r, c = a.shape[1:]
        return (a.reshape(N_OCT, OCT, r, 1, c) * eye).reshape(N_OCT, OCT * r, OCT * c)

    bblk = jnp.concatenate([block_diag(jnp.swapaxes(bb_re, 1, 2)),
                            block_diag(jnp.swapaxes(bb_im, 1, 2))], axis=2)
    ctblk = jnp.concatenate([block_diag(c_re), block_diag(-c_im)], axis=2)

    oct_cols = lambda a: a.reshape(N_OCT, 1, OCT_STATES)
    lcol = jnp.concatenate([oct_cols(lb_re), oct_cols(lb_im),
                            jnp.zeros((N_OCT, 6, OCT_STATES), F32)], axis=1)

    flat = lambda a: a.reshape(1, N_STATES)
    col = lambda a: a.reshape(N_COL, 1, LANES)
    a_tab = jnp.concatenate([col(a_re), col(a_im), col(s_re), col(s_im),
                             jnp.zeros((N_COL, 4, LANES), F32)], axis=1)
    lb = jnp.concatenate([flat(lb_re), flat(lb_im)], axis=0)
    return bblk, ctblk, lcol, a_tab, lb


def kernel(x_prompt, x_sample, state_k_win, state_v_win, state_ssm_re, state_ssm_im, norm1_g, w_in, b_gate, attn_sinks, ssm_lam_re, ssm_lam_im, ssm_log_dt, ssm_b_re, ssm_b_im, ssm_c_re, ssm_c_im, ssm_d, w_glu, b_glu, w_branch_attn, w_branch_ssm, w_out, norm2_g, w_ffn_gate, w_ffn_up, w_ffn_down, norm_f_g):
    depth = w_in.shape[0]
    assert depth == 1
    b, t, _ = x_prompt.shape
    nb, s_len, _ = x_sample.shape
    assert s_len == 1 and state_k_win.shape[2] == WINDOW
    l = 0
    o1 = ATTN_WIDTH + 2 * KV_WIDTH
    o2 = o1 + SSM_WIDTH
    w_in_b = w_in[l].astype(BF)
    wqkv, wu, wg = w_in_b[:, :o1], w_in_b[:, o1:o2], w_in_b[:, o2:]
    g1 = norm1_g[l].reshape(1, D_MODEL)
    g2 = norm2_g[l].reshape(1, D_MODEL)
    gf = norm_f_g.reshape(1, D_MODEL)
    bg = b_gate[l].reshape(1, GATE_WIDTH)
    d = ssm_d[l].reshape(1, SSM_WIDTH)
    wglu = w_glu[l].astype(BF)
    bglu = b_glu[l].reshape(1, SSM_WIDTH)
    wba = w_branch_attn[l].astype(BF)
    wbs = w_branch_ssm[l].astype(BF)
    wo = w_out[l].astype(BF)
    wgate = w_ffn_gate[l].astype(BF)
    wup = w_ffn_up[l].astype(BF)
    wdown = w_ffn_down[l].astype(BF)
    sinks = attn_sinks[l]

    bblk, ctblk, lcol, a_tab, lb = _ssm_tables(
        ssm_lam_re[l], ssm_lam_im[l], ssm_log_dt[l], ssm_b_re[l], ssm_b_im[l], ssm_c_re[l], ssm_c_im[l])

    rc, rs1, rs2 = _rope_tables(jnp.arange(t, dtype=jnp.int32))
    ssm_p, hre_p, him_p = _s5_prompt(x_prompt, g1, wu, bblk, ctblk, lcol, a_tab, d, wglu, bglu)
    x1_p, kwin_p, vwin_p = _mix_prompt(x_prompt, ssm_p, rc, rs1, rs2, sinks, g1, wqkv, wg, bg, wba, wbs, wo)
    y_p = _ffn(x1_p, g2, wgate, wup, wdown, gf, FFN_TB).reshape(b, t, D_MODEL)

    rcs, rs1s, rs2s = _rope_tables(PAST_LEN + jnp.arange(1, dtype=jnp.int32))
    sinkc = jnp.concatenate([sinks, jnp.zeros((QROWS - N_Q_HEADS,), F32)]).reshape(QROWS, 1)
    wbad = jnp.concatenate([wba.reshape(N_Q_HEADS, HEAD_DIM, D_MODEL)] * 2, axis=1)
    xs = x_sample.reshape(nb, D_MODEL)
    o3, kwin_s, vwin_s = _sample_attn(
        xs, state_k_win[l].reshape(nb, WINDOW, KV_WIDTH), state_v_win[l].reshape(nb, WINDOW, KV_WIDTH),
        rcs, rs1s, rs2s, sinkc, g1, wqkv)
    x1_s, hre_s, him_s = _sample_tail(
        xs, o3, state_ssm_re[l].reshape(nb, N_STATES), state_ssm_im[l].reshape(nb, N_STATES),
        g1, wu, wg, bg, wbad, wbs, wo, lb, bblk.astype(BF), ctblk.astype(BF), d, wglu, bglu)
    y_s = _ffn(x1_s, g2, wgate, wup, wdown, gf, nb).reshape(nb, 1, D_MODEL)

    kv_shape_p = (1, b, WINDOW, N_KV_HEADS, HEAD_DIM)
    st_shape_p = (1, b, N_SSM_GROUPS, SSM_STATE)
    kv_shape_s = (1, nb, WINDOW, N_KV_HEADS, HEAD_DIM)
    st_shape_s = (1, nb, N_SSM_GROUPS, SSM_STATE)
    return (y_p, y_s,
            kwin_p.reshape(kv_shape_p), vwin_p.reshape(kv_shape_p),
            hre_p.reshape(st_shape_p), him_p.reshape(st_shape_p),
            kwin_s.reshape(kv_shape_s), vwin_s.reshape(kv_shape_s),
            hre_s.reshape(st_shape_s), him_s.reshape(st_shape_s))
```

```python
import math

import jax
import jax.numpy as jnp
from jax import lax
from jax.experimental import pallas as pl
from jax.experimental.pallas import tpu as pltpu

D_MODEL = 1024
N_Q_HEADS = 8
N_KV_HEADS = 2
HEAD_DIM = 64
ATTN_WIDTH = N_Q_HEADS * HEAD_DIM
KV_WIDTH = N_KV_HEADS * HEAD_DIM
WINDOW = 128
ROPE_DIM = HEAD_DIM // 4
ROPE_HALF = ROPE_DIM // 2
ROPE_THETA = 500000.0
SSM_WIDTH = D_MODEL // 2
SSM_GROUP = 16
N_SSM_GROUPS = SSM_WIDTH // SSM_GROUP
SSM_STATE = 64
N_STATES = N_SSM_GROUPS * SSM_STATE
GATE_WIDTH = 2 * D_MODEL
D_FF = -(-8 * D_MODEL // (3 * 256)) * 256
NORM_EPS = 1e-5
PAST_LEN = 8192

LANES = 128
CHUNK = 8
OCT = LANES // SSM_GROUP
N_OCT = N_SSM_GROUPS // OCT
OCT_STATES = OCT * SSM_STATE
OCT_COL = OCT_STATES // LANES
N_COL = N_STATES // LANES
S5_ROWS = 128
SEG = 16
M_PAIR = 2
S5_QUARTERS = 4
MIX_TB = 512
FFN_TB = 512
NEG_BIG = -1e30
VMEM_LIMIT = 56 * 1024 * 1024

BF = jnp.bfloat16
F32 = jnp.float32


def _dot(a, b):
    return jnp.dot(a, b, preferred_element_type=F32)


def _dot_nt(a, b):
    return lax.dot_general(a, b, (((1,), (1,)), ((), ())), preferred_element_type=F32)


def _dot_nt_split(a, b):
    a_hi, b_hi = a.astype(BF), b.astype(BF)
    a_lo = (a - a_hi.astype(F32)).astype(BF)
    b_lo = (b - b_hi.astype(F32)).astype(BF)
    return _dot_nt(a_hi, b_hi) + (_dot_nt(a_hi, b_lo) + _dot_nt(a_lo, b_hi))


def _rms(x, g):
    return x * lax.rsqrt(jnp.mean(x * x, axis=-1, keepdims=True) + NORM_EPS) * g


def _sigmoid(x):
    return 1.0 / (1.0 + jnp.exp(-x))


def _gelu_tanh(x):
    c = math.sqrt(2.0 / math.pi)
    return 0.5 * x * (1.0 + jnp.tanh(c * (x + 0.044715 * (x * x * x))))


def _rope(a, rc, rs1, rs2):
    return a * rc + pltpu.roll(a, ROPE_HALF, 1) * rs1 + pltpu.roll(a, LANES - ROPE_HALF, 1) * rs2


def _const_spec(shape):
    nd = len(shape)
    return pl.BlockSpec(shape, lambda *_: (0,) * nd, pipeline_mode=pl.Buffered(1))


def _build_chunk_operators(bblk_ref, ctblk_ref, lcol_ref, m_s, e_s, f_s):
    for o in range(N_OCT):
        ct = ctblk_ref[o]
        lr, li = lcol_ref[o, 0:1, :], lcol_ref[o, 1:2, :]
        er, ei = bblk_ref[o, :, :OCT_STATES], bblk_ref[o, :, OCT_STATES:]
        k_blk = []
        for tau in range(CHUNK):
            e_cat = jnp.concatenate([er, ei], axis=1)
            i = CHUNK - 1 - tau
            e_s[o, i * LANES:(i + 1) * LANES, :] = e_cat.astype(BF)
            k_blk.append(_dot_nt_split(e_cat, ct).astype(BF))
            er, ei = er * lr - ei * li, er * li + ei * lr
        zero = jnp.zeros((LANES, LANES), BF)
        for j in range(CHUNK):
            jt, jj = divmod(j, M_PAIR)
            for i in range(M_PAIR * (jt + 1)):
                m_s[jt][o, i * LANES:(i + 1) * LANES, jj * LANES:(jj + 1) * LANES] = (
                    k_blk[j - i] if j >= i else zero)
        tr, ti = ct[:, :OCT_STATES], -ct[:, OCT_STATES:]
        for j in range(CHUNK):
            tr, ti = tr * lr - ti * li, tr * li + ti * lr
            f_s[o, :OCT_STATES, j * LANES:(j + 1) * LANES] = tr.T.astype(BF)
            f_s[o, OCT_STATES:, j * LANES:(j + 1) * LANES] = (-ti).T.astype(BF)


def _s5_prompt_kernel(x_ref, g1_ref, wu_ref, bblk_ref, ctblk_ref, lcol_ref, a_ref, d_ref,
                      wglu_ref, bglu_ref,
                      out_ref, hfin_ref,
                      m0_ref, m1_ref, m2_ref, m3_ref, e_ref, f_ref, us, ys, sre, sim, car):
    m_ref = (m0_ref, m1_ref, m2_ref, m3_ref)
    blk = pl.program_id(1)

    @pl.when((pl.program_id(0) == 0) & (blk == 0))
    def _():
        _build_chunk_operators(bblk_ref, ctblk_ref, lcol_ref, m_ref, e_ref, f_ref)

    @pl.when(blk == 0)
    def _():
        car[...] = jnp.zeros_like(car)

    rows = S5_ROWS
    quarters = [slice(qd * rows * CHUNK // S5_QUARTERS, (qd + 1) * rows * CHUNK // S5_QUARTERS)
                for qd in range(S5_QUARTERS)]
    for qs in quarters:
        hn = _rms(x_ref[qs, :], g1_ref[...]).astype(BF)
        u = _dot(hn, wu_ref[...])
        for cc in range(N_OCT):
            us[cc, qs, :] = u[:, cc * LANES:(cc + 1) * LANES]

    sub = lax.broadcasted_iota(jnp.int32, (8, LANES), 0)

    def scan_column(c):
        tab = a_ref[c]
        are, aim = tab[0:1], tab[1:2]
        bre, bim = tab[2:3], tab[3:4]

        def step(cr, ci, r, keep_entering):
            slab = pl.ds(r, 8, stride=SEG)
            s_r, s_i = sre[c, slab, :], sim[c, slab, :]
            if keep_entering:
                sre[c, slab, :] = cr
                sim[c, slab, :] = ci
            return are * cr - aim * ci + s_r, are * ci + aim * cr + s_i

        cr = jnp.zeros((8, LANES), F32)
        ci = jnp.zeros((8, LANES), F32)
        for r in range(SEG):
            cr, ci = step(cr, ci, r, False)
        cv = car[c]
        pr, pi = cv[0:1], cv[1:2]
        sr = jnp.zeros((8, LANES), F32)
        si = jnp.zeros((8, LANES), F32)
        for s in range(8):
            sr = jnp.where(sub == s, pr, sr)
            si = jnp.where(sub == s, pi, si)
            pr, pi = (bre * pr - bim * pi + cr[s:s + 1], bre * pi + bim * pr + ci[s:s + 1])
        end = jnp.where(sub == 0, pr, jnp.where(sub == 1, pi, 0.0))
        car[c] = end
        hfin_ref[c] = end
        cr, ci = sr, si
        for r in range(SEG):
            cr, ci = step(cr, ci, r, True)

    def chunk_rows(o):
        return jnp.concatenate(
            [us[o, pl.ds(i, rows, stride=CHUNK), :].astype(BF) for i in range(CHUNK)], axis=1)

    n_jt = CHUNK // M_PAIR

    def matmul_piece(o, uo, piece, y_in):
        if piece < 2:
            s = _dot(uo, e_ref[o, :, piece * OCT_STATES:(piece + 1) * OCT_STATES])
            dst = sim if piece else sre
            for cc in range(OCT_COL):
                dst[o * OCT_COL + cc] = s[:, cc * LANES:(cc + 1) * LANES]
        else:
            jts = range(n_jt // 2) if piece == 2 else range(n_jt // 2, n_jt)
            y_in += [_dot(uo[:, :(jt + 1) * M_PAIR * LANES], m_ref[jt][o]) for jt in jts]

    uo = chunk_rows(0)
    y_in = []
    for piece in range(OCT_COL):
        matmul_piece(0, uo, piece, y_in)
    gate = bglu_ref[...]
    for o in range(N_OCT):
        uo = chunk_rows(o + 1) if o + 1 < N_OCT else None
        y_next = []
        for cc in range(OCT_COL):
            if uo is not None:
                matmul_piece(o + 1, uo, cc, y_next)
            scan_column(o * OCT_COL + cc)
        cols = range(o * OCT_COL, (o + 1) * OCT_COL)
        hp = jnp.concatenate([sre[c] for c in cols] + [sim[c] for c in cols], axis=1).astype(BF)
        yo = jnp.concatenate(y_in, axis=1) + _dot(hp, f_ref[o])
        for j in range(CHUNK):
            ys[o, pl.ds(j, rows, stride=CHUNK), :] = yo[:, j * LANES:(j + 1) * LANES]
        ys[o] = _gelu_tanh(ys[o] + d_ref[:, o * LANES:(o + 1) * LANES] * us[o])
        if o % 2 == 1:
            z2 = jnp.concatenate([ys[o - 1], ys[o]], axis=1).astype(BF)
            gate = gate + _dot(z2, wglu_ref[(o - 1) * LANES:(o + 1) * LANES, :])
        y_in = y_next

    for qs in quarters:
        z = jnp.concatenate([ys[o, qs, :] for o in range(N_OCT)], axis=1)
        out_ref[qs, :] = (z * _sigmoid(gate[qs, :])).astype(out_ref.dtype)


def _s5_prompt(x, g1, wu, bblk, ctblk, lcol, a_tab, d, wglu, bglu):
    b, t, _ = x.shape
    tb = S5_ROWS * CHUNK
    nblk = t // tb
    x2 = x.reshape(b * t, D_MODEL)
    row_map = lambda i, j: (i * nblk + j, 0)
    op_shape = (N_OCT, CHUNK * LANES, CHUNK * LANES)
    out, hfin = pl.pallas_call(
        _s5_prompt_kernel,
        grid=(b, nblk),
        in_specs=[
            pl.BlockSpec((tb, D_MODEL), row_map),
            _const_spec((1, D_MODEL)),
            _const_spec((D_MODEL, SSM_WIDTH)),
            _const_spec(bblk.shape),
            _const_spec(ctblk.shape),
            _const_spec(lcol.shape),
            _const_spec(a_tab.shape),
            _const_spec((1, SSM_WIDTH)),
            _const_spec((SSM_WIDTH, SSM_WIDTH)),
            _const_spec((1, SSM_WIDTH)),
        ],
        out_specs=[
            pl.BlockSpec((tb, SSM_WIDTH), row_map),
            pl.BlockSpec((None, N_COL, 8, LANES), lambda i, j: (i, 0, 0, 0)),
        ],
        out_shape=[
            jax.ShapeDtypeStruct((b * t, SSM_WIDTH), BF),
            jax.ShapeDtypeStruct((b, N_COL, 8, LANES), F32),
        ],
        scratch_shapes=[
            *[pltpu.VMEM((N_OCT, (jt + 1) * M_PAIR * LANES, M_PAIR * LANES), BF)
              for jt in range(CHUNK // M_PAIR)],
            pltpu.VMEM(op_shape, BF), pltpu.VMEM(op_shape, BF),
            pltpu.VMEM((N_OCT, tb, LANES), F32), pltpu.VMEM((N_OCT, tb, LANES), F32),
            pltpu.VMEM((N_COL, S5_ROWS, LANES), F32), pltpu.VMEM((N_COL, S5_ROWS, LANES), F32),
            pltpu.VMEM((N_COL, 8, LANES), F32),
        ],
        compiler_params=pltpu.CompilerParams(
            dimension_semantics=("arbitrary", "arbitrary"), vmem_limit_bytes=VMEM_LIMIT),
        name="s5_prompt",
    )(x2, g1, wu, bblk, ctblk, lcol, a_tab, d, wglu, bglu)
    return out, hfin[:, :, 0, :], hfin[:, :, 1, :]


def _gate_cols(hn, wg_ref, bg_ref, cols):
    return _sigmoid(_dot(hn, wg_ref[:, cols]) + bg_ref[:, cols])


def _merge_out(x, gates, attn_proj, ssm_proj, wo_ref):
    merged = gates[:, :D_MODEL] * attn_proj + gates[:, D_MODEL:] * ssm_proj
    return x + _dot(merged.astype(BF), wo_ref[...])


def _half_split(a, ar, lo):
    z = jnp.zeros_like(a)
    return (jnp.where(lo, a, z).astype(BF), jnp.where(lo, z, ar).astype(BF),
            jnp.where(lo, ar, z).astype(BF), jnp.where(lo, z, a).astype(BF))


def _softmax_sink_t(st, bias_t, sink):
    st = st + bias_t
    m = jnp.maximum(jnp.max(st, axis=0, keepdims=True), sink)
    p = jnp.exp(st - m)
    den = jnp.sum(p, axis=0, keepdims=True) + jnp.exp(sink - m)
    return (p * (1.0 / den)).astype(BF)


def _mix_prompt_kernel(sinks_ref, x_ref, ssm_ref, rc_ref, rs1_ref, rs2_ref, g1_ref, wqkv_ref,
                       wg_ref, bg_ref, wba_ref, wbs_ref, wo_ref,
                       x1_ref, kwin_ref, vwin_ref, kprev, vprev):
    t = pl.program_id(1)

    @pl.when(t == 0)
    def _():
        kprev[...] = jnp.zeros_like(kprev)
        vprev[...] = jnp.zeros_like(vprev)

    x = x_ref[...]
    hn = _rms(x, g1_ref[...]).astype(BF)
    qkv = _dot(hn, wqkv_ref[...])
    rc, rs1, rs2 = rc_ref[...], rs1_ref[...], rs2_ref[...]
    scale = HEAD_DIM ** -0.5
    q = [(_rope(qkv[:, c * LANES:(c + 1) * LANES], rc, rs1, rs2) * scale).astype(BF)
         for c in range(ATTN_WIDTH // LANES)]
    k = _rope(qkv[:, ATTN_WIDTH:ATTN_WIDTH + KV_WIDTH], rc, rs1, rs2)
    v = qkv[:, ATTN_WIDTH + KV_WIDTH:ATTN_WIDTH + 2 * KV_WIDTH]

    w = WINDOW
    n_sub = MIX_TB // w
    lane = lax.broadcasted_iota(jnp.int32, (2 * w, LANES), 1)
    lo = lane < HEAD_DIM
    kj = lax.broadcasted_iota(jnp.int32, (2 * w, w), 0)
    qi = lax.broadcasted_iota(jnp.int32, (2 * w, w), 1)
    band = (kj > qi) & (kj <= qi + w)
    first = band & ((kj >= w) | (t > 0))
    bias_band = jnp.where(band, 0.0, NEG_BIG).astype(F32)
    bias_first = jnp.where(first, 0.0, NEG_BIG).astype(F32)
    col = lax.broadcasted_iota(jnp.int32, (1, 2 * w), 1)

    def sink_row(ha, hb):
        return jnp.where(col < w, sinks_ref[ha], sinks_ref[hb])

    ssm_proj = _dot(ssm_ref[...], wbs_ref[...])
    gate_w = GATE_WIDTH // (n_sub * N_KV_HEADS)
    gate_cols = []
    attn_rows = []
    for sb in range(n_sub):
        cur = slice(sb * w, (sb + 1) * w)
        if sb == 0:
            kcat = jnp.concatenate([kprev[...], k[cur]], axis=0)
            vcat = jnp.concatenate([vprev[...], v[cur]], axis=0)
            bias = bias_first
        else:
            kcat = k[(sb - 1) * w:(sb + 1) * w]
            vcat = v[(sb - 1) * w:(sb + 1) * w]
            bias = bias_band
        bias2 = jnp.concatenate([bias, bias], axis=1)
        ka, kb, kc, kd = _half_split(kcat, pltpu.roll(kcat, HEAD_DIM, 1), lo)
        vt = vcat.T
        zero = jnp.zeros((HEAD_DIM, 2 * w), F32)
        v_rows = [(jnp.concatenate([vt[g * HEAD_DIM:(g + 1) * HEAD_DIM], zero], axis=0).astype(BF),
                   jnp.concatenate([zero, vt[g * HEAD_DIM:(g + 1) * HEAD_DIM]], axis=0).astype(BF))
                  for g in range(N_KV_HEADS)]
        outs = []
        for grp, (k_e, k_o) in enumerate(((ka, kb), (kc, kd))):
            qq = jnp.concatenate([q[2 * grp][cur], q[2 * grp + 1][cur]], axis=0)
            s_e, s_o = _dot_nt(k_e, qq), _dot_nt(k_o, qq)
            gc = sb * N_KV_HEADS + grp
            gate_cols.append(_gate_cols(hn, wg_ref, bg_ref, slice(gc * gate_w, (gc + 1) * gate_w)))
            p_e = _softmax_sink_t(s_e, bias2, sink_row(4 * grp, 4 * grp + 2))
            p_o = _softmax_sink_t(s_o, bias2, sink_row(4 * grp + 1, 4 * grp + 3))
            v_e, v_o = v_rows[grp]
            o2 = _dot(v_e, p_e) + _dot(v_o, p_o)
            outs += [o2[:, :w].T, o2[:, w:].T]
        attn_rows.append(jnp.concatenate(outs, axis=1))
    attn = jnp.concatenate(attn_rows, axis=0).astype(BF)
    gates = jnp.concatenate(gate_cols, axis=1)

    kprev[...] = k[MIX_TB - w:]
    vprev[...] = v[MIX_TB - w:]
    kwin_ref[...] = k[MIX_TB - w:]
    vwin_ref[...] = v[MIX_TB - w:]
    x1_ref[...] = _merge_out(x, gates, _dot(attn, wba_ref[...]), ssm_proj, wo_ref)


def _mix_prompt(x, ssm, rc, rs1, rs2, sinks, g1, wqkv, wg, bg, wba, wbs, wo):
    b, t, _ = x.shape
    nblk = t // MIX_TB
    x2 = x.reshape(b * t, D_MODEL)
    row_map = lambda i, j: (i * nblk + j, 0)
    x1, kwin, vwin = pl.pallas_call(
        _mix_prompt_kernel,
        grid=(b, nblk),
        in_specs=[
            pl.BlockSpec(memory_space=pltpu.SMEM),
            pl.BlockSpec((MIX_TB, D_MODEL), row_map),
            pl.BlockSpec((MIX_TB, SSM_WIDTH), row_map),
            pl.BlockSpec((MIX_TB, LANES), lambda i, j: (j, 0)),
            pl.BlockSpec((MIX_TB, LANES), lambda i, j: (j, 0)),
            pl.BlockSpec((MIX_TB, LANES), lambda i, j: (j, 0)),
            _const_spec((1, D_MODEL)),
            _const_spec(wqkv.shape),
            _const_spec(wg.shape),
            _const_spec(bg.shape),
            _const_spec(wba.shape),
            _const_spec(wbs.shape),
            _const_spec(wo.shape),
        ],
        out_specs=[
            pl.BlockSpec((MIX_TB, D_MODEL), row_map),
            pl.BlockSpec((None, WINDOW, KV_WIDTH), lambda i, j: (i, 0, 0)),
            pl.BlockSpec((None, WINDOW, KV_WIDTH), lambda i, j: (i, 0, 0)),
        ],
        out_shape=[
            jax.ShapeDtypeStruct((b * t, D_MODEL), F32),
            jax.ShapeDtypeStruct((b, WINDOW, KV_WIDTH), F32),
            jax.ShapeDtypeStruct((b, WINDOW, KV_WIDTH), F32),
        ],
        scratch_shapes=[pltpu.VMEM((WINDOW, KV_WIDTH), F32), pltpu.VMEM((WINDOW, KV_WIDTH), F32)],
        compiler_params=pltpu.CompilerParams(
            dimension_semantics=("arbitrary", "arbitrary"), vmem_limit_bytes=VMEM_LIMIT),
        name="mix_prompt",
    )(sinks, x2, ssm, rc, rs1, rs2, g1, wqkv, wg, bg, wba, wbs, wo)
    return x1, kwin, vwin


QROWS = 16
SAMPLE_TB = 32


def _sample_attn_kernel(x_ref, kbuf_ref, vbuf_ref, rc_ref, rs1_ref, rs2_ref, sinkc_ref, g1_ref, wqkv_ref,
                        o_ref, kout_ref, vout_ref, qz, knew, vnew):
    nb = x_ref.shape[0]
    hn = _rms(x_ref[...], g1_ref[...]).astype(BF)
    qkv = _dot(hn, wqkv_ref[...])
    rc, rs1, rs2 = rc_ref[...], rs1_ref[...], rs2_ref[...]
    scale = HEAD_DIM ** -0.5
    knew[...] = _rope(qkv[:, ATTN_WIDTH:ATTN_WIDTH + KV_WIDTH], rc, rs1, rs2)
    vnew[...] = qkv[:, ATTN_WIDTH + KV_WIDTH:ATTN_WIDTH + 2 * KV_WIDTH]

    lane = lax.broadcasted_iota(jnp.int32, (nb, LANES), 1)
    lo = lane < HEAD_DIM
    qz[...] = jnp.zeros_like(qz)
    for c in range(ATTN_WIDTH // LANES):
        qc = _rope(qkv[:, c * LANES:(c + 1) * LANES], rc, rs1, rs2) * scale
        qr = pltpu.roll(qc, HEAD_DIM, 1)
        zero = jnp.zeros_like(qc)
        if c < 2:
            even, odd = jnp.where(lo, qc, zero), jnp.where(lo, qr, zero)
        else:
            even, odd = jnp.where(lo, zero, qr), jnp.where(lo, zero, qc)
        qz[pl.ds(2 * c, nb, stride=QROWS), :] = even
        qz[pl.ds(2 * c + 1, nb, stride=QROWS), :] = odd

    def shift(b, carry):
        kout_ref[b, 0:WINDOW - 1, :] = kbuf_ref[b, 1:WINDOW, :]
        vout_ref[b, 0:WINDOW - 1, :] = vbuf_ref[b, 1:WINDOW, :]
        kout_ref[b, WINDOW - 1:WINDOW, :] = knew[pl.ds(b, 1), :]
        vout_ref[b, WINDOW - 1:WINDOW, :] = vnew[pl.ds(b, 1), :]
        return carry

    lax.fori_loop(0, nb, shift, 0)

    sink = sinkc_ref[...]
    q3 = qz[...].reshape(nb, QROWS, LANES).astype(BF)
    s = jnp.einsum('bhd,bkd->bhk', q3, kout_ref[...].astype(BF), preferred_element_type=F32)
    m = jnp.maximum(jnp.max(s, axis=-1, keepdims=True), sink)
    p = jnp.exp(s - m)
    den = jnp.sum(p, axis=-1, keepdims=True) + jnp.exp(sink - m)
    p = (p * (1.0 / den)).astype(BF)
    o3 = jnp.einsum('bhk,bkd->bhd', p, vout_ref[...].astype(BF), preferred_element_type=F32)
    o_ref[...] = o3.reshape(nb * QROWS, LANES)


def _sample_attn(x, kbuf, vbuf, rc, rs1, rs2, sinkc, g1, wqkv):
    nb = x.shape[0]
    tb = SAMPLE_TB
    return pl.pallas_call(
        _sample_attn_kernel,
        grid=(nb // tb,),
        in_specs=[
            pl.BlockSpec((tb, D_MODEL), lambda i: (i, 0)),
            pl.BlockSpec((tb, WINDOW, KV_WIDTH), lambda i: (i, 0, 0)),
            pl.BlockSpec((tb, WINDOW, KV_WIDTH), lambda i: (i, 0, 0)),
            _const_spec(rc.shape), _const_spec(rs1.shape), _const_spec(rs2.shape),
            _const_spec(sinkc.shape), _const_spec(g1.shape), _const_spec(wqkv.shape),
        ],
        out_specs=[
            pl.BlockSpec((tb * QROWS, LANES), lambda i: (i, 0)),
            pl.BlockSpec((tb, WINDOW, KV_WIDTH), lambda i: (i, 0, 0)),
            pl.BlockSpec((tb, WINDOW, KV_WIDTH), lambda i: (i, 0, 0)),
        ],
        out_shape=[
            jax.ShapeDtypeStruct((nb * QROWS, LANES), F32),
            jax.ShapeDtypeStruct((nb, WINDOW, KV_WIDTH), F32),
            jax.ShapeDtypeStruct((nb, WINDOW, KV_WIDTH), F32),
        ],
        scratch_shapes=[pltpu.VMEM((tb * QROWS, LANES), F32),
                        pltpu.VMEM((tb, KV_WIDTH), F32), pltpu.VMEM((tb, KV_WIDTH), F32)],
        compiler_params=pltpu.CompilerParams(
            dimension_semantics=("arbitrary",), vmem_limit_bytes=VMEM_LIMIT),
        name="sample_attn",
    )(x, kbuf, vbuf, rc, rs1, rs2, sinkc, g1, wqkv)


def _sample_tail_kernel(x_ref, o3_ref, h0re_ref, h0im_ref, g1_ref, wu_ref, wg_ref, bg_ref, wbad_ref,
                        wbs_ref, wo_ref, lb_ref, bblk_ref, ctblk_ref, d_ref, wglu_ref, bglu_ref,
                        x1_ref, hre_ref, him_ref):
    nb = x_ref.shape[0]
    x = x_ref[...]
    hn = _rms(x, g1_ref[...]).astype(BF)

    lane = lax.broadcasted_iota(jnp.int32, (nb, LANES), 1)
    lo = lane < HEAD_DIM
    a = jnp.zeros((nb, D_MODEL), F32)
    zero = jnp.zeros((nb, LANES), F32)
    for h in range(N_Q_HEADS):
        oh = o3_ref[pl.ds(h, nb, stride=QROWS), :]
        oh = jnp.where(lo, oh, zero) if h < N_Q_HEADS // 2 else jnp.where(lo, zero, oh)
        a = a + _dot(oh.astype(BF), wbad_ref[h])

    u = _dot(hn, wu_ref[...])
    ub = u.astype(BF)
    lre, lim = lb_ref[0:1, :], lb_ref[1:2, :]
    y_cols = []
    for o in range(N_OCT):
        sl = slice(o * OCT_STATES, (o + 1) * OCT_STATES)
        bu = _dot(ub[:, o * LANES:(o + 1) * LANES], bblk_ref[o])
        h0r, h0i = h0re_ref[:, sl], h0im_ref[:, sl]
        hr = bu[:, :OCT_STATES] + (lre[:, sl] * h0r - lim[:, sl] * h0i)
        hi = bu[:, OCT_STATES:] + (lre[:, sl] * h0i + lim[:, sl] * h0r)
        hre_ref[:, sl] = hr
        him_ref[:, sl] = hi
        y_cols.append(_dot_nt(jnp.concatenate([hr, hi], axis=1).astype(BF), ctblk_ref[o]))
    y = jnp.concatenate(y_cols, axis=1) + d_ref[...] * u
    z = _gelu_tanh(y)
    gate = _dot(z.astype(BF), wglu_ref[...]) + bglu_ref[...]
    ssm = (z * _sigmoid(gate)).astype(BF)

    gates = _gate_cols(hn, wg_ref, bg_ref, slice(0, GATE_WIDTH))
    x1_ref[...] = _merge_out(x, gates, a, _dot(ssm, wbs_ref[...]), wo_ref)


def _sample_tail(x, o3, h0re, h0im, g1, wu, wg, bg, wbad, wbs, wo, lb, bblk, ctblk, d, wglu, bglu):
    nb = x.shape[0]
    args = (x, o3, h0re, h0im, g1, wu, wg, bg, wbad, wbs, wo, lb, bblk, ctblk, d, wglu, bglu)
    out_shapes = ((nb, D_MODEL), (nb, N_STATES), (nb, N_STATES))
    return pl.pallas_call(
        _sample_tail_kernel,
        grid=(1,),
        in_specs=[_const_spec(a.shape) for a in args],
        out_specs=[pl.BlockSpec(s, lambda i: (0, 0)) for s in out_shapes],
        out_shape=[jax.ShapeDtypeStruct(s, F32) for s in out_shapes],
        compiler_params=pltpu.CompilerParams(
            dimension_semantics=("arbitrary",), vmem_limit_bytes=VMEM_LIMIT),
        name="sample_tail",
    )(*args)


def _ffn_kernel(x_ref, g2_ref, wgate_ref, wup_ref, wdown_ref, gf_ref, y_ref):
    x = x_ref[...]
    h = _rms(x, g2_ref[...]).astype(BF)
    gate = _dot(h, wgate_ref[...])
    up = _dot(h, wup_ref[...])
    act = (gate * _sigmoid(gate) * up).astype(BF)
    x2 = x + _dot(act, wdown_ref[...])
    y_ref[...] = _rms(x2, gf_ref[...])


def _ffn(x, g2, wgate, wup, wdown, gf, tb):
    n = x.shape[0]
    return pl.pallas_call(
        _ffn_kernel,
        grid=(n // tb,),
        in_specs=[
            pl.BlockSpec((tb, D_MODEL), lambda i: (i, 0)),
            _const_spec((1, D_MODEL)),
            _const_spec(wgate.shape),
            _const_spec(wup.shape),
            _const_spec(wdown.shape),
            _const_spec((1, D_MODEL)),
        ],
        out_specs=pl.BlockSpec((tb, D_MODEL), lambda i: (i, 0)),
        out_shape=jax.ShapeDtypeStruct((n, D_MODEL), F32),
        compiler_params=pltpu.CompilerParams(
            dimension_semantics=("arbitrary",), vmem_limit_bytes=VMEM_LIMIT),
        name="ffn",
    )(x, g2, wgate, wup, wdown, gf)


def _rope_tables(pos):
    inv_freq = ROPE_THETA ** (-(jnp.arange(ROPE_HALF, dtype=F32) * 2.0 / ROPE_DIM))
    ang = pos.astype(F32)[:, None] * inv_freq[None, :]
    cos, sin = jnp.cos(ang), jnp.sin(ang)
    n = pos.shape[0]
    pad = jnp.zeros((n, HEAD_DIM - ROPE_DIM), F32)
    zero = jnp.zeros_like(sin)
    rc = jnp.concatenate([cos, cos, pad + 1.0], axis=1)
    rs1 = jnp.concatenate([zero, sin, pad], axis=1)
    rs2 = jnp.concatenate([-sin, zero, pad], axis=1)
    rep = LANES // HEAD_DIM
    return jnp.tile(rc, (1, rep)), jnp.tile(rs1, (1, rep)), jnp.tile(rs2, (1, rep))


def _cmul(ar, ai, br, bi):
    return ar * br - ai * bi, ar * bi + ai * br


def _ssm_tables(lam_re, lam_im, log_dt, b_re, b_im, c_re, c_im):
    dt = jnp.exp(log_dt)[:, None]
    mag = jnp.exp(lam_re * dt)
    lb_re = mag * jnp.cos(lam_im * dt)
    lb_im = mag * jnp.sin(lam_im * dt)
    den = lam_re * lam_re + lam_im * lam_im
    nr = lb_re - 1.0
    k_re = ((nr * lam_re + lb_im * lam_im) / den)[..., None]
    k_im = ((lb_im * lam_re - nr * lam_im) / den)[..., None]
    bb_re = k_re * b_re - k_im * b_im
    bb_im = k_re * b_im + k_im * b_re

    a_re, a_im = lb_re, lb_im
    for _ in range(int(math.log2(CHUNK))):
        a_re, a_im = _cmul(a_re, a_im, a_re, a_im)
    s_re, s_im = a_re, a_im
    for _ in range(int(math.log2(SEG))):
        s_re, s_im = _cmul(s_re, s_im, s_re, s_im)

    eye = jnp.eye(OCT, dtype=F32).reshape(1, OCT, 1, OCT, 1)

    def block_diag(a):
        r, c = a.shape[1:]
        return (a.reshape(N_OCT, OCT, r, 1, c) * eye).reshape(N_OCT, OCT * r, OCT * c)

    bblk = jnp.concatenate([block_diag(jnp.swapaxes(bb_re, 1, 2)),
                            block_diag(jnp.swapaxes(bb_im, 1, 2))], axis=2)
    ctblk = jnp.concatenate([block_diag(c_re), block_diag(-c_im)], axis=2)

    oct_cols = lambda a: a.reshape(N_OCT, 1, OCT_STATES)
    lcol = jnp.concatenate([oct_cols(lb_re), oct_cols(lb_im),
                            jnp.zeros((N_OCT, 6, OCT_STATES), F32)], axis=1)

    flat = lambda a: a.reshape(1, N_STATES)
    col = lambda a: a.reshape(N_COL, 1, LANES)
    a_tab = jnp.concatenate([col(a_re), col(a_im), col(s_re), col(s_im),
                             jnp.zeros((N_COL, 4, LANES), F32)], axis=1)
    lb = jnp.concatenate([flat(lb_re), flat(lb_im)], axis=0)
    return bblk, ctblk, lcol, a_tab, lb


def kernel(x_prompt, x_sample, state_k_win, state_v_win, state_ssm_re, state_ssm_im, norm1_g, w_in, b_gate, attn_sinks, ssm_lam_re, ssm_lam_im, ssm_log_dt, ssm_b_re, ssm_b_im, ssm_c_re, ssm_c_im, ssm_d, w_glu, b_glu, w_branch_attn, w_branch_ssm, w_out, norm2_g, w_ffn_gate, w_ffn_up, w_ffn_down, norm_f_g):
    depth = w_in.shape[0]
    assert depth == 1
    b, t, _ = x_prompt.shape
    nb, s_len, _ = x_sample.shape
    assert s_len == 1 and state_k_win.shape[2] == WINDOW
    l = 0
    o1 = ATTN_WIDTH + 2 * KV_WIDTH
    o2 = o1 + SSM_WIDTH
    w_in_b = w_in[l].astype(BF)
    wqkv, wu, wg = w_in_b[:, :o1], w_in_b[:, o1:o2], w_in_b[:, o2:]
    g1 = norm1_g[l].reshape(1, D_MODEL)
    g2 = norm2_g[l].reshape(1, D_MODEL)
    gf = norm_f_g.reshape(1, D_MODEL)
    bg = b_gate[l].reshape(1, GATE_WIDTH)
    d = ssm_d[l].reshape(1, SSM_WIDTH)
    wglu = w_glu[l].astype(BF)
    bglu = b_glu[l].reshape(1, SSM_WIDTH)
    wba = w_branch_attn[l].astype(BF)
    wbs = w_branch_ssm[l].astype(BF)
    wo = w_out[l].astype(BF)
    wgate = w_ffn_gate[l].astype(BF)
    wup = w_ffn_up[l].astype(BF)
    wdown = w_ffn_down[l].astype(BF)
    sinks = attn_sinks[l]

    bblk, ctblk, lcol, a_tab, lb = _ssm_tables(
        ssm_lam_re[l], ssm_lam_im[l], ssm_log_dt[l], ssm_b_re[l], ssm_b_im[l], ssm_c_re[l], ssm_c_im[l])

    rc, rs1, rs2 = _rope_tables(jnp.arange(t, dtype=jnp.int32))
    ssm_p, hre_p, him_p = _s5_prompt(x_prompt, g1, wu, bblk, ctblk, lcol, a_tab, d, wglu, bglu)
    x1_p, kwin_p, vwin_p = _mix_prompt(x_prompt, ssm_p, rc, rs1, rs2, sinks, g1, wqkv, wg, bg, wba, wbs, wo)
    y_p = _ffn(x1_p, g2, wgate, wup, wdown, gf, FFN_TB).reshape(b, t, D_MODEL)

    rcs, rs1s, rs2s = _rope_tables(PAST_LEN + jnp.arange(1, dtype=jnp.int32))
    sinkc = jnp.concatenate([sinks, jnp.zeros((QROWS - N_Q_HEADS,), F32)]).reshape(QROWS, 1)
    wbad = jnp.concatenate([wba.reshape(N_Q_HEADS, HEAD_DIM, D_MODEL)] * 2, axis=1)
    xs = x_sample.reshape(nb, D_MODEL)
    o3, kwin_s, vwin_s = _sample_attn(
        xs, state_k_win[l].reshape(nb, WINDOW, KV_WIDTH), state_v_win[l].reshape(nb, WINDOW, KV_WIDTH),
        rcs, rs1s, rs2s, sinkc, g1, wqkv)
    x1_s, hre_s, him_s = _sample_tail(
        xs, o3, state_ssm_re[l].reshape(nb, N_STATES), state_ssm_im[l].reshape(nb, N_STATES),
        g1, wu, wg, bg, wbad, wbs, wo, lb, bblk.astype(BF), ctblk.astype(BF), d, wglu, bglu)
    y_s = _ffn(x1_s, g2, wgate, wup, wdown, gf, nb).reshape(nb, 1, D_MODEL)

    kv_shape_p = (1, b, WINDOW, N_KV_HEADS, HEAD_DIM)
    st_shape_p = (1, b, N_SSM_GROUPS, SSM_STATE)
    kv_shape_s = (1, nb, WINDOW, N_KV_HEADS, HEAD_DIM)
    st_shape_s = (1, nb, N_SSM_GROUPS, SSM_STATE)
    return (y_p, y_s,
            kwin_p.reshape(kv_shape_p), vwin_p.reshape(kv_shape_p),
            hre_p.reshape(st_shape_p), him_p.reshape(st_shape_p),
            kwin_s.reshape(kv_shape_s), vwin_s.reshape(kv_shape_s),
            hre_s.reshape(st_shape_s), him_s.reshape(st_shape_s))
```

```python
import math

import jax
import jax.numpy as jnp
import numpy as np
from jax import lax
from jax.experimental import pallas as pl
from jax.experimental.pallas import tpu as pltpu

D_MODEL = 1024
N_Q_HEADS = 8
N_KV_HEADS = 2
HEAD_DIM = 64
ATTN_WIDTH = N_Q_HEADS * HEAD_DIM
KV_WIDTH = N_KV_HEADS * HEAD_DIM
WINDOW = 128
ROPE_DIM = HEAD_DIM // 4
ROPE_HALF = ROPE_DIM // 2
ROPE_THETA = 500000.0
SSM_WIDTH = D_MODEL // 2
SSM_GROUP = 16
N_SSM_GROUPS = SSM_WIDTH // SSM_GROUP
SSM_STATE = 64
N_STATES = N_SSM_GROUPS * SSM_STATE
GATE_WIDTH = 2 * D_MODEL
D_FF = -(-8 * D_MODEL // (3 * 256)) * 256
NORM_EPS = 1e-5
PAST_LEN = 8192

LANES = 128
CHUNK = 8
OCT = LANES // SSM_GROUP
N_OCT = N_SSM_GROUPS // OCT
OCT_STATES = OCT * SSM_STATE
OCT_COL = OCT_STATES // LANES
N_COL = N_STATES // LANES
S5_ROWS = 128
SEG = 16
M_PAIR = 2
SEG_PITCH = SEG * CHUNK + 8
MIX_TB = 512
FFN_TB = 512
NEG_BIG = -1e30
VMEM_LIMIT = 56 * 1024 * 1024

BF = jnp.bfloat16
F32 = jnp.float32


def _dot(a, b):
    return jnp.dot(a, b, preferred_element_type=F32)


def _dot_nt(a, b):
    return lax.dot_general(a, b, (((1,), (1,)), ((), ())), preferred_element_type=F32)


def _dot_nt_split(a, b):
    a_hi, b_hi = a.astype(BF), b.astype(BF)
    a_lo = (a - a_hi.astype(F32)).astype(BF)
    b_lo = (b - b_hi.astype(F32)).astype(BF)
    return _dot_nt(a_hi, b_hi) + (_dot_nt(a_hi, b_lo) + _dot_nt(a_lo, b_hi))


def _rms(x, g):
    return x * lax.rsqrt(jnp.mean(x * x, axis=-1, keepdims=True) + NORM_EPS) * g


def _sigmoid(x):
    return 1.0 / (1.0 + jnp.exp(-x))


def _gelu_tanh(x):
    c = math.sqrt(2.0 / math.pi)
    return 0.5 * x * (1.0 + jnp.tanh(c * (x + 0.044715 * (x * x * x))))


def _rope(a, rc, rs1, rs2):
    return a * rc + pltpu.roll(a, ROPE_HALF, 1) * rs1 + pltpu.roll(a, LANES - ROPE_HALF, 1) * rs2


def _const_spec(shape):
    nd = len(shape)
    return pl.BlockSpec(shape, lambda *_: (0,) * nd, pipeline_mode=pl.Buffered(1))


def _build_chunk_operators(bblk_ref, ctblk_ref, lcol_ref, m_s, e_s, f_s):
    for o in range(N_OCT):
        ct = ctblk_ref[o]
        lr, li = lcol_ref[o, 0:1, :], lcol_ref[o, 1:2, :]
        er, ei = bblk_ref[o, :, :OCT_STATES], bblk_ref[o, :, OCT_STATES:]
        k_blk = []
        for tau in range(CHUNK):
            e_cat = jnp.concatenate([er, ei], axis=1)
            i = CHUNK - 1 - tau
            e_s[o, i * LANES:(i + 1) * LANES, :] = e_cat.astype(BF)
            k_blk.append(_dot_nt_split(e_cat, ct).astype(BF))
            er, ei = er * lr - ei * li, er * li + ei * lr
        zero = jnp.zeros((LANES, LANES), BF)
        for j in range(CHUNK):
            jt, jj = divmod(j, M_PAIR)
            for i in range(M_PAIR * (jt + 1)):
                m_s[jt][o, i * LANES:(i + 1) * LANES, jj * LANES:(jj + 1) * LANES] = (
                    k_blk[j - i] if j >= i else zero)
        tr, ti = ct[:, :OCT_STATES], -ct[:, OCT_STATES:]
        for j in range(CHUNK):
            tr, ti = tr * lr - ti * li, tr * li + ti * lr
            f_s[o, :OCT_STATES, j * LANES:(j + 1) * LANES] = tr.T.astype(BF)
            f_s[o, OCT_STATES:, j * LANES:(j + 1) * LANES] = (-ti).T.astype(BF)


def _s5_prompt_kernel(x_ref, g1_ref, wu_ref, bblk_ref, ctblk_ref, lcol_ref, a_ref, d_ref,
                      wglu_ref, bglu_ref,
                      out_ref, hfin_ref,
                      m0_ref, m1_ref, m2_ref, m3_ref, e_ref, f_ref, us, ys, sre, sim, car):
    m_ref = (m0_ref, m1_ref, m2_ref, m3_ref)
    blk = pl.program_id(1)

    @pl.when((pl.program_id(0) == 0) & (blk == 0))
    def _():
        _build_chunk_operators(bblk_ref, ctblk_ref, lcol_ref, m_ref, e_ref, f_ref)

    @pl.when(blk == 0)
    def _():
        car[...] = jnp.zeros_like(car)

    seg_tokens = SEG * CHUNK
    seg_rows = [slice(s * SEG_PITCH, s * SEG_PITCH + seg_tokens) for s in range(8)]
    tok_rows = [slice(s * seg_tokens, (s + 1) * seg_tokens) for s in range(8)]

    for s in range(0, 8, 2):
        hn = _rms(x_ref[s * seg_tokens:(s + 2) * seg_tokens, :], g1_ref[...]).astype(BF)
        u = _dot(hn, wu_ref[...])
        for half in range(2):
            for cc in range(N_OCT):
                us[cc, seg_rows[s + half], :] = u[tok_rows[half], cc * LANES:(cc + 1) * LANES]

    sub = lax.broadcasted_iota(jnp.int32, (8, LANES), 0)

    def scan_column(c):
        tab = a_ref[c]
        are, aim = tab[0:1], tab[1:2]
        bre, bim = tab[2:3], tab[3:4]

        def step(cr, ci, r, keep_entering):
            slab = slice(r * 8, (r + 1) * 8)
            s_r, s_i = sre[c, slab, :], sim[c, slab, :]
            if keep_entering:
                sre[c, slab, :] = cr
                sim[c, slab, :] = ci
            return are * cr - aim * ci + s_r, are * ci + aim * cr + s_i

        cr = jnp.zeros((8, LANES), F32)
        ci = jnp.zeros((8, LANES), F32)
        for r in range(SEG):
            cr, ci = step(cr, ci, r, False)
        cv = car[c]
        pr, pi = cv[0:1], cv[1:2]
        sr = jnp.zeros((8, LANES), F32)
        si = jnp.zeros((8, LANES), F32)
        for s in range(8):
            sr = jnp.where(sub == s, pr, sr)
            si = jnp.where(sub == s, pi, si)
            pr, pi = (bre * pr - bim * pi + cr[s:s + 1], bre * pi + bim * pr + ci[s:s + 1])
        end = jnp.where(sub == 0, pr, jnp.where(sub == 1, pi, 0.0))
        car[c] = end
        hfin_ref[c] = end
        cr, ci = sr, si
        for r in range(SEG):
            cr, ci = step(cr, ci, r, True)

    gate = bglu_ref[...]
    for o in range(N_OCT):
        uo = jnp.concatenate(
            [jnp.concatenate([us[o, pl.ds(r * CHUNK + i, 8, stride=SEG_PITCH), :] for r in range(SEG)],
                             axis=0).astype(BF) for i in range(CHUNK)], axis=1)
        s_end = _dot(uo, e_ref[o])
        cols = range(o * OCT_COL, (o + 1) * OCT_COL)
        for cc, c in enumerate(cols):
            sre[c] = s_end[:, cc * LANES:(cc + 1) * LANES]
            sim[c] = s_end[:, OCT_STATES + cc * LANES:OCT_STATES + (cc + 1) * LANES]
        y_in = jnp.concatenate(
            [_dot(uo[:, :(jt + 1) * M_PAIR * LANES], m_ref[jt][o]) for jt in range(CHUNK // M_PAIR)],
            axis=1)
        for c in cols:
            scan_column(c)
        hp = jnp.concatenate([sre[c] for c in cols] + [sim[c] for c in cols], axis=1).astype(BF)
        yo = y_in + _dot(hp, f_ref[o])
        for r in range(SEG):
            for j in range(CHUNK):
                ys[o, pl.ds(r * CHUNK + j, 8, stride=SEG_PITCH), :] = (
                    yo[r * 8:(r + 1) * 8, j * LANES:(j + 1) * LANES])
        d_o = d_ref[:, o * LANES:(o + 1) * LANES]
        for rs in seg_rows:
            ys[o, rs, :] = _gelu_tanh(ys[o, rs, :] + d_o * us[o, rs, :])
        if o % 2 == 1:
            z2 = jnp.concatenate(
                [jnp.concatenate([ys[o - 1, rs, :], ys[o, rs, :]], axis=1) for rs in seg_rows],
                axis=0).astype(BF)
            gate = gate + _dot(z2, wglu_ref[(o - 1) * LANES:(o + 1) * LANES, :])

    for s in range(8):
        z = jnp.concatenate([ys[o, seg_rows[s], :] for o in range(N_OCT)], axis=1)
        out_ref[tok_rows[s], :] = (z * _sigmoid(gate[tok_rows[s], :])).astype(out_ref.dtype)


def _s5_prompt(x, g1, wu, bblk, ctblk, lcol, a_tab, d, wglu, bglu):
    b, t, _ = x.shape
    tb = S5_ROWS * CHUNK
    nblk = t // tb
    x2 = x.reshape(b * t, D_MODEL)
    row_map = lambda i, j: (i * nblk + j, 0)
    op_shape = (N_OCT, CHUNK * LANES, CHUNK * LANES)
    out, hfin = pl.pallas_call(
        _s5_prompt_kernel,
        grid=(b, nblk),
        in_specs=[
            pl.BlockSpec((tb, D_MODEL), row_map),
            _const_spec((1, D_MODEL)),
            _const_spec((D_MODEL, SSM_WIDTH)),
            _const_spec(bblk.shape),
            _const_spec(ctblk.shape),
            _const_spec(lcol.shape),
            _const_spec(a_tab.shape),
            _const_spec((1, SSM_WIDTH)),
            _const_spec((SSM_WIDTH, SSM_WIDTH)),
            _const_spec((1, SSM_WIDTH)),
        ],
        out_specs=[
            pl.BlockSpec((tb, SSM_WIDTH), row_map),
            pl.BlockSpec((None, N_COL, 8, LANES), lambda i, j: (i, 0, 0, 0)),
        ],
        out_shape=[
            jax.ShapeDtypeStruct((b * t, SSM_WIDTH), BF),
            jax.ShapeDtypeStruct((b, N_COL, 8, LANES), F32),
        ],
        scratch_shapes=[
            *[pltpu.VMEM((N_OCT, (jt + 1) * M_PAIR * LANES, M_PAIR * LANES), BF)
              for jt in range(CHUNK // M_PAIR)],
            pltpu.VMEM(op_shape, BF), pltpu.VMEM(op_shape, BF),
            pltpu.VMEM((N_OCT, 8 * SEG_PITCH, LANES), F32), pltpu.VMEM((N_OCT, 8 * SEG_PITCH, LANES), F32),
            pltpu.VMEM((N_COL, S5_ROWS, LANES), F32), pltpu.VMEM((N_COL, S5_ROWS, LANES), F32),
            pltpu.VMEM((N_COL, 8, LANES), F32),
        ],
        compiler_params=pltpu.CompilerParams(
            dimension_semantics=("arbitrary", "arbitrary"), vmem_limit_bytes=VMEM_LIMIT),
        name="s5_prompt",
    )(x2, g1, wu, bblk, ctblk, lcol, a_tab, d, wglu, bglu)
    return out, hfin[:, :, 0, :], hfin[:, :, 1, :]


def _gate_cols(hn, wg_ref, bg_ref, cols):
    return _sigmoid(_dot(hn, wg_ref[:, cols]) + bg_ref[:, cols])


def _merge_out(x, gates, attn_proj, ssm_proj, wo_ref):
    merged = gates[:, :D_MODEL] * attn_proj + gates[:, D_MODEL:] * ssm_proj
    return x + _dot(merged.astype(BF), wo_ref[...])


def _half_split(a, ar, lo):
    z = jnp.zeros_like(a)
    return (jnp.where(lo, a, z).astype(BF), jnp.where(lo, z, ar).astype(BF),
            jnp.where(lo, ar, z).astype(BF), jnp.where(lo, z, a).astype(BF))


def _softmax_sink_t(st, bias_t, sink):
    st = st + bias_t
    m = jnp.maximum(jnp.max(st, axis=0, keepdims=True), sink)
    p = jnp.exp(st - m)
    den = jnp.sum(p, axis=0, keepdims=True) + jnp.exp(sink - m)
    return (p * (1.0 / den)).astype(BF)


def _mix_prompt_kernel(sinks_ref, x_ref, ssm_ref, rc_ref, rs1_ref, rs2_ref, g1_ref, wqkv_ref,
                       wg_ref, bg_ref, wba_ref, wbs_ref, wo_ref,
                       x1_ref, kwin_ref, vwin_ref, kprev, vprev):
    t = pl.program_id(1)

    @pl.when(t == 0)
    def _():
        kprev[...] = jnp.zeros_like(kprev)
        vprev[...] = jnp.zeros_like(vprev)

    x = x_ref[...]
    hn = _rms(x, g1_ref[...]).astype(BF)
    qkv = _dot(hn, wqkv_ref[...])
    rc, rs1, rs2 = rc_ref[...], rs1_ref[...], rs2_ref[...]
    scale = HEAD_DIM ** -0.5
    q = [(_rope(qkv[:, c * LANES:(c + 1) * LANES], rc, rs1, rs2) * scale).astype(BF)
         for c in range(ATTN_WIDTH // LANES)]
    k = _rope(qkv[:, ATTN_WIDTH:ATTN_WIDTH + KV_WIDTH], rc, rs1, rs2)
    v = qkv[:, ATTN_WIDTH + KV_WIDTH:ATTN_WIDTH + 2 * KV_WIDTH]

    w = WINDOW
    n_sub = MIX_TB // w
    lane = lax.broadcasted_iota(jnp.int32, (2 * w, LANES), 1)
    lo = lane < HEAD_DIM
    kj = lax.broadcasted_iota(jnp.int32, (2 * w, w), 0)
    qi = lax.broadcasted_iota(jnp.int32, (2 * w, w), 1)
    band = (kj > qi) & (kj <= qi + w)
    first = band & ((kj >= w) | (t > 0))
    bias_band = jnp.where(band, 0.0, NEG_BIG).astype(F32)
    bias_first = jnp.where(first, 0.0, NEG_BIG).astype(F32)
    col = lax.broadcasted_iota(jnp.int32, (1, 2 * w), 1)

    def sink_row(ha, hb):
        return jnp.where(col < w, sinks_ref[ha], sinks_ref[hb])

    ssm_proj = _dot(ssm_ref[...], wbs_ref[...])
    gate_w = GATE_WIDTH // (n_sub * N_KV_HEADS)
    gate_cols = []
    attn_rows = []
    for sb in range(n_sub):
        cur = slice(sb * w, (sb + 1) * w)
        if sb == 0:
            kcat = jnp.concatenate([kprev[...], k[cur]], axis=0)
            vcat = jnp.concatenate([vprev[...], v[cur]], axis=0)
            bias = bias_first
        else:
            kcat = k[(sb - 1) * w:(sb + 1) * w]
            vcat = v[(sb - 1) * w:(sb + 1) * w]
            bias = bias_band
        bias2 = jnp.concatenate([bias, bias], axis=1)
        ka, kb, kc, kd = _half_split(kcat, pltpu.roll(kcat, HEAD_DIM, 1), lo)
        vt = vcat.T
        zero = jnp.zeros((HEAD_DIM, 2 * w), F32)
        v_rows = [(jnp.concatenate([vt[g * HEAD_DIM:(g + 1) * HEAD_DIM], zero], axis=0).astype(BF),
                   jnp.concatenate([zero, vt[g * HEAD_DIM:(g + 1) * HEAD_DIM]], axis=0).astype(BF))
                  for g in range(N_KV_HEADS)]
        outs = []
        for grp, (k_e, k_o) in enumerate(((ka, kb), (kc, kd))):
            qq = jnp.concatenate([q[2 * grp][cur], q[2 * grp + 1][cur]], axis=0)
            s_e, s_o = _dot_nt(k_e, qq), _dot_nt(k_o, qq)
            gc = sb * N_KV_HEADS + grp
            gate_cols.append(_gate_cols(hn, wg_ref, bg_ref, slice(gc * gate_w, (gc + 1) * gate_w)))
            p_e = _softmax_sink_t(s_e, bias2, sink_row(4 * grp, 4 * grp + 2))
            p_o = _softmax_sink_t(s_o, bias2, sink_row(4 * grp + 1, 4 * grp + 3))
            v_e, v_o = v_rows[grp]
            o2 = _dot(v_e, p_e) + _dot(v_o, p_o)
            outs += [o2[:, :w].T, o2[:, w:].T]
        attn_rows.append(jnp.concatenate(outs, axis=1))
    attn = jnp.concatenate(attn_rows, axis=0).astype(BF)
    gates = jnp.concatenate(gate_cols, axis=1)

    kprev[...] = k[MIX_TB - w:]
    vprev[...] = v[MIX_TB - w:]
    kwin_ref[...] = k[MIX_TB - w:]
    vwin_ref[...] = v[MIX_TB - w:]
    x1_ref[...] = _merge_out(x, gates, _dot(attn, wba_ref[...]), ssm_proj, wo_ref)


def _mix_prompt(x, ssm, rc, rs1, rs2, sinks, g1, wqkv, wg, bg, wba, wbs, wo):
    b, t, _ = x.shape
    nblk = t // MIX_TB
    x2 = x.reshape(b * t, D_MODEL)
    row_map = lambda i, j: (i * nblk + j, 0)
    x1, kwin, vwin = pl.pallas_call(
        _mix_prompt_kernel,
        grid=(b, nblk),
        in_specs=[
            pl.BlockSpec(memory_space=pltpu.SMEM),
            pl.BlockSpec((MIX_TB, D_MODEL), row_map),
            pl.BlockSpec((MIX_TB, SSM_WIDTH), row_map),
            pl.BlockSpec((MIX_TB, LANES), lambda i, j: (j, 0)),
            pl.BlockSpec((MIX_TB, LANES), lambda i, j: (j, 0)),
            pl.BlockSpec((MIX_TB, LANES), lambda i, j: (j, 0)),
            _const_spec((1, D_MODEL)),
            _const_spec(wqkv.shape),
            _const_spec(wg.shape),
            _const_spec(bg.shape),
            _const_spec(wba.shape),
            _const_spec(wbs.shape),
            _const_spec(wo.shape),
        ],
        out_specs=[
            pl.BlockSpec((MIX_TB, D_MODEL), row_map),
            pl.BlockSpec((None, WINDOW, KV_WIDTH), lambda i, j: (i, 0, 0)),
            pl.BlockSpec((None, WINDOW, KV_WIDTH), lambda i, j: (i, 0, 0)),
        ],
        out_shape=[
            jax.ShapeDtypeStruct((b * t, D_MODEL), F32),
            jax.ShapeDtypeStruct((b, WINDOW, KV_WIDTH), F32),
            jax.ShapeDtypeStruct((b, WINDOW, KV_WIDTH), F32),
        ],
        scratch_shapes=[pltpu.VMEM((WINDOW, KV_WIDTH), F32), pltpu.VMEM((WINDOW, KV_WIDTH), F32)],
        compiler_params=pltpu.CompilerParams(
            dimension_semantics=("arbitrary", "arbitrary"), vmem_limit_bytes=VMEM_LIMIT),
        name="mix_prompt",
    )(sinks, x2, ssm, rc, rs1, rs2, g1, wqkv, wg, bg, wba, wbs, wo)
    return x1, kwin, vwin


QROWS = 16
SAMPLE_TB = 32


def _sample_attn_kernel(x_ref, kbuf_ref, vbuf_ref, rc_ref, rs1_ref, rs2_ref, sinkc_ref, g1_ref, wqkv_ref,
                        o_ref, kout_ref, vout_ref, qz, knew, vnew):
    nb = x_ref.shape[0]
    hn = _rms(x_ref[...], g1_ref[...]).astype(BF)
    qkv = _dot(hn, wqkv_ref[...])
    rc, rs1, rs2 = rc_ref[...], rs1_ref[...], rs2_ref[...]
    scale = HEAD_DIM ** -0.5
    knew[...] = _rope(qkv[:, ATTN_WIDTH:ATTN_WIDTH + KV_WIDTH], rc, rs1, rs2)
    vnew[...] = qkv[:, ATTN_WIDTH + KV_WIDTH:ATTN_WIDTH + 2 * KV_WIDTH]

    lane = lax.broadcasted_iota(jnp.int32, (nb, LANES), 1)
    lo = lane < HEAD_DIM
    qz[...] = jnp.zeros_like(qz)
    for c in range(ATTN_WIDTH // LANES):
        qc = _rope(qkv[:, c * LANES:(c + 1) * LANES], rc, rs1, rs2) * scale
        qr = pltpu.roll(qc, HEAD_DIM, 1)
        zero = jnp.zeros_like(qc)
        if c < 2:
            even, odd = jnp.where(lo, qc, zero), jnp.where(lo, qr, zero)
        else:
            even, odd = jnp.where(lo, zero, qr), jnp.where(lo, zero, qc)
        qz[pl.ds(2 * c, nb, stride=QROWS), :] = even
        qz[pl.ds(2 * c + 1, nb, stride=QROWS), :] = odd

    def shift(b, carry):
        kout_ref[b, 0:WINDOW - 1, :] = kbuf_ref[b, 1:WINDOW, :]
        vout_ref[b, 0:WINDOW - 1, :] = vbuf_ref[b, 1:WINDOW, :]
        kout_ref[b, WINDOW - 1:WINDOW, :] = knew[pl.ds(b, 1), :]
        vout_ref[b, WINDOW - 1:WINDOW, :] = vnew[pl.ds(b, 1), :]
        return carry

    lax.fori_loop(0, nb, shift, 0)

    sink = sinkc_ref[...]
    q3 = qz[...].reshape(nb, QROWS, LANES).astype(BF)
    s = jnp.einsum('bhd,bkd->bhk', q3, kout_ref[...].astype(BF), preferred_element_type=F32)
    m = jnp.maximum(jnp.max(s, axis=-1, keepdims=True), sink)
    p = jnp.exp(s - m)
    den = jnp.sum(p, axis=-1, keepdims=True) + jnp.exp(sink - m)
    p = (p * (1.0 / den)).astype(BF)
    o3 = jnp.einsum('bhk,bkd->bhd', p, vout_ref[...].astype(BF), preferred_element_type=F32)
    o_ref[...] = o3.reshape(nb * QROWS, LANES)


def _sample_attn(x, kbuf, vbuf, rc, rs1, rs2, sinkc, g1, wqkv):
    nb = x.shape[0]
    tb = SAMPLE_TB
    return pl.pallas_call(
        _sample_attn_kernel,
        grid=(nb // tb,),
        in_specs=[
            pl.BlockSpec((tb, D_MODEL), lambda i: (i, 0)),
            pl.BlockSpec((tb, WINDOW, KV_WIDTH), lambda i: (i, 0, 0)),
            pl.BlockSpec((tb, WINDOW, KV_WIDTH), lambda i: (i, 0, 0)),
            _const_spec(rc.shape), _const_spec(rs1.shape), _const_spec(rs2.shape),
            _const_spec(sinkc.shape), _const_spec(g1.shape), _const_spec(wqkv.shape),
        ],
        out_specs=[
            pl.BlockSpec((tb * QROWS, LANES), lambda i: (i, 0)),
            pl.BlockSpec((tb, WINDOW, KV_WIDTH), lambda i: (i, 0, 0)),
            pl.BlockSpec((tb, WINDOW, KV_WIDTH), lambda i: (i, 0, 0)),
        ],
        out_shape=[
            jax.ShapeDtypeStruct((nb * QROWS, LANES), F32),
            jax.ShapeDtypeStruct((nb, WINDOW, KV_WIDTH), F32),
            jax.ShapeDtypeStruct((nb, WINDOW, KV_WIDTH), F32),
        ],
        scratch_shapes=[pltpu.VMEM((tb * QROWS, LANES), F32),
                        pltpu.VMEM((tb, KV_WIDTH), F32), pltpu.VMEM((tb, KV_WIDTH), F32)],
        compiler_params=pltpu.CompilerParams(
            dimension_semantics=("arbitrary",), vmem_limit_bytes=VMEM_LIMIT),
        name="sample_attn",
    )(x, kbuf, vbuf, rc, rs1, rs2, sinkc, g1, wqkv)


def _sample_tail_kernel(x_ref, o3_ref, h0re_ref, h0im_ref, g1_ref, wu_ref, wg_ref, bg_ref, wbad_ref,
                        wbs_ref, wo_ref, lb_ref, bblk_ref, ctblk_ref, d_ref, wglu_ref, bglu_ref,
                        x1_ref, hre_ref, him_ref):
    nb = x_ref.shape[0]
    x = x_ref[...]
    hn = _rms(x, g1_ref[...]).astype(BF)

    lane = lax.broadcasted_iota(jnp.int32, (nb, LANES), 1)
    lo = lane < HEAD_DIM
    a = jnp.zeros((nb, D_MODEL), F32)
    zero = jnp.zeros((nb, LANES), F32)
    for h in range(N_Q_HEADS):
        oh = o3_ref[pl.ds(h, nb, stride=QROWS), :]
        oh = jnp.where(lo, oh, zero) if h < N_Q_HEADS // 2 else jnp.where(lo, zero, oh)
        a = a + _dot(oh.astype(BF), wbad_ref[h])

    u = _dot(hn, wu_ref[...])
    ub = u.astype(BF)
    lre, lim = lb_ref[0:1, :], lb_ref[1:2, :]
    y_cols = []
    for o in range(N_OCT):
        sl = slice(o * OCT_STATES, (o + 1) * OCT_STATES)
        bu = _dot(ub[:, o * LANES:(o + 1) * LANES], bblk_ref[o])
        h0r, h0i = h0re_ref[:, sl], h0im_ref[:, sl]
        hr = bu[:, :OCT_STATES] + (lre[:, sl] * h0r - lim[:, sl] * h0i)
        hi = bu[:, OCT_STATES:] + (lre[:, sl] * h0i + lim[:, sl] * h0r)
        hre_ref[:, sl] = hr
        him_ref[:, sl] = hi
        y_cols.append(_dot_nt(jnp.concatenate([hr, hi], axis=1).astype(BF), ctblk_ref[o]))
    y = jnp.concatenate(y_cols, axis=1) + d_ref[...] * u
    z = _gelu_tanh(y)
    gate = _dot(z.astype(BF), wglu_ref[...]) + bglu_ref[...]
    ssm = (z * _sigmoid(gate)).astype(BF)

    gates = _gate_cols(hn, wg_ref, bg_ref, slice(0, GATE_WIDTH))
    x1_ref[...] = _merge_out(x, gates, a, _dot(ssm, wbs_ref[...]), wo_ref)


def _sample_tail(x, o3, h0re, h0im, g1, wu, wg, bg, wbad, wbs, wo, lb, bblk, ctblk, d, wglu, bglu):
    nb = x.shape[0]
    args = (x, o3, h0re, h0im, g1, wu, wg, bg, wbad, wbs, wo, lb, bblk, ctblk, d, wglu, bglu)
    out_shapes = ((nb, D_MODEL), (nb, N_STATES), (nb, N_STATES))
    return pl.pallas_call(
        _sample_tail_kernel,
        grid=(1,),
        in_specs=[_const_spec(a.shape) for a in args],
        out_specs=[pl.BlockSpec(s, lambda i: (0, 0)) for s in out_shapes],
        out_shape=[jax.ShapeDtypeStruct(s, F32) for s in out_shapes],
        compiler_params=pltpu.CompilerParams(
            dimension_semantics=("arbitrary",), vmem_limit_bytes=VMEM_LIMIT),
        name="sample_tail",
    )(*args)


def _ffn_kernel(x_ref, g2_ref, wgate_ref, wup_ref, wdown_ref, gf_ref, y_ref):
    x = x_ref[...]
    h = _rms(x, g2_ref[...]).astype(BF)
    gate = _dot(h, wgate_ref[...])
    up = _dot(h, wup_ref[...])
    act = (gate * _sigmoid(gate) * up).astype(BF)
    x2 = x + _dot(act, wdown_ref[...])
    y_ref[...] = _rms(x2, gf_ref[...])


def _ffn(x, g2, wgate, wup, wdown, gf, tb):
    n = x.shape[0]
    return pl.pallas_call(
        _ffn_kernel,
        grid=(n // tb,),
        in_specs=[
            pl.BlockSpec((tb, D_MODEL), lambda i: (i, 0)),
            _const_spec((1, D_MODEL)),
            _const_spec(wgate.shape),
            _const_spec(wup.shape),
            _const_spec(wdown.shape),
            _const_spec((1, D_MODEL)),
        ],
        out_specs=pl.BlockSpec((tb, D_MODEL), lambda i: (i, 0)),
        out_shape=jax.ShapeDtypeStruct((n, D_MODEL), F32),
        compiler_params=pltpu.CompilerParams(
            dimension_semantics=("arbitrary",), vmem_limit_bytes=VMEM_LIMIT),
        name="ffn",
    )(x, g2, wgate, wup, wdown, gf)


def _rope_tables(pos):
    pos = np.asarray(pos, np.float64)
    inv_freq = ROPE_THETA ** (-(np.arange(ROPE_HALF, dtype=np.float64) * 2.0 / ROPE_DIM))
    ang = pos[:, None] * inv_freq[None, :]
    cos, sin = np.cos(ang), np.sin(ang)
    pad = np.zeros((pos.shape[0], HEAD_DIM - ROPE_DIM))
    zero = np.zeros_like(sin)
    rc = np.concatenate([cos, cos, pad + 1.0], axis=1)
    rs1 = np.concatenate([zero, sin, pad], axis=1)
    rs2 = np.concatenate([-sin, zero, pad], axis=1)
    rep = LANES // HEAD_DIM
    return tuple(jnp.asarray(np.tile(a, (1, rep)), F32) for a in (rc, rs1, rs2))


def _cmul(ar, ai, br, bi):
    return ar * br - ai * bi, ar * bi + ai * br


def _ssm_tables(lam_re, lam_im, log_dt, b_re, b_im, c_re, c_im):
    dt = jnp.exp(log_dt)[:, None]
    mag = jnp.exp(lam_re * dt)
    lb_re = mag * jnp.cos(lam_im * dt)
    lb_im = mag * jnp.sin(lam_im * dt)
    den = lam_re * lam_re + lam_im * lam_im
    nr = lb_re - 1.0
    k_re = ((nr * lam_re + lb_im * lam_im) / den)[..., None]
    k_im = ((lb_im * lam_re - nr * lam_im) / den)[..., None]
    bb_re = k_re * b_re - k_im * b_im
    bb_im = k_re * b_im + k_im * b_re

    a_re, a_im = lb_re, lb_im
    for _ in range(int(math.log2(CHUNK))):
        a_re, a_im = _cmul(a_re, a_im, a_re, a_im)
    s_re, s_im = a_re, a_im
    for _ in range(int(math.log2(SEG))):
        s_re, s_im = _cmul(s_re, s_im, s_re, s_im)

    eye = jnp.eye(OCT, dtype=F32).reshape(1, OCT, 1, OCT, 1)

    def block_diag(a):
        r, c = a.shape[1:]
        return (a.reshape(N_OCT, OCT, r, 1, c) * eye).reshape(N_OCT, OCT * r, OCT * c)

    bblk = jnp.concatenate([block_diag(jnp.swapaxes(bb_re, 1, 2)),
                            block_diag(jnp.swapaxes(bb_im, 1, 2))], axis=2)
    ctblk = jnp.concatenate([block_diag(c_re), block_diag(-c_im)], axis=2)

    oct_cols = lambda a: a.reshape(N_OCT, 1, OCT_STATES)
    lcol = jnp.concatenate([oct_cols(lb_re), oct_cols(lb_im),
                            jnp.zeros((N_OCT, 6, OCT_STATES), F32)], axis=1)

    flat = lambda a: a.reshape(1, N_STATES)
    col = lambda a: a.reshape(N_COL, 1, LANES)
    a_tab = jnp.concatenate([col(a_re), col(a_im), col(s_re), col(s_im),
                             jnp.zeros((N_COL, 4, LANES), F32)], axis=1)
    lb = jnp.concatenate([flat(lb_re), flat(lb_im)], axis=0)
    return bblk, ctblk, lcol, a_tab, lb


def kernel(x_prompt, x_sample, state_k_win, state_v_win, state_ssm_re, state_ssm_im, norm1_g, w_in, b_gate, attn_sinks, ssm_lam_re, ssm_lam_im, ssm_log_dt, ssm_b_re, ssm_b_im, ssm_c_re, ssm_c_im, ssm_d, w_glu, b_glu, w_branch_attn, w_branch_ssm, w_out, norm2_g, w_ffn_gate, w_ffn_up, w_ffn_down, norm_f_g):
    depth = w_in.shape[0]
    assert depth == 1
    b, t, _ = x_prompt.shape
    nb, s_len, _ = x_sample.shape
    assert s_len == 1 and state_k_win.shape[2] == WINDOW
    l = 0
    o1 = ATTN_WIDTH + 2 * KV_WIDTH
    o2 = o1 + SSM_WIDTH
    wqkv, wu, wg = (w_in[l, :, :o1].astype(BF), w_in[l, :, o1:o2].astype(BF), w_in[l, :, o2:].astype(BF))
    g1 = norm1_g[l].reshape(1, D_MODEL)
    g2 = norm2_g[l].reshape(1, D_MODEL)
    gf = norm_f_g.reshape(1, D_MODEL)
    bg = b_gate[l].reshape(1, GATE_WIDTH)
    d = ssm_d[l].reshape(1, SSM_WIDTH)
    wglu = w_glu[l].astype(BF)
    bglu = b_glu[l].reshape(1, SSM_WIDTH)
    wba = w_branch_attn[l].astype(BF)
    wbs = w_branch_ssm[l].astype(BF)
    wo = w_out[l].astype(BF)
    wgate = w_ffn_gate[l].astype(BF)
    wup = w_ffn_up[l].astype(BF)
    wdown = w_ffn_down[l].astype(BF)
    sinks = attn_sinks[l]

    bblk, ctblk, lcol, a_tab, lb = _ssm_tables(
        ssm_lam_re[l], ssm_lam_im[l], ssm_log_dt[l], ssm_b_re[l], ssm_b_im[l], ssm_c_re[l], ssm_c_im[l])

    rc, rs1, rs2 = _rope_tables(np.arange(t))
    ssm_p, hre_p, him_p = _s5_prompt(x_prompt, g1, wu, bblk, ctblk, lcol, a_tab, d, wglu, bglu)
    x1_p, kwin_p, vwin_p = _mix_prompt(x_prompt, ssm_p, rc, rs1, rs2, sinks, g1, wqkv, wg, bg, wba, wbs, wo)
    y_p = _ffn(x1_p, g2, wgate, wup, wdown, gf, FFN_TB).reshape(b, t, D_MODEL)

    rcs, rs1s, rs2s = _rope_tables(PAST_LEN + np.arange(1))
    sinkc = jnp.concatenate([sinks, jnp.zeros((QROWS - N_Q_HEADS,), F32)]).reshape(QROWS, 1)
    wbad = jnp.concatenate([wba.reshape(N_Q_HEADS, HEAD_DIM, D_MODEL)] * 2, axis=1)
    xs = x_sample.reshape(nb, D_MODEL)
    o3, kwin_s, vwin_s = _sample_attn(
        xs, state_k_win[l].reshape(nb, WINDOW, KV_WIDTH), state_v_win[l].reshape(nb, WINDOW, KV_WIDTH),
        rcs, rs1s, rs2s, sinkc, g1, wqkv)
    x1_s, hre_s, him_s = _sample_tail(
        xs, o3, state_ssm_re[l].reshape(nb, N_STATES), state_ssm_im[l].reshape(nb, N_STATES),
        g1, wu, wg, bg, wbad, wbs, wo, lb, bblk.astype(BF), ctblk.astype(BF), d, wglu, bglu)
    y_s = _ffn(x1_s, g2, wgate, wup, wdown, gf, nb).reshape(nb, 1, D_MODEL)

    kv_shape_p = (1, b, WINDOW, N_KV_HEADS, HEAD_DIM)
    st_shape_p = (1, b, N_SSM_GROUPS, SSM_STATE)
    kv_shape_s = (1, nb, WINDOW, N_KV_HEADS, HEAD_DIM)
    st_shape_s = (1, nb, N_SSM_GROUPS, SSM_STATE)
    return (y_p, y_s,
            kwin_p.reshape(kv_shape_p), vwin_p.reshape(kv_shape_p),
            hre_p.reshape(st_shape_p), him_p.reshape(st_shape_p),
            kwin_s.reshape(kv_shape_s), vwin_s.reshape(kv_shape_s),
            hre_s.reshape(st_shape_s), him_s.reshape(st_shape_s))
```

```python
import math

import jax
import jax.numpy as jnp
import numpy as np
from jax import lax
from jax.experimental import pallas as pl
from jax.experimental.pallas import tpu as pltpu

D_MODEL = 1024
N_Q_HEADS = 8
N_KV_HEADS = 2
HEAD_DIM = 64
ATTN_WIDTH = N_Q_HEADS * HEAD_DIM
KV_WIDTH = N_KV_HEADS * HEAD_DIM
WINDOW = 128
ROPE_DIM = HEAD_DIM // 4
ROPE_HALF = ROPE_DIM // 2
ROPE_THETA = 500000.0
SSM_WIDTH = D_MODEL // 2
SSM_GROUP = 16
N_SSM_GROUPS = SSM_WIDTH // SSM_GROUP
SSM_STATE = 64
N_STATES = N_SSM_GROUPS * SSM_STATE
GATE_WIDTH = 2 * D_MODEL
D_FF = -(-8 * D_MODEL // (3 * 256)) * 256
NORM_EPS = 1e-5
PAST_LEN = 8192

LANES = 128
CHUNK = 8
OCT = LANES // SSM_GROUP
N_OCT = N_SSM_GROUPS // OCT
OCT_STATES = OCT * SSM_STATE
OCT_COL = OCT_STATES // LANES
N_COL = N_STATES // LANES
S5_ROWS = 128
SEG = 16
M_PAIR = 2
SEG_PITCH = SEG * CHUNK + 8
MIX_TB = 512
FFN_TB = 512
NEG_BIG = -1e30
VMEM_LIMIT = 56 * 1024 * 1024

BF = jnp.bfloat16
F32 = jnp.float32


def _dot(a, b):
    return jnp.dot(a, b, preferred_element_type=F32)


def _dot_nt(a, b):
    return lax.dot_general(a, b, (((1,), (1,)), ((), ())), preferred_element_type=F32)


def _dot_nt_split(a, b):
    a_hi, b_hi = a.astype(BF), b.astype(BF)
    a_lo = (a - a_hi.astype(F32)).astype(BF)
    b_lo = (b - b_hi.astype(F32)).astype(BF)
    return _dot_nt(a_hi, b_hi) + (_dot_nt(a_hi, b_lo) + _dot_nt(a_lo, b_hi))


def _rms(x, g):
    return x * lax.rsqrt(jnp.mean(x * x, axis=-1, keepdims=True) + NORM_EPS) * g


def _sigmoid(x):
    return 1.0 / (1.0 + jnp.exp(-x))


def _gelu_tanh(x):
    c = math.sqrt(2.0 / math.pi)
    return 0.5 * x * (1.0 + jnp.tanh(c * (x + 0.044715 * (x * x * x))))


def _rope(a, rc, rs1, rs2):
    return a * rc + pltpu.roll(a, ROPE_HALF, 1) * rs1 + pltpu.roll(a, LANES - ROPE_HALF, 1) * rs2


def _const_spec(shape):
    nd = len(shape)
    return pl.BlockSpec(shape, lambda *_: (0,) * nd, pipeline_mode=pl.Buffered(1))


def _build_chunk_operators(bblk_ref, ctblk_ref, lcol_ref, m_s, e_s, f_s):
    for o in range(N_OCT):
        ct = ctblk_ref[o]
        lr, li = lcol_ref[o, 0:1, :], lcol_ref[o, 1:2, :]
        er, ei = bblk_ref[o, :, :OCT_STATES], bblk_ref[o, :, OCT_STATES:]
        k_blk = []
        for tau in range(CHUNK):
            e_cat = jnp.concatenate([er, ei], axis=1)
            i = CHUNK - 1 - tau
            e_s[o, i * LANES:(i + 1) * LANES, :] = e_cat.astype(BF)
            k_blk.append(_dot_nt_split(e_cat, ct).astype(BF))
            er, ei = er * lr - ei * li, er * li + ei * lr
        zero = jnp.zeros((LANES, LANES), BF)
        for j in range(CHUNK):
            jt, jj = divmod(j, M_PAIR)
            for i in range(M_PAIR * (jt + 1)):
                m_s[jt][o, i * LANES:(i + 1) * LANES, jj * LANES:(jj + 1) * LANES] = (
                    k_blk[j - i] if j >= i else zero)
        tr, ti = ct[:, :OCT_STATES], -ct[:, OCT_STATES:]
        for j in range(CHUNK):
            tr, ti = tr * lr - ti * li, tr * li + ti * lr
            f_s[o, :OCT_STATES, j * LANES:(j + 1) * LANES] = tr.T.astype(BF)
            f_s[o, OCT_STATES:, j * LANES:(j + 1) * LANES] = (-ti).T.astype(BF)


def _s5_prompt_kernel(x_ref, g1_ref, wu_ref, bblk_ref, ctblk_ref, lcol_ref, a_ref, d_ref,
                      wglu_ref, bglu_ref,
                      out_ref, hfin_ref,
                      m0_ref, m1_ref, m2_ref, m3_ref, e_ref, f_ref, us, ys, sre, sim, car):
    m_ref = (m0_ref, m1_ref, m2_ref, m3_ref)
    blk = pl.program_id(1)

    @pl.when((pl.program_id(0) == 0) & (blk == 0))
    def _():
        _build_chunk_operators(bblk_ref, ctblk_ref, lcol_ref, m_ref, e_ref, f_ref)

    @pl.when(blk == 0)
    def _():
        car[...] = jnp.zeros_like(car)

    seg_tokens = SEG * CHUNK
    seg_rows = [slice(s * SEG_PITCH, s * SEG_PITCH + seg_tokens) for s in range(8)]
    tok_rows = [slice(s * seg_tokens, (s + 1) * seg_tokens) for s in range(8)]

    for s in range(0, 8, 2):
        hn = _rms(x_ref[s * seg_tokens:(s + 2) * seg_tokens, :], g1_ref[...]).astype(BF)
        u = _dot(hn, wu_ref[...])
        for half in range(2):
            for cc in range(N_OCT):
                us[cc, seg_rows[s + half], :] = u[tok_rows[half], cc * LANES:(cc + 1) * LANES]

    sub = lax.broadcasted_iota(jnp.int32, (8, LANES), 0)

    def scan_column(c):
        tab = a_ref[c]
        are, aim = tab[0:1], tab[1:2]
        bre, bim = tab[2:3], tab[3:4]

        def step(cr, ci, r, keep_entering):
            slab = slice(r * 8, (r + 1) * 8)
            s_r, s_i = sre[c, slab, :], sim[c, slab, :]
            if keep_entering:
                sre[c, slab, :] = cr
                sim[c, slab, :] = ci
            return are * cr - aim * ci + s_r, are * ci + aim * cr + s_i

        cr = jnp.zeros((8, LANES), F32)
        ci = jnp.zeros((8, LANES), F32)
        for r in range(SEG):
            cr, ci = step(cr, ci, r, False)
        cv = car[c]
        pr, pi = cv[0:1], cv[1:2]
        sr = jnp.zeros((8, LANES), F32)
        si = jnp.zeros((8, LANES), F32)
        for s in range(8):
            sr = jnp.where(sub == s, pr, sr)
            si = jnp.where(sub == s, pi, si)
            pr, pi = (bre * pr - bim * pi + cr[s:s + 1], bre * pi + bim * pr + ci[s:s + 1])
        end = jnp.where(sub == 0, pr, jnp.where(sub == 1, pi, 0.0))
        car[c] = end
        hfin_ref[c] = end
        cr, ci = sr, si
        for r in range(SEG):
            cr, ci = step(cr, ci, r, True)

    gate = bglu_ref[...]
    for o in range(N_OCT):
        uo = jnp.concatenate(
            [jnp.concatenate([us[o, pl.ds(r * CHUNK + i, 8, stride=SEG_PITCH), :] for r in range(SEG)],
                             axis=0).astype(BF) for i in range(CHUNK)], axis=1)
        s_end = _dot(uo, e_ref[o])
        cols = range(o * OCT_COL, (o + 1) * OCT_COL)
        for cc, c in enumerate(cols):
            sre[c] = s_end[:, cc * LANES:(cc + 1) * LANES]
            sim[c] = s_end[:, OCT_STATES + cc * LANES:OCT_STATES + (cc + 1) * LANES]
        y_in = jnp.concatenate(
            [_dot(uo[:, :(jt + 1) * M_PAIR * LANES], m_ref[jt][o]) for jt in range(CHUNK // M_PAIR)],
            axis=1)
        for c in cols:
            scan_column(c)
        hp = jnp.concatenate([sre[c] for c in cols] + [sim[c] for c in cols], axis=1).astype(BF)
        yo = y_in + _dot(hp, f_ref[o])
        for r in range(SEG):
            for j in range(CHUNK):
                ys[o, pl.ds(r * CHUNK + j, 8, stride=SEG_PITCH), :] = (
                    yo[r * 8:(r + 1) * 8, j * LANES:(j + 1) * LANES])
        d_o = d_ref[:, o * LANES:(o + 1) * LANES]
        for rs in seg_rows:
            ys[o, rs, :] = _gelu_tanh(ys[o, rs, :] + d_o * us[o, rs, :])
        if o % 2 == 1:
            z2 = jnp.concatenate(
                [jnp.concatenate([ys[o - 1, rs, :], ys[o, rs, :]], axis=1) for rs in seg_rows],
                axis=0).astype(BF)
            gate = gate + _dot(z2, wglu_ref[(o - 1) * LANES:(o + 1) * LANES, :])

    for s in range(8):
        z = jnp.concatenate([ys[o, seg_rows[s], :] for o in range(N_OCT)], axis=1)
        out_ref[tok_rows[s], :] = (z * _sigmoid(gate[tok_rows[s], :])).astype(out_ref.dtype)


def _s5_prompt(x, g1, wu, bblk, ctblk, lcol, a_tab, d, wglu, bglu):
    b, t, _ = x.shape
    tb = S5_ROWS * CHUNK
    nblk = t // tb
    x2 = x.reshape(b * t, D_MODEL)
    row_map = lambda i, j: (i * nblk + j, 0)
    op_shape = (N_OCT, CHUNK * LANES, CHUNK * LANES)
    out, hfin = pl.pallas_call(
        _s5_prompt_kernel,
        grid=(b, nblk),
        in_specs=[
            pl.BlockSpec((tb, D_MODEL), row_map),
            _const_spec((1, D_MODEL)),
            _const_spec((D_MODEL, SSM_WIDTH)),
            _const_spec(bblk.shape),
            _const_spec(ctblk.shape),
            _const_spec(lcol.shape),
            _const_spec(a_tab.shape),
            _const_spec((1, SSM_WIDTH)),
            _const_spec((SSM_WIDTH, SSM_WIDTH)),
            _const_spec((1, SSM_WIDTH)),
        ],
        out_specs=[
            pl.BlockSpec((tb, SSM_WIDTH), row_map),
            pl.BlockSpec((None, N_COL, 8, LANES), lambda i, j: (i, 0, 0, 0)),
        ],
        out_shape=[
            jax.ShapeDtypeStruct((b * t, SSM_WIDTH), BF),
            jax.ShapeDtypeStruct((b, N_COL, 8, LANES), F32),
        ],
        scratch_shapes=[
            *[pltpu.VMEM((N_OCT, (jt + 1) * M_PAIR * LANES, M_PAIR * LANES), BF)
              for jt in range(CHUNK // M_PAIR)],
            pltpu.VMEM(op_shape, BF), pltpu.VMEM(op_shape, BF),
            pltpu.VMEM((N_OCT, 8 * SEG_PITCH, LANES), F32), pltpu.VMEM((N_OCT, 8 * SEG_PITCH, LANES), F32),
            pltpu.VMEM((N_COL, S5_ROWS, LANES), F32), pltpu.VMEM((N_COL, S5_ROWS, LANES), F32),
            pltpu.VMEM((N_COL, 8, LANES), F32),
        ],
        compiler_params=pltpu.CompilerParams(
            dimension_semantics=("arbitrary", "arbitrary"), vmem_limit_bytes=VMEM_LIMIT),
        name="s5_prompt",
    )(x2, g1, wu, bblk, ctblk, lcol, a_tab, d, wglu, bglu)
    return out, hfin[:, :, 0, :], hfin[:, :, 1, :]


def _gate_cols(hn, wg_ref, bg_ref, cols):
    return _sigmoid(_dot(hn, wg_ref[:, cols]) + bg_ref[:, cols])


def _merge_out(x, gates, attn_proj, ssm_proj, wo_ref):
    merged = gates[:, :D_MODEL] * attn_proj + gates[:, D_MODEL:] * ssm_proj
    return x + _dot(merged.astype(BF), wo_ref[...])


def _half_split(a, ar, lo):
    z = jnp.zeros_like(a)
    return (jnp.where(lo, a, z).astype(BF), jnp.where(lo, z, ar).astype(BF),
            jnp.where(lo, ar, z).astype(BF), jnp.where(lo, z, a).astype(BF))


def _softmax_sink_t(st, bias_t, sink):
    st = st + bias_t
    m = jnp.maximum(jnp.max(st, axis=0, keepdims=True), sink)
    p = jnp.exp(st - m)
    den = jnp.sum(p, axis=0, keepdims=True) + jnp.exp(sink - m)
    return (p * (1.0 / den)).astype(BF)


def _mix_prompt_kernel(sinks_ref, x_ref, ssm_ref, rc_ref, rs1_ref, rs2_ref, g1_ref, wqkv_ref,
                       wg_ref, bg_ref, wba_ref, wbs_ref, wo_ref,
                       x1_ref, kwin_ref, vwin_ref, kprev, vprev):
    t = pl.program_id(1)

    @pl.when(t == 0)
    def _():
        kprev[...] = jnp.zeros_like(kprev)
        vprev[...] = jnp.zeros_like(vprev)

    x = x_ref[...]
    hn = _rms(x, g1_ref[...]).astype(BF)
    qkv = _dot(hn, wqkv_ref[...])
    rc, rs1, rs2 = rc_ref[...], rs1_ref[...], rs2_ref[...]
    scale = HEAD_DIM ** -0.5
    q = [(_rope(qkv[:, c * LANES:(c + 1) * LANES], rc, rs1, rs2) * scale).astype(BF)
         for c in range(ATTN_WIDTH // LANES)]
    k = _rope(qkv[:, ATTN_WIDTH:ATTN_WIDTH + KV_WIDTH], rc, rs1, rs2)
    v = qkv[:, ATTN_WIDTH + KV_WIDTH:ATTN_WIDTH + 2 * KV_WIDTH]

    w = WINDOW
    n_sub = MIX_TB // w
    lane = lax.broadcasted_iota(jnp.int32, (2 * w, LANES), 1)
    lo = lane < HEAD_DIM
    kj = lax.broadcasted_iota(jnp.int32, (2 * w, w), 0)
    qi = lax.broadcasted_iota(jnp.int32, (2 * w, w), 1)
    band = (kj > qi) & (kj <= qi + w)
    first = band & ((kj >= w) | (t > 0))
    bias_band = jnp.where(band, 0.0, NEG_BIG).astype(F32)
    bias_first = jnp.where(first, 0.0, NEG_BIG).astype(F32)
    col = lax.broadcasted_iota(jnp.int32, (1, 2 * w), 1)

    def sink_row(ha, hb):
        return jnp.where(col < w, sinks_ref[ha], sinks_ref[hb])

    ssm_proj = _dot(ssm_ref[...], wbs_ref[...])
    gate_w = GATE_WIDTH // (n_sub * N_KV_HEADS)
    gate_cols = []
    attn_rows = []
    for sb in range(n_sub):
        cur = slice(sb * w, (sb + 1) * w)
        if sb == 0:
            kcat = jnp.concatenate([kprev[...], k[cur]], axis=0)
            vcat = jnp.concatenate([vprev[...], v[cur]], axis=0)
            bias = bias_first
        else:
            kcat = k[(sb - 1) * w:(sb + 1) * w]
            vcat = v[(sb - 1) * w:(sb + 1) * w]
            bias = bias_band
        bias2 = jnp.concatenate([bias, bias], axis=1)
        ka, kb, kc, kd = _half_split(kcat, pltpu.roll(kcat, HEAD_DIM, 1), lo)
        vt = vcat.T
        zero = jnp.zeros((HEAD_DIM, 2 * w), F32)
        v_rows = [(jnp.concatenate([vt[g * HEAD_DIM:(g + 1) * HEAD_DIM], zero], axis=0).astype(BF),
                   jnp.concatenate([zero, vt[g * HEAD_DIM:(g + 1) * HEAD_DIM]], axis=0).astype(BF))
                  for g in range(N_KV_HEADS)]
        outs = []
        for grp, (k_e, k_o) in enumerate(((ka, kb), (kc, kd))):
            qq = jnp.concatenate([q[2 * grp][cur], q[2 * grp + 1][cur]], axis=0)
            s_e, s_o = _dot_nt(k_e, qq), _dot_nt(k_o, qq)
            gc = sb * N_KV_HEADS + grp
            gate_cols.append(_gate_cols(hn, wg_ref, bg_ref, slice(gc * gate_w, (gc + 1) * gate_w)))
            p_e = _softmax_sink_t(s_e, bias2, sink_row(4 * grp, 4 * grp + 2))
            p_o = _softmax_sink_t(s_o, bias2, sink_row(4 * grp + 1, 4 * grp + 3))
            v_e, v_o = v_rows[grp]
            o2 = _dot(v_e, p_e) + _dot(v_o, p_o)
            outs += [o2[:, :w].T, o2[:, w:].T]
        attn_rows.append(jnp.concatenate(outs, axis=1))
    attn = jnp.concatenate(attn_rows, axis=0).astype(BF)
    gates = jnp.concatenate(gate_cols, axis=1)

    kprev[...] = k[MIX_TB - w:]
    vprev[...] = v[MIX_TB - w:]
    kwin_ref[...] = k[MIX_TB - w:]
    vwin_ref[...] = v[MIX_TB - w:]
    x1_ref[...] = _merge_out(x, gates, _dot(attn, wba_ref[...]), ssm_proj, wo_ref)


def _mix_prompt(x, ssm, rc, rs1, rs2, sinks, g1, wqkv, wg, bg, wba, wbs, wo):
    b, t, _ = x.shape
    nblk = t // MIX_TB
    x2 = x.reshape(b * t, D_MODEL)
    row_map = lambda i, j: (i * nblk + j, 0)
    x1, kwin, vwin = pl.pallas_call(
        _mix_prompt_kernel,
        grid=(b, nblk),
        in_specs=[
            pl.BlockSpec(memory_space=pltpu.SMEM),
            pl.BlockSpec((MIX_TB, D_MODEL), row_map),
            pl.BlockSpec((MIX_TB, SSM_WIDTH), row_map),
            pl.BlockSpec((MIX_TB, LANES), lambda i, j: (j, 0)),
            pl.BlockSpec((MIX_TB, LANES), lambda i, j: (j, 0)),
            pl.BlockSpec((MIX_TB, LANES), lambda i, j: (j, 0)),
            _const_spec((1, D_MODEL)),
            _const_spec(wqkv.shape),
            _const_spec(wg.shape),
            _const_spec(bg.shape),
            _const_spec(wba.shape),
            _const_spec(wbs.shape),
            _const_spec(wo.shape),
        ],
        out_specs=[
            pl.BlockSpec((MIX_TB, D_MODEL), row_map),
            pl.BlockSpec((None, WINDOW, KV_WIDTH), lambda i, j: (i, 0, 0)),
            pl.BlockSpec((None, WINDOW, KV_WIDTH), lambda i, j: (i, 0, 0)),
        ],
        out_shape=[
            jax.ShapeDtypeStruct((b * t, D_MODEL), F32),
            jax.ShapeDtypeStruct((b, WINDOW, KV_WIDTH), F32),
            jax.ShapeDtypeStruct((b, WINDOW, KV_WIDTH), F32),
        ],
        scratch_shapes=[pltpu.VMEM((WINDOW, KV_WIDTH), F32), pltpu.VMEM((WINDOW, KV_WIDTH), F32)],
        compiler_params=pltpu.CompilerParams(
            dimension_semantics=("arbitrary", "arbitrary"), vmem_limit_bytes=VMEM_LIMIT),
        name="mix_prompt",
    )(sinks, x2, ssm, rc, rs1, rs2, g1, wqkv, wg, bg, wba, wbs, wo)
    return x1, kwin, vwin


QROWS = 16
SAMPLE_TB = 32


def _sample_attn_kernel(x_ref, kbuf_ref, vbuf_ref, rc_ref, rs1_ref, rs2_ref, sinkc_ref, g1_ref, wqkv_ref,
                        o_ref, kout_ref, vout_ref, qz, knew, vnew):
    nb = x_ref.shape[0]
    hn = _rms(x_ref[...], g1_ref[...]).astype(BF)
    qkv = _dot(hn, wqkv_ref[...])
    rc, rs1, rs2 = rc_ref[...], rs1_ref[...], rs2_ref[...]
    scale = HEAD_DIM ** -0.5
    knew[...] = _rope(qkv[:, ATTN_WIDTH:ATTN_WIDTH + KV_WIDTH], rc, rs1, rs2)
    vnew[...] = qkv[:, ATTN_WIDTH + KV_WIDTH:ATTN_WIDTH + 2 * KV_WIDTH]

    lane = lax.broadcasted_iota(jnp.int32, (nb, LANES), 1)
    lo = lane < HEAD_DIM
    qz[...] = jnp.zeros_like(qz)
    for c in range(ATTN_WIDTH // LANES):
        qc = _rope(qkv[:, c * LANES:(c + 1) * LANES], rc, rs1, rs2) * scale
        qr = pltpu.roll(qc, HEAD_DIM, 1)
        zero = jnp.zeros_like(qc)
        if c < 2:
            even, odd = jnp.where(lo, qc, zero), jnp.where(lo, qr, zero)
        else:
            even, odd = jnp.where(lo, zero, qr), jnp.where(lo, zero, qc)
        qz[pl.ds(2 * c, nb, stride=QROWS), :] = even
        qz[pl.ds(2 * c + 1, nb, stride=QROWS), :] = odd

    def shift(b, carry):
        kout_ref[b, 0:WINDOW - 1, :] = kbuf_ref[b, 1:WINDOW, :]
        vout_ref[b, 0:WINDOW - 1, :] = vbuf_ref[b, 1:WINDOW, :]
        kout_ref[b, WINDOW - 1:WINDOW, :] = knew[pl.ds(b, 1), :]
        vout_ref[b, WINDOW - 1:WINDOW, :] = vnew[pl.ds(b, 1), :]
        return carry

    lax.fori_loop(0, nb, shift, 0)

    sink = sinkc_ref[...]
    q3 = qz[...].reshape(nb, QROWS, LANES).astype(BF)
    s = jnp.einsum('bhd,bkd->bhk', q3, kout_ref[...].astype(BF), preferred_element_type=F32)
    m = jnp.maximum(jnp.max(s, axis=-1, keepdims=True), sink)
    p = jnp.exp(s - m)
    den = jnp.sum(p, axis=-1, keepdims=True) + jnp.exp(sink - m)
    p = (p * (1.0 / den)).astype(BF)
    o3 = jnp.einsum('bhk,bkd->bhd', p, vout_ref[...].astype(BF), preferred_element_type=F32)
    o_ref[...] = o3.reshape(nb * QROWS, LANES)


def _sample_attn(x, kbuf, vbuf, rc, rs1, rs2, sinkc, g1, wqkv):
    nb = x.shape[0]
    tb = SAMPLE_TB
    return pl.pallas_call(
        _sample_attn_kernel,
        grid=(nb // tb,),
        in_specs=[
            pl.BlockSpec((tb, D_MODEL), lambda i: (i, 0)),
            pl.BlockSpec((tb, WINDOW, KV_WIDTH), lambda i: (i, 0, 0)),
            pl.BlockSpec((tb, WINDOW, KV_WIDTH), lambda i: (i, 0, 0)),
            _const_spec(rc.shape), _const_spec(rs1.shape), _const_spec(rs2.shape),
            _const_spec(sinkc.shape), _const_spec(g1.shape), _const_spec(wqkv.shape),
        ],
        out_specs=[
            pl.BlockSpec((tb * QROWS, LANES), lambda i: (i, 0)),
            pl.BlockSpec((tb, WINDOW, KV_WIDTH), lambda i: (i, 0, 0)),
            pl.BlockSpec((tb, WINDOW, KV_WIDTH), lambda i: (i, 0, 0)),
        ],
        out_shape=[
            jax.ShapeDtypeStruct((nb * QROWS, LANES), F32),
            jax.ShapeDtypeStruct((nb, WINDOW, KV_WIDTH), F32),
            jax.ShapeDtypeStruct((nb, WINDOW, KV_WIDTH), F32),
        ],
        scratch_shapes=[pltpu.VMEM((tb * QROWS, LANES), F32),
                        pltpu.VMEM((tb, KV_WIDTH), F32), pltpu.VMEM((tb, KV_WIDTH), F32)],
        compiler_params=pltpu.CompilerParams(
            dimension_semantics=("arbitrary",), vmem_limit_bytes=VMEM_LIMIT),
        name="sample_attn",
    )(x, kbuf, vbuf, rc, rs1, rs2, sinkc, g1, wqkv)


def _sample_tail_kernel(x_ref, o3_ref, h0re_ref, h0im_ref, g1_ref, wu_ref, wg_ref, bg_ref, wbad_ref,
                        wbs_ref, wo_ref, lb_ref, bblk_ref, ctblk_ref, d_ref, wglu_ref, bglu_ref,
                        x1_ref, hre_ref, him_ref):
    nb = x_ref.shape[0]
    x = x_ref[...]
    hn = _rms(x, g1_ref[...]).astype(BF)

    lane = lax.broadcasted_iota(jnp.int32, (nb, LANES), 1)
    lo = lane < HEAD_DIM
    a = jnp.zeros((nb, D_MODEL), F32)
    zero = jnp.zeros((nb, LANES), F32)
    for h in range(N_Q_HEADS):
        oh = o3_ref[pl.ds(h, nb, stride=QROWS), :]
        oh = jnp.where(lo, oh, zero) if h < N_Q_HEADS // 2 else jnp.where(lo, zero, oh)
        a = a + _dot(oh.astype(BF), wbad_ref[h])

    u = _dot(hn, wu_ref[...])
    ub = u.astype(BF)
    lre, lim = lb_ref[0:1, :], lb_ref[1:2, :]
    y_cols = []
    for o in range(N_OCT):
        sl = slice(o * OCT_STATES, (o + 1) * OCT_STATES)
        bu = _dot(ub[:, o * LANES:(o + 1) * LANES], bblk_ref[o])
        h0r, h0i = h0re_ref[:, sl], h0im_ref[:, sl]
        hr = bu[:, :OCT_STATES] + (lre[:, sl] * h0r - lim[:, sl] * h0i)
        hi = bu[:, OCT_STATES:] + (lre[:, sl] * h0i + lim[:, sl] * h0r)
        hre_ref[:, sl] = hr
        him_ref[:, sl] = hi
        y_cols.append(_dot_nt(jnp.concatenate([hr, hi], axis=1).astype(BF), ctblk_ref[o]))
    y = jnp.concatenate(y_cols, axis=1) + d_ref[...] * u
    z = _gelu_tanh(y)
    gate = _dot(z.astype(BF), wglu_ref[...]) + bglu_ref[...]
    ssm = (z * _sigmoid(gate)).astype(BF)

    gates = _gate_cols(hn, wg_ref, bg_ref, slice(0, GATE_WIDTH))
    x1_ref[...] = _merge_out(x, gates, a, _dot(ssm, wbs_ref[...]), wo_ref)


def _sample_tail(x, o3, h0re, h0im, g1, wu, wg, bg, wbad, wbs, wo, lb, bblk, ctblk, d, wglu, bglu):
    nb = x.shape[0]
    args = (x, o3, h0re, h0im, g1, wu, wg, bg, wbad, wbs, wo, lb, bblk, ctblk, d, wglu, bglu)
    out_shapes = ((nb, D_MODEL), (nb, N_STATES), (nb, N_STATES))
    return pl.pallas_call(
        _sample_tail_kernel,
        grid=(1,),
        in_specs=[_const_spec(a.shape) for a in args],
        out_specs=[pl.BlockSpec(s, lambda i: (0, 0)) for s in out_shapes],
        out_shape=[jax.ShapeDtypeStruct(s, F32) for s in out_shapes],
        compiler_params=pltpu.CompilerParams(
            dimension_semantics=("arbitrary",), vmem_limit_bytes=VMEM_LIMIT),
        name="sample_tail",
    )(*args)


FFN_W_CHUNKS = 8


def _weight_copy(w_hbm, stage, sem, k, rows):
    slot = k % 2
    return pltpu.make_async_copy(w_hbm.at[pl.ds(k * rows, rows), :], stage.at[slot], sem.at[slot])


def _load_cast_weight(w_hbm, w_bf, stage, sem):
    rows = stage.shape[1]
    n = w_hbm.shape[0] // rows
    _weight_copy(w_hbm, stage, sem, 0, rows).start()
    for k in range(n):
        if k + 1 < n:
            _weight_copy(w_hbm, stage, sem, k + 1, rows).start()
        _weight_copy(w_hbm, stage, sem, k, rows).wait()
        w_bf[k * rows:(k + 1) * rows, :] = stage[k % 2].astype(BF)


def _ffn_rows(x, g2_ref, wgate, wup, wdown, gf_ref):
    h = _rms(x, g2_ref[...]).astype(BF)
    gate = _dot(h, wgate[...])
    up = _dot(h, wup[...])
    act = (gate * _sigmoid(gate) * up).astype(BF)
    x2 = x + _dot(act, wdown[...])
    return _rms(x2, gf_ref[...])


def _ffn_kernel(xp_ref, xs_ref, g2_ref, wgate_hbm, wup_hbm, wdown_hbm, gf_ref, yp_ref, ys_ref,
                wgate, wup, wdown, stage_in, stage_out, sem):
    i = pl.program_id(0)
    n_prompt = pl.num_programs(0) - 1

    @pl.when(i == 0)
    def _():
        _load_cast_weight(wgate_hbm, wgate, stage_in, sem)
        _load_cast_weight(wup_hbm, wup, stage_in, sem)
        _load_cast_weight(wdown_hbm, wdown, stage_out, sem)

    @pl.when(i < n_prompt)
    def _():
        yp_ref[...] = _ffn_rows(xp_ref[...], g2_ref, wgate, wup, wdown, gf_ref)

    @pl.when(i == n_prompt)
    def _():
        ys_ref[...] = _ffn_rows(xs_ref[...], g2_ref, wgate, wup, wdown, gf_ref)


def _ffn(xp, xs, g2, wgate, wup, wdown, gf):
    n, ns = xp.shape[0], xs.shape[0]
    n_prompt = n // FFN_TB
    prompt_map = lambda i: (jnp.minimum(i, n_prompt - 1), 0)
    return pl.pallas_call(
        _ffn_kernel,
        grid=(n_prompt + 1,),
        in_specs=[
            pl.BlockSpec((FFN_TB, D_MODEL), prompt_map),
            _const_spec((ns, D_MODEL)),
            _const_spec((1, D_MODEL)),
            pl.BlockSpec(memory_space=pl.ANY),
            pl.BlockSpec(memory_space=pl.ANY),
            pl.BlockSpec(memory_space=pl.ANY),
            _const_spec((1, D_MODEL)),
        ],
        out_specs=[
            pl.BlockSpec((FFN_TB, D_MODEL), prompt_map),
            pl.BlockSpec((ns, D_MODEL), lambda i: (0, 0)),
        ],
        out_shape=[
            jax.ShapeDtypeStruct((n, D_MODEL), F32),
            jax.ShapeDtypeStruct((ns, D_MODEL), F32),
        ],
        scratch_shapes=[
            pltpu.VMEM((D_MODEL, D_FF), BF), pltpu.VMEM((D_MODEL, D_FF), BF), pltpu.VMEM((D_FF, D_MODEL), BF),
            pltpu.VMEM((2, D_MODEL // FFN_W_CHUNKS, D_FF), F32),
            pltpu.VMEM((2, D_FF // FFN_W_CHUNKS, D_MODEL), F32),
            pltpu.SemaphoreType.DMA((2,)),
        ],
        compiler_params=pltpu.CompilerParams(
            dimension_semantics=("arbitrary",), vmem_limit_bytes=VMEM_LIMIT),
        name="ffn",
    )(xp, xs, g2, wgate, wup, wdown, gf)


def _rope_tables(pos):
    pos = np.asarray(pos, np.float64)
    inv_freq = ROPE_THETA ** (-(np.arange(ROPE_HALF, dtype=np.float64) * 2.0 / ROPE_DIM))
    ang = pos[:, None] * inv_freq[None, :]
    cos, sin = np.cos(ang), np.sin(ang)
    pad = np.zeros((pos.shape[0], HEAD_DIM - ROPE_DIM))
    zero = np.zeros_like(sin)
    rc = np.concatenate([cos, cos, pad + 1.0], axis=1)
    rs1 = np.concatenate([zero, sin, pad], axis=1)
    rs2 = np.concatenate([-sin, zero, pad], axis=1)
    rep = LANES // HEAD_DIM
    return tuple(jnp.asarray(np.tile(a, (1, rep)), F32) for a in (rc, rs1, rs2))


def _cmul(ar, ai, br, bi):
    return ar * br - ai * bi, ar * bi + ai * br


def _ssm_tables(lam_re, lam_im, log_dt, b_re, b_im, c_re, c_im):
    dt = jnp.exp(log_dt)[:, None]
    mag = jnp.exp(lam_re * dt)
    lb_re = mag * jnp.cos(lam_im * dt)
    lb_im = mag * jnp.sin(lam_im * dt)
    den = lam_re * lam_re + lam_im * lam_im
    nr = lb_re - 1.0
    k_re = ((nr * lam_re + lb_im * lam_im) / den)[..., None]
    k_im = ((lb_im * lam_re - nr * lam_im) / den)[..., None]
    bb_re = k_re * b_re - k_im * b_im
    bb_im = k_re * b_im + k_im * b_re

    a_re, a_im = lb_re, lb_im
    for _ in range(int(math.log2(CHUNK))):
        a_re, a_im = _cmul(a_re, a_im, a_re, a_im)
    s_re, s_im = a_re, a_im
    for _ in range(int(math.log2(SEG))):
        s_re, s_im = _cmul(s_re, s_im, s_re, s_im)

    eye = jnp.eye(OCT, dtype=F32).reshape(1, OCT, 1, OCT, 1)

    def block_diag(a):
        r, c = a.shape[1:]
        return (a.reshape(N_OCT, OCT, r, 1, c) * eye).reshape(N_OCT, OCT * r, OCT * c)

    bblk = jnp.concatenate([block_diag(jnp.swapaxes(bb_re, 1, 2)),
                            block_diag(jnp.swapaxes(bb_im, 1, 2))], axis=2)
    ctblk = jnp.concatenate([block_diag(c_re), block_diag(-c_im)], axis=2)

    oct_cols = lambda a: a.reshape(N_OCT, 1, OCT_STATES)
    lcol = jnp.concatenate([oct_cols(lb_re), oct_cols(lb_im),
                            jnp.zeros((N_OCT, 6, OCT_STATES), F32)], axis=1)

    flat = lambda a: a.reshape(1, N_STATES)
    col = lambda a: a.reshape(N_COL, 1, LANES)
    a_tab = jnp.concatenate([col(a_re), col(a_im), col(s_re), col(s_im),
                             jnp.zeros((N_COL, 4, LANES), F32)], axis=1)
    lb = jnp.concatenate([flat(lb_re), flat(lb_im)], axis=0)
    return bblk, ctblk, lcol, a_tab, lb


def kernel(x_prompt, x_sample, state_k_win, state_v_win, state_ssm_re, state_ssm_im, norm1_g, w_in, b_gate, attn_sinks, ssm_lam_re, ssm_lam_im, ssm_log_dt, ssm_b_re, ssm_b_im, ssm_c_re, ssm_c_im, ssm_d, w_glu, b_glu, w_branch_attn, w_branch_ssm, w_out, norm2_g, w_ffn_gate, w_ffn_up, w_ffn_down, norm_f_g):
    depth = w_in.shape[0]
    assert depth == 1
    b, t, _ = x_prompt.shape
    nb, s_len, _ = x_sample.shape
    assert s_len == 1 and state_k_win.shape[2] == WINDOW
    l = 0
    o1 = ATTN_WIDTH + 2 * KV_WIDTH
    o2 = o1 + SSM_WIDTH
    wqkv, wu, wg = (w_in[l, :, :o1].astype(BF), w_in[l, :, o1:o2].astype(BF), w_in[l, :, o2:].astype(BF))
    g1 = norm1_g[l].reshape(1, D_MODEL)
    g2 = norm2_g[l].reshape(1, D_MODEL)
    gf = norm_f_g.reshape(1, D_MODEL)
    bg = b_gate[l].reshape(1, GATE_WIDTH)
    d = ssm_d[l].reshape(1, SSM_WIDTH)
    wglu = w_glu[l].astype(BF)
    bglu = b_glu[l].reshape(1, SSM_WIDTH)
    wba = w_branch_attn[l].astype(BF)
    wbs = w_branch_ssm[l].astype(BF)
    wo = w_out[l].astype(BF)
    sinks = attn_sinks[l]

    bblk, ctblk, lcol, a_tab, lb = _ssm_tables(
        ssm_lam_re[l], ssm_lam_im[l], ssm_log_dt[l], ssm_b_re[l], ssm_b_im[l], ssm_c_re[l], ssm_c_im[l])

    rc, rs1, rs2 = _rope_tables(np.arange(t))
    ssm_p, hre_p, him_p = _s5_prompt(x_prompt, g1, wu, bblk, ctblk, lcol, a_tab, d, wglu, bglu)
    x1_p, kwin_p, vwin_p = _mix_prompt(x_prompt, ssm_p, rc, rs1, rs2, sinks, g1, wqkv, wg, bg, wba, wbs, wo)

    rcs, rs1s, rs2s = _rope_tables(PAST_LEN + np.arange(1))
    sinkc = jnp.concatenate([sinks, jnp.zeros((QROWS - N_Q_HEADS,), F32)]).reshape(QROWS, 1)
    wbad = jnp.concatenate([wba.reshape(N_Q_HEADS, HEAD_DIM, D_MODEL)] * 2, axis=1)
    xs = x_sample.reshape(nb, D_MODEL)
    o3, kwin_s, vwin_s = _sample_attn(
        xs, state_k_win[l].reshape(nb, WINDOW, KV_WIDTH), state_v_win[l].reshape(nb, WINDOW, KV_WIDTH),
        rcs, rs1s, rs2s, sinkc, g1, wqkv)
    x1_s, hre_s, him_s = _sample_tail(
        xs, o3, state_ssm_re[l].reshape(nb, N_STATES), state_ssm_im[l].reshape(nb, N_STATES),
        g1, wu, wg, bg, wbad, wbs, wo, lb, bblk.astype(BF), ctblk.astype(BF), d, wglu, bglu)
    y_p, y_s = _ffn(x1_p, x1_s, g2, w_ffn_gate[l], w_ffn_up[l], w_ffn_down[l], gf)
    y_p = y_p.reshape(b, t, D_MODEL)
    y_s = y_s.reshape(nb, 1, D_MODEL)

    kv_shape_p = (1, b, WINDOW, N_KV_HEADS, HEAD_DIM)
    st_shape_p = (1, b, N_SSM_GROUPS, SSM_STATE)
    kv_shape_s = (1, nb, WINDOW, N_KV_HEADS, HEAD_DIM)
    st_shape_s = (1, nb, N_SSM_GROUPS, SSM_STATE)
    return (y_p, y_s,
            kwin_p.reshape(kv_shape_p), vwin_p.reshape(kv_shape_p),
            hre_p.reshape(st_shape_p), him_p.reshape(st_shape_p),
            kwin_s.reshape(kv_shape_s), vwin_s.reshape(kv_shape_s),
            hre_s.reshape(st_shape_s), him_s.reshape(st_shape_s))
```

```python
import math

import jax
import jax.numpy as jnp
import numpy as np
from jax import lax
from jax.experimental import pallas as pl
from jax.experimental.pallas import tpu as pltpu

D_MODEL = 1024
N_Q_HEADS = 8
N_KV_HEADS = 2
HEAD_DIM = 64
ATTN_WIDTH = N_Q_HEADS * HEAD_DIM
KV_WIDTH = N_KV_HEADS * HEAD_DIM
WINDOW = 128
ROPE_DIM = HEAD_DIM // 4
ROPE_HALF = ROPE_DIM // 2
ROPE_THETA = 500000.0
SSM_WIDTH = D_MODEL // 2
SSM_GROUP = 16
N_SSM_GROUPS = SSM_WIDTH // SSM_GROUP
SSM_STATE = 64
N_STATES = N_SSM_GROUPS * SSM_STATE
GATE_WIDTH = 2 * D_MODEL
D_FF = -(-8 * D_MODEL // (3 * 256)) * 256
NORM_EPS = 1e-5
PAST_LEN = 8192

LANES = 128
CHUNK = 8
OCT = LANES // SSM_GROUP
N_OCT = N_SSM_GROUPS // OCT
OCT_STATES = OCT * SSM_STATE
OCT_COL = OCT_STATES // LANES
N_COL = N_STATES // LANES
S5_ROWS = 128
SEG = 16
M_PAIR = 2
SEG_PITCH = SEG * CHUNK + 8
MIX_TB = 512
FFN_TB = 512
NEG_BIG = -1e30
VMEM_LIMIT = 56 * 1024 * 1024

BF = jnp.bfloat16
F32 = jnp.float32


def _dot(a, b):
    return jnp.dot(a, b, preferred_element_type=F32)


def _dot_nt(a, b):
    return lax.dot_general(a, b, (((1,), (1,)), ((), ())), preferred_element_type=F32)


def _dot_nt_split(a, b):
    a_hi, b_hi = a.astype(BF), b.astype(BF)
    a_lo = (a - a_hi.astype(F32)).astype(BF)
    b_lo = (b - b_hi.astype(F32)).astype(BF)
    return _dot_nt(a_hi, b_hi) + (_dot_nt(a_hi, b_lo) + _dot_nt(a_lo, b_hi))


def _rms(x, g):
    return x * lax.rsqrt(jnp.mean(x * x, axis=-1, keepdims=True) + NORM_EPS) * g


def _sigmoid(x):
    return 1.0 / (1.0 + jnp.exp(-x))


def _gelu_tanh(x):
    c = math.sqrt(2.0 / math.pi)
    return 0.5 * x * (1.0 + jnp.tanh(c * (x + 0.044715 * (x * x * x))))


def _rope(a, rc, rs1, rs2):
    return a * rc + pltpu.roll(a, ROPE_HALF, 1) * rs1 + pltpu.roll(a, LANES - ROPE_HALF, 1) * rs2


def _const_spec(shape):
    nd = len(shape)
    return pl.BlockSpec(shape, lambda *_: (0,) * nd, pipeline_mode=pl.Buffered(1))


def _build_chunk_operators(bblk_ref, ctblk_ref, lcol_ref, m_s, e_s, f_s):
    for o in range(N_OCT):
        ct = ctblk_ref[o]
        lr, li = lcol_ref[o, 0:1, :], lcol_ref[o, 1:2, :]
        er, ei = bblk_ref[o, :, :OCT_STATES], bblk_ref[o, :, OCT_STATES:]
        k_blk = []
        for tau in range(CHUNK):
            e_cat = jnp.concatenate([er, ei], axis=1)
            i = CHUNK - 1 - tau
            e_s[o, i * LANES:(i + 1) * LANES, :] = e_cat.astype(BF)
            k_blk.append(_dot_nt_split(e_cat, ct).astype(BF))
            er, ei = er * lr - ei * li, er * li + ei * lr
        zero = jnp.zeros((LANES, LANES), BF)
        for j in range(CHUNK):
            jt, jj = divmod(j, M_PAIR)
            for i in range(M_PAIR * (jt + 1)):
                m_s[jt][o, i * LANES:(i + 1) * LANES, jj * LANES:(jj + 1) * LANES] = (
                    k_blk[j - i] if j >= i else zero)
        tr, ti = ct[:, :OCT_STATES], -ct[:, OCT_STATES:]
        for j in range(CHUNK):
            tr, ti = tr * lr - ti * li, tr * li + ti * lr
            f_s[o, :OCT_STATES, j * LANES:(j + 1) * LANES] = tr.T.astype(BF)
            f_s[o, OCT_STATES:, j * LANES:(j + 1) * LANES] = (-ti).T.astype(BF)


def _s5_prompt_kernel(x_ref, g1_ref, wu_ref, bblk_ref, ctblk_ref, lcol_ref, a_ref, d_ref,
                      wglu_ref, bglu_ref,
                      out_ref, hfin_ref,
                      m0_ref, m1_ref, m2_ref, m3_ref, e_ref, f_ref, us, ys, sre, sim, car):
    m_ref = (m0_ref, m1_ref, m2_ref, m3_ref)
    blk = pl.program_id(1)

    @pl.when((pl.program_id(0) == 0) & (blk == 0))
    def _():
        _build_chunk_operators(bblk_ref, ctblk_ref, lcol_ref, m_ref, e_ref, f_ref)

    @pl.when(blk == 0)
    def _():
        car[...] = jnp.zeros_like(car)

    seg_tokens = SEG * CHUNK
    seg_rows = [slice(s * SEG_PITCH, s * SEG_PITCH + seg_tokens) for s in range(8)]
    tok_rows = [slice(s * seg_tokens, (s + 1) * seg_tokens) for s in range(8)]

    for s in range(0, 8, 2):
        hn = _rms(x_ref[s * seg_tokens:(s + 2) * seg_tokens, :], g1_ref[...]).astype(BF)
        u = _dot(hn, wu_ref[...])
        for half in range(2):
            for cc in range(N_OCT):
                us[cc, seg_rows[s + half], :] = u[tok_rows[half], cc * LANES:(cc + 1) * LANES]

    sub = lax.broadcasted_iota(jnp.int32, (8, LANES), 0)

    def scan_column(c):
        tab = a_ref[c]
        are, aim = tab[0:1], tab[1:2]
        bre, bim = tab[2:3], tab[3:4]

        def step(cr, ci, r, keep_entering):
            slab = slice(r * 8, (r + 1) * 8)
            s_r, s_i = sre[c, slab, :], sim[c, slab, :]
            if keep_entering:
                sre[c, slab, :] = cr
                sim[c, slab, :] = ci
            return are * cr - aim * ci + s_r, are * ci + aim * cr + s_i

        cr = jnp.zeros((8, LANES), F32)
        ci = jnp.zeros((8, LANES), F32)
        for r in range(SEG):
            cr, ci = step(cr, ci, r, False)
        cv = car[c]
        pr, pi = cv[0:1], cv[1:2]
        sr = jnp.zeros((8, LANES), F32)
        si = jnp.zeros((8, LANES), F32)
        for s in range(8):
            sr = jnp.where(sub == s, pr, sr)
            si = jnp.where(sub == s, pi, si)
            pr, pi = (bre * pr - bim * pi + cr[s:s + 1], bre * pi + bim * pr + ci[s:s + 1])
        end = jnp.where(sub == 0, pr, jnp.where(sub == 1, pi, 0.0))
        car[c] = end
        hfin_ref[c] = end
        cr, ci = sr, si
        for r in range(SEG):
            cr, ci = step(cr, ci, r, True)

    gate = bglu_ref[...]
    for o in range(N_OCT):
        uo = jnp.concatenate(
            [jnp.concatenate([us[o, pl.ds(r * CHUNK + i, 8, stride=SEG_PITCH), :] for r in range(SEG)],
                             axis=0).astype(BF) for i in range(CHUNK)], axis=1)
        s_end = _dot(uo, e_ref[o])
        cols = range(o * OCT_COL, (o + 1) * OCT_COL)
        for cc, c in enumerate(cols):
            sre[c] = s_end[:, cc * LANES:(cc + 1) * LANES]
            sim[c] = s_end[:, OCT_STATES + cc * LANES:OCT_STATES + (cc + 1) * LANES]
        y_in = jnp.concatenate(
            [_dot(uo[:, :(jt + 1) * M_PAIR * LANES], m_ref[jt][o]) for jt in range(CHUNK // M_PAIR)],
            axis=1)
        for c in cols:
            scan_column(c)
        hp = jnp.concatenate([sre[c] for c in cols] + [sim[c] for c in cols], axis=1).astype(BF)
        yo = y_in + _dot(hp, f_ref[o])
        for r in range(SEG):
            for j in range(CHUNK):
                ys[o, pl.ds(r * CHUNK + j, 8, stride=SEG_PITCH), :] = (
                    yo[r * 8:(r + 1) * 8, j * LANES:(j + 1) * LANES])
        d_o = d_ref[:, o * LANES:(o + 1) * LANES]
        for rs in seg_rows:
            ys[o, rs, :] = _gelu_tanh(ys[o, rs, :] + d_o * us[o, rs, :])
        if o % 2 == 1:
            z2 = jnp.concatenate(
                [jnp.concatenate([ys[o - 1, rs, :], ys[o, rs, :]], axis=1) for rs in seg_rows],
                axis=0).astype(BF)
            gate = gate + _dot(z2, wglu_ref[(o - 1) * LANES:(o + 1) * LANES, :])

    for s in range(8):
        z = jnp.concatenate([ys[o, seg_rows[s], :] for o in range(N_OCT)], axis=1)
        out_ref[tok_rows[s], :] = (z * _sigmoid(gate[tok_rows[s], :])).astype(out_ref.dtype)


def _s5_prompt(x, g1, wu, bblk, ctblk, lcol, a_tab, d, wglu, bglu):
    b, t, _ = x.shape
    tb = S5_ROWS * CHUNK
    nblk = t // tb
    x2 = x.reshape(b * t, D_MODEL)
    row_map = lambda i, j: (i * nblk + j, 0)
    op_shape = (N_OCT, CHUNK * LANES, CHUNK * LANES)
    out, hfin = pl.pallas_call(
        _s5_prompt_kernel,
        grid=(b, nblk),
        in_specs=[
            pl.BlockSpec((tb, D_MODEL), row_map),
            _const_spec((1, D_MODEL)),
            _const_spec((D_MODEL, SSM_WIDTH)),
            _const_spec(bblk.shape),
            _const_spec(ctblk.shape),
            _const_spec(lcol.shape),
            _const_spec(a_tab.shape),
            _const_spec((1, SSM_WIDTH)),
            _const_spec((SSM_WIDTH, SSM_WIDTH)),
            _const_spec((1, SSM_WIDTH)),
        ],
        out_specs=[
            pl.BlockSpec((tb, SSM_WIDTH), row_map),
            pl.BlockSpec((None, N_COL, 8, LANES), lambda i, j: (i, 0, 0, 0)),
        ],
        out_shape=[
            jax.ShapeDtypeStruct((b * t, SSM_WIDTH), BF),
            jax.ShapeDtypeStruct((b, N_COL, 8, LANES), F32),
        ],
        scratch_shapes=[
            *[pltpu.VMEM((N_OCT, (jt + 1) * M_PAIR * LANES, M_PAIR * LANES), BF)
              for jt in range(CHUNK // M_PAIR)],
            pltpu.VMEM(op_shape, BF), pltpu.VMEM(op_shape, BF),
            pltpu.VMEM((N_OCT, 8 * SEG_PITCH, LANES), F32), pltpu.VMEM((N_OCT, 8 * SEG_PITCH, LANES), F32),
            pltpu.VMEM((N_COL, S5_ROWS, LANES), F32), pltpu.VMEM((N_COL, S5_ROWS, LANES), F32),
            pltpu.VMEM((N_COL, 8, LANES), F32),
        ],
        compiler_params=pltpu.CompilerParams(
            dimension_semantics=("arbitrary", "arbitrary"), vmem_limit_bytes=VMEM_LIMIT),
        name="s5_prompt",
    )(x2, g1, wu, bblk, ctblk, lcol, a_tab, d, wglu, bglu)
    return out, hfin[:, :, 0, :], hfin[:, :, 1, :]


def _gate_cols(hn, wg_ref, bg_ref, cols):
    return _sigmoid(_dot(hn, wg_ref[:, cols]) + bg_ref[:, cols])


def _merge_out(x, gates, attn_proj, ssm_proj, wo_ref):
    merged = gates[:, :D_MODEL] * attn_proj + gates[:, D_MODEL:] * ssm_proj
    return x + _dot(merged.astype(BF), wo_ref[...])


def _half_split(a, ar, lo):
    z = jnp.zeros_like(a)
    return (jnp.where(lo, a, z).astype(BF), jnp.where(lo, z, ar).astype(BF),
            jnp.where(lo, ar, z).astype(BF), jnp.where(lo, z, a).astype(BF))


def _softmax_sink_t(st, bias_t, sink):
    st = st + bias_t
    m = jnp.maximum(jnp.max(st, axis=0, keepdims=True), sink)
    p = jnp.exp(st - m)
    den = jnp.sum(p, axis=0, keepdims=True) + jnp.exp(sink - m)
    return (p * (1.0 / den)).astype(BF)


def _mix_prompt_kernel(sinks_ref, x_ref, ssm_ref, rc_ref, rs1_ref, rs2_ref, g1_ref, wqkv_ref,
                       wg_ref, bg_ref, wba_ref, wbs_ref, wo_ref,
                       x1_ref, kwin_ref, vwin_ref, kprev, vprev):
    t = pl.program_id(1)

    @pl.when(t == 0)
    def _():
        kprev[...] = jnp.zeros_like(kprev)
        vprev[...] = jnp.zeros_like(vprev)

    x = x_ref[...]
    hn = _rms(x, g1_ref[...]).astype(BF)
    qkv = _dot(hn, wqkv_ref[...])
    rc, rs1, rs2 = rc_ref[...], rs1_ref[...], rs2_ref[...]
    scale = HEAD_DIM ** -0.5
    q = [(_rope(qkv[:, c * LANES:(c + 1) * LANES], rc, rs1, rs2) * scale).astype(BF)
         for c in range(ATTN_WIDTH // LANES)]
    k = _rope(qkv[:, ATTN_WIDTH:ATTN_WIDTH + KV_WIDTH], rc, rs1, rs2)
    v = qkv[:, ATTN_WIDTH + KV_WIDTH:ATTN_WIDTH + 2 * KV_WIDTH]

    w = WINDOW
    n_sub = MIX_TB // w
    lane = lax.broadcasted_iota(jnp.int32, (2 * w, LANES), 1)
    lo = lane < HEAD_DIM
    kj = lax.broadcasted_iota(jnp.int32, (2 * w, w), 0)
    qi = lax.broadcasted_iota(jnp.int32, (2 * w, w), 1)
    band = (kj > qi) & (kj <= qi + w)
    first = band & ((kj >= w) | (t > 0))
    bias_band = jnp.where(band, 0.0, NEG_BIG).astype(F32)
    bias_first = jnp.where(first, 0.0, NEG_BIG).astype(F32)
    col = lax.broadcasted_iota(jnp.int32, (1, 2 * w), 1)

    def sink_row(ha, hb):
        return jnp.where(col < w, sinks_ref[ha], sinks_ref[hb])

    ssm_proj = _dot(ssm_ref[...], wbs_ref[...])
    gate_w = GATE_WIDTH // (n_sub * N_KV_HEADS)
    gate_cols = []
    attn_rows = []
    for sb in range(n_sub):
        cur = slice(sb * w, (sb + 1) * w)
        if sb == 0:
            kcat = jnp.concatenate([kprev[...], k[cur]], axis=0)
            vcat = jnp.concatenate([vprev[...], v[cur]], axis=0)
            bias = bias_first
        else:
            kcat = k[(sb - 1) * w:(sb + 1) * w]
            vcat = v[(sb - 1) * w:(sb + 1) * w]
            bias = bias_band
        bias2 = jnp.concatenate([bias, bias], axis=1)
        ka, kb, kc, kd = _half_split(kcat, pltpu.roll(kcat, HEAD_DIM, 1), lo)
        vt = vcat.T
        zero = jnp.zeros((HEAD_DIM, 2 * w), F32)
        v_rows = [(jnp.concatenate([vt[g * HEAD_DIM:(g + 1) * HEAD_DIM], zero], axis=0).astype(BF),
                   jnp.concatenate([zero, vt[g * HEAD_DIM:(g + 1) * HEAD_DIM]], axis=0).astype(BF))
                  for g in range(N_KV_HEADS)]
        outs = []
        for grp, (k_e, k_o) in enumerate(((ka, kb), (kc, kd))):
            qq = jnp.concatenate([q[2 * grp][cur], q[2 * grp + 1][cur]], axis=0)
            s_e, s_o = _dot_nt(k_e, qq), _dot_nt(k_o, qq)
            gc = sb * N_KV_HEADS + grp
            gate_cols.append(_gate_cols(hn, wg_ref, bg_ref, slice(gc * gate_w, (gc + 1) * gate_w)))
            p_e = _softmax_sink_t(s_e, bias2, sink_row(4 * grp, 4 * grp + 2))
            p_o = _softmax_sink_t(s_o, bias2, sink_row(4 * grp + 1, 4 * grp + 3))
            v_e, v_o = v_rows[grp]
            o2 = _dot(v_e, p_e) + _dot(v_o, p_o)
            outs += [o2[:, :w].T, o2[:, w:].T]
        attn_rows.append(jnp.concatenate(outs, axis=1))
    attn = jnp.concatenate(attn_rows, axis=0).astype(BF)
    gates = jnp.concatenate(gate_cols, axis=1)

    kprev[...] = k[MIX_TB - w:]
    vprev[...] = v[MIX_TB - w:]
    kwin_ref[...] = k[MIX_TB - w:]
    vwin_ref[...] = v[MIX_TB - w:]
    x1_ref[...] = _merge_out(x, gates, _dot(attn, wba_ref[...]), ssm_proj, wo_ref)


def _mix_prompt(x, ssm, rc, rs1, rs2, sinks, g1, wqkv, wg, bg, wba, wbs, wo):
    b, t, _ = x.shape
    nblk = t // MIX_TB
    x2 = x.reshape(b * t, D_MODEL)
    row_map = lambda i, j: (i * nblk + j, 0)
    x1, kwin, vwin = pl.pallas_call(
        _mix_prompt_kernel,
        grid=(b, nblk),
        in_specs=[
            pl.BlockSpec(memory_space=pltpu.SMEM),
            pl.BlockSpec((MIX_TB, D_MODEL), row_map),
            pl.BlockSpec((MIX_TB, SSM_WIDTH), row_map),
            pl.BlockSpec((MIX_TB, LANES), lambda i, j: (j, 0)),
            pl.BlockSpec((MIX_TB, LANES), lambda i, j: (j, 0)),
            pl.BlockSpec((MIX_TB, LANES), lambda i, j: (j, 0)),
            _const_spec((1, D_MODEL)),
            _const_spec(wqkv.shape),
            _const_spec(wg.shape),
            _const_spec(bg.shape),
            _const_spec(wba.shape),
            _const_spec(wbs.shape),
            _const_spec(wo.shape),
        ],
        out_specs=[
            pl.BlockSpec((MIX_TB, D_MODEL), row_map),
            pl.BlockSpec((None, WINDOW, KV_WIDTH), lambda i, j: (i, 0, 0)),
            pl.BlockSpec((None, WINDOW, KV_WIDTH), lambda i, j: (i, 0, 0)),
        ],
        out_shape=[
            jax.ShapeDtypeStruct((b * t, D_MODEL), F32),
            jax.ShapeDtypeStruct((b, WINDOW, KV_WIDTH), F32),
            jax.ShapeDtypeStruct((b, WINDOW, KV_WIDTH), F32),
        ],
        scratch_shapes=[pltpu.VMEM((WINDOW, KV_WIDTH), F32), pltpu.VMEM((WINDOW, KV_WIDTH), F32)],
        compiler_params=pltpu.CompilerParams(
            dimension_semantics=("arbitrary", "arbitrary"), vmem_limit_bytes=VMEM_LIMIT),
        name="mix_prompt",
    )(sinks, x2, ssm, rc, rs1, rs2, g1, wqkv, wg, bg, wba, wbs, wo)
    return x1, kwin, vwin


QROWS = 16
SAMPLE_TB = 32


def _sample_attn_kernel(x_ref, kbuf_ref, vbuf_ref, rc_ref, rs1_ref, rs2_ref, sinkc_ref, g1_ref, wqkv_ref,
                        o_ref, kout_ref, vout_ref, qz):
    nb = x_ref.shape[0]
    hn = _rms(x_ref[...], g1_ref[...]).astype(BF)
    qkv = _dot(hn, wqkv_ref[...])
    rc, rs1, rs2 = rc_ref[...], rs1_ref[...], rs2_ref[...]
    scale = HEAD_DIM ** -0.5
    k_new = _rope(qkv[:, ATTN_WIDTH:ATTN_WIDTH + KV_WIDTH], rc, rs1, rs2)
    v_new = qkv[:, ATTN_WIDTH + KV_WIDTH:ATTN_WIDTH + 2 * KV_WIDTH]
    pad = jnp.zeros((LANES - nb, KV_WIDTH), F32)
    k_new_t = jnp.concatenate([k_new, pad], axis=0).T
    v_new_t = jnp.concatenate([v_new, pad], axis=0).T

    lane = lax.broadcasted_iota(jnp.int32, (nb, LANES), 1)
    lo = lane < HEAD_DIM
    qz[...] = jnp.zeros_like(qz)
    for c in range(ATTN_WIDTH // LANES):
        qc = _rope(qkv[:, c * LANES:(c + 1) * LANES], rc, rs1, rs2) * scale
        qr = pltpu.roll(qc, HEAD_DIM, 1)
        zero = jnp.zeros_like(qc)
        if c < 2:
            even, odd = jnp.where(lo, qc, zero), jnp.where(lo, qr, zero)
        else:
            even, odd = jnp.where(lo, zero, qr), jnp.where(lo, zero, qc)
        qz[pl.ds(2 * c, nb, stride=QROWS), :] = even
        qz[pl.ds(2 * c + 1, nb, stride=QROWS), :] = odd

    last = lax.broadcasted_iota(jnp.int32, (KV_WIDTH, WINDOW), 1) == WINDOW - 1
    for b in range(nb):
        kout_ref[b] = jnp.where(last, k_new_t[:, b:b + 1], pltpu.roll(kbuf_ref[b], WINDOW - 1, 1))
        vout_ref[b] = jnp.where(last, v_new_t[:, b:b + 1], pltpu.roll(vbuf_ref[b], WINDOW - 1, 1))

    sink = sinkc_ref[...]
    q3 = qz[...].reshape(nb, QROWS, LANES).astype(BF)
    s = jnp.einsum('bhd,bdk->bhk', q3, kout_ref[...].astype(BF), preferred_element_type=F32)
    m = jnp.maximum(jnp.max(s, axis=-1, keepdims=True), sink)
    p = jnp.exp(s - m)
    den = jnp.sum(p, axis=-1, keepdims=True) + jnp.exp(sink - m)
    p = (p * (1.0 / den)).astype(BF)
    o3 = jnp.einsum('bhk,bdk->bhd', p, vout_ref[...].astype(BF), preferred_element_type=F32)
    o_ref[...] = o3.reshape(nb * QROWS, LANES)


def _sample_attn(x, kbuf, vbuf, rc, rs1, rs2, sinkc, g1, wqkv):
    nb = x.shape[0]
    tb = SAMPLE_TB
    return pl.pallas_call(
        _sample_attn_kernel,
        grid=(nb // tb,),
        in_specs=[
            pl.BlockSpec((tb, D_MODEL), lambda i: (i, 0)),
            pl.BlockSpec((tb, KV_WIDTH, WINDOW), lambda i: (i, 0, 0)),
            pl.BlockSpec((tb, KV_WIDTH, WINDOW), lambda i: (i, 0, 0)),
            _const_spec(rc.shape), _const_spec(rs1.shape), _const_spec(rs2.shape),
            _const_spec(sinkc.shape), _const_spec(g1.shape), _const_spec(wqkv.shape),
        ],
        out_specs=[
            pl.BlockSpec((tb * QROWS, LANES), lambda i: (i, 0)),
            pl.BlockSpec((tb, KV_WIDTH, WINDOW), lambda i: (i, 0, 0)),
            pl.BlockSpec((tb, KV_WIDTH, WINDOW), lambda i: (i, 0, 0)),
        ],
        out_shape=[
            jax.ShapeDtypeStruct((nb * QROWS, LANES), F32),
            jax.ShapeDtypeStruct((nb, KV_WIDTH, WINDOW), F32),
            jax.ShapeDtypeStruct((nb, KV_WIDTH, WINDOW), F32),
        ],
        scratch_shapes=[pltpu.VMEM((tb * QROWS, LANES), F32)],
        compiler_params=pltpu.CompilerParams(
            dimension_semantics=("arbitrary",), vmem_limit_bytes=VMEM_LIMIT),
        name="sample_attn",
    )(x, kbuf, vbuf, rc, rs1, rs2, sinkc, g1, wqkv)


def _sample_tail_kernel(x_ref, o3_ref, h0re_ref, h0im_ref, g1_ref, wu_ref, wg_ref, bg_ref, wbad_ref,
                        wbs_ref, wo_ref, lb_ref, bblk_ref, ctblk_ref, d_ref, wglu_ref, bglu_ref,
                        x1_ref, hre_ref, him_ref):
    nb = x_ref.shape[0]
    x = x_ref[...]
    hn = _rms(x, g1_ref[...]).astype(BF)

    lane = lax.broadcasted_iota(jnp.int32, (nb, LANES), 1)
    lo = lane < HEAD_DIM
    a = jnp.zeros((nb, D_MODEL), F32)
    zero = jnp.zeros((nb, LANES), F32)
    for h in range(N_Q_HEADS):
        oh = o3_ref[pl.ds(h, nb, stride=QROWS), :]
        oh = jnp.where(lo, oh, zero) if h < N_Q_HEADS // 2 else jnp.where(lo, zero, oh)
        a = a + _dot(oh.astype(BF), wbad_ref[h])

    u = _dot(hn, wu_ref[...])
    ub = u.astype(BF)
    lre, lim = lb_ref[0:1, :], lb_ref[1:2, :]
    y_cols = []
    for o in range(N_OCT):
        sl = slice(o * OCT_STATES, (o + 1) * OCT_STATES)
        bu = _dot(ub[:, o * LANES:(o + 1) * LANES], bblk_ref[o])
        blocks = [slice(c * LANES, (c + 1) * LANES) for c in range(o * OCT_COL, (o + 1) * OCT_COL)]
        h0r = jnp.concatenate([h0re_ref[rows, :].T for rows in blocks], axis=1)
        h0i = jnp.concatenate([h0im_ref[rows, :].T for rows in blocks], axis=1)
        hr = bu[:, :OCT_STATES] + (lre[:, sl] * h0r - lim[:, sl] * h0i)
        hi = bu[:, OCT_STATES:] + (lre[:, sl] * h0i + lim[:, sl] * h0r)
        for cc, rows in enumerate(blocks):
            hre_ref[rows, :] = hr[:, cc * LANES:(cc + 1) * LANES].T
            him_ref[rows, :] = hi[:, cc * LANES:(cc + 1) * LANES].T
        y_cols.append(_dot_nt(jnp.concatenate([hr, hi], axis=1).astype(BF), ctblk_ref[o]))
    y = jnp.concatenate(y_cols, axis=1) + d_ref[...] * u
    z = _gelu_tanh(y)
    gate = _dot(z.astype(BF), wglu_ref[...]) + bglu_ref[...]
    ssm = (z * _sigmoid(gate)).astype(BF)

    gates = _gate_cols(hn, wg_ref, bg_ref, slice(0, GATE_WIDTH))
    x1_ref[...] = _merge_out(x, gates, a, _dot(ssm, wbs_ref[...]), wo_ref)


def _sample_tail(x, o3, h0re, h0im, g1, wu, wg, bg, wbad, wbs, wo, lb, bblk, ctblk, d, wglu, bglu):
    nb = x.shape[0]
    args = (x, o3, h0re, h0im, g1, wu, wg, bg, wbad, wbs, wo, lb, bblk, ctblk, d, wglu, bglu)
    out_shapes = ((nb, D_MODEL), (N_STATES, nb), (N_STATES, nb))
    return pl.pallas_call(
        _sample_tail_kernel,
        grid=(1,),
        in_specs=[_const_spec(a.shape) for a in args],
        out_specs=[pl.BlockSpec(s, lambda i: (0, 0)) for s in out_shapes],
        out_shape=[jax.ShapeDtypeStruct(s, F32) for s in out_shapes],
        compiler_params=pltpu.CompilerParams(
            dimension_semantics=("arbitrary",), vmem_limit_bytes=VMEM_LIMIT),
        name="sample_tail",
    )(*args)


FFN_W_CHUNKS = 8


def _weight_copy(w_hbm, stage, sem, k, rows):
    slot = k % 2
    return pltpu.make_async_copy(w_hbm.at[pl.ds(k * rows, rows), :], stage.at[slot], sem.at[slot])


def _load_cast_weight(w_hbm, w_bf, stage, sem):
    rows = stage.shape[1]
    n = w_hbm.shape[0] // rows
    _weight_copy(w_hbm, stage, sem, 0, rows).start()
    for k in range(n):
        if k + 1 < n:
            _weight_copy(w_hbm, stage, sem, k + 1, rows).start()
        _weight_copy(w_hbm, stage, sem, k, rows).wait()
        w_bf[k * rows:(k + 1) * rows, :] = stage[k % 2].astype(BF)


def _ffn_rows(x, g2_ref, wgate, wup, wdown, gf_ref):
    h = _rms(x, g2_ref[...]).astype(BF)
    gate = _dot(h, wgate[...])
    up = _dot(h, wup[...])
    act = (gate * _sigmoid(gate) * up).astype(BF)
    x2 = x + _dot(act, wdown[...])
    return _rms(x2, gf_ref[...])


def _ffn_kernel(xp_ref, xs_ref, g2_ref, wgate_hbm, wup_hbm, wdown_hbm, gf_ref, yp_ref, ys_ref,
                wgate, wup, wdown, stage_in, stage_out, sem):
    i = pl.program_id(0)
    n_prompt = pl.num_programs(0) - 1

    @pl.when(i == 0)
    def _():
        _load_cast_weight(wgate_hbm, wgate, stage_in, sem)
        _load_cast_weight(wup_hbm, wup, stage_in, sem)
        _load_cast_weight(wdown_hbm, wdown, stage_out, sem)

    @pl.when(i < n_prompt)
    def _():
        yp_ref[...] = _ffn_rows(xp_ref[...], g2_ref, wgate, wup, wdown, gf_ref)

    @pl.when(i == n_prompt)
    def _():
        ys_ref[...] = _ffn_rows(xs_ref[...], g2_ref, wgate, wup, wdown, gf_ref)


def _ffn(xp, xs, g2, wgate, wup, wdown, gf):
    n, ns = xp.shape[0], xs.shape[0]
    n_prompt = n // FFN_TB
    prompt_map = lambda i: (jnp.minimum(i, n_prompt - 1), 0)
    return pl.pallas_call(
        _ffn_kernel,
        grid=(n_prompt + 1,),
        in_specs=[
            pl.BlockSpec((FFN_TB, D_MODEL), prompt_map),
            _const_spec((ns, D_MODEL)),
            _const_spec((1, D_MODEL)),
            pl.BlockSpec(memory_space=pl.ANY),
            pl.BlockSpec(memory_space=pl.ANY),
            pl.BlockSpec(memory_space=pl.ANY),
            _const_spec((1, D_MODEL)),
        ],
        out_specs=[
            pl.BlockSpec((FFN_TB, D_MODEL), prompt_map),
            pl.BlockSpec((ns, D_MODEL), lambda i: (0, 0)),
        ],
        out_shape=[
            jax.ShapeDtypeStruct((n, D_MODEL), F32),
            jax.ShapeDtypeStruct((ns, D_MODEL), F32),
        ],
        scratch_shapes=[
            pltpu.VMEM((D_MODEL, D_FF), BF), pltpu.VMEM((D_MODEL, D_FF), BF), pltpu.VMEM((D_FF, D_MODEL), BF),
            pltpu.VMEM((2, D_MODEL // FFN_W_CHUNKS, D_FF), F32),
            pltpu.VMEM((2, D_FF // FFN_W_CHUNKS, D_MODEL), F32),
            pltpu.SemaphoreType.DMA((2,)),
        ],
        compiler_params=pltpu.CompilerParams(
            dimension_semantics=("arbitrary",), vmem_limit_bytes=VMEM_LIMIT),
        name="ffn",
    )(xp, xs, g2, wgate, wup, wdown, gf)


def _rope_tables(pos):
    pos = np.asarray(pos, np.float64)
    inv_freq = ROPE_THETA ** (-(np.arange(ROPE_HALF, dtype=np.float64) * 2.0 / ROPE_DIM))
    ang = pos[:, None] * inv_freq[None, :]
    cos, sin = np.cos(ang), np.sin(ang)
    pad = np.zeros((pos.shape[0], HEAD_DIM - ROPE_DIM))
    zero = np.zeros_like(sin)
    rc = np.concatenate([cos, cos, pad + 1.0], axis=1)
    rs1 = np.concatenate([zero, sin, pad], axis=1)
    rs2 = np.concatenate([-sin, zero, pad], axis=1)
    rep = LANES // HEAD_DIM
    return tuple(jnp.asarray(np.tile(a, (1, rep)), F32) for a in (rc, rs1, rs2))


def _cmul(ar, ai, br, bi):
    return ar * br - ai * bi, ar * bi + ai * br


def _ssm_tables(lam_re, lam_im, log_dt, b_re, b_im, c_re, c_im):
    dt = jnp.exp(log_dt)[:, None]
    mag = jnp.exp(lam_re * dt)
    lb_re = mag * jnp.cos(lam_im * dt)
    lb_im = mag * jnp.sin(lam_im * dt)
    den = lam_re * lam_re + lam_im * lam_im
    nr = lb_re - 1.0
    k_re = ((nr * lam_re + lb_im * lam_im) / den)[..., None]
    k_im = ((lb_im * lam_re - nr * lam_im) / den)[..., None]
    bb_re = k_re * b_re - k_im * b_im
    bb_im = k_re * b_im + k_im * b_re

    a_re, a_im = lb_re, lb_im
    for _ in range(int(math.log2(CHUNK))):
        a_re, a_im = _cmul(a_re, a_im, a_re, a_im)
    s_re, s_im = a_re, a_im
    for _ in range(int(math.log2(SEG))):
        s_re, s_im = _cmul(s_re, s_im, s_re, s_im)

    eye = jnp.eye(OCT, dtype=F32).reshape(1, OCT, 1, OCT, 1)

    def block_diag(a):
        r, c = a.shape[1:]
        return (a.reshape(N_OCT, OCT, r, 1, c) * eye).reshape(N_OCT, OCT * r, OCT * c)

    bblk = jnp.concatenate([block_diag(jnp.swapaxes(bb_re, 1, 2)),
                            block_diag(jnp.swapaxes(bb_im, 1, 2))], axis=2)
    ctblk = jnp.concatenate([block_diag(c_re), block_diag(-c_im)], axis=2)

    oct_cols = lambda a: a.reshape(N_OCT, 1, OCT_STATES)
    lcol = jnp.concatenate([oct_cols(lb_re), oct_cols(lb_im),
                            jnp.zeros((N_OCT, 6, OCT_STATES), F32)], axis=1)

    flat = lambda a: a.reshape(1, N_STATES)
    col = lambda a: a.reshape(N_COL, 1, LANES)
    a_tab = jnp.concatenate([col(a_re), col(a_im), col(s_re), col(s_im),
                             jnp.zeros((N_COL, 4, LANES), F32)], axis=1)
    lb = jnp.concatenate([flat(lb_re), flat(lb_im)], axis=0)
    return bblk, ctblk, lcol, a_tab, lb


def kernel(x_prompt, x_sample, state_k_win, state_v_win, state_ssm_re, state_ssm_im, norm1_g, w_in, b_gate, attn_sinks, ssm_lam_re, ssm_lam_im, ssm_log_dt, ssm_b_re, ssm_b_im, ssm_c_re, ssm_c_im, ssm_d, w_glu, b_glu, w_branch_attn, w_branch_ssm, w_out, norm2_g, w_ffn_gate, w_ffn_up, w_ffn_down, norm_f_g):
    depth = w_in.shape[0]
    assert depth == 1
    b, t, _ = x_prompt.shape
    nb, s_len, _ = x_sample.shape
    assert s_len == 1 and state_k_win.shape[2] == WINDOW
    l = 0
    o1 = ATTN_WIDTH + 2 * KV_WIDTH
    o2 = o1 + SSM_WIDTH
    wqkv, wu, wg = (w_in[l, :, :o1].astype(BF), w_in[l, :, o1:o2].astype(BF), w_in[l, :, o2:].astype(BF))
    g1 = norm1_g[l].reshape(1, D_MODEL)
    g2 = norm2_g[l].reshape(1, D_MODEL)
    gf = norm_f_g.reshape(1, D_MODEL)
    bg = b_gate[l].reshape(1, GATE_WIDTH)
    d = ssm_d[l].reshape(1, SSM_WIDTH)
    wglu = w_glu[l].astype(BF)
    bglu = b_glu[l].reshape(1, SSM_WIDTH)
    wba = w_branch_attn[l].astype(BF)
    wbs = w_branch_ssm[l].astype(BF)
    wo = w_out[l].astype(BF)
    sinks = attn_sinks[l]

    bblk, ctblk, lcol, a_tab, lb = _ssm_tables(
        ssm_lam_re[l], ssm_lam_im[l], ssm_log_dt[l], ssm_b_re[l], ssm_b_im[l], ssm_c_re[l], ssm_c_im[l])

    rc, rs1, rs2 = _rope_tables(np.arange(t))
    ssm_p, hre_p, him_p = _s5_prompt(x_prompt, g1, wu, bblk, ctblk, lcol, a_tab, d, wglu, bglu)
    x1_p, kwin_p, vwin_p = _mix_prompt(x_prompt, ssm_p, rc, rs1, rs2, sinks, g1, wqkv, wg, bg, wba, wbs, wo)

    rcs, rs1s, rs2s = _rope_tables(PAST_LEN + np.arange(1))
    sinkc = jnp.concatenate([sinks, jnp.zeros((QROWS - N_Q_HEADS,), F32)]).reshape(QROWS, 1)
    wbad = jnp.concatenate([wba.reshape(N_Q_HEADS, HEAD_DIM, D_MODEL)] * 2, axis=1)
    xs = x_sample.reshape(nb, D_MODEL)
    key_minor = lambda a: jnp.swapaxes(a.reshape(nb, WINDOW, KV_WIDTH), 1, 2)
    o3, kwin_s, vwin_s = _sample_attn(
        xs, key_minor(state_k_win[l]), key_minor(state_v_win[l]), rcs, rs1s, rs2s, sinkc, g1, wqkv)
    x1_s, hre_s, him_s = _sample_tail(
        xs, o3, state_ssm_re[l].reshape(nb, N_STATES).T, state_ssm_im[l].reshape(nb, N_STATES).T,
        g1, wu, wg, bg, wbad, wbs, wo, lb, bblk.astype(BF), ctblk.astype(BF), d, wglu, bglu)
    y_p, y_s = _ffn(x1_p, x1_s, g2, w_ffn_gate[l], w_ffn_up[l], w_ffn_down[l], gf)
    y_p = y_p.reshape(b, t, D_MODEL)
    y_s = y_s.reshape(nb, 1, D_MODEL)

    kv_shape_p = (1, b, WINDOW, N_KV_HEADS, HEAD_DIM)
    st_shape_p = (1, b, N_SSM_GROUPS, SSM_STATE)
    kv_shape_s = (1, nb, WINDOW, N_KV_HEADS, HEAD_DIM)
    st_shape_s = (1, nb, N_SSM_GROUPS, SSM_STATE)
    return (y_p, y_s,
            kwin_p.reshape(kv_shape_p), vwin_p.reshape(kv_shape_p),
            hre_p.reshape(st_shape_p), him_p.reshape(st_shape_p),
            jnp.swapaxes(kwin_s, 1, 2).reshape(kv_shape_s), jnp.swapaxes(vwin_s, 1, 2).reshape(kv_shape_s),
            hre_s.T.reshape(st_shape_s), him_s.T.reshape(st_shape_s))
```

```python
import math

import jax
import jax.numpy as jnp
import numpy as np
from jax import lax
from jax.experimental import pallas as pl
from jax.experimental.pallas import tpu as pltpu

D_MODEL = 1024
N_Q_HEADS = 8
N_KV_HEADS = 2
HEAD_DIM = 64
ATTN_WIDTH = N_Q_HEADS * HEAD_DIM
KV_WIDTH = N_KV_HEADS * HEAD_DIM
WINDOW = 128
ROPE_DIM = HEAD_DIM // 4
ROPE_HALF = ROPE_DIM // 2
ROPE_THETA = 500000.0
SSM_WIDTH = D_MODEL // 2
SSM_GROUP = 16
N_SSM_GROUPS = SSM_WIDTH // SSM_GROUP
SSM_STATE = 64
N_STATES = N_SSM_GROUPS * SSM_STATE
GATE_WIDTH = 2 * D_MODEL
D_FF = -(-8 * D_MODEL // (3 * 256)) * 256
NORM_EPS = 1e-5
PAST_LEN = 8192

LANES = 128
CHUNK = 8
OCT = LANES // SSM_GROUP
N_OCT = N_SSM_GROUPS // OCT
OCT_STATES = OCT * SSM_STATE
OCT_COL = OCT_STATES // LANES
N_COL = N_STATES // LANES
S5_ROWS = 128
SEG = 16
M_PAIR = 2
SEG_PITCH = SEG * CHUNK + 8
MIX_TB = 512
FFN_TB = 1024
FFN_SPLIT = 4
NEG_BIG = -1e30
VMEM_LIMIT = 56 * 1024 * 1024

BF = jnp.bfloat16
F32 = jnp.float32


def _dot(a, b):
    return jnp.dot(a, b, preferred_element_type=F32)


def _dot_nt(a, b):
    return lax.dot_general(a, b, (((1,), (1,)), ((), ())), preferred_element_type=F32)


def _dot_nt_split(a, b):
    a_hi, b_hi = a.astype(BF), b.astype(BF)
    a_lo = (a - a_hi.astype(F32)).astype(BF)
    b_lo = (b - b_hi.astype(F32)).astype(BF)
    return _dot_nt(a_hi, b_hi) + (_dot_nt(a_hi, b_lo) + _dot_nt(a_lo, b_hi))


def _rms(x, g):
    return x * lax.rsqrt(jnp.mean(x * x, axis=-1, keepdims=True) + NORM_EPS) * g


def _sigmoid(x):
    return 1.0 / (1.0 + jnp.exp(-x))


def _gelu_tanh(x):
    c = math.sqrt(2.0 / math.pi)
    return 0.5 * x * (1.0 + jnp.tanh(c * (x + 0.044715 * (x * x * x))))


def _rope(a, rc, rs1, rs2):
    return a * rc + pltpu.roll(a, ROPE_HALF, 1) * rs1 + pltpu.roll(a, LANES - ROPE_HALF, 1) * rs2


def _const_spec(shape):
    nd = len(shape)
    return pl.BlockSpec(shape, lambda *_: (0,) * nd, pipeline_mode=pl.Buffered(1))


def _build_chunk_operators(bblk_ref, ctblk_ref, lcol_ref, m_s, e_s, f_s):
    for o in range(N_OCT):
        ct = ctblk_ref[o]
        lr, li = lcol_ref[o, 0:1, :], lcol_ref[o, 1:2, :]
        er, ei = bblk_ref[o, :, :OCT_STATES], bblk_ref[o, :, OCT_STATES:]
        k_blk = []
        for tau in range(CHUNK):
            e_cat = jnp.concatenate([er, ei], axis=1)
            i = CHUNK - 1 - tau
            e_s[o, i * LANES:(i + 1) * LANES, :] = e_cat.astype(BF)
            k_blk.append(_dot_nt_split(e_cat, ct).astype(BF))
            er, ei = er * lr - ei * li, er * li + ei * lr
        zero = jnp.zeros((LANES, LANES), BF)
        for j in range(CHUNK):
            jt, jj = divmod(j, M_PAIR)
            for i in range(M_PAIR * (jt + 1)):
                m_s[jt][o, i * LANES:(i + 1) * LANES, jj * LANES:(jj + 1) * LANES] = (
                    k_blk[j - i] if j >= i else zero)
        tr, ti = ct[:, :OCT_STATES], -ct[:, OCT_STATES:]
        for j in range(CHUNK):
            tr, ti = tr * lr - ti * li, tr * li + ti * lr
            f_s[o, :OCT_STATES, j * LANES:(j + 1) * LANES] = tr.T.astype(BF)
            f_s[o, OCT_STATES:, j * LANES:(j + 1) * LANES] = (-ti).T.astype(BF)


def _s5_prompt_kernel(x_ref, g1_ref, wu_ref, bblk_ref, ctblk_ref, lcol_ref, a_ref, d_ref,
                      out_ref, hfin_ref,
                      m0_ref, m1_ref, m2_ref, m3_ref, e_ref, f_ref, us, ys, sre, sim, car):
    m_ref = (m0_ref, m1_ref, m2_ref, m3_ref)
    blk = pl.program_id(1)

    @pl.when((pl.program_id(0) == 0) & (blk == 0))
    def _():
        _build_chunk_operators(bblk_ref, ctblk_ref, lcol_ref, m_ref, e_ref, f_ref)

    @pl.when(blk == 0)
    def _():
        car[...] = jnp.zeros_like(car)

    seg_tokens = SEG * CHUNK
    seg_rows = [slice(s * SEG_PITCH, s * SEG_PITCH + seg_tokens) for s in range(8)]
    tok_rows = [slice(s * seg_tokens, (s + 1) * seg_tokens) for s in range(8)]

    for s in range(0, 8, 2):
        hn = _rms(x_ref[s * seg_tokens:(s + 2) * seg_tokens, :], g1_ref[...]).astype(BF)
        u = _dot(hn, wu_ref[...])
        for half in range(2):
            for cc in range(N_OCT):
                us[cc, seg_rows[s + half], :] = u[tok_rows[half], cc * LANES:(cc + 1) * LANES]

    sub = lax.broadcasted_iota(jnp.int32, (8, LANES), 0)

    def scan_column(c):
        tab = a_ref[c]
        are, aim = tab[0:1], tab[1:2]
        bre, bim = tab[2:3], tab[3:4]

        def step(cr, ci, r, keep_entering):
            slab = slice(r * 8, (r + 1) * 8)
            s_r, s_i = sre[c, slab, :], sim[c, slab, :]
            if keep_entering:
                sre[c, slab, :] = cr
                sim[c, slab, :] = ci
            return are * cr - aim * ci + s_r, are * ci + aim * cr + s_i

        cr = jnp.zeros((8, LANES), F32)
        ci = jnp.zeros((8, LANES), F32)
        for r in range(SEG):
            cr, ci = step(cr, ci, r, False)
        cv = car[c]
        pr, pi = cv[0:1], cv[1:2]
        sr = jnp.zeros((8, LANES), F32)
        si = jnp.zeros((8, LANES), F32)
        for s in range(8):
            sr = jnp.where(sub == s, pr, sr)
            si = jnp.where(sub == s, pi, si)
            pr, pi = (bre * pr - bim * pi + cr[s:s + 1], bre * pi + bim * pr + ci[s:s + 1])
        end = jnp.where(sub == 0, pr, jnp.where(sub == 1, pi, 0.0))
        car[c] = end
        hfin_ref[c] = end
        cr, ci = sr, si
        for r in range(SEG):
            cr, ci = step(cr, ci, r, True)

    for o in range(N_OCT):
        uo = jnp.concatenate(
            [jnp.concatenate([us[o, pl.ds(r * CHUNK + i, 8, stride=SEG_PITCH), :] for r in range(SEG)],
                             axis=0).astype(BF) for i in range(CHUNK)], axis=1)
        s_end = _dot(uo, e_ref[o])
        cols = range(o * OCT_COL, (o + 1) * OCT_COL)
        for cc, c in enumerate(cols):
            sre[c] = s_end[:, cc * LANES:(cc + 1) * LANES]
            sim[c] = s_end[:, OCT_STATES + cc * LANES:OCT_STATES + (cc + 1) * LANES]
        y_in = jnp.concatenate(
            [_dot(uo[:, :(jt + 1) * M_PAIR * LANES], m_ref[jt][o]) for jt in range(CHUNK // M_PAIR)],
            axis=1)
        for c in cols:
            scan_column(c)
        hp = jnp.concatenate([sre[c] for c in cols] + [sim[c] for c in cols], axis=1).astype(BF)
        yo = y_in + _dot(hp, f_ref[o])
        for r in range(SEG):
            for j in range(CHUNK):
                ys[o, pl.ds(r * CHUNK + j, 8, stride=SEG_PITCH), :] = (
                    yo[r * 8:(r + 1) * 8, j * LANES:(j + 1) * LANES])
        d_o = d_ref[:, o * LANES:(o + 1) * LANES]
        for s in range(8):
            out_ref[tok_rows[s], o * LANES:(o + 1) * LANES] = (
                ys[o, seg_rows[s], :] + d_o * us[o, seg_rows[s], :])


def _s5_prompt(x, g1, wu, bblk, ctblk, lcol, a_tab, d):
    b, t, _ = x.shape
    tb = S5_ROWS * CHUNK
    nblk = t // tb
    x2 = x.reshape(b * t, D_MODEL)
    row_map = lambda i, j: (i * nblk + j, 0)
    op_shape = (N_OCT, CHUNK * LANES, CHUNK * LANES)
    out, hfin = pl.pallas_call(
        _s5_prompt_kernel,
        grid=(b, nblk),
        in_specs=[
            pl.BlockSpec((tb, D_MODEL), row_map),
            _const_spec((1, D_MODEL)),
            _const_spec((D_MODEL, SSM_WIDTH)),
            _const_spec(bblk.shape),
            _const_spec(ctblk.shape),
            _const_spec(lcol.shape),
            _const_spec(a_tab.shape),
            _const_spec((1, SSM_WIDTH)),
        ],
        out_specs=[
            pl.BlockSpec((tb, SSM_WIDTH), row_map),
            pl.BlockSpec((None, N_COL, 8, LANES), lambda i, j: (i, 0, 0, 0)),
        ],
        out_shape=[
            jax.ShapeDtypeStruct((b * t, SSM_WIDTH), F32),
            jax.ShapeDtypeStruct((b, N_COL, 8, LANES), F32),
        ],
        scratch_shapes=[
            *[pltpu.VMEM((N_OCT, (jt + 1) * M_PAIR * LANES, M_PAIR * LANES), BF)
              for jt in range(CHUNK // M_PAIR)],
            pltpu.VMEM(op_shape, BF), pltpu.VMEM(op_shape, BF),
            pltpu.VMEM((N_OCT, 8 * SEG_PITCH, LANES), F32), pltpu.VMEM((N_OCT, 8 * SEG_PITCH, LANES), F32),
            pltpu.VMEM((N_COL, S5_ROWS, LANES), F32), pltpu.VMEM((N_COL, S5_ROWS, LANES), F32),
            pltpu.VMEM((N_COL, 8, LANES), F32),
        ],
        compiler_params=pltpu.CompilerParams(
            dimension_semantics=("arbitrary", "arbitrary"), vmem_limit_bytes=VMEM_LIMIT),
        name="s5_prompt",
    )(x2, g1, wu, bblk, ctblk, lcol, a_tab, d)
    return out, hfin[:, :, 0, :], hfin[:, :, 1, :]


def _gate_cols(hn, wg_ref, bg_ref, cols):
    return _sigmoid(_dot(hn, wg_ref[:, cols]) + bg_ref[:, cols])


def _merge_out(x, gates, attn_proj, ssm_proj, wo_ref):
    merged = gates[:, :D_MODEL] * attn_proj + gates[:, D_MODEL:] * ssm_proj
    return x + _dot(merged.astype(BF), wo_ref[...])


def _half_split(a, ar, lo):
    z = jnp.zeros_like(a)
    return (jnp.where(lo, a, z).astype(BF), jnp.where(lo, z, ar).astype(BF),
            jnp.where(lo, ar, z).astype(BF), jnp.where(lo, z, a).astype(BF))


def _softmax_sink_t(st, bias_t, sink):
    st = st + bias_t
    m = jnp.maximum(jnp.max(st, axis=0, keepdims=True), sink)
    p = jnp.exp(st - m)
    den = jnp.sum(p, axis=0, keepdims=True) + jnp.exp(sink - m)
    return (p * (1.0 / den)).astype(BF)


def _mix_prompt_kernel(sinks_ref, x_ref, y_ref, rc_ref, rs1_ref, rs2_ref, g1_ref, wqkv_ref,
                       wg_ref, bg_ref, wba_ref, wbs_ref, wo_ref, wglu_ref, bglu_ref,
                       x1_ref, kwin_ref, vwin_ref, kprev, vprev):
    t = pl.program_id(1)

    @pl.when(t == 0)
    def _():
        kprev[...] = jnp.zeros_like(kprev)
        vprev[...] = jnp.zeros_like(vprev)

    x = x_ref[...]
    hn = _rms(x, g1_ref[...]).astype(BF)
    qkv = _dot(hn, wqkv_ref[...])
    rc, rs1, rs2 = rc_ref[...], rs1_ref[...], rs2_ref[...]
    scale = HEAD_DIM ** -0.5
    q = [(_rope(qkv[:, c * LANES:(c + 1) * LANES], rc, rs1, rs2) * scale).astype(BF)
         for c in range(ATTN_WIDTH // LANES)]
    k = _rope(qkv[:, ATTN_WIDTH:ATTN_WIDTH + KV_WIDTH], rc, rs1, rs2)
    v = qkv[:, ATTN_WIDTH + KV_WIDTH:ATTN_WIDTH + 2 * KV_WIDTH]

    w = WINDOW
    n_sub = MIX_TB // w
    lane = lax.broadcasted_iota(jnp.int32, (2 * w, LANES), 1)
    lo = lane < HEAD_DIM
    kj = lax.broadcasted_iota(jnp.int32, (2 * w, w), 0)
    qi = lax.broadcasted_iota(jnp.int32, (2 * w, w), 1)
    band = (kj > qi) & (kj <= qi + w)
    first = band & ((kj >= w) | (t > 0))
    bias_band = jnp.where(band, 0.0, NEG_BIG).astype(F32)
    bias_first = jnp.where(first, 0.0, NEG_BIG).astype(F32)
    col = lax.broadcasted_iota(jnp.int32, (1, 2 * w), 1)

    def sink_row(ha, hb):
        return jnp.where(col < w, sinks_ref[ha], sinks_ref[hb])

    z = _gelu_tanh(y_ref[...])
    ssm = z * _sigmoid(_dot(z.astype(BF), wglu_ref[...]) + bglu_ref[...])
    ssm_proj = _dot(ssm.astype(BF), wbs_ref[...])
    gate_w = GATE_WIDTH // (n_sub * N_KV_HEADS)
    gate_cols = []
    attn_rows = []
    for sb in range(n_sub):
        cur = slice(sb * w, (sb + 1) * w)
        if sb == 0:
            kcat = jnp.concatenate([kprev[...], k[cur]], axis=0)
            vcat = jnp.concatenate([vprev[...], v[cur]], axis=0)
            bias = bias_first
        else:
            kcat = k[(sb - 1) * w:(sb + 1) * w]
            vcat = v[(sb - 1) * w:(sb + 1) * w]
            bias = bias_band
        bias2 = jnp.concatenate([bias, bias], axis=1)
        ka, kb, kc, kd = _half_split(kcat, pltpu.roll(kcat, HEAD_DIM, 1), lo)
        vt = vcat.T
        zero = jnp.zeros((HEAD_DIM, 2 * w), F32)
        v_rows = [(jnp.concatenate([vt[g * HEAD_DIM:(g + 1) * HEAD_DIM], zero], axis=0).astype(BF),
                   jnp.concatenate([zero, vt[g * HEAD_DIM:(g + 1) * HEAD_DIM]], axis=0).astype(BF))
                  for g in range(N_KV_HEADS)]
        outs = []
        for grp, (k_e, k_o) in enumerate(((ka, kb), (kc, kd))):
            qq = jnp.concatenate([q[2 * grp][cur], q[2 * grp + 1][cur]], axis=0)
            s_e, s_o = _dot_nt(k_e, qq), _dot_nt(k_o, qq)
            gc = sb * N_KV_HEADS + grp
            gate_cols.append(_gate_cols(hn, wg_ref, bg_ref, slice(gc * gate_w, (gc + 1) * gate_w)))
            p_e = _softmax_sink_t(s_e, bias2, sink_row(4 * grp, 4 * grp + 2))
            p_o = _softmax_sink_t(s_o, bias2, sink_row(4 * grp + 1, 4 * grp + 3))
            v_e, v_o = v_rows[grp]
            o2 = _dot(v_e, p_e) + _dot(v_o, p_o)
            outs += [o2[:, :w].T, o2[:, w:].T]
        attn_rows.append(jnp.concatenate(outs, axis=1))
    attn = jnp.concatenate(attn_rows, axis=0).astype(BF)
    gates = jnp.concatenate(gate_cols, axis=1)

    kprev[...] = k[MIX_TB - w:]
    vprev[...] = v[MIX_TB - w:]
    kwin_ref[...] = k[MIX_TB - w:]
    vwin_ref[...] = v[MIX_TB - w:]
    x1_ref[...] = _merge_out(x, gates, _dot(attn, wba_ref[...]), ssm_proj, wo_ref)


def _mix_prompt(x, y_ssm, rc, rs1, rs2, sinks, g1, wqkv, wg, bg, wba, wbs, wo, wglu, bglu):
    b, t, _ = x.shape
    nblk = t // MIX_TB
    x2 = x.reshape(b * t, D_MODEL)
    row_map = lambda i, j: (i * nblk + j, 0)
    x1, kwin, vwin = pl.pallas_call(
        _mix_prompt_kernel,
        grid=(b, nblk),
        in_specs=[
            pl.BlockSpec(memory_space=pltpu.SMEM),
            pl.BlockSpec((MIX_TB, D_MODEL), row_map),
            pl.BlockSpec((MIX_TB, SSM_WIDTH), row_map),
            pl.BlockSpec((MIX_TB, LANES), lambda i, j: (j, 0)),
            pl.BlockSpec((MIX_TB, LANES), lambda i, j: (j, 0)),
            pl.BlockSpec((MIX_TB, LANES), lambda i, j: (j, 0)),
            _const_spec((1, D_MODEL)),
            _const_spec(wqkv.shape),
            _const_spec(wg.shape),
            _const_spec(bg.shape),
            _const_spec(wba.shape),
            _const_spec(wbs.shape),
            _const_spec(wo.shape),
            _const_spec(wglu.shape),
            _const_spec(bglu.shape),
        ],
        out_specs=[
            pl.BlockSpec((MIX_TB, D_MODEL), row_map),
            pl.BlockSpec((None, WINDOW, KV_WIDTH), lambda i, j: (i, 0, 0)),
            pl.BlockSpec((None, WINDOW, KV_WIDTH), lambda i, j: (i, 0, 0)),
        ],
        out_shape=[
            jax.ShapeDtypeStruct((b * t, D_MODEL), F32),
            jax.ShapeDtypeStruct((b, WINDOW, KV_WIDTH), F32),
            jax.ShapeDtypeStruct((b, WINDOW, KV_WIDTH), F32),
        ],
        scratch_shapes=[pltpu.VMEM((WINDOW, KV_WIDTH), F32), pltpu.VMEM((WINDOW, KV_WIDTH), F32)],
        compiler_params=pltpu.CompilerParams(
            dimension_semantics=("arbitrary", "arbitrary"), vmem_limit_bytes=VMEM_LIMIT),
        name="mix_prompt",
    )(sinks, x2, y_ssm, rc, rs1, rs2, g1, wqkv, wg, bg, wba, wbs, wo, wglu, bglu)
    return x1, kwin, vwin


QROWS = 16
SAMPLE_TB = 32


def _sample_attn_kernel(x_ref, kbuf_ref, vbuf_ref, rc_ref, rs1_ref, rs2_ref, sinkc_ref, g1_ref, wqkv_ref,
                        o_ref, kout_ref, vout_ref, qz):
    nb = x_ref.shape[0]
    hn = _rms(x_ref[...], g1_ref[...]).astype(BF)
    qkv = _dot(hn, wqkv_ref[...])
    rc, rs1, rs2 = rc_ref[...], rs1_ref[...], rs2_ref[...]
    scale = HEAD_DIM ** -0.5
    k_new = _rope(qkv[:, ATTN_WIDTH:ATTN_WIDTH + KV_WIDTH], rc, rs1, rs2)
    v_new = qkv[:, ATTN_WIDTH + KV_WIDTH:ATTN_WIDTH + 2 * KV_WIDTH]
    pad = jnp.zeros((LANES - nb, KV_WIDTH), F32)
    k_new_t = jnp.concatenate([k_new, pad], axis=0).T
    v_new_t = jnp.concatenate([v_new, pad], axis=0).T

    lane = lax.broadcasted_iota(jnp.int32, (nb, LANES), 1)
    lo = lane < HEAD_DIM
    qz[...] = jnp.zeros_like(qz)
    for c in range(ATTN_WIDTH // LANES):
        qc = _rope(qkv[:, c * LANES:(c + 1) * LANES], rc, rs1, rs2) * scale
        qr = pltpu.roll(qc, HEAD_DIM, 1)
        zero = jnp.zeros_like(qc)
        if c < 2:
            even, odd = jnp.where(lo, qc, zero), jnp.where(lo, qr, zero)
        else:
            even, odd = jnp.where(lo, zero, qr), jnp.where(lo, zero, qc)
        qz[pl.ds(2 * c, nb, stride=QROWS), :] = even
        qz[pl.ds(2 * c + 1, nb, stride=QROWS), :] = odd

    last = lax.broadcasted_iota(jnp.int32, (KV_WIDTH, WINDOW), 1) == WINDOW - 1
    for b in range(nb):
        kout_ref[b] = jnp.where(last, k_new_t[:, b:b + 1], pltpu.roll(kbuf_ref[b], WINDOW - 1, 1))
        vout_ref[b] = jnp.where(last, v_new_t[:, b:b + 1], pltpu.roll(vbuf_ref[b], WINDOW - 1, 1))

    sink = sinkc_ref[...]
    q3 = qz[...].reshape(nb, QROWS, LANES).astype(BF)
    s = jnp.einsum('bhd,bdk->bhk', q3, kout_ref[...].astype(BF), preferred_element_type=F32)
    m = jnp.maximum(jnp.max(s, axis=-1, keepdims=True), sink)
    p = jnp.exp(s - m)
    den = jnp.sum(p, axis=-1, keepdims=True) + jnp.exp(sink - m)
    p = (p * (1.0 / den)).astype(BF)
    o3 = jnp.einsum('bhk,bdk->bhd', p, vout_ref[...].astype(BF), preferred_element_type=F32)
    o_ref[...] = o3.reshape(nb * QROWS, LANES)


def _sample_attn(x, kbuf, vbuf, rc, rs1, rs2, sinkc, g1, wqkv):
    nb = x.shape[0]
    tb = SAMPLE_TB
    return pl.pallas_call(
        _sample_attn_kernel,
        grid=(nb // tb,),
        in_specs=[
            pl.BlockSpec((tb, D_MODEL), lambda i: (i, 0)),
            pl.BlockSpec((tb, KV_WIDTH, WINDOW), lambda i: (i, 0, 0)),
            pl.BlockSpec((tb, KV_WIDTH, WINDOW), lambda i: (i, 0, 0)),
            _const_spec(rc.shape), _const_spec(rs1.shape), _const_spec(rs2.shape),
            _const_spec(sinkc.shape), _const_spec(g1.shape), _const_spec(wqkv.shape),
        ],
        out_specs=[
            pl.BlockSpec((tb * QROWS, LANES), lambda i: (i, 0)),
            pl.BlockSpec((tb, KV_WIDTH, WINDOW), lambda i: (i, 0, 0)),
            pl.BlockSpec((tb, KV_WIDTH, WINDOW), lambda i: (i, 0, 0)),
        ],
        out_shape=[
            jax.ShapeDtypeStruct((nb * QROWS, LANES), F32),
            jax.ShapeDtypeStruct((nb, KV_WIDTH, WINDOW), F32),
            jax.ShapeDtypeStruct((nb, KV_WIDTH, WINDOW), F32),
        ],
        scratch_shapes=[pltpu.VMEM((tb * QROWS, LANES), F32)],
        compiler_params=pltpu.CompilerParams(
            dimension_semantics=("arbitrary",), vmem_limit_bytes=VMEM_LIMIT),
        name="sample_attn",
    )(x, kbuf, vbuf, rc, rs1, rs2, sinkc, g1, wqkv)


def _sample_tail_kernel(x_ref, o3_ref, h0re_ref, h0im_ref, g1_ref, wu_ref, wg_ref, bg_ref, wbad_ref,
                        wbs_ref, wo_ref, lb_ref, bblk_ref, ctblk_ref, d_ref, wglu_ref, bglu_ref,
                        x1_ref, hre_ref, him_ref):
    nb = x_ref.shape[0]
    x = x_ref[...]
    hn = _rms(x, g1_ref[...]).astype(BF)

    lane = lax.broadcasted_iota(jnp.int32, (nb, LANES), 1)
    lo = lane < HEAD_DIM
    a = jnp.zeros((nb, D_MODEL), F32)
    zero = jnp.zeros((nb, LANES), F32)
    for h in range(N_Q_HEADS):
        oh = o3_ref[pl.ds(h, nb, stride=QROWS), :]
        oh = jnp.where(lo, oh, zero) if h < N_Q_HEADS // 2 else jnp.where(lo, zero, oh)
        a = a + _dot(oh.astype(BF), wbad_ref[h])

    u = _dot(hn, wu_ref[...])
    ub = u.astype(BF)
    lre, lim = lb_ref[0:1, :], lb_ref[1:2, :]
    y_cols = []
    for o in range(N_OCT):
        sl = slice(o * OCT_STATES, (o + 1) * OCT_STATES)
        bu = _dot(ub[:, o * LANES:(o + 1) * LANES], bblk_ref[o])
        blocks = [slice(c * LANES, (c + 1) * LANES) for c in range(o * OCT_COL, (o + 1) * OCT_COL)]
        h0r = jnp.concatenate([h0re_ref[rows, :].T for rows in blocks], axis=1)
        h0i = jnp.concatenate([h0im_ref[rows, :].T for rows in blocks], axis=1)
        hr = bu[:, :OCT_STATES] + (lre[:, sl] * h0r - lim[:, sl] * h0i)
        hi = bu[:, OCT_STATES:] + (lre[:, sl] * h0i + lim[:, sl] * h0r)
        for cc, rows in enumerate(blocks):
            hre_ref[rows, :] = hr[:, cc * LANES:(cc + 1) * LANES].T
            him_ref[rows, :] = hi[:, cc * LANES:(cc + 1) * LANES].T
        y_cols.append(_dot_nt(jnp.concatenate([hr, hi], axis=1).astype(BF), ctblk_ref[o]))
    y = jnp.concatenate(y_cols, axis=1) + d_ref[...] * u
    z = _gelu_tanh(y)
    gate = _dot(z.astype(BF), wglu_ref[...]) + bglu_ref[...]
    ssm = (z * _sigmoid(gate)).astype(BF)

    gates = _gate_cols(hn, wg_ref, bg_ref, slice(0, GATE_WIDTH))
    x1_ref[...] = _merge_out(x, gates, a, _dot(ssm, wbs_ref[...]), wo_ref)


def _sample_tail(x, o3, h0re, h0im, g1, wu, wg, bg, wbad, wbs, wo, lb, bblk, ctblk, d, wglu, bglu):
    nb = x.shape[0]
    args = (x, o3, h0re, h0im, g1, wu, wg, bg, wbad, wbs, wo, lb, bblk, ctblk, d, wglu, bglu)
    out_shapes = ((nb, D_MODEL), (N_STATES, nb), (N_STATES, nb))
    return pl.pallas_call(
        _sample_tail_kernel,
        grid=(1,),
        in_specs=[_const_spec(a.shape) for a in args],
        out_specs=[pl.BlockSpec(s, lambda i: (0, 0)) for s in out_shapes],
        out_shape=[jax.ShapeDtypeStruct(s, F32) for s in out_shapes],
        compiler_params=pltpu.CompilerParams(
            dimension_semantics=("arbitrary",), vmem_limit_bytes=VMEM_LIMIT),
        name="sample_tail",
    )(*args)


FFN_W_CHUNKS = 8


def _weight_copy(w_hbm, stage, sem, k, rows):
    slot = k % 2
    return pltpu.make_async_copy(w_hbm.at[pl.ds(k * rows, rows), :], stage.at[slot], sem.at[slot])


def _load_cast_weight(w_hbm, w_bf, stage, sem):
    rows = stage.shape[1]
    n = w_hbm.shape[0] // rows
    _weight_copy(w_hbm, stage, sem, 0, rows).start()
    for k in range(n):
        if k + 1 < n:
            _weight_copy(w_hbm, stage, sem, k + 1, rows).start()
        _weight_copy(w_hbm, stage, sem, k, rows).wait()
        w_bf[k * rows:(k + 1) * rows, :] = stage[k % 2].astype(BF)


def _ffn_rows(x, g2_ref, wgate, wup, wdown, gf_ref):
    h = _rms(x, g2_ref[...]).astype(BF)
    gate = _dot(h, wgate[...])
    up = _dot(h, wup[...])
    act = (gate * _sigmoid(gate) * up).astype(BF)
    x2 = x + _dot(act, wdown[...])
    return _rms(x2, gf_ref[...])


def _ffn_kernel(xp_ref, xs_ref, g2_ref, wgate_hbm, wup_hbm, wdown_hbm, gf_ref, yp_ref, ys_ref,
                wgate, wup, wdown, stage_in, stage_out, sem):
    i = pl.program_id(0)
    n_prompt = pl.num_programs(0) - 1

    @pl.when(i == 0)
    def _():
        _load_cast_weight(wgate_hbm, wgate, stage_in, sem)
        _load_cast_weight(wup_hbm, wup, stage_in, sem)
        _load_cast_weight(wdown_hbm, wdown, stage_out, sem)

    @pl.when(i < n_prompt)
    def _():
        half = FFN_TB // FFN_SPLIT
        for rows in (slice(h * half, (h + 1) * half) for h in range(FFN_SPLIT)):
            yp_ref[rows, :] = _ffn_rows(xp_ref[rows, :], g2_ref, wgate, wup, wdown, gf_ref)

    @pl.when(i == n_prompt)
    def _():
        ys_ref[...] = _ffn_rows(xs_ref[...], g2_ref, wgate, wup, wdown, gf_ref)


def _ffn(xp, xs, g2, wgate, wup, wdown, gf):
    n, ns = xp.shape[0], xs.shape[0]
    n_prompt = n // FFN_TB
    prompt_map = lambda i: (jnp.minimum(i, n_prompt - 1), 0)
    return pl.pallas_call(
        _ffn_kernel,
        grid=(n_prompt + 1,),
        in_specs=[
            pl.BlockSpec((FFN_TB, D_MODEL), prompt_map),
            _const_spec((ns, D_MODEL)),
            _const_spec((1, D_MODEL)),
            pl.BlockSpec(memory_space=pl.ANY),
            pl.BlockSpec(memory_space=pl.ANY),
            pl.BlockSpec(memory_space=pl.ANY),
            _const_spec((1, D_MODEL)),
        ],
        out_specs=[
            pl.BlockSpec((FFN_TB, D_MODEL), prompt_map),
            pl.BlockSpec((ns, D_MODEL), lambda i: (0, 0)),
        ],
        out_shape=[
            jax.ShapeDtypeStruct((n, D_MODEL), F32),
            jax.ShapeDtypeStruct((ns, D_MODEL), F32),
        ],
        scratch_shapes=[
            pltpu.VMEM((D_MODEL, D_FF), BF), pltpu.VMEM((D_MODEL, D_FF), BF), pltpu.VMEM((D_FF, D_MODEL), BF),
            pltpu.VMEM((2, D_MODEL // FFN_W_CHUNKS, D_FF), F32),
            pltpu.VMEM((2, D_FF // FFN_W_CHUNKS, D_MODEL), F32),
            pltpu.SemaphoreType.DMA((2,)),
        ],
        compiler_params=pltpu.CompilerParams(
            dimension_semantics=("arbitrary",), vmem_limit_bytes=VMEM_LIMIT),
        name="ffn",
    )(xp, xs, g2, wgate, wup, wdown, gf)


def _rope_tables(pos):
    pos = np.asarray(pos, np.float64)
    inv_freq = ROPE_THETA ** (-(np.arange(ROPE_HALF, dtype=np.float64) * 2.0 / ROPE_DIM))
    ang = pos[:, None] * inv_freq[None, :]
    cos, sin = np.cos(ang), np.sin(ang)
    pad = np.zeros((pos.shape[0], HEAD_DIM - ROPE_DIM))
    zero = np.zeros_like(sin)
    rc = np.concatenate([cos, cos, pad + 1.0], axis=1)
    rs1 = np.concatenate([zero, sin, pad], axis=1)
    rs2 = np.concatenate([-sin, zero, pad], axis=1)
    rep = LANES // HEAD_DIM
    return tuple(jnp.asarray(np.tile(a, (1, rep)), F32) for a in (rc, rs1, rs2))


def _cmul(ar, ai, br, bi):
    return ar * br - ai * bi, ar * bi + ai * br


def _ssm_tables(lam_re, lam_im, log_dt, b_re, b_im, c_re, c_im):
    dt = jnp.exp(log_dt)[:, None]
    mag = jnp.exp(lam_re * dt)
    lb_re = mag * jnp.cos(lam_im * dt)
    lb_im = mag * jnp.sin(lam_im * dt)
    den = lam_re * lam_re + lam_im * lam_im
    nr = lb_re - 1.0
    k_re = ((nr * lam_re + lb_im * lam_im) / den)[..., None]
    k_im = ((lb_im * lam_re - nr * lam_im) / den)[..., None]
    bb_re = k_re * b_re - k_im * b_im
    bb_im = k_re * b_im + k_im * b_re

    a_re, a_im = lb_re, lb_im
    for _ in range(int(math.log2(CHUNK))):
        a_re, a_im = _cmul(a_re, a_im, a_re, a_im)
    s_re, s_im = a_re, a_im
    for _ in range(int(math.log2(SEG))):
        s_re, s_im = _cmul(s_re, s_im, s_re, s_im)

    eye = jnp.eye(OCT, dtype=F32).reshape(1, OCT, 1, OCT, 1)

    def block_diag(a):
        r, c = a.shape[1:]
        return (a.reshape(N_OCT, OCT, r, 1, c) * eye).reshape(N_OCT, OCT * r, OCT * c)

    bblk = jnp.concatenate([block_diag(jnp.swapaxes(bb_re, 1, 2)),
                            block_diag(jnp.swapaxes(bb_im, 1, 2))], axis=2)
    ctblk = jnp.concatenate([block_diag(c_re), block_diag(-c_im)], axis=2)

    oct_cols = lambda a: a.reshape(N_OCT, 1, OCT_STATES)
    lcol = jnp.concatenate([oct_cols(lb_re), oct_cols(lb_im),
                            jnp.zeros((N_OCT, 6, OCT_STATES), F32)], axis=1)

    flat = lambda a: a.reshape(1, N_STATES)
    col = lambda a: a.reshape(N_COL, 1, LANES)
    a_tab = jnp.concatenate([col(a_re), col(a_im), col(s_re), col(s_im),
                             jnp.zeros((N_COL, 4, LANES), F32)], axis=1)
    lb = jnp.concatenate([flat(lb_re), flat(lb_im)], axis=0)
    return bblk, ctblk, lcol, a_tab, lb


def kernel(x_prompt, x_sample, state_k_win, state_v_win, state_ssm_re, state_ssm_im, norm1_g, w_in, b_gate, attn_sinks, ssm_lam_re, ssm_lam_im, ssm_log_dt, ssm_b_re, ssm_b_im, ssm_c_re, ssm_c_im, ssm_d, w_glu, b_glu, w_branch_attn, w_branch_ssm, w_out, norm2_g, w_ffn_gate, w_ffn_up, w_ffn_down, norm_f_g):
    depth = w_in.shape[0]
    assert depth == 1
    b, t, _ = x_prompt.shape
    nb, s_len, _ = x_sample.shape
    assert s_len == 1 and state_k_win.shape[2] == WINDOW
    l = 0
    o1 = ATTN_WIDTH + 2 * KV_WIDTH
    o2 = o1 + SSM_WIDTH
    wqkv, wu, wg = (w_in[l, :, :o1].astype(BF), w_in[l, :, o1:o2].astype(BF), w_in[l, :, o2:].astype(BF))
    g1 = norm1_g[l].reshape(1, D_MODEL)
    g2 = norm2_g[l].reshape(1, D_MODEL)
    gf = norm_f_g.reshape(1, D_MODEL)
    bg = b_gate[l].reshape(1, GATE_WIDTH)
    d = ssm_d[l].reshape(1, SSM_WIDTH)
    wglu = w_glu[l].astype(BF)
    bglu = b_glu[l].reshape(1, SSM_WIDTH)
    wba = w_branch_attn[l].astype(BF)
    wbs = w_branch_ssm[l].astype(BF)
    wo = w_out[l].astype(BF)
    sinks = attn_sinks[l]

    bblk, ctblk, lcol, a_tab, lb = _ssm_tables(
        ssm_lam_re[l], ssm_lam_im[l], ssm_log_dt[l], ssm_b_re[l], ssm_b_im[l], ssm_c_re[l], ssm_c_im[l])

    rc, rs1, rs2 = _rope_tables(np.arange(t))
    yssm_p, hre_p, him_p = _s5_prompt(x_prompt, g1, wu, bblk, ctblk, lcol, a_tab, d)
    x1_p, kwin_p, vwin_p = _mix_prompt(x_prompt, yssm_p, rc, rs1, rs2, sinks, g1, wqkv, wg, bg, wba, wbs, wo,
                                       wglu, bglu)

    rcs, rs1s, rs2s = _rope_tables(PAST_LEN + np.arange(1))
    sinkc = jnp.concatenate([sinks, jnp.zeros((QROWS - N_Q_HEADS,), F32)]).reshape(QROWS, 1)
    wbad = jnp.concatenate([wba.reshape(N_Q_HEADS, HEAD_DIM, D_MODEL)] * 2, axis=1)
    xs = x_sample.reshape(nb, D_MODEL)
    key_minor = lambda a: jnp.swapaxes(a.reshape(nb, WINDOW, KV_WIDTH), 1, 2)
    o3, kwin_s, vwin_s = _sample_attn(
        xs, key_minor(state_k_win[l]), key_minor(state_v_win[l]), rcs, rs1s, rs2s, sinkc, g1, wqkv)
    x1_s, hre_s, him_s = _sample_tail(
        xs, o3, state_ssm_re[l].reshape(nb, N_STATES).T, state_ssm_im[l].reshape(nb, N_STATES).T,
        g1, wu, wg, bg, wbad, wbs, wo, lb, bblk.astype(BF), ctblk.astype(BF), d, wglu, bglu)
    y_p, y_s = _ffn(x1_p, x1_s, g2, w_ffn_gate[l], w_ffn_up[l], w_ffn_down[l], gf)
    y_p = y_p.reshape(b, t, D_MODEL)
    y_s = y_s.reshape(nb, 1, D_MODEL)

    kv_shape_p = (1, b, WINDOW, N_KV_HEADS, HEAD_DIM)
    st_shape_p = (1, b, N_SSM_GROUPS, SSM_STATE)
    kv_shape_s = (1, nb, WINDOW, N_KV_HEADS, HEAD_DIM)
    st_shape_s = (1, nb, N_SSM_GROUPS, SSM_STATE)
    return (y_p, y_s,
            kwin_p.reshape(kv_shape_p), vwin_p.reshape(kv_shape_p),
            hre_p.reshape(st_shape_p), him_p.reshape(st_shape_p),
            jnp.swapaxes(kwin_s, 1, 2).reshape(kv_shape_s), jnp.swapaxes(vwin_s, 1, 2).reshape(kv_shape_s),
            hre_s.T.reshape(st_shape_s), him_s.T.reshape(st_shape_s))
```

```python
import math

import jax
import jax.numpy as jnp
import numpy as np
from jax import lax
from jax.experimental import pallas as pl
from jax.experimental.pallas import tpu as pltpu

D_MODEL = 1024
N_Q_HEADS = 8
N_KV_HEADS = 2
HEAD_DIM = 64
ATTN_WIDTH = N_Q_HEADS * HEAD_DIM
KV_WIDTH = N_KV_HEADS * HEAD_DIM
WINDOW = 128
ROPE_DIM = HEAD_DIM // 4
ROPE_HALF = ROPE_DIM // 2
ROPE_THETA = 500000.0
SSM_WIDTH = D_MODEL // 2
SSM_GROUP = 16
N_SSM_GROUPS = SSM_WIDTH // SSM_GROUP
SSM_STATE = 64
N_STATES = N_SSM_GROUPS * SSM_STATE
GATE_WIDTH = 2 * D_MODEL
D_FF = -(-8 * D_MODEL // (3 * 256)) * 256
NORM_EPS = 1e-5
PAST_LEN = 8192

LANES = 128
CHUNK = 8
OCT = LANES // SSM_GROUP
N_OCT = N_SSM_GROUPS // OCT
OCT_STATES = OCT * SSM_STATE
OCT_COL = OCT_STATES // LANES
N_COL = N_STATES // LANES
S5_ROWS = 128
SEG = 16
M_PAIR = 2
SEG_PITCH = SEG * CHUNK + 8
MIX_TB = 512
FFN_TB = 1024
FFN_SPLIT = 4
NEG_BIG = -1e30
LOG2_E = math.log2(math.e)
VMEM_LIMIT = 56 * 1024 * 1024

BF = jnp.bfloat16
F32 = jnp.float32


def _dot(a, b):
    return jnp.dot(a, b, preferred_element_type=F32)


def _dot_nt(a, b):
    return lax.dot_general(a, b, (((1,), (1,)), ((), ())), preferred_element_type=F32)


def _dot_nt_split(a, b):
    a_hi, b_hi = a.astype(BF), b.astype(BF)
    a_lo = (a - a_hi.astype(F32)).astype(BF)
    b_lo = (b - b_hi.astype(F32)).astype(BF)
    return _dot_nt(a_hi, b_hi) + (_dot_nt(a_hi, b_lo) + _dot_nt(a_lo, b_hi))


def _rms(x, g):
    return x * lax.rsqrt(jnp.mean(x * x, axis=-1, keepdims=True) + NORM_EPS) * g


def _sigmoid(x):
    return 1.0 / (1.0 + jnp.exp(-x))


def _gelu_tanh(x):
    c = math.sqrt(2.0 / math.pi)
    return 0.5 * x * (1.0 + jnp.tanh(c * (x + 0.044715 * (x * x * x))))


def _rope(a, rc, rs1, rs2):
    return a * rc + pltpu.roll(a, ROPE_HALF, 1) * rs1 + pltpu.roll(a, LANES - ROPE_HALF, 1) * rs2


def _const_spec(shape):
    nd = len(shape)
    return pl.BlockSpec(shape, lambda *_: (0,) * nd, pipeline_mode=pl.Buffered(1))


def _build_chunk_operators(bblk_ref, ctblk_ref, lcol_ref, m_s, e_s, f_s):
    for o in range(N_OCT):
        ct = ctblk_ref[o]
        lr, li = lcol_ref[o, 0:1, :], lcol_ref[o, 1:2, :]
        er, ei = bblk_ref[o, :, :OCT_STATES], bblk_ref[o, :, OCT_STATES:]
        k_blk = []
        for tau in range(CHUNK):
            e_cat = jnp.concatenate([er, ei], axis=1)
            i = CHUNK - 1 - tau
            e_s[o, i * LANES:(i + 1) * LANES, :] = e_cat.astype(BF)
            k_blk.append(_dot_nt_split(e_cat, ct).astype(BF))
            er, ei = er * lr - ei * li, er * li + ei * lr
        zero = jnp.zeros((LANES, LANES), BF)
        for j in range(CHUNK):
            jt, jj = divmod(j, M_PAIR)
            for i in range(M_PAIR * (jt + 1)):
                m_s[jt][o, i * LANES:(i + 1) * LANES, jj * LANES:(jj + 1) * LANES] = (
                    k_blk[j - i] if j >= i else zero)
        tr, ti = ct[:, :OCT_STATES], -ct[:, OCT_STATES:]
        for j in range(CHUNK):
            tr, ti = tr * lr - ti * li, tr * li + ti * lr
            f_s[o, :OCT_STATES, j * LANES:(j + 1) * LANES] = tr.T.astype(BF)
            f_s[o, OCT_STATES:, j * LANES:(j + 1) * LANES] = (-ti).T.astype(BF)


def _s5_prompt_kernel(x_ref, g1_ref, wu_ref, bblk_ref, ctblk_ref, lcol_ref, a_ref, d_ref,
                      out_ref, hfin_ref,
                      m0_ref, m1_ref, m2_ref, m3_ref, e_ref, f_ref, us, ys, sre, sim, car):
    m_ref = (m0_ref, m1_ref, m2_ref, m3_ref)
    blk = pl.program_id(1)

    @pl.when((pl.program_id(0) == 0) & (blk == 0))
    def _():
        _build_chunk_operators(bblk_ref, ctblk_ref, lcol_ref, m_ref, e_ref, f_ref)

    @pl.when(blk == 0)
    def _():
        car[...] = jnp.zeros_like(car)

    seg_tokens = SEG * CHUNK
    seg_rows = [slice(s * SEG_PITCH, s * SEG_PITCH + seg_tokens) for s in range(8)]
    tok_rows = [slice(s * seg_tokens, (s + 1) * seg_tokens) for s in range(8)]

    for s in range(0, 8, 2):
        hn = _rms(x_ref[s * seg_tokens:(s + 2) * seg_tokens, :], g1_ref[...]).astype(BF)
        u = _dot(hn, wu_ref[...])
        for half in range(2):
            for cc in range(N_OCT):
                us[cc, seg_rows[s + half], :] = u[tok_rows[half], cc * LANES:(cc + 1) * LANES]

    sub = lax.broadcasted_iota(jnp.int32, (8, LANES), 0)

    def scan_column(c):
        tab = a_ref[c]
        are, aim = tab[0:1], tab[1:2]
        bre, bim = tab[2:3], tab[3:4]

        def step(cr, ci, r, keep_entering):
            slab = slice(r * 8, (r + 1) * 8)
            s_r, s_i = sre[c, slab, :], sim[c, slab, :]
            if keep_entering:
                sre[c, slab, :] = cr
                sim[c, slab, :] = ci
            return are * cr - aim * ci + s_r, are * ci + aim * cr + s_i

        cr = jnp.zeros((8, LANES), F32)
        ci = jnp.zeros((8, LANES), F32)
        for r in range(SEG):
            cr, ci = step(cr, ci, r, False)
        cv = car[c]
        pr, pi = cv[0:1], cv[1:2]
        sr = jnp.zeros((8, LANES), F32)
        si = jnp.zeros((8, LANES), F32)
        for s in range(8):
            sr = jnp.where(sub == s, pr, sr)
            si = jnp.where(sub == s, pi, si)
            pr, pi = (bre * pr - bim * pi + cr[s:s + 1], bre * pi + bim * pr + ci[s:s + 1])
        end = jnp.where(sub == 0, pr, jnp.where(sub == 1, pi, 0.0))
        car[c] = end
        hfin_ref[c] = end
        cr, ci = sr, si
        for r in range(SEG):
            cr, ci = step(cr, ci, r, True)

    for o in range(N_OCT):
        uo = jnp.concatenate(
            [jnp.concatenate([us[o, pl.ds(r * CHUNK + i, 8, stride=SEG_PITCH), :] for r in range(SEG)],
                             axis=0).astype(BF) for i in range(CHUNK)], axis=1)
        s_end = _dot(uo, e_ref[o])
        cols = range(o * OCT_COL, (o + 1) * OCT_COL)
        for cc, c in enumerate(cols):
            sre[c] = s_end[:, cc * LANES:(cc + 1) * LANES]
            sim[c] = s_end[:, OCT_STATES + cc * LANES:OCT_STATES + (cc + 1) * LANES]
        y_in = jnp.concatenate(
            [_dot(uo[:, :(jt + 1) * M_PAIR * LANES], m_ref[jt][o]) for jt in range(CHUNK // M_PAIR)],
            axis=1)
        for c in cols:
            scan_column(c)
        hp = jnp.concatenate([sre[c] for c in cols] + [sim[c] for c in cols], axis=1).astype(BF)
        yo = y_in + _dot(hp, f_ref[o])
        for r in range(SEG):
            for j in range(CHUNK):
                ys[o, pl.ds(r * CHUNK + j, 8, stride=SEG_PITCH), :] = (
                    yo[r * 8:(r + 1) * 8, j * LANES:(j + 1) * LANES])
        d_o = d_ref[:, o * LANES:(o + 1) * LANES]
        for s in range(8):
            out_ref[tok_rows[s], o * LANES:(o + 1) * LANES] = (
                ys[o, seg_rows[s], :] + d_o * us[o, seg_rows[s], :])


def _s5_prompt(x, g1, wu, bblk, ctblk, lcol, a_tab, d):
    b, t, _ = x.shape
    tb = S5_ROWS * CHUNK
    nblk = t // tb
    x2 = x.reshape(b * t, D_MODEL)
    row_map = lambda i, j: (i * nblk + j, 0)
    op_shape = (N_OCT, CHUNK * LANES, CHUNK * LANES)
    out, hfin = pl.pallas_call(
        _s5_prompt_kernel,
        grid=(b, nblk),
        in_specs=[
            pl.BlockSpec((tb, D_MODEL), row_map),
            _const_spec((1, D_MODEL)),
            _const_spec((D_MODEL, SSM_WIDTH)),
            _const_spec(bblk.shape),
            _const_spec(ctblk.shape),
            _const_spec(lcol.shape),
            _const_spec(a_tab.shape),
            _const_spec((1, SSM_WIDTH)),
        ],
        out_specs=[
            pl.BlockSpec((tb, SSM_WIDTH), row_map),
            pl.BlockSpec((None, N_COL, 8, LANES), lambda i, j: (i, 0, 0, 0)),
        ],
        out_shape=[
            jax.ShapeDtypeStruct((b * t, SSM_WIDTH), F32),
            jax.ShapeDtypeStruct((b, N_COL, 8, LANES), F32),
        ],
        scratch_shapes=[
            *[pltpu.VMEM((N_OCT, (jt + 1) * M_PAIR * LANES, M_PAIR * LANES), BF)
              for jt in range(CHUNK // M_PAIR)],
            pltpu.VMEM(op_shape, BF), pltpu.VMEM(op_shape, BF),
            pltpu.VMEM((N_OCT, 8 * SEG_PITCH, LANES), F32), pltpu.VMEM((N_OCT, 8 * SEG_PITCH, LANES), F32),
            pltpu.VMEM((N_COL, S5_ROWS, LANES), F32), pltpu.VMEM((N_COL, S5_ROWS, LANES), F32),
            pltpu.VMEM((N_COL, 8, LANES), F32),
        ],
        compiler_params=pltpu.CompilerParams(
            dimension_semantics=("arbitrary", "arbitrary"), vmem_limit_bytes=VMEM_LIMIT),
        name="s5_prompt",
    )(x2, g1, wu, bblk, ctblk, lcol, a_tab, d)
    return out, hfin[:, :, 0, :], hfin[:, :, 1, :]


def _gate_cols(hn, wg_ref, bg_ref, cols):
    return _sigmoid(_dot(hn, wg_ref[:, cols]) + bg_ref[:, cols])


def _merge_out(x, gates, attn_proj, ssm_proj, wo_ref):
    merged = gates[:, :D_MODEL] * attn_proj + gates[:, D_MODEL:] * ssm_proj
    return x + _dot(merged.astype(BF), wo_ref[...])


def _half_split(a, ar, lo):
    z = jnp.zeros_like(a)
    return (jnp.where(lo, a, z).astype(BF), jnp.where(lo, z, ar).astype(BF),
            jnp.where(lo, ar, z).astype(BF), jnp.where(lo, z, a).astype(BF))


def _softmax_terms_t(st, bias_t, sink):
    st = st + bias_t
    m = jnp.maximum(jnp.max(st, axis=0, keepdims=True), sink)
    return jnp.exp2(st - m).astype(BF), jnp.exp2(sink - m)


def _mix_prompt_kernel(sinks_ref, x_ref, y_ref, rc_ref, rs1_ref, rs2_ref, g1_ref, wqkv_ref,
                       wg_ref, bg_ref, wba_ref, wbs_ref, wo_ref, wglu_ref, bglu_ref,
                       x1_ref, kwin_ref, vwin_ref, kprev, vprev):
    t = pl.program_id(1)

    @pl.when(t == 0)
    def _():
        kprev[...] = jnp.zeros_like(kprev)
        vprev[...] = jnp.zeros_like(vprev)

    x = x_ref[...]
    hn = _rms(x, g1_ref[...]).astype(BF)
    qkv = _dot(hn, wqkv_ref[...])
    z = _gelu_tanh(y_ref[...])
    glu = _dot(z.astype(BF), wglu_ref[...]) + bglu_ref[...]
    rc, rs1, rs2 = rc_ref[...], rs1_ref[...], rs2_ref[...]
    scale = HEAD_DIM ** -0.5 * LOG2_E
    q = [(_rope(qkv[:, c * LANES:(c + 1) * LANES], rc, rs1, rs2) * scale).astype(BF)
         for c in range(ATTN_WIDTH // LANES)]
    k = _rope(qkv[:, ATTN_WIDTH:ATTN_WIDTH + KV_WIDTH], rc, rs1, rs2)
    v = qkv[:, ATTN_WIDTH + KV_WIDTH:ATTN_WIDTH + 2 * KV_WIDTH]
    ssm = (z * _sigmoid(glu)).astype(BF)
    ssm_proj = _dot(ssm, wbs_ref[...])

    w = WINDOW
    n_sub = MIX_TB // w
    lane = lax.broadcasted_iota(jnp.int32, (2 * w, LANES), 1)
    lo = lane < HEAD_DIM
    kj = lax.broadcasted_iota(jnp.int32, (2 * w, w), 0)
    qi = lax.broadcasted_iota(jnp.int32, (2 * w, w), 1)
    band = (kj > qi) & (kj <= qi + w)
    first = band & ((kj >= w) | (t > 0))
    bias_band = jnp.where(band, 0.0, NEG_BIG).astype(F32)
    bias_first = jnp.where(first, 0.0, NEG_BIG).astype(F32)
    col = lax.broadcasted_iota(jnp.int32, (1, 2 * w), 1)
    ones_row = (lax.broadcasted_iota(jnp.int32, (HEAD_DIM, 2 * w), 0) == 0).astype(F32)

    def sink_row(ha, hb):
        return jnp.where(col < w, sinks_ref[ha], sinks_ref[hb]) * LOG2_E

    gate_w = GATE_WIDTH // (n_sub * N_KV_HEADS)
    gate_cols = []
    attn_rows = []
    for sb in range(n_sub):
        cur = slice(sb * w, (sb + 1) * w)
        if sb == 0:
            kcat = jnp.concatenate([kprev[...], k[cur]], axis=0)
            vcat = jnp.concatenate([vprev[...], v[cur]], axis=0)
            bias = bias_first
        else:
            kcat = k[(sb - 1) * w:(sb + 1) * w]
            vcat = v[(sb - 1) * w:(sb + 1) * w]
            bias = bias_band
        bias2 = jnp.concatenate([bias, bias], axis=1)
        ka, kb, kc, kd = _half_split(kcat, pltpu.roll(kcat, HEAD_DIM, 1), lo)
        vt = vcat.T
        v_rows = [(jnp.concatenate([vt[g * HEAD_DIM:(g + 1) * HEAD_DIM], ones_row], axis=0).astype(BF),
                   jnp.concatenate([ones_row, vt[g * HEAD_DIM:(g + 1) * HEAD_DIM]], axis=0).astype(BF))
                  for g in range(N_KV_HEADS)]
        outs = []
        for grp, (k_e, k_o) in enumerate(((ka, kb), (kc, kd))):
            qq = jnp.concatenate([q[2 * grp][cur], q[2 * grp + 1][cur]], axis=0)
            s_e, s_o = _dot_nt(k_e, qq), _dot_nt(k_o, qq)
            gc = sb * N_KV_HEADS + grp
            gate_cols.append(_gate_cols(hn, wg_ref, bg_ref, slice(gc * gate_w, (gc + 1) * gate_w)))
            p_e, sink_e = _softmax_terms_t(s_e, bias2, sink_row(4 * grp, 4 * grp + 2))
            p_o, sink_o = _softmax_terms_t(s_o, bias2, sink_row(4 * grp + 1, 4 * grp + 3))
            v_e, v_o = v_rows[grp]
            d_e, d_o = _dot(v_e, p_e), _dot(v_o, p_o)
            o2 = jnp.concatenate(
                [d_e[:HEAD_DIM] * (1.0 / (d_e[HEAD_DIM:HEAD_DIM + 1] + sink_e)),
                 d_o[HEAD_DIM:] * (1.0 / (d_o[0:1] + sink_o))], axis=0)
            outs += [o2[:, :w].T, o2[:, w:].T]
        attn_rows.append(jnp.concatenate(outs, axis=1))
    attn = jnp.concatenate(attn_rows, axis=0).astype(BF)
    gates = jnp.concatenate(gate_cols, axis=1)

    kprev[...] = k[MIX_TB - w:]
    vprev[...] = v[MIX_TB - w:]
    kwin_ref[...] = k[MIX_TB - w:]
    vwin_ref[...] = v[MIX_TB - w:]
    x1_ref[...] = _merge_out(x, gates, _dot(attn, wba_ref[...]), ssm_proj, wo_ref)


def _mix_prompt(x, y_ssm, rc, rs1, rs2, sinks, g1, wqkv, wg, bg, wba, wbs, wo, wglu, bglu):
    b, t, _ = x.shape
    nblk = t // MIX_TB
    x2 = x.reshape(b * t, D_MODEL)
    row_map = lambda i, j: (i * nblk + j, 0)
    x1, kwin, vwin = pl.pallas_call(
        _mix_prompt_kernel,
        grid=(b, nblk),
        in_specs=[
            pl.BlockSpec(memory_space=pltpu.SMEM),
            pl.BlockSpec((MIX_TB, D_MODEL), row_map),
            pl.BlockSpec((MIX_TB, SSM_WIDTH), row_map),
            pl.BlockSpec((MIX_TB, LANES), lambda i, j: (j, 0)),
            pl.BlockSpec((MIX_TB, LANES), lambda i, j: (j, 0)),
            pl.BlockSpec((MIX_TB, LANES), lambda i, j: (j, 0)),
            _const_spec((1, D_MODEL)),
            _const_spec(wqkv.shape),
            _const_spec(wg.shape),
            _const_spec(bg.shape),
            _const_spec(wba.shape),
            _const_spec(wbs.shape),
            _const_spec(wo.shape),
            _const_spec(wglu.shape),
            _const_spec(bglu.shape),
        ],
        out_specs=[
            pl.BlockSpec((MIX_TB, D_MODEL), row_map),
            pl.BlockSpec((None, WINDOW, KV_WIDTH), lambda i, j: (i, 0, 0)),
            pl.BlockSpec((None, WINDOW, KV_WIDTH), lambda i, j: (i, 0, 0)),
        ],
        out_shape=[
            jax.ShapeDtypeStruct((b * t, D_MODEL), F32),
            jax.ShapeDtypeStruct((b, WINDOW, KV_WIDTH), F32),
            jax.ShapeDtypeStruct((b, WINDOW, KV_WIDTH), F32),
        ],
        scratch_shapes=[pltpu.VMEM((WINDOW, KV_WIDTH), F32), pltpu.VMEM((WINDOW, KV_WIDTH), F32)],
        compiler_params=pltpu.CompilerParams(
            dimension_semantics=("arbitrary", "arbitrary"), vmem_limit_bytes=VMEM_LIMIT),
        name="mix_prompt",
    )(sinks, x2, y_ssm, rc, rs1, rs2, g1, wqkv, wg, bg, wba, wbs, wo, wglu, bglu)
    return x1, kwin, vwin


QROWS = 16
SAMPLE_TB = 32


def _sample_attn_kernel(x_ref, kbuf_ref, vbuf_ref, rc_ref, rs1_ref, rs2_ref, sinkc_ref, g1_ref, wqkv_ref,
                        o_ref, kout_ref, vout_ref, qz):
    nb = x_ref.shape[0]
    hn = _rms(x_ref[...], g1_ref[...]).astype(BF)
    qkv = _dot(hn, wqkv_ref[...])
    rc, rs1, rs2 = rc_ref[...], rs1_ref[...], rs2_ref[...]
    scale = HEAD_DIM ** -0.5
    k_new = _rope(qkv[:, ATTN_WIDTH:ATTN_WIDTH + KV_WIDTH], rc, rs1, rs2)
    v_new = qkv[:, ATTN_WIDTH + KV_WIDTH:ATTN_WIDTH + 2 * KV_WIDTH]
    pad = jnp.zeros((LANES - nb, KV_WIDTH), F32)
    k_new_t = jnp.concatenate([k_new, pad], axis=0).T
    v_new_t = jnp.concatenate([v_new, pad], axis=0).T

    lane = lax.broadcasted_iota(jnp.int32, (nb, LANES), 1)
    lo = lane < HEAD_DIM
    qz[...] = jnp.zeros_like(qz)
    for c in range(ATTN_WIDTH // LANES):
        qc = _rope(qkv[:, c * LANES:(c + 1) * LANES], rc, rs1, rs2) * scale
        qr = pltpu.roll(qc, HEAD_DIM, 1)
        zero = jnp.zeros_like(qc)
        if c < 2:
            even, odd = jnp.where(lo, qc, zero), jnp.where(lo, qr, zero)
        else:
            even, odd = jnp.where(lo, zero, qr), jnp.where(lo, zero, qc)
        qz[pl.ds(2 * c, nb, stride=QROWS), :] = even
        qz[pl.ds(2 * c + 1, nb, stride=QROWS), :] = odd

    last = lax.broadcasted_iota(jnp.int32, (KV_WIDTH, WINDOW), 1) == WINDOW - 1
    for b in range(nb):
        kout_ref[b] = jnp.where(last, k_new_t[:, b:b + 1], pltpu.roll(kbuf_ref[b], WINDOW - 1, 1))
        vout_ref[b] = jnp.where(last, v_new_t[:, b:b + 1], pltpu.roll(vbuf_ref[b], WINDOW - 1, 1))

    sink = sinkc_ref[...]
    q3 = qz[...].reshape(nb, QROWS, LANES).astype(BF)
    s = jnp.einsum('bhd,bdk->bhk', q3, kout_ref[...].astype(BF), preferred_element_type=F32)
    m = jnp.maximum(jnp.max(s, axis=-1, keepdims=True), sink)
    p = jnp.exp(s - m)
    den = jnp.sum(p, axis=-1, keepdims=True) + jnp.exp(sink - m)
    p = (p * (1.0 / den)).astype(BF)
    o3 = jnp.einsum('bhk,bdk->bhd', p, vout_ref[...].astype(BF), preferred_element_type=F32)
    o_ref[...] = o3.reshape(nb * QROWS, LANES)


def _sample_attn(x, kbuf, vbuf, rc, rs1, rs2, sinkc, g1, wqkv):
    nb = x.shape[0]
    tb = SAMPLE_TB
    return pl.pallas_call(
        _sample_attn_kernel,
        grid=(nb // tb,),
        in_specs=[
            pl.BlockSpec((tb, D_MODEL), lambda i: (i, 0)),
            pl.BlockSpec((tb, KV_WIDTH, WINDOW), lambda i: (i, 0, 0)),
            pl.BlockSpec((tb, KV_WIDTH, WINDOW), lambda i: (i, 0, 0)),
            _const_spec(rc.shape), _const_spec(rs1.shape), _const_spec(rs2.shape),
            _const_spec(sinkc.shape), _const_spec(g1.shape), _const_spec(wqkv.shape),
        ],
        out_specs=[
            pl.BlockSpec((tb * QROWS, LANES), lambda i: (i, 0)),
            pl.BlockSpec((tb, KV_WIDTH, WINDOW), lambda i: (i, 0, 0)),
            pl.BlockSpec((tb, KV_WIDTH, WINDOW), lambda i: (i, 0, 0)),
        ],
        out_shape=[
            jax.ShapeDtypeStruct((nb * QROWS, LANES), F32),
            jax.ShapeDtypeStruct((nb, KV_WIDTH, WINDOW), F32),
            jax.ShapeDtypeStruct((nb, KV_WIDTH, WINDOW), F32),
        ],
        scratch_shapes=[pltpu.VMEM((tb * QROWS, LANES), F32)],
        compiler_params=pltpu.CompilerParams(
            dimension_semantics=("arbitrary",), vmem_limit_bytes=VMEM_LIMIT),
        name="sample_attn",
    )(x, kbuf, vbuf, rc, rs1, rs2, sinkc, g1, wqkv)


def _sample_tail_kernel(x_ref, o3_ref, h0re_ref, h0im_ref, g1_ref, wu_ref, wg_ref, bg_ref, wbad_ref,
                        wbs_ref, wo_ref, lb_ref, bblk_ref, ctblk_ref, d_ref, wglu_ref, bglu_ref,
                        x1_ref, hre_ref, him_ref):
    nb = x_ref.shape[0]
    x = x_ref[...]
    hn = _rms(x, g1_ref[...]).astype(BF)

    lane = lax.broadcasted_iota(jnp.int32, (nb, LANES), 1)
    lo = lane < HEAD_DIM
    a = jnp.zeros((nb, D_MODEL), F32)
    zero = jnp.zeros((nb, LANES), F32)
    for h in range(N_Q_HEADS):
        oh = o3_ref[pl.ds(h, nb, stride=QROWS), :]
        oh = jnp.where(lo, oh, zero) if h < N_Q_HEADS // 2 else jnp.where(lo, zero, oh)
        a = a + _dot(oh.astype(BF), wbad_ref[h])

    u = _dot(hn, wu_ref[...])
    ub = u.astype(BF)
    lre, lim = lb_ref[0:1, :], lb_ref[1:2, :]
    y_cols = []
    for o in range(N_OCT):
        sl = slice(o * OCT_STATES, (o + 1) * OCT_STATES)
        bu = _dot(ub[:, o * LANES:(o + 1) * LANES], bblk_ref[o])
        blocks = [slice(c * LANES, (c + 1) * LANES) for c in range(o * OCT_COL, (o + 1) * OCT_COL)]
        h0r = jnp.concatenate([h0re_ref[rows, :].T for rows in blocks], axis=1)
        h0i = jnp.concatenate([h0im_ref[rows, :].T for rows in blocks], axis=1)
        hr = bu[:, :OCT_STATES] + (lre[:, sl] * h0r - lim[:, sl] * h0i)
        hi = bu[:, OCT_STATES:] + (lre[:, sl] * h0i + lim[:, sl] * h0r)
        for cc, rows in enumerate(blocks):
            hre_ref[rows, :] = hr[:, cc * LANES:(cc + 1) * LANES].T
            him_ref[rows, :] = hi[:, cc * LANES:(cc + 1) * LANES].T
        y_cols.append(_dot_nt(jnp.concatenate([hr, hi], axis=1).astype(BF), ctblk_ref[o]))
    y = jnp.concatenate(y_cols, axis=1) + d_ref[...] * u
    z = _gelu_tanh(y)
    gate = _dot(z.astype(BF), wglu_ref[...]) + bglu_ref[...]
    ssm = (z * _sigmoid(gate)).astype(BF)

    gates = _gate_cols(hn, wg_ref, bg_ref, slice(0, GATE_WIDTH))
    x1_ref[...] = _merge_out(x, gates, a, _dot(ssm, wbs_ref[...]), wo_ref)


def _sample_tail(x, o3, h0re, h0im, g1, wu, wg, bg, wbad, wbs, wo, lb, bblk, ctblk, d, wglu, bglu):
    nb = x.shape[0]
    args = (x, o3, h0re, h0im, g1, wu, wg, bg, wbad, wbs, wo, lb, bblk, ctblk, d, wglu, bglu)
    out_shapes = ((nb, D_MODEL), (N_STATES, nb), (N_STATES, nb))
    return pl.pallas_call(
        _sample_tail_kernel,
        grid=(1,),
        in_specs=[_const_spec(a.shape) for a in args],
        out_specs=[pl.BlockSpec(s, lambda i: (0, 0)) for s in out_shapes],
        out_shape=[jax.ShapeDtypeStruct(s, F32) for s in out_shapes],
        compiler_params=pltpu.CompilerParams(
            dimension_semantics=("arbitrary",), vmem_limit_bytes=VMEM_LIMIT),
        name="sample_tail",
    )(*args)


FFN_W_CHUNKS = 8


def _weight_copy(w_hbm, stage, sem, k, rows):
    slot = k % 2
    return pltpu.make_async_copy(w_hbm.at[pl.ds(k * rows, rows), :], stage.at[slot], sem.at[slot])


def _load_cast_weight(w_hbm, w_bf, stage, sem):
    rows = stage.shape[1]
    n = w_hbm.shape[0] // rows
    _weight_copy(w_hbm, stage, sem, 0, rows).start()
    for k in range(n):
        if k + 1 < n:
            _weight_copy(w_hbm, stage, sem, k + 1, rows).start()
        _weight_copy(w_hbm, stage, sem, k, rows).wait()
        w_bf[k * rows:(k + 1) * rows, :] = stage[k % 2].astype(BF)


def _ffn_rows(x, g2_ref, wgate, wup, wdown, gf_ref):
    h = _rms(x, g2_ref[...]).astype(BF)
    gate = _dot(h, wgate[...])
    up = _dot(h, wup[...])
    act = (gate * _sigmoid(gate) * up).astype(BF)
    x2 = x + _dot(act, wdown[...])
    return _rms(x2, gf_ref[...])


def _ffn_kernel(xp_ref, xs_ref, g2_ref, wgate_hbm, wup_hbm, wdown_hbm, gf_ref, yp_ref, ys_ref,
                wgate, wup, wdown, stage_in, stage_out, sem):
    i = pl.program_id(0)
    n_prompt = pl.num_programs(0) - 1

    @pl.when(i == 0)
    def _():
        _load_cast_weight(wgate_hbm, wgate, stage_in, sem)
        _load_cast_weight(wup_hbm, wup, stage_in, sem)
        _load_cast_weight(wdown_hbm, wdown, stage_out, sem)

    @pl.when(i < n_prompt)
    def _():
        half = FFN_TB // FFN_SPLIT
        for rows in (slice(h * half, (h + 1) * half) for h in range(FFN_SPLIT)):
            yp_ref[rows, :] = _ffn_rows(xp_ref[rows, :], g2_ref, wgate, wup, wdown, gf_ref)

    @pl.when(i == n_prompt)
    def _():
        ys_ref[...] = _ffn_rows(xs_ref[...], g2_ref, wgate, wup, wdown, gf_ref)


def _ffn(xp, xs, g2, wgate, wup, wdown, gf):
    n, ns = xp.shape[0], xs.shape[0]
    n_prompt = n // FFN_TB
    prompt_map = lambda i: (jnp.minimum(i, n_prompt - 1), 0)
    return pl.pallas_call(
        _ffn_kernel,
        grid=(n_prompt + 1,),
        in_specs=[
            pl.BlockSpec((FFN_TB, D_MODEL), prompt_map),
            _const_spec((ns, D_MODEL)),
            _const_spec((1, D_MODEL)),
            pl.BlockSpec(memory_space=pl.ANY),
            pl.BlockSpec(memory_space=pl.ANY),
            pl.BlockSpec(memory_space=pl.ANY),
            _const_spec((1, D_MODEL)),
        ],
        out_specs=[
            pl.BlockSpec((FFN_TB, D_MODEL), prompt_map),
            pl.BlockSpec((ns, D_MODEL), lambda i: (0, 0)),
        ],
        out_shape=[
            jax.ShapeDtypeStruct((n, D_MODEL), F32),
            jax.ShapeDtypeStruct((ns, D_MODEL), F32),
        ],
        scratch_shapes=[
            pltpu.VMEM((D_MODEL, D_FF), BF), pltpu.VMEM((D_MODEL, D_FF), BF), pltpu.VMEM((D_FF, D_MODEL), BF),
            pltpu.VMEM((2, D_MODEL // FFN_W_CHUNKS, D_FF), F32),
            pltpu.VMEM((2, D_FF // FFN_W_CHUNKS, D_MODEL), F32),
            pltpu.SemaphoreType.DMA((2,)),
        ],
        compiler_params=pltpu.CompilerParams(
            dimension_semantics=("arbitrary",), vmem_limit_bytes=VMEM_LIMIT),
        name="ffn",
    )(xp, xs, g2, wgate, wup, wdown, gf)


def _rope_tables(pos):
    pos = np.asarray(pos, np.float64)
    inv_freq = ROPE_THETA ** (-(np.arange(ROPE_HALF, dtype=np.float64) * 2.0 / ROPE_DIM))
    ang = pos[:, None] * inv_freq[None, :]
    cos, sin = np.cos(ang), np.sin(ang)
    pad = np.zeros((pos.shape[0], HEAD_DIM - ROPE_DIM))
    zero = np.zeros_like(sin)
    rc = np.concatenate([cos, cos, pad + 1.0], axis=1)
    rs1 = np.concatenate([zero, sin, pad], axis=1)
    rs2 = np.concatenate([-sin, zero, pad], axis=1)
    rep = LANES // HEAD_DIM
    return tuple(jnp.asarray(np.tile(a, (1, rep)), F32) for a in (rc, rs1, rs2))


def _cmul(ar, ai, br, bi):
    return ar * br - ai * bi, ar * bi + ai * br


def _ssm_tables(lam_re, lam_im, log_dt, b_re, b_im, c_re, c_im):
    dt = jnp.exp(log_dt)[:, None]
    mag = jnp.exp(lam_re * dt)
    lb_re = mag * jnp.cos(lam_im * dt)
    lb_im = mag * jnp.sin(lam_im * dt)
    den = lam_re * lam_re + lam_im * lam_im
    nr = lb_re - 1.0
    k_re = ((nr * lam_re + lb_im * lam_im) / den)[..., None]
    k_im = ((lb_im * lam_re - nr * lam_im) / den)[..., None]
    bb_re = k_re * b_re - k_im * b_im
    bb_im = k_re * b_im + k_im * b_re

    a_re, a_im = lb_re, lb_im
    for _ in range(int(math.log2(CHUNK))):
        a_re, a_im = _cmul(a_re, a_im, a_re, a_im)
    s_re, s_im = a_re, a_im
    for _ in range(int(math.log2(SEG))):
        s_re, s_im = _cmul(s_re, s_im, s_re, s_im)

    eye = jnp.eye(OCT, dtype=F32).reshape(1, OCT, 1, OCT, 1)

    def block_diag(a):
        r, c = a.shape[1:]
        return (a.reshape(N_OCT, OCT, r, 1, c) * eye).reshape(N_OCT, OCT * r, OCT * c)

    bblk = jnp.concatenate([block_diag(jnp.swapaxes(bb_re, 1, 2)),
                            block_diag(jnp.swapaxes(bb_im, 1, 2))], axis=2)
    ctblk = jnp.concatenate([block_diag(c_re), block_diag(-c_im)], axis=2)

    oct_cols = lambda a: a.reshape(N_OCT, 1, OCT_STATES)
    lcol = jnp.concatenate([oct_cols(lb_re), oct_cols(lb_im),
                            jnp.zeros((N_OCT, 6, OCT_STATES), F32)], axis=1)

    flat = lambda a: a.reshape(1, N_STATES)
    col = lambda a: a.reshape(N_COL, 1, LANES)
    a_tab = jnp.concatenate([col(a_re), col(a_im), col(s_re), col(s_im),
                             jnp.zeros((N_COL, 4, LANES), F32)], axis=1)
    lb = jnp.concatenate([flat(lb_re), flat(lb_im)], axis=0)
    return bblk, ctblk, lcol, a_tab, lb


def kernel(x_prompt, x_sample, state_k_win, state_v_win, state_ssm_re, state_ssm_im, norm1_g, w_in, b_gate, attn_sinks, ssm_lam_re, ssm_lam_im, ssm_log_dt, ssm_b_re, ssm_b_im, ssm_c_re, ssm_c_im, ssm_d, w_glu, b_glu, w_branch_attn, w_branch_ssm, w_out, norm2_g, w_ffn_gate, w_ffn_up, w_ffn_down, norm_f_g):
    depth = w_in.shape[0]
    assert depth == 1
    b, t, _ = x_prompt.shape
    nb, s_len, _ = x_sample.shape
    assert s_len == 1 and state_k_win.shape[2] == WINDOW
    l = 0
    o1 = ATTN_WIDTH + 2 * KV_WIDTH
    o2 = o1 + SSM_WIDTH
    wqkv, wu, wg = (w_in[l, :, :o1].astype(BF), w_in[l, :, o1:o2].astype(BF), w_in[l, :, o2:].astype(BF))
    g1 = norm1_g[l].reshape(1, D_MODEL)
    g2 = norm2_g[l].reshape(1, D_MODEL)
    gf = norm_f_g.reshape(1, D_MODEL)
    bg = b_gate[l].reshape(1, GATE_WIDTH)
    d = ssm_d[l].reshape(1, SSM_WIDTH)
    wglu = w_glu[l].astype(BF)
    bglu = b_glu[l].reshape(1, SSM_WIDTH)
    wba = w_branch_attn[l].astype(BF)
    wbs = w_branch_ssm[l].astype(BF)
    wo = w_out[l].astype(BF)
    sinks = attn_sinks[l]

    bblk, ctblk, lcol, a_tab, lb = _ssm_tables(
        ssm_lam_re[l], ssm_lam_im[l], ssm_log_dt[l], ssm_b_re[l], ssm_b_im[l], ssm_c_re[l], ssm_c_im[l])

    rc, rs1, rs2 = _rope_tables(np.arange(t))
    yssm_p, hre_p, him_p = _s5_prompt(x_prompt, g1, wu, bblk, ctblk, lcol, a_tab, d)
    x1_p, kwin_p, vwin_p = _mix_prompt(x_prompt, yssm_p, rc, rs1, rs2, sinks, g1, wqkv, wg, bg, wba, wbs, wo,
                                       wglu, bglu)

    rcs, rs1s, rs2s = _rope_tables(PAST_LEN + np.arange(1))
    sinkc = jnp.concatenate([sinks, jnp.zeros((QROWS - N_Q_HEADS,), F32)]).reshape(QROWS, 1)
    wbad = jnp.concatenate([wba.reshape(N_Q_HEADS, HEAD_DIM, D_MODEL)] * 2, axis=1)
    xs = x_sample.reshape(nb, D_MODEL)
    key_minor = lambda a: jnp.swapaxes(a.reshape(nb, WINDOW, KV_WIDTH), 1, 2)
    o3, kwin_s, vwin_s = _sample_attn(
        xs, key_minor(state_k_win[l]), key_minor(state_v_win[l]), rcs, rs1s, rs2s, sinkc, g1, wqkv)
    x1_s, hre_s, him_s = _sample_tail(
        xs, o3, state_ssm_re[l].reshape(nb, N_STATES).T, state_ssm_im[l].reshape(nb, N_STATES).T,
        g1, wu, wg, bg, wbad, wbs, wo, lb, bblk.astype(BF), ctblk.astype(BF), d, wglu, bglu)
    y_p, y_s = _ffn(x1_p, x1_s, g2, w_ffn_gate[l], w_ffn_up[l], w_ffn_down[l], gf)
    y_p = y_p.reshape(b, t, D_MODEL)
    y_s = y_s.reshape(nb, 1, D_MODEL)

    kv_shape_p = (1, b, WINDOW, N_KV_HEADS, HEAD_DIM)
    st_shape_p = (1, b, N_SSM_GROUPS, SSM_STATE)
    kv_shape_s = (1, nb, WINDOW, N_KV_HEADS, HEAD_DIM)
    st_shape_s = (1, nb, N_SSM_GROUPS, SSM_STATE)
    return (y_p, y_s,
            kwin_p.reshape(kv_shape_p), vwin_p.reshape(kv_shape_p),
            hre_p.reshape(st_shape_p), him_p.reshape(st_shape_p),
            jnp.swapaxes(kwin_s, 1, 2).reshape(kv_shape_s), jnp.swapaxes(vwin_s, 1, 2).reshape(kv_shape_s),
            hre_s.T.reshape(st_shape_s), him_s.T.reshape(st_shape_s))
```

```python
import math

import jax
import jax.numpy as jnp
import numpy as np
from jax import lax
from jax.experimental import pallas as pl
from jax.experimental.pallas import tpu as pltpu

D_MODEL = 1024
N_Q_HEADS = 8
N_KV_HEADS = 2
HEAD_DIM = 64
ATTN_WIDTH = N_Q_HEADS * HEAD_DIM
KV_WIDTH = N_KV_HEADS * HEAD_DIM
WINDOW = 128
ROPE_DIM = HEAD_DIM // 4
ROPE_HALF = ROPE_DIM // 2
ROPE_THETA = 500000.0
SSM_WIDTH = D_MODEL // 2
SSM_GROUP = 16
N_SSM_GROUPS = SSM_WIDTH // SSM_GROUP
SSM_STATE = 64
N_STATES = N_SSM_GROUPS * SSM_STATE
GATE_WIDTH = 2 * D_MODEL
QKV_WIDTH = ATTN_WIDTH + 2 * KV_WIDTH
U_COL0 = QKV_WIDTH
GATE_COL0 = U_COL0 + SSM_WIDTH
IN_WIDTH = GATE_COL0 + GATE_WIDTH
D_FF = -(-8 * D_MODEL // (3 * 256)) * 256
NORM_EPS = 1e-5
PAST_LEN = 8192

LANES = 128
CHUNK = 8
OCT = LANES // SSM_GROUP
N_OCT = N_SSM_GROUPS // OCT
OCT_STATES = OCT * SSM_STATE
OCT_COL = OCT_STATES // LANES
N_COL = N_STATES // LANES
S5_ROWS = 128
SEG = 16
M_PAIR = 2
U_HALF = SSM_WIDTH // 2
SEG_PITCH = SEG * CHUNK + 8
MIX_TB = 512
FFN_TB = 1024
FFN_SPLIT = 4
NEG_BIG = -1e30
LOG2_E = math.log2(math.e)
VMEM_LIMIT = 56 * 1024 * 1024

BF = jnp.bfloat16
F32 = jnp.float32


def _dot(a, b):
    return jnp.dot(a, b, preferred_element_type=F32)


def _dot_nt(a, b):
    return lax.dot_general(a, b, (((1,), (1,)), ((), ())), preferred_element_type=F32)


def _dot_nt_split(a, b):
    a_hi, b_hi = a.astype(BF), b.astype(BF)
    a_lo = (a - a_hi.astype(F32)).astype(BF)
    b_lo = (b - b_hi.astype(F32)).astype(BF)
    return _dot_nt(a_hi, b_hi) + (_dot_nt(a_hi, b_lo) + _dot_nt(a_lo, b_hi))


def _rms(x, g):
    return x * lax.rsqrt(jnp.mean(x * x, axis=-1, keepdims=True) + NORM_EPS) * g


def _sigmoid(x):
    return 1.0 / (1.0 + jnp.exp(-x))


def _gelu_tanh(x):
    c = math.sqrt(2.0 / math.pi)
    return 0.5 * x * (1.0 + jnp.tanh(c * (x + 0.044715 * (x * x * x))))


def _rope(a, rc, rs1, rs2):
    return a * rc + pltpu.roll(a, ROPE_HALF, 1) * rs1 + pltpu.roll(a, LANES - ROPE_HALF, 1) * rs2


def _const_spec(shape):
    nd = len(shape)
    return pl.BlockSpec(shape, lambda *_: (0,) * nd, pipeline_mode=pl.Buffered(1))


def _build_chunk_operators(bblk_ref, ctblk_ref, lcol_ref, m_s, e_s, f_s):
    for o in range(N_OCT):
        ct = ctblk_ref[o]
        lr, li = lcol_ref[o, 0:1, :], lcol_ref[o, 1:2, :]
        er, ei = bblk_ref[o, :, :OCT_STATES], bblk_ref[o, :, OCT_STATES:]
        k_blk = []
        for tau in range(CHUNK):
            e_cat = jnp.concatenate([er, ei], axis=1)
            i = CHUNK - 1 - tau
            e_s[o, i * LANES:(i + 1) * LANES, :] = e_cat.astype(BF)
            k_blk.append(_dot_nt_split(e_cat, ct).astype(BF))
            er, ei = er * lr - ei * li, er * li + ei * lr
        zero = jnp.zeros((LANES, LANES), BF)
        for j in range(CHUNK):
            jt, jj = divmod(j, M_PAIR)
            for i in range(M_PAIR * (jt + 1)):
                m_s[jt][o, i * LANES:(i + 1) * LANES, jj * LANES:(jj + 1) * LANES] = (
                    k_blk[j - i] if j >= i else zero)
        tr, ti = ct[:, :OCT_STATES], -ct[:, OCT_STATES:]
        for j in range(CHUNK):
            tr, ti = tr * lr - ti * li, tr * li + ti * lr
            f_s[o, :OCT_STATES, j * LANES:(j + 1) * LANES] = tr.T.astype(BF)
            f_s[o, OCT_STATES:, j * LANES:(j + 1) * LANES] = (-ti).T.astype(BF)


def _s5_prompt_kernel(x_ref, g1_ref, wu0_ref, wu1_ref, bblk_ref, ctblk_ref, lcol_ref, a_ref, d_ref,
                      out_ref, hfin_ref,
                      m0_ref, m1_ref, m2_ref, m3_ref, e_ref, f_ref, us, ys, sre, sim, car):
    m_ref = (m0_ref, m1_ref, m2_ref, m3_ref)
    blk = pl.program_id(1)

    @pl.when((pl.program_id(0) == 0) & (blk == 0))
    def _():
        _build_chunk_operators(bblk_ref, ctblk_ref, lcol_ref, m_ref, e_ref, f_ref)

    @pl.when(blk == 0)
    def _():
        car[...] = jnp.zeros_like(car)

    seg_tokens = SEG * CHUNK
    seg_rows = [slice(s * SEG_PITCH, s * SEG_PITCH + seg_tokens) for s in range(8)]
    tok_rows = [slice(s * seg_tokens, (s + 1) * seg_tokens) for s in range(8)]

    for s in range(0, 8, 2):
        hn = _rms(x_ref[s * seg_tokens:(s + 2) * seg_tokens, :], g1_ref[...]).astype(BF)
        u = jnp.concatenate([_dot(hn, wu0_ref[...]), _dot(hn, wu1_ref[...])], axis=1)
        for half in range(2):
            for cc in range(N_OCT):
                us[cc, seg_rows[s + half], :] = u[tok_rows[half], cc * LANES:(cc + 1) * LANES]

    sub = lax.broadcasted_iota(jnp.int32, (8, LANES), 0)

    def scan_column(c):
        tab = a_ref[c]
        are, aim = tab[0:1], tab[1:2]
        bre, bim = tab[2:3], tab[3:4]

        def step(cr, ci, r, keep_entering):
            slab = slice(r * 8, (r + 1) * 8)
            s_r, s_i = sre[c, slab, :], sim[c, slab, :]
            if keep_entering:
                sre[c, slab, :] = cr
                sim[c, slab, :] = ci
            return are * cr - aim * ci + s_r, are * ci + aim * cr + s_i

        cr = jnp.zeros((8, LANES), F32)
        ci = jnp.zeros((8, LANES), F32)
        for r in range(SEG):
            cr, ci = step(cr, ci, r, False)
        cv = car[c]
        pr, pi = cv[0:1], cv[1:2]
        sr = jnp.zeros((8, LANES), F32)
        si = jnp.zeros((8, LANES), F32)
        for s in range(8):
            sr = jnp.where(sub == s, pr, sr)
            si = jnp.where(sub == s, pi, si)
            pr, pi = (bre * pr - bim * pi + cr[s:s + 1], bre * pi + bim * pr + ci[s:s + 1])
        end = jnp.where(sub == 0, pr, jnp.where(sub == 1, pi, 0.0))
        car[c] = end
        hfin_ref[c] = end
        cr, ci = sr, si
        for r in range(SEG):
            cr, ci = step(cr, ci, r, True)

    for o in range(N_OCT):
        uo = jnp.concatenate(
            [jnp.concatenate([us[o, pl.ds(r * CHUNK + i, 8, stride=SEG_PITCH), :] for r in range(SEG)],
                             axis=0).astype(BF) for i in range(CHUNK)], axis=1)
        s_end = _dot(uo, e_ref[o])
        cols = range(o * OCT_COL, (o + 1) * OCT_COL)
        for cc, c in enumerate(cols):
            sre[c] = s_end[:, cc * LANES:(cc + 1) * LANES]
            sim[c] = s_end[:, OCT_STATES + cc * LANES:OCT_STATES + (cc + 1) * LANES]
        y_in = jnp.concatenate(
            [_dot(uo[:, :(jt + 1) * M_PAIR * LANES], m_ref[jt][o]) for jt in range(CHUNK // M_PAIR)],
            axis=1)
        for c in cols:
            scan_column(c)
        hp = jnp.concatenate([sre[c] for c in cols] + [sim[c] for c in cols], axis=1).astype(BF)
        yo = y_in + _dot(hp, f_ref[o])
        for r in range(SEG):
            for j in range(CHUNK):
                ys[o, pl.ds(r * CHUNK + j, 8, stride=SEG_PITCH), :] = (
                    yo[r * 8:(r + 1) * 8, j * LANES:(j + 1) * LANES])
        d_o = d_ref[:, o * LANES:(o + 1) * LANES]
        for s in range(8):
            out_ref[tok_rows[s], o * LANES:(o + 1) * LANES] = (
                ys[o, seg_rows[s], :] + d_o * us[o, seg_rows[s], :])


def _s5_prompt(x, g1, w_in, bblk, ctblk, lcol, a_tab, d):
    b, t, _ = x.shape
    tb = S5_ROWS * CHUNK
    nblk = t // tb
    x2 = x.reshape(b * t, D_MODEL)
    row_map = lambda i, j: (i * nblk + j, 0)
    op_shape = (N_OCT, CHUNK * LANES, CHUNK * LANES)
    out, hfin = pl.pallas_call(
        _s5_prompt_kernel,
        grid=(b, nblk),
        in_specs=[
            pl.BlockSpec((tb, D_MODEL), row_map),
            _const_spec((1, D_MODEL)),
            *[pl.BlockSpec((D_MODEL, U_HALF), lambda i, j, c=U_COL0 // U_HALF + h: (0, c),
                           pipeline_mode=pl.Buffered(1)) for h in range(2)],
            _const_spec(bblk.shape),
            _const_spec(ctblk.shape),
            _const_spec(lcol.shape),
            _const_spec(a_tab.shape),
            _const_spec((1, SSM_WIDTH)),
        ],
        out_specs=[
            pl.BlockSpec((tb, SSM_WIDTH), row_map),
            pl.BlockSpec((None, N_COL, 8, LANES), lambda i, j: (i, 0, 0, 0)),
        ],
        out_shape=[
            jax.ShapeDtypeStruct((b * t, SSM_WIDTH), F32),
            jax.ShapeDtypeStruct((b, N_COL, 8, LANES), F32),
        ],
        scratch_shapes=[
            *[pltpu.VMEM((N_OCT, (jt + 1) * M_PAIR * LANES, M_PAIR * LANES), BF)
              for jt in range(CHUNK // M_PAIR)],
            pltpu.VMEM(op_shape, BF), pltpu.VMEM(op_shape, BF),
            pltpu.VMEM((N_OCT, 8 * SEG_PITCH, LANES), F32), pltpu.VMEM((N_OCT, 8 * SEG_PITCH, LANES), F32),
            pltpu.VMEM((N_COL, S5_ROWS, LANES), F32), pltpu.VMEM((N_COL, S5_ROWS, LANES), F32),
            pltpu.VMEM((N_COL, 8, LANES), F32),
        ],
        compiler_params=pltpu.CompilerParams(
            dimension_semantics=("arbitrary", "arbitrary"), vmem_limit_bytes=VMEM_LIMIT),
        name="s5_prompt",
    )(x2, g1, w_in, w_in, bblk, ctblk, lcol, a_tab, d)
    return out, hfin[:, :, 0, :], hfin[:, :, 1, :]


def _gate_cols(hn, win_ref, bg_ref, cols):
    w_cols = slice(GATE_COL0 + cols.start, GATE_COL0 + cols.stop)
    return _sigmoid(_dot(hn, win_ref[:, w_cols]) + bg_ref[:, cols])


def _merge_out(x, gates, attn_proj, ssm_proj, wo_ref):
    merged = gates[:, :D_MODEL] * attn_proj + gates[:, D_MODEL:] * ssm_proj
    return x + _dot(merged.astype(BF), wo_ref[...])


def _half_split(a, ar, lo):
    z = jnp.zeros_like(a)
    return (jnp.where(lo, a, z).astype(BF), jnp.where(lo, z, ar).astype(BF),
            jnp.where(lo, ar, z).astype(BF), jnp.where(lo, z, a).astype(BF))


def _softmax_terms_t(st, bias_t, sink):
    st = st + bias_t
    m = jnp.maximum(jnp.max(st, axis=0, keepdims=True), sink)
    return jnp.exp2(st - m).astype(BF), jnp.exp2(sink - m)


def _mix_prompt_kernel(sinks_ref, x_ref, y_ref, rc_ref, rs1_ref, rs2_ref, g1_ref, win_ref,
                       bg_ref, wba_ref, wbs_ref, wo_ref, wglu_ref, bglu_ref,
                       x1_ref, kwin_ref, vwin_ref, kprev, vprev):
    t = pl.program_id(1)

    @pl.when(t == 0)
    def _():
        kprev[...] = jnp.zeros_like(kprev)
        vprev[...] = jnp.zeros_like(vprev)

    x = x_ref[...]
    hn = _rms(x, g1_ref[...]).astype(BF)
    qkv = _dot(hn, win_ref[:, :QKV_WIDTH])
    z = _gelu_tanh(y_ref[...])
    glu = _dot(z.astype(BF), wglu_ref[...]) + bglu_ref[...]
    rc, rs1, rs2 = rc_ref[...], rs1_ref[...], rs2_ref[...]
    scale = HEAD_DIM ** -0.5 * LOG2_E
    q = [(_rope(qkv[:, c * LANES:(c + 1) * LANES], rc, rs1, rs2) * scale).astype(BF)
         for c in range(ATTN_WIDTH // LANES)]
    k = _rope(qkv[:, ATTN_WIDTH:ATTN_WIDTH + KV_WIDTH], rc, rs1, rs2)
    v = qkv[:, ATTN_WIDTH + KV_WIDTH:ATTN_WIDTH + 2 * KV_WIDTH]
    ssm = (z * _sigmoid(glu)).astype(BF)
    ssm_proj = _dot(ssm, wbs_ref[...])

    w = WINDOW
    n_sub = MIX_TB // w
    lane = lax.broadcasted_iota(jnp.int32, (2 * w, LANES), 1)
    lo = lane < HEAD_DIM
    kj = lax.broadcasted_iota(jnp.int32, (2 * w, w), 0)
    qi = lax.broadcasted_iota(jnp.int32, (2 * w, w), 1)
    band = (kj > qi) & (kj <= qi + w)
    first = band & ((kj >= w) | (t > 0))
    bias_band = jnp.where(band, 0.0, NEG_BIG).astype(F32)
    bias_first = jnp.where(first, 0.0, NEG_BIG).astype(F32)
    col = lax.broadcasted_iota(jnp.int32, (1, 2 * w), 1)
    ones_row = (lax.broadcasted_iota(jnp.int32, (HEAD_DIM, 2 * w), 0) == 0).astype(F32)

    def sink_row(ha, hb):
        return jnp.where(col < w, sinks_ref[ha], sinks_ref[hb]) * LOG2_E

    gate_w = GATE_WIDTH // (n_sub * N_KV_HEADS)
    gate_cols = []
    attn_rows = []
    for sb in range(n_sub):
        cur = slice(sb * w, (sb + 1) * w)
        if sb == 0:
            kcat = jnp.concatenate([kprev[...], k[cur]], axis=0)
            vcat = jnp.concatenate([vprev[...], v[cur]], axis=0)
            bias = bias_first
        else:
            kcat = k[(sb - 1) * w:(sb + 1) * w]
            vcat = v[(sb - 1) * w:(sb + 1) * w]
            bias = bias_band
        bias2 = jnp.concatenate([bias, bias], axis=1)
        ka, kb, kc, kd = _half_split(kcat, pltpu.roll(kcat, HEAD_DIM, 1), lo)
        vt = vcat.T
        v_rows = [(jnp.concatenate([vt[g * HEAD_DIM:(g + 1) * HEAD_DIM], ones_row], axis=0).astype(BF),
                   jnp.concatenate([ones_row, vt[g * HEAD_DIM:(g + 1) * HEAD_DIM]], axis=0).astype(BF))
                  for g in range(N_KV_HEADS)]
        outs = []
        for grp, (k_e, k_o) in enumerate(((ka, kb), (kc, kd))):
            qq = jnp.concatenate([q[2 * grp][cur], q[2 * grp + 1][cur]], axis=0)
            s_e, s_o = _dot_nt(k_e, qq), _dot_nt(k_o, qq)
            gc = sb * N_KV_HEADS + grp
            gate_cols.append(_gate_cols(hn, win_ref, bg_ref, slice(gc * gate_w, (gc + 1) * gate_w)))
            p_e, sink_e = _softmax_terms_t(s_e, bias2, sink_row(4 * grp, 4 * grp + 2))
            p_o, sink_o = _softmax_terms_t(s_o, bias2, sink_row(4 * grp + 1, 4 * grp + 3))
            v_e, v_o = v_rows[grp]
            d_e, d_o = _dot(v_e, p_e), _dot(v_o, p_o)
            o2 = jnp.concatenate(
                [d_e[:HEAD_DIM] * (1.0 / (d_e[HEAD_DIM:HEAD_DIM + 1] + sink_e)),
                 d_o[HEAD_DIM:] * (1.0 / (d_o[0:1] + sink_o))], axis=0)
            outs += [o2[:, :w].T, o2[:, w:].T]
        attn_rows.append(jnp.concatenate(outs, axis=1))
    attn = jnp.concatenate(attn_rows, axis=0).astype(BF)
    gates = jnp.concatenate(gate_cols, axis=1)

    kprev[...] = k[MIX_TB - w:]
    vprev[...] = v[MIX_TB - w:]
    kwin_ref[...] = k[MIX_TB - w:]
    vwin_ref[...] = v[MIX_TB - w:]
    x1_ref[...] = _merge_out(x, gates, _dot(attn, wba_ref[...]), ssm_proj, wo_ref)


def _mix_prompt(x, y_ssm, rc, rs1, rs2, sinks, g1, w_in, bg, wba, wbs, wo, wglu, bglu):
    b, t, _ = x.shape
    nblk = t // MIX_TB
    x2 = x.reshape(b * t, D_MODEL)
    row_map = lambda i, j: (i * nblk + j, 0)
    x1, kwin, vwin = pl.pallas_call(
        _mix_prompt_kernel,
        grid=(b, nblk),
        in_specs=[
            pl.BlockSpec(memory_space=pltpu.SMEM),
            pl.BlockSpec((MIX_TB, D_MODEL), row_map),
            pl.BlockSpec((MIX_TB, SSM_WIDTH), row_map),
            pl.BlockSpec((MIX_TB, LANES), lambda i, j: (j, 0)),
            pl.BlockSpec((MIX_TB, LANES), lambda i, j: (j, 0)),
            pl.BlockSpec((MIX_TB, LANES), lambda i, j: (j, 0)),
            _const_spec((1, D_MODEL)),
            _const_spec(w_in.shape),
            _const_spec(bg.shape),
            _const_spec(wba.shape),
            _const_spec(wbs.shape),
            _const_spec(wo.shape),
            _const_spec(wglu.shape),
            _const_spec(bglu.shape),
        ],
        out_specs=[
            pl.BlockSpec((MIX_TB, D_MODEL), row_map),
            pl.BlockSpec((None, WINDOW, KV_WIDTH), lambda i, j: (i, 0, 0)),
            pl.BlockSpec((None, WINDOW, KV_WIDTH), lambda i, j: (i, 0, 0)),
        ],
        out_shape=[
            jax.ShapeDtypeStruct((b * t, D_MODEL), F32),
            jax.ShapeDtypeStruct((b, WINDOW, KV_WIDTH), F32),
            jax.ShapeDtypeStruct((b, WINDOW, KV_WIDTH), F32),
        ],
        scratch_shapes=[pltpu.VMEM((WINDOW, KV_WIDTH), F32), pltpu.VMEM((WINDOW, KV_WIDTH), F32)],
        compiler_params=pltpu.CompilerParams(
            dimension_semantics=("arbitrary", "arbitrary"), vmem_limit_bytes=VMEM_LIMIT),
        name="mix_prompt",
    )(sinks, x2, y_ssm, rc, rs1, rs2, g1, w_in, bg, wba, wbs, wo, wglu, bglu)
    return x1, kwin, vwin


QROWS = 16
SAMPLE_TB = 32


def _sample_attn_kernel(x_ref, kbuf_ref, vbuf_ref, rc_ref, rs1_ref, rs2_ref, sinkc_ref, g1_ref, wqkv_ref,
                        o_ref, kout_ref, vout_ref, qz):
    nb = x_ref.shape[0]
    hn = _rms(x_ref[...], g1_ref[...]).astype(BF)
    qkv = _dot(hn, wqkv_ref[...])
    rc, rs1, rs2 = rc_ref[...], rs1_ref[...], rs2_ref[...]
    scale = HEAD_DIM ** -0.5
    k_new = _rope(qkv[:, ATTN_WIDTH:ATTN_WIDTH + KV_WIDTH], rc, rs1, rs2)
    v_new = qkv[:, ATTN_WIDTH + KV_WIDTH:ATTN_WIDTH + 2 * KV_WIDTH]
    pad = jnp.zeros((LANES - nb, KV_WIDTH), F32)
    k_new_t = jnp.concatenate([k_new, pad], axis=0).T
    v_new_t = jnp.concatenate([v_new, pad], axis=0).T

    lane = lax.broadcasted_iota(jnp.int32, (nb, LANES), 1)
    lo = lane < HEAD_DIM
    qz[...] = jnp.zeros_like(qz)
    for c in range(ATTN_WIDTH // LANES):
        qc = _rope(qkv[:, c * LANES:(c + 1) * LANES], rc, rs1, rs2) * scale
        qr = pltpu.roll(qc, HEAD_DIM, 1)
        zero = jnp.zeros_like(qc)
        if c < 2:
            even, odd = jnp.where(lo, qc, zero), jnp.where(lo, qr, zero)
        else:
            even, odd = jnp.where(lo, zero, qr), jnp.where(lo, zero, qc)
        qz[pl.ds(2 * c, nb, stride=QROWS), :] = even
        qz[pl.ds(2 * c + 1, nb, stride=QROWS), :] = odd

    last = lax.broadcasted_iota(jnp.int32, (KV_WIDTH, WINDOW), 1) == WINDOW - 1
    for b in range(nb):
        kout_ref[b] = jnp.where(last, k_new_t[:, b:b + 1], pltpu.roll(kbuf_ref[b], WINDOW - 1, 1))
        vout_ref[b] = jnp.where(last, v_new_t[:, b:b + 1], pltpu.roll(vbuf_ref[b], WINDOW - 1, 1))

    sink = sinkc_ref[...]
    q3 = qz[...].reshape(nb, QROWS, LANES).astype(BF)
    s = jnp.einsum('bhd,bdk->bhk', q3, kout_ref[...].astype(BF), preferred_element_type=F32)
    m = jnp.maximum(jnp.max(s, axis=-1, keepdims=True), sink)
    p = jnp.exp(s - m)
    den = jnp.sum(p, axis=-1, keepdims=True) + jnp.exp(sink - m)
    p = (p * (1.0 / den)).astype(BF)
    o3 = jnp.einsum('bhk,bdk->bhd', p, vout_ref[...].astype(BF), preferred_element_type=F32)
    o_ref[...] = o3.reshape(nb * QROWS, LANES)


def _sample_attn(x, kbuf, vbuf, rc, rs1, rs2, sinkc, g1, w_in):
    nb = x.shape[0]
    tb = SAMPLE_TB
    return pl.pallas_call(
        _sample_attn_kernel,
        grid=(nb // tb,),
        in_specs=[
            pl.BlockSpec((tb, D_MODEL), lambda i: (i, 0)),
            pl.BlockSpec((tb, KV_WIDTH, WINDOW), lambda i: (i, 0, 0)),
            pl.BlockSpec((tb, KV_WIDTH, WINDOW), lambda i: (i, 0, 0)),
            _const_spec(rc.shape), _const_spec(rs1.shape), _const_spec(rs2.shape),
            _const_spec(sinkc.shape), _const_spec(g1.shape),
            pl.BlockSpec((D_MODEL, QKV_WIDTH), lambda i: (0, 0), pipeline_mode=pl.Buffered(1)),
        ],
        out_specs=[
            pl.BlockSpec((tb * QROWS, LANES), lambda i: (i, 0)),
            pl.BlockSpec((tb, KV_WIDTH, WINDOW), lambda i: (i, 0, 0)),
            pl.BlockSpec((tb, KV_WIDTH, WINDOW), lambda i: (i, 0, 0)),
        ],
        out_shape=[
            jax.ShapeDtypeStruct((nb * QROWS, LANES), F32),
            jax.ShapeDtypeStruct((nb, KV_WIDTH, WINDOW), F32),
            jax.ShapeDtypeStruct((nb, KV_WIDTH, WINDOW), F32),
        ],
        scratch_shapes=[pltpu.VMEM((tb * QROWS, LANES), F32)],
        compiler_params=pltpu.CompilerParams(
            dimension_semantics=("arbitrary",), vmem_limit_bytes=VMEM_LIMIT),
        name="sample_attn",
    )(x, kbuf, vbuf, rc, rs1, rs2, sinkc, g1, w_in)


def _sample_tail_kernel(x_ref, o3_ref, h0re_ref, h0im_ref, g1_ref, win_ref, bg_ref, wbad_ref,
                        wbs_ref, wo_ref, lb_ref, bblk_ref, ctblk_ref, d_ref, wglu_ref, bglu_ref,
                        x1_ref, hre_ref, him_ref):
    nb = x_ref.shape[0]
    x = x_ref[...]
    hn = _rms(x, g1_ref[...]).astype(BF)

    lane = lax.broadcasted_iota(jnp.int32, (nb, LANES), 1)
    lo = lane < HEAD_DIM
    a = jnp.zeros((nb, D_MODEL), F32)
    zero = jnp.zeros((nb, LANES), F32)
    for h in range(N_Q_HEADS):
        oh = o3_ref[pl.ds(h, nb, stride=QROWS), :]
        oh = jnp.where(lo, oh, zero) if h < N_Q_HEADS // 2 else jnp.where(lo, zero, oh)
        a = a + _dot(oh.astype(BF), wbad_ref[h])

    u = _dot(hn, win_ref[:, U_COL0:GATE_COL0])
    ub = u.astype(BF)
    lre, lim = lb_ref[0:1, :], lb_ref[1:2, :]
    y_cols = []
    for o in range(N_OCT):
        sl = slice(o * OCT_STATES, (o + 1) * OCT_STATES)
        bu = _dot(ub[:, o * LANES:(o + 1) * LANES], bblk_ref[o])
        blocks = [slice(c * LANES, (c + 1) * LANES) for c in range(o * OCT_COL, (o + 1) * OCT_COL)]
        h0r = jnp.concatenate([h0re_ref[rows, :].T for rows in blocks], axis=1)
        h0i = jnp.concatenate([h0im_ref[rows, :].T for rows in blocks], axis=1)
        hr = bu[:, :OCT_STATES] + (lre[:, sl] * h0r - lim[:, sl] * h0i)
        hi = bu[:, OCT_STATES:] + (lre[:, sl] * h0i + lim[:, sl] * h0r)
        for cc, rows in enumerate(blocks):
            hre_ref[rows, :] = hr[:, cc * LANES:(cc + 1) * LANES].T
            him_ref[rows, :] = hi[:, cc * LANES:(cc + 1) * LANES].T
        y_cols.append(_dot_nt(jnp.concatenate([hr, hi], axis=1).astype(BF), ctblk_ref[o]))
    y = jnp.concatenate(y_cols, axis=1) + d_ref[...] * u
    z = _gelu_tanh(y)
    gate = _dot(z.astype(BF), wglu_ref[...]) + bglu_ref[...]
    ssm = (z * _sigmoid(gate)).astype(BF)

    gates = _gate_cols(hn, win_ref, bg_ref, slice(0, GATE_WIDTH))
    x1_ref[...] = _merge_out(x, gates, a, _dot(ssm, wbs_ref[...]), wo_ref)


def _sample_tail(x, o3, h0re, h0im, g1, w_in, bg, wbad, wbs, wo, lb, bblk, ctblk, d, wglu, bglu):
    nb = x.shape[0]
    args = (x, o3, h0re, h0im, g1, w_in, bg, wbad, wbs, wo, lb, bblk, ctblk, d, wglu, bglu)
    out_shapes = ((nb, D_MODEL), (N_STATES, nb), (N_STATES, nb))
    return pl.pallas_call(
        _sample_tail_kernel,
        grid=(1,),
        in_specs=[_const_spec(a.shape) for a in args],
        out_specs=[pl.BlockSpec(s, lambda i: (0, 0)) for s in out_shapes],
        out_shape=[jax.ShapeDtypeStruct(s, F32) for s in out_shapes],
        compiler_params=pltpu.CompilerParams(
            dimension_semantics=("arbitrary",), vmem_limit_bytes=VMEM_LIMIT),
        name="sample_tail",
    )(*args)


FFN_W_CHUNKS = 16
FFN_W_SLOTS = 4


def _stream_cast_weights(jobs):
    tasks, used = [], {}
    for w_hbm, w_bf, stage, sem in jobs:
        rows = stage.shape[1]
        for k in range(w_hbm.shape[0] // rows):
            slot = used.get(id(stage), 0) % stage.shape[0]
            used[id(stage)] = used.get(id(stage), 0) + 1
            copy = pltpu.make_async_copy(w_hbm.at[pl.ds(k * rows, rows), :], stage.at[slot], sem.at[slot])
            tasks.append((copy, w_bf, stage, slot, k * rows, rows))
    ahead = min(stage.shape[0] for _, _, stage, _ in jobs) - 1
    for copy, *_ in tasks[:ahead]:
        copy.start()
    for i, (copy, w_bf, stage, slot, row0, rows) in enumerate(tasks):
        if i + ahead < len(tasks):
            tasks[i + ahead][0].start()
        copy.wait()
        w_bf[row0:row0 + rows, :] = stage[slot].astype(BF)


def _ffn_rows(x, g2_ref, wgate, wup, wdown, gf_ref):
    h = _rms(x, g2_ref[...]).astype(BF)
    gate = _dot(h, wgate[...])
    up = _dot(h, wup[...])
    act = (gate * _sigmoid(gate) * up).astype(BF)
    x2 = x + _dot(act, wdown[...])
    return _rms(x2, gf_ref[...])


def _ffn_kernel(xp_ref, xs_ref, g2_ref, wgate_hbm, wup_hbm, wdown_hbm, gf_ref, yp_ref, ys_ref,
                wgate, wup, wdown, stage_in, stage_out, sem_in, sem_out):
    i = pl.program_id(0)
    n_prompt = pl.num_programs(0) - 1

    @pl.when(i == 0)
    def _():
        _stream_cast_weights([(wgate_hbm, wgate, stage_in, sem_in), (wup_hbm, wup, stage_in, sem_in),
                              (wdown_hbm, wdown, stage_out, sem_out)])

    @pl.when(i < n_prompt)
    def _():
        half = FFN_TB // FFN_SPLIT
        for rows in (slice(h * half, (h + 1) * half) for h in range(FFN_SPLIT)):
            yp_ref[rows, :] = _ffn_rows(xp_ref[rows, :], g2_ref, wgate, wup, wdown, gf_ref)

    @pl.when(i == n_prompt)
    def _():
        ys_ref[...] = _ffn_rows(xs_ref[...], g2_ref, wgate, wup, wdown, gf_ref)


def _ffn(xp, xs, g2, wgate, wup, wdown, gf):
    n, ns = xp.shape[0], xs.shape[0]
    n_prompt = n // FFN_TB
    prompt_map = lambda i: (jnp.minimum(i, n_prompt - 1), 0)
    return pl.pallas_call(
        _ffn_kernel,
        grid=(n_prompt + 1,),
        in_specs=[
            pl.BlockSpec((FFN_TB, D_MODEL), prompt_map),
            _const_spec((ns, D_MODEL)),
            _const_spec((1, D_MODEL)),
            pl.BlockSpec(memory_space=pl.ANY),
            pl.BlockSpec(memory_space=pl.ANY),
            pl.BlockSpec(memory_space=pl.ANY),
            _const_spec((1, D_MODEL)),
        ],
        out_specs=[
            pl.BlockSpec((FFN_TB, D_MODEL), prompt_map),
            pl.BlockSpec((ns, D_MODEL), lambda i: (0, 0)),
        ],
        out_shape=[
            jax.ShapeDtypeStruct((n, D_MODEL), F32),
            jax.ShapeDtypeStruct((ns, D_MODEL), F32),
        ],
        scratch_shapes=[
            pltpu.VMEM((D_MODEL, D_FF), BF), pltpu.VMEM((D_MODEL, D_FF), BF), pltpu.VMEM((D_FF, D_MODEL), BF),
            pltpu.VMEM((FFN_W_SLOTS, D_MODEL // FFN_W_CHUNKS, D_FF), F32),
            pltpu.VMEM((FFN_W_SLOTS, D_FF // FFN_W_CHUNKS, D_MODEL), F32),
            pltpu.SemaphoreType.DMA((FFN_W_SLOTS,)), pltpu.SemaphoreType.DMA((FFN_W_SLOTS,)),
        ],
        compiler_params=pltpu.CompilerParams(
            dimension_semantics=("arbitrary",), vmem_limit_bytes=VMEM_LIMIT),
        name="ffn",
    )(xp, xs, g2, wgate, wup, wdown, gf)


def _rope_tables(pos):
    pos = np.asarray(pos, np.float64)
    inv_freq = ROPE_THETA ** (-(np.arange(ROPE_HALF, dtype=np.float64) * 2.0 / ROPE_DIM))
    ang = pos[:, None] * inv_freq[None, :]
    cos, sin = np.cos(ang), np.sin(ang)
    pad = np.zeros((pos.shape[0], HEAD_DIM - ROPE_DIM))
    zero = np.zeros_like(sin)
    rc = np.concatenate([cos, cos, pad + 1.0], axis=1)
    rs1 = np.concatenate([zero, sin, pad], axis=1)
    rs2 = np.concatenate([-sin, zero, pad], axis=1)
    rep = LANES // HEAD_DIM
    return tuple(jnp.asarray(np.tile(a, (1, rep)), F32) for a in (rc, rs1, rs2))


def _cmul(ar, ai, br, bi):
    return ar * br - ai * bi, ar * bi + ai * br


def _ssm_tables(lam_re, lam_im, log_dt, b_re, b_im, c_re, c_im):
    dt = jnp.exp(log_dt)[:, None]
    mag = jnp.exp(lam_re * dt)
    lb_re = mag * jnp.cos(lam_im * dt)
    lb_im = mag * jnp.sin(lam_im * dt)
    den = lam_re * lam_re + lam_im * lam_im
    nr = lb_re - 1.0
    k_re = ((nr * lam_re + lb_im * lam_im) / den)[..., None]
    k_im = ((lb_im * lam_re - nr * lam_im) / den)[..., None]
    bb_re = k_re * b_re - k_im * b_im
    bb_im = k_re * b_im + k_im * b_re

    a_re, a_im = lb_re, lb_im
    for _ in range(int(math.log2(CHUNK))):
        a_re, a_im = _cmul(a_re, a_im, a_re, a_im)
    s_re, s_im = a_re, a_im
    for _ in range(int(math.log2(SEG))):
        s_re, s_im = _cmul(s_re, s_im, s_re, s_im)

    eye = jnp.eye(OCT, dtype=F32).reshape(1, OCT, 1, OCT, 1)

    def block_diag(a):
        r, c = a.shape[1:]
        return (a.reshape(N_OCT, OCT, r, 1, c) * eye).reshape(N_OCT, OCT * r, OCT * c)

    bblk = jnp.concatenate([block_diag(jnp.swapaxes(bb_re, 1, 2)),
                            block_diag(jnp.swapaxes(bb_im, 1, 2))], axis=2)
    ctblk = jnp.concatenate([block_diag(c_re), block_diag(-c_im)], axis=2)

    oct_cols = lambda a: a.reshape(N_OCT, 1, OCT_STATES)
    lcol = jnp.concatenate([oct_cols(lb_re), oct_cols(lb_im),
                            jnp.zeros((N_OCT, 6, OCT_STATES), F32)], axis=1)

    flat = lambda a: a.reshape(1, N_STATES)
    col = lambda a: a.reshape(N_COL, 1, LANES)
    a_tab = jnp.concatenate([col(a_re), col(a_im), col(s_re), col(s_im),
                             jnp.zeros((N_COL, 4, LANES), F32)], axis=1)
    lb = jnp.concatenate([flat(lb_re), flat(lb_im)], axis=0)
    return bblk, ctblk, lcol, a_tab, lb


def kernel(x_prompt, x_sample, state_k_win, state_v_win, state_ssm_re, state_ssm_im, norm1_g, w_in, b_gate, attn_sinks, ssm_lam_re, ssm_lam_im, ssm_log_dt, ssm_b_re, ssm_b_im, ssm_c_re, ssm_c_im, ssm_d, w_glu, b_glu, w_branch_attn, w_branch_ssm, w_out, norm2_g, w_ffn_gate, w_ffn_up, w_ffn_down, norm_f_g):
    depth = w_in.shape[0]
    assert depth == 1
    b, t, _ = x_prompt.shape
    nb, s_len, _ = x_sample.shape
    assert s_len == 1 and state_k_win.shape[2] == WINDOW
    l = 0
    assert w_in.shape[2] == IN_WIDTH
    w_in_b = w_in[l].astype(BF)
    g1 = norm1_g[l].reshape(1, D_MODEL)
    g2 = norm2_g[l].reshape(1, D_MODEL)
    gf = norm_f_g.reshape(1, D_MODEL)
    bg = b_gate[l].reshape(1, GATE_WIDTH)
    d = ssm_d[l].reshape(1, SSM_WIDTH)
    wglu = w_glu[l].astype(BF)
    bglu = b_glu[l].reshape(1, SSM_WIDTH)
    wba = w_branch_attn[l].astype(BF)
    wbs = w_branch_ssm[l].astype(BF)
    wo = w_out[l].astype(BF)
    sinks = attn_sinks[l]

    bblk, ctblk, lcol, a_tab, lb = _ssm_tables(
        ssm_lam_re[l], ssm_lam_im[l], ssm_log_dt[l], ssm_b_re[l], ssm_b_im[l], ssm_c_re[l], ssm_c_im[l])

    rc, rs1, rs2 = _rope_tables(np.arange(t))
    yssm_p, hre_p, him_p = _s5_prompt(x_prompt, g1, w_in_b, bblk, ctblk, lcol, a_tab, d)
    x1_p, kwin_p, vwin_p = _mix_prompt(x_prompt, yssm_p, rc, rs1, rs2, sinks, g1, w_in_b, bg, wba, wbs, wo,
                                       wglu, bglu)

    rcs, rs1s, rs2s = _rope_tables(PAST_LEN + np.arange(1))
    sinkc = jnp.concatenate([sinks, jnp.zeros((QROWS - N_Q_HEADS,), F32)]).reshape(QROWS, 1)
    wbad = jnp.concatenate([wba.reshape(N_Q_HEADS, HEAD_DIM, D_MODEL)] * 2, axis=1)
    xs = x_sample.reshape(nb, D_MODEL)
    key_minor = lambda a: jnp.swapaxes(a.reshape(nb, WINDOW, KV_WIDTH), 1, 2)
    o3, kwin_s, vwin_s = _sample_attn(
        xs, key_minor(state_k_win[l]), key_minor(state_v_win[l]), rcs, rs1s, rs2s, sinkc, g1, w_in_b)
    x1_s, hre_s, him_s = _sample_tail(
        xs, o3, state_ssm_re[l].reshape(nb, N_STATES).T, state_ssm_im[l].reshape(nb, N_STATES).T,
        g1, w_in_b, bg, wbad, wbs, wo, lb, bblk.astype(BF), ctblk.astype(BF), d, wglu, bglu)
    y_p, y_s = _ffn(x1_p, x1_s, g2, w_ffn_gate[l], w_ffn_up[l], w_ffn_down[l], gf)
    y_p = y_p.reshape(b, t, D_MODEL)
    y_s = y_s.reshape(nb, 1, D_MODEL)

    kv_shape_p = (1, b, WINDOW, N_KV_HEADS, HEAD_DIM)
    st_shape_p = (1, b, N_SSM_GROUPS, SSM_STATE)
    kv_shape_s = (1, nb, WINDOW, N_KV_HEADS, HEAD_DIM)
    st_shape_s = (1, nb, N_SSM_GROUPS, SSM_STATE)
    return (y_p, y_s,
            kwin_p.reshape(kv_shape_p), vwin_p.reshape(kv_shape_p),
            hre_p.reshape(st_shape_p), him_p.reshape(st_shape_p),
            jnp.swapaxes(kwin_s, 1, 2).reshape(kv_shape_s), jnp.swapaxes(vwin_s, 1, 2).reshape(kv_shape_s),
            hre_s.T.reshape(st_shape_s), him_s.T.reshape(st_shape_s))
```

```python
import math

import jax
import jax.numpy as jnp
import numpy as np
from jax import lax
from jax.experimental import pallas as pl
from jax.experimental.pallas import tpu as pltpu

D_MODEL = 1024
N_Q_HEADS = 8
N_KV_HEADS = 2
HEAD_DIM = 64
ATTN_WIDTH = N_Q_HEADS * HEAD_DIM
KV_WIDTH = N_KV_HEADS * HEAD_DIM
WINDOW = 128
ROPE_DIM = HEAD_DIM // 4
ROPE_HALF = ROPE_DIM // 2
ROPE_THETA = 500000.0
SSM_WIDTH = D_MODEL // 2
SSM_GROUP = 16
N_SSM_GROUPS = SSM_WIDTH // SSM_GROUP
SSM_STATE = 64
N_STATES = N_SSM_GROUPS * SSM_STATE
GATE_WIDTH = 2 * D_MODEL
QKV_WIDTH = ATTN_WIDTH + 2 * KV_WIDTH
U_COL0 = QKV_WIDTH
GATE_COL0 = U_COL0 + SSM_WIDTH
IN_WIDTH = GATE_COL0 + GATE_WIDTH
D_FF = -(-8 * D_MODEL // (3 * 256)) * 256
NORM_EPS = 1e-5
PAST_LEN = 8192

LANES = 128
CHUNK = 8
OCT = LANES // SSM_GROUP
N_OCT = N_SSM_GROUPS // OCT
OCT_STATES = OCT * SSM_STATE
OCT_COL = OCT_STATES // LANES
N_COL = N_STATES // LANES
S5_ROWS = 128
SEG = 16
M_PAIR = 2
U_HALF = SSM_WIDTH // 2
SEG_PITCH = SEG * CHUNK + 8
MIX_TB = 512
FFN_TB = 1024
FFN_SPLIT = 4
NEG_BIG = -1e30
LOG2_E = math.log2(math.e)
VMEM_LIMIT = 56 * 1024 * 1024

BF = jnp.bfloat16
F32 = jnp.float32


def _dot(a, b):
    return jnp.dot(a, b, preferred_element_type=F32)


def _dot_nt(a, b):
    return lax.dot_general(a, b, (((1,), (1,)), ((), ())), preferred_element_type=F32)


def _dot_nt_split(a, b):
    a_hi, b_hi = a.astype(BF), b.astype(BF)
    a_lo = (a - a_hi.astype(F32)).astype(BF)
    b_lo = (b - b_hi.astype(F32)).astype(BF)
    return _dot_nt(a_hi, b_hi) + (_dot_nt(a_hi, b_lo) + _dot_nt(a_lo, b_hi))


def _rms(x, g):
    return x * lax.rsqrt(jnp.mean(x * x, axis=-1, keepdims=True) + NORM_EPS) * g


def _sigmoid(x):
    return 1.0 / (1.0 + jnp.exp(-x))


def _gelu_tanh(x):
    c = math.sqrt(2.0 / math.pi)
    return 0.5 * x * (1.0 + jnp.tanh(c * (x + 0.044715 * (x * x * x))))


def _rope(a, rc, rs1, rs2):
    return a * rc + pltpu.roll(a, ROPE_HALF, 1) * rs1 + pltpu.roll(a, LANES - ROPE_HALF, 1) * rs2


def _const_spec(shape):
    nd = len(shape)
    return pl.BlockSpec(shape, lambda *_: (0,) * nd, pipeline_mode=pl.Buffered(1))


def _build_chunk_operators(bblk_ref, ctblk_ref, lcol_ref, m_s, e_s, f_s):
    for o in range(N_OCT):
        ct = ctblk_ref[o]
        lr, li = lcol_ref[o, 0:1, :], lcol_ref[o, 1:2, :]
        er, ei = bblk_ref[o, :, :OCT_STATES], bblk_ref[o, :, OCT_STATES:]
        k_blk = []
        for tau in range(CHUNK):
            e_cat = jnp.concatenate([er, ei], axis=1)
            i = CHUNK - 1 - tau
            e_s[o, i * LANES:(i + 1) * LANES, :] = e_cat.astype(BF)
            k_blk.append(_dot_nt_split(e_cat, ct).astype(BF))
            er, ei = er * lr - ei * li, er * li + ei * lr
        zero = jnp.zeros((LANES, LANES), BF)
        for j in range(CHUNK):
            jt, jj = divmod(j, M_PAIR)
            for i in range(M_PAIR * (jt + 1)):
                m_s[jt][o, i * LANES:(i + 1) * LANES, jj * LANES:(jj + 1) * LANES] = (
                    k_blk[j - i] if j >= i else zero)
        tr, ti = ct[:, :OCT_STATES], -ct[:, OCT_STATES:]
        for j in range(CHUNK):
            tr, ti = tr * lr - ti * li, tr * li + ti * lr
            f_s[o, :OCT_STATES, j * LANES:(j + 1) * LANES] = tr.T.astype(BF)
            f_s[o, OCT_STATES:, j * LANES:(j + 1) * LANES] = (-ti).T.astype(BF)


def _s5_prompt_kernel(x_ref, g1_ref, wu0_ref, wu1_ref, bblk_ref, ctblk_ref, lcol_ref, a_ref, d_ref,
                      out_ref, hfin_ref,
                      m0_ref, m1_ref, m2_ref, m3_ref, e_ref, f_ref, us, ys, sre, sim, car):
    m_ref = (m0_ref, m1_ref, m2_ref, m3_ref)
    blk = pl.program_id(1)

    @pl.when((pl.program_id(0) == 0) & (blk == 0))
    def _():
        _build_chunk_operators(bblk_ref, ctblk_ref, lcol_ref, m_ref, e_ref, f_ref)

    @pl.when(blk == 0)
    def _():
        car[...] = jnp.zeros_like(car)

    seg_tokens = SEG * CHUNK
    seg_rows = [slice(s * SEG_PITCH, s * SEG_PITCH + seg_tokens) for s in range(8)]
    tok_rows = [slice(s * seg_tokens, (s + 1) * seg_tokens) for s in range(8)]

    for s in range(0, 8, 2):
        hn = _rms(x_ref[s * seg_tokens:(s + 2) * seg_tokens, :], g1_ref[...]).astype(BF)
        u = jnp.concatenate([_dot(hn, wu0_ref[...]), _dot(hn, wu1_ref[...])], axis=1)
        for half in range(2):
            for cc in range(N_OCT):
                us[cc, seg_rows[s + half], :] = u[tok_rows[half], cc * LANES:(cc + 1) * LANES]

    sub = lax.broadcasted_iota(jnp.int32, (8, LANES), 0)

    def scan_column(c):
        tab = a_ref[c]
        are, aim = tab[0:1], tab[1:2]
        bre, bim = tab[2:3], tab[3:4]

        def step(cr, ci, r, keep_entering):
            slab = slice(r * 8, (r + 1) * 8)
            s_r, s_i = sre[c, slab, :], sim[c, slab, :]
            if keep_entering:
                sre[c, slab, :] = cr
                sim[c, slab, :] = ci
            return are * cr - aim * ci + s_r, are * ci + aim * cr + s_i

        cr = jnp.zeros((8, LANES), F32)
        ci = jnp.zeros((8, LANES), F32)
        for r in range(SEG):
            cr, ci = step(cr, ci, r, False)
        cv = car[c]
        pr, pi = cv[0:1], cv[1:2]
        sr = jnp.zeros((8, LANES), F32)
        si = jnp.zeros((8, LANES), F32)
        for s in range(8):
            sr = jnp.where(sub == s, pr, sr)
            si = jnp.where(sub == s, pi, si)
            pr, pi = (bre * pr - bim * pi + cr[s:s + 1], bre * pi + bim * pr + ci[s:s + 1])
        end = jnp.where(sub == 0, pr, jnp.where(sub == 1, pi, 0.0))
        car[c] = end
        hfin_ref[c] = end
        cr, ci = sr, si
        for r in range(SEG):
            cr, ci = step(cr, ci, r, True)

    for o in range(N_OCT):
        uo = jnp.concatenate(
            [jnp.concatenate([us[o, pl.ds(r * CHUNK + i, 8, stride=SEG_PITCH), :] for r in range(SEG)],
                             axis=0).astype(BF) for i in range(CHUNK)], axis=1)
        s_end = _dot(uo, e_ref[o])
        cols = range(o * OCT_COL, (o + 1) * OCT_COL)
        for cc, c in enumerate(cols):
            sre[c] = s_end[:, cc * LANES:(cc + 1) * LANES]
            sim[c] = s_end[:, OCT_STATES + cc * LANES:OCT_STATES + (cc + 1) * LANES]
        y_in = jnp.concatenate(
            [_dot(uo[:, :(jt + 1) * M_PAIR * LANES], m_ref[jt][o]) for jt in range(CHUNK // M_PAIR)],
            axis=1)
        for c in cols:
            scan_column(c)
        hp = jnp.concatenate([sre[c] for c in cols] + [sim[c] for c in cols], axis=1).astype(BF)
        yo = y_in + _dot(hp, f_ref[o])
        for r in range(SEG):
            for j in range(CHUNK):
                ys[o, pl.ds(r * CHUNK + j, 8, stride=SEG_PITCH), :] = (
                    yo[r * 8:(r + 1) * 8, j * LANES:(j + 1) * LANES])
        d_o = d_ref[:, o * LANES:(o + 1) * LANES]
        for s in range(8):
            out_ref[tok_rows[s], o * LANES:(o + 1) * LANES] = (
                ys[o, seg_rows[s], :] + d_o * us[o, seg_rows[s], :])


def _s5_prompt(x, g1, w_in, bblk, ctblk, lcol, a_tab, d):
    b, t, _ = x.shape
    tb = S5_ROWS * CHUNK
    nblk = t // tb
    x2 = x.reshape(b * t, D_MODEL)
    row_map = lambda i, j: (i * nblk + j, 0)
    op_shape = (N_OCT, CHUNK * LANES, CHUNK * LANES)
    out, hfin = pl.pallas_call(
        _s5_prompt_kernel,
        grid=(b, nblk),
        in_specs=[
            pl.BlockSpec((tb, D_MODEL), row_map),
            _const_spec((1, D_MODEL)),
            *[pl.BlockSpec((D_MODEL, U_HALF), lambda i, j, c=U_COL0 // U_HALF + h: (0, c),
                           pipeline_mode=pl.Buffered(1)) for h in range(2)],
            _const_spec(bblk.shape),
            _const_spec(ctblk.shape),
            _const_spec(lcol.shape),
            _const_spec(a_tab.shape),
            _const_spec((1, SSM_WIDTH)),
        ],
        out_specs=[
            pl.BlockSpec((tb, SSM_WIDTH), row_map),
            pl.BlockSpec((None, N_COL, 8, LANES), lambda i, j: (i, 0, 0, 0)),
        ],
        out_shape=[
            jax.ShapeDtypeStruct((b * t, SSM_WIDTH), F32),
            jax.ShapeDtypeStruct((b, N_COL, 8, LANES), F32),
        ],
        scratch_shapes=[
            *[pltpu.VMEM((N_OCT, (jt + 1) * M_PAIR * LANES, M_PAIR * LANES), BF)
              for jt in range(CHUNK // M_PAIR)],
            pltpu.VMEM(op_shape, BF), pltpu.VMEM(op_shape, BF),
            pltpu.VMEM((N_OCT, 8 * SEG_PITCH, LANES), F32), pltpu.VMEM((N_OCT, 8 * SEG_PITCH, LANES), F32),
            pltpu.VMEM((N_COL, S5_ROWS, LANES), F32), pltpu.VMEM((N_COL, S5_ROWS, LANES), F32),
            pltpu.VMEM((N_COL, 8, LANES), F32),
        ],
        compiler_params=pltpu.CompilerParams(
            dimension_semantics=("arbitrary", "arbitrary"), vmem_limit_bytes=VMEM_LIMIT),
        name="s5_prompt",
    )(x2, g1, w_in, w_in, bblk, ctblk, lcol, a_tab, d)
    return out, hfin[:, :, 0, :], hfin[:, :, 1, :]


def _gate_cols(hn, win_ref, bg_ref, cols):
    w_cols = slice(GATE_COL0 + cols.start, GATE_COL0 + cols.stop)
    return _sigmoid(_dot(hn, win_ref[:, w_cols]) + bg_ref[:, cols])


def _merge_out(x, gates, attn_proj, ssm_proj, wo_ref):
    merged = gates[:, :D_MODEL] * attn_proj + gates[:, D_MODEL:] * ssm_proj
    return x + _dot(merged.astype(BF), wo_ref[...])


def _half_split(a, ar, lo):
    z = jnp.zeros_like(a)
    return (jnp.where(lo, a, z).astype(BF), jnp.where(lo, z, ar).astype(BF),
            jnp.where(lo, ar, z).astype(BF), jnp.where(lo, z, a).astype(BF))


def _softmax_terms_t(st, bias_t, sink):
    st = st + bias_t
    m = jnp.maximum(jnp.max(st, axis=0, keepdims=True), sink)
    return jnp.exp2(st - m).astype(BF), jnp.exp2(sink - m)


def _mix_prompt_kernel(sinks_ref, x_ref, y_ref, rc_ref, rs1_ref, rs2_ref, g1_ref, win_ref,
                       bg_ref, wba_ref, wbs_ref, wo_ref, wglu_ref, bglu_ref,
                       x1_ref, kwin_ref, vwin_ref, kprev, vprev):
    t = pl.program_id(1)

    @pl.when(t == 0)
    def _():
        kprev[...] = jnp.zeros_like(kprev)
        vprev[...] = jnp.zeros_like(vprev)

    x = x_ref[...]
    hn = _rms(x, g1_ref[...]).astype(BF)
    qkv = _dot(hn, win_ref[:, :QKV_WIDTH])
    z = _gelu_tanh(y_ref[...])
    glu = _dot(z.astype(BF), wglu_ref[...]) + bglu_ref[...]
    rc, rs1, rs2 = rc_ref[...], rs1_ref[...], rs2_ref[...]
    scale = HEAD_DIM ** -0.5 * LOG2_E
    q = [(_rope(qkv[:, c * LANES:(c + 1) * LANES], rc, rs1, rs2) * scale).astype(BF)
         for c in range(ATTN_WIDTH // LANES)]
    k = _rope(qkv[:, ATTN_WIDTH:ATTN_WIDTH + KV_WIDTH], rc, rs1, rs2)
    v = qkv[:, ATTN_WIDTH + KV_WIDTH:ATTN_WIDTH + 2 * KV_WIDTH]
    ssm = (z * _sigmoid(glu)).astype(BF)
    ssm_proj = _dot(ssm, wbs_ref[...])

    w = WINDOW
    n_sub = MIX_TB // w
    lane = lax.broadcasted_iota(jnp.int32, (2 * w, LANES), 1)
    lo = lane < HEAD_DIM
    kj = lax.broadcasted_iota(jnp.int32, (2 * w, w), 0)
    qi = lax.broadcasted_iota(jnp.int32, (2 * w, w), 1)
    band = (kj > qi) & (kj <= qi + w)
    first = band & ((kj >= w) | (t > 0))
    bias_band = jnp.where(band, 0.0, NEG_BIG).astype(F32)
    bias_first = jnp.where(first, 0.0, NEG_BIG).astype(F32)
    col = lax.broadcasted_iota(jnp.int32, (1, 2 * w), 1)
    ones_row = (lax.broadcasted_iota(jnp.int32, (HEAD_DIM, 2 * w), 0) == 0).astype(F32)

    def sink_row(ha, hb):
        return jnp.where(col < w, sinks_ref[ha], sinks_ref[hb]) * LOG2_E

    gate_w = GATE_WIDTH // (n_sub * N_KV_HEADS)
    operands = {}

    def sub_block_operands(sb):
        if sb not in operands:
            cur = slice(sb * w, (sb + 1) * w)
            if sb == 0:
                kcat = jnp.concatenate([kprev[...], k[cur]], axis=0)
                vcat = jnp.concatenate([vprev[...], v[cur]], axis=0)
                bias = bias_first
            else:
                kcat = k[(sb - 1) * w:(sb + 1) * w]
                vcat = v[(sb - 1) * w:(sb + 1) * w]
                bias = bias_band
            k_split = _half_split(kcat, pltpu.roll(kcat, HEAD_DIM, 1), lo)
            vt = vcat.T
            v_rows = [(jnp.concatenate([vt[g * HEAD_DIM:(g + 1) * HEAD_DIM], ones_row], axis=0).astype(BF),
                       jnp.concatenate([ones_row, vt[g * HEAD_DIM:(g + 1) * HEAD_DIM]], axis=0).astype(BF))
                      for g in range(N_KV_HEADS)]
            operands[sb] = (k_split, v_rows, jnp.concatenate([bias, bias], axis=1))
        return operands[sb]

    def scores(sb, grp):
        k_split, _, _ = sub_block_operands(sb)
        cur = slice(sb * w, (sb + 1) * w)
        qq = jnp.concatenate([q[2 * grp][cur], q[2 * grp + 1][cur]], axis=0)
        return _dot_nt(k_split[2 * grp], qq), _dot_nt(k_split[2 * grp + 1], qq)

    def weighted_values(sb, grp, s_e, s_o):
        _, v_rows, bias2 = sub_block_operands(sb)
        p_e, sink_e = _softmax_terms_t(s_e, bias2, sink_row(4 * grp, 4 * grp + 2))
        p_o, sink_o = _softmax_terms_t(s_o, bias2, sink_row(4 * grp + 1, 4 * grp + 3))
        v_e, v_o = v_rows[grp]
        d_e, d_o = _dot(v_e, p_e), _dot(v_o, p_o)
        o2 = jnp.concatenate(
            [d_e[:HEAD_DIM] * (1.0 / (d_e[HEAD_DIM:HEAD_DIM + 1] + sink_e)),
             d_o[HEAD_DIM:] * (1.0 / (d_o[0:1] + sink_o))], axis=0)
        return [o2[:, :w].T, o2[:, w:].T]

    chains = [(sb, grp) for sb in range(n_sub) for grp in range(N_KV_HEADS)]
    gate_cols, outs = [], []
    s_next = scores(*chains[0])
    for i, (sb, grp) in enumerate(chains):
        s_cur = s_next
        gate_cols.append(_gate_cols(hn, win_ref, bg_ref, slice(i * gate_w, (i + 1) * gate_w)))
        if i + 1 < len(chains):
            s_next = scores(*chains[i + 1])
        outs += weighted_values(sb, grp, *s_cur)
    per_sb = 2 * N_KV_HEADS
    attn = jnp.concatenate(
        [jnp.concatenate(outs[sb * per_sb:(sb + 1) * per_sb], axis=1) for sb in range(n_sub)],
        axis=0).astype(BF)
    gates = jnp.concatenate(gate_cols, axis=1)

    kprev[...] = k[MIX_TB - w:]
    vprev[...] = v[MIX_TB - w:]
    kwin_ref[...] = k[MIX_TB - w:]
    vwin_ref[...] = v[MIX_TB - w:]
    x1_ref[...] = _merge_out(x, gates, _dot(attn, wba_ref[...]), ssm_proj, wo_ref)


def _mix_prompt(x, y_ssm, rc, rs1, rs2, sinks, g1, w_in, bg, wba, wbs, wo, wglu, bglu):
    b, t, _ = x.shape
    nblk = t // MIX_TB
    x2 = x.reshape(b * t, D_MODEL)
    row_map = lambda i, j: (i * nblk + j, 0)
    x1, kwin, vwin = pl.pallas_call(
        _mix_prompt_kernel,
        grid=(b, nblk),
        in_specs=[
            pl.BlockSpec(memory_space=pltpu.SMEM),
            pl.BlockSpec((MIX_TB, D_MODEL), row_map),
            pl.BlockSpec((MIX_TB, SSM_WIDTH), row_map),
            pl.BlockSpec((MIX_TB, LANES), lambda i, j: (j, 0)),
            pl.BlockSpec((MIX_TB, LANES), lambda i, j: (j, 0)),
            pl.BlockSpec((MIX_TB, LANES), lambda i, j: (j, 0)),
            _const_spec((1, D_MODEL)),
            _const_spec(w_in.shape),
            _const_spec(bg.shape),
            _const_spec(wba.shape),
            _const_spec(wbs.shape),
            _const_spec(wo.shape),
            _const_spec(wglu.shape),
            _const_spec(bglu.shape),
        ],
        out_specs=[
            pl.BlockSpec((MIX_TB, D_MODEL), row_map),
            pl.BlockSpec((None, WINDOW, KV_WIDTH), lambda i, j: (i, 0, 0)),
            pl.BlockSpec((None, WINDOW, KV_WIDTH), lambda i, j: (i, 0, 0)),
        ],
        out_shape=[
            jax.ShapeDtypeStruct((b * t, D_MODEL), F32),
            jax.ShapeDtypeStruct((b, WINDOW, KV_WIDTH), F32),
            jax.ShapeDtypeStruct((b, WINDOW, KV_WIDTH), F32),
        ],
        scratch_shapes=[pltpu.VMEM((WINDOW, KV_WIDTH), F32), pltpu.VMEM((WINDOW, KV_WIDTH), F32)],
        compiler_params=pltpu.CompilerParams(
            dimension_semantics=("arbitrary", "arbitrary"), vmem_limit_bytes=VMEM_LIMIT),
        name="mix_prompt",
    )(sinks, x2, y_ssm, rc, rs1, rs2, g1, w_in, bg, wba, wbs, wo, wglu, bglu)
    return x1, kwin, vwin


QROWS = 16
SAMPLE_TB = 32


def _sample_layer_kernel(x_ref, kbuf_ref, vbuf_ref, rc_ref, rs1_ref, rs2_ref, sinkc_ref, g1_ref, win_ref,
                         xall_ref, h0re_ref, h0im_ref, bg_ref, wbad_ref, wbs_ref, wo_ref, lb_ref, bblk_ref,
                         ctblk_ref, d_ref, wglu_ref, bglu_ref,
                         kout_ref, vout_ref, x1_ref, hre_ref, him_ref, qz, o3_all):
    nb = x_ref.shape[0]
    step = pl.program_id(0)
    hn = _rms(x_ref[...], g1_ref[...]).astype(BF)
    qkv = _dot(hn, win_ref[:, :QKV_WIDTH])
    rc, rs1, rs2 = rc_ref[...], rs1_ref[...], rs2_ref[...]
    scale = HEAD_DIM ** -0.5
    k_new = _rope(qkv[:, ATTN_WIDTH:ATTN_WIDTH + KV_WIDTH], rc, rs1, rs2)
    v_new = qkv[:, ATTN_WIDTH + KV_WIDTH:ATTN_WIDTH + 2 * KV_WIDTH]
    pad = jnp.zeros((LANES - nb, KV_WIDTH), F32)
    k_new_t = jnp.concatenate([k_new, pad], axis=0).T
    v_new_t = jnp.concatenate([v_new, pad], axis=0).T

    lane = lax.broadcasted_iota(jnp.int32, (nb, LANES), 1)
    lo = lane < HEAD_DIM
    qz[...] = jnp.zeros_like(qz)
    for c in range(ATTN_WIDTH // LANES):
        qc = _rope(qkv[:, c * LANES:(c + 1) * LANES], rc, rs1, rs2) * scale
        qr = pltpu.roll(qc, HEAD_DIM, 1)
        zero = jnp.zeros_like(qc)
        if c < 2:
            even, odd = jnp.where(lo, qc, zero), jnp.where(lo, qr, zero)
        else:
            even, odd = jnp.where(lo, zero, qr), jnp.where(lo, zero, qc)
        qz[pl.ds(2 * c, nb, stride=QROWS), :] = even
        qz[pl.ds(2 * c + 1, nb, stride=QROWS), :] = odd

    last = lax.broadcasted_iota(jnp.int32, (KV_WIDTH, WINDOW), 1) == WINDOW - 1
    for b in range(nb):
        kout_ref[b] = jnp.where(last, k_new_t[:, b:b + 1], pltpu.roll(kbuf_ref[b], WINDOW - 1, 1))
        vout_ref[b] = jnp.where(last, v_new_t[:, b:b + 1], pltpu.roll(vbuf_ref[b], WINDOW - 1, 1))

    sink = sinkc_ref[...]
    q3 = qz[...].reshape(nb, QROWS, LANES).astype(BF)
    s = jnp.einsum('bhd,bdk->bhk', q3, kout_ref[...].astype(BF), preferred_element_type=F32)
    m = jnp.maximum(jnp.max(s, axis=-1, keepdims=True), sink)
    p = jnp.exp(s - m)
    den = jnp.sum(p, axis=-1, keepdims=True) + jnp.exp(sink - m)
    p = (p * (1.0 / den)).astype(BF)
    o3 = jnp.einsum('bhk,bdk->bhd', p, vout_ref[...].astype(BF), preferred_element_type=F32)
    o3_all[pl.ds(pl.multiple_of(step * (nb * QROWS), nb * QROWS), nb * QROWS), :] = o3.reshape(nb * QROWS, LANES)

    @pl.when(step == pl.num_programs(0) - 1)
    def _():
        _sample_tail(xall_ref, o3_all, h0re_ref, h0im_ref, g1_ref, win_ref, bg_ref, wbad_ref, wbs_ref, wo_ref,
                     lb_ref, bblk_ref, ctblk_ref, d_ref, wglu_ref, bglu_ref, x1_ref, hre_ref, him_ref)


def _sample_layer(x, kbuf, vbuf, h0re, h0im, rc, rs1, rs2, sinkc, g1, w_in, bg, wbad, wbs, wo, lb, bblk, ctblk,
                  d, wglu, bglu):
    nb = x.shape[0]
    tb = SAMPLE_TB
    blocked = (x, kbuf, vbuf)
    consts = (rc, rs1, rs2, sinkc, g1, w_in, x, h0re, h0im, bg, wbad, wbs, wo, lb, bblk, ctblk, d, wglu, bglu)
    kv_spec = pl.BlockSpec((tb, KV_WIDTH, WINDOW), lambda i: (i, 0, 0))
    whole = lambda shape: pl.BlockSpec(shape, lambda i: (0,) * len(shape))
    return pl.pallas_call(
        _sample_layer_kernel,
        grid=(nb // tb,),
        in_specs=[pl.BlockSpec((tb, D_MODEL), lambda i: (i, 0)), kv_spec, kv_spec]
        + [_const_spec(a.shape) for a in consts],
        out_specs=[kv_spec, kv_spec, whole((nb, D_MODEL)), whole((N_STATES, nb)), whole((N_STATES, nb))],
        out_shape=[
            jax.ShapeDtypeStruct((nb, KV_WIDTH, WINDOW), F32),
            jax.ShapeDtypeStruct((nb, KV_WIDTH, WINDOW), F32),
            jax.ShapeDtypeStruct((nb, D_MODEL), F32),
            jax.ShapeDtypeStruct((N_STATES, nb), F32),
            jax.ShapeDtypeStruct((N_STATES, nb), F32),
        ],
        scratch_shapes=[pltpu.VMEM((tb * QROWS, LANES), F32), pltpu.VMEM((nb * QROWS, LANES), F32)],
        compiler_params=pltpu.CompilerParams(
            dimension_semantics=("arbitrary",), vmem_limit_bytes=VMEM_LIMIT),
        name="sample_layer",
    )(*blocked, *consts)


def _sample_tail(x_ref, o3_ref, h0re_ref, h0im_ref, g1_ref, win_ref, bg_ref, wbad_ref,
                 wbs_ref, wo_ref, lb_ref, bblk_ref, ctblk_ref, d_ref, wglu_ref, bglu_ref,
                 x1_ref, hre_ref, him_ref):
    nb = x_ref.shape[0]
    x = x_ref[...]
    hn = _rms(x, g1_ref[...]).astype(BF)

    lane = lax.broadcasted_iota(jnp.int32, (nb, LANES), 1)
    lo = lane < HEAD_DIM
    a = jnp.zeros((nb, D_MODEL), F32)
    zero = jnp.zeros((nb, LANES), F32)
    for h in range(N_Q_HEADS):
        oh = o3_ref[pl.ds(h, nb, stride=QROWS), :]
        oh = jnp.where(lo, oh, zero) if h < N_Q_HEADS // 2 else jnp.where(lo, zero, oh)
        a = a + _dot(oh.astype(BF), wbad_ref[h])

    u = _dot(hn, win_ref[:, U_COL0:GATE_COL0])
    ub = u.astype(BF)
    lre, lim = lb_ref[0:1, :], lb_ref[1:2, :]
    y_cols = []
    for o in range(N_OCT):
        sl = slice(o * OCT_STATES, (o + 1) * OCT_STATES)
        bu = _dot(ub[:, o * LANES:(o + 1) * LANES], bblk_ref[o])
        blocks = [slice(c * LANES, (c + 1) * LANES) for c in range(o * OCT_COL, (o + 1) * OCT_COL)]
        h0r = jnp.concatenate([h0re_ref[rows, :].T for rows in blocks], axis=1)
        h0i = jnp.concatenate([h0im_ref[rows, :].T for rows in blocks], axis=1)
        hr = bu[:, :OCT_STATES] + (lre[:, sl] * h0r - lim[:, sl] * h0i)
        hi = bu[:, OCT_STATES:] + (lre[:, sl] * h0i + lim[:, sl] * h0r)
        for cc, rows in enumerate(blocks):
            hre_ref[rows, :] = hr[:, cc * LANES:(cc + 1) * LANES].T
            him_ref[rows, :] = hi[:, cc * LANES:(cc + 1) * LANES].T
        y_cols.append(_dot_nt(jnp.concatenate([hr, hi], axis=1).astype(BF), ctblk_ref[o]))
    y = jnp.concatenate(y_cols, axis=1) + d_ref[...] * u
    z = _gelu_tanh(y)
    gate = _dot(z.astype(BF), wglu_ref[...]) + bglu_ref[...]
    ssm = (z * _sigmoid(gate)).astype(BF)

    gates = _gate_cols(hn, win_ref, bg_ref, slice(0, GATE_WIDTH))
    x1_ref[...] = _merge_out(x, gates, a, _dot(ssm, wbs_ref[...]), wo_ref)


FFN_W_CHUNKS = 16
FFN_W_SLOTS = 4


def _stream_cast_weights(jobs):
    tasks, used = [], {}
    for w_hbm, w_bf, stage, sem in jobs:
        rows = stage.shape[1]
        for k in range(w_hbm.shape[0] // rows):
            slot = used.get(id(stage), 0) % stage.shape[0]
            used[id(stage)] = used.get(id(stage), 0) + 1
            copy = pltpu.make_async_copy(w_hbm.at[pl.ds(k * rows, rows), :], stage.at[slot], sem.at[slot])
            tasks.append((copy, w_bf, stage, slot, k * rows, rows))
    ahead = min(stage.shape[0] for _, _, stage, _ in jobs) - 1
    for copy, *_ in tasks[:ahead]:
        copy.start()
    for i, (copy, w_bf, stage, slot, row0, rows) in enumerate(tasks):
        if i + ahead < len(tasks):
            tasks[i + ahead][0].start()
        copy.wait()
        w_bf[row0:row0 + rows, :] = stage[slot].astype(BF)


def _ffn_rows(x, g2_ref, wgate, wup, wdown, gf_ref):
    h = _rms(x, g2_ref[...]).astype(BF)
    gate = _dot(h, wgate[...])
    up = _dot(h, wup[...])
    act = (gate * _sigmoid(gate) * up).astype(BF)
    x2 = x + _dot(act, wdown[...])
    return _rms(x2, gf_ref[...])


def _ffn_kernel(xp_ref, xs_ref, g2_ref, wgate_hbm, wup_hbm, wdown_hbm, gf_ref, yp_ref, ys_ref,
                wgate, wup, wdown, stage_in, stage_out, sem_in, sem_out):
    i = pl.program_id(0)
    n_prompt = pl.num_programs(0) - 1

    @pl.when(i == 0)
    def _():
        _stream_cast_weights([(wgate_hbm, wgate, stage_in, sem_in), (wup_hbm, wup, stage_in, sem_in),
                              (wdown_hbm, wdown, stage_out, sem_out)])

    @pl.when(i < n_prompt)
    def _():
        half = FFN_TB // FFN_SPLIT
        for rows in (slice(h * half, (h + 1) * half) for h in range(FFN_SPLIT)):
            yp_ref[rows, :] = _ffn_rows(xp_ref[rows, :], g2_ref, wgate, wup, wdown, gf_ref)

    @pl.when(i == n_prompt)
    def _():
        ys_ref[...] = _ffn_rows(xs_ref[...], g2_ref, wgate, wup, wdown, gf_ref)


def _ffn(xp, xs, g2, wgate, wup, wdown, gf):
    n, ns = xp.shape[0], xs.shape[0]
    n_prompt = n // FFN_TB
    prompt_map = lambda i: (jnp.minimum(i, n_prompt - 1), 0)
    return pl.pallas_call(
        _ffn_kernel,
        grid=(n_prompt + 1,),
        in_specs=[
            pl.BlockSpec((FFN_TB, D_MODEL), prompt_map),
            _const_spec((ns, D_MODEL)),
            _const_spec((1, D_MODEL)),
            pl.BlockSpec(memory_space=pl.ANY),
            pl.BlockSpec(memory_space=pl.ANY),
            pl.BlockSpec(memory_space=pl.ANY),
            _const_spec((1, D_MODEL)),
        ],
        out_specs=[
            pl.BlockSpec((FFN_TB, D_MODEL), prompt_map),
            pl.BlockSpec((ns, D_MODEL), lambda i: (0, 0)),
        ],
        out_shape=[
            jax.ShapeDtypeStruct((n, D_MODEL), F32),
            jax.ShapeDtypeStruct((ns, D_MODEL), F32),
        ],
        scratch_shapes=[
            pltpu.VMEM((D_MODEL, D_FF), BF), pltpu.VMEM((D_MODEL, D_FF), BF), pltpu.VMEM((D_FF, D_MODEL), BF),
            pltpu.VMEM((FFN_W_SLOTS, D_MODEL // FFN_W_CHUNKS, D_FF), F32),
            pltpu.VMEM((FFN_W_SLOTS, D_FF // FFN_W_CHUNKS, D_MODEL), F32),
            pltpu.SemaphoreType.DMA((FFN_W_SLOTS,)), pltpu.SemaphoreType.DMA((FFN_W_SLOTS,)),
        ],
        compiler_params=pltpu.CompilerParams(
            dimension_semantics=("arbitrary",), vmem_limit_bytes=VMEM_LIMIT),
        name="ffn",
    )(xp, xs, g2, wgate, wup, wdown, gf)


def _rope_tables(pos):
    pos = np.asarray(pos, np.float64)
    inv_freq = ROPE_THETA ** (-(np.arange(ROPE_HALF, dtype=np.float64) * 2.0 / ROPE_DIM))
    ang = pos[:, None] * inv_freq[None, :]
    cos, sin = np.cos(ang), np.sin(ang)
    pad = np.zeros((pos.shape[0], HEAD_DIM - ROPE_DIM))
    zero = np.zeros_like(sin)
    rc = np.concatenate([cos, cos, pad + 1.0], axis=1)
    rs1 = np.concatenate([zero, sin, pad], axis=1)
    rs2 = np.concatenate([-sin, zero, pad], axis=1)
    rep = LANES // HEAD_DIM
    return tuple(jnp.asarray(np.tile(a, (1, rep)), F32) for a in (rc, rs1, rs2))


def _cmul(ar, ai, br, bi):
    return ar * br - ai * bi, ar * bi + ai * br


def _ssm_tables(lam_re, lam_im, log_dt, b_re, b_im, c_re, c_im):
    dt = jnp.exp(log_dt)[:, None]
    mag = jnp.exp(lam_re * dt)
    lb_re = mag * jnp.cos(lam_im * dt)
    lb_im = mag * jnp.sin(lam_im * dt)
    den = lam_re * lam_re + lam_im * lam_im
    nr = lb_re - 1.0
    k_re = ((nr * lam_re + lb_im * lam_im) / den)[..., None]
    k_im = ((lb_im * lam_re - nr * lam_im) / den)[..., None]
    bb_re = k_re * b_re - k_im * b_im
    bb_im = k_re * b_im + k_im * b_re

    a_re, a_im = lb_re, lb_im
    for _ in range(int(math.log2(CHUNK))):
        a_re, a_im = _cmul(a_re, a_im, a_re, a_im)
    s_re, s_im = a_re, a_im
    for _ in range(int(math.log2(SEG))):
        s_re, s_im = _cmul(s_re, s_im, s_re, s_im)

    eye = jnp.eye(OCT, dtype=F32).reshape(1, OCT, 1, OCT, 1)

    def block_diag(a):
        r, c = a.shape[1:]
        return (a.reshape(N_OCT, OCT, r, 1, c) * eye).reshape(N_OCT, OCT * r, OCT * c)

    bblk = jnp.concatenate([block_diag(jnp.swapaxes(bb_re, 1, 2)),
                            block_diag(jnp.swapaxes(bb_im, 1, 2))], axis=2)
    ctblk = jnp.concatenate([block_diag(c_re), block_diag(-c_im)], axis=2)

    oct_cols = lambda a: a.reshape(N_OCT, 1, OCT_STATES)
    lcol = jnp.concatenate([oct_cols(lb_re), oct_cols(lb_im),
                            jnp.zeros((N_OCT, 6, OCT_STATES), F32)], axis=1)

    flat = lambda a: a.reshape(1, N_STATES)
    col = lambda a: a.reshape(N_COL, 1, LANES)
    a_tab = jnp.concatenate([col(a_re), col(a_im), col(s_re), col(s_im),
                             jnp.zeros((N_COL, 4, LANES), F32)], axis=1)
    lb = jnp.concatenate([flat(lb_re), flat(lb_im)], axis=0)
    return bblk, ctblk, lcol, a_tab, lb


def kernel(x_prompt, x_sample, state_k_win, state_v_win, state_ssm_re, state_ssm_im, norm1_g, w_in, b_gate, attn_sinks, ssm_lam_re, ssm_lam_im, ssm_log_dt, ssm_b_re, ssm_b_im, ssm_c_re, ssm_c_im, ssm_d, w_glu, b_glu, w_branch_attn, w_branch_ssm, w_out, norm2_g, w_ffn_gate, w_ffn_up, w_ffn_down, norm_f_g):
    depth = w_in.shape[0]
    assert depth == 1
    b, t, _ = x_prompt.shape
    nb, s_len, _ = x_sample.shape
    assert s_len == 1 and state_k_win.shape[2] == WINDOW
    l = 0
    assert w_in.shape[2] == IN_WIDTH
    w_in_b = w_in[l].astype(BF)
    g1 = norm1_g[l].reshape(1, D_MODEL)
    g2 = norm2_g[l].reshape(1, D_MODEL)
    gf = norm_f_g.reshape(1, D_MODEL)
    bg = b_gate[l].reshape(1, GATE_WIDTH)
    d = ssm_d[l].reshape(1, SSM_WIDTH)
    wglu = w_glu[l].astype(BF)
    bglu = b_glu[l].reshape(1, SSM_WIDTH)
    wba = w_branch_attn[l].astype(BF)
    wbs = w_branch_ssm[l].astype(BF)
    wo = w_out[l].astype(BF)
    sinks = attn_sinks[l]

    bblk, ctblk, lcol, a_tab, lb = _ssm_tables(
        ssm_lam_re[l], ssm_lam_im[l], ssm_log_dt[l], ssm_b_re[l], ssm_b_im[l], ssm_c_re[l], ssm_c_im[l])

    rc, rs1, rs2 = _rope_tables(np.arange(t))
    yssm_p, hre_p, him_p = _s5_prompt(x_prompt, g1, w_in_b, bblk, ctblk, lcol, a_tab, d)
    x1_p, kwin_p, vwin_p = _mix_prompt(x_prompt, yssm_p, rc, rs1, rs2, sinks, g1, w_in_b, bg, wba, wbs, wo,
                                       wglu, bglu)

    rcs, rs1s, rs2s = _rope_tables(PAST_LEN + np.arange(1))
    sinkc = jnp.concatenate([sinks, jnp.zeros((QROWS - N_Q_HEADS,), F32)]).reshape(QROWS, 1)
    wbad = jnp.concatenate([wba.reshape(N_Q_HEADS, HEAD_DIM, D_MODEL)] * 2, axis=1)
    xs = x_sample.reshape(nb, D_MODEL)
    key_minor = lambda a: jnp.swapaxes(a.reshape(nb, WINDOW, KV_WIDTH), 1, 2)
    kwin_s, vwin_s, x1_s, hre_s, him_s = _sample_layer(
        xs, key_minor(state_k_win[l]), key_minor(state_v_win[l]),
        state_ssm_re[l].reshape(nb, N_STATES).T, state_ssm_im[l].reshape(nb, N_STATES).T,
        rcs, rs1s, rs2s, sinkc, g1, w_in_b, bg, wbad, wbs, wo, lb, bblk.astype(BF), ctblk.astype(BF),
        d, wglu, bglu)
    y_p, y_s = _ffn(x1_p, x1_s, g2, w_ffn_gate[l], w_ffn_up[l], w_ffn_down[l], gf)
    y_p = y_p.reshape(b, t, D_MODEL)
    y_s = y_s.reshape(nb, 1, D_MODEL)

    kv_shape_p = (1, b, WINDOW, N_KV_HEADS, HEAD_DIM)
    st_shape_p = (1, b, N_SSM_GROUPS, SSM_STATE)
    kv_shape_s = (1, nb, WINDOW, N_KV_HEADS, HEAD_DIM)
    st_shape_s = (1, nb, N_SSM_GROUPS, SSM_STATE)
    return (y_p, y_s,
            kwin_p.reshape(kv_shape_p), vwin_p.reshape(kv_shape_p),
            hre_p.reshape(st_shape_p), him_p.reshape(st_shape_p),
            jnp.swapaxes(kwin_s, 1, 2).reshape(kv_shape_s), jnp.swapaxes(vwin_s, 1, 2).reshape(kv_shape_s),
            hre_s.T.reshape(st_shape_s), him_s.T.reshape(st_shape_s))
```

```python
import math

import jax
import jax.numpy as jnp
import numpy as np
from jax import lax
from jax.experimental import pallas as pl
from jax.experimental.pallas import tpu as pltpu

D_MODEL = 1024
N_Q_HEADS = 8
N_KV_HEADS = 2
HEAD_DIM = 64
ATTN_WIDTH = N_Q_HEADS * HEAD_DIM
KV_WIDTH = N_KV_HEADS * HEAD_DIM
WINDOW = 128
ROPE_DIM = HEAD_DIM // 4
ROPE_HALF = ROPE_DIM // 2
ROPE_THETA = 500000.0
SSM_WIDTH = D_MODEL // 2
SSM_GROUP = 16
N_SSM_GROUPS = SSM_WIDTH // SSM_GROUP
SSM_STATE = 64
N_STATES = N_SSM_GROUPS * SSM_STATE
GATE_WIDTH = 2 * D_MODEL
QKV_WIDTH = ATTN_WIDTH + 2 * KV_WIDTH
U_COL0 = QKV_WIDTH
GATE_COL0 = U_COL0 + SSM_WIDTH
IN_WIDTH = GATE_COL0 + GATE_WIDTH
D_FF = -(-8 * D_MODEL // (3 * 256)) * 256
NORM_EPS = 1e-5
PAST_LEN = 8192

LANES = 128
SUBLANES = 8
CHUNK = 8
OCT = LANES // SSM_GROUP
N_OCT = N_SSM_GROUPS // OCT
OCT_STATES = OCT * SSM_STATE
OCT_COL = OCT_STATES // LANES
N_COL = N_STATES // LANES
N_SEG = SUBLANES
SEG = 16
S5_ROWS = N_SEG * SEG
M_PAIR = 2
U_HALF = SSM_WIDTH // 2
SEG_PITCH = SEG * CHUNK + SUBLANES
MIX_TB = 512
FFN_TB = 1024
FFN_SPLIT = 4
NEG_BIG = -1e30
LOG2_E = math.log2(math.e)
VMEM_LIMIT = 56 * 1024 * 1024

BF = jnp.bfloat16
F32 = jnp.float32


def _dot(a, b):
    return jnp.dot(a, b, preferred_element_type=F32)


def _dot_nt(a, b):
    return lax.dot_general(a, b, (((1,), (1,)), ((), ())), preferred_element_type=F32)


def _dot_nt_split(a, b):
    a_hi, b_hi = a.astype(BF), b.astype(BF)
    a_lo = (a - a_hi.astype(F32)).astype(BF)
    b_lo = (b - b_hi.astype(F32)).astype(BF)
    return _dot_nt(a_hi, b_hi) + (_dot_nt(a_hi, b_lo) + _dot_nt(a_lo, b_hi))


def _rms(x, g):
    return x * lax.rsqrt(jnp.mean(x * x, axis=-1, keepdims=True) + NORM_EPS) * g


def _sigmoid(x):
    return 1.0 / (1.0 + jnp.exp2(x * -LOG2_E))


def _gelu_tanh(x):
    c = math.sqrt(2.0 / math.pi)
    return 0.5 * x * (1.0 + jnp.tanh(c * (x + 0.044715 * (x * x * x))))


def _rope(a, rc, rs1, rs2):
    return a * rc + pltpu.roll(a, ROPE_HALF, 1) * rs1 + pltpu.roll(a, LANES - ROPE_HALF, 1) * rs2


def _const_spec(shape):
    nd = len(shape)
    return pl.BlockSpec(shape, lambda *_: (0,) * nd, pipeline_mode=pl.Buffered(1))


def _build_chunk_operators(bblk_ref, ctblk_ref, lcol_ref, m_s, e_s, f_s):
    for o in range(N_OCT):
        ct = ctblk_ref[o]
        lr, li = lcol_ref[o, 0:1, :], lcol_ref[o, 1:2, :]
        er, ei = bblk_ref[o, :, :OCT_STATES], bblk_ref[o, :, OCT_STATES:]
        k_blk = []
        for tau in range(CHUNK):
            e_cat = jnp.concatenate([er, ei], axis=1)
            i = CHUNK - 1 - tau
            e_s[o, i * LANES:(i + 1) * LANES, :] = e_cat.astype(BF)
            k_blk.append(_dot_nt_split(e_cat, ct).astype(BF))
            er, ei = er * lr - ei * li, er * li + ei * lr
        zero = jnp.zeros((LANES, LANES), BF)
        for j in range(CHUNK):
            jt, jj = divmod(j, M_PAIR)
            for i in range(M_PAIR * (jt + 1)):
                m_s[jt][o, i * LANES:(i + 1) * LANES, jj * LANES:(jj + 1) * LANES] = (
                    k_blk[j - i] if j >= i else zero)
        tr, ti = ct[:, :OCT_STATES], -ct[:, OCT_STATES:]
        for j in range(CHUNK):
            tr, ti = tr * lr - ti * li, tr * li + ti * lr
            f_s[o, :OCT_STATES, j * LANES:(j + 1) * LANES] = tr.T.astype(BF)
            f_s[o, OCT_STATES:, j * LANES:(j + 1) * LANES] = (-ti).T.astype(BF)


def _s5_prompt_kernel(x_ref, g1_ref, wu0_ref, wu1_ref, bblk_ref, ctblk_ref, lcol_ref, a_ref, d_ref,
                      out_ref, hfin_ref,
                      m0_ref, m1_ref, m2_ref, m3_ref, e_ref, f_ref, us, ys, sre, sim, car):
    m_ref = (m0_ref, m1_ref, m2_ref, m3_ref)
    blk = pl.program_id(1)

    @pl.when((pl.program_id(0) == 0) & (blk == 0))
    def _():
        _build_chunk_operators(bblk_ref, ctblk_ref, lcol_ref, m_ref, e_ref, f_ref)

    @pl.when(blk == 0)
    def _():
        car[...] = jnp.zeros_like(car)

    seg_tokens = SEG * CHUNK
    seg_rows = [slice(s * SEG_PITCH, s * SEG_PITCH + seg_tokens) for s in range(N_SEG)]
    tok_rows = [slice(s * seg_tokens, (s + 1) * seg_tokens) for s in range(N_SEG)]

    for s in range(0, N_SEG, 2):
        hn = _rms(x_ref[s * seg_tokens:(s + 2) * seg_tokens, :], g1_ref[...]).astype(BF)
        u = jnp.concatenate([_dot(hn, wu0_ref[...]), _dot(hn, wu1_ref[...])], axis=1)
        for half in range(2):
            for cc in range(N_OCT):
                us[cc, seg_rows[s + half], :] = u[tok_rows[half], cc * LANES:(cc + 1) * LANES]

    sub = lax.broadcasted_iota(jnp.int32, (N_SEG, LANES), 0)

    def scan_column(c):
        tab = a_ref[c]
        are, aim = tab[0:1], tab[1:2]
        bre, bim = tab[2:3], tab[3:4]

        def step(cr, ci, r, keep_entering):
            slab = slice(r * N_SEG, (r + 1) * N_SEG)
            s_r, s_i = sre[c, slab, :], sim[c, slab, :]
            if keep_entering:
                sre[c, slab, :] = cr
                sim[c, slab, :] = ci
            return are * cr - aim * ci + s_r, are * ci + aim * cr + s_i

        cr = jnp.zeros((N_SEG, LANES), F32)
        ci = jnp.zeros((N_SEG, LANES), F32)
        for r in range(SEG):
            cr, ci = step(cr, ci, r, False)
        cv = car[c]
        pr, pi = cv[0:1], cv[1:2]
        sr = jnp.zeros((N_SEG, LANES), F32)
        si = jnp.zeros((N_SEG, LANES), F32)
        for s in range(N_SEG):
            sr = jnp.where(sub == s, pr, sr)
            si = jnp.where(sub == s, pi, si)
            pr, pi = (bre * pr - bim * pi + cr[s:s + 1], bre * pi + bim * pr + ci[s:s + 1])
        end = jnp.where(sub == 0, pr, jnp.where(sub == 1, pi, 0.0))
        car[c] = end
        hfin_ref[c] = end
        cr, ci = sr, si
        for r in range(SEG):
            cr, ci = step(cr, ci, r, True)

    for o in range(N_OCT):
        uo = jnp.concatenate(
            [jnp.concatenate([us[o, pl.ds(r * CHUNK + i, N_SEG, stride=SEG_PITCH), :] for r in range(SEG)],
                             axis=0).astype(BF) for i in range(CHUNK)], axis=1)
        s_end = _dot(uo, e_ref[o])
        cols = range(o * OCT_COL, (o + 1) * OCT_COL)
        for cc, c in enumerate(cols):
            sre[c] = s_end[:, cc * LANES:(cc + 1) * LANES]
            sim[c] = s_end[:, OCT_STATES + cc * LANES:OCT_STATES + (cc + 1) * LANES]
        y_in = jnp.concatenate(
            [_dot(uo[:, :(jt + 1) * M_PAIR * LANES], m_ref[jt][o]) for jt in range(CHUNK // M_PAIR)],
            axis=1)
        for c in cols:
            scan_column(c)
        hp = jnp.concatenate([sre[c] for c in cols] + [sim[c] for c in cols], axis=1).astype(BF)
        yo = y_in + _dot(hp, f_ref[o])
        for r in range(SEG):
            for j in range(CHUNK):
                ys[o, pl.ds(r * CHUNK + j, N_SEG, stride=SEG_PITCH), :] = (
                    yo[r * N_SEG:(r + 1) * N_SEG, j * LANES:(j + 1) * LANES])
        d_o = d_ref[:, o * LANES:(o + 1) * LANES]
        for s in range(N_SEG):
            out_ref[tok_rows[s], o * LANES:(o + 1) * LANES] = (
                ys[o, seg_rows[s], :] + d_o * us[o, seg_rows[s], :])


def _s5_prompt(x, g1, w_in, bblk, ctblk, lcol, a_tab, d):
    b, t, _ = x.shape
    tb = S5_ROWS * CHUNK
    nblk = t // tb
    x2 = x.reshape(b * t, D_MODEL)
    row_map = lambda i, j: (i * nblk + j, 0)
    op_shape = (N_OCT, CHUNK * LANES, CHUNK * LANES)
    out, hfin = pl.pallas_call(
        _s5_prompt_kernel,
        grid=(b, nblk),
        in_specs=[
            pl.BlockSpec((tb, D_MODEL), row_map),
            _const_spec((1, D_MODEL)),
            *[pl.BlockSpec((D_MODEL, U_HALF), lambda i, j, c=U_COL0 // U_HALF + h: (0, c),
                           pipeline_mode=pl.Buffered(1)) for h in range(2)],
            _const_spec(bblk.shape),
            _const_spec(ctblk.shape),
            _const_spec(lcol.shape),
            _const_spec(a_tab.shape),
            _const_spec((1, SSM_WIDTH)),
        ],
        out_specs=[
            pl.BlockSpec((tb, SSM_WIDTH), row_map),
            pl.BlockSpec((None, N_COL, SUBLANES, LANES), lambda i, j: (i, 0, 0, 0)),
        ],
        out_shape=[
            jax.ShapeDtypeStruct((b * t, SSM_WIDTH), F32),
            jax.ShapeDtypeStruct((b, N_COL, SUBLANES, LANES), F32),
        ],
        scratch_shapes=[
            *[pltpu.VMEM((N_OCT, (jt + 1) * M_PAIR * LANES, M_PAIR * LANES), BF)
              for jt in range(CHUNK // M_PAIR)],
            pltpu.VMEM(op_shape, BF), pltpu.VMEM(op_shape, BF),
            pltpu.VMEM((N_OCT, N_SEG * SEG_PITCH, LANES), F32),
            pltpu.VMEM((N_OCT, N_SEG * SEG_PITCH, LANES), F32),
            pltpu.VMEM((N_COL, S5_ROWS, LANES), F32), pltpu.VMEM((N_COL, S5_ROWS, LANES), F32),
            pltpu.VMEM((N_COL, SUBLANES, LANES), F32),
        ],
        compiler_params=pltpu.CompilerParams(
            dimension_semantics=("arbitrary", "arbitrary"), vmem_limit_bytes=VMEM_LIMIT),
        name="s5_prompt",
    )(x2, g1, w_in, w_in, bblk, ctblk, lcol, a_tab, d)
    return out, hfin[:, :, 0, :], hfin[:, :, 1, :]


def _gate_cols(hn, win_ref, bg_ref, cols):
    w_cols = slice(GATE_COL0 + cols.start, GATE_COL0 + cols.stop)
    return _sigmoid(_dot(hn, win_ref[:, w_cols]) + bg_ref[:, cols])


def _merge_out(x, gates, attn_proj, ssm_proj, wo_ref):
    merged = gates[:, :D_MODEL] * attn_proj + gates[:, D_MODEL:] * ssm_proj
    return x + _dot(merged.astype(BF), wo_ref[...])


def _half_split(a, ar, lo):
    z = jnp.zeros_like(a)
    return (jnp.where(lo, a, z).astype(BF), jnp.where(lo, z, ar).astype(BF),
            jnp.where(lo, ar, z).astype(BF), jnp.where(lo, z, a).astype(BF))


def _softmax_terms_t(st, bias_t, sink):
    st = st + bias_t
    m = jnp.maximum(jnp.max(st, axis=0, keepdims=True), sink)
    return jnp.exp2(st - m).astype(BF), jnp.exp2(sink - m)


def _mix_prompt_kernel(sinks_ref, x_ref, y_ref, rc_ref, rs1_ref, rs2_ref, g1_ref, win_ref,
                       bg_ref, wba_ref, wbs_ref, wo_ref, wglu_ref, bglu_ref,
                       x1_ref, kwin_ref, vwin_ref, kprev, vprev):
    t = pl.program_id(1)

    @pl.when(t == 0)
    def _():
        kprev[...] = jnp.zeros_like(kprev)
        vprev[...] = jnp.zeros_like(vprev)

    x = x_ref[...]
    hn = _rms(x, g1_ref[...]).astype(BF)
    qkv = _dot(hn, win_ref[:, :QKV_WIDTH])
    z = _gelu_tanh(y_ref[...])
    glu = _dot(z.astype(BF), wglu_ref[...]) + bglu_ref[...]
    rc, rs1, rs2 = rc_ref[...], rs1_ref[...], rs2_ref[...]
    scale = HEAD_DIM ** -0.5 * LOG2_E
    q = [(_rope(qkv[:, c * LANES:(c + 1) * LANES], rc, rs1, rs2) * scale).astype(BF)
         for c in range(ATTN_WIDTH // LANES)]
    k = _rope(qkv[:, ATTN_WIDTH:ATTN_WIDTH + KV_WIDTH], rc, rs1, rs2)
    v = qkv[:, ATTN_WIDTH + KV_WIDTH:ATTN_WIDTH + 2 * KV_WIDTH]
    ssm = (z * _sigmoid(glu)).astype(BF)
    ssm_proj = _dot(ssm, wbs_ref[...])

    w = WINDOW
    n_sub = MIX_TB // w
    lane = lax.broadcasted_iota(jnp.int32, (2 * w, LANES), 1)
    lo = lane < HEAD_DIM
    kj = lax.broadcasted_iota(jnp.int32, (2 * w, w), 0)
    qi = lax.broadcasted_iota(jnp.int32, (2 * w, w), 1)
    band = (kj > qi) & (kj <= qi + w)
    first = band & ((kj >= w) | (t > 0))
    bias_band = jnp.where(band, 0.0, NEG_BIG).astype(F32)
    bias_first = jnp.where(first, 0.0, NEG_BIG).astype(F32)
    col = lax.broadcasted_iota(jnp.int32, (1, 2 * w), 1)
    ones_row = (lax.broadcasted_iota(jnp.int32, (HEAD_DIM, 2 * w), 0) == 0).astype(F32)

    def sink_row(ha, hb):
        return jnp.where(col < w, sinks_ref[ha], sinks_ref[hb]) * LOG2_E

    gate_w = GATE_WIDTH // (n_sub * N_KV_HEADS)
    operands = {}

    def sub_block_operands(sb):
        if sb not in operands:
            cur = slice(sb * w, (sb + 1) * w)
            if sb == 0:
                kcat = jnp.concatenate([kprev[...], k[cur]], axis=0)
                vcat = jnp.concatenate([vprev[...], v[cur]], axis=0)
                bias = bias_first
            else:
                kcat = k[(sb - 1) * w:(sb + 1) * w]
                vcat = v[(sb - 1) * w:(sb + 1) * w]
                bias = bias_band
            k_split = _half_split(kcat, pltpu.roll(kcat, HEAD_DIM, 1), lo)
            vt = vcat.T
            v_rows = [(jnp.concatenate([vt[g * HEAD_DIM:(g + 1) * HEAD_DIM], ones_row], axis=0).astype(BF),
                       jnp.concatenate([ones_row, vt[g * HEAD_DIM:(g + 1) * HEAD_DIM]], axis=0).astype(BF))
                      for g in range(N_KV_HEADS)]
            operands[sb] = (k_split, v_rows, jnp.concatenate([bias, bias], axis=1))
        return operands[sb]

    def scores(sb, grp):
        k_split, _, _ = sub_block_operands(sb)
        cur = slice(sb * w, (sb + 1) * w)
        qq = jnp.concatenate([q[2 * grp][cur], q[2 * grp + 1][cur]], axis=0)
        return _dot_nt(k_split[2 * grp], qq), _dot_nt(k_split[2 * grp + 1], qq)

    def weighted_values(sb, grp, s_e, s_o):
        _, v_rows, bias2 = sub_block_operands(sb)
        p_e, sink_e = _softmax_terms_t(s_e, bias2, sink_row(4 * grp, 4 * grp + 2))
        p_o, sink_o = _softmax_terms_t(s_o, bias2, sink_row(4 * grp + 1, 4 * grp + 3))
        v_e, v_o = v_rows[grp]
        d_e, d_o = _dot(v_e, p_e), _dot(v_o, p_o)
        o2 = jnp.concatenate(
            [d_e[:HEAD_DIM] * (1.0 / (d_e[HEAD_DIM:HEAD_DIM + 1] + sink_e)),
             d_o[HEAD_DIM:] * (1.0 / (d_o[0:1] + sink_o))], axis=0)
        return [o2[:, :w].T, o2[:, w:].T]

    chains = [(sb, grp) for sb in range(n_sub) for grp in range(N_KV_HEADS)]
    gate_cols, outs = [], []
    s_next = scores(*chains[0])
    for i, (sb, grp) in enumerate(chains):
        s_cur = s_next
        gate_cols.append(_gate_cols(hn, win_ref, bg_ref, slice(i * gate_w, (i + 1) * gate_w)))
        if i + 1 < len(chains):
            s_next = scores(*chains[i + 1])
        outs += weighted_values(sb, grp, *s_cur)
    per_sb = 2 * N_KV_HEADS
    attn = jnp.concatenate(
        [jnp.concatenate(outs[sb * per_sb:(sb + 1) * per_sb], axis=1) for sb in range(n_sub)],
        axis=0).astype(BF)
    gates = jnp.concatenate(gate_cols, axis=1)

    kprev[...] = k[MIX_TB - w:]
    vprev[...] = v[MIX_TB - w:]
    kwin_ref[...] = k[MIX_TB - w:]
    vwin_ref[...] = v[MIX_TB - w:]
    x1_ref[...] = _merge_out(x, gates, _dot(attn, wba_ref[...]), ssm_proj, wo_ref)


def _mix_prompt(x, y_ssm, rc, rs1, rs2, sinks, g1, w_in, bg, wba, wbs, wo, wglu, bglu):
    b, t, _ = x.shape
    nblk = t // MIX_TB
    x2 = x.reshape(b * t, D_MODEL)
    row_map = lambda i, j: (i * nblk + j, 0)
    x1, kwin, vwin = pl.pallas_call(
        _mix_prompt_kernel,
        grid=(b, nblk),
        in_specs=[
            pl.BlockSpec(memory_space=pltpu.SMEM),
            pl.BlockSpec((MIX_TB, D_MODEL), row_map),
            pl.BlockSpec((MIX_TB, SSM_WIDTH), row_map),
            pl.BlockSpec((MIX_TB, LANES), lambda i, j: (j, 0)),
            pl.BlockSpec((MIX_TB, LANES), lambda i, j: (j, 0)),
            pl.BlockSpec((MIX_TB, LANES), lambda i, j: (j, 0)),
            _const_spec((1, D_MODEL)),
            _const_spec(w_in.shape),
            _const_spec(bg.shape),
            _const_spec(wba.shape),
            _const_spec(wbs.shape),
            _const_spec(wo.shape),
            _const_spec(wglu.shape),
            _const_spec(bglu.shape),
        ],
        out_specs=[
            pl.BlockSpec((MIX_TB, D_MODEL), row_map),
            pl.BlockSpec((None, WINDOW, KV_WIDTH), lambda i, j: (i, 0, 0)),
            pl.BlockSpec((None, WINDOW, KV_WIDTH), lambda i, j: (i, 0, 0)),
        ],
        out_shape=[
            jax.ShapeDtypeStruct((b * t, D_MODEL), F32),
            jax.ShapeDtypeStruct((b, WINDOW, KV_WIDTH), F32),
            jax.ShapeDtypeStruct((b, WINDOW, KV_WIDTH), F32),
        ],
        scratch_shapes=[pltpu.VMEM((WINDOW, KV_WIDTH), F32), pltpu.VMEM((WINDOW, KV_WIDTH), F32)],
        compiler_params=pltpu.CompilerParams(
            dimension_semantics=("arbitrary", "arbitrary"), vmem_limit_bytes=VMEM_LIMIT),
        name="mix_prompt",
    )(sinks, x2, y_ssm, rc, rs1, rs2, g1, w_in, bg, wba, wbs, wo, wglu, bglu)
    return x1, kwin, vwin


QROWS = 16
SAMPLE_TB = 32


def _sample_layer_kernel(x_ref, kbuf_ref, vbuf_ref, rc_ref, rs1_ref, rs2_ref, sinkc_ref, g1_ref, win_ref,
                         xall_ref, h0re_ref, h0im_ref, bg_ref, wbad_ref, wbs_ref, wo_ref, lb_ref, bblk_ref,
                         ctblk_ref, d_ref, wglu_ref, bglu_ref,
                         kout_ref, vout_ref, x1_ref, hre_ref, him_ref, qz, o3_all):
    nb = x_ref.shape[0]
    step = pl.program_id(0)
    hn = _rms(x_ref[...], g1_ref[...]).astype(BF)
    qkv = _dot(hn, win_ref[:, :QKV_WIDTH])
    rc, rs1, rs2 = rc_ref[...], rs1_ref[...], rs2_ref[...]
    scale = HEAD_DIM ** -0.5
    k_new = _rope(qkv[:, ATTN_WIDTH:ATTN_WIDTH + KV_WIDTH], rc, rs1, rs2)
    v_new = qkv[:, ATTN_WIDTH + KV_WIDTH:ATTN_WIDTH + 2 * KV_WIDTH]
    pad = jnp.zeros((LANES - nb, KV_WIDTH), F32)
    k_new_t = jnp.concatenate([k_new, pad], axis=0).T
    v_new_t = jnp.concatenate([v_new, pad], axis=0).T

    lane = lax.broadcasted_iota(jnp.int32, (nb, LANES), 1)
    lo = lane < HEAD_DIM
    qz[...] = jnp.zeros_like(qz)
    for c in range(ATTN_WIDTH // LANES):
        qc = _rope(qkv[:, c * LANES:(c + 1) * LANES], rc, rs1, rs2) * scale
        qr = pltpu.roll(qc, HEAD_DIM, 1)
        zero = jnp.zeros_like(qc)
        if c < 2:
            even, odd = jnp.where(lo, qc, zero), jnp.where(lo, qr, zero)
        else:
            even, odd = jnp.where(lo, zero, qr), jnp.where(lo, zero, qc)
        qz[pl.ds(2 * c, nb, stride=QROWS), :] = even
        qz[pl.ds(2 * c + 1, nb, stride=QROWS), :] = odd

    last = lax.broadcasted_iota(jnp.int32, (KV_WIDTH, WINDOW), 1) == WINDOW - 1
    for b in range(nb):
        kout_ref[b] = jnp.where(last, k_new_t[:, b:b + 1], pltpu.roll(kbuf_ref[b], WINDOW - 1, 1))
        vout_ref[b] = jnp.where(last, v_new_t[:, b:b + 1], pltpu.roll(vbuf_ref[b], WINDOW - 1, 1))

    sink = sinkc_ref[...]
    q3 = qz[...].reshape(nb, QROWS, LANES).astype(BF)
    s = jnp.einsum('bhd,bdk->bhk', q3, kout_ref[...].astype(BF), preferred_element_type=F32)
    m = jnp.maximum(jnp.max(s, axis=-1, keepdims=True), sink)
    p = jnp.exp(s - m)
    den = jnp.sum(p, axis=-1, keepdims=True) + jnp.exp(sink - m)
    p = (p * (1.0 / den)).astype(BF)
    o3 = jnp.einsum('bhk,bdk->bhd', p, vout_ref[...].astype(BF), preferred_element_type=F32)
    o3_all[pl.ds(pl.multiple_of(step * (nb * QROWS), nb * QROWS), nb * QROWS), :] = o3.reshape(nb * QROWS, LANES)

    @pl.when(step == pl.num_programs(0) - 1)
    def _():
        _sample_tail(xall_ref, o3_all, h0re_ref, h0im_ref, g1_ref, win_ref, bg_ref, wbad_ref, wbs_ref, wo_ref,
                     lb_ref, bblk_ref, ctblk_ref, d_ref, wglu_ref, bglu_ref, x1_ref, hre_ref, him_ref)


def _sample_layer(x, kbuf, vbuf, h0re, h0im, rc, rs1, rs2, sinkc, g1, w_in, bg, wbad, wbs, wo, lb, bblk, ctblk,
                  d, wglu, bglu):
    nb = x.shape[0]
    tb = SAMPLE_TB
    blocked = (x, kbuf, vbuf)
    consts = (rc, rs1, rs2, sinkc, g1, w_in, x, h0re, h0im, bg, wbad, wbs, wo, lb, bblk, ctblk, d, wglu, bglu)
    kv_spec = pl.BlockSpec((tb, KV_WIDTH, WINDOW), lambda i: (i, 0, 0))
    whole = lambda shape: pl.BlockSpec(shape, lambda i: (0,) * len(shape))
    return pl.pallas_call(
        _sample_layer_kernel,
        grid=(nb // tb,),
        in_specs=[pl.BlockSpec((tb, D_MODEL), lambda i: (i, 0)), kv_spec, kv_spec]
        + [_const_spec(a.shape) for a in consts],
        out_specs=[kv_spec, kv_spec, whole((nb, D_MODEL)), whole((N_STATES, nb)), whole((N_STATES, nb))],
        out_shape=[
            jax.ShapeDtypeStruct((nb, KV_WIDTH, WINDOW), F32),
            jax.ShapeDtypeStruct((nb, KV_WIDTH, WINDOW), F32),
            jax.ShapeDtypeStruct((nb, D_MODEL), F32),
            jax.ShapeDtypeStruct((N_STATES, nb), F32),
            jax.ShapeDtypeStruct((N_STATES, nb), F32),
        ],
        scratch_shapes=[pltpu.VMEM((tb * QROWS, LANES), F32), pltpu.VMEM((nb * QROWS, LANES), F32)],
        compiler_params=pltpu.CompilerParams(
            dimension_semantics=("arbitrary",), vmem_limit_bytes=VMEM_LIMIT),
        name="sample_layer",
    )(*blocked, *consts)


def _sample_tail(x_ref, o3_ref, h0re_ref, h0im_ref, g1_ref, win_ref, bg_ref, wbad_ref,
                 wbs_ref, wo_ref, lb_ref, bblk_ref, ctblk_ref, d_ref, wglu_ref, bglu_ref,
                 x1_ref, hre_ref, him_ref):
    nb = x_ref.shape[0]
    x = x_ref[...]
    hn = _rms(x, g1_ref[...]).astype(BF)

    lane = lax.broadcasted_iota(jnp.int32, (nb, LANES), 1)
    lo = lane < HEAD_DIM
    a = jnp.zeros((nb, D_MODEL), F32)
    zero = jnp.zeros((nb, LANES), F32)
    for h in range(N_Q_HEADS):
        oh = o3_ref[pl.ds(h, nb, stride=QROWS), :]
        oh = jnp.where(lo, oh, zero) if h < N_Q_HEADS // 2 else jnp.where(lo, zero, oh)
        a = a + _dot(oh.astype(BF), wbad_ref[h])

    u = _dot(hn, win_ref[:, U_COL0:GATE_COL0])
    ub = u.astype(BF)
    lre, lim = lb_ref[0:1, :], lb_ref[1:2, :]
    y_cols = []
    for o in range(N_OCT):
        sl = slice(o * OCT_STATES, (o + 1) * OCT_STATES)
        bu = _dot(ub[:, o * LANES:(o + 1) * LANES], bblk_ref[o])
        blocks = [slice(c * LANES, (c + 1) * LANES) for c in range(o * OCT_COL, (o + 1) * OCT_COL)]
        h0r = jnp.concatenate([h0re_ref[rows, :].T for rows in blocks], axis=1)
        h0i = jnp.concatenate([h0im_ref[rows, :].T for rows in blocks], axis=1)
        hr = bu[:, :OCT_STATES] + (lre[:, sl] * h0r - lim[:, sl] * h0i)
        hi = bu[:, OCT_STATES:] + (lre[:, sl] * h0i + lim[:, sl] * h0r)
        for cc, rows in enumerate(blocks):
            hre_ref[rows, :] = hr[:, cc * LANES:(cc + 1) * LANES].T
            him_ref[rows, :] = hi[:, cc * LANES:(cc + 1) * LANES].T
        y_cols.append(_dot_nt(jnp.concatenate([hr, hi], axis=1).astype(BF), ctblk_ref[o]))
    y = jnp.concatenate(y_cols, axis=1) + d_ref[...] * u
    z = _gelu_tanh(y)
    gate = _dot(z.astype(BF), wglu_ref[...]) + bglu_ref[...]
    ssm = (z * _sigmoid(gate)).astype(BF)

    gates = _gate_cols(hn, win_ref, bg_ref, slice(0, GATE_WIDTH))
    x1_ref[...] = _merge_out(x, gates, a, _dot(ssm, wbs_ref[...]), wo_ref)


FFN_W_CHUNKS = 16
FFN_W_SLOTS = 4


def _stream_cast_weights(jobs):
    tasks, used = [], {}
    for w_hbm, w_bf, stage, sem in jobs:
        rows = stage.shape[1]
        for k in range(w_hbm.shape[0] // rows):
            slot = used.get(id(stage), 0) % stage.shape[0]
            used[id(stage)] = used.get(id(stage), 0) + 1
            copy = pltpu.make_async_copy(w_hbm.at[pl.ds(k * rows, rows), :], stage.at[slot], sem.at[slot])
            tasks.append((copy, w_bf, stage, slot, k * rows, rows))
    ahead = min(stage.shape[0] for _, _, stage, _ in jobs) - 1
    for copy, *_ in tasks[:ahead]:
        copy.start()
    for i, (copy, w_bf, stage, slot, row0, rows) in enumerate(tasks):
        if i + ahead < len(tasks):
            tasks[i + ahead][0].start()
        copy.wait()
        w_bf[row0:row0 + rows, :] = stage[slot].astype(BF)


def _ffn_rows(x, g2_ref, wgate, wup, wdown, gf_ref):
    h = _rms(x, g2_ref[...]).astype(BF)
    gate = _dot(h, wgate[...])
    up = _dot(h, wup[...])
    half_gate = 0.5 * gate
    act = ((half_gate + half_gate * jnp.tanh(half_gate)) * up).astype(BF)
    x2 = x + _dot(act, wdown[...])
    return _rms(x2, gf_ref[...])


def _ffn_kernel(xp_ref, xs_ref, g2_ref, wgate_hbm, wup_hbm, wdown_hbm, gf_ref, yp_ref, ys_ref,
                wgate, wup, wdown, stage_in, stage_out, sem_in, sem_out):
    i = pl.program_id(0)
    n_prompt = pl.num_programs(0) - 1

    @pl.when(i == 0)
    def _():
        _stream_cast_weights([(wgate_hbm, wgate, stage_in, sem_in), (wup_hbm, wup, stage_in, sem_in),
                              (wdown_hbm, wdown, stage_out, sem_out)])

    @pl.when(i < n_prompt)
    def _():
        half = FFN_TB // FFN_SPLIT
        for rows in (slice(h * half, (h + 1) * half) for h in range(FFN_SPLIT)):
            yp_ref[rows, :] = _ffn_rows(xp_ref[rows, :], g2_ref, wgate, wup, wdown, gf_ref)

    @pl.when(i == n_prompt)
    def _():
        ys_ref[...] = _ffn_rows(xs_ref[...], g2_ref, wgate, wup, wdown, gf_ref)


def _ffn(xp, xs, g2, wgate, wup, wdown, gf):
    n, ns = xp.shape[0], xs.shape[0]
    n_prompt = n // FFN_TB
    prompt_map = lambda i: (jnp.minimum(i, n_prompt - 1), 0)
    return pl.pallas_call(
        _ffn_kernel,
        grid=(n_prompt + 1,),
        in_specs=[
            pl.BlockSpec((FFN_TB, D_MODEL), prompt_map),
            _const_spec((ns, D_MODEL)),
            _const_spec((1, D_MODEL)),
            pl.BlockSpec(memory_space=pl.ANY),
            pl.BlockSpec(memory_space=pl.ANY),
            pl.BlockSpec(memory_space=pl.ANY),
            _const_spec((1, D_MODEL)),
        ],
        out_specs=[
            pl.BlockSpec((FFN_TB, D_MODEL), prompt_map),
            pl.BlockSpec((ns, D_MODEL), lambda i: (0, 0)),
        ],
        out_shape=[
            jax.ShapeDtypeStruct((n, D_MODEL), F32),
            jax.ShapeDtypeStruct((ns, D_MODEL), F32),
        ],
        scratch_shapes=[
            pltpu.VMEM((D_MODEL, D_FF), BF), pltpu.VMEM((D_MODEL, D_FF), BF), pltpu.VMEM((D_FF, D_MODEL), BF),
            pltpu.VMEM((FFN_W_SLOTS, D_MODEL // FFN_W_CHUNKS, D_FF), F32),
            pltpu.VMEM((FFN_W_SLOTS, D_FF // FFN_W_CHUNKS, D_MODEL), F32),
            pltpu.SemaphoreType.DMA((FFN_W_SLOTS,)), pltpu.SemaphoreType.DMA((FFN_W_SLOTS,)),
        ],
        compiler_params=pltpu.CompilerParams(
            dimension_semantics=("arbitrary",), vmem_limit_bytes=VMEM_LIMIT),
        name="ffn",
    )(xp, xs, g2, wgate, wup, wdown, gf)


def _rope_tables(pos):
    pos = np.asarray(pos, np.float64)
    inv_freq = ROPE_THETA ** (-(np.arange(ROPE_HALF, dtype=np.float64) * 2.0 / ROPE_DIM))
    ang = pos[:, None] * inv_freq[None, :]
    cos, sin = np.cos(ang), np.sin(ang)
    pad = np.zeros((pos.shape[0], HEAD_DIM - ROPE_DIM))
    zero = np.zeros_like(sin)
    rc = np.concatenate([cos, cos, pad + 1.0], axis=1)
    rs1 = np.concatenate([zero, sin, pad], axis=1)
    rs2 = np.concatenate([-sin, zero, pad], axis=1)
    rep = LANES // HEAD_DIM
    return tuple(jnp.asarray(np.tile(a, (1, rep)), F32) for a in (rc, rs1, rs2))


def _cmul(ar, ai, br, bi):
    return ar * br - ai * bi, ar * bi + ai * br


def _ssm_tables(lam_re, lam_im, log_dt, b_re, b_im, c_re, c_im):
    dt = jnp.exp(log_dt)[:, None]
    mag = jnp.exp(lam_re * dt)
    lb_re = mag * jnp.cos(lam_im * dt)
    lb_im = mag * jnp.sin(lam_im * dt)
    den = lam_re * lam_re + lam_im * lam_im
    nr = lb_re - 1.0
    k_re = ((nr * lam_re + lb_im * lam_im) / den)[..., None]
    k_im = ((lb_im * lam_re - nr * lam_im) / den)[..., None]
    bb_re = k_re * b_re - k_im * b_im
    bb_im = k_re * b_im + k_im * b_re

    a_re, a_im = lb_re, lb_im
    for _ in range(int(math.log2(CHUNK))):
        a_re, a_im = _cmul(a_re, a_im, a_re, a_im)
    s_re, s_im = a_re, a_im
    for _ in range(int(math.log2(SEG))):
        s_re, s_im = _cmul(s_re, s_im, s_re, s_im)

    eye = jnp.eye(OCT, dtype=F32).reshape(1, OCT, 1, OCT, 1)

    def block_diag(a):
        r, c = a.shape[1:]
        return (a.reshape(N_OCT, OCT, r, 1, c) * eye).reshape(N_OCT, OCT * r, OCT * c)

    bblk = jnp.concatenate([block_diag(jnp.swapaxes(bb_re, 1, 2)),
                            block_diag(jnp.swapaxes(bb_im, 1, 2))], axis=2)
    ctblk = jnp.concatenate([block_diag(c_re), block_diag(-c_im)], axis=2)

    oct_cols = lambda a: a.reshape(N_OCT, 1, OCT_STATES)
    lcol = jnp.concatenate([oct_cols(lb_re), oct_cols(lb_im),
                            jnp.zeros((N_OCT, SUBLANES - 2, OCT_STATES), F32)], axis=1)

    flat = lambda a: a.reshape(1, N_STATES)
    col = lambda a: a.reshape(N_COL, 1, LANES)
    a_tab = jnp.concatenate([col(a_re), col(a_im), col(s_re), col(s_im),
                             jnp.zeros((N_COL, SUBLANES - 4, LANES), F32)], axis=1)
    lb = jnp.concatenate([flat(lb_re), flat(lb_im)], axis=0)
    return bblk, ctblk, lcol, a_tab, lb


def kernel(x_prompt, x_sample, state_k_win, state_v_win, state_ssm_re, state_ssm_im, norm1_g, w_in, b_gate, attn_sinks, ssm_lam_re, ssm_lam_im, ssm_log_dt, ssm_b_re, ssm_b_im, ssm_c_re, ssm_c_im, ssm_d, w_glu, b_glu, w_branch_attn, w_branch_ssm, w_out, norm2_g, w_ffn_gate, w_ffn_up, w_ffn_down, norm_f_g):
    depth = w_in.shape[0]
    assert depth == 1
    b, t, _ = x_prompt.shape
    nb, s_len, _ = x_sample.shape
    assert s_len == 1 and state_k_win.shape[2] == WINDOW
    l = 0
    assert w_in.shape[2] == IN_WIDTH
    w_in_b = w_in[l].astype(BF)
    g1 = norm1_g[l].reshape(1, D_MODEL)
    g2 = norm2_g[l].reshape(1, D_MODEL)
    gf = norm_f_g.reshape(1, D_MODEL)
    bg = b_gate[l].reshape(1, GATE_WIDTH)
    d = ssm_d[l].reshape(1, SSM_WIDTH)
    wglu = w_glu[l].astype(BF)
    bglu = b_glu[l].reshape(1, SSM_WIDTH)
    wba = w_branch_attn[l].astype(BF)
    wbs = w_branch_ssm[l].astype(BF)
    wo = w_out[l].astype(BF)
    sinks = attn_sinks[l]

    bblk, ctblk, lcol, a_tab, lb = _ssm_tables(
        ssm_lam_re[l], ssm_lam_im[l], ssm_log_dt[l], ssm_b_re[l], ssm_b_im[l], ssm_c_re[l], ssm_c_im[l])

    rc, rs1, rs2 = _rope_tables(np.arange(t))
    yssm_p, hre_p, him_p = _s5_prompt(x_prompt, g1, w_in_b, bblk, ctblk, lcol, a_tab, d)
    x1_p, kwin_p, vwin_p = _mix_prompt(x_prompt, yssm_p, rc, rs1, rs2, sinks, g1, w_in_b, bg, wba, wbs, wo,
                                       wglu, bglu)

    rcs, rs1s, rs2s = _rope_tables(PAST_LEN + np.arange(1))
    sinkc = jnp.concatenate([sinks, jnp.zeros((QROWS - N_Q_HEADS,), F32)]).reshape(QROWS, 1)
    wbad = jnp.concatenate([wba.reshape(N_Q_HEADS, HEAD_DIM, D_MODEL)] * 2, axis=1)
    xs = x_sample.reshape(nb, D_MODEL)
    key_minor = lambda a: jnp.swapaxes(a.reshape(nb, WINDOW, KV_WIDTH), 1, 2)
    kwin_s, vwin_s, x1_s, hre_s, him_s = _sample_layer(
        xs, key_minor(state_k_win[l]), key_minor(state_v_win[l]),
        state_ssm_re[l].reshape(nb, N_STATES).T, state_ssm_im[l].reshape(nb, N_STATES).T,
        rcs, rs1s, rs2s, sinkc, g1, w_in_b, bg, wbad, wbs, wo, lb, bblk.astype(BF), ctblk.astype(BF),
        d, wglu, bglu)
    y_p, y_s = _ffn(x1_p, x1_s, g2, w_ffn_gate[l], w_ffn_up[l], w_ffn_down[l], gf)
    y_p = y_p.reshape(b, t, D_MODEL)
    y_s = y_s.reshape(nb, 1, D_MODEL)

    kv_shape_p = (1, b, WINDOW, N_KV_HEADS, HEAD_DIM)
    st_shape_p = (1, b, N_SSM_GROUPS, SSM_STATE)
    kv_shape_s = (1, nb, WINDOW, N_KV_HEADS, HEAD_DIM)
    st_shape_s = (1, nb, N_SSM_GROUPS, SSM_STATE)
    return (y_p, y_s,
            kwin_p.reshape(kv_shape_p), vwin_p.reshape(kv_shape_p),
            hre_p.reshape(st_shape_p), him_p.reshape(st_shape_p),
            jnp.swapaxes(kwin_s, 1, 2).reshape(kv_shape_s), jnp.swapaxes(vwin_s, 1, 2).reshape(kv_shape_s),
            hre_s.T.reshape(st_shape_s), him_s.T.reshape(st_shape_s))
```

```python
import math

import jax
import jax.numpy as jnp
import numpy as np
from jax import lax
from jax.experimental import pallas as pl
from jax.experimental.pallas import tpu as pltpu

D_MODEL = 1024
N_Q_HEADS = 8
N_KV_HEADS = 2
HEAD_DIM = 64
ATTN_WIDTH = N_Q_HEADS * HEAD_DIM
KV_WIDTH = N_KV_HEADS * HEAD_DIM
WINDOW = 128
ROPE_DIM = HEAD_DIM // 4
ROPE_HALF = ROPE_DIM // 2
ROPE_THETA = 500000.0
SSM_WIDTH = D_MODEL // 2
SSM_GROUP = 16
N_SSM_GROUPS = SSM_WIDTH // SSM_GROUP
SSM_STATE = 64
N_STATES = N_SSM_GROUPS * SSM_STATE
GATE_WIDTH = 2 * D_MODEL
QKV_WIDTH = ATTN_WIDTH + 2 * KV_WIDTH
U_COL0 = QKV_WIDTH
GATE_COL0 = U_COL0 + SSM_WIDTH
IN_WIDTH = GATE_COL0 + GATE_WIDTH
D_FF = -(-8 * D_MODEL // (3 * 256)) * 256
NORM_EPS = 1e-5
PAST_LEN = 8192

LANES = 128
SUBLANES = 8
CHUNK = 8
OCT = LANES // SSM_GROUP
N_OCT = N_SSM_GROUPS // OCT
OCT_STATES = OCT * SSM_STATE
OCT_COL = OCT_STATES // LANES
N_COL = N_STATES // LANES
N_SEG = SUBLANES
SEG = 16
S5_ROWS = N_SEG * SEG
M_PAIR = 2
U_HALF = SSM_WIDTH // 2
SEG_PITCH = SEG * CHUNK + SUBLANES
MIX_TB = 512
FFN_TB = 1024
FFN_SPLIT = 4
NEG_BIG = -1e30
LOG2_E = math.log2(math.e)
VMEM_LIMIT = 56 * 1024 * 1024

BF = jnp.bfloat16
F32 = jnp.float32


def _dot(a, b):
    return jnp.dot(a, b, preferred_element_type=F32)


def _dot_nt(a, b):
    return lax.dot_general(a, b, (((1,), (1,)), ((), ())), preferred_element_type=F32)


def _dot_nt_split(a, b):
    a_hi, b_hi = a.astype(BF), b.astype(BF)
    a_lo = (a - a_hi.astype(F32)).astype(BF)
    b_lo = (b - b_hi.astype(F32)).astype(BF)
    return _dot_nt(a_hi, b_hi) + (_dot_nt(a_hi, b_lo) + _dot_nt(a_lo, b_hi))


def _rms(x, g):
    return x * lax.rsqrt(jnp.mean(x * x, axis=-1, keepdims=True) + NORM_EPS) * g


def _sigmoid(x):
    return 1.0 / (1.0 + jnp.exp(-x))


def _gelu_tanh(x):
    c = math.sqrt(2.0 / math.pi)
    return 0.5 * x * (1.0 + jnp.tanh(c * (x + 0.044715 * (x * x * x))))


def _rope(a, rc, rs1, rs2):
    return a * rc + pltpu.roll(a, ROPE_HALF, 1) * rs1 + pltpu.roll(a, LANES - ROPE_HALF, 1) * rs2


def _const_spec(shape):
    nd = len(shape)
    return pl.BlockSpec(shape, lambda *_: (0,) * nd, pipeline_mode=pl.Buffered(1))


def _build_chunk_operators(bblk_ref, ctblk_ref, lcol_ref, m_s, e_s, f_s):
    for o in range(N_OCT):
        ct = ctblk_ref[o]
        lr, li = lcol_ref[o, 0:1, :], lcol_ref[o, 1:2, :]
        er, ei = bblk_ref[o, :, :OCT_STATES], bblk_ref[o, :, OCT_STATES:]
        k_blk = []
        for tau in range(CHUNK):
            e_cat = jnp.concatenate([er, ei], axis=1)
            i = CHUNK - 1 - tau
            e_s[o, i * LANES:(i + 1) * LANES, :] = e_cat.astype(BF)
            k_blk.append(_dot_nt_split(e_cat, ct).astype(BF))
            er, ei = er * lr - ei * li, er * li + ei * lr
        zero = jnp.zeros((LANES, LANES), BF)
        for j in range(CHUNK):
            jt, jj = divmod(j, M_PAIR)
            for i in range(M_PAIR * (jt + 1)):
                m_s[jt][o, i * LANES:(i + 1) * LANES, jj * LANES:(jj + 1) * LANES] = (
                    k_blk[j - i] if j >= i else zero)
        tr, ti = ct[:, :OCT_STATES], -ct[:, OCT_STATES:]
        for j in range(CHUNK):
            tr, ti = tr * lr - ti * li, tr * li + ti * lr
            f_s[o, :OCT_STATES, j * LANES:(j + 1) * LANES] = tr.T.astype(BF)
            f_s[o, OCT_STATES:, j * LANES:(j + 1) * LANES] = (-ti).T.astype(BF)


def _s5_prompt_kernel(x_ref, g1_ref, wu0_ref, wu1_ref, bblk_ref, ctblk_ref, lcol_ref, a_ref, d_ref,
                      out_ref, hfin_ref,
                      m0_ref, m1_ref, m2_ref, m3_ref, e_ref, f_ref, us, ys, sre, sim, car):
    m_ref = (m0_ref, m1_ref, m2_ref, m3_ref)
    blk = pl.program_id(1)

    @pl.when((pl.program_id(0) == 0) & (blk == 0))
    def _():
        _build_chunk_operators(bblk_ref, ctblk_ref, lcol_ref, m_ref, e_ref, f_ref)

    @pl.when(blk == 0)
    def _():
        car[...] = jnp.zeros_like(car)

    seg_tokens = SEG * CHUNK
    seg_rows = [slice(s * SEG_PITCH, s * SEG_PITCH + seg_tokens) for s in range(N_SEG)]
    tok_rows = [slice(s * seg_tokens, (s + 1) * seg_tokens) for s in range(N_SEG)]

    for s in range(0, N_SEG, 2):
        hn = _rms(x_ref[s * seg_tokens:(s + 2) * seg_tokens, :], g1_ref[...]).astype(BF)
        u = jnp.concatenate([_dot(hn, wu0_ref[...]), _dot(hn, wu1_ref[...])], axis=1)
        for half in range(2):
            for cc in range(N_OCT):
                us[cc, seg_rows[s + half], :] = u[tok_rows[half], cc * LANES:(cc + 1) * LANES]

    sub = lax.broadcasted_iota(jnp.int32, (N_SEG, LANES), 0)

    def scan_column(c):
        tab = a_ref[c]
        are, aim = tab[0:1], tab[1:2]
        bre, bim = tab[2:3], tab[3:4]

        def step(cr, ci, r, keep_entering):
            slab = slice(r * N_SEG, (r + 1) * N_SEG)
            s_r, s_i = sre[c, slab, :], sim[c, slab, :]
            if keep_entering:
                sre[c, slab, :] = cr
                sim[c, slab, :] = ci
            return are * cr - aim * ci + s_r, are * ci + aim * cr + s_i

        cr = jnp.zeros((N_SEG, LANES), F32)
        ci = jnp.zeros((N_SEG, LANES), F32)
        for r in range(SEG):
            cr, ci = step(cr, ci, r, False)
        cv = car[c]
        pr, pi = cv[0:1], cv[1:2]
        sr = jnp.zeros((N_SEG, LANES), F32)
        si = jnp.zeros((N_SEG, LANES), F32)
        for s in range(N_SEG):
            sr = jnp.where(sub == s, pr, sr)
            si = jnp.where(sub == s, pi, si)
            pr, pi = (bre * pr - bim * pi + cr[s:s + 1], bre * pi + bim * pr + ci[s:s + 1])
        end = jnp.where(sub == 0, pr, jnp.where(sub == 1, pi, 0.0))
        car[c] = end
        hfin_ref[c] = end
        cr, ci = sr, si
        for r in range(SEG):
            cr, ci = step(cr, ci, r, True)

    for o in range(N_OCT):
        uo = jnp.concatenate(
            [jnp.concatenate([us[o, pl.ds(r * CHUNK + i, N_SEG, stride=SEG_PITCH), :] for r in range(SEG)],
                             axis=0).astype(BF) for i in range(CHUNK)], axis=1)
        s_end = _dot(uo, e_ref[o])
        cols = range(o * OCT_COL, (o + 1) * OCT_COL)
        for cc, c in enumerate(cols):
            sre[c] = s_end[:, cc * LANES:(cc + 1) * LANES]
            sim[c] = s_end[:, OCT_STATES + cc * LANES:OCT_STATES + (cc + 1) * LANES]
        y_in = jnp.concatenate(
            [_dot(uo[:, :(jt + 1) * M_PAIR * LANES], m_ref[jt][o]) for jt in range(CHUNK // M_PAIR)],
            axis=1)
        for c in cols:
            scan_column(c)
        hp = jnp.concatenate([sre[c] for c in cols] + [sim[c] for c in cols], axis=1).astype(BF)
        yo = y_in + _dot(hp, f_ref[o])
        for r in range(SEG):
            for j in range(CHUNK):
                ys[o, pl.ds(r * CHUNK + j, N_SEG, stride=SEG_PITCH), :] = (
                    yo[r * N_SEG:(r + 1) * N_SEG, j * LANES:(j + 1) * LANES])
        d_o = d_ref[:, o * LANES:(o + 1) * LANES]
        for s in range(N_SEG):
            out_ref[tok_rows[s], o * LANES:(o + 1) * LANES] = (
                ys[o, seg_rows[s], :] + d_o * us[o, seg_rows[s], :])


def _s5_prompt(x, g1, w_in, bblk, ctblk, lcol, a_tab, d):
    b, t, _ = x.shape
    tb = S5_ROWS * CHUNK
    nblk = t // tb
    x2 = x.reshape(b * t, D_MODEL)
    row_map = lambda i, j: (i * nblk + j, 0)
    op_shape = (N_OCT, CHUNK * LANES, CHUNK * LANES)
    out, hfin = pl.pallas_call(
        _s5_prompt_kernel,
        grid=(b, nblk),
        in_specs=[
            pl.BlockSpec((tb, D_MODEL), row_map),
            _const_spec((1, D_MODEL)),
            *[pl.BlockSpec((D_MODEL, U_HALF), lambda i, j, c=U_COL0 // U_HALF + h: (0, c),
                           pipeline_mode=pl.Buffered(1)) for h in range(2)],
            _const_spec(bblk.shape),
            _const_spec(ctblk.shape),
            _const_spec(lcol.shape),
            _const_spec(a_tab.shape),
            _const_spec((1, SSM_WIDTH)),
        ],
        out_specs=[
            pl.BlockSpec((tb, SSM_WIDTH), row_map),
            pl.BlockSpec((None, N_COL, SUBLANES, LANES), lambda i, j: (i, 0, 0, 0)),
        ],
        out_shape=[
            jax.ShapeDtypeStruct((b * t, SSM_WIDTH), F32),
            jax.ShapeDtypeStruct((b, N_COL, SUBLANES, LANES), F32),
        ],
        scratch_shapes=[
            *[pltpu.VMEM((N_OCT, (jt + 1) * M_PAIR * LANES, M_PAIR * LANES), BF)
              for jt in range(CHUNK // M_PAIR)],
            pltpu.VMEM(op_shape, BF), pltpu.VMEM(op_shape, BF),
            pltpu.VMEM((N_OCT, N_SEG * SEG_PITCH, LANES), F32),
            pltpu.VMEM((N_OCT, N_SEG * SEG_PITCH, LANES), F32),
            pltpu.VMEM((N_COL, S5_ROWS, LANES), F32), pltpu.VMEM((N_COL, S5_ROWS, LANES), F32),
            pltpu.VMEM((N_COL, SUBLANES, LANES), F32),
        ],
        compiler_params=pltpu.CompilerParams(
            dimension_semantics=("arbitrary", "arbitrary"), vmem_limit_bytes=VMEM_LIMIT),
        name="s5_prompt",
    )(x2, g1, w_in, w_in, bblk, ctblk, lcol, a_tab, d)
    return out, hfin[:, :, 0, :], hfin[:, :, 1, :]


def _gate_cols(hn, win_ref, bg_ref, cols):
    w_cols = slice(GATE_COL0 + cols.start, GATE_COL0 + cols.stop)
    return _sigmoid(_dot(hn, win_ref[:, w_cols]) + bg_ref[:, cols])


def _merge_out(x, gates, attn_proj, ssm_proj, wo_ref):
    merged = gates[:, :D_MODEL] * attn_proj + gates[:, D_MODEL:] * ssm_proj
    return x + _dot(merged.astype(BF), wo_ref[...])


def _half_split(a, ar, lo):
    z = jnp.zeros_like(a)
    return (jnp.where(lo, a, z).astype(BF), jnp.where(lo, z, ar).astype(BF),
            jnp.where(lo, ar, z).astype(BF), jnp.where(lo, z, a).astype(BF))


def _softmax_terms_t(st, bias_t, sink):
    st = st + bias_t
    m = jnp.maximum(jnp.max(st, axis=0, keepdims=True), sink)
    return jnp.exp2(st - m).astype(BF), jnp.exp2(sink - m)


def _mix_prompt_kernel(sinks_ref, x_ref, y_ref, rc_ref, rs1_ref, rs2_ref, g1_ref, win_ref,
                       bg_ref, wba_ref, wbs_ref, wo_ref, wglu_ref, bglu_ref,
                       x1_ref, kwin_ref, vwin_ref, kprev, vprev):
    t = pl.program_id(1)

    @pl.when(t == 0)
    def _():
        kprev[...] = jnp.zeros_like(kprev)
        vprev[...] = jnp.zeros_like(vprev)

    x = x_ref[...]
    hn = _rms(x, g1_ref[...]).astype(BF)
    qkv = _dot(hn, win_ref[:, :QKV_WIDTH])
    z = _gelu_tanh(y_ref[...])
    glu = _dot(z.astype(BF), wglu_ref[...]) + bglu_ref[...]
    rc, rs1, rs2 = rc_ref[...], rs1_ref[...], rs2_ref[...]
    scale = HEAD_DIM ** -0.5 * LOG2_E
    q = [(_rope(qkv[:, c * LANES:(c + 1) * LANES], rc, rs1, rs2) * scale).astype(BF)
         for c in range(ATTN_WIDTH // LANES)]
    k = _rope(qkv[:, ATTN_WIDTH:ATTN_WIDTH + KV_WIDTH], rc, rs1, rs2)
    v = qkv[:, ATTN_WIDTH + KV_WIDTH:ATTN_WIDTH + 2 * KV_WIDTH]
    ssm = (z * _sigmoid(glu)).astype(BF)
    ssm_proj = _dot(ssm, wbs_ref[...])

    w = WINDOW
    n_sub = MIX_TB // w
    lane = lax.broadcasted_iota(jnp.int32, (2 * w, LANES), 1)
    lo = lane < HEAD_DIM
    kj = lax.broadcasted_iota(jnp.int32, (2 * w, w), 0)
    qi = lax.broadcasted_iota(jnp.int32, (2 * w, w), 1)
    band = (kj > qi) & (kj <= qi + w)
    first = band & ((kj >= w) | (t > 0))
    bias_band = jnp.where(band, 0.0, NEG_BIG).astype(F32)
    bias_first = jnp.where(first, 0.0, NEG_BIG).astype(F32)
    col = lax.broadcasted_iota(jnp.int32, (1, 2 * w), 1)
    ones_row = (lax.broadcasted_iota(jnp.int32, (HEAD_DIM, 2 * w), 0) == 0).astype(F32)

    def sink_row(ha, hb):
        return jnp.where(col < w, sinks_ref[ha], sinks_ref[hb]) * LOG2_E

    gate_w = GATE_WIDTH // (n_sub * N_KV_HEADS)
    operands = {}

    def sub_block_operands(sb):
        if sb not in operands:
            cur = slice(sb * w, (sb + 1) * w)
            if sb == 0:
                kcat = jnp.concatenate([kprev[...], k[cur]], axis=0)
                vcat = jnp.concatenate([vprev[...], v[cur]], axis=0)
                bias = bias_first
            else:
                kcat = k[(sb - 1) * w:(sb + 1) * w]
                vcat = v[(sb - 1) * w:(sb + 1) * w]
                bias = bias_band
            k_split = _half_split(kcat, pltpu.roll(kcat, HEAD_DIM, 1), lo)
            vt = vcat.T
            v_rows = [(jnp.concatenate([vt[g * HEAD_DIM:(g + 1) * HEAD_DIM], ones_row], axis=0).astype(BF),
                       jnp.concatenate([ones_row, vt[g * HEAD_DIM:(g + 1) * HEAD_DIM]], axis=0).astype(BF))
                      for g in range(N_KV_HEADS)]
            operands[sb] = (k_split, v_rows, jnp.concatenate([bias, bias], axis=1))
        return operands[sb]

    def scores(sb, grp):
        k_split, _, _ = sub_block_operands(sb)
        cur = slice(sb * w, (sb + 1) * w)
        qq = jnp.concatenate([q[2 * grp][cur], q[2 * grp + 1][cur]], axis=0)
        return _dot_nt(k_split[2 * grp], qq), _dot_nt(k_split[2 * grp + 1], qq)

    def weighted_values(sb, grp, s_e, s_o):
        _, v_rows, bias2 = sub_block_operands(sb)
        p_e, sink_e = _softmax_terms_t(s_e, bias2, sink_row(4 * grp, 4 * grp + 2))
        p_o, sink_o = _softmax_terms_t(s_o, bias2, sink_row(4 * grp + 1, 4 * grp + 3))
        v_e, v_o = v_rows[grp]
        d_e, d_o = _dot(v_e, p_e), _dot(v_o, p_o)
        o2 = jnp.concatenate(
            [d_e[:HEAD_DIM] * (1.0 / (d_e[HEAD_DIM:HEAD_DIM + 1] + sink_e)),
             d_o[HEAD_DIM:] * (1.0 / (d_o[0:1] + sink_o))], axis=0)
        return [o2[:, :w].T, o2[:, w:].T]

    chains = [(sb, grp) for sb in range(n_sub) for grp in range(N_KV_HEADS)]
    gate_cols, outs = [], []
    s_next = scores(*chains[0])
    for i, (sb, grp) in enumerate(chains):
        s_cur = s_next
        gate_cols.append(_gate_cols(hn, win_ref, bg_ref, slice(i * gate_w, (i + 1) * gate_w)))
        if i + 1 < len(chains):
            s_next = scores(*chains[i + 1])
        outs += weighted_values(sb, grp, *s_cur)
    per_sb = 2 * N_KV_HEADS
    attn = jnp.concatenate(
        [jnp.concatenate(outs[sb * per_sb:(sb + 1) * per_sb], axis=1) for sb in range(n_sub)],
        axis=0).astype(BF)
    gates = jnp.concatenate(gate_cols, axis=1)

    kprev[...] = k[MIX_TB - w:]
    vprev[...] = v[MIX_TB - w:]
    kwin_ref[...] = k[MIX_TB - w:]
    vwin_ref[...] = v[MIX_TB - w:]
    x1_ref[...] = _merge_out(x, gates, _dot(attn, wba_ref[...]), ssm_proj, wo_ref)


def _mix_prompt(x, y_ssm, rc, rs1, rs2, sinks, g1, w_in, bg, wba, wbs, wo, wglu, bglu):
    b, t, _ = x.shape
    nblk = t // MIX_TB
    x2 = x.reshape(b * t, D_MODEL)
    row_map = lambda i, j: (i * nblk + j, 0)
    x1, kwin, vwin = pl.pallas_call(
        _mix_prompt_kernel,
        grid=(b, nblk),
        in_specs=[
            pl.BlockSpec(memory_space=pltpu.SMEM),
            pl.BlockSpec((MIX_TB, D_MODEL), row_map),
            pl.BlockSpec((MIX_TB, SSM_WIDTH), row_map),
            pl.BlockSpec((MIX_TB, LANES), lambda i, j: (j, 0)),
            pl.BlockSpec((MIX_TB, LANES), lambda i, j: (j, 0)),
            pl.BlockSpec((MIX_TB, LANES), lambda i, j: (j, 0)),
            _const_spec((1, D_MODEL)),
            _const_spec(w_in.shape),
            _const_spec(bg.shape),
            _const_spec(wba.shape),
            _const_spec(wbs.shape),
            _const_spec(wo.shape),
            _const_spec(wglu.shape),
            _const_spec(bglu.shape),
        ],
        out_specs=[
            pl.BlockSpec((MIX_TB, D_MODEL), row_map),
            pl.BlockSpec((None, WINDOW, KV_WIDTH), lambda i, j: (i, 0, 0)),
            pl.BlockSpec((None, WINDOW, KV_WIDTH), lambda i, j: (i, 0, 0)),
        ],
        out_shape=[
            jax.ShapeDtypeStruct((b * t, D_MODEL), F32),
            jax.ShapeDtypeStruct((b, WINDOW, KV_WIDTH), F32),
            jax.ShapeDtypeStruct((b, WINDOW, KV_WIDTH), F32),
        ],
        scratch_shapes=[pltpu.VMEM((WINDOW, KV_WIDTH), F32), pltpu.VMEM((WINDOW, KV_WIDTH), F32)],
        compiler_params=pltpu.CompilerParams(
            dimension_semantics=("arbitrary", "arbitrary"), vmem_limit_bytes=VMEM_LIMIT),
        name="mix_prompt",
    )(sinks, x2, y_ssm, rc, rs1, rs2, g1, w_in, bg, wba, wbs, wo, wglu, bglu)
    return x1, kwin, vwin


QROWS = 16
SAMPLE_TB = 32


def _sample_layer_kernel(x_ref, kbuf_ref, vbuf_ref, rc_ref, rs1_ref, rs2_ref, sinkc_ref, g1_ref, win_ref,
                         xall_ref, h0re_ref, h0im_ref, bg_ref, wbad_ref, wbs_ref, wo_ref, lb_ref, bblk_ref,
                         ctblk_ref, d_ref, wglu_ref, bglu_ref,
                         kout_ref, vout_ref, x1_ref, hre_ref, him_ref, qz, o3_all):
    nb = x_ref.shape[0]
    step = pl.program_id(0)
    hn = _rms(x_ref[...], g1_ref[...]).astype(BF)
    qkv = _dot(hn, win_ref[:, :QKV_WIDTH])
    rc, rs1, rs2 = rc_ref[...], rs1_ref[...], rs2_ref[...]
    scale = HEAD_DIM ** -0.5
    k_new = _rope(qkv[:, ATTN_WIDTH:ATTN_WIDTH + KV_WIDTH], rc, rs1, rs2)
    v_new = qkv[:, ATTN_WIDTH + KV_WIDTH:ATTN_WIDTH + 2 * KV_WIDTH]
    pad = jnp.zeros((LANES - nb, KV_WIDTH), F32)
    k_new_t = jnp.concatenate([k_new, pad], axis=0).T
    v_new_t = jnp.concatenate([v_new, pad], axis=0).T

    lane = lax.broadcasted_iota(jnp.int32, (nb, LANES), 1)
    lo = lane < HEAD_DIM
    qz[...] = jnp.zeros_like(qz)
    for c in range(ATTN_WIDTH // LANES):
        qc = _rope(qkv[:, c * LANES:(c + 1) * LANES], rc, rs1, rs2) * scale
        qr = pltpu.roll(qc, HEAD_DIM, 1)
        zero = jnp.zeros_like(qc)
        if c < 2:
            even, odd = jnp.where(lo, qc, zero), jnp.where(lo, qr, zero)
        else:
            even, odd = jnp.where(lo, zero, qr), jnp.where(lo, zero, qc)
        qz[pl.ds(2 * c, nb, stride=QROWS), :] = even
        qz[pl.ds(2 * c + 1, nb, stride=QROWS), :] = odd

    last = lax.broadcasted_iota(jnp.int32, (KV_WIDTH, WINDOW), 1) == WINDOW - 1
    for b in range(nb):
        kout_ref[b] = jnp.where(last, k_new_t[:, b:b + 1], pltpu.roll(kbuf_ref[b], WINDOW - 1, 1))
        vout_ref[b] = jnp.where(last, v_new_t[:, b:b + 1], pltpu.roll(vbuf_ref[b], WINDOW - 1, 1))

    sink = sinkc_ref[...]
    q3 = qz[...].reshape(nb, QROWS, LANES).astype(BF)
    s = jnp.einsum('bhd,bdk->bhk', q3, kout_ref[...].astype(BF), preferred_element_type=F32)
    m = jnp.maximum(jnp.max(s, axis=-1, keepdims=True), sink)
    p = jnp.exp(s - m)
    den = jnp.sum(p, axis=-1, keepdims=True) + jnp.exp(sink - m)
    p = (p * (1.0 / den)).astype(BF)
    o3 = jnp.einsum('bhk,bdk->bhd', p, vout_ref[...].astype(BF), preferred_element_type=F32)
    o3_all[pl.ds(pl.multiple_of(step * (nb * QROWS), nb * QROWS), nb * QROWS), :] = o3.reshape(nb * QROWS, LANES)

    @pl.when(step == pl.num_programs(0) - 1)
    def _():
        _sample_tail(xall_ref, o3_all, h0re_ref, h0im_ref, g1_ref, win_ref, bg_ref, wbad_ref, wbs_ref, wo_ref,
                     lb_ref, bblk_ref, ctblk_ref, d_ref, wglu_ref, bglu_ref, x1_ref, hre_ref, him_ref)


def _sample_layer(x, kbuf, vbuf, h0re, h0im, rc, rs1, rs2, sinkc, g1, w_in, bg, wbad, wbs, wo, lb, bblk, ctblk,
                  d, wglu, bglu):
    nb = x.shape[0]
    tb = SAMPLE_TB
    blocked = (x, kbuf, vbuf)
    consts = (rc, rs1, rs2, sinkc, g1, w_in, x, h0re, h0im, bg, wbad, wbs, wo, lb, bblk, ctblk, d, wglu, bglu)
    kv_spec = pl.BlockSpec((tb, KV_WIDTH, WINDOW), lambda i: (i, 0, 0))
    whole = lambda shape: pl.BlockSpec(shape, lambda i: (0,) * len(shape))
    return pl.pallas_call(
        _sample_layer_kernel,
        grid=(nb // tb,),
        in_specs=[pl.BlockSpec((tb, D_MODEL), lambda i: (i, 0)), kv_spec, kv_spec]
        + [_const_spec(a.shape) for a in consts],
        out_specs=[kv_spec, kv_spec, whole((nb, D_MODEL)), whole((N_STATES, nb)), whole((N_STATES, nb))],
        out_shape=[
            jax.ShapeDtypeStruct((nb, KV_WIDTH, WINDOW), F32),
            jax.ShapeDtypeStruct((nb, KV_WIDTH, WINDOW), F32),
            jax.ShapeDtypeStruct((nb, D_MODEL), F32),
            jax.ShapeDtypeStruct((N_STATES, nb), F32),
            jax.ShapeDtypeStruct((N_STATES, nb), F32),
        ],
        scratch_shapes=[pltpu.VMEM((tb * QROWS, LANES), F32), pltpu.VMEM((nb * QROWS, LANES), F32)],
        compiler_params=pltpu.CompilerParams(
            dimension_semantics=("arbitrary",), vmem_limit_bytes=VMEM_LIMIT),
        name="sample_layer",
    )(*blocked, *consts)


def _sample_tail(x_ref, o3_ref, h0re_ref, h0im_ref, g1_ref, win_ref, bg_ref, wbad_ref,
                 wbs_ref, wo_ref, lb_ref, bblk_ref, ctblk_ref, d_ref, wglu_ref, bglu_ref,
                 x1_ref, hre_ref, him_ref):
    nb = x_ref.shape[0]
    x = x_ref[...]
    hn = _rms(x, g1_ref[...]).astype(BF)

    lane = lax.broadcasted_iota(jnp.int32, (nb, LANES), 1)
    lo = lane < HEAD_DIM
    a = jnp.zeros((nb, D_MODEL), F32)
    zero = jnp.zeros((nb, LANES), F32)
    for h in range(N_Q_HEADS):
        oh = o3_ref[pl.ds(h, nb, stride=QROWS), :]
        oh = jnp.where(lo, oh, zero) if h < N_Q_HEADS // 2 else jnp.where(lo, zero, oh)
        a = a + _dot(oh.astype(BF), wbad_ref[h])

    u = _dot(hn, win_ref[:, U_COL0:GATE_COL0])
    ub = u.astype(BF)
    lre, lim = lb_ref[0:1, :], lb_ref[1:2, :]
    y_cols = []
    for o in range(N_OCT):
        sl = slice(o * OCT_STATES, (o + 1) * OCT_STATES)
        bu = _dot(ub[:, o * LANES:(o + 1) * LANES], bblk_ref[o])
        blocks = [slice(c * LANES, (c + 1) * LANES) for c in range(o * OCT_COL, (o + 1) * OCT_COL)]
        h0r = jnp.concatenate([h0re_ref[rows, :].T for rows in blocks], axis=1)
        h0i = jnp.concatenate([h0im_ref[rows, :].T for rows in blocks], axis=1)
        hr = bu[:, :OCT_STATES] + (lre[:, sl] * h0r - lim[:, sl] * h0i)
        hi = bu[:, OCT_STATES:] + (lre[:, sl] * h0i + lim[:, sl] * h0r)
        for cc, rows in enumerate(blocks):
            hre_ref[rows, :] = hr[:, cc * LANES:(cc + 1) * LANES].T
            him_ref[rows, :] = hi[:, cc * LANES:(cc + 1) * LANES].T
        y_cols.append(_dot_nt(jnp.concatenate([hr, hi], axis=1).astype(BF), ctblk_ref[o]))
    y = jnp.concatenate(y_cols, axis=1) + d_ref[...] * u
    z = _gelu_tanh(y)
    gate = _dot(z.astype(BF), wglu_ref[...]) + bglu_ref[...]
    ssm = (z * _sigmoid(gate)).astype(BF)

    gates = _gate_cols(hn, win_ref, bg_ref, slice(0, GATE_WIDTH))
    x1_ref[...] = _merge_out(x, gates, a, _dot(ssm, wbs_ref[...]), wo_ref)


FFN_W_CHUNKS = 16
FFN_W_SLOTS = 4


def _stream_cast_weights(jobs):
    tasks, used = [], {}
    for w_hbm, w_bf, stage, sem in jobs:
        rows = stage.shape[1]
        for k in range(w_hbm.shape[0] // rows):
            slot = used.get(id(stage), 0) % stage.shape[0]
            used[id(stage)] = used.get(id(stage), 0) + 1
            copy = pltpu.make_async_copy(w_hbm.at[pl.ds(k * rows, rows), :], stage.at[slot], sem.at[slot])
            tasks.append((copy, w_bf, stage, slot, k * rows, rows))
    ahead = min(stage.shape[0] for _, _, stage, _ in jobs) - 1
    for copy, *_ in tasks[:ahead]:
        copy.start()
    for i, (copy, w_bf, stage, slot, row0, rows) in enumerate(tasks):
        if i + ahead < len(tasks):
            tasks[i + ahead][0].start()
        copy.wait()
        w_bf[row0:row0 + rows, :] = stage[slot].astype(BF)


def _ffn_rows(x, g2_ref, wgate, wup, wdown, gf_ref):
    h = _rms(x, g2_ref[...]).astype(BF)
    gate = _dot(h, wgate[...])
    up = _dot(h, wup[...])
    half_gate = 0.5 * gate
    act = ((half_gate + half_gate * jnp.tanh(half_gate)) * up).astype(BF)
    x2 = x + _dot(act, wdown[...])
    return _rms(x2, gf_ref[...])


def _ffn_kernel(xp_ref, xs_ref, g2_ref, wgate_hbm, wup_hbm, wdown_hbm, gf_ref, yp_ref, ys_ref,
                wgate, wup, wdown, stage_in, stage_out, sem_in, sem_out):
    i = pl.program_id(0)
    n_prompt = pl.num_programs(0) - 1

    @pl.when(i == 0)
    def _():
        _stream_cast_weights([(wgate_hbm, wgate, stage_in, sem_in), (wup_hbm, wup, stage_in, sem_in),
                              (wdown_hbm, wdown, stage_out, sem_out)])

    @pl.when(i < n_prompt)
    def _():
        half = FFN_TB // FFN_SPLIT
        for rows in (slice(h * half, (h + 1) * half) for h in range(FFN_SPLIT)):
            yp_ref[rows, :] = _ffn_rows(xp_ref[rows, :], g2_ref, wgate, wup, wdown, gf_ref)

    @pl.when(i == n_prompt)
    def _():
        ys_ref[...] = _ffn_rows(xs_ref[...], g2_ref, wgate, wup, wdown, gf_ref)


def _ffn(xp, xs, g2, wgate, wup, wdown, gf):
    n, ns = xp.shape[0], xs.shape[0]
    n_prompt = n // FFN_TB
    prompt_map = lambda i: (jnp.minimum(i, n_prompt - 1), 0)
    return pl.pallas_call(
        _ffn_kernel,
        grid=(n_prompt + 1,),
        in_specs=[
            pl.BlockSpec((FFN_TB, D_MODEL), prompt_map),
            _const_spec((ns, D_MODEL)),
            _const_spec((1, D_MODEL)),
            pl.BlockSpec(memory_space=pl.ANY),
            pl.BlockSpec(memory_space=pl.ANY),
            pl.BlockSpec(memory_space=pl.ANY),
            _const_spec((1, D_MODEL)),
        ],
        out_specs=[
            pl.BlockSpec((FFN_TB, D_MODEL), prompt_map),
            pl.BlockSpec((ns, D_MODEL), lambda i: (0, 0)),
        ],
        out_shape=[
            jax.ShapeDtypeStruct((n, D_MODEL), F32),
            jax.ShapeDtypeStruct((ns, D_MODEL), F32),
        ],
        scratch_shapes=[
            pltpu.VMEM((D_MODEL, D_FF), BF), pltpu.VMEM((D_MODEL, D_FF), BF), pltpu.VMEM((D_FF, D_MODEL), BF),
            pltpu.VMEM((FFN_W_SLOTS, D_MODEL // FFN_W_CHUNKS, D_FF), F32),
            pltpu.VMEM((FFN_W_SLOTS, D_FF // FFN_W_CHUNKS, D_MODEL), F32),
            pltpu.SemaphoreType.DMA((FFN_W_SLOTS,)), pltpu.SemaphoreType.DMA((FFN_W_SLOTS,)),
        ],
        compiler_params=pltpu.CompilerParams(
            dimension_semantics=("arbitrary",), vmem_limit_bytes=VMEM_LIMIT),
        name="ffn",
    )(xp, xs, g2, wgate, wup, wdown, gf)


def _rope_tables(pos):
    pos = np.asarray(pos, np.float64)
    inv_freq = ROPE_THETA ** (-(np.arange(ROPE_HALF, dtype=np.float64) * 2.0 / ROPE_DIM))
    ang = pos[:, None] * inv_freq[None, :]
    cos, sin = np.cos(ang), np.sin(ang)
    pad = np.zeros((pos.shape[0], HEAD_DIM - ROPE_DIM))
    zero = np.zeros_like(sin)
    rc = np.concatenate([cos, cos, pad + 1.0], axis=1)
    rs1 = np.concatenate([zero, sin, pad], axis=1)
    rs2 = np.concatenate([-sin, zero, pad], axis=1)
    rep = LANES // HEAD_DIM
    return tuple(jnp.asarray(np.tile(a, (1, rep)), F32) for a in (rc, rs1, rs2))


def _cmul(ar, ai, br, bi):
    return ar * br - ai * bi, ar * bi + ai * br


def _ssm_tables(lam_re, lam_im, log_dt, b_re, b_im, c_re, c_im):
    dt = jnp.exp(log_dt)[:, None]
    mag = jnp.exp(lam_re * dt)
    lb_re = mag * jnp.cos(lam_im * dt)
    lb_im = mag * jnp.sin(lam_im * dt)
    den = lam_re * lam_re + lam_im * lam_im
    nr = lb_re - 1.0
    k_re = ((nr * lam_re + lb_im * lam_im) / den)[..., None]
    k_im = ((lb_im * lam_re - nr * lam_im) / den)[..., None]
    bb_re = k_re * b_re - k_im * b_im
    bb_im = k_re * b_im + k_im * b_re

    a_re, a_im = lb_re, lb_im
    for _ in range(int(math.log2(CHUNK))):
        a_re, a_im = _cmul(a_re, a_im, a_re, a_im)
    s_re, s_im = a_re, a_im
    for _ in range(int(math.log2(SEG))):
        s_re, s_im = _cmul(s_re, s_im, s_re, s_im)

    eye = jnp.eye(OCT, dtype=F32).reshape(1, OCT, 1, OCT, 1)

    def block_diag(a):
        r, c = a.shape[1:]
        return (a.reshape(N_OCT, OCT, r, 1, c) * eye).reshape(N_OCT, OCT * r, OCT * c)

    bblk = jnp.concatenate([block_diag(jnp.swapaxes(bb_re, 1, 2)),
                            block_diag(jnp.swapaxes(bb_im, 1, 2))], axis=2)
    ctblk = jnp.concatenate([block_diag(c_re), block_diag(-c_im)], axis=2)

    oct_cols = lambda a: a.reshape(N_OCT, 1, OCT_STATES)
    lcol = jnp.concatenate([oct_cols(lb_re), oct_cols(lb_im),
                            jnp.zeros((N_OCT, SUBLANES - 2, OCT_STATES), F32)], axis=1)

    flat = lambda a: a.reshape(1, N_STATES)
    col = lambda a: a.reshape(N_COL, 1, LANES)
    a_tab = jnp.concatenate([col(a_re), col(a_im), col(s_re), col(s_im),
                             jnp.zeros((N_COL, SUBLANES - 4, LANES), F32)], axis=1)
    lb = jnp.concatenate([flat(lb_re), flat(lb_im)], axis=0)
    return bblk, ctblk, lcol, a_tab, lb


def kernel(x_prompt, x_sample, state_k_win, state_v_win, state_ssm_re, state_ssm_im, norm1_g, w_in, b_gate, attn_sinks, ssm_lam_re, ssm_lam_im, ssm_log_dt, ssm_b_re, ssm_b_im, ssm_c_re, ssm_c_im, ssm_d, w_glu, b_glu, w_branch_attn, w_branch_ssm, w_out, norm2_g, w_ffn_gate, w_ffn_up, w_ffn_down, norm_f_g):
    depth = w_in.shape[0]
    assert depth == 1
    b, t, _ = x_prompt.shape
    nb, s_len, _ = x_sample.shape
    assert s_len == 1 and state_k_win.shape[2] == WINDOW
    l = 0
    assert w_in.shape[2] == IN_WIDTH
    w_in_b = w_in[l].astype(BF)
    g1 = norm1_g[l].reshape(1, D_MODEL)
    g2 = norm2_g[l].reshape(1, D_MODEL)
    gf = norm_f_g.reshape(1, D_MODEL)
    bg = b_gate[l].reshape(1, GATE_WIDTH)
    d = ssm_d[l].reshape(1, SSM_WIDTH)
    wglu = w_glu[l].astype(BF)
    bglu = b_glu[l].reshape(1, SSM_WIDTH)
    wba = w_branch_attn[l].astype(BF)
    wbs = w_branch_ssm[l].astype(BF)
    wo = w_out[l].astype(BF)
    sinks = attn_sinks[l]

    bblk, ctblk, lcol, a_tab, lb = _ssm_tables(
        ssm_lam_re[l], ssm_lam_im[l], ssm_log_dt[l], ssm_b_re[l], ssm_b_im[l], ssm_c_re[l], ssm_c_im[l])

    rc, rs1, rs2 = _rope_tables(np.arange(t))
    yssm_p, hre_p, him_p = _s5_prompt(x_prompt, g1, w_in_b, bblk, ctblk, lcol, a_tab, d)
    x1_p, kwin_p, vwin_p = _mix_prompt(x_prompt, yssm_p, rc, rs1, rs2, sinks, g1, w_in_b, bg, wba, wbs, wo,
                                       wglu, bglu)

    rcs, rs1s, rs2s = _rope_tables(PAST_LEN + np.arange(1))
    sinkc = jnp.concatenate([sinks, jnp.zeros((QROWS - N_Q_HEADS,), F32)]).reshape(QROWS, 1)
    wbad = jnp.concatenate([wba.reshape(N_Q_HEADS, HEAD_DIM, D_MODEL)] * 2, axis=1)
    xs = x_sample.reshape(nb, D_MODEL)
    key_minor = lambda a: jnp.swapaxes(a.reshape(nb, WINDOW, KV_WIDTH), 1, 2)
    kwin_s, vwin_s, x1_s, hre_s, him_s = _sample_layer(
        xs, key_minor(state_k_win[l]), key_minor(state_v_win[l]),
        state_ssm_re[l].reshape(nb, N_STATES).T, state_ssm_im[l].reshape(nb, N_STATES).T,
        rcs, rs1s, rs2s, sinkc, g1, w_in_b, bg, wbad, wbs, wo, lb, bblk.astype(BF), ctblk.astype(BF),
        d, wglu, bglu)
    y_p, y_s = _ffn(x1_p, x1_s, g2, w_ffn_gate[l], w_ffn_up[l], w_ffn_down[l], gf)
    y_p = y_p.reshape(b, t, D_MODEL)
    y_s = y_s.reshape(nb, 1, D_MODEL)

    kv_shape_p = (1, b, WINDOW, N_KV_HEADS, HEAD_DIM)
    st_shape_p = (1, b, N_SSM_GROUPS, SSM_STATE)
    kv_shape_s = (1, nb, WINDOW, N_KV_HEADS, HEAD_DIM)
    st_shape_s = (1, nb, N_SSM_GROUPS, SSM_STATE)
    return (y_p, y_s,
            kwin_p.reshape(kv_shape_p), vwin_p.reshape(kv_shape_p),
            hre_p.reshape(st_shape_p), him_p.reshape(st_shape_p),
            jnp.swapaxes(kwin_s, 1, 2).reshape(kv_shape_s), jnp.swapaxes(vwin_s, 1, 2).reshape(kv_shape_s),
            hre_s.T.reshape(st_shape_s), him_s.T.reshape(st_shape_s))
```

```python
import math

import jax
import jax.numpy as jnp
import numpy as np
from jax import lax
from jax.experimental import pallas as pl
from jax.experimental.pallas import tpu as pltpu

D_MODEL = 1024
N_Q_HEADS = 8
N_KV_HEADS = 2
HEAD_DIM = 64
ATTN_WIDTH = N_Q_HEADS * HEAD_DIM
KV_WIDTH = N_KV_HEADS * HEAD_DIM
WINDOW = 128
ROPE_DIM = HEAD_DIM // 4
ROPE_HALF = ROPE_DIM // 2
ROPE_THETA = 500000.0
SSM_WIDTH = D_MODEL // 2
SSM_GROUP = 16
N_SSM_GROUPS = SSM_WIDTH // SSM_GROUP
SSM_STATE = 64
N_STATES = N_SSM_GROUPS * SSM_STATE
GATE_WIDTH = 2 * D_MODEL
QKV_WIDTH = ATTN_WIDTH + 2 * KV_WIDTH
U_COL0 = QKV_WIDTH
GATE_COL0 = U_COL0 + SSM_WIDTH
IN_WIDTH = GATE_COL0 + GATE_WIDTH
D_FF = -(-8 * D_MODEL // (3 * 256)) * 256
NORM_EPS = 1e-5
PAST_LEN = 8192

LANES = 128
SUBLANES = 8
CHUNK = 8
OCT = LANES // SSM_GROUP
N_OCT = N_SSM_GROUPS // OCT
OCT_STATES = OCT * SSM_STATE
OCT_COL = OCT_STATES // LANES
N_COL = N_STATES // LANES
N_SEG = SUBLANES
SEG = 16
S5_ROWS = N_SEG * SEG
M_PAIR = 2
U_HALF = SSM_WIDTH // 2
SEG_PITCH = SEG * CHUNK + SUBLANES
MIX_TB = 512
FFN_TB = 1024
FFN_SPLIT = 4
NEG_BIG = -1e30
LOG2_E = math.log2(math.e)
VMEM_LIMIT = 56 * 1024 * 1024

BF = jnp.bfloat16
F32 = jnp.float32


def _dot(a, b):
    return jnp.dot(a, b, preferred_element_type=F32)


def _dot_nt(a, b):
    return lax.dot_general(a, b, (((1,), (1,)), ((), ())), preferred_element_type=F32)


def _dot_nt_split(a, b):
    a_hi, b_hi = a.astype(BF), b.astype(BF)
    a_lo = (a - a_hi.astype(F32)).astype(BF)
    b_lo = (b - b_hi.astype(F32)).astype(BF)
    return _dot_nt(a_hi, b_hi) + (_dot_nt(a_hi, b_lo) + _dot_nt(a_lo, b_hi))


def _rms(x, g):
    return x * lax.rsqrt(jnp.mean(x * x, axis=-1, keepdims=True) + NORM_EPS) * g


def _sigmoid(x):
    return 1.0 / (1.0 + jnp.exp(-x))


def _gelu_tanh(x):
    c = math.sqrt(2.0 / math.pi)
    return 0.5 * x * (1.0 + jnp.tanh(c * (x + 0.044715 * (x * x * x))))


def _rope(a, rc, rs1, rs2):
    return a * rc + pltpu.roll(a, ROPE_HALF, 1) * rs1 + pltpu.roll(a, LANES - ROPE_HALF, 1) * rs2


def _const_spec(shape):
    nd = len(shape)
    return pl.BlockSpec(shape, lambda *_: (0,) * nd, pipeline_mode=pl.Buffered(1))


def _build_chunk_operators(bblk_ref, ctblk_ref, lcol_ref, m_s, e_s, f_s):
    for o in range(N_OCT):
        ct = ctblk_ref[o]
        lr, li = lcol_ref[o, 0:1, :], lcol_ref[o, 1:2, :]
        er, ei = bblk_ref[o, :, :OCT_STATES], bblk_ref[o, :, OCT_STATES:]
        k_blk = []
        for tau in range(CHUNK):
            e_cat = jnp.concatenate([er, ei], axis=1)
            i = CHUNK - 1 - tau
            e_s[o, i * LANES:(i + 1) * LANES, :] = e_cat.astype(BF)
            k_blk.append(_dot_nt_split(e_cat, ct).astype(BF))
            er, ei = er * lr - ei * li, er * li + ei * lr
        zero = jnp.zeros((LANES, LANES), BF)
        for j in range(CHUNK):
            jt, jj = divmod(j, M_PAIR)
            for i in range(M_PAIR * (jt + 1)):
                m_s[jt][o, i * LANES:(i + 1) * LANES, jj * LANES:(jj + 1) * LANES] = (
                    k_blk[j - i] if j >= i else zero)
        tr, ti = ct[:, :OCT_STATES], -ct[:, OCT_STATES:]
        for j in range(CHUNK):
            tr, ti = tr * lr - ti * li, tr * li + ti * lr
            f_s[o, :OCT_STATES, j * LANES:(j + 1) * LANES] = tr.T.astype(BF)
            f_s[o, OCT_STATES:, j * LANES:(j + 1) * LANES] = (-ti).T.astype(BF)


def _s5_prompt_kernel(x_ref, g1_ref, wu0_ref, wu1_ref, bblk_ref, ctblk_ref, lcol_ref, a_ref, d_ref,
                      out_ref, hfin_ref,
                      m0_ref, m1_ref, m2_ref, m3_ref, e_ref, f_ref, us, ys, sre, sim, car):
    m_ref = (m0_ref, m1_ref, m2_ref, m3_ref)
    blk = pl.program_id(1)

    @pl.when((pl.program_id(0) == 0) & (blk == 0))
    def _():
        _build_chunk_operators(bblk_ref, ctblk_ref, lcol_ref, m_ref, e_ref, f_ref)

    @pl.when(blk == 0)
    def _():
        car[...] = jnp.zeros_like(car)

    seg_tokens = SEG * CHUNK
    seg_rows = [slice(s * SEG_PITCH, s * SEG_PITCH + seg_tokens) for s in range(N_SEG)]
    tok_rows = [slice(s * seg_tokens, (s + 1) * seg_tokens) for s in range(N_SEG)]

    for s in range(0, N_SEG, 2):
        hn = _rms(x_ref[s * seg_tokens:(s + 2) * seg_tokens, :], g1_ref[...]).astype(BF)
        u = jnp.concatenate([_dot(hn, wu0_ref[...]), _dot(hn, wu1_ref[...])], axis=1)
        for half in range(2):
            for cc in range(N_OCT):
                us[cc, seg_rows[s + half], :] = u[tok_rows[half], cc * LANES:(cc + 1) * LANES]

    sub = lax.broadcasted_iota(jnp.int32, (N_SEG, LANES), 0)

    def scan_column(c):
        tab = a_ref[c]
        are, aim = tab[0:1], tab[1:2]
        bre, bim = tab[2:3], tab[3:4]

        def step(cr, ci, r, keep_entering):
            slab = slice(r * N_SEG, (r + 1) * N_SEG)
            s_r, s_i = sre[c, slab, :], sim[c, slab, :]
            if keep_entering:
                sre[c, slab, :] = cr
                sim[c, slab, :] = ci
            return are * cr - aim * ci + s_r, are * ci + aim * cr + s_i

        cr = jnp.zeros((N_SEG, LANES), F32)
        ci = jnp.zeros((N_SEG, LANES), F32)
        for r in range(SEG):
            cr, ci = step(cr, ci, r, False)
        cv = car[c]
        pr, pi = cv[0:1], cv[1:2]
        sr = jnp.zeros((N_SEG, LANES), F32)
        si = jnp.zeros((N_SEG, LANES), F32)
        for s in range(N_SEG):
            sr = jnp.where(sub == s, pr, sr)
            si = jnp.where(sub == s, pi, si)
            pr, pi = (bre * pr - bim * pi + cr[s:s + 1], bre * pi + bim * pr + ci[s:s + 1])
        end = jnp.where(sub == 0, pr, jnp.where(sub == 1, pi, 0.0))
        car[c] = end
        hfin_ref[c] = end
        cr, ci = sr, si
        for r in range(SEG):
            cr, ci = step(cr, ci, r, True)

    for o in range(N_OCT):
        uo = jnp.concatenate(
            [jnp.concatenate([us[o, pl.ds(r * CHUNK + i, N_SEG, stride=SEG_PITCH), :] for r in range(SEG)],
                             axis=0).astype(BF) for i in range(CHUNK)], axis=1)
        s_end = _dot(uo, e_ref[o])
        cols = range(o * OCT_COL, (o + 1) * OCT_COL)
        for cc, c in enumerate(cols):
            sre[c] = s_end[:, cc * LANES:(cc + 1) * LANES]
            sim[c] = s_end[:, OCT_STATES + cc * LANES:OCT_STATES + (cc + 1) * LANES]
        y_in = jnp.concatenate(
            [_dot(uo[:, :(jt + 1) * M_PAIR * LANES], m_ref[jt][o]) for jt in range(CHUNK // M_PAIR)],
            axis=1)
        for c in cols:
            scan_column(c)
        hp = jnp.concatenate([sre[c] for c in cols] + [sim[c] for c in cols], axis=1).astype(BF)
        yo = y_in + _dot(hp, f_ref[o])
        for r in range(SEG):
            for j in range(CHUNK):
                ys[o, pl.ds(r * CHUNK + j, N_SEG, stride=SEG_PITCH), :] = (
                    yo[r * N_SEG:(r + 1) * N_SEG, j * LANES:(j + 1) * LANES])
        d_o = d_ref[:, o * LANES:(o + 1) * LANES]
        for s in range(N_SEG):
            out_ref[tok_rows[s], o * LANES:(o + 1) * LANES] = (
                ys[o, seg_rows[s], :] + d_o * us[o, seg_rows[s], :])


def _s5_prompt(x, g1, w_in, bblk, ctblk, lcol, a_tab, d):
    b, t, _ = x.shape
    tb = S5_ROWS * CHUNK
    nblk = t // tb
    x2 = x.reshape(b * t, D_MODEL)
    row_map = lambda i, j: (i * nblk + j, 0)
    op_shape = (N_OCT, CHUNK * LANES, CHUNK * LANES)
    out, hfin = pl.pallas_call(
        _s5_prompt_kernel,
        grid=(b, nblk),
        in_specs=[
            pl.BlockSpec((tb, D_MODEL), row_map),
            _const_spec((1, D_MODEL)),
            *[pl.BlockSpec((D_MODEL, U_HALF), lambda i, j, c=U_COL0 // U_HALF + h: (0, c),
                           pipeline_mode=pl.Buffered(1)) for h in range(2)],
            _const_spec(bblk.shape),
            _const_spec(ctblk.shape),
            _const_spec(lcol.shape),
            _const_spec(a_tab.shape),
            _const_spec((1, SSM_WIDTH)),
        ],
        out_specs=[
            pl.BlockSpec((tb, SSM_WIDTH), row_map),
            pl.BlockSpec((None, N_COL, SUBLANES, LANES), lambda i, j: (i, 0, 0, 0)),
        ],
        out_shape=[
            jax.ShapeDtypeStruct((b * t, SSM_WIDTH), F32),
            jax.ShapeDtypeStruct((b, N_COL, SUBLANES, LANES), F32),
        ],
        scratch_shapes=[
            *[pltpu.VMEM((N_OCT, (jt + 1) * M_PAIR * LANES, M_PAIR * LANES), BF)
              for jt in range(CHUNK // M_PAIR)],
            pltpu.VMEM(op_shape, BF), pltpu.VMEM(op_shape, BF),
            pltpu.VMEM((N_OCT, N_SEG * SEG_PITCH, LANES), F32),
            pltpu.VMEM((N_OCT, N_SEG * SEG_PITCH, LANES), F32),
            pltpu.VMEM((N_COL, S5_ROWS, LANES), F32), pltpu.VMEM((N_COL, S5_ROWS, LANES), F32),
            pltpu.VMEM((N_COL, SUBLANES, LANES), F32),
        ],
        compiler_params=pltpu.CompilerParams(
            dimension_semantics=("arbitrary", "arbitrary"), vmem_limit_bytes=VMEM_LIMIT),
        name="s5_prompt",
    )(x2, g1, w_in, w_in, bblk, ctblk, lcol, a_tab, d)
    return out, hfin[:, :, 0, :], hfin[:, :, 1, :]


def _gate_cols(hn, win_ref, bg_ref, cols):
    w_cols = slice(GATE_COL0 + cols.start, GATE_COL0 + cols.stop)
    return _sigmoid(_dot(hn, win_ref[:, w_cols]) + bg_ref[:, cols])


def _merge_out(x, gates, attn_proj, ssm_proj, wo_ref):
    merged = gates[:, :D_MODEL] * attn_proj + gates[:, D_MODEL:] * ssm_proj
    return x + _dot(merged.astype(BF), wo_ref[...])


def _half_split(a, ar, lo):
    z = jnp.zeros_like(a)
    return (jnp.where(lo, a, z).astype(BF), jnp.where(lo, z, ar).astype(BF),
            jnp.where(lo, ar, z).astype(BF), jnp.where(lo, z, a).astype(BF))


def _softmax_terms_t(st, bias_t, sink):
    st = st + bias_t
    m = jnp.maximum(jnp.max(st, axis=0, keepdims=True), sink)
    return jnp.exp2(st - m).astype(BF), jnp.exp2(sink - m)


def _mix_prompt_kernel(sinks_ref, x_ref, y_ref, rc_ref, rs1_ref, rs2_ref, g1_ref, win_ref,
                       bg_ref, wba_ref, wbs_ref, wo_ref, wglu_ref, bglu_ref,
                       x1_ref, kwin_ref, vwin_ref, kprev, vprev):
    t = pl.program_id(1)

    @pl.when(t == 0)
    def _():
        kprev[...] = jnp.zeros_like(kprev)
        vprev[...] = jnp.zeros_like(vprev)

    x = x_ref[...]
    hn = _rms(x, g1_ref[...]).astype(BF)
    qkv = _dot(hn, win_ref[:, :QKV_WIDTH])
    z = _gelu_tanh(y_ref[...])
    glu = _dot(z.astype(BF), wglu_ref[...]) + bglu_ref[...]
    rc, rs1, rs2 = rc_ref[...], rs1_ref[...], rs2_ref[...]
    scale = HEAD_DIM ** -0.5 * LOG2_E
    q = [(_rope(qkv[:, c * LANES:(c + 1) * LANES], rc, rs1, rs2) * scale).astype(BF)
         for c in range(ATTN_WIDTH // LANES)]
    k = _rope(qkv[:, ATTN_WIDTH:ATTN_WIDTH + KV_WIDTH], rc, rs1, rs2)
    v = qkv[:, ATTN_WIDTH + KV_WIDTH:ATTN_WIDTH + 2 * KV_WIDTH]
    ssm = (z * _sigmoid(glu)).astype(BF)
    ssm_proj = _dot(ssm, wbs_ref[...])

    w = WINDOW
    n_sub = MIX_TB // w
    lane = lax.broadcasted_iota(jnp.int32, (2 * w, LANES), 1)
    lo = lane < HEAD_DIM
    kj = lax.broadcasted_iota(jnp.int32, (2 * w, w), 0)
    qi = lax.broadcasted_iota(jnp.int32, (2 * w, w), 1)
    band = (kj > qi) & (kj <= qi + w)
    first = band & ((kj >= w) | (t > 0))
    bias_band = jnp.where(band, 0.0, NEG_BIG).astype(F32)
    bias_first = jnp.where(first, 0.0, NEG_BIG).astype(F32)
    col = lax.broadcasted_iota(jnp.int32, (1, 2 * w), 1)
    sum_r = lax.broadcasted_iota(jnp.int32, (2 * SUBLANES, 4 * w), 0)
    sum_c = lax.broadcasted_iota(jnp.int32, (2 * SUBLANES, 4 * w), 1)
    sum_rows = (((sum_r == 0) & (sum_c < 2 * w)) | ((sum_r == 1) & (sum_c >= 2 * w))).astype(F32)

    def sink_row(ha, hb):
        return jnp.where(col < w, sinks_ref[ha], sinks_ref[hb]) * LOG2_E

    gate_w = GATE_WIDTH // (n_sub * N_KV_HEADS)
    operands = {}

    def sub_block_operands(sb):
        if sb not in operands:
            cur = slice(sb * w, (sb + 1) * w)
            if sb == 0:
                kcat = jnp.concatenate([kprev[...], k[cur]], axis=0)
                vcat = jnp.concatenate([vprev[...], v[cur]], axis=0)
                bias = bias_first
            else:
                kcat = k[(sb - 1) * w:(sb + 1) * w]
                vcat = v[(sb - 1) * w:(sb + 1) * w]
                bias = bias_band
            k_split = _half_split(kcat, pltpu.roll(kcat, HEAD_DIM, 1), lo)
            k_stack = [jnp.concatenate(k_split[2 * g:2 * g + 2], axis=0) for g in range(N_KV_HEADS)]
            vt = vcat.T
            zero = jnp.zeros((HEAD_DIM, 2 * w), F32)
            v_stack = []
            for g in range(N_KV_HEADS):
                vg = vt[g * HEAD_DIM:(g + 1) * HEAD_DIM]
                v_stack.append(jnp.concatenate(
                    [jnp.concatenate([vg, zero], axis=1), jnp.concatenate([zero, vg], axis=1), sum_rows],
                    axis=0).astype(BF))
            operands[sb] = (k_stack, v_stack, jnp.concatenate([bias, bias], axis=1))
        return operands[sb]

    def scores(sb, grp):
        k_stack, _, _ = sub_block_operands(sb)
        cur = slice(sb * w, (sb + 1) * w)
        qq = jnp.concatenate([q[2 * grp][cur], q[2 * grp + 1][cur]], axis=0)
        return _dot_nt(k_stack[grp], qq)

    def weighted_values(sb, grp, s):
        _, v_stack, bias2 = sub_block_operands(sb)
        p_e, sink_e = _softmax_terms_t(s[:2 * w], bias2, sink_row(4 * grp, 4 * grp + 2))
        p_o, sink_o = _softmax_terms_t(s[2 * w:], bias2, sink_row(4 * grp + 1, 4 * grp + 3))
        d = _dot(v_stack[grp], jnp.concatenate([p_e, p_o], axis=0))
        heads = 2 * HEAD_DIM
        o2 = jnp.concatenate(
            [d[:HEAD_DIM] * (1.0 / (d[heads:heads + 1] + sink_e)),
             d[HEAD_DIM:heads] * (1.0 / (d[heads + 1:heads + 2] + sink_o))], axis=0)
        return [o2[:, :w].T, o2[:, w:].T]

    chains = [(sb, grp) for sb in range(n_sub) for grp in range(N_KV_HEADS)]
    gate_cols, outs = [], []
    s_next = scores(*chains[0])
    for i, (sb, grp) in enumerate(chains):
        s_cur = s_next
        gate_cols.append(_gate_cols(hn, win_ref, bg_ref, slice(i * gate_w, (i + 1) * gate_w)))
        if i + 1 < len(chains):
            s_next = scores(*chains[i + 1])
        outs += weighted_values(sb, grp, s_cur)
    per_sb = 2 * N_KV_HEADS
    attn = jnp.concatenate(
        [jnp.concatenate(outs[sb * per_sb:(sb + 1) * per_sb], axis=1) for sb in range(n_sub)],
        axis=0).astype(BF)
    gates = jnp.concatenate(gate_cols, axis=1)

    kprev[...] = k[MIX_TB - w:]
    vprev[...] = v[MIX_TB - w:]
    kwin_ref[...] = k[MIX_TB - w:]
    vwin_ref[...] = v[MIX_TB - w:]
    x1_ref[...] = _merge_out(x, gates, _dot(attn, wba_ref[...]), ssm_proj, wo_ref)


def _mix_prompt(x, y_ssm, rc, rs1, rs2, sinks, g1, w_in, bg, wba, wbs, wo, wglu, bglu):
    b, t, _ = x.shape
    nblk = t // MIX_TB
    x2 = x.reshape(b * t, D_MODEL)
    row_map = lambda i, j: (i * nblk + j, 0)
    x1, kwin, vwin = pl.pallas_call(
        _mix_prompt_kernel,
        grid=(b, nblk),
        in_specs=[
            pl.BlockSpec(memory_space=pltpu.SMEM),
            pl.BlockSpec((MIX_TB, D_MODEL), row_map),
            pl.BlockSpec((MIX_TB, SSM_WIDTH), row_map),
            pl.BlockSpec((MIX_TB, LANES), lambda i, j: (j, 0)),
            pl.BlockSpec((MIX_TB, LANES), lambda i, j: (j, 0)),
            pl.BlockSpec((MIX_TB, LANES), lambda i, j: (j, 0)),
            _const_spec((1, D_MODEL)),
            _const_spec(w_in.shape),
            _const_spec(bg.shape),
            _const_spec(wba.shape),
            _const_spec(wbs.shape),
            _const_spec(wo.shape),
            _const_spec(wglu.shape),
            _const_spec(bglu.shape),
        ],
        out_specs=[
            pl.BlockSpec((MIX_TB, D_MODEL), row_map),
            pl.BlockSpec((None, WINDOW, KV_WIDTH), lambda i, j: (i, 0, 0)),
            pl.BlockSpec((None, WINDOW, KV_WIDTH), lambda i, j: (i, 0, 0)),
        ],
        out_shape=[
            jax.ShapeDtypeStruct((b * t, D_MODEL), F32),
            jax.ShapeDtypeStruct((b, WINDOW, KV_WIDTH), F32),
            jax.ShapeDtypeStruct((b, WINDOW, KV_WIDTH), F32),
        ],
        scratch_shapes=[pltpu.VMEM((WINDOW, KV_WIDTH), F32), pltpu.VMEM((WINDOW, KV_WIDTH), F32)],
        compiler_params=pltpu.CompilerParams(
            dimension_semantics=("arbitrary", "arbitrary"), vmem_limit_bytes=VMEM_LIMIT),
        name="mix_prompt",
    )(sinks, x2, y_ssm, rc, rs1, rs2, g1, w_in, bg, wba, wbs, wo, wglu, bglu)
    return x1, kwin, vwin


QROWS = 16
SAMPLE_TB = 32


def _sample_layer_kernel(x_ref, kbuf_ref, vbuf_ref, rc_ref, rs1_ref, rs2_ref, sinkc_ref, g1_ref, win_ref,
                         xall_ref, h0re_ref, h0im_ref, bg_ref, wbad_ref, wbs_ref, wo_ref, lb_ref, bblk_ref,
                         ctblk_ref, d_ref, wglu_ref, bglu_ref,
                         kout_ref, vout_ref, x1_ref, hre_ref, him_ref, qz, o3_all):
    nb = x_ref.shape[0]
    step = pl.program_id(0)
    hn = _rms(x_ref[...], g1_ref[...]).astype(BF)
    qkv = _dot(hn, win_ref[:, :QKV_WIDTH])
    rc, rs1, rs2 = rc_ref[...], rs1_ref[...], rs2_ref[...]
    scale = HEAD_DIM ** -0.5
    k_new = _rope(qkv[:, ATTN_WIDTH:ATTN_WIDTH + KV_WIDTH], rc, rs1, rs2)
    v_new = qkv[:, ATTN_WIDTH + KV_WIDTH:ATTN_WIDTH + 2 * KV_WIDTH]
    pad = jnp.zeros((LANES - nb, KV_WIDTH), F32)
    k_new_t = jnp.concatenate([k_new, pad], axis=0).T
    v_new_t = jnp.concatenate([v_new, pad], axis=0).T

    lane = lax.broadcasted_iota(jnp.int32, (nb, LANES), 1)
    lo = lane < HEAD_DIM
    qz[...] = jnp.zeros_like(qz)
    for c in range(ATTN_WIDTH // LANES):
        qc = _rope(qkv[:, c * LANES:(c + 1) * LANES], rc, rs1, rs2) * scale
        qr = pltpu.roll(qc, HEAD_DIM, 1)
        zero = jnp.zeros_like(qc)
        if c < 2:
            even, odd = jnp.where(lo, qc, zero), jnp.where(lo, qr, zero)
        else:
            even, odd = jnp.where(lo, zero, qr), jnp.where(lo, zero, qc)
        qz[pl.ds(2 * c, nb, stride=QROWS), :] = even
        qz[pl.ds(2 * c + 1, nb, stride=QROWS), :] = odd

    last = lax.broadcasted_iota(jnp.int32, (KV_WIDTH, WINDOW), 1) == WINDOW - 1
    for b in range(nb):
        kout_ref[b] = jnp.where(last, k_new_t[:, b:b + 1], pltpu.roll(kbuf_ref[b], WINDOW - 1, 1))
        vout_ref[b] = jnp.where(last, v_new_t[:, b:b + 1], pltpu.roll(vbuf_ref[b], WINDOW - 1, 1))

    sink = sinkc_ref[...]
    q3 = qz[...].reshape(nb, QROWS, LANES).astype(BF)
    s = jnp.einsum('bhd,bdk->bhk', q3, kout_ref[...].astype(BF), preferred_element_type=F32)
    m = jnp.maximum(jnp.max(s, axis=-1, keepdims=True), sink)
    p = jnp.exp(s - m)
    den = jnp.sum(p, axis=-1, keepdims=True) + jnp.exp(sink - m)
    p = (p * (1.0 / den)).astype(BF)
    o3 = jnp.einsum('bhk,bdk->bhd', p, vout_ref[...].astype(BF), preferred_element_type=F32)
    o3_all[pl.ds(pl.multiple_of(step * (nb * QROWS), nb * QROWS), nb * QROWS), :] = o3.reshape(nb * QROWS, LANES)

    @pl.when(step == pl.num_programs(0) - 1)
    def _():
        _sample_tail(xall_ref, o3_all, h0re_ref, h0im_ref, g1_ref, win_ref, bg_ref, wbad_ref, wbs_ref, wo_ref,
                     lb_ref, bblk_ref, ctblk_ref, d_ref, wglu_ref, bglu_ref, x1_ref, hre_ref, him_ref)


def _sample_layer(x, kbuf, vbuf, h0re, h0im, rc, rs1, rs2, sinkc, g1, w_in, bg, wbad, wbs, wo, lb, bblk, ctblk,
                  d, wglu, bglu):
    nb = x.shape[0]
    tb = SAMPLE_TB
    blocked = (x, kbuf, vbuf)
    consts = (rc, rs1, rs2, sinkc, g1, w_in, x, h0re, h0im, bg, wbad, wbs, wo, lb, bblk, ctblk, d, wglu, bglu)
    kv_spec = pl.BlockSpec((tb, KV_WIDTH, WINDOW), lambda i: (i, 0, 0))
    whole = lambda shape: pl.BlockSpec(shape, lambda i: (0,) * len(shape))
    return pl.pallas_call(
        _sample_layer_kernel,
        grid=(nb // tb,),
        in_specs=[pl.BlockSpec((tb, D_MODEL), lambda i: (i, 0)), kv_spec, kv_spec]
        + [_const_spec(a.shape) for a in consts],
        out_specs=[kv_spec, kv_spec, whole((nb, D_MODEL)), whole((N_STATES, nb)), whole((N_STATES, nb))],
        out_shape=[
            jax.ShapeDtypeStruct((nb, KV_WIDTH, WINDOW), F32),
            jax.ShapeDtypeStruct((nb, KV_WIDTH, WINDOW), F32),
            jax.ShapeDtypeStruct((nb, D_MODEL), F32),
            jax.ShapeDtypeStruct((N_STATES, nb), F32),
            jax.ShapeDtypeStruct((N_STATES, nb), F32),
        ],
        scratch_shapes=[pltpu.VMEM((tb * QROWS, LANES), F32), pltpu.VMEM((nb * QROWS, LANES), F32)],
        compiler_params=pltpu.CompilerParams(
            dimension_semantics=("arbitrary",), vmem_limit_bytes=VMEM_LIMIT),
        name="sample_layer",
    )(*blocked, *consts)


def _sample_tail(x_ref, o3_ref, h0re_ref, h0im_ref, g1_ref, win_ref, bg_ref, wbad_ref,
                 wbs_ref, wo_ref, lb_ref, bblk_ref, ctblk_ref, d_ref, wglu_ref, bglu_ref,
                 x1_ref, hre_ref, him_ref):
    nb = x_ref.shape[0]
    x = x_ref[...]
    hn = _rms(x, g1_ref[...]).astype(BF)

    lane = lax.broadcasted_iota(jnp.int32, (nb, LANES), 1)
    lo = lane < HEAD_DIM
    a = jnp.zeros((nb, D_MODEL), F32)
    zero = jnp.zeros((nb, LANES), F32)
    for h in range(N_Q_HEADS):
        oh = o3_ref[pl.ds(h, nb, stride=QROWS), :]
        oh = jnp.where(lo, oh, zero) if h < N_Q_HEADS // 2 else jnp.where(lo, zero, oh)
        a = a + _dot(oh.astype(BF), wbad_ref[h])

    u = _dot(hn, win_ref[:, U_COL0:GATE_COL0])
    ub = u.astype(BF)
    lre, lim = lb_ref[0:1, :], lb_ref[1:2, :]
    y_cols = []
    for o in range(N_OCT):
        sl = slice(o * OCT_STATES, (o + 1) * OCT_STATES)
        bu = _dot(ub[:, o * LANES:(o + 1) * LANES], bblk_ref[o])
        blocks = [slice(c * LANES, (c + 1) * LANES) for c in range(o * OCT_COL, (o + 1) * OCT_COL)]
        h0r = jnp.concatenate([h0re_ref[rows, :].T for rows in blocks], axis=1)
        h0i = jnp.concatenate([h0im_ref[rows, :].T for rows in blocks], axis=1)
        hr = bu[:, :OCT_STATES] + (lre[:, sl] * h0r - lim[:, sl] * h0i)
        hi = bu[:, OCT_STATES:] + (lre[:, sl] * h0i + lim[:, sl] * h0r)
        for cc, rows in enumerate(blocks):
            hre_ref[rows, :] = hr[:, cc * LANES:(cc + 1) * LANES].T
            him_ref[rows, :] = hi[:, cc * LANES:(cc + 1) * LANES].T
        y_cols.append(_dot_nt(jnp.concatenate([hr, hi], axis=1).astype(BF), ctblk_ref[o]))
    y = jnp.concatenate(y_cols, axis=1) + d_ref[...] * u
    z = _gelu_tanh(y)
    gate = _dot(z.astype(BF), wglu_ref[...]) + bglu_ref[...]
    ssm = (z * _sigmoid(gate)).astype(BF)

    gates = _gate_cols(hn, win_ref, bg_ref, slice(0, GATE_WIDTH))
    x1_ref[...] = _merge_out(x, gates, a, _dot(ssm, wbs_ref[...]), wo_ref)


FFN_W_CHUNKS = 16
FFN_W_SLOTS = 4


def _stream_cast_weights(jobs):
    tasks, used = [], {}
    for w_hbm, w_bf, stage, sem in jobs:
        rows = stage.shape[1]
        for k in range(w_hbm.shape[0] // rows):
            slot = used.get(id(stage), 0) % stage.shape[0]
            used[id(stage)] = used.get(id(stage), 0) + 1
            copy = pltpu.make_async_copy(w_hbm.at[pl.ds(k * rows, rows), :], stage.at[slot], sem.at[slot])
            tasks.append((copy, w_bf, stage, slot, k * rows, rows))
    ahead = min(stage.shape[0] for _, _, stage, _ in jobs) - 1
    for copy, *_ in tasks[:ahead]:
        copy.start()
    for i, (copy, w_bf, stage, slot, row0, rows) in enumerate(tasks):
        if i + ahead < len(tasks):
            tasks[i + ahead][0].start()
        copy.wait()
        w_bf[row0:row0 + rows, :] = stage[slot].astype(BF)


def _ffn_rows(x, g2_ref, wgate, wup, wdown, gf_ref):
    h = _rms(x, g2_ref[...]).astype(BF)
    gate = _dot(h, wgate[...])
    up = _dot(h, wup[...])
    half_gate = 0.5 * gate
    act = ((half_gate + half_gate * jnp.tanh(half_gate)) * up).astype(BF)
    x2 = x + _dot(act, wdown[...])
    return _rms(x2, gf_ref[...])


def _ffn_kernel(xp_ref, xs_ref, g2_ref, wgate_hbm, wup_hbm, wdown_hbm, gf_ref, yp_ref, ys_ref,
                wgate, wup, wdown, stage_in, stage_out, sem_in, sem_out):
    i = pl.program_id(0)
    n_prompt = pl.num_programs(0) - 1

    @pl.when(i == 0)
    def _():
        _stream_cast_weights([(wgate_hbm, wgate, stage_in, sem_in), (wup_hbm, wup, stage_in, sem_in),
                              (wdown_hbm, wdown, stage_out, sem_out)])

    @pl.when(i < n_prompt)
    def _():
        half = FFN_TB // FFN_SPLIT
        for rows in (slice(h * half, (h + 1) * half) for h in range(FFN_SPLIT)):
            yp_ref[rows, :] = _ffn_rows(xp_ref[rows, :], g2_ref, wgate, wup, wdown, gf_ref)

    @pl.when(i == n_prompt)
    def _():
        ys_ref[...] = _ffn_rows(xs_ref[...], g2_ref, wgate, wup, wdown, gf_ref)


def _ffn(xp, xs, g2, wgate, wup, wdown, gf):
    n, ns = xp.shape[0], xs.shape[0]
    n_prompt = n // FFN_TB
    prompt_map = lambda i: (jnp.minimum(i, n_prompt - 1), 0)
    return pl.pallas_call(
        _ffn_kernel,
        grid=(n_prompt + 1,),
        in_specs=[
            pl.BlockSpec((FFN_TB, D_MODEL), prompt_map),
            _const_spec((ns, D_MODEL)),
            _const_spec((1, D_MODEL)),
            pl.BlockSpec(memory_space=pl.ANY),
            pl.BlockSpec(memory_space=pl.ANY),
            pl.BlockSpec(memory_space=pl.ANY),
            _const_spec((1, D_MODEL)),
        ],
        out_specs=[
            pl.BlockSpec((FFN_TB, D_MODEL), prompt_map),
            pl.BlockSpec((ns, D_MODEL), lambda i: (0, 0)),
        ],
        out_shape=[
            jax.ShapeDtypeStruct((n, D_MODEL), F32),
            jax.ShapeDtypeStruct((ns, D_MODEL), F32),
        ],
        scratch_shapes=[
            pltpu.VMEM((D_MODEL, D_FF), BF), pltpu.VMEM((D_MODEL, D_FF), BF), pltpu.VMEM((D_FF, D_MODEL), BF),
            pltpu.VMEM((FFN_W_SLOTS, D_MODEL // FFN_W_CHUNKS, D_FF), F32),
            pltpu.VMEM((FFN_W_SLOTS, D_FF // FFN_W_CHUNKS, D_MODEL), F32),
            pltpu.SemaphoreType.DMA((FFN_W_SLOTS,)), pltpu.SemaphoreType.DMA((FFN_W_SLOTS,)),
        ],
        compiler_params=pltpu.CompilerParams(
            dimension_semantics=("arbitrary",), vmem_limit_bytes=VMEM_LIMIT),
        name="ffn",
    )(xp, xs, g2, wgate, wup, wdown, gf)


def _rope_tables(pos):
    pos = np.asarray(pos, np.float64)
    inv_freq = ROPE_THETA ** (-(np.arange(ROPE_HALF, dtype=np.float64) * 2.0 / ROPE_DIM))
    ang = pos[:, None] * inv_freq[None, :]
    cos, sin = np.cos(ang), np.sin(ang)
    pad = np.zeros((pos.shape[0], HEAD_DIM - ROPE_DIM))
    zero = np.zeros_like(sin)
    rc = np.concatenate([cos, cos, pad + 1.0], axis=1)
    rs1 = np.concatenate([zero, sin, pad], axis=1)
    rs2 = np.concatenate([-sin, zero, pad], axis=1)
    rep = LANES // HEAD_DIM
    return tuple(jnp.asarray(np.tile(a, (1, rep)), F32) for a in (rc, rs1, rs2))


def _cmul(ar, ai, br, bi):
    return ar * br - ai * bi, ar * bi + ai * br


def _ssm_tables(lam_re, lam_im, log_dt, b_re, b_im, c_re, c_im):
    dt = jnp.exp(log_dt)[:, None]
    mag = jnp.exp(lam_re * dt)
    lb_re = mag * jnp.cos(lam_im * dt)
    lb_im = mag * jnp.sin(lam_im * dt)
    den = lam_re * lam_re + lam_im * lam_im
    nr = lb_re - 1.0
    k_re = ((nr * lam_re + lb_im * lam_im) / den)[..., None]
    k_im = ((lb_im * lam_re - nr * lam_im) / den)[..., None]
    bb_re = k_re * b_re - k_im * b_im
    bb_im = k_re * b_im + k_im * b_re

    a_re, a_im = lb_re, lb_im
    for _ in range(int(math.log2(CHUNK))):
        a_re, a_im = _cmul(a_re, a_im, a_re, a_im)
    s_re, s_im = a_re, a_im
    for _ in range(int(math.log2(SEG))):
        s_re, s_im = _cmul(s_re, s_im, s_re, s_im)

    eye = jnp.eye(OCT, dtype=F32).reshape(1, OCT, 1, OCT, 1)

    def block_diag(a):
        r, c = a.shape[1:]
        return (a.reshape(N_OCT, OCT, r, 1, c) * eye).reshape(N_OCT, OCT * r, OCT * c)

    bblk = jnp.concatenate([block_diag(jnp.swapaxes(bb_re, 1, 2)),
                            block_diag(jnp.swapaxes(bb_im, 1, 2))], axis=2)
    ctblk = jnp.concatenate([block_diag(c_re), block_diag(-c_im)], axis=2)

    oct_cols = lambda a: a.reshape(N_OCT, 1, OCT_STATES)
    lcol = jnp.concatenate([oct_cols(lb_re), oct_cols(lb_im),
                            jnp.zeros((N_OCT, SUBLANES - 2, OCT_STATES), F32)], axis=1)

    flat = lambda a: a.reshape(1, N_STATES)
    col = lambda a: a.reshape(N_COL, 1, LANES)
    a_tab = jnp.concatenate([col(a_re), col(a_im), col(s_re), col(s_im),
                             jnp.zeros((N_COL, SUBLANES - 4, LANES), F32)], axis=1)
    lb = jnp.concatenate([flat(lb_re), flat(lb_im)], axis=0)
    return bblk, ctblk, lcol, a_tab, lb


def kernel(x_prompt, x_sample, state_k_win, state_v_win, state_ssm_re, state_ssm_im, norm1_g, w_in, b_gate, attn_sinks, ssm_lam_re, ssm_lam_im, ssm_log_dt, ssm_b_re, ssm_b_im, ssm_c_re, ssm_c_im, ssm_d, w_glu, b_glu, w_branch_attn, w_branch_ssm, w_out, norm2_g, w_ffn_gate, w_ffn_up, w_ffn_down, norm_f_g):
    depth = w_in.shape[0]
    assert depth == 1
    b, t, _ = x_prompt.shape
    nb, s_len, _ = x_sample.shape
    assert s_len == 1 and state_k_win.shape[2] == WINDOW
    l = 0
    assert w_in.shape[2] == IN_WIDTH
    w_in_b = w_in[l].astype(BF)
    g1 = norm1_g[l].reshape(1, D_MODEL)
    g2 = norm2_g[l].reshape(1, D_MODEL)
    gf = norm_f_g.reshape(1, D_MODEL)
    bg = b_gate[l].reshape(1, GATE_WIDTH)
    d = ssm_d[l].reshape(1, SSM_WIDTH)
    wglu = w_glu[l].astype(BF)
    bglu = b_glu[l].reshape(1, SSM_WIDTH)
    wba = w_branch_attn[l].astype(BF)
    wbs = w_branch_ssm[l].astype(BF)
    wo = w_out[l].astype(BF)
    sinks = attn_sinks[l]

    bblk, ctblk, lcol, a_tab, lb = _ssm_tables(
        ssm_lam_re[l], ssm_lam_im[l], ssm_log_dt[l], ssm_b_re[l], ssm_b_im[l], ssm_c_re[l], ssm_c_im[l])

    rc, rs1, rs2 = _rope_tables(np.arange(t))
    yssm_p, hre_p, him_p = _s5_prompt(x_prompt, g1, w_in_b, bblk, ctblk, lcol, a_tab, d)
    x1_p, kwin_p, vwin_p = _mix_prompt(x_prompt, yssm_p, rc, rs1, rs2, sinks, g1, w_in_b, bg, wba, wbs, wo,
                                       wglu, bglu)

    rcs, rs1s, rs2s = _rope_tables(PAST_LEN + np.arange(1))
    sinkc = jnp.concatenate([sinks, jnp.zeros((QROWS - N_Q_HEADS,), F32)]).reshape(QROWS, 1)
    wbad = jnp.concatenate([wba.reshape(N_Q_HEADS, HEAD_DIM, D_MODEL)] * 2, axis=1)
    xs = x_sample.reshape(nb, D_MODEL)
    key_minor = lambda a: jnp.swapaxes(a.reshape(nb, WINDOW, KV_WIDTH), 1, 2)
    kwin_s, vwin_s, x1_s, hre_s, him_s = _sample_layer(
        xs, key_minor(state_k_win[l]), key_minor(state_v_win[l]),
        state_ssm_re[l].reshape(nb, N_STATES).T, state_ssm_im[l].reshape(nb, N_STATES).T,
        rcs, rs1s, rs2s, sinkc, g1, w_in_b, bg, wbad, wbs, wo, lb, bblk.astype(BF), ctblk.astype(BF),
        d, wglu, bglu)
    y_p, y_s = _ffn(x1_p, x1_s, g2, w_ffn_gate[l], w_ffn_up[l], w_ffn_down[l], gf)
    y_p = y_p.reshape(b, t, D_MODEL)
    y_s = y_s.reshape(nb, 1, D_MODEL)

    kv_shape_p = (1, b, WINDOW, N_KV_HEADS, HEAD_DIM)
    st_shape_p = (1, b, N_SSM_GROUPS, SSM_STATE)
    kv_shape_s = (1, nb, WINDOW, N_KV_HEADS, HEAD_DIM)
    st_shape_s = (1, nb, N_SSM_GROUPS, SSM_STATE)
    return (y_p, y_s,
            kwin_p.reshape(kv_shape_p), vwin_p.reshape(kv_shape_p),
            hre_p.reshape(st_shape_p), him_p.reshape(st_shape_p),
            jnp.swapaxes(kwin_s, 1, 2).reshape(kv_shape_s), jnp.swapaxes(vwin_s, 1, 2).reshape(kv_shape_s),
            hre_s.T.reshape(st_shape_s), him_s.T.reshape(st_shape_s))
```

```python
import math

import jax
import jax.numpy as jnp
import numpy as np
from jax import lax
from jax.experimental import pallas as pl
from jax.experimental.pallas import tpu as pltpu

D_MODEL = 1024
N_Q_HEADS = 8
N_KV_HEADS = 2
HEAD_DIM = 64
ATTN_WIDTH = N_Q_HEADS * HEAD_DIM
KV_WIDTH = N_KV_HEADS * HEAD_DIM
WINDOW = 128
ROPE_DIM = HEAD_DIM // 4
ROPE_HALF = ROPE_DIM // 2
ROPE_THETA = 500000.0
SSM_WIDTH = D_MODEL // 2
SSM_GROUP = 16
N_SSM_GROUPS = SSM_WIDTH // SSM_GROUP
SSM_STATE = 64
N_STATES = N_SSM_GROUPS * SSM_STATE
GATE_WIDTH = 2 * D_MODEL
QKV_WIDTH = ATTN_WIDTH + 2 * KV_WIDTH
U_COL0 = QKV_WIDTH
GATE_COL0 = U_COL0 + SSM_WIDTH
IN_WIDTH = GATE_COL0 + GATE_WIDTH
D_FF = -(-8 * D_MODEL // (3 * 256)) * 256
NORM_EPS = 1e-5
PAST_LEN = 8192

LANES = 128
SUBLANES = 8
CHUNK = 8
OCT = LANES // SSM_GROUP
N_OCT = N_SSM_GROUPS // OCT
OCT_STATES = OCT * SSM_STATE
OCT_COL = OCT_STATES // LANES
N_COL = N_STATES // LANES
N_SEG = SUBLANES
SEG = 16
S5_ROWS = N_SEG * SEG
M_PAIR = 4
U_HALF = SSM_WIDTH // 2
SEG_PITCH = SEG * CHUNK + SUBLANES
MIX_TB = 512
FFN_TB = 1024
FFN_SPLIT = 4
NEG_BIG = -1e30
LOG2_E = math.log2(math.e)
VMEM_LIMIT = 56 * 1024 * 1024

BF = jnp.bfloat16
F32 = jnp.float32


def _dot(a, b):
    return jnp.dot(a, b, preferred_element_type=F32)


def _dot_nt(a, b):
    return lax.dot_general(a, b, (((1,), (1,)), ((), ())), preferred_element_type=F32)


def _dot_nt_split(a, b):
    a_hi, b_hi = a.astype(BF), b.astype(BF)
    a_lo = (a - a_hi.astype(F32)).astype(BF)
    b_lo = (b - b_hi.astype(F32)).astype(BF)
    return _dot_nt(a_hi, b_hi) + (_dot_nt(a_hi, b_lo) + _dot_nt(a_lo, b_hi))


def _rms(x, g):
    return x * lax.rsqrt(jnp.mean(x * x, axis=-1, keepdims=True) + NORM_EPS) * g


def _sigmoid(x):
    return 1.0 / (1.0 + jnp.exp(-x))


def _gelu_tanh(x):
    c = math.sqrt(2.0 / math.pi)
    return 0.5 * x * (1.0 + jnp.tanh(c * (x + 0.044715 * (x * x * x))))


def _rope(a, rc, rs1, rs2):
    return a * rc + pltpu.roll(a, ROPE_HALF, 1) * rs1 + pltpu.roll(a, LANES - ROPE_HALF, 1) * rs2


def _const_spec(shape):
    nd = len(shape)
    return pl.BlockSpec(shape, lambda *_: (0,) * nd, pipeline_mode=pl.Buffered(1))


def _build_chunk_operators(bblk_ref, ctblk_ref, lcol_ref, m_s, e_s, f_s):
    for o in range(N_OCT):
        ct = ctblk_ref[o]
        lr, li = lcol_ref[o, 0:1, :], lcol_ref[o, 1:2, :]
        er, ei = bblk_ref[o, :, :OCT_STATES], bblk_ref[o, :, OCT_STATES:]
        k_blk = []
        for tau in range(CHUNK):
            e_cat = jnp.concatenate([er, ei], axis=1)
            i = CHUNK - 1 - tau
            e_s[o, i * LANES:(i + 1) * LANES, :] = e_cat.astype(BF)
            k_blk.append(_dot_nt_split(e_cat, ct).astype(BF))
            er, ei = er * lr - ei * li, er * li + ei * lr
        zero = jnp.zeros((LANES, LANES), BF)
        for j in range(CHUNK):
            jt, jj = divmod(j, M_PAIR)
            for i in range(M_PAIR * (jt + 1)):
                m_s[jt][o, i * LANES:(i + 1) * LANES, jj * LANES:(jj + 1) * LANES] = (
                    k_blk[j - i] if j >= i else zero)
        tr, ti = ct[:, :OCT_STATES], -ct[:, OCT_STATES:]
        for j in range(CHUNK):
            tr, ti = tr * lr - ti * li, tr * li + ti * lr
            f_s[o, :OCT_STATES, j * LANES:(j + 1) * LANES] = tr.T.astype(BF)
            f_s[o, OCT_STATES:, j * LANES:(j + 1) * LANES] = (-ti).T.astype(BF)


def _s5_prompt_kernel(x_ref, g1_ref, wu0_ref, wu1_ref, bblk_ref, ctblk_ref, lcol_ref, a_ref, d_ref,
                      out_ref, hfin_ref, *scratch):
    m_ref = scratch[:CHUNK // M_PAIR]
    e_ref, f_ref, us, ys, sre, sim, car = scratch[CHUNK // M_PAIR:]
    blk = pl.program_id(1)

    @pl.when((pl.program_id(0) == 0) & (blk == 0))
    def _():
        _build_chunk_operators(bblk_ref, ctblk_ref, lcol_ref, m_ref, e_ref, f_ref)

    @pl.when(blk == 0)
    def _():
        car[...] = jnp.zeros_like(car)

    seg_tokens = SEG * CHUNK
    seg_rows = [slice(s * SEG_PITCH, s * SEG_PITCH + seg_tokens) for s in range(N_SEG)]
    tok_rows = [slice(s * seg_tokens, (s + 1) * seg_tokens) for s in range(N_SEG)]

    for s in range(0, N_SEG, 2):
        hn = _rms(x_ref[s * seg_tokens:(s + 2) * seg_tokens, :], g1_ref[...]).astype(BF)
        u = jnp.concatenate([_dot(hn, wu0_ref[...]), _dot(hn, wu1_ref[...])], axis=1)
        for half in range(2):
            for cc in range(N_OCT):
                us[cc, seg_rows[s + half], :] = u[tok_rows[half], cc * LANES:(cc + 1) * LANES]

    sub = lax.broadcasted_iota(jnp.int32, (N_SEG, LANES), 0)

    def scan_column(c):
        tab = a_ref[c]
        are, aim = tab[0:1], tab[1:2]
        bre, bim = tab[2:3], tab[3:4]

        def step(cr, ci, r, keep_entering):
            slab = slice(r * N_SEG, (r + 1) * N_SEG)
            s_r, s_i = sre[c, slab, :], sim[c, slab, :]
            if keep_entering:
                sre[c, slab, :] = cr
                sim[c, slab, :] = ci
            return are * cr - aim * ci + s_r, are * ci + aim * cr + s_i

        cr = jnp.zeros((N_SEG, LANES), F32)
        ci = jnp.zeros((N_SEG, LANES), F32)
        for r in range(SEG):
            cr, ci = step(cr, ci, r, False)
        cv = car[c]
        pr, pi = cv[0:1], cv[1:2]
        sr = jnp.zeros((N_SEG, LANES), F32)
        si = jnp.zeros((N_SEG, LANES), F32)
        for s in range(N_SEG):
            sr = jnp.where(sub == s, pr, sr)
            si = jnp.where(sub == s, pi, si)
            pr, pi = (bre * pr - bim * pi + cr[s:s + 1], bre * pi + bim * pr + ci[s:s + 1])
        end = jnp.where(sub == 0, pr, jnp.where(sub == 1, pi, 0.0))
        car[c] = end
        hfin_ref[c] = end
        cr, ci = sr, si
        for r in range(SEG):
            cr, ci = step(cr, ci, r, True)

    for o in range(N_OCT):
        uo = jnp.concatenate(
            [jnp.concatenate([us[o, pl.ds(r * CHUNK + i, N_SEG, stride=SEG_PITCH), :] for r in range(SEG)],
                             axis=0).astype(BF) for i in range(CHUNK)], axis=1)
        s_end = _dot(uo, e_ref[o])
        cols = range(o * OCT_COL, (o + 1) * OCT_COL)
        for cc, c in enumerate(cols):
            sre[c] = s_end[:, cc * LANES:(cc + 1) * LANES]
            sim[c] = s_end[:, OCT_STATES + cc * LANES:OCT_STATES + (cc + 1) * LANES]
        y_in = jnp.concatenate(
            [_dot(uo[:, :(jt + 1) * M_PAIR * LANES], m_ref[jt][o]) for jt in range(CHUNK // M_PAIR)],
            axis=1)
        for c in cols:
            scan_column(c)
        hp = jnp.concatenate([sre[c] for c in cols] + [sim[c] for c in cols], axis=1).astype(BF)
        yo = y_in + _dot(hp, f_ref[o])
        for r in range(SEG):
            for j in range(CHUNK):
                ys[o, pl.ds(r * CHUNK + j, N_SEG, stride=SEG_PITCH), :] = (
                    yo[r * N_SEG:(r + 1) * N_SEG, j * LANES:(j + 1) * LANES])
        d_o = d_ref[:, o * LANES:(o + 1) * LANES]
        for s in range(N_SEG):
            out_ref[tok_rows[s], o * LANES:(o + 1) * LANES] = (
                ys[o, seg_rows[s], :] + d_o * us[o, seg_rows[s], :])


def _s5_prompt(x, g1, w_in, bblk, ctblk, lcol, a_tab, d):
    b, t, _ = x.shape
    tb = S5_ROWS * CHUNK
    nblk = t // tb
    x2 = x.reshape(b * t, D_MODEL)
    row_map = lambda i, j: (i * nblk + j, 0)
    op_shape = (N_OCT, CHUNK * LANES, CHUNK * LANES)
    out, hfin = pl.pallas_call(
        _s5_prompt_kernel,
        grid=(b, nblk),
        in_specs=[
            pl.BlockSpec((tb, D_MODEL), row_map),
            _const_spec((1, D_MODEL)),
            *[pl.BlockSpec((D_MODEL, U_HALF), lambda i, j, c=U_COL0 // U_HALF + h: (0, c),
                           pipeline_mode=pl.Buffered(1)) for h in range(2)],
            _const_spec(bblk.shape),
            _const_spec(ctblk.shape),
            _const_spec(lcol.shape),
            _const_spec(a_tab.shape),
            _const_spec((1, SSM_WIDTH)),
        ],
        out_specs=[
            pl.BlockSpec((tb, SSM_WIDTH), row_map),
            pl.BlockSpec((None, N_COL, SUBLANES, LANES), lambda i, j: (i, 0, 0, 0)),
        ],
        out_shape=[
            jax.ShapeDtypeStruct((b * t, SSM_WIDTH), F32),
            jax.ShapeDtypeStruct((b, N_COL, SUBLANES, LANES), F32),
        ],
        scratch_shapes=[
            *[pltpu.VMEM((N_OCT, (jt + 1) * M_PAIR * LANES, M_PAIR * LANES), BF)
              for jt in range(CHUNK // M_PAIR)],
            pltpu.VMEM(op_shape, BF), pltpu.VMEM(op_shape, BF),
            pltpu.VMEM((N_OCT, N_SEG * SEG_PITCH, LANES), F32),
            pltpu.VMEM((N_OCT, N_SEG * SEG_PITCH, LANES), F32),
            pltpu.VMEM((N_COL, S5_ROWS, LANES), F32), pltpu.VMEM((N_COL, S5_ROWS, LANES), F32),
            pltpu.VMEM((N_COL, SUBLANES, LANES), F32),
        ],
        compiler_params=pltpu.CompilerParams(
            dimension_semantics=("arbitrary", "arbitrary"), vmem_limit_bytes=VMEM_LIMIT),
        name="s5_prompt",
    )(x2, g1, w_in, w_in, bblk, ctblk, lcol, a_tab, d)
    return out, hfin[:, :, 0, :], hfin[:, :, 1, :]


def _gate_cols(hn, win_ref, bg_ref, cols):
    w_cols = slice(GATE_COL0 + cols.start, GATE_COL0 + cols.stop)
    return _sigmoid(_dot(hn, win_ref[:, w_cols]) + bg_ref[:, cols])


def _merge_out(x, gates, attn_proj, ssm_proj, wo_ref):
    merged = gates[:, :D_MODEL] * attn_proj + gates[:, D_MODEL:] * ssm_proj
    return x + _dot(merged.astype(BF), wo_ref[...])


def _half_split(a, ar, lo):
    z = jnp.zeros_like(a)
    return (jnp.where(lo, a, z).astype(BF), jnp.where(lo, z, ar).astype(BF),
            jnp.where(lo, ar, z).astype(BF), jnp.where(lo, z, a).astype(BF))


def _softmax_terms_t(st, bias_t, sink):
    st = st + bias_t
    m = jnp.maximum(jnp.max(st, axis=0, keepdims=True), sink)
    return jnp.exp2(st - m).astype(BF), jnp.exp2(sink - m)


def _mix_prompt_kernel(sinks_ref, x_ref, y_ref, rc_ref, rs1_ref, rs2_ref, g1_ref, win_ref,
                       bg_ref, wba_ref, wbs_ref, wo_ref, wglu_ref, bglu_ref,
                       x1_ref, kwin_ref, vwin_ref, kprev, vprev):
    t = pl.program_id(1)

    @pl.when(t == 0)
    def _():
        kprev[...] = jnp.zeros_like(kprev)
        vprev[...] = jnp.zeros_like(vprev)

    x = x_ref[...]
    hn = _rms(x, g1_ref[...]).astype(BF)
    qkv = _dot(hn, win_ref[:, :QKV_WIDTH])
    z = _gelu_tanh(y_ref[...])
    glu = _dot(z.astype(BF), wglu_ref[...]) + bglu_ref[...]
    rc, rs1, rs2 = rc_ref[...], rs1_ref[...], rs2_ref[...]
    scale = HEAD_DIM ** -0.5 * LOG2_E
    q = [(_rope(qkv[:, c * LANES:(c + 1) * LANES], rc, rs1, rs2) * scale).astype(BF)
         for c in range(ATTN_WIDTH // LANES)]
    k = _rope(qkv[:, ATTN_WIDTH:ATTN_WIDTH + KV_WIDTH], rc, rs1, rs2)
    v = qkv[:, ATTN_WIDTH + KV_WIDTH:ATTN_WIDTH + 2 * KV_WIDTH]
    ssm = (z * _sigmoid(glu)).astype(BF)
    ssm_proj = _dot(ssm, wbs_ref[...])

    w = WINDOW
    n_sub = MIX_TB // w
    lane = lax.broadcasted_iota(jnp.int32, (2 * w, LANES), 1)
    lo = lane < HEAD_DIM
    kj = lax.broadcasted_iota(jnp.int32, (2 * w, w), 0)
    qi = lax.broadcasted_iota(jnp.int32, (2 * w, w), 1)
    band = (kj > qi) & (kj <= qi + w)
    first = band & ((kj >= w) | (t > 0))
    bias_band = jnp.where(band, 0.0, NEG_BIG).astype(F32)
    bias_first = jnp.where(first, 0.0, NEG_BIG).astype(F32)
    col = lax.broadcasted_iota(jnp.int32, (1, 2 * w), 1)
    sum_r = lax.broadcasted_iota(jnp.int32, (2 * SUBLANES, 4 * w), 0)
    sum_c = lax.broadcasted_iota(jnp.int32, (2 * SUBLANES, 4 * w), 1)
    sum_rows = (((sum_r == 0) & (sum_c < 2 * w)) | ((sum_r == 1) & (sum_c >= 2 * w))).astype(F32)

    def sink_row(ha, hb):
        return jnp.where(col < w, sinks_ref[ha], sinks_ref[hb]) * LOG2_E

    gate_w = GATE_WIDTH // (n_sub * N_KV_HEADS)
    operands = {}

    def sub_block_operands(sb):
        if sb not in operands:
            cur = slice(sb * w, (sb + 1) * w)
            if sb == 0:
                kcat = jnp.concatenate([kprev[...], k[cur]], axis=0)
                vcat = jnp.concatenate([vprev[...], v[cur]], axis=0)
                bias = bias_first
            else:
                kcat = k[(sb - 1) * w:(sb + 1) * w]
                vcat = v[(sb - 1) * w:(sb + 1) * w]
                bias = bias_band
            k_split = _half_split(kcat, pltpu.roll(kcat, HEAD_DIM, 1), lo)
            k_stack = [jnp.concatenate(k_split[2 * g:2 * g + 2], axis=0) for g in range(N_KV_HEADS)]
            vt = vcat.T
            zero = jnp.zeros((HEAD_DIM, 2 * w), F32)
            v_stack = []
            for g in range(N_KV_HEADS):
                vg = vt[g * HEAD_DIM:(g + 1) * HEAD_DIM]
                v_stack.append(jnp.concatenate(
                    [jnp.concatenate([vg, zero], axis=1), jnp.concatenate([zero, vg], axis=1), sum_rows],
                    axis=0).astype(BF))
            operands[sb] = (k_stack, v_stack, jnp.concatenate([bias, bias], axis=1))
        return operands[sb]

    def scores(sb, grp):
        k_stack, _, _ = sub_block_operands(sb)
        cur = slice(sb * w, (sb + 1) * w)
        qq = jnp.concatenate([q[2 * grp][cur], q[2 * grp + 1][cur]], axis=0)
        return _dot_nt(k_stack[grp], qq)

    def weighted_values(sb, grp, s):
        _, v_stack, bias2 = sub_block_operands(sb)
        p_e, sink_e = _softmax_terms_t(s[:2 * w], bias2, sink_row(4 * grp, 4 * grp + 2))
        p_o, sink_o = _softmax_terms_t(s[2 * w:], bias2, sink_row(4 * grp + 1, 4 * grp + 3))
        d = _dot(v_stack[grp], jnp.concatenate([p_e, p_o], axis=0))
        heads = 2 * HEAD_DIM
        o2 = jnp.concatenate(
            [d[:HEAD_DIM] * (1.0 / (d[heads:heads + 1] + sink_e)),
             d[HEAD_DIM:heads] * (1.0 / (d[heads + 1:heads + 2] + sink_o))], axis=0)
        return [o2[:, :w].T, o2[:, w:].T]

    chains = [(sb, grp) for sb in range(n_sub) for grp in range(N_KV_HEADS)]
    gate_cols, outs = [], []
    s_next = scores(*chains[0])
    for i, (sb, grp) in enumerate(chains):
        s_cur = s_next
        gate_cols.append(_gate_cols(hn, win_ref, bg_ref, slice(i * gate_w, (i + 1) * gate_w)))
        if i + 1 < len(chains):
            s_next = scores(*chains[i + 1])
        outs += weighted_values(sb, grp, s_cur)
    per_sb = 2 * N_KV_HEADS
    attn = jnp.concatenate(
        [jnp.concatenate(outs[sb * per_sb:(sb + 1) * per_sb], axis=1) for sb in range(n_sub)],
        axis=0).astype(BF)
    gates = jnp.concatenate(gate_cols, axis=1)

    kprev[...] = k[MIX_TB - w:]
    vprev[...] = v[MIX_TB - w:]
    kwin_ref[...] = k[MIX_TB - w:]
    vwin_ref[...] = v[MIX_TB - w:]
    x1_ref[...] = _merge_out(x, gates, _dot(attn, wba_ref[...]), ssm_proj, wo_ref)


def _mix_prompt(x, y_ssm, rc, rs1, rs2, sinks, g1, w_in, bg, wba, wbs, wo, wglu, bglu):
    b, t, _ = x.shape
    nblk = t // MIX_TB
    x2 = x.reshape(b * t, D_MODEL)
    row_map = lambda i, j: (i * nblk + j, 0)
    x1, kwin, vwin = pl.pallas_call(
        _mix_prompt_kernel,
        grid=(b, nblk),
        in_specs=[
            pl.BlockSpec(memory_space=pltpu.SMEM),
            pl.BlockSpec((MIX_TB, D_MODEL), row_map),
            pl.BlockSpec((MIX_TB, SSM_WIDTH), row_map),
            pl.BlockSpec((MIX_TB, LANES), lambda i, j: (j, 0)),
            pl.BlockSpec((MIX_TB, LANES), lambda i, j: (j, 0)),
            pl.BlockSpec((MIX_TB, LANES), lambda i, j: (j, 0)),
            _const_spec((1, D_MODEL)),
            _const_spec(w_in.shape),
            _const_spec(bg.shape),
            _const_spec(wba.shape),
            _const_spec(wbs.shape),
            _const_spec(wo.shape),
            _const_spec(wglu.shape),
            _const_spec(bglu.shape),
        ],
        out_specs=[
            pl.BlockSpec((MIX_TB, D_MODEL), row_map),
            pl.BlockSpec((None, WINDOW, KV_WIDTH), lambda i, j: (i, 0, 0)),
            pl.BlockSpec((None, WINDOW, KV_WIDTH), lambda i, j: (i, 0, 0)),
        ],
        out_shape=[
            jax.ShapeDtypeStruct((b * t, D_MODEL), F32),
            jax.ShapeDtypeStruct((b, WINDOW, KV_WIDTH), F32),
            jax.ShapeDtypeStruct((b, WINDOW, KV_WIDTH), F32),
        ],
        scratch_shapes=[pltpu.VMEM((WINDOW, KV_WIDTH), F32), pltpu.VMEM((WINDOW, KV_WIDTH), F32)],
        compiler_params=pltpu.CompilerParams(
            dimension_semantics=("arbitrary", "arbitrary"), vmem_limit_bytes=VMEM_LIMIT),
        name="mix_prompt",
    )(sinks, x2, y_ssm, rc, rs1, rs2, g1, w_in, bg, wba, wbs, wo, wglu, bglu)
    return x1, kwin, vwin


QROWS = 16
SAMPLE_TB = 32


def _sample_layer_kernel(x_ref, kbuf_ref, vbuf_ref, rc_ref, rs1_ref, rs2_ref, sinkc_ref, g1_ref, win_ref,
                         xall_ref, h0re_ref, h0im_ref, bg_ref, wbad_ref, wbs_ref, wo_ref, lb_ref, bblk_ref,
                         ctblk_ref, d_ref, wglu_ref, bglu_ref,
                         kout_ref, vout_ref, x1_ref, hre_ref, him_ref, qz, o3_all):
    nb = x_ref.shape[0]
    step = pl.program_id(0)
    hn = _rms(x_ref[...], g1_ref[...]).astype(BF)
    qkv = _dot(hn, win_ref[:, :QKV_WIDTH])
    rc, rs1, rs2 = rc_ref[...], rs1_ref[...], rs2_ref[...]
    scale = HEAD_DIM ** -0.5
    k_new = _rope(qkv[:, ATTN_WIDTH:ATTN_WIDTH + KV_WIDTH], rc, rs1, rs2)
    v_new = qkv[:, ATTN_WIDTH + KV_WIDTH:ATTN_WIDTH + 2 * KV_WIDTH]
    pad = jnp.zeros((LANES - nb, KV_WIDTH), F32)
    k_new_t = jnp.concatenate([k_new, pad], axis=0).T
    v_new_t = jnp.concatenate([v_new, pad], axis=0).T

    lane = lax.broadcasted_iota(jnp.int32, (nb, LANES), 1)
    lo = lane < HEAD_DIM
    qz[...] = jnp.zeros_like(qz)
    for c in range(ATTN_WIDTH // LANES):
        qc = _rope(qkv[:, c * LANES:(c + 1) * LANES], rc, rs1, rs2) * scale
        qr = pltpu.roll(qc, HEAD_DIM, 1)
        zero = jnp.zeros_like(qc)
        if c < 2:
            even, odd = jnp.where(lo, qc, zero), jnp.where(lo, qr, zero)
        else:
            even, odd = jnp.where(lo, zero, qr), jnp.where(lo, zero, qc)
        qz[pl.ds(2 * c, nb, stride=QROWS), :] = even
        qz[pl.ds(2 * c + 1, nb, stride=QROWS), :] = odd

    last = lax.broadcasted_iota(jnp.int32, (KV_WIDTH, WINDOW), 1) == WINDOW - 1
    for b in range(nb):
        kout_ref[b] = jnp.where(last, k_new_t[:, b:b + 1], pltpu.roll(kbuf_ref[b], WINDOW - 1, 1))
        vout_ref[b] = jnp.where(last, v_new_t[:, b:b + 1], pltpu.roll(vbuf_ref[b], WINDOW - 1, 1))

    sink = sinkc_ref[...]
    q3 = qz[...].reshape(nb, QROWS, LANES).astype(BF)
    s = jnp.einsum('bhd,bdk->bhk', q3, kout_ref[...].astype(BF), preferred_element_type=F32)
    m = jnp.maximum(jnp.max(s, axis=-1, keepdims=True), sink)
    p = jnp.exp(s - m)
    den = jnp.sum(p, axis=-1, keepdims=True) + jnp.exp(sink - m)
    p = (p * (1.0 / den)).astype(BF)
    o3 = jnp.einsum('bhk,bdk->bhd', p, vout_ref[...].astype(BF), preferred_element_type=F32)
    o3_all[pl.ds(pl.multiple_of(step * (nb * QROWS), nb * QROWS), nb * QROWS), :] = o3.reshape(nb * QROWS, LANES)

    @pl.when(step == pl.num_programs(0) - 1)
    def _():
        _sample_tail(xall_ref, o3_all, h0re_ref, h0im_ref, g1_ref, win_ref, bg_ref, wbad_ref, wbs_ref, wo_ref,
                     lb_ref, bblk_ref, ctblk_ref, d_ref, wglu_ref, bglu_ref, x1_ref, hre_ref, him_ref)


def _sample_layer(x, kbuf, vbuf, h0re, h0im, rc, rs1, rs2, sinkc, g1, w_in, bg, wbad, wbs, wo, lb, bblk, ctblk,
                  d, wglu, bglu):
    nb = x.shape[0]
    tb = SAMPLE_TB
    blocked = (x, kbuf, vbuf)
    consts = (rc, rs1, rs2, sinkc, g1, w_in, x, h0re, h0im, bg, wbad, wbs, wo, lb, bblk, ctblk, d, wglu, bglu)
    kv_spec = pl.BlockSpec((tb, KV_WIDTH, WINDOW), lambda i: (i, 0, 0))
    whole = lambda shape: pl.BlockSpec(shape, lambda i: (0,) * len(shape))
    return pl.pallas_call(
        _sample_layer_kernel,
        grid=(nb // tb,),
        in_specs=[pl.BlockSpec((tb, D_MODEL), lambda i: (i, 0)), kv_spec, kv_spec]
        + [_const_spec(a.shape) for a in consts],
        out_specs=[kv_spec, kv_spec, whole((nb, D_MODEL)), whole((N_STATES, nb)), whole((N_STATES, nb))],
        out_shape=[
            jax.ShapeDtypeStruct((nb, KV_WIDTH, WINDOW), F32),
            jax.ShapeDtypeStruct((nb, KV_WIDTH, WINDOW), F32),
            jax.ShapeDtypeStruct((nb, D_MODEL), F32),
            jax.ShapeDtypeStruct((N_STATES, nb), F32),
            jax.ShapeDtypeStruct((N_STATES, nb), F32),
        ],
        scratch_shapes=[pltpu.VMEM((tb * QROWS, LANES), F32), pltpu.VMEM((nb * QROWS, LANES), F32)],
        compiler_params=pltpu.CompilerParams(
            dimension_semantics=("arbitrary",), vmem_limit_bytes=VMEM_LIMIT),
        name="sample_layer",
    )(*blocked, *consts)


def _sample_tail(x_ref, o3_ref, h0re_ref, h0im_ref, g1_ref, win_ref, bg_ref, wbad_ref,
                 wbs_ref, wo_ref, lb_ref, bblk_ref, ctblk_ref, d_ref, wglu_ref, bglu_ref,
                 x1_ref, hre_ref, him_ref):
    nb = x_ref.shape[0]
    x = x_ref[...]
    hn = _rms(x, g1_ref[...]).astype(BF)

    lane = lax.broadcasted_iota(jnp.int32, (nb, LANES), 1)
    lo = lane < HEAD_DIM
    a = jnp.zeros((nb, D_MODEL), F32)
    zero = jnp.zeros((nb, LANES), F32)
    for h in range(N_Q_HEADS):
        oh = o3_ref[pl.ds(h, nb, stride=QROWS), :]
        oh = jnp.where(lo, oh, zero) if h < N_Q_HEADS // 2 else jnp.where(lo, zero, oh)
        a = a + _dot(oh.astype(BF), wbad_ref[h])

    u = _dot(hn, win_ref[:, U_COL0:GATE_COL0])
    ub = u.astype(BF)
    lre, lim = lb_ref[0:1, :], lb_ref[1:2, :]
    y_cols = []
    for o in range(N_OCT):
        sl = slice(o * OCT_STATES, (o + 1) * OCT_STATES)
        bu = _dot(ub[:, o * LANES:(o + 1) * LANES], bblk_ref[o])
        blocks = [slice(c * LANES, (c + 1) * LANES) for c in range(o * OCT_COL, (o + 1) * OCT_COL)]
        h0r = jnp.concatenate([h0re_ref[rows, :].T for rows in blocks], axis=1)
        h0i = jnp.concatenate([h0im_ref[rows, :].T for rows in blocks], axis=1)
        hr = bu[:, :OCT_STATES] + (lre[:, sl] * h0r - lim[:, sl] * h0i)
        hi = bu[:, OCT_STATES:] + (lre[:, sl] * h0i + lim[:, sl] * h0r)
        for cc, rows in enumerate(blocks):
            hre_ref[rows, :] = hr[:, cc * LANES:(cc + 1) * LANES].T
            him_ref[rows, :] = hi[:, cc * LANES:(cc + 1) * LANES].T
        y_cols.append(_dot_nt(jnp.concatenate([hr, hi], axis=1).astype(BF), ctblk_ref[o]))
    y = jnp.concatenate(y_cols, axis=1) + d_ref[...] * u
    z = _gelu_tanh(y)
    gate = _dot(z.astype(BF), wglu_ref[...]) + bglu_ref[...]
    ssm = (z * _sigmoid(gate)).astype(BF)

    gates = _gate_cols(hn, win_ref, bg_ref, slice(0, GATE_WIDTH))
    x1_ref[...] = _merge_out(x, gates, a, _dot(ssm, wbs_ref[...]), wo_ref)


FFN_W_CHUNKS = 16
FFN_W_SLOTS = 4


def _stream_cast_weights(jobs):
    tasks, used = [], {}
    for w_hbm, w_bf, stage, sem in jobs:
        rows = stage.shape[1]
        for k in range(w_hbm.shape[0] // rows):
            slot = used.get(id(stage), 0) % stage.shape[0]
            used[id(stage)] = used.get(id(stage), 0) + 1
            copy = pltpu.make_async_copy(w_hbm.at[pl.ds(k * rows, rows), :], stage.at[slot], sem.at[slot])
            tasks.append((copy, w_bf, stage, slot, k * rows, rows))
    ahead = min(stage.shape[0] for _, _, stage, _ in jobs) - 1
    for copy, *_ in tasks[:ahead]:
        copy.start()
    for i, (copy, w_bf, stage, slot, row0, rows) in enumerate(tasks):
        if i + ahead < len(tasks):
            tasks[i + ahead][0].start()
        copy.wait()
        w_bf[row0:row0 + rows, :] = stage[slot].astype(BF)


def _ffn_rows(x, g2_ref, wgu, wdown, gf_ref):
    h = _rms(x, g2_ref[...]).astype(BF)
    gate_up = _dot(h, wgu[...])
    gate, up = gate_up[:, :D_FF], gate_up[:, D_FF:]
    half_gate = 0.5 * gate
    act = ((half_gate + half_gate * jnp.tanh(half_gate)) * up).astype(BF)
    x2 = x + _dot(act, wdown[...])
    return _rms(x2, gf_ref[...])


def _ffn_kernel(xp_ref, xs_ref, g2_ref, wgate_hbm, wup_hbm, wdown_hbm, gf_ref, yp_ref, ys_ref,
                wgu, wdown, stage_in, stage_out, sem_in, sem_out):
    i = pl.program_id(0)
    n_prompt = pl.num_programs(0) - 1

    @pl.when(i == 0)
    def _():
        _stream_cast_weights([(wgate_hbm, wgu.at[:, :D_FF], stage_in, sem_in),
                              (wup_hbm, wgu.at[:, D_FF:], stage_in, sem_in),
                              (wdown_hbm, wdown, stage_out, sem_out)])

    @pl.when(i < n_prompt)
    def _():
        half = FFN_TB // FFN_SPLIT
        for rows in (slice(h * half, (h + 1) * half) for h in range(FFN_SPLIT)):
            yp_ref[rows, :] = _ffn_rows(xp_ref[rows, :], g2_ref, wgu, wdown, gf_ref)

    @pl.when(i == n_prompt)
    def _():
        ys_ref[...] = _ffn_rows(xs_ref[...], g2_ref, wgu, wdown, gf_ref)


def _ffn(xp, xs, g2, wgate, wup, wdown, gf):
    n, ns = xp.shape[0], xs.shape[0]
    n_prompt = n // FFN_TB
    prompt_map = lambda i: (jnp.minimum(i, n_prompt - 1), 0)
    return pl.pallas_call(
        _ffn_kernel,
        grid=(n_prompt + 1,),
        in_specs=[
            pl.BlockSpec((FFN_TB, D_MODEL), prompt_map),
            _const_spec((ns, D_MODEL)),
            _const_spec((1, D_MODEL)),
            pl.BlockSpec(memory_space=pl.ANY),
            pl.BlockSpec(memory_space=pl.ANY),
            pl.BlockSpec(memory_space=pl.ANY),
            _const_spec((1, D_MODEL)),
        ],
        out_specs=[
            pl.BlockSpec((FFN_TB, D_MODEL), prompt_map),
            pl.BlockSpec((ns, D_MODEL), lambda i: (0, 0)),
        ],
        out_shape=[
            jax.ShapeDtypeStruct((n, D_MODEL), F32),
            jax.ShapeDtypeStruct((ns, D_MODEL), F32),
        ],
        scratch_shapes=[
            pltpu.VMEM((D_MODEL, 2 * D_FF), BF), pltpu.VMEM((D_FF, D_MODEL), BF),
            pltpu.VMEM((FFN_W_SLOTS, D_MODEL // FFN_W_CHUNKS, D_FF), F32),
            pltpu.VMEM((FFN_W_SLOTS, D_FF // FFN_W_CHUNKS, D_MODEL), F32),
            pltpu.SemaphoreType.DMA((FFN_W_SLOTS,)), pltpu.SemaphoreType.DMA((FFN_W_SLOTS,)),
        ],
        compiler_params=pltpu.CompilerParams(
            dimension_semantics=("arbitrary",), vmem_limit_bytes=VMEM_LIMIT),
        name="ffn",
    )(xp, xs, g2, wgate, wup, wdown, gf)


def _rope_tables(pos):
    pos = np.asarray(pos, np.float64)
    inv_freq = ROPE_THETA ** (-(np.arange(ROPE_HALF, dtype=np.float64) * 2.0 / ROPE_DIM))
    ang = pos[:, None] * inv_freq[None, :]
    cos, sin = np.cos(ang), np.sin(ang)
    pad = np.zeros((pos.shape[0], HEAD_DIM - ROPE_DIM))
    zero = np.zeros_like(sin)
    rc = np.concatenate([cos, cos, pad + 1.0], axis=1)
    rs1 = np.concatenate([zero, sin, pad], axis=1)
    rs2 = np.concatenate([-sin, zero, pad], axis=1)
    rep = LANES // HEAD_DIM
    return tuple(jnp.asarray(np.tile(a, (1, rep)), F32) for a in (rc, rs1, rs2))


def _cmul(ar, ai, br, bi):
    return ar * br - ai * bi, ar * bi + ai * br


def _ssm_tables(lam_re, lam_im, log_dt, b_re, b_im, c_re, c_im):
    dt = jnp.exp(log_dt)[:, None]
    mag = jnp.exp(lam_re * dt)
    lb_re = mag * jnp.cos(lam_im * dt)
    lb_im = mag * jnp.sin(lam_im * dt)
    den = lam_re * lam_re + lam_im * lam_im
    nr = lb_re - 1.0
    k_re = ((nr * lam_re + lb_im * lam_im) / den)[..., None]
    k_im = ((lb_im * lam_re - nr * lam_im) / den)[..., None]
    bb_re = k_re * b_re - k_im * b_im
    bb_im = k_re * b_im + k_im * b_re

    a_re, a_im = lb_re, lb_im
    for _ in range(int(math.log2(CHUNK))):
        a_re, a_im = _cmul(a_re, a_im, a_re, a_im)
    s_re, s_im = a_re, a_im
    for _ in range(int(math.log2(SEG))):
        s_re, s_im = _cmul(s_re, s_im, s_re, s_im)

    eye = jnp.eye(OCT, dtype=F32).reshape(1, OCT, 1, OCT, 1)

    def block_diag(a):
        r, c = a.shape[1:]
        return (a.reshape(N_OCT, OCT, r, 1, c) * eye).reshape(N_OCT, OCT * r, OCT * c)

    bblk = jnp.concatenate([block_diag(jnp.swapaxes(bb_re, 1, 2)),
                            block_diag(jnp.swapaxes(bb_im, 1, 2))], axis=2)
    ctblk = jnp.concatenate([block_diag(c_re), block_diag(-c_im)], axis=2)

    oct_cols = lambda a: a.reshape(N_OCT, 1, OCT_STATES)
    lcol = jnp.concatenate([oct_cols(lb_re), oct_cols(lb_im),
                            jnp.zeros((N_OCT, SUBLANES - 2, OCT_STATES), F32)], axis=1)

    flat = lambda a: a.reshape(1, N_STATES)
    col = lambda a: a.reshape(N_COL, 1, LANES)
    a_tab = jnp.concatenate([col(a_re), col(a_im), col(s_re), col(s_im),
                             jnp.zeros((N_COL, SUBLANES - 4, LANES), F32)], axis=1)
    lb = jnp.concatenate([flat(lb_re), flat(lb_im)], axis=0)
    return bblk, ctblk, lcol, a_tab, lb


def kernel(x_prompt, x_sample, state_k_win, state_v_win, state_ssm_re, state_ssm_im, norm1_g, w_in, b_gate, attn_sinks, ssm_lam_re, ssm_lam_im, ssm_log_dt, ssm_b_re, ssm_b_im, ssm_c_re, ssm_c_im, ssm_d, w_glu, b_glu, w_branch_attn, w_branch_ssm, w_out, norm2_g, w_ffn_gate, w_ffn_up, w_ffn_down, norm_f_g):
    depth = w_in.shape[0]
    assert depth == 1
    b, t, _ = x_prompt.shape
    nb, s_len, _ = x_sample.shape
    assert s_len == 1 and state_k_win.shape[2] == WINDOW
    l = 0
    assert w_in.shape[2] == IN_WIDTH
    w_in_b = w_in[l].astype(BF)
    g1 = norm1_g[l].reshape(1, D_MODEL)
    g2 = norm2_g[l].reshape(1, D_MODEL)
    gf = norm_f_g.reshape(1, D_MODEL)
    bg = b_gate[l].reshape(1, GATE_WIDTH)
    d = ssm_d[l].reshape(1, SSM_WIDTH)
    wglu = w_glu[l].astype(BF)
    bglu = b_glu[l].reshape(1, SSM_WIDTH)
    wba = w_branch_attn[l].astype(BF)
    wbs = w_branch_ssm[l].astype(BF)
    wo = w_out[l].astype(BF)
    sinks = attn_sinks[l]

    bblk, ctblk, lcol, a_tab, lb = _ssm_tables(
        ssm_lam_re[l], ssm_lam_im[l], ssm_log_dt[l], ssm_b_re[l], ssm_b_im[l], ssm_c_re[l], ssm_c_im[l])

    rc, rs1, rs2 = _rope_tables(np.arange(t))
    yssm_p, hre_p, him_p = _s5_prompt(x_prompt, g1, w_in_b, bblk, ctblk, lcol, a_tab, d)
    x1_p, kwin_p, vwin_p = _mix_prompt(x_prompt, yssm_p, rc, rs1, rs2, sinks, g1, w_in_b, bg, wba, wbs, wo,
                                       wglu, bglu)

    rcs, rs1s, rs2s = _rope_tables(PAST_LEN + np.arange(1))
    sinkc = jnp.concatenate([sinks, jnp.zeros((QROWS - N_Q_HEADS,), F32)]).reshape(QROWS, 1)
    wbad = jnp.concatenate([wba.reshape(N_Q_HEADS, HEAD_DIM, D_MODEL)] * 2, axis=1)
    xs = x_sample.reshape(nb, D_MODEL)
    key_minor = lambda a: jnp.swapaxes(a.reshape(nb, WINDOW, KV_WIDTH), 1, 2)
    kwin_s, vwin_s, x1_s, hre_s, him_s = _sample_layer(
        xs, key_minor(state_k_win[l]), key_minor(state_v_win[l]),
        state_ssm_re[l].reshape(nb, N_STATES).T, state_ssm_im[l].reshape(nb, N_STATES).T,
        rcs, rs1s, rs2s, sinkc, g1, w_in_b, bg, wbad, wbs, wo, lb, bblk.astype(BF), ctblk.astype(BF),
        d, wglu, bglu)
    y_p, y_s = _ffn(x1_p, x1_s, g2, w_ffn_gate[l], w_ffn_up[l], w_ffn_down[l], gf)
    y_p = y_p.reshape(b, t, D_MODEL)
    y_s = y_s.reshape(nb, 1, D_MODEL)

    kv_shape_p = (1, b, WINDOW, N_KV_HEADS, HEAD_DIM)
    st_shape_p = (1, b, N_SSM_GROUPS, SSM_STATE)
    kv_shape_s = (1, nb, WINDOW, N_KV_HEADS, HEAD_DIM)
    st_shape_s = (1, nb, N_SSM_GROUPS, SSM_STATE)
    return (y_p, y_s,
            kwin_p.reshape(kv_shape_p), vwin_p.reshape(kv_shape_p),
            hre_p.reshape(st_shape_p), him_p.reshape(st_shape_p),
            jnp.swapaxes(kwin_s, 1, 2).reshape(kv_shape_s), jnp.swapaxes(vwin_s, 1, 2).reshape(kv_shape_s),
            hre_s.T.reshape(st_shape_s), him_s.T.reshape(st_shape_s))
```

```python
import math

import jax
import jax.numpy as jnp
import numpy as np
from jax import lax
from jax.experimental import pallas as pl
from jax.experimental.pallas import tpu as pltpu

D_MODEL = 1024
N_Q_HEADS = 8
N_KV_HEADS = 2
HEAD_DIM = 64
ATTN_WIDTH = N_Q_HEADS * HEAD_DIM
KV_WIDTH = N_KV_HEADS * HEAD_DIM
WINDOW = 128
ROPE_DIM = HEAD_DIM // 4
ROPE_HALF = ROPE_DIM // 2
ROPE_THETA = 500000.0
SSM_WIDTH = D_MODEL // 2
SSM_GROUP = 16
N_SSM_GROUPS = SSM_WIDTH // SSM_GROUP
SSM_STATE = 64
N_STATES = N_SSM_GROUPS * SSM_STATE
GATE_WIDTH = 2 * D_MODEL
QKV_WIDTH = ATTN_WIDTH + 2 * KV_WIDTH
U_COL0 = QKV_WIDTH
GATE_COL0 = U_COL0 + SSM_WIDTH
IN_WIDTH = GATE_COL0 + GATE_WIDTH
D_FF = -(-8 * D_MODEL // (3 * 256)) * 256
NORM_EPS = 1e-5
PAST_LEN = 8192

LANES = 128
SUBLANES = 8
CHUNK = 8
OCT = LANES // SSM_GROUP
N_OCT = N_SSM_GROUPS // OCT
OCT_STATES = OCT * SSM_STATE
OCT_COL = OCT_STATES // LANES
N_COL = N_STATES // LANES
N_SEG = SUBLANES
SEG = 16
S5_ROWS = N_SEG * SEG
M_PAIR = 2
U_HALF = SSM_WIDTH // 2
SEG_PITCH = SEG * CHUNK + SUBLANES
MIX_TB = 512
FFN_TB = 1024
FFN_SPLIT = 4
NEG_BIG = -1e30
LOG2_E = math.log2(math.e)
VMEM_LIMIT = 56 * 1024 * 1024

BF = jnp.bfloat16
F32 = jnp.float32


def _dot(a, b):
    return jnp.dot(a, b, preferred_element_type=F32)


def _dot_nt(a, b):
    return lax.dot_general(a, b, (((1,), (1,)), ((), ())), preferred_element_type=F32)


def _dot_nt_split(a, b):
    a_hi, b_hi = a.astype(BF), b.astype(BF)
    a_lo = (a - a_hi.astype(F32)).astype(BF)
    b_lo = (b - b_hi.astype(F32)).astype(BF)
    return _dot_nt(a_hi, b_hi) + (_dot_nt(a_hi, b_lo) + _dot_nt(a_lo, b_hi))


def _rms(x, g):
    return x * lax.rsqrt(jnp.mean(x * x, axis=-1, keepdims=True) + NORM_EPS) * g


def _sigmoid(x):
    return 1.0 / (1.0 + jnp.exp(-x))


def _gelu_tanh(x):
    c = math.sqrt(2.0 / math.pi)
    return 0.5 * x * (1.0 + jnp.tanh(c * (x + 0.044715 * (x * x * x))))


def _rope(a, rc, rs1, rs2):
    return a * rc + pltpu.roll(a, ROPE_HALF, 1) * rs1 + pltpu.roll(a, LANES - ROPE_HALF, 1) * rs2


def _const_spec(shape):
    nd = len(shape)
    return pl.BlockSpec(shape, lambda *_: (0,) * nd, pipeline_mode=pl.Buffered(1))


def _build_chunk_operators(bblk_ref, ctblk_ref, lcol_ref, m_s, e_s, f_s):
    for o in range(N_OCT):
        ct = ctblk_ref[o]
        lr, li = lcol_ref[o, 0:1, :], lcol_ref[o, 1:2, :]
        er, ei = bblk_ref[o, :, :OCT_STATES], bblk_ref[o, :, OCT_STATES:]
        k_blk = []
        for tau in range(CHUNK):
            e_cat = jnp.concatenate([er, ei], axis=1)
            i = CHUNK - 1 - tau
            e_s[o, i * LANES:(i + 1) * LANES, :] = e_cat.astype(BF)
            k_blk.append(_dot_nt_split(e_cat, ct).astype(BF))
            er, ei = er * lr - ei * li, er * li + ei * lr
        zero = jnp.zeros((LANES, LANES), BF)
        for j in range(CHUNK):
            jt, jj = divmod(j, M_PAIR)
            for i in range(M_PAIR * (jt + 1)):
                m_s[jt][o, i * LANES:(i + 1) * LANES, jj * LANES:(jj + 1) * LANES] = (
                    k_blk[j - i] if j >= i else zero)
        tr, ti = ct[:, :OCT_STATES], -ct[:, OCT_STATES:]
        for j in range(CHUNK):
            tr, ti = tr * lr - ti * li, tr * li + ti * lr
            f_s[o, :OCT_STATES, j * LANES:(j + 1) * LANES] = tr.T.astype(BF)
            f_s[o, OCT_STATES:, j * LANES:(j + 1) * LANES] = (-ti).T.astype(BF)


def _s5_prompt_kernel(x_ref, g1_ref, wu0_ref, wu1_ref, bblk_ref, ctblk_ref, lcol_ref, a_ref, d_ref,
                      out_ref, hfin_ref,
                      m0_ref, m1_ref, m2_ref, m3_ref, e_ref, f_ref, us, ys, sre, sim, car):
    m_ref = (m0_ref, m1_ref, m2_ref, m3_ref)
    blk = pl.program_id(1)

    @pl.when((pl.program_id(0) == 0) & (blk == 0))
    def _():
        _build_chunk_operators(bblk_ref, ctblk_ref, lcol_ref, m_ref, e_ref, f_ref)

    @pl.when(blk == 0)
    def _():
        car[...] = jnp.zeros_like(car)

    seg_tokens = SEG * CHUNK
    seg_rows = [slice(s * SEG_PITCH, s * SEG_PITCH + seg_tokens) for s in range(N_SEG)]
    tok_rows = [slice(s * seg_tokens, (s + 1) * seg_tokens) for s in range(N_SEG)]

    for s in range(0, N_SEG, 2):
        hn = _rms(x_ref[s * seg_tokens:(s + 2) * seg_tokens, :], g1_ref[...]).astype(BF)
        u = jnp.concatenate([_dot(hn, wu0_ref[...]), _dot(hn, wu1_ref[...])], axis=1)
        for half in range(2):
            for cc in range(N_OCT):
                us[cc, seg_rows[s + half], :] = u[tok_rows[half], cc * LANES:(cc + 1) * LANES]

    sub = lax.broadcasted_iota(jnp.int32, (N_SEG, LANES), 0)

    def scan_column(c):
        tab = a_ref[c]
        are, aim = tab[0:1], tab[1:2]
        bre, bim = tab[2:3], tab[3:4]

        def step(cr, ci, r, keep_entering):
            slab = slice(r * N_SEG, (r + 1) * N_SEG)
            s_r, s_i = sre[c, slab, :], sim[c, slab, :]
            if keep_entering:
                sre[c, slab, :] = cr
                sim[c, slab, :] = ci
            return are * cr - aim * ci + s_r, are * ci + aim * cr + s_i

        cr = jnp.zeros((N_SEG, LANES), F32)
        ci = jnp.zeros((N_SEG, LANES), F32)
        for r in range(SEG):
            cr, ci = step(cr, ci, r, False)
        cv = car[c]
        pr, pi = cv[0:1], cv[1:2]
        sr = jnp.zeros((N_SEG, LANES), F32)
        si = jnp.zeros((N_SEG, LANES), F32)
        for s in range(N_SEG):
            sr = jnp.where(sub == s, pr, sr)
            si = jnp.where(sub == s, pi, si)
            pr, pi = (bre * pr - bim * pi + cr[s:s + 1], bre * pi + bim * pr + ci[s:s + 1])
        end = jnp.where(sub == 0, pr, jnp.where(sub == 1, pi, 0.0))
        car[c] = end
        hfin_ref[c] = end
        cr, ci = sr, si
        for r in range(SEG):
            cr, ci = step(cr, ci, r, True)

    for o in range(N_OCT):
        uo = jnp.concatenate(
            [jnp.concatenate([us[o, pl.ds(r * CHUNK + i, N_SEG, stride=SEG_PITCH), :] for r in range(SEG)],
                             axis=0).astype(BF) for i in range(CHUNK)], axis=1)
        s_end = _dot(uo, e_ref[o])
        cols = range(o * OCT_COL, (o + 1) * OCT_COL)
        for cc, c in enumerate(cols):
            sre[c] = s_end[:, cc * LANES:(cc + 1) * LANES]
            sim[c] = s_end[:, OCT_STATES + cc * LANES:OCT_STATES + (cc + 1) * LANES]
        y_in = jnp.concatenate(
            [_dot(uo[:, :(jt + 1) * M_PAIR * LANES], m_ref[jt][o]) for jt in range(CHUNK // M_PAIR)],
            axis=1)
        for c in cols:
            scan_column(c)
        hp = jnp.concatenate([sre[c] for c in cols] + [sim[c] for c in cols], axis=1).astype(BF)
        yo = y_in + _dot(hp, f_ref[o])
        for r in range(SEG):
            for j in range(CHUNK):
                ys[o, pl.ds(r * CHUNK + j, N_SEG, stride=SEG_PITCH), :] = (
                    yo[r * N_SEG:(r + 1) * N_SEG, j * LANES:(j + 1) * LANES])
        d_o = d_ref[:, o * LANES:(o + 1) * LANES]
        for s in range(N_SEG):
            out_ref[tok_rows[s], o * LANES:(o + 1) * LANES] = (
                ys[o, seg_rows[s], :] + d_o * us[o, seg_rows[s], :])


def _s5_prompt(x, g1, w_in, bblk, ctblk, lcol, a_tab, d):
    b, t, _ = x.shape
    tb = S5_ROWS * CHUNK
    nblk = t // tb
    x2 = x.reshape(b * t, D_MODEL)
    row_map = lambda i, j: (i * nblk + j, 0)
    op_shape = (N_OCT, CHUNK * LANES, CHUNK * LANES)
    out, hfin = pl.pallas_call(
        _s5_prompt_kernel,
        grid=(b, nblk),
        in_specs=[
            pl.BlockSpec((tb, D_MODEL), row_map),
            _const_spec((1, D_MODEL)),
            *[pl.BlockSpec((D_MODEL, U_HALF), lambda i, j, c=U_COL0 // U_HALF + h: (0, c),
                           pipeline_mode=pl.Buffered(1)) for h in range(2)],
            _const_spec(bblk.shape),
            _const_spec(ctblk.shape),
            _const_spec(lcol.shape),
            _const_spec(a_tab.shape),
            _const_spec((1, SSM_WIDTH)),
        ],
        out_specs=[
            pl.BlockSpec((tb, SSM_WIDTH), row_map),
            pl.BlockSpec((None, N_COL, SUBLANES, LANES), lambda i, j: (i, 0, 0, 0)),
        ],
        out_shape=[
            jax.ShapeDtypeStruct((b * t, SSM_WIDTH), F32),
            jax.ShapeDtypeStruct((b, N_COL, SUBLANES, LANES), F32),
        ],
        scratch_shapes=[
            *[pltpu.VMEM((N_OCT, (jt + 1) * M_PAIR * LANES, M_PAIR * LANES), BF)
              for jt in range(CHUNK // M_PAIR)],
            pltpu.VMEM(op_shape, BF), pltpu.VMEM(op_shape, BF),
            pltpu.VMEM((N_OCT, N_SEG * SEG_PITCH, LANES), F32),
            pltpu.VMEM((N_OCT, N_SEG * SEG_PITCH, LANES), F32),
            pltpu.VMEM((N_COL, S5_ROWS, LANES), F32), pltpu.VMEM((N_COL, S5_ROWS, LANES), F32),
            pltpu.VMEM((N_COL, SUBLANES, LANES), F32),
        ],
        compiler_params=pltpu.CompilerParams(
            dimension_semantics=("arbitrary", "arbitrary"), vmem_limit_bytes=VMEM_LIMIT),
        name="s5_prompt",
    )(x2, g1, w_in, w_in, bblk, ctblk, lcol, a_tab, d)
    return out, hfin[:, :, 0, :], hfin[:, :, 1, :]


def _gate_cols(hn, win_ref, bg_ref, cols):
    w_cols = slice(GATE_COL0 + cols.start, GATE_COL0 + cols.stop)
    return _sigmoid(_dot(hn, win_ref[:, w_cols]) + bg_ref[:, cols])


def _merge_out(x, gates, attn_proj, ssm_proj, wo_ref):
    merged = gates[:, :D_MODEL] * attn_proj + gates[:, D_MODEL:] * ssm_proj
    return x + _dot(merged.astype(BF), wo_ref[...])


def _half_split(a, ar, lo):
    z = jnp.zeros_like(a)
    return (jnp.where(lo, a, z).astype(BF), jnp.where(lo, z, ar).astype(BF),
            jnp.where(lo, ar, z).astype(BF), jnp.where(lo, z, a).astype(BF))


def _softmax_terms_t(st, bias_t, sink):
    st = st + bias_t
    m = jnp.maximum(jnp.max(st, axis=0, keepdims=True), sink)
    return jnp.exp2(st - m).astype(BF), jnp.exp2(sink - m)


def _mix_prompt_kernel(sinks_ref, x_ref, y_ref, rc_ref, rs1_ref, rs2_ref, g1_ref, win_ref,
                       bg_ref, wba_ref, wbs_ref, wo_ref, wglu_ref, bglu_ref,
                       x1_ref, kwin_ref, vwin_ref, kprev, vprev):
    t = pl.program_id(1)

    @pl.when(t == 0)
    def _():
        kprev[...] = jnp.zeros_like(kprev)
        vprev[...] = jnp.zeros_like(vprev)

    x = x_ref[...]
    hn = _rms(x, g1_ref[...]).astype(BF)
    qkv = _dot(hn, win_ref[:, :QKV_WIDTH])
    z = _gelu_tanh(y_ref[...])
    glu = _dot(z.astype(BF), wglu_ref[...]) + bglu_ref[...]
    rc, rs1, rs2 = rc_ref[...], rs1_ref[...], rs2_ref[...]
    scale = HEAD_DIM ** -0.5 * LOG2_E
    q = [(_rope(qkv[:, c * LANES:(c + 1) * LANES], rc, rs1, rs2) * scale).astype(BF)
         for c in range(ATTN_WIDTH // LANES)]
    k = _rope(qkv[:, ATTN_WIDTH:ATTN_WIDTH + KV_WIDTH], rc, rs1, rs2)
    v = qkv[:, ATTN_WIDTH + KV_WIDTH:ATTN_WIDTH + 2 * KV_WIDTH]
    ssm = (z * _sigmoid(glu)).astype(BF)
    ssm_proj = _dot(ssm, wbs_ref[...])

    w = WINDOW
    n_sub = MIX_TB // w
    lane = lax.broadcasted_iota(jnp.int32, (2 * w, LANES), 1)
    lo = lane < HEAD_DIM
    kj = lax.broadcasted_iota(jnp.int32, (2 * w, w), 0)
    qi = lax.broadcasted_iota(jnp.int32, (2 * w, w), 1)
    band = (kj > qi) & (kj <= qi + w)
    first = band & ((kj >= w) | (t > 0))
    bias_band = jnp.where(band, 0.0, NEG_BIG).astype(F32)
    bias_first = jnp.where(first, 0.0, NEG_BIG).astype(F32)
    col = lax.broadcasted_iota(jnp.int32, (1, 2 * w), 1)
    sum_r = lax.broadcasted_iota(jnp.int32, (2 * SUBLANES, 4 * w), 0)
    sum_c = lax.broadcasted_iota(jnp.int32, (2 * SUBLANES, 4 * w), 1)
    sum_rows = (((sum_r == 0) & (sum_c < 2 * w)) | ((sum_r == 1) & (sum_c >= 2 * w))).astype(F32)

    def sink_row(ha, hb):
        return jnp.where(col < w, sinks_ref[ha], sinks_ref[hb]) * LOG2_E

    gate_w = GATE_WIDTH // (n_sub * N_KV_HEADS)
    operands = {}

    def sub_block_operands(sb):
        if sb not in operands:
            cur = slice(sb * w, (sb + 1) * w)
            if sb == 0:
                kcat = jnp.concatenate([kprev[...], k[cur]], axis=0)
                vcat = jnp.concatenate([vprev[...], v[cur]], axis=0)
                bias = bias_first
            else:
                kcat = k[(sb - 1) * w:(sb + 1) * w]
                vcat = v[(sb - 1) * w:(sb + 1) * w]
                bias = bias_band
            k_split = _half_split(kcat, pltpu.roll(kcat, HEAD_DIM, 1), lo)
            k_stack = [jnp.concatenate(k_split[2 * g:2 * g + 2], axis=0) for g in range(N_KV_HEADS)]
            vt = vcat.T
            zero = jnp.zeros((HEAD_DIM, 2 * w), F32)
            v_stack = []
            for g in range(N_KV_HEADS):
                vg = vt[g * HEAD_DIM:(g + 1) * HEAD_DIM]
                v_stack.append(jnp.concatenate(
                    [jnp.concatenate([vg, zero], axis=1), jnp.concatenate([zero, vg], axis=1), sum_rows],
                    axis=0).astype(BF))
            operands[sb] = (k_stack, v_stack, jnp.concatenate([bias, bias], axis=1))
        return operands[sb]

    def scores(sb, grp):
        k_stack, _, _ = sub_block_operands(sb)
        cur = slice(sb * w, (sb + 1) * w)
        qq = jnp.concatenate([q[2 * grp][cur], q[2 * grp + 1][cur]], axis=0)
        return _dot_nt(k_stack[grp], qq)

    def weighted_values(sb, grp, s):
        _, v_stack, bias2 = sub_block_operands(sb)
        p_e, sink_e = _softmax_terms_t(s[:2 * w], bias2, sink_row(4 * grp, 4 * grp + 2))
        p_o, sink_o = _softmax_terms_t(s[2 * w:], bias2, sink_row(4 * grp + 1, 4 * grp + 3))
        d = _dot(v_stack[grp], jnp.concatenate([p_e, p_o], axis=0))
        heads = 2 * HEAD_DIM
        o2 = jnp.concatenate(
            [d[:HEAD_DIM] * (1.0 / (d[heads:heads + 1] + sink_e)),
             d[HEAD_DIM:heads] * (1.0 / (d[heads + 1:heads + 2] + sink_o))], axis=0)
        return [o2[:, :w].T, o2[:, w:].T]

    chains = [(sb, grp) for sb in range(n_sub) for grp in range(N_KV_HEADS)]
    gate_cols, outs = [], []
    s_next = scores(*chains[0])
    for i, (sb, grp) in enumerate(chains):
        s_cur = s_next
        gate_cols.append(_gate_cols(hn, win_ref, bg_ref, slice(i * gate_w, (i + 1) * gate_w)))
        if i + 1 < len(chains):
            s_next = scores(*chains[i + 1])
        outs += weighted_values(sb, grp, s_cur)
    per_sb = 2 * N_KV_HEADS
    attn = jnp.concatenate(
        [jnp.concatenate(outs[sb * per_sb:(sb + 1) * per_sb], axis=1) for sb in range(n_sub)],
        axis=0).astype(BF)
    gates = jnp.concatenate(gate_cols, axis=1)

    kprev[...] = k[MIX_TB - w:]
    vprev[...] = v[MIX_TB - w:]
    kwin_ref[...] = k[MIX_TB - w:]
    vwin_ref[...] = v[MIX_TB - w:]
    x1_ref[...] = _merge_out(x, gates, _dot(attn, wba_ref[...]), ssm_proj, wo_ref)


def _mix_prompt(x, y_ssm, rc, rs1, rs2, sinks, g1, w_in, bg, wba, wbs, wo, wglu, bglu):
    b, t, _ = x.shape
    nblk = t // MIX_TB
    x2 = x.reshape(b * t, D_MODEL)
    row_map = lambda i, j: (i * nblk + j, 0)
    x1, kwin, vwin = pl.pallas_call(
        _mix_prompt_kernel,
        grid=(b, nblk),
        in_specs=[
            pl.BlockSpec(memory_space=pltpu.SMEM),
            pl.BlockSpec((MIX_TB, D_MODEL), row_map),
            pl.BlockSpec((MIX_TB, SSM_WIDTH), row_map),
            pl.BlockSpec((MIX_TB, LANES), lambda i, j: (j, 0)),
            pl.BlockSpec((MIX_TB, LANES), lambda i, j: (j, 0)),
            pl.BlockSpec((MIX_TB, LANES), lambda i, j: (j, 0)),
            _const_spec((1, D_MODEL)),
            _const_spec(w_in.shape),
            _const_spec(bg.shape),
            _const_spec(wba.shape),
            _const_spec(wbs.shape),
            _const_spec(wo.shape),
            _const_spec(wglu.shape),
            _const_spec(bglu.shape),
        ],
        out_specs=[
            pl.BlockSpec((MIX_TB, D_MODEL), row_map),
            pl.BlockSpec((None, WINDOW, KV_WIDTH), lambda i, j: (i, 0, 0)),
            pl.BlockSpec((None, WINDOW, KV_WIDTH), lambda i, j: (i, 0, 0)),
        ],
        out_shape=[
            jax.ShapeDtypeStruct((b * t, D_MODEL), F32),
            jax.ShapeDtypeStruct((b, WINDOW, KV_WIDTH), F32),
            jax.ShapeDtypeStruct((b, WINDOW, KV_WIDTH), F32),
        ],
        scratch_shapes=[pltpu.VMEM((WINDOW, KV_WIDTH), F32), pltpu.VMEM((WINDOW, KV_WIDTH), F32)],
        compiler_params=pltpu.CompilerParams(
            dimension_semantics=("arbitrary", "arbitrary"), vmem_limit_bytes=VMEM_LIMIT),
        name="mix_prompt",
    )(sinks, x2, y_ssm, rc, rs1, rs2, g1, w_in, bg, wba, wbs, wo, wglu, bglu)
    return x1, kwin, vwin


QROWS = 16
SAMPLE_TB = 32


def _sample_layer_kernel(x_ref, kbuf_ref, vbuf_ref, rc_ref, rs1_ref, rs2_ref, sinkc_ref, g1_ref, wqkv_ref,
                         xall_ref, h0re_ref, h0im_ref, bg_ref, lb_ref, d_ref, wglu_ref, bglu_ref,
                         win_hbm, wbad_hbm, wbs_hbm, wo_hbm, bblk_hbm, ctblk_hbm,
                         kout_ref, vout_ref, x1_ref, hre_ref, him_ref,
                         qz, o3_all, win_s, wbad_s, wbs_s, wo_s, bblk_s, ctblk_s, sem):
    nb = x_ref.shape[0]
    step = pl.program_id(0)
    last_step = pl.num_programs(0) - 1
    late = ((win_hbm, win_s), (wbad_hbm, wbad_s), (wbs_hbm, wbs_s), (wo_hbm, wo_s), (bblk_hbm, bblk_s),
            (ctblk_hbm, ctblk_s))
    late_copies = [pltpu.make_async_copy(src, dst, sem.at[i]) for i, (src, dst) in enumerate(late)]

    @pl.when(step == 0)
    def _():
        for copy in late_copies:
            copy.start()

    hn = _rms(x_ref[...], g1_ref[...]).astype(BF)
    qkv = _dot(hn, wqkv_ref[...])
    rc, rs1, rs2 = rc_ref[...], rs1_ref[...], rs2_ref[...]
    scale = HEAD_DIM ** -0.5
    k_new = _rope(qkv[:, ATTN_WIDTH:ATTN_WIDTH + KV_WIDTH], rc, rs1, rs2)
    v_new = qkv[:, ATTN_WIDTH + KV_WIDTH:ATTN_WIDTH + 2 * KV_WIDTH]
    pad = jnp.zeros((LANES - nb, KV_WIDTH), F32)
    k_new_t = jnp.concatenate([k_new, pad], axis=0).T
    v_new_t = jnp.concatenate([v_new, pad], axis=0).T

    lane = lax.broadcasted_iota(jnp.int32, (nb, LANES), 1)
    lo = lane < HEAD_DIM
    qz[...] = jnp.zeros_like(qz)
    for c in range(ATTN_WIDTH // LANES):
        qc = _rope(qkv[:, c * LANES:(c + 1) * LANES], rc, rs1, rs2) * scale
        qr = pltpu.roll(qc, HEAD_DIM, 1)
        zero = jnp.zeros_like(qc)
        if c < 2:
            even, odd = jnp.where(lo, qc, zero), jnp.where(lo, qr, zero)
        else:
            even, odd = jnp.where(lo, zero, qr), jnp.where(lo, zero, qc)
        qz[pl.ds(2 * c, nb, stride=QROWS), :] = even
        qz[pl.ds(2 * c + 1, nb, stride=QROWS), :] = odd

    last = lax.broadcasted_iota(jnp.int32, (KV_WIDTH, WINDOW), 1) == WINDOW - 1
    for b in range(nb):
        kout_ref[b] = jnp.where(last, k_new_t[:, b:b + 1], pltpu.roll(kbuf_ref[b], WINDOW - 1, 1))
        vout_ref[b] = jnp.where(last, v_new_t[:, b:b + 1], pltpu.roll(vbuf_ref[b], WINDOW - 1, 1))

    sink = sinkc_ref[...]
    q3 = qz[...].reshape(nb, QROWS, LANES).astype(BF)
    s = jnp.einsum('bhd,bdk->bhk', q3, kout_ref[...].astype(BF), preferred_element_type=F32)
    m = jnp.maximum(jnp.max(s, axis=-1, keepdims=True), sink)
    p = jnp.exp(s - m)
    den = jnp.sum(p, axis=-1, keepdims=True) + jnp.exp(sink - m)
    p = (p * (1.0 / den)).astype(BF)
    o3 = jnp.einsum('bhk,bdk->bhd', p, vout_ref[...].astype(BF), preferred_element_type=F32)
    o3_all[pl.ds(pl.multiple_of(step * (nb * QROWS), nb * QROWS), nb * QROWS), :] = o3.reshape(nb * QROWS, LANES)

    @pl.when(step == last_step)
    def _():
        for copy in late_copies:
            copy.wait()
        _sample_tail(xall_ref, o3_all, h0re_ref, h0im_ref, g1_ref, win_s, bg_ref, wbad_s, wbs_s, wo_s,
                     lb_ref, bblk_s, ctblk_s, d_ref, wglu_ref, bglu_ref, x1_ref, hre_ref, him_ref)


def _sample_layer(x, kbuf, vbuf, h0re, h0im, rc, rs1, rs2, sinkc, g1, w_in, bg, wbad, wbs, wo, lb, bblk, ctblk,
                  d, wglu, bglu):
    nb = x.shape[0]
    tb = SAMPLE_TB
    blocked = (x, kbuf, vbuf)
    early = (rc, rs1, rs2, sinkc, g1)
    consts = (x, h0re, h0im, bg, lb, d, wglu, bglu)
    late = (w_in, wbad, wbs, wo, bblk, ctblk)
    kv_spec = pl.BlockSpec((tb, KV_WIDTH, WINDOW), lambda i: (i, 0, 0))
    whole = lambda shape: pl.BlockSpec(shape, lambda i: (0,) * len(shape))
    return pl.pallas_call(
        _sample_layer_kernel,
        grid=(nb // tb,),
        in_specs=[pl.BlockSpec((tb, D_MODEL), lambda i: (i, 0)), kv_spec, kv_spec]
        + [_const_spec(a.shape) for a in early]
        + [pl.BlockSpec((D_MODEL, QKV_WIDTH), lambda i: (0, 0), pipeline_mode=pl.Buffered(1))]
        + [_const_spec(a.shape) for a in consts]
        + [pl.BlockSpec(memory_space=pl.ANY) for _ in late],
        out_specs=[kv_spec, kv_spec, whole((nb, D_MODEL)), whole((N_STATES, nb)), whole((N_STATES, nb))],
        out_shape=[
            jax.ShapeDtypeStruct((nb, KV_WIDTH, WINDOW), F32),
            jax.ShapeDtypeStruct((nb, KV_WIDTH, WINDOW), F32),
            jax.ShapeDtypeStruct((nb, D_MODEL), F32),
            jax.ShapeDtypeStruct((N_STATES, nb), F32),
            jax.ShapeDtypeStruct((N_STATES, nb), F32),
        ],
        scratch_shapes=[pltpu.VMEM((tb * QROWS, LANES), F32), pltpu.VMEM((nb * QROWS, LANES), F32)]
        + [pltpu.VMEM(a.shape, a.dtype) for a in late] + [pltpu.SemaphoreType.DMA((len(late),))],
        compiler_params=pltpu.CompilerParams(
            dimension_semantics=("arbitrary",), vmem_limit_bytes=VMEM_LIMIT),
        name="sample_layer",
    )(*blocked, *early, w_in, *consts, *late)


def _sample_tail(x_ref, o3_ref, h0re_ref, h0im_ref, g1_ref, win_ref, bg_ref, wbad_ref,
                 wbs_ref, wo_ref, lb_ref, bblk_ref, ctblk_ref, d_ref, wglu_ref, bglu_ref,
                 x1_ref, hre_ref, him_ref):
    nb = x_ref.shape[0]
    x = x_ref[...]
    hn = _rms(x, g1_ref[...]).astype(BF)

    lane = lax.broadcasted_iota(jnp.int32, (nb, LANES), 1)
    lo = lane < HEAD_DIM
    a = jnp.zeros((nb, D_MODEL), F32)
    zero = jnp.zeros((nb, LANES), F32)
    for h in range(N_Q_HEADS):
        oh = o3_ref[pl.ds(h, nb, stride=QROWS), :]
        oh = jnp.where(lo, oh, zero) if h < N_Q_HEADS // 2 else jnp.where(lo, zero, oh)
        a = a + _dot(oh.astype(BF), wbad_ref[h])

    u = _dot(hn, win_ref[:, U_COL0:GATE_COL0])
    ub = u.astype(BF)
    lre, lim = lb_ref[0:1, :], lb_ref[1:2, :]
    y_cols = []
    for o in range(N_OCT):
        sl = slice(o * OCT_STATES, (o + 1) * OCT_STATES)
        bu = _dot(ub[:, o * LANES:(o + 1) * LANES], bblk_ref[o])
        blocks = [slice(c * LANES, (c + 1) * LANES) for c in range(o * OCT_COL, (o + 1) * OCT_COL)]
        h0r = jnp.concatenate([h0re_ref[rows, :].T for rows in blocks], axis=1)
        h0i = jnp.concatenate([h0im_ref[rows, :].T for rows in blocks], axis=1)
        hr = bu[:, :OCT_STATES] + (lre[:, sl] * h0r - lim[:, sl] * h0i)
        hi = bu[:, OCT_STATES:] + (lre[:, sl] * h0i + lim[:, sl] * h0r)
        for cc, rows in enumerate(blocks):
            hre_ref[rows, :] = hr[:, cc * LANES:(cc + 1) * LANES].T
            him_ref[rows, :] = hi[:, cc * LANES:(cc + 1) * LANES].T
        y_cols.append(_dot_nt(jnp.concatenate([hr, hi], axis=1).astype(BF), ctblk_ref[o]))
    y = jnp.concatenate(y_cols, axis=1) + d_ref[...] * u
    z = _gelu_tanh(y)
    gate = _dot(z.astype(BF), wglu_ref[...]) + bglu_ref[...]
    ssm = (z * _sigmoid(gate)).astype(BF)

    gates = _gate_cols(hn, win_ref, bg_ref, slice(0, GATE_WIDTH))
    x1_ref[...] = _merge_out(x, gates, a, _dot(ssm, wbs_ref[...]), wo_ref)


FFN_W_CHUNKS = 16
FFN_W_SLOTS = 4


def _stream_cast_weights(jobs):
    tasks, used = [], {}
    for w_hbm, w_bf, stage, sem in jobs:
        rows = stage.shape[1]
        for k in range(w_hbm.shape[0] // rows):
            slot = used.get(id(stage), 0) % stage.shape[0]
            used[id(stage)] = used.get(id(stage), 0) + 1
            copy = pltpu.make_async_copy(w_hbm.at[pl.ds(k * rows, rows), :], stage.at[slot], sem.at[slot])
            tasks.append((copy, w_bf, stage, slot, k * rows, rows))
    ahead = min(stage.shape[0] for _, _, stage, _ in jobs) - 1
    for copy, *_ in tasks[:ahead]:
        copy.start()
    for i, (copy, w_bf, stage, slot, row0, rows) in enumerate(tasks):
        if i + ahead < len(tasks):
            tasks[i + ahead][0].start()
        copy.wait()
        w_bf[row0:row0 + rows, :] = stage[slot].astype(BF)


def _ffn_rows(x, g2_ref, wgate, wup, wdown, gf_ref):
    h = _rms(x, g2_ref[...]).astype(BF)
    gate = _dot(h, wgate[...])
    up = _dot(h, wup[...])
    half_gate = 0.5 * gate
    act = ((half_gate + half_gate * jnp.tanh(half_gate)) * up).astype(BF)
    x2 = x + _dot(act, wdown[...])
    return _rms(x2, gf_ref[...])


def _ffn_kernel(xp_ref, xs_ref, g2_ref, wgate_hbm, wup_hbm, wdown_hbm, gf_ref, yp_ref, ys_ref,
                wgate, wup, wdown, stage_in, stage_out, sem_in, sem_out):
    i = pl.program_id(0)
    n_prompt = pl.num_programs(0) - 1

    @pl.when(i == 0)
    def _():
        _stream_cast_weights([(wgate_hbm, wgate, stage_in, sem_in), (wup_hbm, wup, stage_in, sem_in),
                              (wdown_hbm, wdown, stage_out, sem_out)])

    @pl.when(i < n_prompt)
    def _():
        half = FFN_TB // FFN_SPLIT
        for rows in (slice(h * half, (h + 1) * half) for h in range(FFN_SPLIT)):
            yp_ref[rows, :] = _ffn_rows(xp_ref[rows, :], g2_ref, wgate, wup, wdown, gf_ref)

    @pl.when(i == n_prompt)
    def _():
        ys_ref[...] = _ffn_rows(xs_ref[...], g2_ref, wgate, wup, wdown, gf_ref)


def _ffn(xp, xs, g2, wgate, wup, wdown, gf):
    n, ns = xp.shape[0], xs.shape[0]
    n_prompt = n // FFN_TB
    prompt_map = lambda i: (jnp.minimum(i, n_prompt - 1), 0)
    return pl.pallas_call(
        _ffn_kernel,
        grid=(n_prompt + 1,),
        in_specs=[
            pl.BlockSpec((FFN_TB, D_MODEL), prompt_map),
            _const_spec((ns, D_MODEL)),
            _const_spec((1, D_MODEL)),
            pl.BlockSpec(memory_space=pl.ANY),
            pl.BlockSpec(memory_space=pl.ANY),
            pl.BlockSpec(memory_space=pl.ANY),
            _const_spec((1, D_MODEL)),
        ],
        out_specs=[
            pl.BlockSpec((FFN_TB, D_MODEL), prompt_map),
            pl.BlockSpec((ns, D_MODEL), lambda i: (0, 0)),
        ],
        out_shape=[
            jax.ShapeDtypeStruct((n, D_MODEL), F32),
            jax.ShapeDtypeStruct((ns, D_MODEL), F32),
        ],
        scratch_shapes=[
            pltpu.VMEM((D_MODEL, D_FF), BF), pltpu.VMEM((D_MODEL, D_FF), BF), pltpu.VMEM((D_FF, D_MODEL), BF),
            pltpu.VMEM((FFN_W_SLOTS, D_MODEL // FFN_W_CHUNKS, D_FF), F32),
            pltpu.VMEM((FFN_W_SLOTS, D_FF // FFN_W_CHUNKS, D_MODEL), F32),
            pltpu.SemaphoreType.DMA((FFN_W_SLOTS,)), pltpu.SemaphoreType.DMA((FFN_W_SLOTS,)),
        ],
        compiler_params=pltpu.CompilerParams(
            dimension_semantics=("arbitrary",), vmem_limit_bytes=VMEM_LIMIT),
        name="ffn",
    )(xp, xs, g2, wgate, wup, wdown, gf)


def _rope_tables(pos):
    pos = np.asarray(pos, np.float64)
    inv_freq = ROPE_THETA ** (-(np.arange(ROPE_HALF, dtype=np.float64) * 2.0 / ROPE_DIM))
    ang = pos[:, None] * inv_freq[None, :]
    cos, sin = np.cos(ang), np.sin(ang)
    pad = np.zeros((pos.shape[0], HEAD_DIM - ROPE_DIM))
    zero = np.zeros_like(sin)
    rc = np.concatenate([cos, cos, pad + 1.0], axis=1)
    rs1 = np.concatenate([zero, sin, pad], axis=1)
    rs2 = np.concatenate([-sin, zero, pad], axis=1)
    rep = LANES // HEAD_DIM
    return tuple(jnp.asarray(np.tile(a, (1, rep)), F32) for a in (rc, rs1, rs2))


def _cmul(ar, ai, br, bi):
    return ar * br - ai * bi, ar * bi + ai * br


def _ssm_tables(lam_re, lam_im, log_dt, b_re, b_im, c_re, c_im):
    dt = jnp.exp(log_dt)[:, None]
    mag = jnp.exp(lam_re * dt)
    lb_re = mag * jnp.cos(lam_im * dt)
    lb_im = mag * jnp.sin(lam_im * dt)
    den = lam_re * lam_re + lam_im * lam_im
    nr = lb_re - 1.0
    k_re = ((nr * lam_re + lb_im * lam_im) / den)[..., None]
    k_im = ((lb_im * lam_re - nr * lam_im) / den)[..., None]
    bb_re = k_re * b_re - k_im * b_im
    bb_im = k_re * b_im + k_im * b_re

    a_re, a_im = lb_re, lb_im
    for _ in range(int(math.log2(CHUNK))):
        a_re, a_im = _cmul(a_re, a_im, a_re, a_im)
    s_re, s_im = a_re, a_im
    for _ in range(int(math.log2(SEG))):
        s_re, s_im = _cmul(s_re, s_im, s_re, s_im)

    eye = jnp.eye(OCT, dtype=F32).reshape(1, OCT, 1, OCT, 1)

    def block_diag(a):
        r, c = a.shape[1:]
        return (a.reshape(N_OCT, OCT, r, 1, c) * eye).reshape(N_OCT, OCT * r, OCT * c)

    bblk = jnp.concatenate([block_diag(jnp.swapaxes(bb_re, 1, 2)),
                            block_diag(jnp.swapaxes(bb_im, 1, 2))], axis=2)
    ctblk = jnp.concatenate([block_diag(c_re), block_diag(-c_im)], axis=2)

    oct_cols = lambda a: a.reshape(N_OCT, 1, OCT_STATES)
    lcol = jnp.concatenate([oct_cols(lb_re), oct_cols(lb_im),
                            jnp.zeros((N_OCT, SUBLANES - 2, OCT_STATES), F32)], axis=1)

    flat = lambda a: a.reshape(1, N_STATES)
    col = lambda a: a.reshape(N_COL, 1, LANES)
    a_tab = jnp.concatenate([col(a_re), col(a_im), col(s_re), col(s_im),
                             jnp.zeros((N_COL, SUBLANES - 4, LANES), F32)], axis=1)
    lb = jnp.concatenate([flat(lb_re), flat(lb_im)], axis=0)
    return bblk, ctblk, lcol, a_tab, lb


def kernel(x_prompt, x_sample, state_k_win, state_v_win, state_ssm_re, state_ssm_im, norm1_g, w_in, b_gate, attn_sinks, ssm_lam_re, ssm_lam_im, ssm_log_dt, ssm_b_re, ssm_b_im, ssm_c_re, ssm_c_im, ssm_d, w_glu, b_glu, w_branch_attn, w_branch_ssm, w_out, norm2_g, w_ffn_gate, w_ffn_up, w_ffn_down, norm_f_g):
    depth = w_in.shape[0]
    assert depth == 1
    b, t, _ = x_prompt.shape
    nb, s_len, _ = x_sample.shape
    assert s_len == 1 and state_k_win.shape[2] == WINDOW
    l = 0
    assert w_in.shape[2] == IN_WIDTH
    w_in_b = w_in[l].astype(BF)
    g1 = norm1_g[l].reshape(1, D_MODEL)
    g2 = norm2_g[l].reshape(1, D_MODEL)
    gf = norm_f_g.reshape(1, D_MODEL)
    bg = b_gate[l].reshape(1, GATE_WIDTH)
    d = ssm_d[l].reshape(1, SSM_WIDTH)
    wglu = w_glu[l].astype(BF)
    bglu = b_glu[l].reshape(1, SSM_WIDTH)
    wba = w_branch_attn[l].astype(BF)
    wbs = w_branch_ssm[l].astype(BF)
    wo = w_out[l].astype(BF)
    sinks = attn_sinks[l]

    bblk, ctblk, lcol, a_tab, lb = _ssm_tables(
        ssm_lam_re[l], ssm_lam_im[l], ssm_log_dt[l], ssm_b_re[l], ssm_b_im[l], ssm_c_re[l], ssm_c_im[l])

    rc, rs1, rs2 = _rope_tables(np.arange(t))
    yssm_p, hre_p, him_p = _s5_prompt(x_prompt, g1, w_in_b, bblk, ctblk, lcol, a_tab, d)
    x1_p, kwin_p, vwin_p = _mix_prompt(x_prompt, yssm_p, rc, rs1, rs2, sinks, g1, w_in_b, bg, wba, wbs, wo,
                                       wglu, bglu)

    rcs, rs1s, rs2s = _rope_tables(PAST_LEN + np.arange(1))
    sinkc = jnp.concatenate([sinks, jnp.zeros((QROWS - N_Q_HEADS,), F32)]).reshape(QROWS, 1)
    wbad = jnp.concatenate([wba.reshape(N_Q_HEADS, HEAD_DIM, D_MODEL)] * 2, axis=1)
    xs = x_sample.reshape(nb, D_MODEL)
    key_minor = lambda a: jnp.swapaxes(a.reshape(nb, WINDOW, KV_WIDTH), 1, 2)
    kwin_s, vwin_s, x1_s, hre_s, him_s = _sample_layer(
        xs, key_minor(state_k_win[l]), key_minor(state_v_win[l]),
        state_ssm_re[l].reshape(nb, N_STATES).T, state_ssm_im[l].reshape(nb, N_STATES).T,
        rcs, rs1s, rs2s, sinkc, g1, w_in_b, bg, wbad, wbs, wo, lb, bblk.astype(BF), ctblk.astype(BF),
        d, wglu, bglu)
    y_p, y_s = _ffn(x1_p, x1_s, g2, w_ffn_gate[l], w_ffn_up[l], w_ffn_down[l], gf)
    y_p = y_p.reshape(b, t, D_MODEL)
    y_s = y_s.reshape(nb, 1, D_MODEL)

    kv_shape_p = (1, b, WINDOW, N_KV_HEADS, HEAD_DIM)
    st_shape_p = (1, b, N_SSM_GROUPS, SSM_STATE)
    kv_shape_s = (1, nb, WINDOW, N_KV_HEADS, HEAD_DIM)
    st_shape_s = (1, nb, N_SSM_GROUPS, SSM_STATE)
    return (y_p, y_s,
            kwin_p.reshape(kv_shape_p), vwin_p.reshape(kv_shape_p),
            hre_p.reshape(st_shape_p), him_p.reshape(st_shape_p),
            jnp.swapaxes(kwin_s, 1, 2).reshape(kv_shape_s), jnp.swapaxes(vwin_s, 1, 2).reshape(kv_shape_s),
            hre_s.T.reshape(st_shape_s), him_s.T.reshape(st_shape_s))
```

```python
import math

import jax
import jax.numpy as jnp
import numpy as np
from jax import lax
from jax.experimental import pallas as pl
from jax.experimental.pallas import tpu as pltpu

D_MODEL = 1024
N_Q_HEADS = 8
N_KV_HEADS = 2
HEAD_DIM = 64
ATTN_WIDTH = N_Q_HEADS * HEAD_DIM
KV_WIDTH = N_KV_HEADS * HEAD_DIM
WINDOW = 128
ROPE_DIM = HEAD_DIM // 4
ROPE_HALF = ROPE_DIM // 2
ROPE_THETA = 500000.0
SSM_WIDTH = D_MODEL // 2
SSM_GROUP = 16
N_SSM_GROUPS = SSM_WIDTH // SSM_GROUP
SSM_STATE = 64
N_STATES = N_SSM_GROUPS * SSM_STATE
GATE_WIDTH = 2 * D_MODEL
QKV_WIDTH = ATTN_WIDTH + 2 * KV_WIDTH
U_COL0 = QKV_WIDTH
GATE_COL0 = U_COL0 + SSM_WIDTH
IN_WIDTH = GATE_COL0 + GATE_WIDTH
D_FF = -(-8 * D_MODEL // (3 * 256)) * 256
NORM_EPS = 1e-5
PAST_LEN = 8192

LANES = 128
SUBLANES = 8
CHUNK = 8
OCT = LANES // SSM_GROUP
N_OCT = N_SSM_GROUPS // OCT
OCT_STATES = OCT * SSM_STATE
OCT_COL = OCT_STATES // LANES
N_COL = N_STATES // LANES
N_SEG = SUBLANES
SEG = 16
S5_ROWS = N_SEG * SEG
M_PAIR = 2
U_HALF = SSM_WIDTH // 2
SEG_PITCH = SEG * CHUNK + SUBLANES
MIX_TB = 512
FFN_TB = 1024
FFN_SPLIT = 4
NEG_BIG = -1e30
LOG2_E = math.log2(math.e)
VMEM_LIMIT = 56 * 1024 * 1024

BF = jnp.bfloat16
F32 = jnp.float32


def _dot(a, b):
    return jnp.dot(a, b, preferred_element_type=F32)


def _dot_nt(a, b):
    return lax.dot_general(a, b, (((1,), (1,)), ((), ())), preferred_element_type=F32)


def _dot_nt_split(a, b):
    a_hi, b_hi = a.astype(BF), b.astype(BF)
    a_lo = (a - a_hi.astype(F32)).astype(BF)
    b_lo = (b - b_hi.astype(F32)).astype(BF)
    return _dot_nt(a_hi, b_hi) + (_dot_nt(a_hi, b_lo) + _dot_nt(a_lo, b_hi))


def _rms(x, g):
    return x * lax.rsqrt(jnp.mean(x * x, axis=-1, keepdims=True) + NORM_EPS) * g


def _sigmoid(x):
    return 1.0 / (1.0 + jnp.exp(-x))


def _gelu_tanh(x):
    c = math.sqrt(2.0 / math.pi)
    return 0.5 * x * (1.0 + jnp.tanh(c * (x + 0.044715 * (x * x * x))))


def _rope(a, rc, rs1, rs2):
    return a * rc + pltpu.roll(a, ROPE_HALF, 1) * rs1 + pltpu.roll(a, LANES - ROPE_HALF, 1) * rs2


def _const_spec(shape):
    nd = len(shape)
    return pl.BlockSpec(shape, lambda *_: (0,) * nd, pipeline_mode=pl.Buffered(1))


def _build_chunk_operators(bblk_ref, ctblk_ref, lcol_ref, m_s, e_s, f_s):
    for o in range(N_OCT):
        ct = ctblk_ref[o]
        lr, li = lcol_ref[o, 0:1, :], lcol_ref[o, 1:2, :]
        er, ei = bblk_ref[o, :, :OCT_STATES], bblk_ref[o, :, OCT_STATES:]
        k_blk = []
        for tau in range(CHUNK):
            e_cat = jnp.concatenate([er, ei], axis=1)
            i = CHUNK - 1 - tau
            e_s[o, i * LANES:(i + 1) * LANES, :] = e_cat.astype(BF)
            k_blk.append(_dot_nt_split(e_cat, ct).astype(BF))
            er, ei = er * lr - ei * li, er * li + ei * lr
        zero = jnp.zeros((LANES, LANES), BF)
        for j in range(CHUNK):
            jt, jj = divmod(j, M_PAIR)
            for i in range(M_PAIR * (jt + 1)):
                m_s[jt][o, i * LANES:(i + 1) * LANES, jj * LANES:(jj + 1) * LANES] = (
                    k_blk[j - i] if j >= i else zero)
        tr, ti = ct[:, :OCT_STATES], -ct[:, OCT_STATES:]
        for j in range(CHUNK):
            tr, ti = tr * lr - ti * li, tr * li + ti * lr
            f_s[o, :OCT_STATES, j * LANES:(j + 1) * LANES] = tr.T.astype(BF)
            f_s[o, OCT_STATES:, j * LANES:(j + 1) * LANES] = (-ti).T.astype(BF)


def _s5_prompt_kernel(x_ref, g1_ref, wu0_ref, wu1_ref, bblk_ref, ctblk_ref, lcol_ref, a_ref, d_ref,
                      out_ref, hfin_ref,
                      m0_ref, m1_ref, m2_ref, m3_ref, e_ref, f_ref, us, ys, sre, sim, car):
    m_ref = (m0_ref, m1_ref, m2_ref, m3_ref)
    blk = pl.program_id(1)

    @pl.when((pl.program_id(0) == 0) & (blk == 0))
    def _():
        _build_chunk_operators(bblk_ref, ctblk_ref, lcol_ref, m_ref, e_ref, f_ref)

    @pl.when(blk == 0)
    def _():
        car[...] = jnp.zeros_like(car)

    seg_tokens = SEG * CHUNK
    seg_rows = [slice(s * SEG_PITCH, s * SEG_PITCH + seg_tokens) for s in range(N_SEG)]
    tok_rows = [slice(s * seg_tokens, (s + 1) * seg_tokens) for s in range(N_SEG)]

    for s in range(0, N_SEG, 2):
        hn = _rms(x_ref[s * seg_tokens:(s + 2) * seg_tokens, :], g1_ref[...]).astype(BF)
        u = jnp.concatenate([_dot(hn, wu0_ref[...].astype(BF)), _dot(hn, wu1_ref[...].astype(BF))],
                            axis=1)
        for half in range(2):
            for cc in range(N_OCT):
                us[cc, seg_rows[s + half], :] = u[tok_rows[half], cc * LANES:(cc + 1) * LANES]

    sub = lax.broadcasted_iota(jnp.int32, (N_SEG, LANES), 0)

    def scan_column(c):
        tab = a_ref[c]
        are, aim = tab[0:1], tab[1:2]
        bre, bim = tab[2:3], tab[3:4]

        def step(cr, ci, r, keep_entering):
            slab = slice(r * N_SEG, (r + 1) * N_SEG)
            s_r, s_i = sre[c, slab, :], sim[c, slab, :]
            if keep_entering:
                sre[c, slab, :] = cr
                sim[c, slab, :] = ci
            return are * cr - aim * ci + s_r, are * ci + aim * cr + s_i

        cr = jnp.zeros((N_SEG, LANES), F32)
        ci = jnp.zeros((N_SEG, LANES), F32)
        for r in range(SEG):
            cr, ci = step(cr, ci, r, False)
        cv = car[c]
        pr, pi = cv[0:1], cv[1:2]
        sr = jnp.zeros((N_SEG, LANES), F32)
        si = jnp.zeros((N_SEG, LANES), F32)
        for s in range(N_SEG):
            sr = jnp.where(sub == s, pr, sr)
            si = jnp.where(sub == s, pi, si)
            pr, pi = (bre * pr - bim * pi + cr[s:s + 1], bre * pi + bim * pr + ci[s:s + 1])
        end = jnp.where(sub == 0, pr, jnp.where(sub == 1, pi, 0.0))
        car[c] = end
        hfin_ref[c] = end
        cr, ci = sr, si
        for r in range(SEG):
            cr, ci = step(cr, ci, r, True)

    for o in range(N_OCT):
        uo = jnp.concatenate(
            [jnp.concatenate([us[o, pl.ds(r * CHUNK + i, N_SEG, stride=SEG_PITCH), :] for r in range(SEG)],
                             axis=0).astype(BF) for i in range(CHUNK)], axis=1)
        s_end = _dot(uo, e_ref[o])
        cols = range(o * OCT_COL, (o + 1) * OCT_COL)
        for cc, c in enumerate(cols):
            sre[c] = s_end[:, cc * LANES:(cc + 1) * LANES]
            sim[c] = s_end[:, OCT_STATES + cc * LANES:OCT_STATES + (cc + 1) * LANES]
        y_in = jnp.concatenate(
            [_dot(uo[:, :(jt + 1) * M_PAIR * LANES], m_ref[jt][o]) for jt in range(CHUNK // M_PAIR)],
            axis=1)
        for c in cols:
            scan_column(c)
        hp = jnp.concatenate([sre[c] for c in cols] + [sim[c] for c in cols], axis=1).astype(BF)
        yo = y_in + _dot(hp, f_ref[o])
        for r in range(SEG):
            for j in range(CHUNK):
                ys[o, pl.ds(r * CHUNK + j, N_SEG, stride=SEG_PITCH), :] = (
                    yo[r * N_SEG:(r + 1) * N_SEG, j * LANES:(j + 1) * LANES])
        d_o = d_ref[:, o * LANES:(o + 1) * LANES]
        for s in range(N_SEG):
            out_ref[tok_rows[s], o * LANES:(o + 1) * LANES] = (
                ys[o, seg_rows[s], :] + d_o * us[o, seg_rows[s], :])


def _s5_prompt(x, g1, w_in, bblk, ctblk, lcol, a_tab, d):
    b, t, _ = x.shape
    tb = S5_ROWS * CHUNK
    nblk = t // tb
    x2 = x.reshape(b * t, D_MODEL)
    row_map = lambda i, j: (i * nblk + j, 0)
    op_shape = (N_OCT, CHUNK * LANES, CHUNK * LANES)
    out, hfin = pl.pallas_call(
        _s5_prompt_kernel,
        grid=(b, nblk),
        in_specs=[
            pl.BlockSpec((tb, D_MODEL), row_map),
            _const_spec((1, D_MODEL)),
            *[pl.BlockSpec((D_MODEL, U_HALF), lambda i, j, c=U_COL0 // U_HALF + h: (0, c),
                           pipeline_mode=pl.Buffered(1)) for h in range(2)],
            _const_spec(bblk.shape),
            _const_spec(ctblk.shape),
            _const_spec(lcol.shape),
            _const_spec(a_tab.shape),
            _const_spec((1, SSM_WIDTH)),
        ],
        out_specs=[
            pl.BlockSpec((tb, SSM_WIDTH), row_map),
            pl.BlockSpec((None, N_COL, SUBLANES, LANES), lambda i, j: (i, 0, 0, 0)),
        ],
        out_shape=[
            jax.ShapeDtypeStruct((b * t, SSM_WIDTH), F32),
            jax.ShapeDtypeStruct((b, N_COL, SUBLANES, LANES), F32),
        ],
        scratch_shapes=[
            *[pltpu.VMEM((N_OCT, (jt + 1) * M_PAIR * LANES, M_PAIR * LANES), BF)
              for jt in range(CHUNK // M_PAIR)],
            pltpu.VMEM(op_shape, BF), pltpu.VMEM(op_shape, BF),
            pltpu.VMEM((N_OCT, N_SEG * SEG_PITCH, LANES), F32),
            pltpu.VMEM((N_OCT, N_SEG * SEG_PITCH, LANES), F32),
            pltpu.VMEM((N_COL, S5_ROWS, LANES), F32), pltpu.VMEM((N_COL, S5_ROWS, LANES), F32),
            pltpu.VMEM((N_COL, SUBLANES, LANES), F32),
        ],
        compiler_params=pltpu.CompilerParams(
            dimension_semantics=("arbitrary", "arbitrary"), vmem_limit_bytes=VMEM_LIMIT),
        name="s5_prompt",
    )(x2, g1, w_in, w_in, bblk, ctblk, lcol, a_tab, d)
    return out, hfin[:, :, 0, :], hfin[:, :, 1, :]


def _gate_cols(hn, win_ref, bg_ref, cols):
    w_cols = slice(GATE_COL0 + cols.start, GATE_COL0 + cols.stop)
    return _sigmoid(_dot(hn, win_ref[:, w_cols]) + bg_ref[:, cols])


def _merge_out(x, gates, attn_proj, ssm_proj, wo_ref):
    merged = gates[:, :D_MODEL] * attn_proj + gates[:, D_MODEL:] * ssm_proj
    return x + _dot(merged.astype(BF), wo_ref[...])


def _half_split(a, ar, lo):
    z = jnp.zeros_like(a)
    return (jnp.where(lo, a, z).astype(BF), jnp.where(lo, z, ar).astype(BF),
            jnp.where(lo, ar, z).astype(BF), jnp.where(lo, z, a).astype(BF))


def _softmax_terms_t(st, bias_t, sink):
    st = st + bias_t
    m = jnp.maximum(jnp.max(st, axis=0, keepdims=True), sink)
    return jnp.exp2(st - m).astype(BF), jnp.exp2(sink - m)


def _mix_prompt_kernel(sinks_ref, x_ref, y_ref, rc_ref, rs1_ref, rs2_ref, g1_ref, win_hbm,
                       bg_ref, wba_ref, wbs_ref, wo_ref, wglu_ref, bglu_ref,
                       x1_ref, kwin_ref, vwin_ref, winb_hbm, kprev, vprev, win_ref, stage, sem_stage, sem_out):
    t = pl.program_id(1)
    seq = pl.program_id(0)
    share_copy = pltpu.make_async_copy(win_ref, winb_hbm, sem_out.at[0])

    @pl.when((seq == 0) & (t == 0))
    def _():
        _stream_cast_weights([(win_hbm, win_ref, stage, sem_stage)])
        share_copy.start()

    @pl.when((seq == pl.num_programs(0) - 1) & (t == pl.num_programs(1) - 1))
    def _():
        share_copy.wait()

    @pl.when(t == 0)
    def _():
        kprev[...] = jnp.zeros_like(kprev)
        vprev[...] = jnp.zeros_like(vprev)

    x = x_ref[...]
    hn = _rms(x, g1_ref[...]).astype(BF)
    qkv = _dot(hn, win_ref[:, :QKV_WIDTH])
    z = _gelu_tanh(y_ref[...])
    glu = _dot(z.astype(BF), wglu_ref[...]) + bglu_ref[...]
    rc, rs1, rs2 = rc_ref[...], rs1_ref[...], rs2_ref[...]
    scale = HEAD_DIM ** -0.5 * LOG2_E
    q = [(_rope(qkv[:, c * LANES:(c + 1) * LANES], rc, rs1, rs2) * scale).astype(BF)
         for c in range(ATTN_WIDTH // LANES)]
    k = _rope(qkv[:, ATTN_WIDTH:ATTN_WIDTH + KV_WIDTH], rc, rs1, rs2)
    v = qkv[:, ATTN_WIDTH + KV_WIDTH:ATTN_WIDTH + 2 * KV_WIDTH]
    ssm = (z * _sigmoid(glu)).astype(BF)
    ssm_proj = _dot(ssm, wbs_ref[...])

    w = WINDOW
    n_sub = MIX_TB // w
    lane = lax.broadcasted_iota(jnp.int32, (2 * w, LANES), 1)
    lo = lane < HEAD_DIM
    kj = lax.broadcasted_iota(jnp.int32, (2 * w, w), 0)
    qi = lax.broadcasted_iota(jnp.int32, (2 * w, w), 1)
    band = (kj > qi) & (kj <= qi + w)
    first = band & ((kj >= w) | (t > 0))
    bias_band = jnp.where(band, 0.0, NEG_BIG).astype(F32)
    bias_first = jnp.where(first, 0.0, NEG_BIG).astype(F32)
    col = lax.broadcasted_iota(jnp.int32, (1, 2 * w), 1)
    sum_r = lax.broadcasted_iota(jnp.int32, (2 * SUBLANES, 4 * w), 0)
    sum_c = lax.broadcasted_iota(jnp.int32, (2 * SUBLANES, 4 * w), 1)
    sum_rows = (((sum_r == 0) & (sum_c < 2 * w)) | ((sum_r == 1) & (sum_c >= 2 * w))).astype(F32)

    def sink_row(ha, hb):
        return jnp.where(col < w, sinks_ref[ha], sinks_ref[hb]) * LOG2_E

    gate_w = GATE_WIDTH // (n_sub * N_KV_HEADS)
    operands = {}

    def sub_block_operands(sb):
        if sb not in operands:
            cur = slice(sb * w, (sb + 1) * w)
            if sb == 0:
                kcat = jnp.concatenate([kprev[...], k[cur]], axis=0)
                vcat = jnp.concatenate([vprev[...], v[cur]], axis=0)
                bias = bias_first
            else:
                kcat = k[(sb - 1) * w:(sb + 1) * w]
                vcat = v[(sb - 1) * w:(sb + 1) * w]
                bias = bias_band
            k_split = _half_split(kcat, pltpu.roll(kcat, HEAD_DIM, 1), lo)
            k_stack = [jnp.concatenate(k_split[2 * g:2 * g + 2], axis=0) for g in range(N_KV_HEADS)]
            vt = vcat.T
            zero = jnp.zeros((HEAD_DIM, 2 * w), F32)
            v_stack = []
            for g in range(N_KV_HEADS):
                vg = vt[g * HEAD_DIM:(g + 1) * HEAD_DIM]
                v_stack.append(jnp.concatenate(
                    [jnp.concatenate([vg, zero], axis=1), jnp.concatenate([zero, vg], axis=1), sum_rows],
                    axis=0).astype(BF))
            operands[sb] = (k_stack, v_stack, jnp.concatenate([bias, bias], axis=1))
        return operands[sb]

    def scores(sb, grp):
        k_stack, _, _ = sub_block_operands(sb)
        cur = slice(sb * w, (sb + 1) * w)
        qq = jnp.concatenate([q[2 * grp][cur], q[2 * grp + 1][cur]], axis=0)
        return _dot_nt(k_stack[grp], qq)

    def weighted_values(sb, grp, s):
        _, v_stack, bias2 = sub_block_operands(sb)
        p_e, sink_e = _softmax_terms_t(s[:2 * w], bias2, sink_row(4 * grp, 4 * grp + 2))
        p_o, sink_o = _softmax_terms_t(s[2 * w:], bias2, sink_row(4 * grp + 1, 4 * grp + 3))
        d = _dot(v_stack[grp], jnp.concatenate([p_e, p_o], axis=0))
        heads = 2 * HEAD_DIM
        o2 = jnp.concatenate(
            [d[:HEAD_DIM] * (1.0 / (d[heads:heads + 1] + sink_e)),
             d[HEAD_DIM:heads] * (1.0 / (d[heads + 1:heads + 2] + sink_o))], axis=0)
        return [o2[:, :w].T, o2[:, w:].T]

    chains = [(sb, grp) for sb in range(n_sub) for grp in range(N_KV_HEADS)]
    gate_cols, outs = [], []
    s_next = scores(*chains[0])
    for i, (sb, grp) in enumerate(chains):
        s_cur = s_next
        gate_cols.append(_gate_cols(hn, win_ref, bg_ref, slice(i * gate_w, (i + 1) * gate_w)))
        if i + 1 < len(chains):
            s_next = scores(*chains[i + 1])
        outs += weighted_values(sb, grp, s_cur)
    per_sb = 2 * N_KV_HEADS
    attn = jnp.concatenate(
        [jnp.concatenate(outs[sb * per_sb:(sb + 1) * per_sb], axis=1) for sb in range(n_sub)],
        axis=0).astype(BF)
    gates = jnp.concatenate(gate_cols, axis=1)

    kprev[...] = k[MIX_TB - w:]
    vprev[...] = v[MIX_TB - w:]
    kwin_ref[...] = k[MIX_TB - w:]
    vwin_ref[...] = v[MIX_TB - w:]
    x1_ref[...] = _merge_out(x, gates, _dot(attn, wba_ref[...]), ssm_proj, wo_ref)


def _mix_prompt(x, y_ssm, rc, rs1, rs2, sinks, g1, w_in, bg, wba, wbs, wo, wglu, bglu):
    b, t, _ = x.shape
    nblk = t // MIX_TB
    x2 = x.reshape(b * t, D_MODEL)
    row_map = lambda i, j: (i * nblk + j, 0)
    x1, kwin, vwin, w_in_bf = pl.pallas_call(
        _mix_prompt_kernel,
        grid=(b, nblk),
        in_specs=[
            pl.BlockSpec(memory_space=pltpu.SMEM),
            pl.BlockSpec((MIX_TB, D_MODEL), row_map),
            pl.BlockSpec((MIX_TB, SSM_WIDTH), row_map),
            pl.BlockSpec((MIX_TB, LANES), lambda i, j: (j, 0)),
            pl.BlockSpec((MIX_TB, LANES), lambda i, j: (j, 0)),
            pl.BlockSpec((MIX_TB, LANES), lambda i, j: (j, 0)),
            _const_spec((1, D_MODEL)),
            pl.BlockSpec(memory_space=pl.ANY),
            _const_spec(bg.shape),
            _const_spec(wba.shape),
            _const_spec(wbs.shape),
            _const_spec(wo.shape),
            _const_spec(wglu.shape),
            _const_spec(bglu.shape),
        ],
        out_specs=[
            pl.BlockSpec((MIX_TB, D_MODEL), row_map),
            pl.BlockSpec((None, WINDOW, KV_WIDTH), lambda i, j: (i, 0, 0)),
            pl.BlockSpec((None, WINDOW, KV_WIDTH), lambda i, j: (i, 0, 0)),
            pl.BlockSpec(memory_space=pl.ANY),
        ],
        out_shape=[
            jax.ShapeDtypeStruct((b * t, D_MODEL), F32),
            jax.ShapeDtypeStruct((b, WINDOW, KV_WIDTH), F32),
            jax.ShapeDtypeStruct((b, WINDOW, KV_WIDTH), F32),
            jax.ShapeDtypeStruct(w_in.shape, BF),
        ],
        scratch_shapes=[pltpu.VMEM((WINDOW, KV_WIDTH), F32), pltpu.VMEM((WINDOW, KV_WIDTH), F32),
                        pltpu.VMEM(w_in.shape, BF),
                        pltpu.VMEM((FFN_W_SLOTS, D_MODEL // FFN_W_CHUNKS, IN_WIDTH), F32),
                        pltpu.SemaphoreType.DMA((FFN_W_SLOTS,)), pltpu.SemaphoreType.DMA((1,))],
        compiler_params=pltpu.CompilerParams(
            dimension_semantics=("arbitrary", "arbitrary"), vmem_limit_bytes=VMEM_LIMIT),
        name="mix_prompt",
    )(sinks, x2, y_ssm, rc, rs1, rs2, g1, w_in, bg, wba, wbs, wo, wglu, bglu)
    return x1, kwin, vwin, w_in_bf


QROWS = 16
SAMPLE_TB = 32


def _sample_layer_kernel(x_ref, kbuf_ref, vbuf_ref, rc_ref, rs1_ref, rs2_ref, sinkc_ref, g1_ref, wqkv_ref,
                         xall_ref, h0re_ref, h0im_ref, bg_ref, lb_ref, d_ref, wglu_ref, bglu_ref,
                         win_hbm, wbad_hbm, wbs_hbm, wo_hbm, bblk_hbm, ctblk_hbm,
                         kout_ref, vout_ref, x1_ref, hre_ref, him_ref,
                         qz, o3_all, win_s, wbad_s, wbs_s, wo_s, bblk_s, ctblk_s, sem):
    nb = x_ref.shape[0]
    step = pl.program_id(0)
    last_step = pl.num_programs(0) - 1
    late = ((win_hbm, win_s), (wbad_hbm, wbad_s), (wbs_hbm, wbs_s), (wo_hbm, wo_s), (bblk_hbm, bblk_s),
            (ctblk_hbm, ctblk_s))
    late_copies = [pltpu.make_async_copy(src, dst, sem.at[i]) for i, (src, dst) in enumerate(late)]

    @pl.when(step == 0)
    def _():
        for copy in late_copies:
            copy.start()

    hn = _rms(x_ref[...], g1_ref[...]).astype(BF)
    qkv = _dot(hn, wqkv_ref[...])
    rc, rs1, rs2 = rc_ref[...], rs1_ref[...], rs2_ref[...]
    scale = HEAD_DIM ** -0.5
    k_new = _rope(qkv[:, ATTN_WIDTH:ATTN_WIDTH + KV_WIDTH], rc, rs1, rs2)
    v_new = qkv[:, ATTN_WIDTH + KV_WIDTH:ATTN_WIDTH + 2 * KV_WIDTH]
    pad = jnp.zeros((LANES - nb, KV_WIDTH), F32)
    k_new_t = jnp.concatenate([k_new, pad], axis=0).T
    v_new_t = jnp.concatenate([v_new, pad], axis=0).T

    lane = lax.broadcasted_iota(jnp.int32, (nb, LANES), 1)
    lo = lane < HEAD_DIM
    qz[...] = jnp.zeros_like(qz)
    for c in range(ATTN_WIDTH // LANES):
        qc = _rope(qkv[:, c * LANES:(c + 1) * LANES], rc, rs1, rs2) * scale
        qr = pltpu.roll(qc, HEAD_DIM, 1)
        zero = jnp.zeros_like(qc)
        if c < 2:
            even, odd = jnp.where(lo, qc, zero), jnp.where(lo, qr, zero)
        else:
            even, odd = jnp.where(lo, zero, qr), jnp.where(lo, zero, qc)
        qz[pl.ds(2 * c, nb, stride=QROWS), :] = even
        qz[pl.ds(2 * c + 1, nb, stride=QROWS), :] = odd

    last = lax.broadcasted_iota(jnp.int32, (KV_WIDTH, WINDOW), 1) == WINDOW - 1
    for b in range(nb):
        kout_ref[b] = jnp.where(last, k_new_t[:, b:b + 1], pltpu.roll(kbuf_ref[b], WINDOW - 1, 1))
        vout_ref[b] = jnp.where(last, v_new_t[:, b:b + 1], pltpu.roll(vbuf_ref[b], WINDOW - 1, 1))

    sink = sinkc_ref[...]
    q3 = qz[...].reshape(nb, QROWS, LANES).astype(BF)
    s = jnp.einsum('bhd,bdk->bhk', q3, kout_ref[...].astype(BF), preferred_element_type=F32)
    m = jnp.maximum(jnp.max(s, axis=-1, keepdims=True), sink)
    p = jnp.exp(s - m)
    den = jnp.sum(p, axis=-1, keepdims=True) + jnp.exp(sink - m)
    p = (p * (1.0 / den)).astype(BF)
    o3 = jnp.einsum('bhk,bdk->bhd', p, vout_ref[...].astype(BF), preferred_element_type=F32)
    o3_all[pl.ds(pl.multiple_of(step * (nb * QROWS), nb * QROWS), nb * QROWS), :] = o3.reshape(nb * QROWS, LANES)

    @pl.when(step == last_step)
    def _():
        for copy in late_copies:
            copy.wait()
        _sample_tail(xall_ref, o3_all, h0re_ref, h0im_ref, g1_ref, win_s, bg_ref, wbad_s, wbs_s, wo_s,
                     lb_ref, bblk_s, ctblk_s, d_ref, wglu_ref, bglu_ref, x1_ref, hre_ref, him_ref)


def _sample_layer(x, kbuf, vbuf, h0re, h0im, rc, rs1, rs2, sinkc, g1, w_in, bg, wbad, wbs, wo, lb, bblk, ctblk,
                  d, wglu, bglu):
    nb = x.shape[0]
    tb = SAMPLE_TB
    blocked = (x, kbuf, vbuf)
    early = (rc, rs1, rs2, sinkc, g1)
    consts = (x, h0re, h0im, bg, lb, d, wglu, bglu)
    late = (w_in, wbad, wbs, wo, bblk, ctblk)
    kv_spec = pl.BlockSpec((tb, KV_WIDTH, WINDOW), lambda i: (i, 0, 0))
    whole = lambda shape: pl.BlockSpec(shape, lambda i: (0,) * len(shape))
    return pl.pallas_call(
        _sample_layer_kernel,
        grid=(nb // tb,),
        in_specs=[pl.BlockSpec((tb, D_MODEL), lambda i: (i, 0)), kv_spec, kv_spec]
        + [_const_spec(a.shape) for a in early]
        + [pl.BlockSpec((D_MODEL, QKV_WIDTH), lambda i: (0, 0), pipeline_mode=pl.Buffered(1))]
        + [_const_spec(a.shape) for a in consts]
        + [pl.BlockSpec(memory_space=pl.ANY) for _ in late],
        out_specs=[kv_spec, kv_spec, whole((nb, D_MODEL)), whole((N_STATES, nb)), whole((N_STATES, nb))],
        out_shape=[
            jax.ShapeDtypeStruct((nb, KV_WIDTH, WINDOW), F32),
            jax.ShapeDtypeStruct((nb, KV_WIDTH, WINDOW), F32),
            jax.ShapeDtypeStruct((nb, D_MODEL), F32),
            jax.ShapeDtypeStruct((N_STATES, nb), F32),
            jax.ShapeDtypeStruct((N_STATES, nb), F32),
        ],
        scratch_shapes=[pltpu.VMEM((tb * QROWS, LANES), F32), pltpu.VMEM((nb * QROWS, LANES), F32)]
        + [pltpu.VMEM(a.shape, a.dtype) for a in late] + [pltpu.SemaphoreType.DMA((len(late),))],
        compiler_params=pltpu.CompilerParams(
            dimension_semantics=("arbitrary",), vmem_limit_bytes=VMEM_LIMIT),
        name="sample_layer",
    )(*blocked, *early, w_in, *consts, *late)


def _sample_tail(x_ref, o3_ref, h0re_ref, h0im_ref, g1_ref, win_ref, bg_ref, wbad_ref,
                 wbs_ref, wo_ref, lb_ref, bblk_ref, ctblk_ref, d_ref, wglu_ref, bglu_ref,
                 x1_ref, hre_ref, him_ref):
    nb = x_ref.shape[0]
    x = x_ref[...]
    hn = _rms(x, g1_ref[...]).astype(BF)

    lane = lax.broadcasted_iota(jnp.int32, (nb, LANES), 1)
    lo = lane < HEAD_DIM
    a = jnp.zeros((nb, D_MODEL), F32)
    zero = jnp.zeros((nb, LANES), F32)
    for h in range(N_Q_HEADS):
        oh = o3_ref[pl.ds(h, nb, stride=QROWS), :]
        oh = jnp.where(lo, oh, zero) if h < N_Q_HEADS // 2 else jnp.where(lo, zero, oh)
        a = a + _dot(oh.astype(BF), wbad_ref[h])

    u = _dot(hn, win_ref[:, U_COL0:GATE_COL0])
    ub = u.astype(BF)
    lre, lim = lb_ref[0:1, :], lb_ref[1:2, :]
    y_cols = []
    for o in range(N_OCT):
        sl = slice(o * OCT_STATES, (o + 1) * OCT_STATES)
        bu = _dot(ub[:, o * LANES:(o + 1) * LANES], bblk_ref[o])
        blocks = [slice(c * LANES, (c + 1) * LANES) for c in range(o * OCT_COL, (o + 1) * OCT_COL)]
        h0r = jnp.concatenate([h0re_ref[rows, :].T for rows in blocks], axis=1)
        h0i = jnp.concatenate([h0im_ref[rows, :].T for rows in blocks], axis=1)
        hr = bu[:, :OCT_STATES] + (lre[:, sl] * h0r - lim[:, sl] * h0i)
        hi = bu[:, OCT_STATES:] + (lre[:, sl] * h0i + lim[:, sl] * h0r)
        for cc, rows in enumerate(blocks):
            hre_ref[rows, :] = hr[:, cc * LANES:(cc + 1) * LANES].T
            him_ref[rows, :] = hi[:, cc * LANES:(cc + 1) * LANES].T
        y_cols.append(_dot_nt(jnp.concatenate([hr, hi], axis=1).astype(BF), ctblk_ref[o]))
    y = jnp.concatenate(y_cols, axis=1) + d_ref[...] * u
    z = _gelu_tanh(y)
    gate = _dot(z.astype(BF), wglu_ref[...]) + bglu_ref[...]
    ssm = (z * _sigmoid(gate)).astype(BF)

    gates = _gate_cols(hn, win_ref, bg_ref, slice(0, GATE_WIDTH))
    x1_ref[...] = _merge_out(x, gates, a, _dot(ssm, wbs_ref[...]), wo_ref)


FFN_W_CHUNKS = 16
FFN_W_SLOTS = 4


def _stream_cast_weights(jobs):
    tasks, used = [], {}
    for w_hbm, w_bf, stage, sem in jobs:
        rows = stage.shape[1]
        for k in range(w_hbm.shape[0] // rows):
            slot = used.get(id(stage), 0) % stage.shape[0]
            used[id(stage)] = used.get(id(stage), 0) + 1
            copy = pltpu.make_async_copy(w_hbm.at[pl.ds(k * rows, rows), :], stage.at[slot], sem.at[slot])
            tasks.append((copy, w_bf, stage, slot, k * rows, rows))
    ahead = min(stage.shape[0] for _, _, stage, _ in jobs) - 1
    for copy, *_ in tasks[:ahead]:
        copy.start()
    for i, (copy, w_bf, stage, slot, row0, rows) in enumerate(tasks):
        if i + ahead < len(tasks):
            tasks[i + ahead][0].start()
        copy.wait()
        w_bf[row0:row0 + rows, :] = stage[slot].astype(BF)


def _ffn_rows(x, g2_ref, wgate, wup, wdown, gf_ref):
    h = _rms(x, g2_ref[...]).astype(BF)
    gate = _dot(h, wgate[...])
    up = _dot(h, wup[...])
    half_gate = 0.5 * gate
    act = ((half_gate + half_gate * jnp.tanh(half_gate)) * up).astype(BF)
    x2 = x + _dot(act, wdown[...])
    return _rms(x2, gf_ref[...])


def _ffn_kernel(xp_ref, xs_ref, g2_ref, wgate_hbm, wup_hbm, wdown_hbm, gf_ref, yp_ref, ys_ref,
                wgate, wup, wdown, stage_in, stage_out, sem_in, sem_out):
    i = pl.program_id(0)
    n_prompt = pl.num_programs(0) - 1

    @pl.when(i == 0)
    def _():
        _stream_cast_weights([(wgate_hbm, wgate, stage_in, sem_in), (wup_hbm, wup, stage_in, sem_in),
                              (wdown_hbm, wdown, stage_out, sem_out)])

    @pl.when(i < n_prompt)
    def _():
        half = FFN_TB // FFN_SPLIT
        for rows in (slice(h * half, (h + 1) * half) for h in range(FFN_SPLIT)):
            yp_ref[rows, :] = _ffn_rows(xp_ref[rows, :], g2_ref, wgate, wup, wdown, gf_ref)

    @pl.when(i == n_prompt)
    def _():
        ys_ref[...] = _ffn_rows(xs_ref[...], g2_ref, wgate, wup, wdown, gf_ref)


def _ffn(xp, xs, g2, wgate, wup, wdown, gf):
    n, ns = xp.shape[0], xs.shape[0]
    n_prompt = n // FFN_TB
    prompt_map = lambda i: (jnp.minimum(i, n_prompt - 1), 0)
    return pl.pallas_call(
        _ffn_kernel,
        grid=(n_prompt + 1,),
        in_specs=[
            pl.BlockSpec((FFN_TB, D_MODEL), prompt_map),
            _const_spec((ns, D_MODEL)),
            _const_spec((1, D_MODEL)),
            pl.BlockSpec(memory_space=pl.ANY),
            pl.BlockSpec(memory_space=pl.ANY),
            pl.BlockSpec(memory_space=pl.ANY),
            _const_spec((1, D_MODEL)),
        ],
        out_specs=[
            pl.BlockSpec((FFN_TB, D_MODEL), prompt_map),
            pl.BlockSpec((ns, D_MODEL), lambda i: (0, 0)),
        ],
        out_shape=[
            jax.ShapeDtypeStruct((n, D_MODEL), F32),
            jax.ShapeDtypeStruct((ns, D_MODEL), F32),
        ],
        scratch_shapes=[
            pltpu.VMEM((D_MODEL, D_FF), BF), pltpu.VMEM((D_MODEL, D_FF), BF), pltpu.VMEM((D_FF, D_MODEL), BF),
            pltpu.VMEM((FFN_W_SLOTS, D_MODEL // FFN_W_CHUNKS, D_FF), F32),
            pltpu.VMEM((FFN_W_SLOTS, D_FF // FFN_W_CHUNKS, D_MODEL), F32),
            pltpu.SemaphoreType.DMA((FFN_W_SLOTS,)), pltpu.SemaphoreType.DMA((FFN_W_SLOTS,)),
        ],
        compiler_params=pltpu.CompilerParams(
            dimension_semantics=("arbitrary",), vmem_limit_bytes=VMEM_LIMIT),
        name="ffn",
    )(xp, xs, g2, wgate, wup, wdown, gf)


def _rope_tables(pos):
    pos = np.asarray(pos, np.float64)
    inv_freq = ROPE_THETA ** (-(np.arange(ROPE_HALF, dtype=np.float64) * 2.0 / ROPE_DIM))
    ang = pos[:, None] * inv_freq[None, :]
    cos, sin = np.cos(ang), np.sin(ang)
    pad = np.zeros((pos.shape[0], HEAD_DIM - ROPE_DIM))
    zero = np.zeros_like(sin)
    rc = np.concatenate([cos, cos, pad + 1.0], axis=1)
    rs1 = np.concatenate([zero, sin, pad], axis=1)
    rs2 = np.concatenate([-sin, zero, pad], axis=1)
    rep = LANES // HEAD_DIM
    return tuple(jnp.asarray(np.tile(a, (1, rep)), F32) for a in (rc, rs1, rs2))


def _cmul(ar, ai, br, bi):
    return ar * br - ai * bi, ar * bi + ai * br


def _ssm_tables(lam_re, lam_im, log_dt, b_re, b_im, c_re, c_im):
    dt = jnp.exp(log_dt)[:, None]
    mag = jnp.exp(lam_re * dt)
    lb_re = mag * jnp.cos(lam_im * dt)
    lb_im = mag * jnp.sin(lam_im * dt)
    den = lam_re * lam_re + lam_im * lam_im
    nr = lb_re - 1.0
    k_re = ((nr * lam_re + lb_im * lam_im) / den)[..., None]
    k_im = ((lb_im * lam_re - nr * lam_im) / den)[..., None]
    bb_re = k_re * b_re - k_im * b_im
    bb_im = k_re * b_im + k_im * b_re

    a_re, a_im = lb_re, lb_im
    for _ in range(int(math.log2(CHUNK))):
        a_re, a_im = _cmul(a_re, a_im, a_re, a_im)
    s_re, s_im = a_re, a_im
    for _ in range(int(math.log2(SEG))):
        s_re, s_im = _cmul(s_re, s_im, s_re, s_im)

    eye = jnp.eye(OCT, dtype=F32).reshape(1, OCT, 1, OCT, 1)

    def block_diag(a):
        r, c = a.shape[1:]
        return (a.reshape(N_OCT, OCT, r, 1, c) * eye).reshape(N_OCT, OCT * r, OCT * c)

    bblk = jnp.concatenate([block_diag(jnp.swapaxes(bb_re, 1, 2)),
                            block_diag(jnp.swapaxes(bb_im, 1, 2))], axis=2)
    ctblk = jnp.concatenate([block_diag(c_re), block_diag(-c_im)], axis=2)

    oct_cols = lambda a: a.reshape(N_OCT, 1, OCT_STATES)
    lcol = jnp.concatenate([oct_cols(lb_re), oct_cols(lb_im),
                            jnp.zeros((N_OCT, SUBLANES - 2, OCT_STATES), F32)], axis=1)

    flat = lambda a: a.reshape(1, N_STATES)
    col = lambda a: a.reshape(N_COL, 1, LANES)
    a_tab = jnp.concatenate([col(a_re), col(a_im), col(s_re), col(s_im),
                             jnp.zeros((N_COL, SUBLANES - 4, LANES), F32)], axis=1)
    lb = jnp.concatenate([flat(lb_re), flat(lb_im)], axis=0)
    return bblk, ctblk, lcol, a_tab, lb


def kernel(x_prompt, x_sample, state_k_win, state_v_win, state_ssm_re, state_ssm_im, norm1_g, w_in, b_gate, attn_sinks, ssm_lam_re, ssm_lam_im, ssm_log_dt, ssm_b_re, ssm_b_im, ssm_c_re, ssm_c_im, ssm_d, w_glu, b_glu, w_branch_attn, w_branch_ssm, w_out, norm2_g, w_ffn_gate, w_ffn_up, w_ffn_down, norm_f_g):
    depth = w_in.shape[0]
    assert depth == 1
    b, t, _ = x_prompt.shape
    nb, s_len, _ = x_sample.shape
    assert s_len == 1 and state_k_win.shape[2] == WINDOW
    l = 0
    assert w_in.shape[2] == IN_WIDTH
    w_in_f = w_in[l]
    g1 = norm1_g[l].reshape(1, D_MODEL)
    g2 = norm2_g[l].reshape(1, D_MODEL)
    gf = norm_f_g.reshape(1, D_MODEL)
    bg = b_gate[l].reshape(1, GATE_WIDTH)
    d = ssm_d[l].reshape(1, SSM_WIDTH)
    wglu = w_glu[l].astype(BF)
    bglu = b_glu[l].reshape(1, SSM_WIDTH)
    wba = w_branch_attn[l].astype(BF)
    wbs = w_branch_ssm[l].astype(BF)
    wo = w_out[l].astype(BF)
    sinks = attn_sinks[l]

    bblk, ctblk, lcol, a_tab, lb = _ssm_tables(
        ssm_lam_re[l], ssm_lam_im[l], ssm_log_dt[l], ssm_b_re[l], ssm_b_im[l], ssm_c_re[l], ssm_c_im[l])

    rc, rs1, rs2 = _rope_tables(np.arange(t))
    yssm_p, hre_p, him_p = _s5_prompt(x_prompt, g1, w_in_f, bblk, ctblk, lcol, a_tab, d)
    x1_p, kwin_p, vwin_p, w_in_b = _mix_prompt(x_prompt, yssm_p, rc, rs1, rs2, sinks, g1, w_in_f, bg, wba, wbs, wo,
                                               wglu, bglu)

    rcs, rs1s, rs2s = _rope_tables(PAST_LEN + np.arange(1))
    sinkc = jnp.concatenate([sinks, jnp.zeros((QROWS - N_Q_HEADS,), F32)]).reshape(QROWS, 1)
    wbad = jnp.concatenate([wba.reshape(N_Q_HEADS, HEAD_DIM, D_MODEL)] * 2, axis=1)
    xs = x_sample.reshape(nb, D_MODEL)
    key_minor = lambda a: jnp.swapaxes(a.reshape(nb, WINDOW, KV_WIDTH), 1, 2)
    kwin_s, vwin_s, x1_s, hre_s, him_s = _sample_layer(
        xs, key_minor(state_k_win[l]), key_minor(state_v_win[l]),
        state_ssm_re[l].reshape(nb, N_STATES).T, state_ssm_im[l].reshape(nb, N_STATES).T,
        rcs, rs1s, rs2s, sinkc, g1, w_in_b, bg, wbad, wbs, wo, lb, bblk.astype(BF), ctblk.astype(BF),
        d, wglu, bglu)
    y_p, y_s = _ffn(x1_p, x1_s, g2, w_ffn_gate[l], w_ffn_up[l], w_ffn_down[l], gf)
    y_p = y_p.reshape(b, t, D_MODEL)
    y_s = y_s.reshape(nb, 1, D_MODEL)

    kv_shape_p = (1, b, WINDOW, N_KV_HEADS, HEAD_DIM)
    st_shape_p = (1, b, N_SSM_GROUPS, SSM_STATE)
    kv_shape_s = (1, nb, WINDOW, N_KV_HEADS, HEAD_DIM)
    st_shape_s = (1, nb, N_SSM_GROUPS, SSM_STATE)
    return (y_p, y_s,
            kwin_p.reshape(kv_shape_p), vwin_p.reshape(kv_shape_p),
            hre_p.reshape(st_shape_p), him_p.reshape(st_shape_p),
            jnp.swapaxes(kwin_s, 1, 2).reshape(kv_shape_s), jnp.swapaxes(vwin_s, 1, 2).reshape(kv_shape_s),
            hre_s.T.reshape(st_shape_s), him_s.T.reshape(st_shape_s))
```

```python
import math

import jax
import jax.numpy as jnp
import numpy as np
from jax import lax
from jax.experimental import pallas as pl
from jax.experimental.pallas import tpu as pltpu

D_MODEL = 1024
N_Q_HEADS = 8
N_KV_HEADS = 2
HEAD_DIM = 64
ATTN_WIDTH = N_Q_HEADS * HEAD_DIM
KV_WIDTH = N_KV_HEADS * HEAD_DIM
WINDOW = 128
ROPE_DIM = HEAD_DIM // 4
ROPE_HALF = ROPE_DIM // 2
ROPE_THETA = 500000.0
SSM_WIDTH = D_MODEL // 2
SSM_GROUP = 16
N_SSM_GROUPS = SSM_WIDTH // SSM_GROUP
SSM_STATE = 64
N_STATES = N_SSM_GROUPS * SSM_STATE
GATE_WIDTH = 2 * D_MODEL
QKV_WIDTH = ATTN_WIDTH + 2 * KV_WIDTH
U_COL0 = QKV_WIDTH
GATE_COL0 = U_COL0 + SSM_WIDTH
IN_WIDTH = GATE_COL0 + GATE_WIDTH
D_FF = -(-8 * D_MODEL // (3 * 256)) * 256
NORM_EPS = 1e-5
PAST_LEN = 8192

LANES = 128
SUBLANES = 8
CHUNK = 8
OCT = LANES // SSM_GROUP
N_OCT = N_SSM_GROUPS // OCT
OCT_STATES = OCT * SSM_STATE
OCT_COL = OCT_STATES // LANES
N_COL = N_STATES // LANES
N_SEG = SUBLANES
SEG = 16
S5_ROWS = N_SEG * SEG
M_PAIR = 2
U_HALF = SSM_WIDTH // 2
SEG_PITCH = SEG * CHUNK + SUBLANES
MIX_TB = 512
FFN_TB = 1024
FFN_SPLIT = 4
NEG_BIG = -1e30
LOG2_E = math.log2(math.e)
VMEM_LIMIT = 56 * 1024 * 1024

BF = jnp.bfloat16
F32 = jnp.float32


def _dot(a, b):
    return jnp.dot(a, b, preferred_element_type=F32)


def _dot_nt(a, b):
    return lax.dot_general(a, b, (((1,), (1,)), ((), ())), preferred_element_type=F32)


def _dot_nt_split(a, b):
    a_hi, b_hi = a.astype(BF), b.astype(BF)
    a_lo = (a - a_hi.astype(F32)).astype(BF)
    b_lo = (b - b_hi.astype(F32)).astype(BF)
    return _dot_nt(a_hi, b_hi) + (_dot_nt(a_hi, b_lo) + _dot_nt(a_lo, b_hi))


def _rms(x, g):
    return x * lax.rsqrt(jnp.mean(x * x, axis=-1, keepdims=True) + NORM_EPS) * g


def _sigmoid(x):
    return 1.0 / (1.0 + jnp.exp(-x))


def _gelu_tanh(x):
    c = math.sqrt(2.0 / math.pi)
    return 0.5 * x * (1.0 + jnp.tanh(c * (x + 0.044715 * (x * x * x))))


def _rope(a, rc, rs1, rs2):
    return a * rc + pltpu.roll(a, ROPE_HALF, 1) * rs1 + pltpu.roll(a, LANES - ROPE_HALF, 1) * rs2


def _const_spec(shape):
    nd = len(shape)
    return pl.BlockSpec(shape, lambda *_: (0,) * nd, pipeline_mode=pl.Buffered(1))


def _build_chunk_operators(bblk_ref, ctblk_ref, lcol_ref, m_s, e_s, f_s):
    for o in range(N_OCT):
        ct = jnp.concatenate([ctblk_ref[o, :, :OCT_STATES], -ctblk_ref[o, :, OCT_STATES:]],
                             axis=1)
        lr, li = lcol_ref[o, 0:1, :], lcol_ref[o, 1:2, :]
        er, ei = bblk_ref[o, :, :OCT_STATES], bblk_ref[o, :, OCT_STATES:]
        k_blk = []
        for tau in range(CHUNK):
            e_cat = jnp.concatenate([er, ei], axis=1)
            i = CHUNK - 1 - tau
            e_s[o, i * LANES:(i + 1) * LANES, :] = e_cat.astype(BF)
            k_blk.append(_dot_nt_split(e_cat, ct).astype(BF))
            er, ei = er * lr - ei * li, er * li + ei * lr
        zero = jnp.zeros((LANES, LANES), BF)
        for j in range(CHUNK):
            jt, jj = divmod(j, M_PAIR)
            for i in range(M_PAIR * (jt + 1)):
                m_s[jt][o, i * LANES:(i + 1) * LANES, jj * LANES:(jj + 1) * LANES] = (
                    k_blk[j - i] if j >= i else zero)
        tr, ti = ct[:, :OCT_STATES], -ct[:, OCT_STATES:]
        for j in range(CHUNK):
            tr, ti = tr * lr - ti * li, tr * li + ti * lr
            f_s[o, :OCT_STATES, j * LANES:(j + 1) * LANES] = tr.T.astype(BF)
            f_s[o, OCT_STATES:, j * LANES:(j + 1) * LANES] = (-ti).T.astype(BF)


def _s5_prompt_kernel(x_ref, g1_ref, wu0_ref, wu1_ref, bblk_ref, ctblk_ref, lcol_ref, a_ref, d_ref,
                      out_ref, hfin_ref,
                      m0_ref, m1_ref, m2_ref, m3_ref, e_ref, f_ref, us, ys, sre, sim, car):
    m_ref = (m0_ref, m1_ref, m2_ref, m3_ref)
    blk = pl.program_id(1)

    @pl.when((pl.program_id(0) == 0) & (blk == 0))
    def _():
        _build_chunk_operators(bblk_ref, ctblk_ref, lcol_ref, m_ref, e_ref, f_ref)

    @pl.when(blk == 0)
    def _():
        car[...] = jnp.zeros_like(car)

    seg_tokens = SEG * CHUNK
    seg_rows = [slice(s * SEG_PITCH, s * SEG_PITCH + seg_tokens) for s in range(N_SEG)]
    tok_rows = [slice(s * seg_tokens, (s + 1) * seg_tokens) for s in range(N_SEG)]

    for s in range(0, N_SEG, 2):
        hn = _rms(x_ref[s * seg_tokens:(s + 2) * seg_tokens, :], g1_ref[...]).astype(BF)
        u = jnp.concatenate([_dot(hn, wu0_ref[...].astype(BF)), _dot(hn, wu1_ref[...].astype(BF))],
                            axis=1)
        for half in range(2):
            for cc in range(N_OCT):
                us[cc, seg_rows[s + half], :] = u[tok_rows[half], cc * LANES:(cc + 1) * LANES]

    sub = lax.broadcasted_iota(jnp.int32, (N_SEG, LANES), 0)

    def scan_column(c):
        tab = a_ref[c]
        are, aim = tab[0:1], tab[1:2]
        bre, bim = tab[2:3], tab[3:4]

        def step(cr, ci, r, keep_entering):
            slab = slice(r * N_SEG, (r + 1) * N_SEG)
            s_r, s_i = sre[c, slab, :], sim[c, slab, :]
            if keep_entering:
                sre[c, slab, :] = cr
                sim[c, slab, :] = ci
            return are * cr - aim * ci + s_r, are * ci + aim * cr + s_i

        cr = jnp.zeros((N_SEG, LANES), F32)
        ci = jnp.zeros((N_SEG, LANES), F32)
        for r in range(SEG):
            cr, ci = step(cr, ci, r, False)
        cv = car[c]
        pr, pi = cv[0:1], cv[1:2]
        sr = jnp.zeros((N_SEG, LANES), F32)
        si = jnp.zeros((N_SEG, LANES), F32)
        for s in range(N_SEG):
            sr = jnp.where(sub == s, pr, sr)
            si = jnp.where(sub == s, pi, si)
            pr, pi = (bre * pr - bim * pi + cr[s:s + 1], bre * pi + bim * pr + ci[s:s + 1])
        end = jnp.where(sub == 0, pr, jnp.where(sub == 1, pi, 0.0))
        car[c] = end
        hfin_ref[c] = end
        cr, ci = sr, si
        for r in range(SEG):
            cr, ci = step(cr, ci, r, True)

    for o in range(N_OCT):
        uo = jnp.concatenate(
            [jnp.concatenate([us[o, pl.ds(r * CHUNK + i, N_SEG, stride=SEG_PITCH), :] for r in range(SEG)],
                             axis=0).astype(BF) for i in range(CHUNK)], axis=1)
        s_end = _dot(uo, e_ref[o])
        cols = range(o * OCT_COL, (o + 1) * OCT_COL)
        for cc, c in enumerate(cols):
            sre[c] = s_end[:, cc * LANES:(cc + 1) * LANES]
            sim[c] = s_end[:, OCT_STATES + cc * LANES:OCT_STATES + (cc + 1) * LANES]
        y_in = jnp.concatenate(
            [_dot(uo[:, :(jt + 1) * M_PAIR * LANES], m_ref[jt][o]) for jt in range(CHUNK // M_PAIR)],
            axis=1)
        for c in cols:
            scan_column(c)
        hp = jnp.concatenate([sre[c] for c in cols] + [sim[c] for c in cols], axis=1).astype(BF)
        yo = y_in + _dot(hp, f_ref[o])
        for r in range(SEG):
            for j in range(CHUNK):
                ys[o, pl.ds(r * CHUNK + j, N_SEG, stride=SEG_PITCH), :] = (
                    yo[r * N_SEG:(r + 1) * N_SEG, j * LANES:(j + 1) * LANES])
        d_o = d_ref[:, o * LANES:(o + 1) * LANES]
        for s in range(N_SEG):
            out_ref[tok_rows[s], o * LANES:(o + 1) * LANES] = (
                ys[o, seg_rows[s], :] + d_o * us[o, seg_rows[s], :])


def _s5_prompt(x, g1, w_in, bblk, ctblk, lcol, a_tab, d):
    b, t, _ = x.shape
    tb = S5_ROWS * CHUNK
    nblk = t // tb
    x2 = x.reshape(b * t, D_MODEL)
    row_map = lambda i, j: (i * nblk + j, 0)
    op_shape = (N_OCT, CHUNK * LANES, CHUNK * LANES)
    out, hfin = pl.pallas_call(
        _s5_prompt_kernel,
        grid=(b, nblk),
        in_specs=[
            pl.BlockSpec((tb, D_MODEL), row_map),
            _const_spec((1, D_MODEL)),
            *[pl.BlockSpec((D_MODEL, U_HALF), lambda i, j, c=U_COL0 // U_HALF + h: (0, c),
                           pipeline_mode=pl.Buffered(1)) for h in range(2)],
            _const_spec(bblk.shape),
            _const_spec(ctblk.shape),
            _const_spec(lcol.shape),
            _const_spec(a_tab.shape),
            _const_spec((1, SSM_WIDTH)),
        ],
        out_specs=[
            pl.BlockSpec((tb, SSM_WIDTH), row_map),
            pl.BlockSpec((None, N_COL, SUBLANES, LANES), lambda i, j: (i, 0, 0, 0)),
        ],
        out_shape=[
            jax.ShapeDtypeStruct((b * t, SSM_WIDTH), F32),
            jax.ShapeDtypeStruct((b, N_COL, SUBLANES, LANES), F32),
        ],
        scratch_shapes=[
            *[pltpu.VMEM((N_OCT, (jt + 1) * M_PAIR * LANES, M_PAIR * LANES), BF)
              for jt in range(CHUNK // M_PAIR)],
            pltpu.VMEM(op_shape, BF), pltpu.VMEM(op_shape, BF),
            pltpu.VMEM((N_OCT, N_SEG * SEG_PITCH, LANES), F32),
            pltpu.VMEM((N_OCT, N_SEG * SEG_PITCH, LANES), F32),
            pltpu.VMEM((N_COL, S5_ROWS, LANES), F32), pltpu.VMEM((N_COL, S5_ROWS, LANES), F32),
            pltpu.VMEM((N_COL, SUBLANES, LANES), F32),
        ],
        compiler_params=pltpu.CompilerParams(
            dimension_semantics=("arbitrary", "arbitrary"), vmem_limit_bytes=VMEM_LIMIT),
        name="s5_prompt",
    )(x2, g1, w_in, w_in, bblk, ctblk, lcol, a_tab, d)
    return out, hfin[:, :, 0, :], hfin[:, :, 1, :]


def _gate_cols(hn, win_ref, bg_ref, cols):
    w_cols = slice(GATE_COL0 + cols.start, GATE_COL0 + cols.stop)
    return _sigmoid(_dot(hn, win_ref[:, w_cols]) + bg_ref[:, cols])


def _merge_out(x, gates, attn_proj, ssm_proj, wo_ref):
    merged = gates[:, :D_MODEL] * attn_proj + gates[:, D_MODEL:] * ssm_proj
    return x + _dot(merged.astype(BF), wo_ref[...])


def _half_split(a, ar, lo):
    z = jnp.zeros_like(a)
    return (jnp.where(lo, a, z).astype(BF), jnp.where(lo, z, ar).astype(BF),
            jnp.where(lo, ar, z).astype(BF), jnp.where(lo, z, a).astype(BF))


def _softmax_terms_t(st, bias_t, sink):
    st = st + bias_t
    m = jnp.maximum(jnp.max(st, axis=0, keepdims=True), sink)
    return jnp.exp2(st - m).astype(BF), jnp.exp2(sink - m)


def _mix_prompt_kernel(sinks_ref, x_ref, y_ref, rc_ref, rs1_ref, rs2_ref, g1_ref, win_hbm,
                       bg_ref, wba_ref, wbs_ref, wo_ref, wglu_ref, bglu_ref,
                       x1_ref, kwin_ref, vwin_ref, winb_hbm, kprev, vprev, win_ref, stage, sem_stage, sem_out):
    t = pl.program_id(1)
    seq = pl.program_id(0)
    share_copy = pltpu.make_async_copy(win_ref, winb_hbm, sem_out.at[0])

    @pl.when((seq == 0) & (t == 0))
    def _():
        _stream_cast_weights([(win_hbm, win_ref, stage, sem_stage)])
        share_copy.start()

    @pl.when((seq == pl.num_programs(0) - 1) & (t == pl.num_programs(1) - 1))
    def _():
        share_copy.wait()

    @pl.when(t == 0)
    def _():
        kprev[...] = jnp.zeros_like(kprev)
        vprev[...] = jnp.zeros_like(vprev)

    x = x_ref[...]
    hn = _rms(x, g1_ref[...]).astype(BF)
    qkv = _dot(hn, win_ref[:, :QKV_WIDTH])
    z = _gelu_tanh(y_ref[...])
    glu = _dot(z.astype(BF), wglu_ref[...]) + bglu_ref[...]
    rc, rs1, rs2 = rc_ref[...], rs1_ref[...], rs2_ref[...]
    scale = HEAD_DIM ** -0.5 * LOG2_E
    q = [(_rope(qkv[:, c * LANES:(c + 1) * LANES], rc, rs1, rs2) * scale).astype(BF)
         for c in range(ATTN_WIDTH // LANES)]
    k = _rope(qkv[:, ATTN_WIDTH:ATTN_WIDTH + KV_WIDTH], rc, rs1, rs2)
    v = qkv[:, ATTN_WIDTH + KV_WIDTH:ATTN_WIDTH + 2 * KV_WIDTH]
    ssm = (z * _sigmoid(glu)).astype(BF)
    ssm_proj = _dot(ssm, wbs_ref[...])

    w = WINDOW
    n_sub = MIX_TB // w
    lane = lax.broadcasted_iota(jnp.int32, (2 * w, LANES), 1)
    lo = lane < HEAD_DIM
    kj = lax.broadcasted_iota(jnp.int32, (2 * w, w), 0)
    qi = lax.broadcasted_iota(jnp.int32, (2 * w, w), 1)
    band = (kj > qi) & (kj <= qi + w)
    first = band & ((kj >= w) | (t > 0))
    bias_band = jnp.where(band, 0.0, NEG_BIG).astype(F32)
    bias_first = jnp.where(first, 0.0, NEG_BIG).astype(F32)
    col = lax.broadcasted_iota(jnp.int32, (1, 2 * w), 1)
    sum_r = lax.broadcasted_iota(jnp.int32, (2 * SUBLANES, 4 * w), 0)
    sum_c = lax.broadcasted_iota(jnp.int32, (2 * SUBLANES, 4 * w), 1)
    sum_rows = (((sum_r == 0) & (sum_c < 2 * w)) | ((sum_r == 1) & (sum_c >= 2 * w))).astype(F32)

    def sink_row(ha, hb):
        return jnp.where(col < w, sinks_ref[ha], sinks_ref[hb]) * LOG2_E

    gate_w = GATE_WIDTH // (n_sub * N_KV_HEADS)
    operands = {}

    def sub_block_operands(sb):
        if sb not in operands:
            cur = slice(sb * w, (sb + 1) * w)
            if sb == 0:
                kcat = jnp.concatenate([kprev[...], k[cur]], axis=0)
                vcat = jnp.concatenate([vprev[...], v[cur]], axis=0)
                bias = bias_first
            else:
                kcat = k[(sb - 1) * w:(sb + 1) * w]
                vcat = v[(sb - 1) * w:(sb + 1) * w]
                bias = bias_band
            k_split = _half_split(kcat, pltpu.roll(kcat, HEAD_DIM, 1), lo)
            k_stack = [jnp.concatenate(k_split[2 * g:2 * g + 2], axis=0) for g in range(N_KV_HEADS)]
            vt = vcat.T
            zero = jnp.zeros((HEAD_DIM, 2 * w), F32)
            v_stack = []
            for g in range(N_KV_HEADS):
                vg = vt[g * HEAD_DIM:(g + 1) * HEAD_DIM]
                v_stack.append(jnp.concatenate(
                    [jnp.concatenate([vg, zero], axis=1), jnp.concatenate([zero, vg], axis=1), sum_rows],
                    axis=0).astype(BF))
            operands[sb] = (k_stack, v_stack, jnp.concatenate([bias, bias], axis=1))
        return operands[sb]

    def scores(sb, grp):
        k_stack, _, _ = sub_block_operands(sb)
        cur = slice(sb * w, (sb + 1) * w)
        qq = jnp.concatenate([q[2 * grp][cur], q[2 * grp + 1][cur]], axis=0)
        return _dot_nt(k_stack[grp], qq)

    def weighted_values(sb, grp, s):
        _, v_stack, bias2 = sub_block_operands(sb)
        p_e, sink_e = _softmax_terms_t(s[:2 * w], bias2, sink_row(4 * grp, 4 * grp + 2))
        p_o, sink_o = _softmax_terms_t(s[2 * w:], bias2, sink_row(4 * grp + 1, 4 * grp + 3))
        d = _dot(v_stack[grp], jnp.concatenate([p_e, p_o], axis=0))
        heads = 2 * HEAD_DIM
        o2 = jnp.concatenate(
            [d[:HEAD_DIM] * (1.0 / (d[heads:heads + 1] + sink_e)),
             d[HEAD_DIM:heads] * (1.0 / (d[heads + 1:heads + 2] + sink_o))], axis=0)
        return [o2[:, :w].T, o2[:, w:].T]

    chains = [(sb, grp) for sb in range(n_sub) for grp in range(N_KV_HEADS)]
    gate_cols, outs = [], []
    s_next = scores(*chains[0])
    for i, (sb, grp) in enumerate(chains):
        s_cur = s_next
        gate_cols.append(_gate_cols(hn, win_ref, bg_ref, slice(i * gate_w, (i + 1) * gate_w)))
        if i + 1 < len(chains):
            s_next = scores(*chains[i + 1])
        outs += weighted_values(sb, grp, s_cur)
    per_sb = 2 * N_KV_HEADS
    attn = jnp.concatenate(
        [jnp.concatenate(outs[sb * per_sb:(sb + 1) * per_sb], axis=1) for sb in range(n_sub)],
        axis=0).astype(BF)
    gates = jnp.concatenate(gate_cols, axis=1)

    kprev[...] = k[MIX_TB - w:]
    vprev[...] = v[MIX_TB - w:]
    kwin_ref[...] = k[MIX_TB - w:]
    vwin_ref[...] = v[MIX_TB - w:]
    x1_ref[...] = _merge_out(x, gates, _dot(attn, wba_ref[...]), ssm_proj, wo_ref)


def _mix_prompt(x, y_ssm, rc, rs1, rs2, sinks, g1, w_in, bg, wba, wbs, wo, wglu, bglu):
    b, t, _ = x.shape
    nblk = t // MIX_TB
    x2 = x.reshape(b * t, D_MODEL)
    row_map = lambda i, j: (i * nblk + j, 0)
    x1, kwin, vwin, w_in_bf = pl.pallas_call(
        _mix_prompt_kernel,
        grid=(b, nblk),
        in_specs=[
            pl.BlockSpec(memory_space=pltpu.SMEM),
            pl.BlockSpec((MIX_TB, D_MODEL), row_map),
            pl.BlockSpec((MIX_TB, SSM_WIDTH), row_map),
            pl.BlockSpec((MIX_TB, LANES), lambda i, j: (j, 0)),
            pl.BlockSpec((MIX_TB, LANES), lambda i, j: (j, 0)),
            pl.BlockSpec((MIX_TB, LANES), lambda i, j: (j, 0)),
            _const_spec((1, D_MODEL)),
            pl.BlockSpec(memory_space=pl.ANY),
            _const_spec(bg.shape),
            _const_spec(wba.shape),
            _const_spec(wbs.shape),
            _const_spec(wo.shape),
            _const_spec(wglu.shape),
            _const_spec(bglu.shape),
        ],
        out_specs=[
            pl.BlockSpec((MIX_TB, D_MODEL), row_map),
            pl.BlockSpec((None, WINDOW, KV_WIDTH), lambda i, j: (i, 0, 0)),
            pl.BlockSpec((None, WINDOW, KV_WIDTH), lambda i, j: (i, 0, 0)),
            pl.BlockSpec(memory_space=pl.ANY),
        ],
        out_shape=[
            jax.ShapeDtypeStruct((b * t, D_MODEL), F32),
            jax.ShapeDtypeStruct((b, WINDOW, KV_WIDTH), F32),
            jax.ShapeDtypeStruct((b, WINDOW, KV_WIDTH), F32),
            jax.ShapeDtypeStruct(w_in.shape, BF),
        ],
        scratch_shapes=[pltpu.VMEM((WINDOW, KV_WIDTH), F32), pltpu.VMEM((WINDOW, KV_WIDTH), F32),
                        pltpu.VMEM(w_in.shape, BF),
                        pltpu.VMEM((FFN_W_SLOTS, D_MODEL // FFN_W_CHUNKS, IN_WIDTH), F32),
                        pltpu.SemaphoreType.DMA((FFN_W_SLOTS,)), pltpu.SemaphoreType.DMA((1,))],
        compiler_params=pltpu.CompilerParams(
            dimension_semantics=("arbitrary", "arbitrary"), vmem_limit_bytes=VMEM_LIMIT),
        name="mix_prompt",
    )(sinks, x2, y_ssm, rc, rs1, rs2, g1, w_in, bg, wba, wbs, wo, wglu, bglu)
    return x1, kwin, vwin, w_in_bf


QROWS = 16
SAMPLE_TB = 32


def _sample_layer_kernel(x_ref, kbuf_ref, vbuf_ref, rc_ref, rs1_ref, rs2_ref, sinkc_ref, g1_ref, wqkv_ref,
                         xall_ref, h0re_ref, h0im_ref, bg_ref, lb_ref, d_ref, wglu_ref, bglu_ref,
                         win_hbm, wba_hbm, wbs_hbm, wo_hbm, bblk_hbm, ctblk_hbm,
                         kout_ref, vout_ref, x1_ref, hre_ref, him_ref,
                         qz, o3_all, win_s, wba_s, wbs_s, wo_s, bblk_s, ctblk_s, sem):
    nb = x_ref.shape[0]
    step = pl.program_id(0)
    last_step = pl.num_programs(0) - 1
    late = ((win_hbm, win_s), (wba_hbm, wba_s), (wbs_hbm, wbs_s), (wo_hbm, wo_s), (bblk_hbm, bblk_s),
            (ctblk_hbm, ctblk_s))
    late_copies = [pltpu.make_async_copy(src, dst, sem.at[i]) for i, (src, dst) in enumerate(late)]

    @pl.when(step == 0)
    def _():
        for copy in late_copies:
            copy.start()

    hn = _rms(x_ref[...], g1_ref[...]).astype(BF)
    qkv = _dot(hn, wqkv_ref[...])
    rc, rs1, rs2 = rc_ref[...], rs1_ref[...], rs2_ref[...]
    scale = HEAD_DIM ** -0.5
    k_new = _rope(qkv[:, ATTN_WIDTH:ATTN_WIDTH + KV_WIDTH], rc, rs1, rs2)
    v_new = qkv[:, ATTN_WIDTH + KV_WIDTH:ATTN_WIDTH + 2 * KV_WIDTH]
    pad = jnp.zeros((LANES - nb, KV_WIDTH), F32)
    k_new_t = jnp.concatenate([k_new, pad], axis=0).T
    v_new_t = jnp.concatenate([v_new, pad], axis=0).T

    lane = lax.broadcasted_iota(jnp.int32, (nb, LANES), 1)
    lo = lane < HEAD_DIM
    qz[...] = jnp.zeros_like(qz)
    for c in range(ATTN_WIDTH // LANES):
        qc = _rope(qkv[:, c * LANES:(c + 1) * LANES], rc, rs1, rs2) * scale
        qr = pltpu.roll(qc, HEAD_DIM, 1)
        zero = jnp.zeros_like(qc)
        if c < 2:
            even, odd = jnp.where(lo, qc, zero), jnp.where(lo, qr, zero)
        else:
            even, odd = jnp.where(lo, zero, qr), jnp.where(lo, zero, qc)
        qz[pl.ds(2 * c, nb, stride=QROWS), :] = even
        qz[pl.ds(2 * c + 1, nb, stride=QROWS), :] = odd

    last = lax.broadcasted_iota(jnp.int32, (KV_WIDTH, WINDOW), 1) == WINDOW - 1
    for b in range(nb):
        kout_ref[b] = jnp.where(last, k_new_t[:, b:b + 1], pltpu.roll(kbuf_ref[b], WINDOW - 1, 1))
        vout_ref[b] = jnp.where(last, v_new_t[:, b:b + 1], pltpu.roll(vbuf_ref[b], WINDOW - 1, 1))

    sink = sinkc_ref[...]
    q3 = qz[...].reshape(nb, QROWS, LANES).astype(BF)
    s = jnp.einsum('bhd,bdk->bhk', q3, kout_ref[...].astype(BF), preferred_element_type=F32)
    m = jnp.maximum(jnp.max(s, axis=-1, keepdims=True), sink)
    p = jnp.exp(s - m)
    den = jnp.sum(p, axis=-1, keepdims=True) + jnp.exp(sink - m)
    p = (p * (1.0 / den)).astype(BF)
    o3 = jnp.einsum('bhk,bdk->bhd', p, vout_ref[...].astype(BF), preferred_element_type=F32)
    o3_all[pl.ds(pl.multiple_of(step * (nb * QROWS), nb * QROWS), nb * QROWS), :] = o3.reshape(nb * QROWS, LANES)

    @pl.when(step == last_step)
    def _():
        for copy in late_copies:
            copy.wait()
        _sample_tail(xall_ref, o3_all, h0re_ref, h0im_ref, g1_ref, win_s, bg_ref, wba_s, wbs_s, wo_s,
                     lb_ref, bblk_s, ctblk_s, d_ref, wglu_ref, bglu_ref, x1_ref, hre_ref, him_ref)


def _sample_layer(x, kbuf, vbuf, h0re, h0im, rc, rs1, rs2, sinkc, g1, w_in, bg, wba, wbs, wo, lb, bblk, ctblk,
                  d, wglu, bglu):
    nb = x.shape[0]
    tb = SAMPLE_TB
    blocked = (x, kbuf, vbuf)
    early = (rc, rs1, rs2, sinkc, g1)
    consts = (x, h0re, h0im, bg, lb, d, wglu, bglu)
    late = (w_in, wba, wbs, wo, bblk, ctblk)
    kv_spec = pl.BlockSpec((tb, KV_WIDTH, WINDOW), lambda i: (i, 0, 0))
    whole = lambda shape: pl.BlockSpec(shape, lambda i: (0,) * len(shape))
    return pl.pallas_call(
        _sample_layer_kernel,
        grid=(nb // tb,),
        in_specs=[pl.BlockSpec((tb, D_MODEL), lambda i: (i, 0)), kv_spec, kv_spec]
        + [_const_spec(a.shape) for a in early]
        + [pl.BlockSpec((D_MODEL, QKV_WIDTH), lambda i: (0, 0), pipeline_mode=pl.Buffered(1))]
        + [_const_spec(a.shape) for a in consts]
        + [pl.BlockSpec(memory_space=pl.ANY) for _ in late],
        out_specs=[kv_spec, kv_spec, whole((nb, D_MODEL)), whole((N_STATES, nb)), whole((N_STATES, nb))],
        out_shape=[
            jax.ShapeDtypeStruct((nb, KV_WIDTH, WINDOW), F32),
            jax.ShapeDtypeStruct((nb, KV_WIDTH, WINDOW), F32),
            jax.ShapeDtypeStruct((nb, D_MODEL), F32),
            jax.ShapeDtypeStruct((N_STATES, nb), F32),
            jax.ShapeDtypeStruct((N_STATES, nb), F32),
        ],
        scratch_shapes=[pltpu.VMEM((tb * QROWS, LANES), F32), pltpu.VMEM((nb * QROWS, LANES), F32)]
        + [pltpu.VMEM(a.shape, a.dtype) for a in late] + [pltpu.SemaphoreType.DMA((len(late),))],
        compiler_params=pltpu.CompilerParams(
            dimension_semantics=("arbitrary",), vmem_limit_bytes=VMEM_LIMIT),
        name="sample_layer",
    )(*blocked, *early, w_in, *consts, *late)


def _sample_tail(x_ref, o3_ref, h0re_ref, h0im_ref, g1_ref, win_ref, bg_ref, wba_ref,
                 wbs_ref, wo_ref, lb_ref, bblk_ref, ctblk_ref, d_ref, wglu_ref, bglu_ref,
                 x1_ref, hre_ref, him_ref):
    nb = x_ref.shape[0]
    x = x_ref[...]
    hn = _rms(x, g1_ref[...]).astype(BF)

    lane = lax.broadcasted_iota(jnp.int32, (nb, LANES), 1)
    lo = lane < HEAD_DIM
    a = jnp.zeros((nb, D_MODEL), F32)
    zero = jnp.zeros((nb, LANES), F32)
    for h in range(N_Q_HEADS):
        oh = o3_ref[pl.ds(h, nb, stride=QROWS), :]
        oh = jnp.where(lo, oh, zero) if h < N_Q_HEADS // 2 else jnp.where(lo, zero, oh)
        w_h = wba_ref[h * HEAD_DIM:(h + 1) * HEAD_DIM, :]
        a = a + _dot(oh.astype(BF), jnp.concatenate([w_h, w_h], axis=0))

    u = _dot(hn, win_ref[:, U_COL0:GATE_COL0])
    ub = u.astype(BF)
    lre, lim = lb_ref[0:1, :], lb_ref[1:2, :]
    y_cols = []
    for o in range(N_OCT):
        sl = slice(o * OCT_STATES, (o + 1) * OCT_STATES)
        bu = _dot(ub[:, o * LANES:(o + 1) * LANES], bblk_ref[o].astype(BF))
        blocks = [slice(c * LANES, (c + 1) * LANES) for c in range(o * OCT_COL, (o + 1) * OCT_COL)]
        h0r = jnp.concatenate([h0re_ref[rows, :].T for rows in blocks], axis=1)
        h0i = jnp.concatenate([h0im_ref[rows, :].T for rows in blocks], axis=1)
        hr = bu[:, :OCT_STATES] + (lre[:, sl] * h0r - lim[:, sl] * h0i)
        hi = bu[:, OCT_STATES:] + (lre[:, sl] * h0i + lim[:, sl] * h0r)
        for cc, rows in enumerate(blocks):
            hre_ref[rows, :] = hr[:, cc * LANES:(cc + 1) * LANES].T
            him_ref[rows, :] = hi[:, cc * LANES:(cc + 1) * LANES].T
        y_cols.append(_dot_nt(jnp.concatenate([hr, -hi], axis=1).astype(BF), ctblk_ref[o].astype(BF)))
    y = jnp.concatenate(y_cols, axis=1) + d_ref[...] * u
    z = _gelu_tanh(y)
    gate = _dot(z.astype(BF), wglu_ref[...]) + bglu_ref[...]
    ssm = (z * _sigmoid(gate)).astype(BF)

    gates = _gate_cols(hn, win_ref, bg_ref, slice(0, GATE_WIDTH))
    x1_ref[...] = _merge_out(x, gates, a, _dot(ssm, wbs_ref[...]), wo_ref)


FFN_W_CHUNKS = 16
FFN_W_SLOTS = 4


def _stream_cast_weights(jobs):
    tasks, used = [], {}
    for w_hbm, w_bf, stage, sem in jobs:
        rows = stage.shape[1]
        for k in range(w_hbm.shape[0] // rows):
            slot = used.get(id(stage), 0) % stage.shape[0]
            used[id(stage)] = used.get(id(stage), 0) + 1
            copy = pltpu.make_async_copy(w_hbm.at[pl.ds(k * rows, rows), :], stage.at[slot], sem.at[slot])
            tasks.append((copy, w_bf, stage, slot, k * rows, rows))
    ahead = min(stage.shape[0] for _, _, stage, _ in jobs) - 1
    for copy, *_ in tasks[:ahead]:
        copy.start()
    for i, (copy, w_bf, stage, slot, row0, rows) in enumerate(tasks):
        if i + ahead < len(tasks):
            tasks[i + ahead][0].start()
        copy.wait()
        w_bf[row0:row0 + rows, :] = stage[slot].astype(BF)


def _ffn_rows(x, g2_ref, wgate, wup, wdown, gf_ref):
    h = _rms(x, g2_ref[...]).astype(BF)
    gate = _dot(h, wgate[...])
    up = _dot(h, wup[...])
    half_gate = 0.5 * gate
    act = ((half_gate + half_gate * jnp.tanh(half_gate)) * up).astype(BF)
    x2 = x + _dot(act, wdown[...])
    return _rms(x2, gf_ref[...])


def _ffn_kernel(xp_ref, xs_ref, g2_ref, wgate_hbm, wup_hbm, wdown_hbm, gf_ref, yp_ref, ys_ref,
                wgate, wup, wdown, stage_in, stage_out, sem_in, sem_out):
    i = pl.program_id(0)
    n_prompt = pl.num_programs(0) - 1

    @pl.when(i == 0)
    def _():
        _stream_cast_weights([(wgate_hbm, wgate, stage_in, sem_in), (wup_hbm, wup, stage_in, sem_in),
                              (wdown_hbm, wdown, stage_out, sem_out)])

    @pl.when(i < n_prompt)
    def _():
        half = FFN_TB // FFN_SPLIT
        for rows in (slice(h * half, (h + 1) * half) for h in range(FFN_SPLIT)):
            yp_ref[rows, :] = _ffn_rows(xp_ref[rows, :], g2_ref, wgate, wup, wdown, gf_ref)

    @pl.when(i == n_prompt)
    def _():
        ys_ref[...] = _ffn_rows(xs_ref[...], g2_ref, wgate, wup, wdown, gf_ref)


def _ffn(xp, xs, g2, wgate, wup, wdown, gf):
    n, ns = xp.shape[0], xs.shape[0]
    n_prompt = n // FFN_TB
    prompt_map = lambda i: (jnp.minimum(i, n_prompt - 1), 0)
    return pl.pallas_call(
        _ffn_kernel,
        grid=(n_prompt + 1,),
        in_specs=[
            pl.BlockSpec((FFN_TB, D_MODEL), prompt_map),
            _const_spec((ns, D_MODEL)),
            _const_spec((1, D_MODEL)),
            pl.BlockSpec(memory_space=pl.ANY),
            pl.BlockSpec(memory_space=pl.ANY),
            pl.BlockSpec(memory_space=pl.ANY),
            _const_spec((1, D_MODEL)),
        ],
        out_specs=[
            pl.BlockSpec((FFN_TB, D_MODEL), prompt_map),
            pl.BlockSpec((ns, D_MODEL), lambda i: (0, 0)),
        ],
        out_shape=[
            jax.ShapeDtypeStruct((n, D_MODEL), F32),
            jax.ShapeDtypeStruct((ns, D_MODEL), F32),
        ],
        scratch_shapes=[
            pltpu.VMEM((D_MODEL, D_FF), BF), pltpu.VMEM((D_MODEL, D_FF), BF), pltpu.VMEM((D_FF, D_MODEL), BF),
            pltpu.VMEM((FFN_W_SLOTS, D_MODEL // FFN_W_CHUNKS, D_FF), F32),
            pltpu.VMEM((FFN_W_SLOTS, D_FF // FFN_W_CHUNKS, D_MODEL), F32),
            pltpu.SemaphoreType.DMA((FFN_W_SLOTS,)), pltpu.SemaphoreType.DMA((FFN_W_SLOTS,)),
        ],
        compiler_params=pltpu.CompilerParams(
            dimension_semantics=("arbitrary",), vmem_limit_bytes=VMEM_LIMIT),
        name="ffn",
    )(xp, xs, g2, wgate, wup, wdown, gf)


def _rope_tables(pos):
    pos = np.asarray(pos, np.float64)
    inv_freq = ROPE_THETA ** (-(np.arange(ROPE_HALF, dtype=np.float64) * 2.0 / ROPE_DIM))
    ang = pos[:, None] * inv_freq[None, :]
    cos, sin = np.cos(ang), np.sin(ang)
    pad = np.zeros((pos.shape[0], HEAD_DIM - ROPE_DIM))
    zero = np.zeros_like(sin)
    rc = np.concatenate([cos, cos, pad + 1.0], axis=1)
    rs1 = np.concatenate([zero, sin, pad], axis=1)
    rs2 = np.concatenate([-sin, zero, pad], axis=1)
    rep = LANES // HEAD_DIM
    return tuple(jnp.asarray(np.tile(a, (1, rep)), F32) for a in (rc, rs1, rs2))


def _cmul(ar, ai, br, bi):
    return ar * br - ai * bi, ar * bi + ai * br


def _ssm_tables(lam_re, lam_im, log_dt, b_re, b_im, c_re, c_im):
    dt = jnp.exp(log_dt)[:, None]
    mag = jnp.exp(lam_re * dt)
    lb_re = mag * jnp.cos(lam_im * dt)
    lb_im = mag * jnp.sin(lam_im * dt)
    den = lam_re * lam_re + lam_im * lam_im
    nr = lb_re - 1.0
    k_re = ((nr * lam_re + lb_im * lam_im) / den)[..., None]
    k_im = ((lb_im * lam_re - nr * lam_im) / den)[..., None]
    bb_re = k_re * b_re - k_im * b_im
    bb_im = k_re * b_im + k_im * b_re

    a_re, a_im = lb_re, lb_im
    for _ in range(int(math.log2(CHUNK))):
        a_re, a_im = _cmul(a_re, a_im, a_re, a_im)
    s_re, s_im = a_re, a_im
    for _ in range(int(math.log2(SEG))):
        s_re, s_im = _cmul(s_re, s_im, s_re, s_im)

    eye = jnp.eye(OCT, dtype=F32).reshape(1, OCT, 1, OCT, 1)

    def block_diag(a):
        r, c = a.shape[1:]
        return (a.reshape(N_OCT, OCT, r, 1, c) * eye).reshape(N_OCT, OCT * r, OCT * c)

    bblk = jnp.concatenate([block_diag(jnp.swapaxes(bb_re, 1, 2)),
                            block_diag(jnp.swapaxes(bb_im, 1, 2))], axis=2)
    ctblk = jnp.concatenate([block_diag(c_re), block_diag(c_im)], axis=2)

    oct_cols = lambda a: a.reshape(N_OCT, 1, OCT_STATES)
    lcol = jnp.concatenate([oct_cols(lb_re), oct_cols(lb_im),
                            jnp.zeros((N_OCT, SUBLANES - 2, OCT_STATES), F32)], axis=1)

    flat = lambda a: a.reshape(1, N_STATES)
    col = lambda a: a.reshape(N_COL, 1, LANES)
    a_tab = jnp.concatenate([col(a_re), col(a_im), col(s_re), col(s_im),
                             jnp.zeros((N_COL, SUBLANES - 4, LANES), F32)], axis=1)
    lb = jnp.concatenate([flat(lb_re), flat(lb_im)], axis=0)
    return bblk, ctblk, lcol, a_tab, lb


def kernel(x_prompt, x_sample, state_k_win, state_v_win, state_ssm_re, state_ssm_im, norm1_g, w_in, b_gate, attn_sinks, ssm_lam_re, ssm_lam_im, ssm_log_dt, ssm_b_re, ssm_b_im, ssm_c_re, ssm_c_im, ssm_d, w_glu, b_glu, w_branch_attn, w_branch_ssm, w_out, norm2_g, w_ffn_gate, w_ffn_up, w_ffn_down, norm_f_g):
    depth = w_in.shape[0]
    assert depth == 1
    b, t, _ = x_prompt.shape
    nb, s_len, _ = x_sample.shape
    assert s_len == 1 and state_k_win.shape[2] == WINDOW
    l = 0
    assert w_in.shape[2] == IN_WIDTH
    w_in_f = w_in[l]
    g1 = norm1_g[l].reshape(1, D_MODEL)
    g2 = norm2_g[l].reshape(1, D_MODEL)
    gf = norm_f_g.reshape(1, D_MODEL)
    bg = b_gate[l].reshape(1, GATE_WIDTH)
    d = ssm_d[l].reshape(1, SSM_WIDTH)
    wglu = w_glu[l].astype(BF)
    bglu = b_glu[l].reshape(1, SSM_WIDTH)
    wba = w_branch_attn[l].astype(BF)
    wbs = w_branch_ssm[l].astype(BF)
    wo = w_out[l].astype(BF)
    sinks = attn_sinks[l]

    bblk, ctblk, lcol, a_tab, lb = _ssm_tables(
        ssm_lam_re[l], ssm_lam_im[l], ssm_log_dt[l], ssm_b_re[l], ssm_b_im[l], ssm_c_re[l], ssm_c_im[l])

    rc, rs1, rs2 = _rope_tables(np.arange(t))
    yssm_p, hre_p, him_p = _s5_prompt(x_prompt, g1, w_in_f, bblk, ctblk, lcol, a_tab, d)
    x1_p, kwin_p, vwin_p, w_in_b = _mix_prompt(x_prompt, yssm_p, rc, rs1, rs2, sinks, g1, w_in_f, bg, wba, wbs, wo,
                                               wglu, bglu)

    rcs, rs1s, rs2s = _rope_tables(PAST_LEN + np.arange(1))
    sinkc = jnp.concatenate([sinks, jnp.zeros((QROWS - N_Q_HEADS,), F32)]).reshape(QROWS, 1)
    xs = x_sample.reshape(nb, D_MODEL)
    key_minor = lambda a: jnp.swapaxes(a.reshape(nb, WINDOW, KV_WIDTH), 1, 2)
    kwin_s, vwin_s, x1_s, hre_s, him_s = _sample_layer(
        xs, key_minor(state_k_win[l]), key_minor(state_v_win[l]),
        state_ssm_re[l].reshape(nb, N_STATES).T, state_ssm_im[l].reshape(nb, N_STATES).T,
        rcs, rs1s, rs2s, sinkc, g1, w_in_b, bg, wba, wbs, wo, lb, bblk, ctblk,
        d, wglu, bglu)
    y_p, y_s = _ffn(x1_p, x1_s, g2, w_ffn_gate[l], w_ffn_up[l], w_ffn_down[l], gf)
    y_p = y_p.reshape(b, t, D_MODEL)
    y_s = y_s.reshape(nb, 1, D_MODEL)

    kv_shape_p = (1, b, WINDOW, N_KV_HEADS, HEAD_DIM)
    st_shape_p = (1, b, N_SSM_GROUPS, SSM_STATE)
    kv_shape_s = (1, nb, WINDOW, N_KV_HEADS, HEAD_DIM)
    st_shape_s = (1, nb, N_SSM_GROUPS, SSM_STATE)
    return (y_p, y_s,
            kwin_p.reshape(kv_shape_p), vwin_p.reshape(kv_shape_p),
            hre_p.reshape(st_shape_p), him_p.reshape(st_shape_p),
            jnp.swapaxes(kwin_s, 1, 2).reshape(kv_shape_s), jnp.swapaxes(vwin_s, 1, 2).reshape(kv_shape_s),
            hre_s.T.reshape(st_shape_s), him_s.T.reshape(st_shape_s))
```

```python
import math

import jax
import jax.numpy as jnp
import numpy as np
from jax import lax
from jax.experimental import pallas as pl
from jax.experimental.pallas import tpu as pltpu

D_MODEL = 1024
N_Q_HEADS = 8
N_KV_HEADS = 2
HEAD_DIM = 64
ATTN_WIDTH = N_Q_HEADS * HEAD_DIM
KV_WIDTH = N_KV_HEADS * HEAD_DIM
WINDOW = 128
ROPE_DIM = HEAD_DIM // 4
ROPE_HALF = ROPE_DIM // 2
ROPE_THETA = 500000.0
SSM_WIDTH = D_MODEL // 2
SSM_GROUP = 16
N_SSM_GROUPS = SSM_WIDTH // SSM_GROUP
SSM_STATE = 64
N_STATES = N_SSM_GROUPS * SSM_STATE
GATE_WIDTH = 2 * D_MODEL
QKV_WIDTH = ATTN_WIDTH + 2 * KV_WIDTH
U_COL0 = QKV_WIDTH
GATE_COL0 = U_COL0 + SSM_WIDTH
IN_WIDTH = GATE_COL0 + GATE_WIDTH
D_FF = -(-8 * D_MODEL // (3 * 256)) * 256
NORM_EPS = 1e-5
PAST_LEN = 8192

LANES = 128
SUBLANES = 8
CHUNK = 8
OCT = LANES // SSM_GROUP
N_OCT = N_SSM_GROUPS // OCT
OCT_STATES = OCT * SSM_STATE
OCT_COL = OCT_STATES // LANES
N_COL = N_STATES // LANES
N_SEG = SUBLANES
SEG = 16
S5_ROWS = N_SEG * SEG
M_PAIR = 2
U_HALF = SSM_WIDTH // 2
SEG_PITCH = SEG * CHUNK + SUBLANES
MIX_TB = 512
FFN_TB = 1024
FFN_SPLIT = 4
NEG_BIG = -1e30
LOG2_E = math.log2(math.e)
VMEM_LIMIT = 56 * 1024 * 1024

BF = jnp.bfloat16
F32 = jnp.float32


def _dot(a, b):
    return jnp.dot(a, b, preferred_element_type=F32)


def _dot_nt(a, b):
    return lax.dot_general(a, b, (((1,), (1,)), ((), ())), preferred_element_type=F32)


def _dot_nt_split(a, b):
    a_hi, b_hi = a.astype(BF), b.astype(BF)
    a_lo = (a - a_hi.astype(F32)).astype(BF)
    b_lo = (b - b_hi.astype(F32)).astype(BF)
    return _dot_nt(a_hi, b_hi) + (_dot_nt(a_hi, b_lo) + _dot_nt(a_lo, b_hi))


def _rms(x, g):
    return x * lax.rsqrt(jnp.mean(x * x, axis=-1, keepdims=True) + NORM_EPS) * g


def _sigmoid(x):
    return 1.0 / (1.0 + jnp.exp(-x))


def _gelu_tanh(x):
    c = math.sqrt(2.0 / math.pi)
    return 0.5 * x * (1.0 + jnp.tanh(c * (x + 0.044715 * (x * x * x))))


def _rope(a, rc, rs1, rs2):
    return a * rc + pltpu.roll(a, ROPE_HALF, 1) * rs1 + pltpu.roll(a, LANES - ROPE_HALF, 1) * rs2


def _const_spec(shape):
    nd = len(shape)
    return pl.BlockSpec(shape, lambda *_: (0,) * nd, pipeline_mode=pl.Buffered(1))


def _build_chunk_operators(bblk_ref, ctblk_ref, lcol_ref, m_s, e_s, f_s):
    for o in range(N_OCT):
        ct = jnp.concatenate([ctblk_ref[o, :, :OCT_STATES], -ctblk_ref[o, :, OCT_STATES:]],
                             axis=1)
        lr, li = lcol_ref[o, 0:1, :], lcol_ref[o, 1:2, :]
        er, ei = bblk_ref[o, :, :OCT_STATES], bblk_ref[o, :, OCT_STATES:]
        k_blk = []
        for tau in range(CHUNK):
            e_cat = jnp.concatenate([er, ei], axis=1)
            i = CHUNK - 1 - tau
            e_s[o, i * LANES:(i + 1) * LANES, :] = e_cat.astype(BF)
            k_blk.append(_dot_nt_split(e_cat, ct).astype(BF))
            er, ei = er * lr - ei * li, er * li + ei * lr
        zero = jnp.zeros((LANES, LANES), BF)
        for j in range(CHUNK):
            jt, jj = divmod(j, M_PAIR)
            for i in range(M_PAIR * (jt + 1)):
                m_s[jt][o, i * LANES:(i + 1) * LANES, jj * LANES:(jj + 1) * LANES] = (
                    k_blk[j - i] if j >= i else zero)
        tr, ti = ct[:, :OCT_STATES], -ct[:, OCT_STATES:]
        for j in range(CHUNK):
            tr, ti = tr * lr - ti * li, tr * li + ti * lr
            f_s[o, :OCT_STATES, j * LANES:(j + 1) * LANES] = tr.T.astype(BF)
            f_s[o, OCT_STATES:, j * LANES:(j + 1) * LANES] = (-ti).T.astype(BF)


def _s5_prompt_kernel(x_ref, g1_ref, wu0_ref, wu1_ref, bblk_ref, ctblk_ref, lcol_ref, a_ref, d_ref,
                      out_ref, hfin_ref,
                      m0_ref, m1_ref, m2_ref, m3_ref, e_ref, f_ref, us, ys, sre, sim, car):
    m_ref = (m0_ref, m1_ref, m2_ref, m3_ref)
    blk = pl.program_id(1)

    @pl.when((pl.program_id(0) == 0) & (blk == 0))
    def _():
        _build_chunk_operators(bblk_ref, ctblk_ref, lcol_ref, m_ref, e_ref, f_ref)

    @pl.when(blk == 0)
    def _():
        car[...] = jnp.zeros_like(car)

    seg_tokens = SEG * CHUNK
    seg_rows = [slice(s * SEG_PITCH, s * SEG_PITCH + seg_tokens) for s in range(N_SEG)]
    tok_rows = [slice(s * seg_tokens, (s + 1) * seg_tokens) for s in range(N_SEG)]

    for s in range(0, N_SEG, 2):
        hn = _rms(x_ref[s * seg_tokens:(s + 2) * seg_tokens, :], g1_ref[...]).astype(BF)
        u = jnp.concatenate([_dot(hn, wu0_ref[...].astype(BF)), _dot(hn, wu1_ref[...].astype(BF))],
                            axis=1)
        for half in range(2):
            for cc in range(N_OCT):
                us[cc, seg_rows[s + half], :] = u[tok_rows[half], cc * LANES:(cc + 1) * LANES]

    sub = lax.broadcasted_iota(jnp.int32, (N_SEG, LANES), 0)

    def scan_column(c):
        tab = a_ref[c]
        are, aim = tab[0:1], tab[1:2]
        bre, bim = tab[2:3], tab[3:4]

        def step(cr, ci, r, keep_entering):
            slab = slice(r * N_SEG, (r + 1) * N_SEG)
            s_r, s_i = sre[c, slab, :], sim[c, slab, :]
            if keep_entering:
                sre[c, slab, :] = cr
                sim[c, slab, :] = ci
            return are * cr - aim * ci + s_r, are * ci + aim * cr + s_i

        cr = jnp.zeros((N_SEG, LANES), F32)
        ci = jnp.zeros((N_SEG, LANES), F32)
        for r in range(SEG):
            cr, ci = step(cr, ci, r, False)
        cv = car[c]
        pr, pi = cv[0:1], cv[1:2]
        sr = jnp.zeros((N_SEG, LANES), F32)
        si = jnp.zeros((N_SEG, LANES), F32)
        for s in range(N_SEG):
            sr = jnp.where(sub == s, pr, sr)
            si = jnp.where(sub == s, pi, si)
            pr, pi = (bre * pr - bim * pi + cr[s:s + 1], bre * pi + bim * pr + ci[s:s + 1])
        end = jnp.where(sub == 0, pr, jnp.where(sub == 1, pi, 0.0))
        car[c] = end
        hfin_ref[c] = end
        cr, ci = sr, si
        for r in range(SEG):
            cr, ci = step(cr, ci, r, True)

    for o in range(N_OCT):
        uo = jnp.concatenate(
            [jnp.concatenate([us[o, pl.ds(r * CHUNK + i, N_SEG, stride=SEG_PITCH), :] for r in range(SEG)],
                             axis=0).astype(BF) for i in range(CHUNK)], axis=1)
        s_end = _dot(uo, e_ref[o])
        cols = range(o * OCT_COL, (o + 1) * OCT_COL)
        for cc, c in enumerate(cols):
            sre[c] = s_end[:, cc * LANES:(cc + 1) * LANES]
            sim[c] = s_end[:, OCT_STATES + cc * LANES:OCT_STATES + (cc + 1) * LANES]
        y_in = jnp.concatenate(
            [_dot(uo[:, :(jt + 1) * M_PAIR * LANES], m_ref[jt][o]) for jt in range(CHUNK // M_PAIR)],
            axis=1)
        for c in cols:
            scan_column(c)
        hp = jnp.concatenate([sre[c] for c in cols] + [sim[c] for c in cols], axis=1).astype(BF)
        yo = y_in + _dot(hp, f_ref[o])
        for r in range(SEG):
            for j in range(CHUNK):
                ys[o, pl.ds(r * CHUNK + j, N_SEG, stride=SEG_PITCH), :] = (
                    yo[r * N_SEG:(r + 1) * N_SEG, j * LANES:(j + 1) * LANES])
        d_o = d_ref[:, o * LANES:(o + 1) * LANES]
        for s in range(N_SEG):
            out_ref[tok_rows[s], o * LANES:(o + 1) * LANES] = (
                ys[o, seg_rows[s], :] + d_o * us[o, seg_rows[s], :])


def _s5_prompt(x, g1, w_in, bblk, ctblk, lcol, a_tab, d):
    b, t, _ = x.shape
    tb = S5_ROWS * CHUNK
    nblk = t // tb
    x2 = x.reshape(b * t, D_MODEL)
    row_map = lambda i, j: (i * nblk + j, 0)
    op_shape = (N_OCT, CHUNK * LANES, CHUNK * LANES)
    out, hfin = pl.pallas_call(
        _s5_prompt_kernel,
        grid=(b, nblk),
        in_specs=[
            pl.BlockSpec((tb, D_MODEL), row_map),
            _const_spec((1, D_MODEL)),
            *[pl.BlockSpec((D_MODEL, U_HALF), lambda i, j, c=U_COL0 // U_HALF + h: (0, c),
                           pipeline_mode=pl.Buffered(1)) for h in range(2)],
            _const_spec(bblk.shape),
            _const_spec(ctblk.shape),
            _const_spec(lcol.shape),
            _const_spec(a_tab.shape),
            _const_spec((1, SSM_WIDTH)),
        ],
        out_specs=[
            pl.BlockSpec((tb, SSM_WIDTH), row_map),
            pl.BlockSpec((None, N_COL, SUBLANES, LANES), lambda i, j: (i, 0, 0, 0)),
        ],
        out_shape=[
            jax.ShapeDtypeStruct((b * t, SSM_WIDTH), F32),
            jax.ShapeDtypeStruct((b, N_COL, SUBLANES, LANES), F32),
        ],
        scratch_shapes=[
            *[pltpu.VMEM((N_OCT, (jt + 1) * M_PAIR * LANES, M_PAIR * LANES), BF)
              for jt in range(CHUNK // M_PAIR)],
            pltpu.VMEM(op_shape, BF), pltpu.VMEM(op_shape, BF),
            pltpu.VMEM((N_OCT, N_SEG * SEG_PITCH, LANES), F32),
            pltpu.VMEM((N_OCT, N_SEG * SEG_PITCH, LANES), F32),
            pltpu.VMEM((N_COL, S5_ROWS, LANES), F32), pltpu.VMEM((N_COL, S5_ROWS, LANES), F32),
            pltpu.VMEM((N_COL, SUBLANES, LANES), F32),
        ],
        compiler_params=pltpu.CompilerParams(
            dimension_semantics=("arbitrary", "arbitrary"), vmem_limit_bytes=VMEM_LIMIT),
        name="s5_prompt",
    )(x2, g1, w_in, w_in, bblk, ctblk, lcol, a_tab, d)
    return out, hfin[:, :, 0, :], hfin[:, :, 1, :]


def _gate_cols(hn, win_ref, bg_ref, cols):
    w_cols = slice(GATE_COL0 + cols.start, GATE_COL0 + cols.stop)
    return _sigmoid(_dot(hn, win_ref[:, w_cols]) + bg_ref[:, cols])


def _merge_out(x, gates, attn_proj, ssm_proj, wo_ref):
    merged = gates[:, :D_MODEL] * attn_proj + gates[:, D_MODEL:] * ssm_proj
    return x + _dot(merged.astype(BF), wo_ref[...])


def _half_split(a, ar, lo):
    z = jnp.zeros_like(a)
    return (jnp.where(lo, a, z).astype(BF), jnp.where(lo, z, ar).astype(BF),
            jnp.where(lo, ar, z).astype(BF), jnp.where(lo, z, a).astype(BF))


def _softmax_terms_t(st, bias_t, sink):
    st = st + bias_t
    m = jnp.maximum(jnp.max(st, axis=0, keepdims=True), sink)
    return jnp.exp2(st - m).astype(BF), jnp.exp2(sink - m)


def _mix_prompt_kernel(sinks_ref, x_ref, y_ref, rc_ref, rs1_ref, rs2_ref, g1_ref, win_hbm,
                       bg_ref, wba_ref, wbs_ref, wo_ref, wglu_ref, bglu_ref,
                       x1_ref, kwin_ref, vwin_ref, winb_hbm, kprev, vprev, win_ref, stage, sem_stage, sem_out):
    t = pl.program_id(1)
    seq = pl.program_id(0)
    share_copy = pltpu.make_async_copy(win_ref, winb_hbm, sem_out.at[0])

    @pl.when((seq == 0) & (t == 0))
    def _():
        _stream_cast_weights([(win_hbm, win_ref, stage, sem_stage)])
        share_copy.start()

    @pl.when((seq == pl.num_programs(0) - 1) & (t == pl.num_programs(1) - 1))
    def _():
        share_copy.wait()

    @pl.when(t == 0)
    def _():
        kprev[...] = jnp.zeros_like(kprev)
        vprev[...] = jnp.zeros_like(vprev)

    x = x_ref[...]
    hn = _rms(x, g1_ref[...]).astype(BF)
    qkv = _dot(hn, win_ref[:, :QKV_WIDTH])
    z = _gelu_tanh(y_ref[...])
    glu = _dot(z.astype(BF), wglu_ref[...]) + bglu_ref[...]
    rc, rs1, rs2 = rc_ref[...], rs1_ref[...], rs2_ref[...]
    scale = HEAD_DIM ** -0.5 * LOG2_E
    q = [(_rope(qkv[:, c * LANES:(c + 1) * LANES], rc, rs1, rs2) * scale).astype(BF)
         for c in range(ATTN_WIDTH // LANES)]
    k = _rope(qkv[:, ATTN_WIDTH:ATTN_WIDTH + KV_WIDTH], rc, rs1, rs2)
    v = qkv[:, ATTN_WIDTH + KV_WIDTH:ATTN_WIDTH + 2 * KV_WIDTH]
    ssm = (z * _sigmoid(glu)).astype(BF)
    ssm_proj = _dot(ssm, wbs_ref[...])

    w = WINDOW
    n_sub = MIX_TB // w
    lane = lax.broadcasted_iota(jnp.int32, (2 * w, LANES), 1)
    lo = lane < HEAD_DIM
    kj = lax.broadcasted_iota(jnp.int32, (2 * w, w), 0)
    qi = lax.broadcasted_iota(jnp.int32, (2 * w, w), 1)
    band = (kj > qi) & (kj <= qi + w)
    first = band & ((kj >= w) | (t > 0))
    bias_band = jnp.where(band, 0.0, NEG_BIG).astype(F32)
    bias_first = jnp.where(first, 0.0, NEG_BIG).astype(F32)
    col = lax.broadcasted_iota(jnp.int32, (1, 2 * w), 1)
    sum_r = lax.broadcasted_iota(jnp.int32, (2 * SUBLANES, 4 * w), 0)
    sum_c = lax.broadcasted_iota(jnp.int32, (2 * SUBLANES, 4 * w), 1)
    sum_rows = (((sum_r == 0) & (sum_c < 2 * w)) | ((sum_r == 1) & (sum_c >= 2 * w))).astype(F32)

    def sink_row(ha, hb):
        return jnp.where(col < w, sinks_ref[ha], sinks_ref[hb]) * LOG2_E

    gate_w = GATE_WIDTH // (n_sub * N_KV_HEADS)
    operands = {}

    def sub_block_operands(sb):
        if sb not in operands:
            cur = slice(sb * w, (sb + 1) * w)
            if sb == 0:
                kcat = jnp.concatenate([kprev[...], k[cur]], axis=0)
                vcat = jnp.concatenate([vprev[...], v[cur]], axis=0)
                bias = bias_first
            else:
                kcat = k[(sb - 1) * w:(sb + 1) * w]
                vcat = v[(sb - 1) * w:(sb + 1) * w]
                bias = bias_band
            k_split = _half_split(kcat, pltpu.roll(kcat, HEAD_DIM, 1), lo)
            k_stack = [jnp.concatenate(k_split[2 * g:2 * g + 2], axis=0) for g in range(N_KV_HEADS)]
            vt = vcat.T
            zero = jnp.zeros((HEAD_DIM, 2 * w), F32)
            v_stack = []
            for g in range(N_KV_HEADS):
                vg = vt[g * HEAD_DIM:(g + 1) * HEAD_DIM]
                v_stack.append(jnp.concatenate(
                    [jnp.concatenate([vg, zero], axis=1), jnp.concatenate([zero, vg], axis=1), sum_rows],
                    axis=0).astype(BF))
            operands[sb] = (k_stack, v_stack, jnp.concatenate([bias, bias], axis=1))
        return operands[sb]

    def scores(sb, grp):
        k_stack, _, _ = sub_block_operands(sb)
        cur = slice(sb * w, (sb + 1) * w)
        qq = jnp.concatenate([q[2 * grp][cur], q[2 * grp + 1][cur]], axis=0)
        return _dot_nt(k_stack[grp], qq)

    def weighted_values(sb, grp, s):
        _, v_stack, bias2 = sub_block_operands(sb)
        p_e, sink_e = _softmax_terms_t(s[:2 * w], bias2, sink_row(4 * grp, 4 * grp + 2))
        p_o, sink_o = _softmax_terms_t(s[2 * w:], bias2, sink_row(4 * grp + 1, 4 * grp + 3))
        d = _dot(v_stack[grp], jnp.concatenate([p_e, p_o], axis=0))
        heads = 2 * HEAD_DIM
        o2 = jnp.concatenate(
            [d[:HEAD_DIM] * (1.0 / (d[heads:heads + 1] + sink_e)),
             d[HEAD_DIM:heads] * (1.0 / (d[heads + 1:heads + 2] + sink_o))], axis=0)
        return [o2[:, :w].T, o2[:, w:].T]

    chains = [(sb, grp) for sb in range(n_sub) for grp in range(N_KV_HEADS)]
    gate_cols, outs = [], []
    s_next = scores(*chains[0])
    for i, (sb, grp) in enumerate(chains):
        s_cur = s_next
        gate_cols.append(_gate_cols(hn, win_ref, bg_ref, slice(i * gate_w, (i + 1) * gate_w)))
        if i + 1 < len(chains):
            s_next = scores(*chains[i + 1])
        outs += weighted_values(sb, grp, s_cur)
    per_sb = 2 * N_KV_HEADS
    attn = jnp.concatenate(
        [jnp.concatenate(outs[sb * per_sb:(sb + 1) * per_sb], axis=1) for sb in range(n_sub)],
        axis=0).astype(BF)
    gates = jnp.concatenate(gate_cols, axis=1)

    kprev[...] = k[MIX_TB - w:]
    vprev[...] = v[MIX_TB - w:]
    kwin_ref[...] = k[MIX_TB - w:].T
    vwin_ref[...] = v[MIX_TB - w:].T
    x1_ref[...] = _merge_out(x, gates, _dot(attn, wba_ref[...]), ssm_proj, wo_ref)


def _mix_prompt(x, y_ssm, rc, rs1, rs2, sinks, g1, w_in, bg, wba, wbs, wo, wglu, bglu):
    b, t, _ = x.shape
    nblk = t // MIX_TB
    x2 = x.reshape(b * t, D_MODEL)
    row_map = lambda i, j: (i * nblk + j, 0)
    x1, kwin, vwin, w_in_bf = pl.pallas_call(
        _mix_prompt_kernel,
        grid=(b, nblk),
        in_specs=[
            pl.BlockSpec(memory_space=pltpu.SMEM),
            pl.BlockSpec((MIX_TB, D_MODEL), row_map),
            pl.BlockSpec((MIX_TB, SSM_WIDTH), row_map),
            pl.BlockSpec((MIX_TB, LANES), lambda i, j: (j, 0)),
            pl.BlockSpec((MIX_TB, LANES), lambda i, j: (j, 0)),
            pl.BlockSpec((MIX_TB, LANES), lambda i, j: (j, 0)),
            _const_spec((1, D_MODEL)),
            pl.BlockSpec(memory_space=pl.ANY),
            _const_spec(bg.shape),
            _const_spec(wba.shape),
            _const_spec(wbs.shape),
            _const_spec(wo.shape),
            _const_spec(wglu.shape),
            _const_spec(bglu.shape),
        ],
        out_specs=[
            pl.BlockSpec((MIX_TB, D_MODEL), row_map),
            pl.BlockSpec((None, KV_WIDTH, WINDOW), lambda i, j: (i, 0, 0)),
            pl.BlockSpec((None, KV_WIDTH, WINDOW), lambda i, j: (i, 0, 0)),
            pl.BlockSpec(memory_space=pl.ANY),
        ],
        out_shape=[
            jax.ShapeDtypeStruct((b * t, D_MODEL), F32),
            jax.ShapeDtypeStruct((b, KV_WIDTH, WINDOW), F32),
            jax.ShapeDtypeStruct((b, KV_WIDTH, WINDOW), F32),
            jax.ShapeDtypeStruct(w_in.shape, BF),
        ],
        scratch_shapes=[pltpu.VMEM((WINDOW, KV_WIDTH), F32), pltpu.VMEM((WINDOW, KV_WIDTH), F32),
                        pltpu.VMEM(w_in.shape, BF),
                        pltpu.VMEM((FFN_W_SLOTS, D_MODEL // FFN_W_CHUNKS, IN_WIDTH), F32),
                        pltpu.SemaphoreType.DMA((FFN_W_SLOTS,)), pltpu.SemaphoreType.DMA((1,))],
        compiler_params=pltpu.CompilerParams(
            dimension_semantics=("arbitrary", "arbitrary"), vmem_limit_bytes=VMEM_LIMIT),
        name="mix_prompt",
    )(sinks, x2, y_ssm, rc, rs1, rs2, g1, w_in, bg, wba, wbs, wo, wglu, bglu)
    return x1, kwin, vwin, w_in_bf


QROWS = 16
SAMPLE_TB = 32


def _sample_layer_kernel(x_ref, kbuf_ref, vbuf_ref, rc_ref, rs1_ref, rs2_ref, sinkc_ref, g1_ref, wqkv_ref,
                         xall_ref, h0re_ref, h0im_ref, bg_ref, lb_ref, d_ref, wglu_ref, bglu_ref,
                         win_hbm, wba_hbm, wbs_hbm, wo_hbm, bblk_hbm, ctblk_hbm,
                         kout_ref, vout_ref, x1_ref, hre_ref, him_ref,
                         qz, o3_all, win_s, wba_s, wbs_s, wo_s, bblk_s, ctblk_s, sem):
    nb = x_ref.shape[0]
    step = pl.program_id(0)
    last_step = pl.num_programs(0) - 1
    late = ((win_hbm, win_s), (wba_hbm, wba_s), (wbs_hbm, wbs_s), (wo_hbm, wo_s), (bblk_hbm, bblk_s),
            (ctblk_hbm, ctblk_s))
    late_copies = [pltpu.make_async_copy(src, dst, sem.at[i]) for i, (src, dst) in enumerate(late)]

    @pl.when(step == 0)
    def _():
        for copy in late_copies:
            copy.start()

    hn = _rms(x_ref[...], g1_ref[...]).astype(BF)
    qkv = _dot(hn, wqkv_ref[...])
    rc, rs1, rs2 = rc_ref[...], rs1_ref[...], rs2_ref[...]
    scale = HEAD_DIM ** -0.5
    k_new = _rope(qkv[:, ATTN_WIDTH:ATTN_WIDTH + KV_WIDTH], rc, rs1, rs2)
    v_new = qkv[:, ATTN_WIDTH + KV_WIDTH:ATTN_WIDTH + 2 * KV_WIDTH]
    pad = jnp.zeros((LANES - nb, KV_WIDTH), F32)
    k_new_t = jnp.concatenate([k_new, pad], axis=0).T
    v_new_t = jnp.concatenate([v_new, pad], axis=0).T

    lane = lax.broadcasted_iota(jnp.int32, (nb, LANES), 1)
    lo = lane < HEAD_DIM
    qz[...] = jnp.zeros_like(qz)
    for c in range(ATTN_WIDTH // LANES):
        qc = _rope(qkv[:, c * LANES:(c + 1) * LANES], rc, rs1, rs2) * scale
        qr = pltpu.roll(qc, HEAD_DIM, 1)
        zero = jnp.zeros_like(qc)
        if c < 2:
            even, odd = jnp.where(lo, qc, zero), jnp.where(lo, qr, zero)
        else:
            even, odd = jnp.where(lo, zero, qr), jnp.where(lo, zero, qc)
        qz[pl.ds(2 * c, nb, stride=QROWS), :] = even
        qz[pl.ds(2 * c + 1, nb, stride=QROWS), :] = odd

    last = lax.broadcasted_iota(jnp.int32, (KV_WIDTH, WINDOW), 1) == WINDOW - 1
    for b in range(nb):
        kout_ref[b] = jnp.where(last, k_new_t[:, b:b + 1], pltpu.roll(kbuf_ref[b], WINDOW - 1, 1))
        vout_ref[b] = jnp.where(last, v_new_t[:, b:b + 1], pltpu.roll(vbuf_ref[b], WINDOW - 1, 1))

    sink = sinkc_ref[...]
    q3 = qz[...].reshape(nb, QROWS, LANES).astype(BF)
    s = jnp.einsum('bhd,bdk->bhk', q3, kout_ref[...].astype(BF), preferred_element_type=F32)
    m = jnp.maximum(jnp.max(s, axis=-1, keepdims=True), sink)
    p = jnp.exp(s - m)
    den = jnp.sum(p, axis=-1, keepdims=True) + jnp.exp(sink - m)
    p = (p * (1.0 / den)).astype(BF)
    o3 = jnp.einsum('bhk,bdk->bhd', p, vout_ref[...].astype(BF), preferred_element_type=F32)
    o3_all[pl.ds(pl.multiple_of(step * (nb * QROWS), nb * QROWS), nb * QROWS), :] = o3.reshape(nb * QROWS, LANES)

    @pl.when(step == last_step)
    def _():
        for copy in late_copies:
            copy.wait()
        _sample_tail(xall_ref, o3_all, h0re_ref, h0im_ref, g1_ref, win_s, bg_ref, wba_s, wbs_s, wo_s,
                     lb_ref, bblk_s, ctblk_s, d_ref, wglu_ref, bglu_ref, x1_ref, hre_ref, him_ref)


def _sample_layer(x, kbuf, vbuf, h0re, h0im, rc, rs1, rs2, sinkc, g1, w_in, bg, wba, wbs, wo, lb, bblk, ctblk,
                  d, wglu, bglu):
    nb = x.shape[0]
    tb = SAMPLE_TB
    blocked = (x, kbuf, vbuf)
    early = (rc, rs1, rs2, sinkc, g1)
    consts = (x, h0re, h0im, bg, lb, d, wglu, bglu)
    late = (w_in, wba, wbs, wo, bblk, ctblk)
    kv_spec = pl.BlockSpec((tb, KV_WIDTH, WINDOW), lambda i: (i, 0, 0))
    whole = lambda shape: pl.BlockSpec(shape, lambda i: (0,) * len(shape))
    return pl.pallas_call(
        _sample_layer_kernel,
        grid=(nb // tb,),
        in_specs=[pl.BlockSpec((tb, D_MODEL), lambda i: (i, 0)), kv_spec, kv_spec]
        + [_const_spec(a.shape) for a in early]
        + [pl.BlockSpec((D_MODEL, QKV_WIDTH), lambda i: (0, 0), pipeline_mode=pl.Buffered(1))]
        + [_const_spec(a.shape) for a in consts]
        + [pl.BlockSpec(memory_space=pl.ANY) for _ in late],
        out_specs=[kv_spec, kv_spec, whole((nb, D_MODEL)), whole((N_STATES, nb)), whole((N_STATES, nb))],
        out_shape=[
            jax.ShapeDtypeStruct((nb, KV_WIDTH, WINDOW), F32),
            jax.ShapeDtypeStruct((nb, KV_WIDTH, WINDOW), F32),
            jax.ShapeDtypeStruct((nb, D_MODEL), F32),
            jax.ShapeDtypeStruct((N_STATES, nb), F32),
            jax.ShapeDtypeStruct((N_STATES, nb), F32),
        ],
        scratch_shapes=[pltpu.VMEM((tb * QROWS, LANES), F32), pltpu.VMEM((nb * QROWS, LANES), F32)]
        + [pltpu.VMEM(a.shape, a.dtype) for a in late] + [pltpu.SemaphoreType.DMA((len(late),))],
        compiler_params=pltpu.CompilerParams(
            dimension_semantics=("arbitrary",), vmem_limit_bytes=VMEM_LIMIT),
        name="sample_layer",
    )(*blocked, *early, w_in, *consts, *late)


def _sample_tail(x_ref, o3_ref, h0re_ref, h0im_ref, g1_ref, win_ref, bg_ref, wba_ref,
                 wbs_ref, wo_ref, lb_ref, bblk_ref, ctblk_ref, d_ref, wglu_ref, bglu_ref,
                 x1_ref, hre_ref, him_ref):
    nb = x_ref.shape[0]
    x = x_ref[...]
    hn = _rms(x, g1_ref[...]).astype(BF)

    lane = lax.broadcasted_iota(jnp.int32, (nb, LANES), 1)
    lo = lane < HEAD_DIM
    a = jnp.zeros((nb, D_MODEL), F32)
    zero = jnp.zeros((nb, LANES), F32)
    for h in range(N_Q_HEADS):
        oh = o3_ref[pl.ds(h, nb, stride=QROWS), :]
        oh = jnp.where(lo, oh, zero) if h < N_Q_HEADS // 2 else jnp.where(lo, zero, oh)
        w_h = wba_ref[h * HEAD_DIM:(h + 1) * HEAD_DIM, :]
        a = a + _dot(oh.astype(BF), jnp.concatenate([w_h, w_h], axis=0))

    u = _dot(hn, win_ref[:, U_COL0:GATE_COL0])
    ub = u.astype(BF)
    lre, lim = lb_ref[0:1, :], lb_ref[1:2, :]
    y_cols = []
    for o in range(N_OCT):
        sl = slice(o * OCT_STATES, (o + 1) * OCT_STATES)
        bu = _dot(ub[:, o * LANES:(o + 1) * LANES], bblk_ref[o].astype(BF))
        blocks = [slice(c * LANES, (c + 1) * LANES) for c in range(o * OCT_COL, (o + 1) * OCT_COL)]
        h0r = jnp.concatenate([h0re_ref[rows, :].T for rows in blocks], axis=1)
        h0i = jnp.concatenate([h0im_ref[rows, :].T for rows in blocks], axis=1)
        hr = bu[:, :OCT_STATES] + (lre[:, sl] * h0r - lim[:, sl] * h0i)
        hi = bu[:, OCT_STATES:] + (lre[:, sl] * h0i + lim[:, sl] * h0r)
        for cc, rows in enumerate(blocks):
            hre_ref[rows, :] = hr[:, cc * LANES:(cc + 1) * LANES].T
            him_ref[rows, :] = hi[:, cc * LANES:(cc + 1) * LANES].T
        y_cols.append(_dot_nt(jnp.concatenate([hr, -hi], axis=1).astype(BF), ctblk_ref[o].astype(BF)))
    y = jnp.concatenate(y_cols, axis=1) + d_ref[...] * u
    z = _gelu_tanh(y)
    gate = _dot(z.astype(BF), wglu_ref[...]) + bglu_ref[...]
    ssm = (z * _sigmoid(gate)).astype(BF)

    gates = _gate_cols(hn, win_ref, bg_ref, slice(0, GATE_WIDTH))
    x1_ref[...] = _merge_out(x, gates, a, _dot(ssm, wbs_ref[...]), wo_ref)


FFN_W_CHUNKS = 16
FFN_W_SLOTS = 4


def _stream_cast_weights(jobs):
    tasks, used = [], {}
    for w_hbm, w_bf, stage, sem in jobs:
        rows = stage.shape[1]
        for k in range(w_hbm.shape[0] // rows):
            slot = used.get(id(stage), 0) % stage.shape[0]
            used[id(stage)] = used.get(id(stage), 0) + 1
            copy = pltpu.make_async_copy(w_hbm.at[pl.ds(k * rows, rows), :], stage.at[slot], sem.at[slot])
            tasks.append((copy, w_bf, stage, slot, k * rows, rows))
    ahead = min(stage.shape[0] for _, _, stage, _ in jobs) - 1
    for copy, *_ in tasks[:ahead]:
        copy.start()
    for i, (copy, w_bf, stage, slot, row0, rows) in enumerate(tasks):
        if i + ahead < len(tasks):
            tasks[i + ahead][0].start()
        copy.wait()
        w_bf[row0:row0 + rows, :] = stage[slot].astype(BF)


def _ffn_rows(x, g2_ref, wgate, wup, wdown, gf_ref):
    h = _rms(x, g2_ref[...]).astype(BF)
    gate = _dot(h, wgate[...])
    up = _dot(h, wup[...])
    half_gate = 0.5 * gate
    act = ((half_gate + half_gate * jnp.tanh(half_gate)) * up).astype(BF)
    x2 = x + _dot(act, wdown[...])
    return _rms(x2, gf_ref[...])


def _ffn_kernel(xp_ref, xs_ref, g2_ref, wgate_hbm, wup_hbm, wdown_hbm, gf_ref, yp_ref, ys_ref,
                wgate, wup, wdown, stage_in, stage_out, sem_in, sem_out):
    i = pl.program_id(0)
    n_prompt = pl.num_programs(0) - 1

    @pl.when(i == 0)
    def _():
        _stream_cast_weights([(wgate_hbm, wgate, stage_in, sem_in), (wup_hbm, wup, stage_in, sem_in),
                              (wdown_hbm, wdown, stage_out, sem_out)])

    @pl.when(i < n_prompt)
    def _():
        half = FFN_TB // FFN_SPLIT
        for rows in (slice(h * half, (h + 1) * half) for h in range(FFN_SPLIT)):
            yp_ref[rows, :] = _ffn_rows(xp_ref[rows, :], g2_ref, wgate, wup, wdown, gf_ref)

    @pl.when(i == n_prompt)
    def _():
        ys_ref[...] = _ffn_rows(xs_ref[...], g2_ref, wgate, wup, wdown, gf_ref)


def _ffn(xp, xs, g2, wgate, wup, wdown, gf):
    n, ns = xp.shape[0], xs.shape[0]
    n_prompt = n // FFN_TB
    prompt_map = lambda i: (jnp.minimum(i, n_prompt - 1), 0)
    return pl.pallas_call(
        _ffn_kernel,
        grid=(n_prompt + 1,),
        in_specs=[
            pl.BlockSpec((FFN_TB, D_MODEL), prompt_map),
            _const_spec((ns, D_MODEL)),
            _const_spec((1, D_MODEL)),
            pl.BlockSpec(memory_space=pl.ANY),
            pl.BlockSpec(memory_space=pl.ANY),
            pl.BlockSpec(memory_space=pl.ANY),
            _const_spec((1, D_MODEL)),
        ],
        out_specs=[
            pl.BlockSpec((FFN_TB, D_MODEL), prompt_map),
            pl.BlockSpec((ns, D_MODEL), lambda i: (0, 0)),
        ],
        out_shape=[
            jax.ShapeDtypeStruct((n, D_MODEL), F32),
            jax.ShapeDtypeStruct((ns, D_MODEL), F32),
        ],
        scratch_shapes=[
            pltpu.VMEM((D_MODEL, D_FF), BF), pltpu.VMEM((D_MODEL, D_FF), BF), pltpu.VMEM((D_FF, D_MODEL), BF),
            pltpu.VMEM((FFN_W_SLOTS, D_MODEL // FFN_W_CHUNKS, D_FF), F32),
            pltpu.VMEM((FFN_W_SLOTS, D_FF // FFN_W_CHUNKS, D_MODEL), F32),
            pltpu.SemaphoreType.DMA((FFN_W_SLOTS,)), pltpu.SemaphoreType.DMA((FFN_W_SLOTS,)),
        ],
        compiler_params=pltpu.CompilerParams(
            dimension_semantics=("arbitrary",), vmem_limit_bytes=VMEM_LIMIT),
        name="ffn",
    )(xp, xs, g2, wgate, wup, wdown, gf)


def _rope_tables(pos):
    pos = np.asarray(pos, np.float64)
    inv_freq = ROPE_THETA ** (-(np.arange(ROPE_HALF, dtype=np.float64) * 2.0 / ROPE_DIM))
    ang = pos[:, None] * inv_freq[None, :]
    cos, sin = np.cos(ang), np.sin(ang)
    pad = np.zeros((pos.shape[0], HEAD_DIM - ROPE_DIM))
    zero = np.zeros_like(sin)
    rc = np.concatenate([cos, cos, pad + 1.0], axis=1)
    rs1 = np.concatenate([zero, sin, pad], axis=1)
    rs2 = np.concatenate([-sin, zero, pad], axis=1)
    rep = LANES // HEAD_DIM
    return tuple(jnp.asarray(np.tile(a, (1, rep)), F32) for a in (rc, rs1, rs2))


def _cmul(ar, ai, br, bi):
    return ar * br - ai * bi, ar * bi + ai * br


def _ssm_tables(lam_re, lam_im, log_dt, b_re, b_im, c_re, c_im):
    dt = jnp.exp(log_dt)[:, None]
    mag = jnp.exp(lam_re * dt)
    lb_re = mag * jnp.cos(lam_im * dt)
    lb_im = mag * jnp.sin(lam_im * dt)
    den = lam_re * lam_re + lam_im * lam_im
    nr = lb_re - 1.0
    k_re = ((nr * lam_re + lb_im * lam_im) / den)[..., None]
    k_im = ((lb_im * lam_re - nr * lam_im) / den)[..., None]
    bb_re = k_re * b_re - k_im * b_im
    bb_im = k_re * b_im + k_im * b_re

    a_re, a_im = lb_re, lb_im
    for _ in range(int(math.log2(CHUNK))):
        a_re, a_im = _cmul(a_re, a_im, a_re, a_im)
    s_re, s_im = a_re, a_im
    for _ in range(int(math.log2(SEG))):
        s_re, s_im = _cmul(s_re, s_im, s_re, s_im)

    eye = jnp.eye(OCT, dtype=F32).reshape(1, OCT, 1, OCT, 1)

    def block_diag(a):
        r, c = a.shape[1:]
        return (a.reshape(N_OCT, OCT, r, 1, c) * eye).reshape(N_OCT, OCT * r, OCT * c)

    bblk = jnp.concatenate([block_diag(jnp.swapaxes(bb_re, 1, 2)),
                            block_diag(jnp.swapaxes(bb_im, 1, 2))], axis=2)
    ctblk = jnp.concatenate([block_diag(c_re), block_diag(c_im)], axis=2)

    oct_cols = lambda a: a.reshape(N_OCT, 1, OCT_STATES)
    lcol = jnp.concatenate([oct_cols(lb_re), oct_cols(lb_im),
                            jnp.zeros((N_OCT, SUBLANES - 2, OCT_STATES), F32)], axis=1)

    flat = lambda a: a.reshape(1, N_STATES)
    col = lambda a: a.reshape(N_COL, 1, LANES)
    a_tab = jnp.concatenate([col(a_re), col(a_im), col(s_re), col(s_im),
                             jnp.zeros((N_COL, SUBLANES - 4, LANES), F32)], axis=1)
    lb = jnp.concatenate([flat(lb_re), flat(lb_im)], axis=0)
    return bblk, ctblk, lcol, a_tab, lb


def kernel(x_prompt, x_sample, state_k_win, state_v_win, state_ssm_re, state_ssm_im, norm1_g, w_in, b_gate, attn_sinks, ssm_lam_re, ssm_lam_im, ssm_log_dt, ssm_b_re, ssm_b_im, ssm_c_re, ssm_c_im, ssm_d, w_glu, b_glu, w_branch_attn, w_branch_ssm, w_out, norm2_g, w_ffn_gate, w_ffn_up, w_ffn_down, norm_f_g):
    depth = w_in.shape[0]
    assert depth == 1
    b, t, _ = x_prompt.shape
    nb, s_len, _ = x_sample.shape
    assert s_len == 1 and state_k_win.shape[2] == WINDOW
    l = 0
    assert w_in.shape[2] == IN_WIDTH
    w_in_f = w_in[l]
    g1 = norm1_g[l].reshape(1, D_MODEL)
    g2 = norm2_g[l].reshape(1, D_MODEL)
    gf = norm_f_g.reshape(1, D_MODEL)
    bg = b_gate[l].reshape(1, GATE_WIDTH)
    d = ssm_d[l].reshape(1, SSM_WIDTH)
    wglu = w_glu[l].astype(BF)
    bglu = b_glu[l].reshape(1, SSM_WIDTH)
    wba = w_branch_attn[l].astype(BF)
    wbs = w_branch_ssm[l].astype(BF)
    wo = w_out[l].astype(BF)
    sinks = attn_sinks[l]

    bblk, ctblk, lcol, a_tab, lb = _ssm_tables(
        ssm_lam_re[l], ssm_lam_im[l], ssm_log_dt[l], ssm_b_re[l], ssm_b_im[l], ssm_c_re[l], ssm_c_im[l])

    rc, rs1, rs2 = _rope_tables(np.arange(t))
    yssm_p, hre_p, him_p = _s5_prompt(x_prompt, g1, w_in_f, bblk, ctblk, lcol, a_tab, d)
    x1_p, kwin_p, vwin_p, w_in_b = _mix_prompt(x_prompt, yssm_p, rc, rs1, rs2, sinks, g1, w_in_f, bg, wba, wbs, wo,
                                               wglu, bglu)

    rcs, rs1s, rs2s = _rope_tables(PAST_LEN + np.arange(1))
    sinkc = jnp.concatenate([sinks, jnp.zeros((QROWS - N_Q_HEADS,), F32)]).reshape(QROWS, 1)
    xs = x_sample.reshape(nb, D_MODEL)
    key_minor = lambda a: jnp.swapaxes(a.reshape(nb, WINDOW, KV_WIDTH), 1, 2)
    kwin_s, vwin_s, x1_s, hre_s, him_s = _sample_layer(
        xs, key_minor(state_k_win[l]), key_minor(state_v_win[l]),
        state_ssm_re[l].reshape(nb, N_STATES).T, state_ssm_im[l].reshape(nb, N_STATES).T,
        rcs, rs1s, rs2s, sinkc, g1, w_in_b, bg, wba, wbs, wo, lb, bblk, ctblk,
        d, wglu, bglu)
    y_p, y_s = _ffn(x1_p, x1_s, g2, w_ffn_gate[l], w_ffn_up[l], w_ffn_down[l], gf)
    y_p = y_p.reshape(b, t, D_MODEL)
    y_s = y_s.reshape(nb, 1, D_MODEL)

    kv_shape_p = (1, b, WINDOW, N_KV_HEADS, HEAD_DIM)
    st_shape_p = (1, b, N_SSM_GROUPS, SSM_STATE)
    kv_shape_s = (1, nb, WINDOW, N_KV_HEADS, HEAD_DIM)
    st_shape_s = (1, nb, N_SSM_GROUPS, SSM_STATE)
    return (y_p, y_s,
            jnp.swapaxes(kwin_p, 1, 2).reshape(kv_shape_p), jnp.swapaxes(vwin_p, 1, 2).reshape(kv_shape_p),
            hre_p.reshape(st_shape_p), him_p.reshape(st_shape_p),
            jnp.swapaxes(kwin_s, 1, 2).reshape(kv_shape_s), jnp.swapaxes(vwin_s, 1, 2).reshape(kv_shape_s),
            hre_s.T.reshape(st_shape_s), him_s.T.reshape(st_shape_s))
```

```python
import math

import jax
import jax.numpy as jnp
import numpy as np
from jax import lax
from jax.experimental import pallas as pl
from jax.experimental.pallas import tpu as pltpu

D_MODEL = 1024
N_Q_HEADS = 8
N_KV_HEADS = 2
HEAD_DIM = 64
ATTN_WIDTH = N_Q_HEADS * HEAD_DIM
KV_WIDTH = N_KV_HEADS * HEAD_DIM
WINDOW = 128
ROPE_DIM = HEAD_DIM // 4
ROPE_HALF = ROPE_DIM // 2
ROPE_THETA = 500000.0
SSM_WIDTH = D_MODEL // 2
SSM_GROUP = 16
N_SSM_GROUPS = SSM_WIDTH // SSM_GROUP
SSM_STATE = 64
N_STATES = N_SSM_GROUPS * SSM_STATE
GATE_WIDTH = 2 * D_MODEL
QKV_WIDTH = ATTN_WIDTH + 2 * KV_WIDTH
U_COL0 = QKV_WIDTH
GATE_COL0 = U_COL0 + SSM_WIDTH
IN_WIDTH = GATE_COL0 + GATE_WIDTH
D_FF = -(-8 * D_MODEL // (3 * 256)) * 256
NORM_EPS = 1e-5
PAST_LEN = 8192

LANES = 128
SUBLANES = 8
CHUNK = 8
OCT = LANES // SSM_GROUP
N_OCT = N_SSM_GROUPS // OCT
OCT_STATES = OCT * SSM_STATE
OCT_COL = OCT_STATES // LANES
N_COL = N_STATES // LANES
N_SEG = SUBLANES
SEG = 16
S5_ROWS = N_SEG * SEG
M_PAIR = 2
U_HALF = SSM_WIDTH // 2
SEG_PITCH = SEG * CHUNK + SUBLANES
MIX_TB = 512
FFN_TB = 1024
FFN_SPLIT = 4
NEG_BIG = -1e30
LOG2_E = math.log2(math.e)
VMEM_LIMIT = 56 * 1024 * 1024

BF = jnp.bfloat16
F32 = jnp.float32


def _dot(a, b):
    return jnp.dot(a, b, preferred_element_type=F32)


def _dot_nt(a, b):
    return lax.dot_general(a, b, (((1,), (1,)), ((), ())), preferred_element_type=F32)


def _dot_nt_split(a, b):
    a_hi, b_hi = a.astype(BF), b.astype(BF)
    a_lo = (a - a_hi.astype(F32)).astype(BF)
    b_lo = (b - b_hi.astype(F32)).astype(BF)
    return _dot_nt(a_hi, b_hi) + (_dot_nt(a_hi, b_lo) + _dot_nt(a_lo, b_hi))


def _rms(x, g):
    return x * lax.rsqrt(jnp.mean(x * x, axis=-1, keepdims=True) + NORM_EPS) * g


def _sigmoid(x):
    return 1.0 / (1.0 + jnp.exp(-x))


def _gelu_tanh(x):
    c = math.sqrt(2.0 / math.pi)
    return 0.5 * x * (1.0 + jnp.tanh(c * (x + 0.044715 * (x * x * x))))


def _rope(a, rc, rs1, rs2):
    return a * rc + pltpu.roll(a, ROPE_HALF, 1) * rs1 + pltpu.roll(a, LANES - ROPE_HALF, 1) * rs2


def _const_spec(shape):
    nd = len(shape)
    return pl.BlockSpec(shape, lambda *_: (0,) * nd, pipeline_mode=pl.Buffered(1))


def _build_chunk_operators(bblk_ref, ctblk_ref, lcol_ref, m_s, e_s, f_s):
    for o in range(N_OCT):
        ct = jnp.concatenate([ctblk_ref[o, :, :OCT_STATES], -ctblk_ref[o, :, OCT_STATES:]],
                             axis=1)
        lr, li = lcol_ref[o, 0:1, :], lcol_ref[o, 1:2, :]
        er, ei = bblk_ref[o, :, :OCT_STATES], bblk_ref[o, :, OCT_STATES:]
        k_blk = []
        for tau in range(CHUNK):
            e_cat = jnp.concatenate([er, ei], axis=1)
            i = CHUNK - 1 - tau
            e_s[o, i * LANES:(i + 1) * LANES, :] = e_cat.astype(BF)
            k_blk.append(_dot_nt_split(e_cat, ct).astype(BF))
            er, ei = er * lr - ei * li, er * li + ei * lr
        zero = jnp.zeros((LANES, LANES), BF)
        for j in range(CHUNK):
            jt, jj = divmod(j, M_PAIR)
            for i in range(M_PAIR * (jt + 1)):
                m_s[jt][o, i * LANES:(i + 1) * LANES, jj * LANES:(jj + 1) * LANES] = (
                    k_blk[j - i] if j >= i else zero)
        tr, ti = ct[:, :OCT_STATES], -ct[:, OCT_STATES:]
        for j in range(CHUNK):
            tr, ti = tr * lr - ti * li, tr * li + ti * lr
            f_s[o, :OCT_STATES, j * LANES:(j + 1) * LANES] = tr.T.astype(BF)
            f_s[o, OCT_STATES:, j * LANES:(j + 1) * LANES] = (-ti).T.astype(BF)


def _s5_prompt_kernel(x_ref, g1_ref, wu0_ref, wu1_ref, bblk_ref, ctblk_ref, lcol_ref, a_ref, d_ref,
                      out_ref, hfin_ref,
                      m0_ref, m1_ref, m2_ref, m3_ref, e_ref, f_ref, us, ys, sre, sim, car):
    m_ref = (m0_ref, m1_ref, m2_ref, m3_ref)
    blk = pl.program_id(1)

    @pl.when((pl.program_id(0) == 0) & (blk == 0))
    def _():
        _build_chunk_operators(bblk_ref, ctblk_ref, lcol_ref, m_ref, e_ref, f_ref)

    @pl.when(blk == 0)
    def _():
        car[...] = jnp.zeros_like(car)

    seg_tokens = SEG * CHUNK
    seg_rows = [slice(s * SEG_PITCH, s * SEG_PITCH + seg_tokens) for s in range(N_SEG)]
    tok_rows = [slice(s * seg_tokens, (s + 1) * seg_tokens) for s in range(N_SEG)]

    for s in range(0, N_SEG, 2):
        hn = _rms(x_ref[s * seg_tokens:(s + 2) * seg_tokens, :], g1_ref[...]).astype(BF)
        u = jnp.concatenate([_dot(hn, wu0_ref[...].astype(BF)), _dot(hn, wu1_ref[...].astype(BF))],
                            axis=1)
        for half in range(2):
            for cc in range(N_OCT):
                us[cc, seg_rows[s + half], :] = u[tok_rows[half], cc * LANES:(cc + 1) * LANES]

    sub = lax.broadcasted_iota(jnp.int32, (N_SEG, LANES), 0)

    def scan_column(c):
        tab = a_ref[c]
        are, aim = tab[0:1], tab[1:2]
        bre, bim = tab[2:3], tab[3:4]

        def step(cr, ci, r, keep_entering):
            slab = slice(r * N_SEG, (r + 1) * N_SEG)
            s_r, s_i = sre[c, slab, :], sim[c, slab, :]
            if keep_entering:
                sre[c, slab, :] = cr
                sim[c, slab, :] = ci
            return are * cr - aim * ci + s_r, are * ci + aim * cr + s_i

        cr = jnp.zeros((N_SEG, LANES), F32)
        ci = jnp.zeros((N_SEG, LANES), F32)
        for r in range(SEG):
            cr, ci = step(cr, ci, r, False)
        cv = car[c]
        pr, pi = cv[0:1], cv[1:2]
        sr = jnp.zeros((N_SEG, LANES), F32)
        si = jnp.zeros((N_SEG, LANES), F32)
        for s in range(N_SEG):
            sr = jnp.where(sub == s, pr, sr)
            si = jnp.where(sub == s, pi, si)
            pr, pi = (bre * pr - bim * pi + cr[s:s + 1], bre * pi + bim * pr + ci[s:s + 1])
        end = jnp.where(sub == 0, pr, jnp.where(sub == 1, pi, 0.0))
        car[c] = end
        hfin_ref[c] = end
        cr, ci = sr, si
        for r in range(SEG):
            cr, ci = step(cr, ci, r, True)

    for o in range(N_OCT):
        uo = jnp.concatenate(
            [jnp.concatenate([us[o, pl.ds(r * CHUNK + i, N_SEG, stride=SEG_PITCH), :] for r in range(SEG)],
                             axis=0).astype(BF) for i in range(CHUNK)], axis=1)
        s_end = _dot(uo, e_ref[o])
        cols = range(o * OCT_COL, (o + 1) * OCT_COL)
        for cc, c in enumerate(cols):
            sre[c] = s_end[:, cc * LANES:(cc + 1) * LANES]
            sim[c] = s_end[:, OCT_STATES + cc * LANES:OCT_STATES + (cc + 1) * LANES]
        y_in = jnp.concatenate(
            [_dot(uo[:, :(jt + 1) * M_PAIR * LANES], m_ref[jt][o]) for jt in range(CHUNK // M_PAIR)],
            axis=1)
        for c in cols:
            scan_column(c)
        hp = jnp.concatenate([sre[c] for c in cols] + [sim[c] for c in cols], axis=1).astype(BF)
        yo = y_in + _dot(hp, f_ref[o])
        for r in range(SEG):
            for j in range(CHUNK):
                ys[o, pl.ds(r * CHUNK + j, N_SEG, stride=SEG_PITCH), :] = (
                    yo[r * N_SEG:(r + 1) * N_SEG, j * LANES:(j + 1) * LANES])
        d_o = d_ref[:, o * LANES:(o + 1) * LANES]
        for s in range(N_SEG):
            out_ref[tok_rows[s], o * LANES:(o + 1) * LANES] = (
                ys[o, seg_rows[s], :] + d_o * us[o, seg_rows[s], :])


def _s5_prompt(x, g1, w_in, bblk, ctblk, lcol, a_tab, d):
    b, t, _ = x.shape
    tb = S5_ROWS * CHUNK
    nblk = t // tb
    x2 = x.reshape(b * t, D_MODEL)
    row_map = lambda i, j: (i * nblk + j, 0)
    op_shape = (N_OCT, CHUNK * LANES, CHUNK * LANES)
    out, hfin = pl.pallas_call(
        _s5_prompt_kernel,
        grid=(b, nblk),
        in_specs=[
            pl.BlockSpec((tb, D_MODEL), row_map),
            _const_spec((1, D_MODEL)),
            *[pl.BlockSpec((D_MODEL, U_HALF), lambda i, j, c=U_COL0 // U_HALF + h: (0, c),
                           pipeline_mode=pl.Buffered(1)) for h in range(2)],
            _const_spec(bblk.shape),
            _const_spec(ctblk.shape),
            _const_spec(lcol.shape),
            _const_spec(a_tab.shape),
            _const_spec((1, SSM_WIDTH)),
        ],
        out_specs=[
            pl.BlockSpec((tb, SSM_WIDTH), row_map),
            pl.BlockSpec((None, N_COL, SUBLANES, LANES), lambda i, j: (i, 0, 0, 0)),
        ],
        out_shape=[
            jax.ShapeDtypeStruct((b * t, SSM_WIDTH), F32),
            jax.ShapeDtypeStruct((b, N_COL, SUBLANES, LANES), F32),
        ],
        scratch_shapes=[
            *[pltpu.VMEM((N_OCT, (jt + 1) * M_PAIR * LANES, M_PAIR * LANES), BF)
              for jt in range(CHUNK // M_PAIR)],
            pltpu.VMEM(op_shape, BF), pltpu.VMEM(op_shape, BF),
            pltpu.VMEM((N_OCT, N_SEG * SEG_PITCH, LANES), F32),
            pltpu.VMEM((N_OCT, N_SEG * SEG_PITCH, LANES), F32),
            pltpu.VMEM((N_COL, S5_ROWS, LANES), F32), pltpu.VMEM((N_COL, S5_ROWS, LANES), F32),
            pltpu.VMEM((N_COL, SUBLANES, LANES), F32),
        ],
        compiler_params=pltpu.CompilerParams(
            dimension_semantics=("arbitrary", "arbitrary"), vmem_limit_bytes=VMEM_LIMIT),
        name="s5_prompt",
    )(x2, g1, w_in, w_in, bblk, ctblk, lcol, a_tab, d)
    return out, hfin[:, :, 0, :], hfin[:, :, 1, :]


def _gate_cols(hn, win_ref, bg_ref, cols):
    w_cols = slice(GATE_COL0 + cols.start, GATE_COL0 + cols.stop)
    return _sigmoid(_dot(hn, win_ref[:, w_cols]) + bg_ref[:, cols])


def _merge_out(x, gates, attn_proj, ssm_proj, wo_ref):
    merged = gates[:, :D_MODEL] * attn_proj + gates[:, D_MODEL:] * ssm_proj
    return x + _dot(merged.astype(BF), wo_ref[...].astype(BF))


def _half_split(a, ar, lo):
    z = jnp.zeros_like(a)
    return (jnp.where(lo, a, z).astype(BF), jnp.where(lo, z, ar).astype(BF),
            jnp.where(lo, ar, z).astype(BF), jnp.where(lo, z, a).astype(BF))


def _softmax_terms_t(st, bias_t, sink):
    st = st + bias_t
    m = jnp.maximum(jnp.max(st, axis=0, keepdims=True), sink)
    return jnp.exp2(st - m).astype(BF), jnp.exp2(sink - m)


def _mix_prompt_kernel(sinks_ref, x_ref, y_ref, rc_ref, rs1_ref, rs2_ref, g1_ref, win_hbm,
                       bg_ref, wba_hbm, wbs_hbm, wo_hbm, wglu_hbm, bglu_ref,
                       x1_ref, kwin_ref, vwin_ref, winb_hbm, kprev, vprev, win_ref, wba_ref, wbs_ref, wo_ref,
                       wglu_ref, stage, stage_d, stage_g, sem_stage, sem_d, sem_g, sem_out):
    t = pl.program_id(1)
    seq = pl.program_id(0)
    share_copy = pltpu.make_async_copy(win_ref, winb_hbm, sem_out.at[0])

    @pl.when((seq == 0) & (t == 0))
    def _():
        _stream_cast_weights([(win_hbm, win_ref, stage, sem_stage), (wglu_hbm, wglu_ref, stage_g, sem_g),
                              (wbs_hbm, wbs_ref, stage_d, sem_d), (wba_hbm, wba_ref, stage_d, sem_d),
                              (wo_hbm, wo_ref, stage_d, sem_d)])
        share_copy.start()

    @pl.when((seq == pl.num_programs(0) - 1) & (t == pl.num_programs(1) - 1))
    def _():
        share_copy.wait()

    @pl.when(t == 0)
    def _():
        kprev[...] = jnp.zeros_like(kprev)
        vprev[...] = jnp.zeros_like(vprev)

    x = x_ref[...]
    hn = _rms(x, g1_ref[...]).astype(BF)
    qkv = _dot(hn, win_ref[:, :QKV_WIDTH])
    z = _gelu_tanh(y_ref[...])
    glu = _dot(z.astype(BF), wglu_ref[...]) + bglu_ref[...]
    rc, rs1, rs2 = rc_ref[...], rs1_ref[...], rs2_ref[...]
    scale = HEAD_DIM ** -0.5 * LOG2_E
    q = [(_rope(qkv[:, c * LANES:(c + 1) * LANES], rc, rs1, rs2) * scale).astype(BF)
         for c in range(ATTN_WIDTH // LANES)]
    k = _rope(qkv[:, ATTN_WIDTH:ATTN_WIDTH + KV_WIDTH], rc, rs1, rs2)
    v = qkv[:, ATTN_WIDTH + KV_WIDTH:ATTN_WIDTH + 2 * KV_WIDTH]
    ssm = (z * _sigmoid(glu)).astype(BF)
    ssm_proj = _dot(ssm, wbs_ref[...])

    w = WINDOW
    n_sub = MIX_TB // w
    lane = lax.broadcasted_iota(jnp.int32, (2 * w, LANES), 1)
    lo = lane < HEAD_DIM
    kj = lax.broadcasted_iota(jnp.int32, (2 * w, w), 0)
    qi = lax.broadcasted_iota(jnp.int32, (2 * w, w), 1)
    band = (kj > qi) & (kj <= qi + w)
    first = band & ((kj >= w) | (t > 0))
    bias_band = jnp.where(band, 0.0, NEG_BIG).astype(F32)
    bias_first = jnp.where(first, 0.0, NEG_BIG).astype(F32)
    col = lax.broadcasted_iota(jnp.int32, (1, 2 * w), 1)
    sum_r = lax.broadcasted_iota(jnp.int32, (2 * SUBLANES, 4 * w), 0)
    sum_c = lax.broadcasted_iota(jnp.int32, (2 * SUBLANES, 4 * w), 1)
    sum_rows = (((sum_r == 0) & (sum_c < 2 * w)) | ((sum_r == 1) & (sum_c >= 2 * w))).astype(F32)

    def sink_row(ha, hb):
        return jnp.where(col < w, sinks_ref[ha], sinks_ref[hb]) * LOG2_E

    gate_w = GATE_WIDTH // (n_sub * N_KV_HEADS)
    operands = {}

    def sub_block_operands(sb):
        if sb not in operands:
            cur = slice(sb * w, (sb + 1) * w)
            if sb == 0:
                kcat = jnp.concatenate([kprev[...], k[cur]], axis=0)
                vcat = jnp.concatenate([vprev[...], v[cur]], axis=0)
                bias = bias_first
            else:
                kcat = k[(sb - 1) * w:(sb + 1) * w]
                vcat = v[(sb - 1) * w:(sb + 1) * w]
                bias = bias_band
            k_split = _half_split(kcat, pltpu.roll(kcat, HEAD_DIM, 1), lo)
            k_stack = [jnp.concatenate(k_split[2 * g:2 * g + 2], axis=0) for g in range(N_KV_HEADS)]
            vt = vcat.T
            zero = jnp.zeros((HEAD_DIM, 2 * w), F32)
            v_stack = []
            for g in range(N_KV_HEADS):
                vg = vt[g * HEAD_DIM:(g + 1) * HEAD_DIM]
                v_stack.append(jnp.concatenate(
                    [jnp.concatenate([vg, zero], axis=1), jnp.concatenate([zero, vg], axis=1), sum_rows],
                    axis=0).astype(BF))
            operands[sb] = (k_stack, v_stack, jnp.concatenate([bias, bias], axis=1))
        return operands[sb]

    def scores(sb, grp):
        k_stack, _, _ = sub_block_operands(sb)
        cur = slice(sb * w, (sb + 1) * w)
        qq = jnp.concatenate([q[2 * grp][cur], q[2 * grp + 1][cur]], axis=0)
        return _dot_nt(k_stack[grp], qq)

    def weighted_values(sb, grp, s):
        _, v_stack, bias2 = sub_block_operands(sb)
        p_e, sink_e = _softmax_terms_t(s[:2 * w], bias2, sink_row(4 * grp, 4 * grp + 2))
        p_o, sink_o = _softmax_terms_t(s[2 * w:], bias2, sink_row(4 * grp + 1, 4 * grp + 3))
        d = _dot(v_stack[grp], jnp.concatenate([p_e, p_o], axis=0))
        heads = 2 * HEAD_DIM
        o2 = jnp.concatenate(
            [d[:HEAD_DIM] * (1.0 / (d[heads:heads + 1] + sink_e)),
             d[HEAD_DIM:heads] * (1.0 / (d[heads + 1:heads + 2] + sink_o))], axis=0)
        return [o2[:, :w].T, o2[:, w:].T]

    chains = [(sb, grp) for sb in range(n_sub) for grp in range(N_KV_HEADS)]
    gate_cols, outs = [], []
    s_next = scores(*chains[0])
    for i, (sb, grp) in enumerate(chains):
        s_cur = s_next
        gate_cols.append(_gate_cols(hn, win_ref, bg_ref, slice(i * gate_w, (i + 1) * gate_w)))
        if i + 1 < len(chains):
            s_next = scores(*chains[i + 1])
        outs += weighted_values(sb, grp, s_cur)
    per_sb = 2 * N_KV_HEADS
    attn = jnp.concatenate(
        [jnp.concatenate(outs[sb * per_sb:(sb + 1) * per_sb], axis=1) for sb in range(n_sub)],
        axis=0).astype(BF)
    gates = jnp.concatenate(gate_cols, axis=1)

    kprev[...] = k[MIX_TB - w:]
    vprev[...] = v[MIX_TB - w:]
    kwin_ref[...] = k[MIX_TB - w:].T
    vwin_ref[...] = v[MIX_TB - w:].T
    x1_ref[...] = _merge_out(x, gates, _dot(attn, wba_ref[...]), ssm_proj, wo_ref)


def _mix_prompt(x, y_ssm, rc, rs1, rs2, sinks, g1, w_in, bg, wba, wbs, wo, wglu, bglu):
    b, t, _ = x.shape
    nblk = t // MIX_TB
    x2 = x.reshape(b * t, D_MODEL)
    row_map = lambda i, j: (i * nblk + j, 0)
    x1, kwin, vwin, w_in_bf = pl.pallas_call(
        _mix_prompt_kernel,
        grid=(b, nblk),
        in_specs=[
            pl.BlockSpec(memory_space=pltpu.SMEM),
            pl.BlockSpec((MIX_TB, D_MODEL), row_map),
            pl.BlockSpec((MIX_TB, SSM_WIDTH), row_map),
            pl.BlockSpec((MIX_TB, LANES), lambda i, j: (j, 0)),
            pl.BlockSpec((MIX_TB, LANES), lambda i, j: (j, 0)),
            pl.BlockSpec((MIX_TB, LANES), lambda i, j: (j, 0)),
            _const_spec((1, D_MODEL)),
            pl.BlockSpec(memory_space=pl.ANY),
            _const_spec(bg.shape),
            pl.BlockSpec(memory_space=pl.ANY),
            pl.BlockSpec(memory_space=pl.ANY),
            pl.BlockSpec(memory_space=pl.ANY),
            pl.BlockSpec(memory_space=pl.ANY),
            _const_spec(bglu.shape),
        ],
        out_specs=[
            pl.BlockSpec((MIX_TB, D_MODEL), row_map),
            pl.BlockSpec((None, KV_WIDTH, WINDOW), lambda i, j: (i, 0, 0)),
            pl.BlockSpec((None, KV_WIDTH, WINDOW), lambda i, j: (i, 0, 0)),
            pl.BlockSpec(memory_space=pl.ANY),
        ],
        out_shape=[
            jax.ShapeDtypeStruct((b * t, D_MODEL), F32),
            jax.ShapeDtypeStruct((b, KV_WIDTH, WINDOW), F32),
            jax.ShapeDtypeStruct((b, KV_WIDTH, WINDOW), F32),
            jax.ShapeDtypeStruct(w_in.shape, BF),
        ],
        scratch_shapes=[pltpu.VMEM((WINDOW, KV_WIDTH), F32), pltpu.VMEM((WINDOW, KV_WIDTH), F32),
                        pltpu.VMEM(w_in.shape, BF), pltpu.VMEM(wba.shape, BF), pltpu.VMEM(wbs.shape, BF),
                        pltpu.VMEM(wo.shape, BF), pltpu.VMEM(wglu.shape, BF),
                        pltpu.VMEM((FFN_W_SLOTS, D_MODEL // FFN_W_CHUNKS, IN_WIDTH), F32),
                        pltpu.VMEM((FFN_W_SLOTS, D_MODEL // FFN_W_CHUNKS, D_MODEL), F32),
                        pltpu.VMEM((FFN_W_SLOTS, D_MODEL // FFN_W_CHUNKS, SSM_WIDTH), F32),
                        pltpu.SemaphoreType.DMA((FFN_W_SLOTS,)), pltpu.SemaphoreType.DMA((FFN_W_SLOTS,)),
                        pltpu.SemaphoreType.DMA((FFN_W_SLOTS,)), pltpu.SemaphoreType.DMA((1,))],
        compiler_params=pltpu.CompilerParams(
            dimension_semantics=("arbitrary", "arbitrary"), vmem_limit_bytes=VMEM_LIMIT),
        name="mix_prompt",
    )(sinks, x2, y_ssm, rc, rs1, rs2, g1, w_in, bg, wba, wbs, wo, wglu, bglu)
    return x1, kwin, vwin, w_in_bf


QROWS = 16
SAMPLE_TB = 32


def _sample_layer_kernel(x_ref, kbuf_ref, vbuf_ref, rc_ref, rs1_ref, rs2_ref, sinkc_ref, g1_ref, wqkv_ref,
                         xall_ref, h0re_ref, h0im_ref, bg_ref, lb_ref, d_ref, wglu_ref, bglu_ref,
                         win_hbm, wba_hbm, wbs_hbm, wo_hbm, bblk_hbm, ctblk_hbm,
                         kout_ref, vout_ref, x1_ref, hre_ref, him_ref,
                         qz, o3_all, win_s, wba_s, wbs_s, wo_s, bblk_s, ctblk_s, sem):
    nb = x_ref.shape[0]
    step = pl.program_id(0)
    last_step = pl.num_programs(0) - 1
    late = ((win_hbm, win_s), (wba_hbm, wba_s), (wbs_hbm, wbs_s), (wo_hbm, wo_s), (bblk_hbm, bblk_s),
            (ctblk_hbm, ctblk_s))
    late_copies = [pltpu.make_async_copy(src, dst, sem.at[i]) for i, (src, dst) in enumerate(late)]

    @pl.when(step == 0)
    def _():
        for copy in late_copies:
            copy.start()

    hn = _rms(x_ref[...], g1_ref[...]).astype(BF)
    qkv = _dot(hn, wqkv_ref[...])
    rc, rs1, rs2 = rc_ref[...], rs1_ref[...], rs2_ref[...]
    scale = HEAD_DIM ** -0.5
    k_new = _rope(qkv[:, ATTN_WIDTH:ATTN_WIDTH + KV_WIDTH], rc, rs1, rs2)
    v_new = qkv[:, ATTN_WIDTH + KV_WIDTH:ATTN_WIDTH + 2 * KV_WIDTH]
    pad = jnp.zeros((LANES - nb, KV_WIDTH), F32)
    k_new_t = jnp.concatenate([k_new, pad], axis=0).T
    v_new_t = jnp.concatenate([v_new, pad], axis=0).T

    lane = lax.broadcasted_iota(jnp.int32, (nb, LANES), 1)
    lo = lane < HEAD_DIM
    qz[...] = jnp.zeros_like(qz)
    for c in range(ATTN_WIDTH // LANES):
        qc = _rope(qkv[:, c * LANES:(c + 1) * LANES], rc, rs1, rs2) * scale
        qr = pltpu.roll(qc, HEAD_DIM, 1)
        zero = jnp.zeros_like(qc)
        if c < 2:
            even, odd = jnp.where(lo, qc, zero), jnp.where(lo, qr, zero)
        else:
            even, odd = jnp.where(lo, zero, qr), jnp.where(lo, zero, qc)
        qz[pl.ds(2 * c, nb, stride=QROWS), :] = even
        qz[pl.ds(2 * c + 1, nb, stride=QROWS), :] = odd

    last = lax.broadcasted_iota(jnp.int32, (KV_WIDTH, WINDOW), 1) == WINDOW - 1
    for b in range(nb):
        kout_ref[b] = jnp.where(last, k_new_t[:, b:b + 1], pltpu.roll(kbuf_ref[b], WINDOW - 1, 1))
        vout_ref[b] = jnp.where(last, v_new_t[:, b:b + 1], pltpu.roll(vbuf_ref[b], WINDOW - 1, 1))

    sink = sinkc_ref[...]
    q3 = qz[...].reshape(nb, QROWS, LANES).astype(BF)
    s = jnp.einsum('bhd,bdk->bhk', q3, kout_ref[...].astype(BF), preferred_element_type=F32)
    m = jnp.maximum(jnp.max(s, axis=-1, keepdims=True), sink)
    p = jnp.exp(s - m)
    den = jnp.sum(p, axis=-1, keepdims=True) + jnp.exp(sink - m)
    p = (p * (1.0 / den)).astype(BF)
    o3 = jnp.einsum('bhk,bdk->bhd', p, vout_ref[...].astype(BF), preferred_element_type=F32)
    o3_all[pl.ds(pl.multiple_of(step * (nb * QROWS), nb * QROWS), nb * QROWS), :] = o3.reshape(nb * QROWS, LANES)

    @pl.when(step == last_step)
    def _():
        for copy in late_copies:
            copy.wait()
        _sample_tail(xall_ref, o3_all, h0re_ref, h0im_ref, g1_ref, win_s, bg_ref, wba_s, wbs_s, wo_s,
                     lb_ref, bblk_s, ctblk_s, d_ref, wglu_ref, bglu_ref, x1_ref, hre_ref, him_ref)


def _sample_layer(x, kbuf, vbuf, h0re, h0im, rc, rs1, rs2, sinkc, g1, w_in, bg, wba, wbs, wo, lb, bblk, ctblk,
                  d, wglu, bglu):
    nb = x.shape[0]
    tb = SAMPLE_TB
    blocked = (x, kbuf, vbuf)
    early = (rc, rs1, rs2, sinkc, g1)
    consts = (x, h0re, h0im, bg, lb, d, wglu, bglu)
    late = (w_in, wba, wbs, wo, bblk, ctblk)
    kv_spec = pl.BlockSpec((tb, KV_WIDTH, WINDOW), lambda i: (i, 0, 0))
    whole = lambda shape: pl.BlockSpec(shape, lambda i: (0,) * len(shape))
    return pl.pallas_call(
        _sample_layer_kernel,
        grid=(nb // tb,),
        in_specs=[pl.BlockSpec((tb, D_MODEL), lambda i: (i, 0)), kv_spec, kv_spec]
        + [_const_spec(a.shape) for a in early]
        + [pl.BlockSpec((D_MODEL, QKV_WIDTH), lambda i: (0, 0), pipeline_mode=pl.Buffered(1))]
        + [_const_spec(a.shape) for a in consts]
        + [pl.BlockSpec(memory_space=pl.ANY) for _ in late],
        out_specs=[kv_spec, kv_spec, whole((nb, D_MODEL)), whole((N_STATES, nb)), whole((N_STATES, nb))],
        out_shape=[
            jax.ShapeDtypeStruct((nb, KV_WIDTH, WINDOW), F32),
            jax.ShapeDtypeStruct((nb, KV_WIDTH, WINDOW), F32),
            jax.ShapeDtypeStruct((nb, D_MODEL), F32),
            jax.ShapeDtypeStruct((N_STATES, nb), F32),
            jax.ShapeDtypeStruct((N_STATES, nb), F32),
        ],
        scratch_shapes=[pltpu.VMEM((tb * QROWS, LANES), F32), pltpu.VMEM((nb * QROWS, LANES), F32)]
        + [pltpu.VMEM(a.shape, a.dtype) for a in late] + [pltpu.SemaphoreType.DMA((len(late),))],
        compiler_params=pltpu.CompilerParams(
            dimension_semantics=("arbitrary",), vmem_limit_bytes=VMEM_LIMIT),
        name="sample_layer",
    )(*blocked, *early, w_in, *consts, *late)


def _sample_tail(x_ref, o3_ref, h0re_ref, h0im_ref, g1_ref, win_ref, bg_ref, wba_ref,
                 wbs_ref, wo_ref, lb_ref, bblk_ref, ctblk_ref, d_ref, wglu_ref, bglu_ref,
                 x1_ref, hre_ref, him_ref):
    nb = x_ref.shape[0]
    x = x_ref[...]
    hn = _rms(x, g1_ref[...]).astype(BF)

    lane = lax.broadcasted_iota(jnp.int32, (nb, LANES), 1)
    lo = lane < HEAD_DIM
    a = jnp.zeros((nb, D_MODEL), F32)
    zero = jnp.zeros((nb, LANES), F32)
    for h in range(N_Q_HEADS):
        oh = o3_ref[pl.ds(h, nb, stride=QROWS), :]
        oh = jnp.where(lo, oh, zero) if h < N_Q_HEADS // 2 else jnp.where(lo, zero, oh)
        w_h = wba_ref[h * HEAD_DIM:(h + 1) * HEAD_DIM, :].astype(BF)
        a = a + _dot(oh.astype(BF), jnp.concatenate([w_h, w_h], axis=0))

    u = _dot(hn, win_ref[:, U_COL0:GATE_COL0])
    ub = u.astype(BF)
    lre, lim = lb_ref[0:1, :], lb_ref[1:2, :]
    y_cols = []
    for o in range(N_OCT):
        sl = slice(o * OCT_STATES, (o + 1) * OCT_STATES)
        bu = _dot(ub[:, o * LANES:(o + 1) * LANES], bblk_ref[o].astype(BF))
        blocks = [slice(c * LANES, (c + 1) * LANES) for c in range(o * OCT_COL, (o + 1) * OCT_COL)]
        h0r = jnp.concatenate([h0re_ref[rows, :].T for rows in blocks], axis=1)
        h0i = jnp.concatenate([h0im_ref[rows, :].T for rows in blocks], axis=1)
        hr = bu[:, :OCT_STATES] + (lre[:, sl] * h0r - lim[:, sl] * h0i)
        hi = bu[:, OCT_STATES:] + (lre[:, sl] * h0i + lim[:, sl] * h0r)
        for cc, rows in enumerate(blocks):
            hre_ref[rows, :] = hr[:, cc * LANES:(cc + 1) * LANES].T
            him_ref[rows, :] = hi[:, cc * LANES:(cc + 1) * LANES].T
        y_cols.append(_dot_nt(jnp.concatenate([hr, -hi], axis=1).astype(BF), ctblk_ref[o].astype(BF)))
    y = jnp.concatenate(y_cols, axis=1) + d_ref[...] * u
    z = _gelu_tanh(y)
    gate = _dot(z.astype(BF), wglu_ref[...].astype(BF)) + bglu_ref[...]
    ssm = (z * _sigmoid(gate)).astype(BF)

    gates = _gate_cols(hn, win_ref, bg_ref, slice(0, GATE_WIDTH))
    x1_ref[...] = _merge_out(x, gates, a, _dot(ssm, wbs_ref[...].astype(BF)), wo_ref)


FFN_W_CHUNKS = 16
FFN_W_SLOTS = 4


def _stream_cast_weights(jobs):
    tasks, used = [], {}
    for w_hbm, w_bf, stage, sem in jobs:
        rows = stage.shape[1]
        for k in range(w_hbm.shape[0] // rows):
            slot = used.get(id(stage), 0) % stage.shape[0]
            used[id(stage)] = used.get(id(stage), 0) + 1
            copy = pltpu.make_async_copy(w_hbm.at[pl.ds(k * rows, rows), :], stage.at[slot], sem.at[slot])
            tasks.append((copy, w_bf, stage, slot, k * rows, rows))
    ahead = min(stage.shape[0] for _, _, stage, _ in jobs) - 1
    for copy, *_ in tasks[:ahead]:
        copy.start()
    for i, (copy, w_bf, stage, slot, row0, rows) in enumerate(tasks):
        if i + ahead < len(tasks):
            tasks[i + ahead][0].start()
        copy.wait()
        w_bf[row0:row0 + rows, :] = stage[slot].astype(BF)


def _ffn_rows(x, g2_ref, wgate, wup, wdown, gf_ref):
    h = _rms(x, g2_ref[...]).astype(BF)
    gate = _dot(h, wgate[...])
    up = _dot(h, wup[...])
    half_gate = 0.5 * gate
    act = ((half_gate + half_gate * jnp.tanh(half_gate)) * up).astype(BF)
    x2 = x + _dot(act, wdown[...])
    return _rms(x2, gf_ref[...])


def _ffn_kernel(xp_ref, xs_ref, g2_ref, wgate_hbm, wup_hbm, wdown_hbm, gf_ref, yp_ref, ys_ref,
                wgate, wup, wdown, stage_in, stage_out, sem_in, sem_out):
    i = pl.program_id(0)
    n_prompt = pl.num_programs(0) - 1

    @pl.when(i == 0)
    def _():
        _stream_cast_weights([(wgate_hbm, wgate, stage_in, sem_in), (wup_hbm, wup, stage_in, sem_in),
                              (wdown_hbm, wdown, stage_out, sem_out)])

    @pl.when(i < n_prompt)
    def _():
        half = FFN_TB // FFN_SPLIT
        for rows in (slice(h * half, (h + 1) * half) for h in range(FFN_SPLIT)):
            yp_ref[rows, :] = _ffn_rows(xp_ref[rows, :], g2_ref, wgate, wup, wdown, gf_ref)

    @pl.when(i == n_prompt)
    def _():
        ys_ref[...] = _ffn_rows(xs_ref[...], g2_ref, wgate, wup, wdown, gf_ref)


def _ffn(xp, xs, g2, wgate, wup, wdown, gf):
    n, ns = xp.shape[0], xs.shape[0]
    n_prompt = n // FFN_TB
    prompt_map = lambda i: (jnp.minimum(i, n_prompt - 1), 0)
    return pl.pallas_call(
        _ffn_kernel,
        grid=(n_prompt + 1,),
        in_specs=[
            pl.BlockSpec((FFN_TB, D_MODEL), prompt_map),
            _const_spec((ns, D_MODEL)),
            _const_spec((1, D_MODEL)),
            pl.BlockSpec(memory_space=pl.ANY),
            pl.BlockSpec(memory_space=pl.ANY),
            pl.BlockSpec(memory_space=pl.ANY),
            _const_spec((1, D_MODEL)),
        ],
        out_specs=[
            pl.BlockSpec((FFN_TB, D_MODEL), prompt_map),
            pl.BlockSpec((ns, D_MODEL), lambda i: (0, 0)),
        ],
        out_shape=[
            jax.ShapeDtypeStruct((n, D_MODEL), F32),
            jax.ShapeDtypeStruct((ns, D_MODEL), F32),
        ],
        scratch_shapes=[
            pltpu.VMEM((D_MODEL, D_FF), BF), pltpu.VMEM((D_MODEL, D_FF), BF), pltpu.VMEM((D_FF, D_MODEL), BF),
            pltpu.VMEM((FFN_W_SLOTS, D_MODEL // FFN_W_CHUNKS, D_FF), F32),
            pltpu.VMEM((FFN_W_SLOTS, D_FF // FFN_W_CHUNKS, D_MODEL), F32),
            pltpu.SemaphoreType.DMA((FFN_W_SLOTS,)), pltpu.SemaphoreType.DMA((FFN_W_SLOTS,)),
        ],
        compiler_params=pltpu.CompilerParams(
            dimension_semantics=("arbitrary",), vmem_limit_bytes=VMEM_LIMIT),
        name="ffn",
    )(xp, xs, g2, wgate, wup, wdown, gf)


def _rope_tables(pos):
    pos = np.asarray(pos, np.float64)
    inv_freq = ROPE_THETA ** (-(np.arange(ROPE_HALF, dtype=np.float64) * 2.0 / ROPE_DIM))
    ang = pos[:, None] * inv_freq[None, :]
    cos, sin = np.cos(ang), np.sin(ang)
    pad = np.zeros((pos.shape[0], HEAD_DIM - ROPE_DIM))
    zero = np.zeros_like(sin)
    rc = np.concatenate([cos, cos, pad + 1.0], axis=1)
    rs1 = np.concatenate([zero, sin, pad], axis=1)
    rs2 = np.concatenate([-sin, zero, pad], axis=1)
    rep = LANES // HEAD_DIM
    return tuple(jnp.asarray(np.tile(a, (1, rep)), F32) for a in (rc, rs1, rs2))


def _cmul(ar, ai, br, bi):
    return ar * br - ai * bi, ar * bi + ai * br


def _ssm_tables(lam_re, lam_im, log_dt, b_re, b_im, c_re, c_im):
    dt = jnp.exp(log_dt)[:, None]
    mag = jnp.exp(lam_re * dt)
    lb_re = mag * jnp.cos(lam_im * dt)
    lb_im = mag * jnp.sin(lam_im * dt)
    den = lam_re * lam_re + lam_im * lam_im
    nr = lb_re - 1.0
    k_re = ((nr * lam_re + lb_im * lam_im) / den)[..., None]
    k_im = ((lb_im * lam_re - nr * lam_im) / den)[..., None]
    bb_re = k_re * b_re - k_im * b_im
    bb_im = k_re * b_im + k_im * b_re

    a_re, a_im = lb_re, lb_im
    for _ in range(int(math.log2(CHUNK))):
        a_re, a_im = _cmul(a_re, a_im, a_re, a_im)
    s_re, s_im = a_re, a_im
    for _ in range(int(math.log2(SEG))):
        s_re, s_im = _cmul(s_re, s_im, s_re, s_im)

    eye = jnp.eye(OCT, dtype=F32).reshape(1, OCT, 1, OCT, 1)

    def block_diag(a):
        r, c = a.shape[1:]
        return (a.reshape(N_OCT, OCT, r, 1, c) * eye).reshape(N_OCT, OCT * r, OCT * c)

    bblk = jnp.concatenate([block_diag(jnp.swapaxes(bb_re, 1, 2)),
                            block_diag(jnp.swapaxes(bb_im, 1, 2))], axis=2)
    ctblk = jnp.concatenate([block_diag(c_re), block_diag(c_im)], axis=2)

    oct_cols = lambda a: a.reshape(N_OCT, 1, OCT_STATES)
    lcol = jnp.concatenate([oct_cols(lb_re), oct_cols(lb_im),
                            jnp.zeros((N_OCT, SUBLANES - 2, OCT_STATES), F32)], axis=1)

    flat = lambda a: a.reshape(1, N_STATES)
    col = lambda a: a.reshape(N_COL, 1, LANES)
    a_tab = jnp.concatenate([col(a_re), col(a_im), col(s_re), col(s_im),
                             jnp.zeros((N_COL, SUBLANES - 4, LANES), F32)], axis=1)
    lb = jnp.concatenate([flat(lb_re), flat(lb_im)], axis=0)
    return bblk, ctblk, lcol, a_tab, lb


def kernel(x_prompt, x_sample, state_k_win, state_v_win, state_ssm_re, state_ssm_im, norm1_g, w_in, b_gate, attn_sinks, ssm_lam_re, ssm_lam_im, ssm_log_dt, ssm_b_re, ssm_b_im, ssm_c_re, ssm_c_im, ssm_d, w_glu, b_glu, w_branch_attn, w_branch_ssm, w_out, norm2_g, w_ffn_gate, w_ffn_up, w_ffn_down, norm_f_g):
    depth = w_in.shape[0]
    assert depth == 1
    b, t, _ = x_prompt.shape
    nb, s_len, _ = x_sample.shape
    assert s_len == 1 and state_k_win.shape[2] == WINDOW
    l = 0
    assert w_in.shape[2] == IN_WIDTH
    w_in_f = w_in[l]
    g1 = norm1_g[l].reshape(1, D_MODEL)
    g2 = norm2_g[l].reshape(1, D_MODEL)
    gf = norm_f_g.reshape(1, D_MODEL)
    bg = b_gate[l].reshape(1, GATE_WIDTH)
    d = ssm_d[l].reshape(1, SSM_WIDTH)
    wglu = w_glu[l]
    bglu = b_glu[l].reshape(1, SSM_WIDTH)
    wba = w_branch_attn[l]
    wbs = w_branch_ssm[l]
    wo = w_out[l]
    sinks = attn_sinks[l]

    bblk, ctblk, lcol, a_tab, lb = _ssm_tables(
        ssm_lam_re[l], ssm_lam_im[l], ssm_log_dt[l], ssm_b_re[l], ssm_b_im[l], ssm_c_re[l], ssm_c_im[l])

    rc, rs1, rs2 = _rope_tables(np.arange(t))
    yssm_p, hre_p, him_p = _s5_prompt(x_prompt, g1, w_in_f, bblk, ctblk, lcol, a_tab, d)
    x1_p, kwin_p, vwin_p, w_in_b = _mix_prompt(x_prompt, yssm_p, rc, rs1, rs2, sinks, g1, w_in_f, bg, wba, wbs, wo,
                                               wglu, bglu)

    rcs, rs1s, rs2s = _rope_tables(PAST_LEN + np.arange(1))
    sinkc = jnp.concatenate([sinks, jnp.zeros((QROWS - N_Q_HEADS,), F32)]).reshape(QROWS, 1)
    xs = x_sample.reshape(nb, D_MODEL)
    key_minor = lambda a: jnp.swapaxes(a.reshape(nb, WINDOW, KV_WIDTH), 1, 2)
    kwin_s, vwin_s, x1_s, hre_s, him_s = _sample_layer(
        xs, key_minor(state_k_win[l]), key_minor(state_v_win[l]),
        state_ssm_re[l].reshape(nb, N_STATES).T, state_ssm_im[l].reshape(nb, N_STATES).T,
        rcs, rs1s, rs2s, sinkc, g1, w_in_b, bg, wba, wbs, wo, lb, bblk, ctblk,
        d, wglu, bglu)
    y_p, y_s = _ffn(x1_p, x1_s, g2, w_ffn_gate[l], w_ffn_up[l], w_ffn_down[l], gf)
    y_p = y_p.reshape(b, t, D_MODEL)
    y_s = y_s.reshape(nb, 1, D_MODEL)

    kv_shape_p = (1, b, WINDOW, N_KV_HEADS, HEAD_DIM)
    st_shape_p = (1, b, N_SSM_GROUPS, SSM_STATE)
    kv_shape_s = (1, nb, WINDOW, N_KV_HEADS, HEAD_DIM)
    st_shape_s = (1, nb, N_SSM_GROUPS, SSM_STATE)
    return (y_p, y_s,
            jnp.swapaxes(kwin_p, 1, 2).reshape(kv_shape_p), jnp.swapaxes(vwin_p, 1, 2).reshape(kv_shape_p),
            hre_p.reshape(st_shape_p), him_p.reshape(st_shape_p),
            jnp.swapaxes(kwin_s, 1, 2).reshape(kv_shape_s), jnp.swapaxes(vwin_s, 1, 2).reshape(kv_shape_s),
            hre_s.T.reshape(st_shape_s), him_s.T.reshape(st_shape_s))
```

```python
import math

import jax
import jax.numpy as jnp
import numpy as np
from jax import lax
from jax.experimental import pallas as pl
from jax.experimental.pallas import tpu as pltpu

D_MODEL = 1024
N_Q_HEADS = 8
N_KV_HEADS = 2
HEAD_DIM = 64
ATTN_WIDTH = N_Q_HEADS * HEAD_DIM
KV_WIDTH = N_KV_HEADS * HEAD_DIM
WINDOW = 128
ROPE_DIM = HEAD_DIM // 4
ROPE_HALF = ROPE_DIM // 2
ROPE_THETA = 500000.0
SSM_WIDTH = D_MODEL // 2
SSM_GROUP = 16
N_SSM_GROUPS = SSM_WIDTH // SSM_GROUP
SSM_STATE = 64
N_STATES = N_SSM_GROUPS * SSM_STATE
GATE_WIDTH = 2 * D_MODEL
QKV_WIDTH = ATTN_WIDTH + 2 * KV_WIDTH
U_COL0 = QKV_WIDTH
GATE_COL0 = U_COL0 + SSM_WIDTH
IN_WIDTH = GATE_COL0 + GATE_WIDTH
D_FF = -(-8 * D_MODEL // (3 * 256)) * 256
NORM_EPS = 1e-5
PAST_LEN = 8192

LANES = 128
SUBLANES = 8
CHUNK = 4
OCT = LANES // SSM_GROUP
N_OCT = N_SSM_GROUPS // OCT
OCT_STATES = OCT * SSM_STATE
OCT_COL = OCT_STATES // LANES
N_COL = N_STATES // LANES
N_SEG = SUBLANES
SEG = 32
S5_ROWS = N_SEG * SEG
M_PAIR = 2
U_HALF = SSM_WIDTH // 2
SEG_PITCH = SEG * CHUNK + SUBLANES
MIX_TB = 512
FFN_TB = 1024
FFN_SPLIT = 4
NEG_BIG = -1e30
LOG2_E = math.log2(math.e)
VMEM_LIMIT = 56 * 1024 * 1024

BF = jnp.bfloat16
F32 = jnp.float32


def _dot(a, b):
    return jnp.dot(a, b, preferred_element_type=F32)


def _dot_nt(a, b):
    return lax.dot_general(a, b, (((1,), (1,)), ((), ())), preferred_element_type=F32)


def _dot_nt_split(a, b):
    a_hi, b_hi = a.astype(BF), b.astype(BF)
    a_lo = (a - a_hi.astype(F32)).astype(BF)
    b_lo = (b - b_hi.astype(F32)).astype(BF)
    return _dot_nt(a_hi, b_hi) + (_dot_nt(a_hi, b_lo) + _dot_nt(a_lo, b_hi))


def _rms(x, g):
    return x * lax.rsqrt(jnp.mean(x * x, axis=-1, keepdims=True) + NORM_EPS) * g


def _sigmoid(x):
    return 1.0 / (1.0 + jnp.exp(-x))


def _gelu_tanh(x):
    c = math.sqrt(2.0 / math.pi)
    return 0.5 * x * (1.0 + jnp.tanh(c * (x + 0.044715 * (x * x * x))))


def _rope(a, rc, rs1, rs2):
    return a * rc + pltpu.roll(a, ROPE_HALF, 1) * rs1 + pltpu.roll(a, LANES - ROPE_HALF, 1) * rs2


def _const_spec(shape):
    nd = len(shape)
    return pl.BlockSpec(shape, lambda *_: (0,) * nd, pipeline_mode=pl.Buffered(1))


def _build_chunk_operators(bblk_ref, ctblk_ref, lcol_ref, m_s, e_s, f_s):
    for o in range(N_OCT):
        ct = jnp.concatenate([ctblk_ref[o, :, :OCT_STATES], -ctblk_ref[o, :, OCT_STATES:]],
                             axis=1)
        lr, li = lcol_ref[o, 0:1, :], lcol_ref[o, 1:2, :]
        er, ei = bblk_ref[o, :, :OCT_STATES], bblk_ref[o, :, OCT_STATES:]
        k_blk = []
        for tau in range(CHUNK):
            e_cat = jnp.concatenate([er, ei], axis=1)
            i = CHUNK - 1 - tau
            e_s[o, i * LANES:(i + 1) * LANES, :] = e_cat.astype(BF)
            k_blk.append(_dot_nt_split(e_cat, ct).astype(BF))
            er, ei = er * lr - ei * li, er * li + ei * lr
        zero = jnp.zeros((LANES, LANES), BF)
        for j in range(CHUNK):
            jt, jj = divmod(j, M_PAIR)
            for i in range(M_PAIR * (jt + 1)):
                m_s[jt][o, i * LANES:(i + 1) * LANES, jj * LANES:(jj + 1) * LANES] = (
                    k_blk[j - i] if j >= i else zero)
        tr, ti = ct[:, :OCT_STATES], -ct[:, OCT_STATES:]
        for j in range(CHUNK):
            tr, ti = tr * lr - ti * li, tr * li + ti * lr
            f_s[o, :OCT_STATES, j * LANES:(j + 1) * LANES] = tr.T.astype(BF)
            f_s[o, OCT_STATES:, j * LANES:(j + 1) * LANES] = (-ti).T.astype(BF)


def _s5_prompt_kernel(x_ref, g1_ref, wu0_ref, wu1_ref, bblk_ref, ctblk_ref, lcol_ref, a_ref, d_ref,
                      out_ref, hfin_ref, *scratch):
    m_ref = scratch[:CHUNK // M_PAIR]
    e_ref, f_ref, us, ys, sre, sim, car = scratch[CHUNK // M_PAIR:]
    blk = pl.program_id(1)

    @pl.when((pl.program_id(0) == 0) & (blk == 0))
    def _():
        _build_chunk_operators(bblk_ref, ctblk_ref, lcol_ref, m_ref, e_ref, f_ref)

    @pl.when(blk == 0)
    def _():
        car[...] = jnp.zeros_like(car)

    seg_tokens = SEG * CHUNK
    seg_rows = [slice(s * SEG_PITCH, s * SEG_PITCH + seg_tokens) for s in range(N_SEG)]
    tok_rows = [slice(s * seg_tokens, (s + 1) * seg_tokens) for s in range(N_SEG)]

    for s in range(0, N_SEG, 2):
        hn = _rms(x_ref[s * seg_tokens:(s + 2) * seg_tokens, :], g1_ref[...]).astype(BF)
        u = jnp.concatenate([_dot(hn, wu0_ref[...].astype(BF)), _dot(hn, wu1_ref[...].astype(BF))],
                            axis=1)
        for half in range(2):
            for cc in range(N_OCT):
                us[cc, seg_rows[s + half], :] = u[tok_rows[half], cc * LANES:(cc + 1) * LANES]

    sub = lax.broadcasted_iota(jnp.int32, (N_SEG, LANES), 0)

    def scan_column(c):
        tab = a_ref[c]
        are, aim = tab[0:1], tab[1:2]
        bre, bim = tab[2:3], tab[3:4]

        def step(cr, ci, r, keep_entering):
            slab = slice(r * N_SEG, (r + 1) * N_SEG)
            s_r, s_i = sre[c, slab, :], sim[c, slab, :]
            if keep_entering:
                sre[c, slab, :] = cr
                sim[c, slab, :] = ci
            return are * cr - aim * ci + s_r, are * ci + aim * cr + s_i

        cr = jnp.zeros((N_SEG, LANES), F32)
        ci = jnp.zeros((N_SEG, LANES), F32)
        for r in range(SEG):
            cr, ci = step(cr, ci, r, False)
        cv = car[c]
        pr, pi = cv[0:1], cv[1:2]
        sr = jnp.zeros((N_SEG, LANES), F32)
        si = jnp.zeros((N_SEG, LANES), F32)
        for s in range(N_SEG):
            sr = jnp.where(sub == s, pr, sr)
            si = jnp.where(sub == s, pi, si)
            pr, pi = (bre * pr - bim * pi + cr[s:s + 1], bre * pi + bim * pr + ci[s:s + 1])
        end = jnp.where(sub == 0, pr, jnp.where(sub == 1, pi, 0.0))
        car[c] = end
        hfin_ref[c] = end
        cr, ci = sr, si
        for r in range(SEG):
            cr, ci = step(cr, ci, r, True)

    for o in range(N_OCT):
        uo = jnp.concatenate(
            [jnp.concatenate([us[o, pl.ds(r * CHUNK + i, N_SEG, stride=SEG_PITCH), :] for r in range(SEG)],
                             axis=0).astype(BF) for i in range(CHUNK)], axis=1)
        s_end = _dot(uo, e_ref[o])
        cols = range(o * OCT_COL, (o + 1) * OCT_COL)
        for cc, c in enumerate(cols):
            sre[c] = s_end[:, cc * LANES:(cc + 1) * LANES]
            sim[c] = s_end[:, OCT_STATES + cc * LANES:OCT_STATES + (cc + 1) * LANES]
        y_in = jnp.concatenate(
            [_dot(uo[:, :(jt + 1) * M_PAIR * LANES], m_ref[jt][o]) for jt in range(CHUNK // M_PAIR)],
            axis=1)
        for c in cols:
            scan_column(c)
        hp = jnp.concatenate([sre[c] for c in cols] + [sim[c] for c in cols], axis=1).astype(BF)
        yo = y_in + _dot(hp, f_ref[o])
        for r in range(SEG):
            for j in range(CHUNK):
                ys[o, pl.ds(r * CHUNK + j, N_SEG, stride=SEG_PITCH), :] = (
                    yo[r * N_SEG:(r + 1) * N_SEG, j * LANES:(j + 1) * LANES])
        d_o = d_ref[:, o * LANES:(o + 1) * LANES]
        for s in range(N_SEG):
            out_ref[tok_rows[s], o * LANES:(o + 1) * LANES] = (
                ys[o, seg_rows[s], :] + d_o * us[o, seg_rows[s], :])


def _s5_prompt(x, g1, w_in, bblk, ctblk, lcol, a_tab, d):
    b, t, _ = x.shape
    tb = S5_ROWS * CHUNK
    nblk = t // tb
    x2 = x.reshape(b * t, D_MODEL)
    row_map = lambda i, j: (i * nblk + j, 0)
    e_shape = (N_OCT, CHUNK * LANES, 2 * OCT_STATES)
    f_shape = (N_OCT, 2 * OCT_STATES, CHUNK * LANES)
    out, hfin = pl.pallas_call(
        _s5_prompt_kernel,
        grid=(b, nblk),
        in_specs=[
            pl.BlockSpec((tb, D_MODEL), row_map),
            _const_spec((1, D_MODEL)),
            *[pl.BlockSpec((D_MODEL, U_HALF), lambda i, j, c=U_COL0 // U_HALF + h: (0, c),
                           pipeline_mode=pl.Buffered(1)) for h in range(2)],
            _const_spec(bblk.shape),
            _const_spec(ctblk.shape),
            _const_spec(lcol.shape),
            _const_spec(a_tab.shape),
            _const_spec((1, SSM_WIDTH)),
        ],
        out_specs=[
            pl.BlockSpec((tb, SSM_WIDTH), row_map),
            pl.BlockSpec((None, N_COL, SUBLANES, LANES), lambda i, j: (i, 0, 0, 0)),
        ],
        out_shape=[
            jax.ShapeDtypeStruct((b * t, SSM_WIDTH), F32),
            jax.ShapeDtypeStruct((b, N_COL, SUBLANES, LANES), F32),
        ],
        scratch_shapes=[
            *[pltpu.VMEM((N_OCT, (jt + 1) * M_PAIR * LANES, M_PAIR * LANES), BF)
              for jt in range(CHUNK // M_PAIR)],
            pltpu.VMEM(e_shape, BF), pltpu.VMEM(f_shape, BF),
            pltpu.VMEM((N_OCT, N_SEG * SEG_PITCH, LANES), F32),
            pltpu.VMEM((N_OCT, N_SEG * SEG_PITCH, LANES), F32),
            pltpu.VMEM((N_COL, S5_ROWS, LANES), F32), pltpu.VMEM((N_COL, S5_ROWS, LANES), F32),
            pltpu.VMEM((N_COL, SUBLANES, LANES), F32),
        ],
        compiler_params=pltpu.CompilerParams(
            dimension_semantics=("arbitrary", "arbitrary"), vmem_limit_bytes=VMEM_LIMIT),
        name="s5_prompt",
    )(x2, g1, w_in, w_in, bblk, ctblk, lcol, a_tab, d)
    return out, hfin[:, :, 0, :], hfin[:, :, 1, :]


def _gate_cols(hn, win_ref, bg_ref, cols):
    w_cols = slice(GATE_COL0 + cols.start, GATE_COL0 + cols.stop)
    return _sigmoid(_dot(hn, win_ref[:, w_cols]) + bg_ref[:, cols])


def _merge_out(x, gates, attn_proj, ssm_proj, wo_ref):
    merged = gates[:, :D_MODEL] * attn_proj + gates[:, D_MODEL:] * ssm_proj
    return x + _dot(merged.astype(BF), wo_ref[...])


def _half_split(a, ar, lo):
    z = jnp.zeros_like(a)
    return (jnp.where(lo, a, z).astype(BF), jnp.where(lo, z, ar).astype(BF),
            jnp.where(lo, ar, z).astype(BF), jnp.where(lo, z, a).astype(BF))


def _softmax_terms_t(st, bias_t, sink):
    st = st + bias_t
    m = jnp.maximum(jnp.max(st, axis=0, keepdims=True), sink)
    return jnp.exp2(st - m).astype(BF), jnp.exp2(sink - m)


def _mix_prompt_kernel(sinks_ref, x_ref, y_ref, rc_ref, rs1_ref, rs2_ref, g1_ref, win_hbm,
                       bg_ref, wba_ref, wbs_ref, wo_ref, wglu_ref, bglu_ref,
                       x1_ref, kwin_ref, vwin_ref, winb_hbm, kprev, vprev, win_ref, stage, sem_stage, sem_out):
    t = pl.program_id(1)
    seq = pl.program_id(0)
    share_copy = pltpu.make_async_copy(win_ref, winb_hbm, sem_out.at[0])

    @pl.when((seq == 0) & (t == 0))
    def _():
        _stream_cast_weights([(win_hbm, win_ref, stage, sem_stage)])
        share_copy.start()

    @pl.when((seq == pl.num_programs(0) - 1) & (t == pl.num_programs(1) - 1))
    def _():
        share_copy.wait()

    @pl.when(t == 0)
    def _():
        kprev[...] = jnp.zeros_like(kprev)
        vprev[...] = jnp.zeros_like(vprev)

    x = x_ref[...]
    hn = _rms(x, g1_ref[...]).astype(BF)
    qkv = _dot(hn, win_ref[:, :QKV_WIDTH])
    z = _gelu_tanh(y_ref[...])
    glu = _dot(z.astype(BF), wglu_ref[...]) + bglu_ref[...]
    rc, rs1, rs2 = rc_ref[...], rs1_ref[...], rs2_ref[...]
    scale = HEAD_DIM ** -0.5 * LOG2_E
    q = [(_rope(qkv[:, c * LANES:(c + 1) * LANES], rc, rs1, rs2) * scale).astype(BF)
         for c in range(ATTN_WIDTH // LANES)]
    k = _rope(qkv[:, ATTN_WIDTH:ATTN_WIDTH + KV_WIDTH], rc, rs1, rs2)
    v = qkv[:, ATTN_WIDTH + KV_WIDTH:ATTN_WIDTH + 2 * KV_WIDTH]
    ssm = (z * _sigmoid(glu)).astype(BF)
    ssm_proj = _dot(ssm, wbs_ref[...])

    w = WINDOW
    n_sub = MIX_TB // w
    lane = lax.broadcasted_iota(jnp.int32, (2 * w, LANES), 1)
    lo = lane < HEAD_DIM
    kj = lax.broadcasted_iota(jnp.int32, (2 * w, w), 0)
    qi = lax.broadcasted_iota(jnp.int32, (2 * w, w), 1)
    band = (kj > qi) & (kj <= qi + w)
    first = band & ((kj >= w) | (t > 0))
    bias_band = jnp.where(band, 0.0, NEG_BIG).astype(F32)
    bias_first = jnp.where(first, 0.0, NEG_BIG).astype(F32)
    col = lax.broadcasted_iota(jnp.int32, (1, 2 * w), 1)
    sum_r = lax.broadcasted_iota(jnp.int32, (2 * SUBLANES, 4 * w), 0)
    sum_c = lax.broadcasted_iota(jnp.int32, (2 * SUBLANES, 4 * w), 1)
    sum_rows = (((sum_r == 0) & (sum_c < 2 * w)) | ((sum_r == 1) & (sum_c >= 2 * w))).astype(F32)

    def sink_row(ha, hb):
        return jnp.where(col < w, sinks_ref[ha], sinks_ref[hb]) * LOG2_E

    gate_w = GATE_WIDTH // (n_sub * N_KV_HEADS)
    operands = {}

    def sub_block_operands(sb):
        if sb not in operands:
            cur = slice(sb * w, (sb + 1) * w)
            if sb == 0:
                kcat = jnp.concatenate([kprev[...], k[cur]], axis=0)
                vcat = jnp.concatenate([vprev[...], v[cur]], axis=0)
                bias = bias_first
            else:
                kcat = k[(sb - 1) * w:(sb + 1) * w]
                vcat = v[(sb - 1) * w:(sb + 1) * w]
                bias = bias_band
            k_split = _half_split(kcat, pltpu.roll(kcat, HEAD_DIM, 1), lo)
            k_stack = [jnp.concatenate(k_split[2 * g:2 * g + 2], axis=0) for g in range(N_KV_HEADS)]
            vt = vcat.T
            zero = jnp.zeros((HEAD_DIM, 2 * w), F32)
            v_stack = []
            for g in range(N_KV_HEADS):
                vg = vt[g * HEAD_DIM:(g + 1) * HEAD_DIM]
                v_stack.append(jnp.concatenate(
                    [jnp.concatenate([vg, zero], axis=1), jnp.concatenate([zero, vg], axis=1), sum_rows],
                    axis=0).astype(BF))
            operands[sb] = (k_stack, v_stack, jnp.concatenate([bias, bias], axis=1))
        return operands[sb]

    def scores(sb, grp):
        k_stack, _, _ = sub_block_operands(sb)
        cur = slice(sb * w, (sb + 1) * w)
        qq = jnp.concatenate([q[2 * grp][cur], q[2 * grp + 1][cur]], axis=0)
        return _dot_nt(k_stack[grp], qq)

    def weighted_values(sb, grp, s):
        _, v_stack, bias2 = sub_block_operands(sb)
        p_e, sink_e = _softmax_terms_t(s[:2 * w], bias2, sink_row(4 * grp, 4 * grp + 2))
        p_o, sink_o = _softmax_terms_t(s[2 * w:], bias2, sink_row(4 * grp + 1, 4 * grp + 3))
        d = _dot(v_stack[grp], jnp.concatenate([p_e, p_o], axis=0))
        heads = 2 * HEAD_DIM
        o2 = jnp.concatenate(
            [d[:HEAD_DIM] * (1.0 / (d[heads:heads + 1] + sink_e)),
             d[HEAD_DIM:heads] * (1.0 / (d[heads + 1:heads + 2] + sink_o))], axis=0)
        return [o2[:, :w].T, o2[:, w:].T]

    chains = [(sb, grp) for sb in range(n_sub) for grp in range(N_KV_HEADS)]
    gate_cols, outs = [], []
    s_next = scores(*chains[0])
    for i, (sb, grp) in enumerate(chains):
        s_cur = s_next
        gate_cols.append(_gate_cols(hn, win_ref, bg_ref, slice(i * gate_w, (i + 1) * gate_w)))
        if i + 1 < len(chains):
            s_next = scores(*chains[i + 1])
        outs += weighted_values(sb, grp, s_cur)
    per_sb = 2 * N_KV_HEADS
    attn = jnp.concatenate(
        [jnp.concatenate(outs[sb * per_sb:(sb + 1) * per_sb], axis=1) for sb in range(n_sub)],
        axis=0).astype(BF)
    gates = jnp.concatenate(gate_cols, axis=1)

    kprev[...] = k[MIX_TB - w:]
    vprev[...] = v[MIX_TB - w:]
    kwin_ref[...] = k[MIX_TB - w:].T
    vwin_ref[...] = v[MIX_TB - w:].T
    x1_ref[...] = _merge_out(x, gates, _dot(attn, wba_ref[...]), ssm_proj, wo_ref)


def _mix_prompt(x, y_ssm, rc, rs1, rs2, sinks, g1, w_in, bg, wba, wbs, wo, wglu, bglu):
    b, t, _ = x.shape
    nblk = t // MIX_TB
    x2 = x.reshape(b * t, D_MODEL)
    row_map = lambda i, j: (i * nblk + j, 0)
    x1, kwin, vwin, w_in_bf = pl.pallas_call(
        _mix_prompt_kernel,
        grid=(b, nblk),
        in_specs=[
            pl.BlockSpec(memory_space=pltpu.SMEM),
            pl.BlockSpec((MIX_TB, D_MODEL), row_map),
            pl.BlockSpec((MIX_TB, SSM_WIDTH), row_map),
            pl.BlockSpec((MIX_TB, LANES), lambda i, j: (j, 0)),
            pl.BlockSpec((MIX_TB, LANES), lambda i, j: (j, 0)),
            pl.BlockSpec((MIX_TB, LANES), lambda i, j: (j, 0)),
            _const_spec((1, D_MODEL)),
            pl.BlockSpec(memory_space=pl.ANY),
            _const_spec(bg.shape),
            _const_spec(wba.shape),
            _const_spec(wbs.shape),
            _const_spec(wo.shape),
            _const_spec(wglu.shape),
            _const_spec(bglu.shape),
        ],
        out_specs=[
            pl.BlockSpec((MIX_TB, D_MODEL), row_map),
            pl.BlockSpec((None, KV_WIDTH, WINDOW), lambda i, j: (i, 0, 0)),
            pl.BlockSpec((None, KV_WIDTH, WINDOW), lambda i, j: (i, 0, 0)),
            pl.BlockSpec(memory_space=pl.ANY),
        ],
        out_shape=[
            jax.ShapeDtypeStruct((b * t, D_MODEL), F32),
            jax.ShapeDtypeStruct((b, KV_WIDTH, WINDOW), F32),
            jax.ShapeDtypeStruct((b, KV_WIDTH, WINDOW), F32),
            jax.ShapeDtypeStruct(w_in.shape, BF),
        ],
        scratch_shapes=[pltpu.VMEM((WINDOW, KV_WIDTH), F32), pltpu.VMEM((WINDOW, KV_WIDTH), F32),
                        pltpu.VMEM(w_in.shape, BF),
                        pltpu.VMEM((FFN_W_SLOTS, D_MODEL // FFN_W_CHUNKS, IN_WIDTH), F32),
                        pltpu.SemaphoreType.DMA((FFN_W_SLOTS,)), pltpu.SemaphoreType.DMA((1,))],
        compiler_params=pltpu.CompilerParams(
            dimension_semantics=("arbitrary", "arbitrary"), vmem_limit_bytes=VMEM_LIMIT),
        name="mix_prompt",
    )(sinks, x2, y_ssm, rc, rs1, rs2, g1, w_in, bg, wba, wbs, wo, wglu, bglu)
    return x1, kwin, vwin, w_in_bf


QROWS = 16
SAMPLE_TB = 32


def _sample_layer_kernel(x_ref, kbuf_ref, vbuf_ref, rc_ref, rs1_ref, rs2_ref, sinkc_ref, g1_ref, wqkv_ref,
                         xall_ref, h0re_ref, h0im_ref, bg_ref, lb_ref, d_ref, wglu_ref, bglu_ref,
                         win_hbm, wba_hbm, wbs_hbm, wo_hbm, bblk_hbm, ctblk_hbm,
                         kout_ref, vout_ref, x1_ref, hre_ref, him_ref,
                         qz, o3_all, win_s, wba_s, wbs_s, wo_s, bblk_s, ctblk_s, sem):
    nb = x_ref.shape[0]
    step = pl.program_id(0)
    last_step = pl.num_programs(0) - 1
    late = ((win_hbm, win_s), (wba_hbm, wba_s), (wbs_hbm, wbs_s), (wo_hbm, wo_s), (bblk_hbm, bblk_s),
            (ctblk_hbm, ctblk_s))
    late_copies = [pltpu.make_async_copy(src, dst, sem.at[i]) for i, (src, dst) in enumerate(late)]

    @pl.when(step == 0)
    def _():
        for copy in late_copies:
            copy.start()

    hn = _rms(x_ref[...], g1_ref[...]).astype(BF)
    qkv = _dot(hn, wqkv_ref[...])
    rc, rs1, rs2 = rc_ref[...], rs1_ref[...], rs2_ref[...]
    scale = HEAD_DIM ** -0.5
    k_new = _rope(qkv[:, ATTN_WIDTH:ATTN_WIDTH + KV_WIDTH], rc, rs1, rs2)
    v_new = qkv[:, ATTN_WIDTH + KV_WIDTH:ATTN_WIDTH + 2 * KV_WIDTH]
    pad = jnp.zeros((LANES - nb, KV_WIDTH), F32)
    k_new_t = jnp.concatenate([k_new, pad], axis=0).T
    v_new_t = jnp.concatenate([v_new, pad], axis=0).T

    lane = lax.broadcasted_iota(jnp.int32, (nb, LANES), 1)
    lo = lane < HEAD_DIM
    qz[...] = jnp.zeros_like(qz)
    for c in range(ATTN_WIDTH // LANES):
        qc = _rope(qkv[:, c * LANES:(c + 1) * LANES], rc, rs1, rs2) * scale
        qr = pltpu.roll(qc, HEAD_DIM, 1)
        zero = jnp.zeros_like(qc)
        if c < 2:
            even, odd = jnp.where(lo, qc, zero), jnp.where(lo, qr, zero)
        else:
            even, odd = jnp.where(lo, zero, qr), jnp.where(lo, zero, qc)
        qz[pl.ds(2 * c, nb, stride=QROWS), :] = even
        qz[pl.ds(2 * c + 1, nb, stride=QROWS), :] = odd

    last = lax.broadcasted_iota(jnp.int32, (KV_WIDTH, WINDOW), 1) == WINDOW - 1
    for b in range(nb):
        kout_ref[b] = jnp.where(last, k_new_t[:, b:b + 1], pltpu.roll(kbuf_ref[b], WINDOW - 1, 1))
        vout_ref[b] = jnp.where(last, v_new_t[:, b:b + 1], pltpu.roll(vbuf_ref[b], WINDOW - 1, 1))

    sink = sinkc_ref[...]
    q3 = qz[...].reshape(nb, QROWS, LANES).astype(BF)
    s = jnp.einsum('bhd,bdk->bhk', q3, kout_ref[...].astype(BF), preferred_element_type=F32)
    m = jnp.maximum(jnp.max(s, axis=-1, keepdims=True), sink)
    p = jnp.exp(s - m)
    den = jnp.sum(p, axis=-1, keepdims=True) + jnp.exp(sink - m)
    p = (p * (1.0 / den)).astype(BF)
    o3 = jnp.einsum('bhk,bdk->bhd', p, vout_ref[...].astype(BF), preferred_element_type=F32)
    o3_all[pl.ds(pl.multiple_of(step * (nb * QROWS), nb * QROWS), nb * QROWS), :] = o3.reshape(nb * QROWS, LANES)

    @pl.when(step == last_step)
    def _():
        for copy in late_copies:
            copy.wait()
        _sample_tail(xall_ref, o3_all, h0re_ref, h0im_ref, g1_ref, win_s, bg_ref, wba_s, wbs_s, wo_s,
                     lb_ref, bblk_s, ctblk_s, d_ref, wglu_ref, bglu_ref, x1_ref, hre_ref, him_ref)


def _sample_layer(x, kbuf, vbuf, h0re, h0im, rc, rs1, rs2, sinkc, g1, w_in, bg, wba, wbs, wo, lb, bblk, ctblk,
                  d, wglu, bglu):
    nb = x.shape[0]
    tb = SAMPLE_TB
    blocked = (x, kbuf, vbuf)
    early = (rc, rs1, rs2, sinkc, g1)
    consts = (x, h0re, h0im, bg, lb, d, wglu, bglu)
    late = (w_in, wba, wbs, wo, bblk, ctblk)
    kv_spec = pl.BlockSpec((tb, KV_WIDTH, WINDOW), lambda i: (i, 0, 0))
    whole = lambda shape: pl.BlockSpec(shape, lambda i: (0,) * len(shape))
    return pl.pallas_call(
        _sample_layer_kernel,
        grid=(nb // tb,),
        in_specs=[pl.BlockSpec((tb, D_MODEL), lambda i: (i, 0)), kv_spec, kv_spec]
        + [_const_spec(a.shape) for a in early]
        + [pl.BlockSpec((D_MODEL, QKV_WIDTH), lambda i: (0, 0), pipeline_mode=pl.Buffered(1))]
        + [_const_spec(a.shape) for a in consts]
        + [pl.BlockSpec(memory_space=pl.ANY) for _ in late],
        out_specs=[kv_spec, kv_spec, whole((nb, D_MODEL)), whole((N_STATES, nb)), whole((N_STATES, nb))],
        out_shape=[
            jax.ShapeDtypeStruct((nb, KV_WIDTH, WINDOW), F32),
            jax.ShapeDtypeStruct((nb, KV_WIDTH, WINDOW), F32),
            jax.ShapeDtypeStruct((nb, D_MODEL), F32),
            jax.ShapeDtypeStruct((N_STATES, nb), F32),
            jax.ShapeDtypeStruct((N_STATES, nb), F32),
        ],
        scratch_shapes=[pltpu.VMEM((tb * QROWS, LANES), F32), pltpu.VMEM((nb * QROWS, LANES), F32)]
        + [pltpu.VMEM(a.shape, a.dtype) for a in late] + [pltpu.SemaphoreType.DMA((len(late),))],
        compiler_params=pltpu.CompilerParams(
            dimension_semantics=("arbitrary",), vmem_limit_bytes=VMEM_LIMIT),
        name="sample_layer",
    )(*blocked, *early, w_in, *consts, *late)


def _sample_tail(x_ref, o3_ref, h0re_ref, h0im_ref, g1_ref, win_ref, bg_ref, wba_ref,
                 wbs_ref, wo_ref, lb_ref, bblk_ref, ctblk_ref, d_ref, wglu_ref, bglu_ref,
                 x1_ref, hre_ref, him_ref):
    nb = x_ref.shape[0]
    x = x_ref[...]
    hn = _rms(x, g1_ref[...]).astype(BF)

    lane = lax.broadcasted_iota(jnp.int32, (nb, LANES), 1)
    lo = lane < HEAD_DIM
    a = jnp.zeros((nb, D_MODEL), F32)
    zero = jnp.zeros((nb, LANES), F32)
    for h in range(N_Q_HEADS):
        oh = o3_ref[pl.ds(h, nb, stride=QROWS), :]
        oh = jnp.where(lo, oh, zero) if h < N_Q_HEADS // 2 else jnp.where(lo, zero, oh)
        w_h = wba_ref[h * HEAD_DIM:(h + 1) * HEAD_DIM, :]
        a = a + _dot(oh.astype(BF), jnp.concatenate([w_h, w_h], axis=0))

    u = _dot(hn, win_ref[:, U_COL0:GATE_COL0])
    ub = u.astype(BF)
    lre, lim = lb_ref[0:1, :], lb_ref[1:2, :]
    y_cols = []
    for o in range(N_OCT):
        sl = slice(o * OCT_STATES, (o + 1) * OCT_STATES)
        bu = _dot(ub[:, o * LANES:(o + 1) * LANES], bblk_ref[o].astype(BF))
        blocks = [slice(c * LANES, (c + 1) * LANES) for c in range(o * OCT_COL, (o + 1) * OCT_COL)]
        h0r = jnp.concatenate([h0re_ref[rows, :].T for rows in blocks], axis=1)
        h0i = jnp.concatenate([h0im_ref[rows, :].T for rows in blocks], axis=1)
        hr = bu[:, :OCT_STATES] + (lre[:, sl] * h0r - lim[:, sl] * h0i)
        hi = bu[:, OCT_STATES:] + (lre[:, sl] * h0i + lim[:, sl] * h0r)
        for cc, rows in enumerate(blocks):
            hre_ref[rows, :] = hr[:, cc * LANES:(cc + 1) * LANES].T
            him_ref[rows, :] = hi[:, cc * LANES:(cc + 1) * LANES].T
        y_cols.append(_dot_nt(jnp.concatenate([hr, -hi], axis=1).astype(BF), ctblk_ref[o].astype(BF)))
    y = jnp.concatenate(y_cols, axis=1) + d_ref[...] * u
    z = _gelu_tanh(y)
    gate = _dot(z.astype(BF), wglu_ref[...]) + bglu_ref[...]
    ssm = (z * _sigmoid(gate)).astype(BF)

    gates = _gate_cols(hn, win_ref, bg_ref, slice(0, GATE_WIDTH))
    x1_ref[...] = _merge_out(x, gates, a, _dot(ssm, wbs_ref[...]), wo_ref)


FFN_W_CHUNKS = 16
FFN_W_SLOTS = 4


def _stream_cast_weights(jobs):
    tasks, used = [], {}
    for w_hbm, w_bf, stage, sem in jobs:
        rows = stage.shape[1]
        for k in range(w_hbm.shape[0] // rows):
            slot = used.get(id(stage), 0) % stage.shape[0]
            used[id(stage)] = used.get(id(stage), 0) + 1
            copy = pltpu.make_async_copy(w_hbm.at[pl.ds(k * rows, rows), :], stage.at[slot], sem.at[slot])
            tasks.append((copy, w_bf, stage, slot, k * rows, rows))
    ahead = min(stage.shape[0] for _, _, stage, _ in jobs) - 1
    for copy, *_ in tasks[:ahead]:
        copy.start()
    for i, (copy, w_bf, stage, slot, row0, rows) in enumerate(tasks):
        if i + ahead < len(tasks):
            tasks[i + ahead][0].start()
        copy.wait()
        w_bf[row0:row0 + rows, :] = stage[slot].astype(BF)


def _ffn_rows(x, g2_ref, wgate, wup, wdown, gf_ref):
    h = _rms(x, g2_ref[...]).astype(BF)
    gate = _dot(h, wgate[...])
    up = _dot(h, wup[...])
    half_gate = 0.5 * gate
    act = ((half_gate + half_gate * jnp.tanh(half_gate)) * up).astype(BF)
    x2 = x + _dot(act, wdown[...])
    return _rms(x2, gf_ref[...])


def _ffn_kernel(xp_ref, xs_ref, g2_ref, wgate_hbm, wup_hbm, wdown_hbm, gf_ref, yp_ref, ys_ref,
                wgate, wup, wdown, stage_in, stage_out, sem_in, sem_out):
    i = pl.program_id(0)
    n_prompt = pl.num_programs(0) - 1

    @pl.when(i == 0)
    def _():
        _stream_cast_weights([(wgate_hbm, wgate, stage_in, sem_in), (wup_hbm, wup, stage_in, sem_in),
                              (wdown_hbm, wdown, stage_out, sem_out)])

    @pl.when(i < n_prompt)
    def _():
        half = FFN_TB // FFN_SPLIT
        for rows in (slice(h * half, (h + 1) * half) for h in range(FFN_SPLIT)):
            yp_ref[rows, :] = _ffn_rows(xp_ref[rows, :], g2_ref, wgate, wup, wdown, gf_ref)

    @pl.when(i == n_prompt)
    def _():
        ys_ref[...] = _ffn_rows(xs_ref[...], g2_ref, wgate, wup, wdown, gf_ref)


def _ffn(xp, xs, g2, wgate, wup, wdown, gf):
    n, ns = xp.shape[0], xs.shape[0]
    n_prompt = n // FFN_TB
    prompt_map = lambda i: (jnp.minimum(i, n_prompt - 1), 0)
    return pl.pallas_call(
        _ffn_kernel,
        grid=(n_prompt + 1,),
        in_specs=[
            pl.BlockSpec((FFN_TB, D_MODEL), prompt_map),
            _const_spec((ns, D_MODEL)),
            _const_spec((1, D_MODEL)),
            pl.BlockSpec(memory_space=pl.ANY),
            pl.BlockSpec(memory_space=pl.ANY),
            pl.BlockSpec(memory_space=pl.ANY),
            _const_spec((1, D_MODEL)),
        ],
        out_specs=[
            pl.BlockSpec((FFN_TB, D_MODEL), prompt_map),
            pl.BlockSpec((ns, D_MODEL), lambda i: (0, 0)),
        ],
        out_shape=[
            jax.ShapeDtypeStruct((n, D_MODEL), F32),
            jax.ShapeDtypeStruct((ns, D_MODEL), F32),
        ],
        scratch_shapes=[
            pltpu.VMEM((D_MODEL, D_FF), BF), pltpu.VMEM((D_MODEL, D_FF), BF), pltpu.VMEM((D_FF, D_MODEL), BF),
            pltpu.VMEM((FFN_W_SLOTS, D_MODEL // FFN_W_CHUNKS, D_FF), F32),
            pltpu.VMEM((FFN_W_SLOTS, D_FF // FFN_W_CHUNKS, D_MODEL), F32),
            pltpu.SemaphoreType.DMA((FFN_W_SLOTS,)), pltpu.SemaphoreType.DMA((FFN_W_SLOTS,)),
        ],
        compiler_params=pltpu.CompilerParams(
            dimension_semantics=("arbitrary",), vmem_limit_bytes=VMEM_LIMIT),
        name="ffn",
    )(xp, xs, g2, wgate, wup, wdown, gf)


def _rope_tables(pos):
    pos = np.asarray(pos, np.float64)
    inv_freq = ROPE_THETA ** (-(np.arange(ROPE_HALF, dtype=np.float64) * 2.0 / ROPE_DIM))
    ang = pos[:, None] * inv_freq[None, :]
    cos, sin = np.cos(ang), np.sin(ang)
    pad = np.zeros((pos.shape[0], HEAD_DIM - ROPE_DIM))
    zero = np.zeros_like(sin)
    rc = np.concatenate([cos, cos, pad + 1.0], axis=1)
    rs1 = np.concatenate([zero, sin, pad], axis=1)
    rs2 = np.concatenate([-sin, zero, pad], axis=1)
    rep = LANES // HEAD_DIM
    return tuple(jnp.asarray(np.tile(a, (1, rep)), F32) for a in (rc, rs1, rs2))


def _cmul(ar, ai, br, bi):
    return ar * br - ai * bi, ar * bi + ai * br


def _ssm_tables(lam_re, lam_im, log_dt, b_re, b_im, c_re, c_im):
    dt = jnp.exp(log_dt)[:, None]
    mag = jnp.exp(lam_re * dt)
    lb_re = mag * jnp.cos(lam_im * dt)
    lb_im = mag * jnp.sin(lam_im * dt)
    den = lam_re * lam_re + lam_im * lam_im
    nr = lb_re - 1.0
    k_re = ((nr * lam_re + lb_im * lam_im) / den)[..., None]
    k_im = ((lb_im * lam_re - nr * lam_im) / den)[..., None]
    bb_re = k_re * b_re - k_im * b_im
    bb_im = k_re * b_im + k_im * b_re

    a_re, a_im = lb_re, lb_im
    for _ in range(int(math.log2(CHUNK))):
        a_re, a_im = _cmul(a_re, a_im, a_re, a_im)
    s_re, s_im = a_re, a_im
    for _ in range(int(math.log2(SEG))):
        s_re, s_im = _cmul(s_re, s_im, s_re, s_im)

    eye = jnp.eye(OCT, dtype=F32).reshape(1, OCT, 1, OCT, 1)

    def block_diag(a):
        r, c = a.shape[1:]
        return (a.reshape(N_OCT, OCT, r, 1, c) * eye).reshape(N_OCT, OCT * r, OCT * c)

    bblk = jnp.concatenate([block_diag(jnp.swapaxes(bb_re, 1, 2)),
                            block_diag(jnp.swapaxes(bb_im, 1, 2))], axis=2)
    ctblk = jnp.concatenate([block_diag(c_re), block_diag(c_im)], axis=2)

    oct_cols = lambda a: a.reshape(N_OCT, 1, OCT_STATES)
    lcol = jnp.concatenate([oct_cols(lb_re), oct_cols(lb_im),
                            jnp.zeros((N_OCT, SUBLANES - 2, OCT_STATES), F32)], axis=1)

    flat = lambda a: a.reshape(1, N_STATES)
    col = lambda a: a.reshape(N_COL, 1, LANES)
    a_tab = jnp.concatenate([col(a_re), col(a_im), col(s_re), col(s_im),
                             jnp.zeros((N_COL, SUBLANES - 4, LANES), F32)], axis=1)
    lb = jnp.concatenate([flat(lb_re), flat(lb_im)], axis=0)
    return bblk, ctblk, lcol, a_tab, lb


def kernel(x_prompt, x_sample, state_k_win, state_v_win, state_ssm_re, state_ssm_im, norm1_g, w_in, b_gate, attn_sinks, ssm_lam_re, ssm_lam_im, ssm_log_dt, ssm_b_re, ssm_b_im, ssm_c_re, ssm_c_im, ssm_d, w_glu, b_glu, w_branch_attn, w_branch_ssm, w_out, norm2_g, w_ffn_gate, w_ffn_up, w_ffn_down, norm_f_g):
    depth = w_in.shape[0]
    assert depth == 1
    b, t, _ = x_prompt.shape
    nb, s_len, _ = x_sample.shape
    assert s_len == 1 and state_k_win.shape[2] == WINDOW
    l = 0
    assert w_in.shape[2] == IN_WIDTH
    w_in_f = w_in[l]
    g1 = norm1_g[l].reshape(1, D_MODEL)
    g2 = norm2_g[l].reshape(1, D_MODEL)
    gf = norm_f_g.reshape(1, D_MODEL)
    bg = b_gate[l].reshape(1, GATE_WIDTH)
    d = ssm_d[l].reshape(1, SSM_WIDTH)
    wglu = w_glu[l].astype(BF)
    bglu = b_glu[l].reshape(1, SSM_WIDTH)
    wba = w_branch_attn[l].astype(BF)
    wbs = w_branch_ssm[l].astype(BF)
    wo = w_out[l].astype(BF)
    sinks = attn_sinks[l]

    bblk, ctblk, lcol, a_tab, lb = _ssm_tables(
        ssm_lam_re[l], ssm_lam_im[l], ssm_log_dt[l], ssm_b_re[l], ssm_b_im[l], ssm_c_re[l], ssm_c_im[l])

    rc, rs1, rs2 = _rope_tables(np.arange(t))
    yssm_p, hre_p, him_p = _s5_prompt(x_prompt, g1, w_in_f, bblk, ctblk, lcol, a_tab, d)
    x1_p, kwin_p, vwin_p, w_in_b = _mix_prompt(x_prompt, yssm_p, rc, rs1, rs2, sinks, g1, w_in_f, bg, wba, wbs, wo,
                                               wglu, bglu)

    rcs, rs1s, rs2s = _rope_tables(PAST_LEN + np.arange(1))
    sinkc = jnp.concatenate([sinks, jnp.zeros((QROWS - N_Q_HEADS,), F32)]).reshape(QROWS, 1)
    xs = x_sample.reshape(nb, D_MODEL)
    key_minor = lambda a: jnp.swapaxes(a.reshape(nb, WINDOW, KV_WIDTH), 1, 2)
    kwin_s, vwin_s, x1_s, hre_s, him_s = _sample_layer(
        xs, key_minor(state_k_win[l]), key_minor(state_v_win[l]),
        state_ssm_re[l].reshape(nb, N_STATES).T, state_ssm_im[l].reshape(nb, N_STATES).T,
        rcs, rs1s, rs2s, sinkc, g1, w_in_b, bg, wba, wbs, wo, lb, bblk, ctblk,
        d, wglu, bglu)
    y_p, y_s = _ffn(x1_p, x1_s, g2, w_ffn_gate[l], w_ffn_up[l], w_ffn_down[l], gf)
    y_p = y_p.reshape(b, t, D_MODEL)
    y_s = y_s.reshape(nb, 1, D_MODEL)

    kv_shape_p = (1, b, WINDOW, N_KV_HEADS, HEAD_DIM)
    st_shape_p = (1, b, N_SSM_GROUPS, SSM_STATE)
    kv_shape_s = (1, nb, WINDOW, N_KV_HEADS, HEAD_DIM)
    st_shape_s = (1, nb, N_SSM_GROUPS, SSM_STATE)
    return (y_p, y_s,
            jnp.swapaxes(kwin_p, 1, 2).reshape(kv_shape_p), jnp.swapaxes(vwin_p, 1, 2).reshape(kv_shape_p),
            hre_p.reshape(st_shape_p), him_p.reshape(st_shape_p),
            jnp.swapaxes(kwin_s, 1, 2).reshape(kv_shape_s), jnp.swapaxes(vwin_s, 1, 2).reshape(kv_shape_s),
            hre_s.T.reshape(st_shape_s), him_s.T.reshape(st_shape_s))
```

```python
import math

import jax
import jax.numpy as jnp
import numpy as np
from jax import lax
from jax.experimental import pallas as pl
from jax.experimental.pallas import tpu as pltpu

D_MODEL = 1024
N_Q_HEADS = 8
N_KV_HEADS = 2
HEAD_DIM = 64
ATTN_WIDTH = N_Q_HEADS * HEAD_DIM
KV_WIDTH = N_KV_HEADS * HEAD_DIM
WINDOW = 128
ROPE_DIM = HEAD_DIM // 4
ROPE_HALF = ROPE_DIM // 2
ROPE_THETA = 500000.0
SSM_WIDTH = D_MODEL // 2
SSM_GROUP = 16
N_SSM_GROUPS = SSM_WIDTH // SSM_GROUP
SSM_STATE = 64
N_STATES = N_SSM_GROUPS * SSM_STATE
GATE_WIDTH = 2 * D_MODEL
QKV_WIDTH = ATTN_WIDTH + 2 * KV_WIDTH
U_COL0 = QKV_WIDTH
GATE_COL0 = U_COL0 + SSM_WIDTH
IN_WIDTH = GATE_COL0 + GATE_WIDTH
D_FF = -(-8 * D_MODEL // (3 * 256)) * 256
NORM_EPS = 1e-5
PAST_LEN = 8192

LANES = 128
SUBLANES = 8
CHUNK = 4
OCT = LANES // SSM_GROUP
N_OCT = N_SSM_GROUPS // OCT
OCT_STATES = OCT * SSM_STATE
OCT_COL = OCT_STATES // LANES
N_COL = N_STATES // LANES
N_SEG = SUBLANES
SEG = 64
S5_ROWS = N_SEG * SEG
M_PAIR = 2
U_HALF = SSM_WIDTH // 2
SEG_PITCH = SEG * CHUNK + SUBLANES
MIX_TB = 512
FFN_TB = 1024
FFN_SPLIT = 4
NEG_BIG = -1e30
LOG2_E = math.log2(math.e)
VMEM_LIMIT = 56 * 1024 * 1024

BF = jnp.bfloat16
F32 = jnp.float32


def _dot(a, b):
    return jnp.dot(a, b, preferred_element_type=F32)


def _dot_nt(a, b):
    return lax.dot_general(a, b, (((1,), (1,)), ((), ())), preferred_element_type=F32)


def _dot_nt_split(a, b):
    a_hi, b_hi = a.astype(BF), b.astype(BF)
    a_lo = (a - a_hi.astype(F32)).astype(BF)
    b_lo = (b - b_hi.astype(F32)).astype(BF)
    return _dot_nt(a_hi, b_hi) + (_dot_nt(a_hi, b_lo) + _dot_nt(a_lo, b_hi))


def _rms(x, g):
    return x * lax.rsqrt(jnp.mean(x * x, axis=-1, keepdims=True) + NORM_EPS) * g


def _sigmoid(x):
    return 1.0 / (1.0 + jnp.exp(-x))


def _gelu_tanh(x):
    c = math.sqrt(2.0 / math.pi)
    return 0.5 * x * (1.0 + jnp.tanh(c * (x + 0.044715 * (x * x * x))))


def _rope(a, rc, rs1, rs2):
    return a * rc + pltpu.roll(a, ROPE_HALF, 1) * rs1 + pltpu.roll(a, LANES - ROPE_HALF, 1) * rs2


def _const_spec(shape):
    nd = len(shape)
    return pl.BlockSpec(shape, lambda *_: (0,) * nd, pipeline_mode=pl.Buffered(1))


def _build_chunk_operators(bblk_ref, ctblk_ref, lcol_ref, m_s, e_s, f_s):
    for o in range(N_OCT):
        ct = jnp.concatenate([ctblk_ref[o, :, :OCT_STATES], -ctblk_ref[o, :, OCT_STATES:]],
                             axis=1)
        lr, li = lcol_ref[o, 0:1, :], lcol_ref[o, 1:2, :]
        er, ei = bblk_ref[o, :, :OCT_STATES], bblk_ref[o, :, OCT_STATES:]
        k_blk = []
        for tau in range(CHUNK):
            e_cat = jnp.concatenate([er, ei], axis=1)
            i = CHUNK - 1 - tau
            e_s[o, i * LANES:(i + 1) * LANES, :] = e_cat.astype(BF)
            k_blk.append(_dot_nt_split(e_cat, ct).astype(BF))
            er, ei = er * lr - ei * li, er * li + ei * lr
        zero = jnp.zeros((LANES, LANES), BF)
        for j in range(CHUNK):
            jt, jj = divmod(j, M_PAIR)
            for i in range(M_PAIR * (jt + 1)):
                m_s[jt][o, i * LANES:(i + 1) * LANES, jj * LANES:(jj + 1) * LANES] = (
                    k_blk[j - i] if j >= i else zero)
        tr, ti = ct[:, :OCT_STATES], -ct[:, OCT_STATES:]
        for j in range(CHUNK):
            tr, ti = tr * lr - ti * li, tr * li + ti * lr
            f_s[o, :OCT_STATES, j * LANES:(j + 1) * LANES] = tr.T.astype(BF)
            f_s[o, OCT_STATES:, j * LANES:(j + 1) * LANES] = (-ti).T.astype(BF)


def _s5_prompt_kernel(x_ref, g1_ref, wu0_ref, wu1_ref, bblk_ref, ctblk_ref, lcol_ref, a_ref, d_ref,
                      out_ref, hfin_ref, *scratch):
    m_ref = scratch[:CHUNK // M_PAIR]
    e_ref, f_ref, us, ys, sre, sim, car = scratch[CHUNK // M_PAIR:]
    blk = pl.program_id(1)

    @pl.when((pl.program_id(0) == 0) & (blk == 0))
    def _():
        _build_chunk_operators(bblk_ref, ctblk_ref, lcol_ref, m_ref, e_ref, f_ref)

    @pl.when(blk == 0)
    def _():
        car[...] = jnp.zeros_like(car)

    seg_tokens = SEG * CHUNK
    seg_rows = [slice(s * SEG_PITCH, s * SEG_PITCH + seg_tokens) for s in range(N_SEG)]
    tok_rows = [slice(s * seg_tokens, (s + 1) * seg_tokens) for s in range(N_SEG)]

    for s in range(0, N_SEG, 2):
        hn = _rms(x_ref[s * seg_tokens:(s + 2) * seg_tokens, :], g1_ref[...]).astype(BF)
        u = jnp.concatenate([_dot(hn, wu0_ref[...].astype(BF)), _dot(hn, wu1_ref[...].astype(BF))],
                            axis=1)
        for half in range(2):
            for cc in range(N_OCT):
                us[cc, seg_rows[s + half], :] = u[tok_rows[half], cc * LANES:(cc + 1) * LANES]

    sub = lax.broadcasted_iota(jnp.int32, (N_SEG, LANES), 0)

    def scan_column(c):
        tab = a_ref[c]
        are, aim = tab[0:1], tab[1:2]
        bre, bim = tab[2:3], tab[3:4]

        def step(cr, ci, r, keep_entering):
            slab = slice(r * N_SEG, (r + 1) * N_SEG)
            s_r, s_i = sre[c, slab, :], sim[c, slab, :]
            if keep_entering:
                sre[c, slab, :] = cr
                sim[c, slab, :] = ci
            return are * cr - aim * ci + s_r, are * ci + aim * cr + s_i

        cr = jnp.zeros((N_SEG, LANES), F32)
        ci = jnp.zeros((N_SEG, LANES), F32)
        for r in range(SEG):
            cr, ci = step(cr, ci, r, False)
        cv = car[c]
        pr, pi = cv[0:1], cv[1:2]
        sr = jnp.zeros((N_SEG, LANES), F32)
        si = jnp.zeros((N_SEG, LANES), F32)
        for s in range(N_SEG):
            sr = jnp.where(sub == s, pr, sr)
            si = jnp.where(sub == s, pi, si)
            pr, pi = (bre * pr - bim * pi + cr[s:s + 1], bre * pi + bim * pr + ci[s:s + 1])
        end = jnp.where(sub == 0, pr, jnp.where(sub == 1, pi, 0.0))
        car[c] = end
        hfin_ref[c] = end
        cr, ci = sr, si
        for r in range(SEG):
            cr, ci = step(cr, ci, r, True)

    for o in range(N_OCT):
        uo = jnp.concatenate(
            [jnp.concatenate([us[o, pl.ds(r * CHUNK + i, N_SEG, stride=SEG_PITCH), :] for r in range(SEG)],
                             axis=0).astype(BF) for i in range(CHUNK)], axis=1)
        s_end = _dot(uo, e_ref[o])
        cols = range(o * OCT_COL, (o + 1) * OCT_COL)
        for cc, c in enumerate(cols):
            sre[c] = s_end[:, cc * LANES:(cc + 1) * LANES]
            sim[c] = s_end[:, OCT_STATES + cc * LANES:OCT_STATES + (cc + 1) * LANES]
        y_in = jnp.concatenate(
            [_dot(uo[:, :(jt + 1) * M_PAIR * LANES], m_ref[jt][o]) for jt in range(CHUNK // M_PAIR)],
            axis=1)
        for c in cols:
            scan_column(c)
        hp = jnp.concatenate([sre[c] for c in cols] + [sim[c] for c in cols], axis=1).astype(BF)
        yo = y_in + _dot(hp, f_ref[o])
        for r in range(SEG):
            for j in range(CHUNK):
                ys[o, pl.ds(r * CHUNK + j, N_SEG, stride=SEG_PITCH), :] = (
                    yo[r * N_SEG:(r + 1) * N_SEG, j * LANES:(j + 1) * LANES])
        d_o = d_ref[:, o * LANES:(o + 1) * LANES]
        for s in range(N_SEG):
            out_ref[tok_rows[s], o * LANES:(o + 1) * LANES] = (
                ys[o, seg_rows[s], :] + d_o * us[o, seg_rows[s], :])


def _s5_prompt(x, g1, w_in, bblk, ctblk, lcol, a_tab, d):
    b, t, _ = x.shape
    tb = S5_ROWS * CHUNK
    nblk = t // tb
    x2 = x.reshape(b * t, D_MODEL)
    row_map = lambda i, j: (i * nblk + j, 0)
    e_shape = (N_OCT, CHUNK * LANES, 2 * OCT_STATES)
    f_shape = (N_OCT, 2 * OCT_STATES, CHUNK * LANES)
    out, hfin = pl.pallas_call(
        _s5_prompt_kernel,
        grid=(b, nblk),
        in_specs=[
            pl.BlockSpec((tb, D_MODEL), row_map),
            _const_spec((1, D_MODEL)),
            *[pl.BlockSpec((D_MODEL, U_HALF), lambda i, j, c=U_COL0 // U_HALF + h: (0, c),
                           pipeline_mode=pl.Buffered(1)) for h in range(2)],
            _const_spec(bblk.shape),
            _const_spec(ctblk.shape),
            _const_spec(lcol.shape),
            _const_spec(a_tab.shape),
            _const_spec((1, SSM_WIDTH)),
        ],
        out_specs=[
            pl.BlockSpec((tb, SSM_WIDTH), row_map),
            pl.BlockSpec((None, N_COL, SUBLANES, LANES), lambda i, j: (i, 0, 0, 0)),
        ],
        out_shape=[
            jax.ShapeDtypeStruct((b * t, SSM_WIDTH), F32),
            jax.ShapeDtypeStruct((b, N_COL, SUBLANES, LANES), F32),
        ],
        scratch_shapes=[
            *[pltpu.VMEM((N_OCT, (jt + 1) * M_PAIR * LANES, M_PAIR * LANES), BF)
              for jt in range(CHUNK // M_PAIR)],
            pltpu.VMEM(e_shape, BF), pltpu.VMEM(f_shape, BF),
            pltpu.VMEM((N_OCT, N_SEG * SEG_PITCH, LANES), F32),
            pltpu.VMEM((N_OCT, N_SEG * SEG_PITCH, LANES), F32),
            pltpu.VMEM((N_COL, S5_ROWS, LANES), F32), pltpu.VMEM((N_COL, S5_ROWS, LANES), F32),
            pltpu.VMEM((N_COL, SUBLANES, LANES), F32),
        ],
        compiler_params=pltpu.CompilerParams(
            dimension_semantics=("arbitrary", "arbitrary"), vmem_limit_bytes=VMEM_LIMIT),
        name="s5_prompt",
    )(x2, g1, w_in, w_in, bblk, ctblk, lcol, a_tab, d)
    return out, hfin[:, :, 0, :], hfin[:, :, 1, :]


def _gate_cols(hn, win_ref, bg_ref, cols):
    w_cols = slice(GATE_COL0 + cols.start, GATE_COL0 + cols.stop)
    return _sigmoid(_dot(hn, win_ref[:, w_cols]) + bg_ref[:, cols])


def _merge_out(x, gates, attn_proj, ssm_proj, wo_ref):
    merged = gates[:, :D_MODEL] * attn_proj + gates[:, D_MODEL:] * ssm_proj
    return x + _dot(merged.astype(BF), wo_ref[...])


def _half_split(a, ar, lo):
    z = jnp.zeros_like(a)
    return (jnp.where(lo, a, z).astype(BF), jnp.where(lo, z, ar).astype(BF),
            jnp.where(lo, ar, z).astype(BF), jnp.where(lo, z, a).astype(BF))


def _softmax_terms_t(st, bias_t, sink):
    st = st + bias_t
    m = jnp.maximum(jnp.max(st, axis=0, keepdims=True), sink)
    return jnp.exp2(st - m).astype(BF), jnp.exp2(sink - m)


def _mix_prompt_kernel(sinks_ref, x_ref, y_ref, rc_ref, rs1_ref, rs2_ref, g1_ref, win_hbm,
                       bg_ref, wba_ref, wbs_ref, wo_ref, wglu_ref, bglu_ref,
                       x1_ref, kwin_ref, vwin_ref, winb_hbm, kprev, vprev, win_ref, stage, sem_stage, sem_out):
    t = pl.program_id(1)
    seq = pl.program_id(0)
    share_copy = pltpu.make_async_copy(win_ref, winb_hbm, sem_out.at[0])

    @pl.when((seq == 0) & (t == 0))
    def _():
        _stream_cast_weights([(win_hbm, win_ref, stage, sem_stage)])
        share_copy.start()

    @pl.when((seq == pl.num_programs(0) - 1) & (t == pl.num_programs(1) - 1))
    def _():
        share_copy.wait()

    @pl.when(t == 0)
    def _():
        kprev[...] = jnp.zeros_like(kprev)
        vprev[...] = jnp.zeros_like(vprev)

    x = x_ref[...]
    hn = _rms(x, g1_ref[...]).astype(BF)
    qkv = _dot(hn, win_ref[:, :QKV_WIDTH])
    z = _gelu_tanh(y_ref[...])
    glu = _dot(z.astype(BF), wglu_ref[...]) + bglu_ref[...]
    rc, rs1, rs2 = rc_ref[...], rs1_ref[...], rs2_ref[...]
    scale = HEAD_DIM ** -0.5 * LOG2_E
    q = [(_rope(qkv[:, c * LANES:(c + 1) * LANES], rc, rs1, rs2) * scale).astype(BF)
         for c in range(ATTN_WIDTH // LANES)]
    k = _rope(qkv[:, ATTN_WIDTH:ATTN_WIDTH + KV_WIDTH], rc, rs1, rs2)
    v = qkv[:, ATTN_WIDTH + KV_WIDTH:ATTN_WIDTH + 2 * KV_WIDTH]
    ssm = (z * _sigmoid(glu)).astype(BF)
    ssm_proj = _dot(ssm, wbs_ref[...])

    w = WINDOW
    n_sub = MIX_TB // w
    lane = lax.broadcasted_iota(jnp.int32, (2 * w, LANES), 1)
    lo = lane < HEAD_DIM
    kj = lax.broadcasted_iota(jnp.int32, (2 * w, w), 0)
    qi = lax.broadcasted_iota(jnp.int32, (2 * w, w), 1)
    band = (kj > qi) & (kj <= qi + w)
    first = band & ((kj >= w) | (t > 0))
    bias_band = jnp.where(band, 0.0, NEG_BIG).astype(F32)
    bias_first = jnp.where(first, 0.0, NEG_BIG).astype(F32)
    col = lax.broadcasted_iota(jnp.int32, (1, 2 * w), 1)
    sum_r = lax.broadcasted_iota(jnp.int32, (2 * SUBLANES, 4 * w), 0)
    sum_c = lax.broadcasted_iota(jnp.int32, (2 * SUBLANES, 4 * w), 1)
    sum_rows = (((sum_r == 0) & (sum_c < 2 * w)) | ((sum_r == 1) & (sum_c >= 2 * w))).astype(F32)

    def sink_row(ha, hb):
        return jnp.where(col < w, sinks_ref[ha], sinks_ref[hb]) * LOG2_E

    gate_w = GATE_WIDTH // (n_sub * N_KV_HEADS)
    operands = {}

    def sub_block_operands(sb):
        if sb not in operands:
            cur = slice(sb * w, (sb + 1) * w)
            if sb == 0:
                kcat = jnp.concatenate([kprev[...], k[cur]], axis=0)
                vcat = jnp.concatenate([vprev[...], v[cur]], axis=0)
                bias = bias_first
            else:
                kcat = k[(sb - 1) * w:(sb + 1) * w]
                vcat = v[(sb - 1) * w:(sb + 1) * w]
                bias = bias_band
            k_split = _half_split(kcat, pltpu.roll(kcat, HEAD_DIM, 1), lo)
            k_stack = [jnp.concatenate(k_split[2 * g:2 * g + 2], axis=0) for g in range(N_KV_HEADS)]
            vt = vcat.T
            zero = jnp.zeros((HEAD_DIM, 2 * w), F32)
            v_stack = []
            for g in range(N_KV_HEADS):
                vg = vt[g * HEAD_DIM:(g + 1) * HEAD_DIM]
                v_stack.append(jnp.concatenate(
                    [jnp.concatenate([vg, zero], axis=1), jnp.concatenate([zero, vg], axis=1), sum_rows],
                    axis=0).astype(BF))
            operands[sb] = (k_stack, v_stack, jnp.concatenate([bias, bias], axis=1))
        return operands[sb]

    def scores(sb, grp):
        k_stack, _, _ = sub_block_operands(sb)
        cur = slice(sb * w, (sb + 1) * w)
        qq = jnp.concatenate([q[2 * grp][cur], q[2 * grp + 1][cur]], axis=0)
        return _dot_nt(k_stack[grp], qq)

    def weighted_values(sb, grp, s):
        _, v_stack, bias2 = sub_block_operands(sb)
        p_e, sink_e = _softmax_terms_t(s[:2 * w], bias2, sink_row(4 * grp, 4 * grp + 2))
        p_o, sink_o = _softmax_terms_t(s[2 * w:], bias2, sink_row(4 * grp + 1, 4 * grp + 3))
        d = _dot(v_stack[grp], jnp.concatenate([p_e, p_o], axis=0))
        heads = 2 * HEAD_DIM
        o2 = jnp.concatenate(
            [d[:HEAD_DIM] * (1.0 / (d[heads:heads + 1] + sink_e)),
             d[HEAD_DIM:heads] * (1.0 / (d[heads + 1:heads + 2] + sink_o))], axis=0)
        return [o2[:, :w].T, o2[:, w:].T]

    chains = [(sb, grp) for sb in range(n_sub) for grp in range(N_KV_HEADS)]
    gate_cols, outs = [], []
    s_next = scores(*chains[0])
    for i, (sb, grp) in enumerate(chains):
        s_cur = s_next
        gate_cols.append(_gate_cols(hn, win_ref, bg_ref, slice(i * gate_w, (i + 1) * gate_w)))
        if i + 1 < len(chains):
            s_next = scores(*chains[i + 1])
        outs += weighted_values(sb, grp, s_cur)
    per_sb = 2 * N_KV_HEADS
    attn = jnp.concatenate(
        [jnp.concatenate(outs[sb * per_sb:(sb + 1) * per_sb], axis=1) for sb in range(n_sub)],
        axis=0).astype(BF)
    gates = jnp.concatenate(gate_cols, axis=1)

    kprev[...] = k[MIX_TB - w:]
    vprev[...] = v[MIX_TB - w:]
    kwin_ref[...] = k[MIX_TB - w:].T
    vwin_ref[...] = v[MIX_TB - w:].T
    x1_ref[...] = _merge_out(x, gates, _dot(attn, wba_ref[...]), ssm_proj, wo_ref)


def _mix_prompt(x, y_ssm, rc, rs1, rs2, sinks, g1, w_in, bg, wba, wbs, wo, wglu, bglu):
    b, t, _ = x.shape
    nblk = t // MIX_TB
    x2 = x.reshape(b * t, D_MODEL)
    row_map = lambda i, j: (i * nblk + j, 0)
    x1, kwin, vwin, w_in_bf = pl.pallas_call(
        _mix_prompt_kernel,
        grid=(b, nblk),
        in_specs=[
            pl.BlockSpec(memory_space=pltpu.SMEM),
            pl.BlockSpec((MIX_TB, D_MODEL), row_map),
            pl.BlockSpec((MIX_TB, SSM_WIDTH), row_map),
            pl.BlockSpec((MIX_TB, LANES), lambda i, j: (j, 0)),
            pl.BlockSpec((MIX_TB, LANES), lambda i, j: (j, 0)),
            pl.BlockSpec((MIX_TB, LANES), lambda i, j: (j, 0)),
            _const_spec((1, D_MODEL)),
            pl.BlockSpec(memory_space=pl.ANY),
            _const_spec(bg.shape),
            _const_spec(wba.shape),
            _const_spec(wbs.shape),
            _const_spec(wo.shape),
            _const_spec(wglu.shape),
            _const_spec(bglu.shape),
        ],
        out_specs=[
            pl.BlockSpec((MIX_TB, D_MODEL), row_map),
            pl.BlockSpec((None, KV_WIDTH, WINDOW), lambda i, j: (i, 0, 0)),
            pl.BlockSpec((None, KV_WIDTH, WINDOW), lambda i, j: (i, 0, 0)),
            pl.BlockSpec(memory_space=pl.ANY),
        ],
        out_shape=[
            jax.ShapeDtypeStruct((b * t, D_MODEL), F32),
            jax.ShapeDtypeStruct((b, KV_WIDTH, WINDOW), F32),
            jax.ShapeDtypeStruct((b, KV_WIDTH, WINDOW), F32),
            jax.ShapeDtypeStruct(w_in.shape, BF),
        ],
        scratch_shapes=[pltpu.VMEM((WINDOW, KV_WIDTH), F32), pltpu.VMEM((WINDOW, KV_WIDTH), F32),
                        pltpu.VMEM(w_in.shape, BF),
                        pltpu.VMEM((FFN_W_SLOTS, D_MODEL // FFN_W_CHUNKS, IN_WIDTH), F32),
                        pltpu.SemaphoreType.DMA((FFN_W_SLOTS,)), pltpu.SemaphoreType.DMA((1,))],
        compiler_params=pltpu.CompilerParams(
            dimension_semantics=("arbitrary", "arbitrary"), vmem_limit_bytes=VMEM_LIMIT),
        name="mix_prompt",
    )(sinks, x2, y_ssm, rc, rs1, rs2, g1, w_in, bg, wba, wbs, wo, wglu, bglu)
    return x1, kwin, vwin, w_in_bf


QROWS = 16
SAMPLE_TB = 32


def _sample_layer_kernel(x_ref, kbuf_ref, vbuf_ref, rc_ref, rs1_ref, rs2_ref, sinkc_ref, g1_ref, wqkv_ref,
                         xall_ref, h0re_ref, h0im_ref, bg_ref, lb_ref, d_ref, wglu_ref, bglu_ref,
                         win_hbm, wba_hbm, wbs_hbm, wo_hbm, bblk_hbm, ctblk_hbm,
                         kout_ref, vout_ref, x1_ref, hre_ref, him_ref,
                         qz, o3_all, win_s, wba_s, wbs_s, wo_s, bblk_s, ctblk_s, sem):
    nb = x_ref.shape[0]
    step = pl.program_id(0)
    last_step = pl.num_programs(0) - 1
    late = ((win_hbm, win_s), (wba_hbm, wba_s), (wbs_hbm, wbs_s), (wo_hbm, wo_s), (bblk_hbm, bblk_s),
            (ctblk_hbm, ctblk_s))
    late_copies = [pltpu.make_async_copy(src, dst, sem.at[i]) for i, (src, dst) in enumerate(late)]

    @pl.when(step == 0)
    def _():
        for copy in late_copies:
            copy.start()

    hn = _rms(x_ref[...], g1_ref[...]).astype(BF)
    qkv = _dot(hn, wqkv_ref[...])
    rc, rs1, rs2 = rc_ref[...], rs1_ref[...], rs2_ref[...]
    scale = HEAD_DIM ** -0.5
    k_new = _rope(qkv[:, ATTN_WIDTH:ATTN_WIDTH + KV_WIDTH], rc, rs1, rs2)
    v_new = qkv[:, ATTN_WIDTH + KV_WIDTH:ATTN_WIDTH + 2 * KV_WIDTH]
    pad = jnp.zeros((LANES - nb, KV_WIDTH), F32)
    k_new_t = jnp.concatenate([k_new, pad], axis=0).T
    v_new_t = jnp.concatenate([v_new, pad], axis=0).T

    lane = lax.broadcasted_iota(jnp.int32, (nb, LANES), 1)
    lo = lane < HEAD_DIM
    qz[...] = jnp.zeros_like(qz)
    for c in range(ATTN_WIDTH // LANES):
        qc = _rope(qkv[:, c * LANES:(c + 1) * LANES], rc, rs1, rs2) * scale
        qr = pltpu.roll(qc, HEAD_DIM, 1)
        zero = jnp.zeros_like(qc)
        if c < 2:
            even, odd = jnp.where(lo, qc, zero), jnp.where(lo, qr, zero)
        else:
            even, odd = jnp.where(lo, zero, qr), jnp.where(lo, zero, qc)
        qz[pl.ds(2 * c, nb, stride=QROWS), :] = even
        qz[pl.ds(2 * c + 1, nb, stride=QROWS), :] = odd

    last = lax.broadcasted_iota(jnp.int32, (KV_WIDTH, WINDOW), 1) == WINDOW - 1
    for b in range(nb):
        kout_ref[b] = jnp.where(last, k_new_t[:, b:b + 1], pltpu.roll(kbuf_ref[b], WINDOW - 1, 1))
        vout_ref[b] = jnp.where(last, v_new_t[:, b:b + 1], pltpu.roll(vbuf_ref[b], WINDOW - 1, 1))

    sink = sinkc_ref[...]
    q3 = qz[...].reshape(nb, QROWS, LANES).astype(BF)
    s = jnp.einsum('bhd,bdk->bhk', q3, kout_ref[...].astype(BF), preferred_element_type=F32)
    m = jnp.maximum(jnp.max(s, axis=-1, keepdims=True), sink)
    p = jnp.exp(s - m)
    den = jnp.sum(p, axis=-1, keepdims=True) + jnp.exp(sink - m)
    p = (p * (1.0 / den)).astype(BF)
    o3 = jnp.einsum('bhk,bdk->bhd', p, vout_ref[...].astype(BF), preferred_element_type=F32)
    o3_all[pl.ds(pl.multiple_of(step * (nb * QROWS), nb * QROWS), nb * QROWS), :] = o3.reshape(nb * QROWS, LANES)

    @pl.when(step == last_step)
    def _():
        for copy in late_copies:
            copy.wait()
        _sample_tail(xall_ref, o3_all, h0re_ref, h0im_ref, g1_ref, win_s, bg_ref, wba_s, wbs_s, wo_s,
                     lb_ref, bblk_s, ctblk_s, d_ref, wglu_ref, bglu_ref, x1_ref, hre_ref, him_ref)


def _sample_layer(x, kbuf, vbuf, h0re, h0im, rc, rs1, rs2, sinkc, g1, w_in, bg, wba, wbs, wo, lb, bblk, ctblk,
                  d, wglu, bglu):
    nb = x.shape[0]
    tb = SAMPLE_TB
    blocked = (x, kbuf, vbuf)
    early = (rc, rs1, rs2, sinkc, g1)
    consts = (x, h0re, h0im, bg, lb, d, wglu, bglu)
    late = (w_in, wba, wbs, wo, bblk, ctblk)
    kv_spec = pl.BlockSpec((tb, KV_WIDTH, WINDOW), lambda i: (i, 0, 0))
    whole = lambda shape: pl.BlockSpec(shape, lambda i: (0,) * len(shape))
    return pl.pallas_call(
        _sample_layer_kernel,
        grid=(nb // tb,),
        in_specs=[pl.BlockSpec((tb, D_MODEL), lambda i: (i, 0)), kv_spec, kv_spec]
        + [_const_spec(a.shape) for a in early]
        + [pl.BlockSpec((D_MODEL, QKV_WIDTH), lambda i: (0, 0), pipeline_mode=pl.Buffered(1))]
        + [_const_spec(a.shape) for a in consts]
        + [pl.BlockSpec(memory_space=pl.ANY) for _ in late],
        out_specs=[kv_spec, kv_spec, whole((nb, D_MODEL)), whole((N_STATES, nb)), whole((N_STATES, nb))],
        out_shape=[
            jax.ShapeDtypeStruct((nb, KV_WIDTH, WINDOW), F32),
            jax.ShapeDtypeStruct((nb, KV_WIDTH, WINDOW), F32),
            jax.ShapeDtypeStruct((nb, D_MODEL), F32),
            jax.ShapeDtypeStruct((N_STATES, nb), F32),
            jax.ShapeDtypeStruct((N_STATES, nb), F32),
        ],
        scratch_shapes=[pltpu.VMEM((tb * QROWS, LANES), F32), pltpu.VMEM((nb * QROWS, LANES), F32)]
        + [pltpu.VMEM(a.shape, a.dtype) for a in late] + [pltpu.SemaphoreType.DMA((len(late),))],
        compiler_params=pltpu.CompilerParams(
            dimension_semantics=("arbitrary",), vmem_limit_bytes=VMEM_LIMIT),
        name="sample_layer",
    )(*blocked, *early, w_in, *consts, *late)


def _sample_tail(x_ref, o3_ref, h0re_ref, h0im_ref, g1_ref, win_ref, bg_ref, wba_ref,
                 wbs_ref, wo_ref, lb_ref, bblk_ref, ctblk_ref, d_ref, wglu_ref, bglu_ref,
                 x1_ref, hre_ref, him_ref):
    nb = x_ref.shape[0]
    x = x_ref[...]
    hn = _rms(x, g1_ref[...]).astype(BF)

    lane = lax.broadcasted_iota(jnp.int32, (nb, LANES), 1)
    lo = lane < HEAD_DIM
    a = jnp.zeros((nb, D_MODEL), F32)
    zero = jnp.zeros((nb, LANES), F32)
    for h in range(N_Q_HEADS):
        oh = o3_ref[pl.ds(h, nb, stride=QROWS), :]
        oh = jnp.where(lo, oh, zero) if h < N_Q_HEADS // 2 else jnp.where(lo, zero, oh)
        w_h = wba_ref[h * HEAD_DIM:(h + 1) * HEAD_DIM, :]
        a = a + _dot(oh.astype(BF), jnp.concatenate([w_h, w_h], axis=0))

    u = _dot(hn, win_ref[:, U_COL0:GATE_COL0])
    ub = u.astype(BF)
    lre, lim = lb_ref[0:1, :], lb_ref[1:2, :]
    y_cols = []
    for o in range(N_OCT):
        sl = slice(o * OCT_STATES, (o + 1) * OCT_STATES)
        bu = _dot(ub[:, o * LANES:(o + 1) * LANES], bblk_ref[o].astype(BF))
        blocks = [slice(c * LANES, (c + 1) * LANES) for c in range(o * OCT_COL, (o + 1) * OCT_COL)]
        h0r = jnp.concatenate([h0re_ref[rows, :].T for rows in blocks], axis=1)
        h0i = jnp.concatenate([h0im_ref[rows, :].T for rows in blocks], axis=1)
        hr = bu[:, :OCT_STATES] + (lre[:, sl] * h0r - lim[:, sl] * h0i)
        hi = bu[:, OCT_STATES:] + (lre[:, sl] * h0i + lim[:, sl] * h0r)
        for cc, rows in enumerate(blocks):
            hre_ref[rows, :] = hr[:, cc * LANES:(cc + 1) * LANES].T
            him_ref[rows, :] = hi[:, cc * LANES:(cc + 1) * LANES].T
        y_cols.append(_dot_nt(jnp.concatenate([hr, -hi], axis=1).astype(BF), ctblk_ref[o].astype(BF)))
    y = jnp.concatenate(y_cols, axis=1) + d_ref[...] * u
    z = _gelu_tanh(y)
    gate = _dot(z.astype(BF), wglu_ref[...]) + bglu_ref[...]
    ssm = (z * _sigmoid(gate)).astype(BF)

    gates = _gate_cols(hn, win_ref, bg_ref, slice(0, GATE_WIDTH))
    x1_ref[...] = _merge_out(x, gates, a, _dot(ssm, wbs_ref[...]), wo_ref)


FFN_W_CHUNKS = 16
FFN_W_SLOTS = 4


def _stream_cast_weights(jobs):
    tasks, used = [], {}
    for w_hbm, w_bf, stage, sem in jobs:
        rows = stage.shape[1]
        for k in range(w_hbm.shape[0] // rows):
            slot = used.get(id(stage), 0) % stage.shape[0]
            used[id(stage)] = used.get(id(stage), 0) + 1
            copy = pltpu.make_async_copy(w_hbm.at[pl.ds(k * rows, rows), :], stage.at[slot], sem.at[slot])
            tasks.append((copy, w_bf, stage, slot, k * rows, rows))
    ahead = min(stage.shape[0] for _, _, stage, _ in jobs) - 1
    for copy, *_ in tasks[:ahead]:
        copy.start()
    for i, (copy, w_bf, stage, slot, row0, rows) in enumerate(tasks):
        if i + ahead < len(tasks):
            tasks[i + ahead][0].start()
        copy.wait()
        w_bf[row0:row0 + rows, :] = stage[slot].astype(BF)


def _ffn_rows(x, g2_ref, wgate, wup, wdown, gf_ref):
    h = _rms(x, g2_ref[...]).astype(BF)
    gate = _dot(h, wgate[...])
    up = _dot(h, wup[...])
    half_gate = 0.5 * gate
    act = ((half_gate + half_gate * jnp.tanh(half_gate)) * up).astype(BF)
    x2 = x + _dot(act, wdown[...])
    return _rms(x2, gf_ref[...])


def _ffn_kernel(xp_ref, xs_ref, g2_ref, wgate_hbm, wup_hbm, wdown_hbm, gf_ref, yp_ref, ys_ref,
                wgate, wup, wdown, stage_in, stage_out, sem_in, sem_out):
    i = pl.program_id(0)
    n_prompt = pl.num_programs(0) - 1

    @pl.when(i == 0)
    def _():
        _stream_cast_weights([(wgate_hbm, wgate, stage_in, sem_in), (wup_hbm, wup, stage_in, sem_in),
                              (wdown_hbm, wdown, stage_out, sem_out)])

    @pl.when(i < n_prompt)
    def _():
        half = FFN_TB // FFN_SPLIT
        for rows in (slice(h * half, (h + 1) * half) for h in range(FFN_SPLIT)):
            yp_ref[rows, :] = _ffn_rows(xp_ref[rows, :], g2_ref, wgate, wup, wdown, gf_ref)

    @pl.when(i == n_prompt)
    def _():
        ys_ref[...] = _ffn_rows(xs_ref[...], g2_ref, wgate, wup, wdown, gf_ref)


def _ffn(xp, xs, g2, wgate, wup, wdown, gf):
    n, ns = xp.shape[0], xs.shape[0]
    n_prompt = n // FFN_TB
    prompt_map = lambda i: (jnp.minimum(i, n_prompt - 1), 0)
    return pl.pallas_call(
        _ffn_kernel,
        grid=(n_prompt + 1,),
        in_specs=[
            pl.BlockSpec((FFN_TB, D_MODEL), prompt_map),
            _const_spec((ns, D_MODEL)),
            _const_spec((1, D_MODEL)),
            pl.BlockSpec(memory_space=pl.ANY),
            pl.BlockSpec(memory_space=pl.ANY),
            pl.BlockSpec(memory_space=pl.ANY),
            _const_spec((1, D_MODEL)),
        ],
        out_specs=[
            pl.BlockSpec((FFN_TB, D_MODEL), prompt_map),
            pl.BlockSpec((ns, D_MODEL), lambda i: (0, 0)),
        ],
        out_shape=[
            jax.ShapeDtypeStruct((n, D_MODEL), F32),
            jax.ShapeDtypeStruct((ns, D_MODEL), F32),
        ],
        scratch_shapes=[
            pltpu.VMEM((D_MODEL, D_FF), BF), pltpu.VMEM((D_MODEL, D_FF), BF), pltpu.VMEM((D_FF, D_MODEL), BF),
            pltpu.VMEM((FFN_W_SLOTS, D_MODEL // FFN_W_CHUNKS, D_FF), F32),
            pltpu.VMEM((FFN_W_SLOTS, D_FF // FFN_W_CHUNKS, D_MODEL), F32),
            pltpu.SemaphoreType.DMA((FFN_W_SLOTS,)), pltpu.SemaphoreType.DMA((FFN_W_SLOTS,)),
        ],
        compiler_params=pltpu.CompilerParams(
            dimension_semantics=("arbitrary",), vmem_limit_bytes=VMEM_LIMIT),
        name="ffn",
    )(xp, xs, g2, wgate, wup, wdown, gf)


def _rope_tables(pos):
    pos = np.asarray(pos, np.float64)
    inv_freq = ROPE_THETA ** (-(np.arange(ROPE_HALF, dtype=np.float64) * 2.0 / ROPE_DIM))
    ang = pos[:, None] * inv_freq[None, :]
    cos, sin = np.cos(ang), np.sin(ang)
    pad = np.zeros((pos.shape[0], HEAD_DIM - ROPE_DIM))
    zero = np.zeros_like(sin)
    rc = np.concatenate([cos, cos, pad + 1.0], axis=1)
    rs1 = np.concatenate([zero, sin, pad], axis=1)
    rs2 = np.concatenate([-sin, zero, pad], axis=1)
    rep = LANES // HEAD_DIM
    return tuple(jnp.asarray(np.tile(a, (1, rep)), F32) for a in (rc, rs1, rs2))


def _cmul(ar, ai, br, bi):
    return ar * br - ai * bi, ar * bi + ai * br


def _ssm_tables(lam_re, lam_im, log_dt, b_re, b_im, c_re, c_im):
    dt = jnp.exp(log_dt)[:, None]
    mag = jnp.exp(lam_re * dt)
    lb_re = mag * jnp.cos(lam_im * dt)
    lb_im = mag * jnp.sin(lam_im * dt)
    den = lam_re * lam_re + lam_im * lam_im
    nr = lb_re - 1.0
    k_re = ((nr * lam_re + lb_im * lam_im) / den)[..., None]
    k_im = ((lb_im * lam_re - nr * lam_im) / den)[..., None]
    bb_re = k_re * b_re - k_im * b_im
    bb_im = k_re * b_im + k_im * b_re

    a_re, a_im = lb_re, lb_im
    for _ in range(int(math.log2(CHUNK))):
        a_re, a_im = _cmul(a_re, a_im, a_re, a_im)
    s_re, s_im = a_re, a_im
    for _ in range(int(math.log2(SEG))):
        s_re, s_im = _cmul(s_re, s_im, s_re, s_im)

    eye = jnp.eye(OCT, dtype=F32).reshape(1, OCT, 1, OCT, 1)

    def block_diag(a):
        r, c = a.shape[1:]
        return (a.reshape(N_OCT, OCT, r, 1, c) * eye).reshape(N_OCT, OCT * r, OCT * c)

    bblk = jnp.concatenate([block_diag(jnp.swapaxes(bb_re, 1, 2)),
                            block_diag(jnp.swapaxes(bb_im, 1, 2))], axis=2)
    ctblk = jnp.concatenate([block_diag(c_re), block_diag(c_im)], axis=2)

    oct_cols = lambda a: a.reshape(N_OCT, 1, OCT_STATES)
    lcol = jnp.concatenate([oct_cols(lb_re), oct_cols(lb_im),
                            jnp.zeros((N_OCT, SUBLANES - 2, OCT_STATES), F32)], axis=1)

    flat = lambda a: a.reshape(1, N_STATES)
    col = lambda a: a.reshape(N_COL, 1, LANES)
    a_tab = jnp.concatenate([col(a_re), col(a_im), col(s_re), col(s_im),
                             jnp.zeros((N_COL, SUBLANES - 4, LANES), F32)], axis=1)
    lb = jnp.concatenate([flat(lb_re), flat(lb_im)], axis=0)
    return bblk, ctblk, lcol, a_tab, lb


def kernel(x_prompt, x_sample, state_k_win, state_v_win, state_ssm_re, state_ssm_im, norm1_g, w_in, b_gate, attn_sinks, ssm_lam_re, ssm_lam_im, ssm_log_dt, ssm_b_re, ssm_b_im, ssm_c_re, ssm_c_im, ssm_d, w_glu, b_glu, w_branch_attn, w_branch_ssm, w_out, norm2_g, w_ffn_gate, w_ffn_up, w_ffn_down, norm_f_g):
    depth = w_in.shape[0]
    assert depth == 1
    b, t, _ = x_prompt.shape
    nb, s_len, _ = x_sample.shape
    assert s_len == 1 and state_k_win.shape[2] == WINDOW
    l = 0
    assert w_in.shape[2] == IN_WIDTH
    w_in_f = w_in[l]
    g1 = norm1_g[l].reshape(1, D_MODEL)
    g2 = norm2_g[l].reshape(1, D_MODEL)
    gf = norm_f_g.reshape(1, D_MODEL)
    bg = b_gate[l].reshape(1, GATE_WIDTH)
    d = ssm_d[l].reshape(1, SSM_WIDTH)
    wglu = w_glu[l].astype(BF)
    bglu = b_glu[l].reshape(1, SSM_WIDTH)
    wba = w_branch_attn[l].astype(BF)
    wbs = w_branch_ssm[l].astype(BF)
    wo = w_out[l].astype(BF)
    sinks = attn_sinks[l]

    bblk, ctblk, lcol, a_tab, lb = _ssm_tables(
        ssm_lam_re[l], ssm_lam_im[l], ssm_log_dt[l], ssm_b_re[l], ssm_b_im[l], ssm_c_re[l], ssm_c_im[l])

    rc, rs1, rs2 = _rope_tables(np.arange(t))
    yssm_p, hre_p, him_p = _s5_prompt(x_prompt, g1, w_in_f, bblk, ctblk, lcol, a_tab, d)
    x1_p, kwin_p, vwin_p, w_in_b = _mix_prompt(x_prompt, yssm_p, rc, rs1, rs2, sinks, g1, w_in_f, bg, wba, wbs, wo,
                                               wglu, bglu)

    rcs, rs1s, rs2s = _rope_tables(PAST_LEN + np.arange(1))
    sinkc = jnp.concatenate([sinks, jnp.zeros((QROWS - N_Q_HEADS,), F32)]).reshape(QROWS, 1)
    xs = x_sample.reshape(nb, D_MODEL)
    key_minor = lambda a: jnp.swapaxes(a.reshape(nb, WINDOW, KV_WIDTH), 1, 2)
    kwin_s, vwin_s, x1_s, hre_s, him_s = _sample_layer(
        xs, key_minor(state_k_win[l]), key_minor(state_v_win[l]),
        state_ssm_re[l].reshape(nb, N_STATES).T, state_ssm_im[l].reshape(nb, N_STATES).T,
        rcs, rs1s, rs2s, sinkc, g1, w_in_b, bg, wba, wbs, wo, lb, bblk, ctblk,
        d, wglu, bglu)
    y_p, y_s = _ffn(x1_p, x1_s, g2, w_ffn_gate[l], w_ffn_up[l], w_ffn_down[l], gf)
    y_p = y_p.reshape(b, t, D_MODEL)
    y_s = y_s.reshape(nb, 1, D_MODEL)

    kv_shape_p = (1, b, WINDOW, N_KV_HEADS, HEAD_DIM)
    st_shape_p = (1, b, N_SSM_GROUPS, SSM_STATE)
    kv_shape_s = (1, nb, WINDOW, N_KV_HEADS, HEAD_DIM)
    st_shape_s = (1, nb, N_SSM_GROUPS, SSM_STATE)
    return (y_p, y_s,
            jnp.swapaxes(kwin_p, 1, 2).reshape(kv_shape_p), jnp.swapaxes(vwin_p, 1, 2).reshape(kv_shape_p),
            hre_p.reshape(st_shape_p), him_p.reshape(st_shape_p),
            jnp.swapaxes(kwin_s, 1, 2).reshape(kv_shape_s), jnp.swapaxes(vwin_s, 1, 2).reshape(kv_shape_s),
            hre_s.T.reshape(st_shape_s), him_s.T.reshape(st_shape_s))
```
